```python
import jax, jax.numpy as jnp
from jax import lax
import numpy as np

D_MODEL = 1024
BATCH = 4
SEQ = 4096
DEPTH = 1
DEC_BATCH = 32
DEC_SEQ = 4
PAST_LEN = 8192
PAGE_SIZE = 128

D_MIX = D_MODEL
NSA_WIDTH = D_MIX // 2
DH = 64
H_NSA = NSA_WIDTH // DH
KVH = 2
G_NSA = H_NSA // KVH
CMP_LEN = 32
CMP_STRIDE = 16
CMP_RATIO = CMP_LEN // CMP_STRIDE
CMP_HID = 2 * DH
SLC_LEN = 64
SLC_TOP = 16
WINDOW = 512
ROT_DIM = DH // 4
ROPE_THETA = 500000.0
FORCE_SCORE = 1.0e4
GLA_WIDTH = D_MIX - NSA_WIDTH
DV = 128
H_GLA = GLA_WIDTH // DV
DK = DV // 2
GLA_LR = 16
GLA_TAU = 16.0
GLA_CHUNK = 64
Q_BLOCK = 128
EPS = 1e-6
NEG = -1.0e30

IN_SIZES = (H_NSA * DH, 2 * KVH * DH, 2 * KVH * DH, 2 * KVH * DH, 3 * H_NSA, NSA_WIDTH,
            H_GLA * DK, H_GLA * DK, H_GLA * DV, GLA_LR, GLA_WIDTH)
D_IN = sum(IN_SIZES)
IN_OFFSETS = [int(v) for v in np.cumsum(IN_SIZES)[:-1]]

kernel_name = "hymba_nsa_gla_hybrid_step"


def rmsnorm(x, g):
    xf = x.astype(jnp.float32)
    y = xf * lax.rsqrt(jnp.mean(xf * xf, axis=-1, keepdims=True) + EPS)
    return (y * g.astype(jnp.float32)).astype(x.dtype)


def rope_partial(x, pos):
    half = ROT_DIM // 2
    inv = ROPE_THETA ** (-(jnp.arange(half, dtype=jnp.float32) / half))
    ang = jnp.asarray(pos).astype(jnp.float32)[:, None] * inv[None, :]
    cos = jnp.cos(ang)[None, :, None, :]
    sin = jnp.sin(ang)[None, :, None, :]
    xf = x.astype(jnp.float32)
    x1, x2 = xf[..., :half], xf[..., half:ROT_DIM]
    out = jnp.concatenate([x1 * cos - x2 * sin, x2 * cos + x1 * sin, xf[..., ROT_DIM:]], axis=-1)
    return out.astype(x.dtype)


def rotate_k(kv, pos):
    return jnp.stack([rope_partial(kv[:, :, 0], pos), kv[:, :, 1]], axis=2)


def masked_softmax(s, mask):
    s = jnp.where(mask, s.astype(jnp.float32), NEG)
    p = jax.nn.softmax(s, axis=-1)
    return jnp.where(mask, p, 0.0)


def attend(q, k, v, mask):
    s = jnp.einsum('bqkgd,bmkd->bkgqm', q, k)
    p = masked_softmax(s, mask)
    return jnp.einsum('bkgqm,bmkd->bqkgd', p, v), p


def to_blocks(a, axis, nb):
    shp = a.shape
    a = a.reshape(shp[:axis] + (nb, shp[axis] // nb) + shp[axis + 1:])
    return jnp.moveaxis(a, axis, 0)


def from_blocks(a, axis):
    a = jnp.moveaxis(a, 0, axis)
    return a.reshape(a.shape[:axis] + (-1,) + a.shape[axis + 2:])


def compress_kv(kv_raw, cmp_pe, cmp_w1, cmp_b1, cmp_w2):
    b, L = kv_raw.shape[:2]
    n_chunk = L // CMP_STRIDE
    n_cmp = n_chunk - CMP_RATIO + 1
    chunks = kv_raw[:, :n_chunk * CMP_STRIDE].reshape(b, n_chunk, CMP_STRIDE, 2, KVH, DH)
    w1 = cmp_w1.reshape(2, CMP_RATIO, CMP_STRIDE, DH, CMP_HID)
    part = jnp.einsum('bnsckd,crsdh->rbnckh', chunks, w1)
    pre = sum(part[r][:, r:r + n_cmp] for r in range(CMP_RATIO))
    pe_term = jnp.einsum('cld,cldh->ch', cmp_pe, cmp_w1.reshape(2, CMP_LEN, DH, CMP_HID))
    h = jax.nn.silu(pre + (pe_term + cmp_b1)[:, None, :])
    out = jnp.einsum('bnckh,chd->bnckd', h, cmp_w2)
    end_pos = np.arange(n_cmp) * CMP_STRIDE + CMP_LEN - 1
    return out, end_pos


def cmp_branch(q, kv_seq, qpos, cmp_pe, cmp_w1, cmp_b1, cmp_w2):
    L = kv_seq.shape[1]
    kvc, end_pos = compress_kv(kv_seq, cmp_pe, cmp_w1, cmp_b1, cmp_w2)
    k_c = rope_partial(kvc[:, :, 0], end_pos)
    v_c = kvc[:, :, 1]
    mask = end_pos[None, :] <= qpos[:, None]
    o, p = attend(q, k_c, v_c, mask)
    n_cmp = end_pos.shape[0]
    n_slc = -(-L // SLC_LEN)
    start = np.arange(n_cmp) * CMP_STRIDE
    j = np.arange(n_slc)
    cover = ((start[:, None] < (j[None, :] + 1) * SLC_LEN)
             & (start[:, None] + CMP_LEN > j[None, :] * SLC_LEN)).astype(np.float32)
    imp = jnp.einsum('bkgqn,nm->bkqm', p, cover)
    qblk = qpos // SLC_LEN
    valid = j[None, :] <= qblk[:, None]
    forced = (j[None, :] == 0) | (j[None, :] == qblk[:, None]) | (j[None, :] == qblk[:, None] - 1)
    score = jnp.where(valid, jnp.where(forced, FORCE_SCORE, imp), -1.0)
    _, idx = lax.top_k(score, min(SLC_TOP, n_slc))
    selv = idx <= jnp.asarray(qblk, jnp.int32)[None, None, :, None]
    return o, idx, selv


def slc_attend(q, kblk, vblk, idx, selv, qpos):
    b, nq = q.shape[0], q.shape[1]
    bi = jnp.arange(b)[:, None, None, None]
    hi = jnp.arange(KVH)[None, :, None, None]
    kg = kblk[bi, hi, idx]
    vg = vblk[bi, hi, idx]
    t = idx.shape[-1]
    kpos = idx[..., None] * SLC_LEN + jnp.arange(SLC_LEN)
    mask = selv[..., None] & (kpos <= qpos[None, None, :, None, None])
    s = jnp.einsum('bqkgd,bkqtsd->bkgqts', q, kg).reshape(b, KVH, G_NSA, nq, t * SLC_LEN)
    p = masked_softmax(s, mask.reshape(b, KVH, 1, nq, t * SLC_LEN))
    p = p.reshape(b, KVH, G_NSA, nq, t, SLC_LEN)
    return jnp.einsum('bkgqts,bkqtsd->bqkgd', p, vg)


def slc_branch(q, kv_seq, idx, selv, qpos):
    b, L = kv_seq.shape[:2]
    n_slc = -(-L // SLC_LEN)
    pad = n_slc * SLC_LEN - L
    kv = jnp.pad(kv_seq, ((0, 0), (0, pad), (0, 0), (0, 0), (0, 0)))
    blk = kv.reshape(b, n_slc, SLC_LEN, 2, KVH, DH).transpose(3, 0, 4, 1, 2, 5)
    kblk, vblk = blk[0], blk[1]
    nq = q.shape[1]
    qb = Q_BLOCK if nq % Q_BLOCK == 0 else nq
    nb = nq // qb
    xs = (to_blocks(q, 1, nb), to_blocks(idx, 2, nb), to_blocks(selv, 2, nb),
          to_blocks(jnp.asarray(qpos, jnp.int32), 0, nb))
    out = lax.map(lambda a: slc_attend(a[0], kblk, vblk, a[1], a[2], a[3]), xs)
    return from_blocks(out, 1)


def window_attend(q, k, v, qpos, kpos):
    kp, qp = kpos[None, :], qpos[:, None]
    mask = (kp <= qp) & (kp > qp - WINDOW) & (kp >= 0)
    o, _ = attend(q, k, v, mask)
    return o


def window_prompt(q, kvw):
    b, L = kvw.shape[:2]
    nb = L // Q_BLOCK
    nwb = WINDOW // Q_BLOCK
    padded = jnp.concatenate([jnp.zeros((b, WINDOW) + kvw.shape[2:], kvw.dtype), kvw], axis=1)
    blocks = padded.reshape(b, nwb + nb, Q_BLOCK, 2, KVH, DH)
    band = jnp.concatenate([blocks[:, i:i + nb] for i in range(nwb + 1)], axis=2)
    band = jnp.moveaxis(band, 1, 0)
    kpos = np.arange(nb)[:, None] * Q_BLOCK - WINDOW + np.arange((nwb + 1) * Q_BLOCK)[None, :]
    qpos = np.arange(L).reshape(nb, Q_BLOCK)
    xs = (to_blocks(q, 1, nb), band, jnp.asarray(qpos, jnp.int32), jnp.asarray(kpos, jnp.int32))
    out = lax.map(lambda a: window_attend(a[0], a[1][:, :, 0], a[1][:, :, 1], a[2], a[3]), xs)
    return from_blocks(out, 1)


def gla_chunked(q, k, v, log_a, s0):
    b, L, h, _ = q.shape
    c = GLA_CHUNK if L >= GLA_CHUNK else L
    n = -(-L // c)
    pad = n * c - L

    def prep(a):
        a = jnp.pad(a, ((0, 0), (0, pad), (0, 0), (0, 0)))
        return jnp.moveaxis(a.reshape(b, n, c, h, a.shape[-1]), 1, 0)

    causal = np.tril(np.ones((c, c), bool))[None, :, :, None, None]

    def step(S, inp):
        qc, kc, vc, ac = inp
        bc = jnp.cumsum(ac, axis=1)
        o_inter = jnp.einsum('bihd,bhde->bihe', qc * jnp.exp(bc), S)
        diff = bc[:, :, None] - bc[:, None, :]
        decay = jnp.where(causal, jnp.exp(jnp.where(causal, diff, 0.0)), 0.0)
        A = jnp.einsum('bihd,bjhd,bijhd->bhij', qc, kc, decay)
        o_intra = jnp.einsum('bhij,bjhe->bihe', A, vc)
        b_last = bc[:, -1]
        S_new = S * jnp.exp(b_last)[..., None] + jnp.einsum(
            'bjhd,bjhe->bhde', kc * jnp.exp(b_last[:, None] - bc), vc)
        return S_new, o_inter + o_intra

    s_fin, o = lax.scan(step, s0, (prep(q), prep(k), prep(v), prep(log_a)))
    o = jnp.moveaxis(o, 0, 1).reshape(b, n * c, h, -1)[:, :L]
    return o, s_fin


def mixer_inputs(x, qpos, norm_g, w_in, gla_wa2, gla_ba):
    b, L, _ = x.shape
    hn = rmsnorm(x, norm_g)
    (q, kvc, kvs, kvw, gate, z_nsa, gq, gk, gv, glr, z_gla) = jnp.split(hn @ w_in, IN_OFFSETS, axis=-1)
    pos = jnp.asarray(qpos, jnp.int32)
    q = (rope_partial(q.reshape(b, L, H_NSA, DH), pos) * (DH ** -0.5)).reshape(b, L, KVH, G_NSA, DH)
    kvc = kvc.reshape(b, L, 2, KVH, DH)
    kvs = rotate_k(kvs.reshape(b, L, 2, KVH, DH), pos)
    kvw = rotate_k(kvw.reshape(b, L, 2, KVH, DH), pos)
    gate = jax.nn.sigmoid(gate.astype(jnp.float32)).reshape(b, L, 3, KVH, G_NSA, 1)
    gq = gq.reshape(b, L, H_GLA, DK) * (DK ** -0.5)
    gk = gk.reshape(b, L, H_GLA, DK)
    gv = gv.reshape(b, L, H_GLA, DV)
    log_a = (jax.nn.log_sigmoid((glr @ gla_wa2 + gla_ba).astype(jnp.float32)) / GLA_TAU).reshape(b, L, H_GLA, DK)
    return q, kvc, kvs, kvw, gate, z_nsa, gq, gk, gv, log_a, z_gla


def mixer_output(x, gate, o_cmp, o_slc, o_win, z_nsa, o_gla, z_gla, gla_norm_g, w_out):
    b, L, _ = x.shape
    o_nsa = gate[:, :, 0] * o_cmp + gate[:, :, 1] * o_slc + gate[:, :, 2] * o_win
    y_nsa = o_nsa.reshape(b, L, NSA_WIDTH) * jax.nn.silu(z_nsa)
    y_gla = rmsnorm(o_gla, gla_norm_g).reshape(b, L, GLA_WIDTH) * jax.nn.silu(z_gla)
    return (x + jnp.concatenate([y_nsa, y_gla], axis=-1) @ w_out).astype(x.dtype)


def prompt_layer(x, norm_g, w_in, cmp_pe, cmp_w1, cmp_b1, cmp_w2, gla_wa2, gla_ba, gla_norm_g, w_out):
    b, L, _ = x.shape
    qpos = np.arange(L)
    q, kvc, kvs, kvw, gate, z_nsa, gq, gk, gv, log_a, z_gla = mixer_inputs(x, qpos, norm_g, w_in, gla_wa2, gla_ba)
    o_cmp, idx, selv = cmp_branch(q, kvc, qpos, cmp_pe, cmp_w1, cmp_b1, cmp_w2)
    o_slc = slc_branch(q, kvs, idx, selv, qpos)
    o_win = window_prompt(q, kvw)
    o_gla, s_fin = gla_chunked(gq, gk, gv, log_a, jnp.zeros((b, H_GLA, DK, DV), jnp.float32))
    y = mixer_output(x, gate, o_cmp, o_slc, o_win, z_nsa, o_gla, z_gla, gla_norm_g, w_out)
    wlen = min(WINDOW, L)
    return y, kvc, kvs, kvw[:, L - wlen:], s_fin.astype(x.dtype)


def sample_layer(x, cache_c, cache_s, cache_w, state, page_table,
                 norm_g, w_in, cmp_pe, cmp_w1, cmp_b1, cmp_w2, gla_wa2, gla_ba, gla_norm_g, w_out):
    b, L, _ = x.shape
    qpos = PAST_LEN + np.arange(L)
    q, kvc, kvs, kvw, gate, z_nsa, gq, gk, gv, log_a, z_gla = mixer_inputs(x, qpos, norm_g, w_in, gla_wa2, gla_ba)
    past_c = cache_c[page_table].reshape(b, -1, 2, KVH, DH)
    past_s = cache_s[page_table].reshape(b, -1, 2, KVH, DH)
    full_c = jnp.concatenate([past_c, kvc.astype(past_c.dtype)], axis=1)
    full_s = jnp.concatenate([past_s, kvs.astype(past_s.dtype)], axis=1)
    o_cmp, idx, selv = cmp_branch(q, full_c, qpos, cmp_pe, cmp_w1, cmp_b1, cmp_w2)
    o_slc = slc_branch(q, full_s, idx, selv, qpos)
    wbuf = cache_w.shape[1]
    full_w = jnp.concatenate([cache_w, kvw.astype(cache_w.dtype)], axis=1)
    kpos = PAST_LEN - wbuf + np.arange(wbuf + L)
    o_win = window_attend(q, full_w[:, :, 0], full_w[:, :, 1], qpos, kpos)
    o_gla, s_new = gla_chunked(gq, gk, gv, log_a, state.astype(jnp.float32))
    y = mixer_output(x, gate, o_cmp, o_slc, o_win, z_nsa, o_gla, z_gla, gla_norm_g, w_out)
    return y, kvc, kvs, full_w[:, L:], s_new.astype(state.dtype)


def setup_inputs(seed: int = 0) -> dict:
    key = jax.random.key(seed)
    ks = jax.random.split(key, 20)
    nrm = jax.random.normal
    f32 = jnp.float32
    n_pages = PAST_LEN // PAGE_SIZE
    n_used = DEC_BATCH * n_pages
    n_pool = (5 * n_used + 3) // 4
    wbuf = min(WINDOW, PAST_LEN)
    perm = jax.random.permutation(ks[0], n_pool)
    page_table = perm[:n_used].reshape(DEC_BATCH, n_pages).astype(jnp.int32)
    return {
        "x_prompt": nrm(ks[1], (BATCH, SEQ, D_MODEL), f32),
        "x_sample": nrm(ks[2], (DEC_BATCH, DEC_SEQ, D_MODEL), f32),
        "cache_cmp_kv": nrm(ks[3], (DEPTH, n_pool, PAGE_SIZE, 2, KVH, DH), f32),
        "cache_slc_kv": nrm(ks[4], (DEPTH, n_pool, PAGE_SIZE, 2, KVH, DH), f32),
        "cache_win_kv": nrm(ks[5], (DEPTH, DEC_BATCH, wbuf, 2, KVH, DH), f32),
        "state_gla": nrm(ks[6], (DEPTH, DEC_BATCH, H_GLA, DK, DV), f32),
        "page_table": page_table,
        "norm_in_gain": 1.0 + 0.02 * nrm(ks[7], (DEPTH, D_MODEL), f32),
        "w_in": nrm(ks[8], (DEPTH, D_MODEL, D_IN), f32) * D_MODEL ** -0.5,
        "cmp_pe": 0.1 * nrm(ks[9], (DEPTH, 2, CMP_LEN, DH), f32),
        "cmp_w1": nrm(ks[10], (DEPTH, 2, CMP_LEN * DH, CMP_HID), f32) * (CMP_LEN * DH) ** -0.5,
        "cmp_b1": 0.02 * nrm(ks[11], (DEPTH, 2, CMP_HID), f32),
        "cmp_w2": nrm(ks[12], (DEPTH, 2, CMP_HID, DH), f32) * CMP_HID ** -0.5,
        "gla_wa2": nrm(ks[13], (DEPTH, GLA_LR, H_GLA * DK), f32) * GLA_LR ** -0.5,
        "gla_ba": 0.02 * nrm(ks[14], (DEPTH, H_GLA * DK), f32),
        "gla_norm_gain": 1.0 + 0.02 * nrm(ks[15], (DEPTH, DV), f32),
        "w_out": nrm(ks[16], (DEPTH, D_MIX, D_MODEL), f32) * D_MIX ** -0.5,
        "norm_out_gain": 1.0 + 0.02 * nrm(ks[17], (D_MODEL,), f32),
    }


def reference(x_prompt, x_sample, cache_cmp_kv, cache_slc_kv, cache_win_kv, state_gla, page_table,
              norm_in_gain, w_in, cmp_pe, cmp_w1, cmp_b1, cmp_w2, gla_wa2, gla_ba, gla_norm_gain,
              w_out, norm_out_gain):
    hp, hs = x_prompt, x_sample
    cmp_p, cmp_s, slc_p, slc_s, win_p, win_s, gla_p, gla_s = [], [], [], [], [], [], [], []
    for layer in range(DEPTH):
        wts = (norm_in_gain[layer], w_in[layer], cmp_pe[layer], cmp_w1[layer], cmp_b1[layer],
               cmp_w2[layer], gla_wa2[layer], gla_ba[layer], gla_norm_gain[layer], w_out[layer])
        hp, c1, s1, w1, g1 = prompt_layer(hp, *wts)
        hs, c2, s2, w2, g2 = sample_layer(hs, cache_cmp_kv[layer], cache_slc_kv[layer], cache_win_kv[layer],
                                          state_gla[layer], page_table, *wts)
        cmp_p.append(c1); cmp_s.append(c2); slc_p.append(s1); slc_s.append(s2)
        win_p.append(w1); win_s.append(w2); gla_p.append(g1); gla_s.append(g2)
    y_prompt = rmsnorm(hp, norm_out_gain)
    y_sample = rmsnorm(hs, norm_out_gain)
    return (y_prompt, y_sample, jnp.stack(cmp_p), jnp.stack(cmp_s), jnp.stack(slc_p), jnp.stack(slc_s),
            jnp.stack(win_p), jnp.stack(win_s), jnp.stack(gla_p), jnp.stack(gla_s))
```

```python
import functools

import numpy as np
import jax
import jax.numpy as jnp
from jax import lax
from jax.experimental import pallas as pl
from jax.experimental.pallas import tpu as pltpu

f32 = jnp.float32
bf16 = jnp.bfloat16

D_MODEL = 1024
DH = 64
H_NSA = 8
KVH = 2
G_NSA = 4
NSA_WIDTH = H_NSA * DH
CMP_LEN = 32
CMP_STRIDE = 16
CMP_HID = 128
SLC_LEN = 64
SLC_TOP = 16
WINDOW = 512
ROT_HALF = 8
ROPE_THETA = 500000.0
FORCE_SCORE = 1.0e4
H_GLA = 4
DK = 64
DV = 128
GLA_WIDTH = H_GLA * DV
GLA_LR = 16
GLA_TAU = 16.0
GLA_CHUNK = 64
GLA_SUB = 16
EPS = 1e-6
NEG = -1.0e30
PAGE_SIZE = 128
KV_ROW = 2 * KVH * DH

IN_SIZES = (H_NSA * DH, KV_ROW, KV_ROW, KV_ROW, 3 * H_NSA, NSA_WIDTH,
            H_GLA * DK, H_GLA * DK, H_GLA * DV, GLA_LR, GLA_WIDTH)
IN_OFFSETS = [0] + [int(v) for v in np.cumsum(IN_SIZES)]

LANES = 128
VMEM_LIMIT = 56 * 1024 * 1024

P_Q, P_KVC, P_KVS, P_KVW, P_ZN, P_GQ, P_GK, P_GV, P_ZG, P_MISC = (
    0, 512, 768, 1024, 1280, 1792, 2048, 2304, 2816, 3328)
D_PACK = P_MISC + LANES
MISC_GATE = 0
MISC_GLR = 32


def _cparams(sem):
    return pltpu.CompilerParams(dimension_semantics=sem, vmem_limit_bytes=VMEM_LIMIT)


def _sigmoid(x):
    return 1.0 / (1.0 + jnp.exp(-x))


def _silu(x):
    return x * _sigmoid(x)


def _log_sigmoid(x):
    return jnp.minimum(x, 0.0) - jnp.log1p(jnp.exp(-jnp.abs(x)))


def _dot(a, b):
    return jnp.dot(a, b, preferred_element_type=f32)


def _dot_nt(a, b):
    return lax.dot_general(a, b, (((1,), (1,)), ((), ())), preferred_element_type=f32)


def _dot_tn(a, b):
    return lax.dot_general(a, b, (((0,), (0,)), ((), ())), preferred_element_type=f32)


def _rope_tables(pos):
    n = pos.shape[0]
    inv = ROPE_THETA ** (-(jnp.arange(ROT_HALF, dtype=f32) / ROT_HALF))
    ang = pos.astype(f32)[:, None] * inv[None, :]
    cos, sin = jnp.cos(ang), jnp.sin(ang)
    z8 = jnp.zeros((n, ROT_HALF), f32)
    rest = DH - 2 * ROT_HALF
    c64 = jnp.concatenate([cos, cos, jnp.ones((n, rest), f32)], axis=-1)
    sa64 = jnp.concatenate([-sin, z8, jnp.zeros((n, rest), f32)], axis=-1)
    sb64 = jnp.concatenate([z8, sin, jnp.zeros((n, rest), f32)], axis=-1)
    tile = lambda t: jnp.concatenate([t, t], axis=-1)
    return tile(c64), tile(sa64), tile(sb64)


def _rope128(v, cos, sa, sb):
    return v * cos + pltpu.roll(v, LANES - ROT_HALF, 1) * sa + pltpu.roll(v, ROT_HALF, 1) * sb


def _inproj_kernel(x_ref, g_ref, w_ref, cos_ref, sa_ref, sb_ref, wa2_ref, ba_ref,
                   q_ref, kvc_ref, kvs_ref, kvw_ref, ks_ref, vs_ref, kw_ref, vw_ref,
                   gate_ref, zn_ref, gq_ref, gk_ref, gv_ref, la_ref, zg_ref):
    x = x_ref[...]
    ms = jnp.mean(x * x, axis=-1, keepdims=True)
    hn = (x * lax.rsqrt(ms + EPS) * g_ref[...]).astype(bf16)
    cos, sa, sb = cos_ref[...], sa_ref[...], sb_ref[...]

    def proj(off, width):
        return _dot(hn, w_ref[:, off:off + width])

    qp = proj(P_Q, NSA_WIDTH)
    for c in range(NSA_WIDTH // LANES):
        r = _rope128(qp[:, c * LANES:(c + 1) * LANES], cos, sa, sb) * (DH ** -0.5)
        q_ref[0, 2 * c] = r[:, :DH].astype(bf16)
        q_ref[0, 2 * c + 1] = r[:, DH:].astype(bf16)

    kvc_ref[...] = proj(P_KVC, KV_ROW)

    for off, kv_ref, k_ref, v_ref in ((P_KVS, kvs_ref, ks_ref, vs_ref), (P_KVW, kvw_ref, kw_ref, vw_ref)):
        p = proj(off, KV_ROW)
        k = _rope128(p[:, :LANES], cos, sa, sb)
        v = p[:, LANES:]
        kv_ref[:, :LANES] = k
        kv_ref[:, LANES:] = v
        for h in range(KVH):
            k_ref[0, h] = k[:, h * DH:(h + 1) * DH].astype(bf16)
            v_ref[0, h] = v[:, h * DH:(h + 1) * DH].astype(bf16)

    zn_ref[...] = _silu(proj(P_ZN, NSA_WIDTH))
    gq_ref[...] = proj(P_GQ, H_GLA * DK) * (DK ** -0.5)
    gk_ref[...] = proj(P_GK, H_GLA * DK)
    gv_ref[...] = proj(P_GV, H_GLA * DV)
    zg_ref[...] = _silu(proj(P_ZG, GLA_WIDTH))

    misc = proj(P_MISC, LANES)
    gate_ref[...] = _sigmoid(misc)
    xa = _dot(misc.astype(bf16), wa2_ref[...]) + ba_ref[...]
    la_ref[...] = _log_sigmoid(xa) / GLA_TAU


def _pack_w_in(w_in):
    o = IN_OFFSETS
    seg = lambda i: w_in[:, o[i]:o[i + 1]]
    misc = jnp.zeros((D_MODEL, LANES), w_in.dtype)
    misc = misc.at[:, MISC_GATE:MISC_GATE + 3 * H_NSA].set(seg(4))
    misc = misc.at[:, MISC_GLR:MISC_GLR + GLA_LR].set(seg(9))
    cols = [seg(0), seg(1), seg(2), seg(3), seg(5), seg(6), seg(7), seg(8), seg(10), misc]
    return jnp.concatenate(cols, axis=1).astype(bf16)


def _pad_wa2(wa2):
    pad = jnp.zeros((LANES, H_GLA * DK), wa2.dtype).at[MISC_GLR:MISC_GLR + GLA_LR].set(wa2)
    return pad.astype(bf16)


def _inproj(x, pos, norm_g, w_pack, wa2_pad, ba, tm):
    bk, lk, _ = x.shape
    n = bk * lk
    nt = lk // tm
    cos, sa, sb = _rope_tables(pos)
    tok = lambda w: pl.BlockSpec((tm, w), lambda i: (i, 0))
    tab = pl.BlockSpec((tm, LANES), lambda i: (i % nt, 0))
    full = lambda a: pl.BlockSpec(a.shape, lambda i: (0,) * a.ndim)
    hm = lambda h: pl.BlockSpec((1, h, tm, DH), lambda i: (i // nt, 0, i % nt, 0))
    g2 = norm_g.reshape(1, D_MODEL)
    ba2 = ba.reshape(1, H_GLA * DK)
    out_shape = (
        jax.ShapeDtypeStruct((bk, H_NSA, lk, DH), bf16),
        jax.ShapeDtypeStruct((n, KV_ROW), f32),
        jax.ShapeDtypeStruct((n, KV_ROW), f32),
        jax.ShapeDtypeStruct((n, KV_ROW), f32),
        jax.ShapeDtypeStruct((bk, KVH, lk, DH), bf16),
        jax.ShapeDtypeStruct((bk, KVH, lk, DH), bf16),
        jax.ShapeDtypeStruct((bk, KVH, lk, DH), bf16),
        jax.ShapeDtypeStruct((bk, KVH, lk, DH), bf16),
        jax.ShapeDtypeStruct((n, LANES), f32),
        jax.ShapeDtypeStruct((n, NSA_WIDTH), f32),
        jax.ShapeDtypeStruct((n, H_GLA * DK), f32),
        jax.ShapeDtypeStruct((n, H_GLA * DK), f32),
        jax.ShapeDtypeStruct((n, H_GLA * DV), f32),
        jax.ShapeDtypeStruct((n, H_GLA * DK), f32),
        jax.ShapeDtypeStruct((n, GLA_WIDTH), f32),
    )
    out_specs = (hm(H_NSA), tok(KV_ROW), tok(KV_ROW), tok(KV_ROW), hm(KVH), hm(KVH), hm(KVH), hm(KVH),
                 tok(LANES), tok(NSA_WIDTH), tok(H_GLA * DK), tok(H_GLA * DK), tok(H_GLA * DV),
                 tok(H_GLA * DK), tok(GLA_WIDTH))
    return pl.pallas_call(
        _inproj_kernel,
        grid=(n // tm,),
        in_specs=[tok(D_MODEL), full(g2), full(w_pack), tab, tab, tab, full(wa2_pad), full(ba2)],
        out_specs=out_specs,
        out_shape=out_shape,
        compiler_params=_cparams(("parallel",)),
        name="inproj",
    )(x.reshape(n, D_MODEL), g2, w_pack, cos, sa, sb, wa2_pad, ba2)


CHUNK_W = CMP_STRIDE * KV_ROW
PART_W = 2 * 2 * KVH * CMP_HID
HID_W = 2 * KVH * CMP_HID


def _cmp_w1_big(cmp_w1):
    w1r = cmp_w1.reshape(2, CMP_LEN // CMP_STRIDE, CMP_STRIDE, DH, CMP_HID)
    eye = jnp.eye(2, dtype=cmp_w1.dtype)
    big = jnp.einsum('crsdh,cC,kK->sCKdrckh', w1r, eye, eye)
    return big.reshape(CHUNK_W, PART_W).astype(bf16)


def _cmp_w2_big(cmp_w2):
    eye = jnp.eye(2, dtype=cmp_w2.dtype)
    big = jnp.einsum('chd,cC,kK->ckhCKd', cmp_w2, eye, eye)
    return big.reshape(HID_W, KV_ROW).astype(bf16)


def _cmp_part_kernel(x_ref, w_ref, o_ref):
    o_ref[...] = _dot(x_ref[...].astype(bf16), w_ref[...])


def _cmp_part(chunks, w1_big, tr):
    r = chunks.shape[0]
    return pl.pallas_call(
        _cmp_part_kernel,
        grid=(r // tr,),
        in_specs=[pl.BlockSpec((tr, CHUNK_W), lambda i: (i, 0)),
                  pl.BlockSpec((CHUNK_W, PART_W), lambda i: (0, 0))],
        out_specs=pl.BlockSpec((tr, PART_W), lambda i: (i, 0)),
        out_shape=jax.ShapeDtypeStruct((r, PART_W), f32),
        compiler_params=_cparams(("parallel",)),
        name="cmp_part",
    )(chunks, w1_big)


def _cmp_bias_kernel(pe_ref, w1_ref, b1_ref, o_ref):
    for c in range(2):
        o_ref[c] = _dot(pe_ref[c], w1_ref[c]) + b1_ref[c]


def _cmp_bias(cmp_pe, cmp_w1, cmp_b1):
    pe = jnp.broadcast_to(cmp_pe.reshape(2, 1, CMP_LEN * DH), (2, 8, CMP_LEN * DH))
    b1 = jnp.broadcast_to(cmp_b1.reshape(2, 1, CMP_HID), (2, 8, CMP_HID))
    out = pl.pallas_call(
        _cmp_bias_kernel,
        out_shape=jax.ShapeDtypeStruct((2, 8, CMP_HID), f32),
        name="cmp_bias",
    )(pe, cmp_w1, b1)
    row = out[:, 0, :]
    return jnp.broadcast_to(row[:, None, :], (2, KVH, CMP_HID)).reshape(1, HID_W)


def _compress_finish(part, bias, w2_big, cos, sa, sb):
    n = part.shape[0]
    pre = part[:, :HID_W] + pltpu.roll(part[:, HID_W:], n - 1, 0)
    h = _silu(pre + bias)
    out = _dot(h.astype(bf16), w2_big)
    return _rope128(out[:, :LANES], cos, sa, sb), out[:, LANES:]


def _masked_softmax(s, mask, axis):
    s = jnp.where(mask, s, NEG)
    e = jnp.exp(s - jnp.max(s, axis=axis, keepdims=True))
    return jnp.where(mask, e / jnp.sum(e, axis=axis, keepdims=True), 0.0)


def _split_bf16(x):
    hi = x.astype(bf16)
    return hi, (x - hi.astype(f32)).astype(bf16)


def _topk_mask(score, k, axis):
    n = score.shape[axis]
    idx = lax.broadcasted_iota(jnp.int32, score.shape, axis)
    sel = jnp.zeros(score.shape, f32)
    for _ in range(k):
        m = jnp.max(score, axis=axis, keepdims=True)
        first = jnp.min(jnp.where(score == m, idx, n), axis=axis, keepdims=True)
        pick = idx == first
        sel = jnp.where(pick, 1.0, sel)
        score = jnp.where(pick, NEG, score)
    return sel


def _cover_t(n_slc, n_chunk):
    start = np.arange(n_chunk)[None, :] * CMP_STRIDE
    j = np.arange(n_slc)[:, None]
    cov = (start < (j + 1) * SLC_LEN) & (start + CMP_LEN > j * SLC_LEN)
    return jnp.asarray(cov.astype(np.float32), dtype=bf16)


def _cmp_attn_kernel(part_ref, bias_ref, w2_ref, cos_ref, sa_ref, sb_ref, q_ref, gate_ref, cov_ref,
                     o_ref, sel_ref, kc_ref, vc_ref, *, tq):
    t = pl.program_id(1)

    @pl.when(t == 0)
    def _():
        k, v = _compress_finish(part_ref[0], bias_ref[...], w2_ref[...], cos_ref[...], sa_ref[...], sb_ref[...])
        for h in range(KVH):
            kc_ref[h] = k[:, h * DH:(h + 1) * DH].astype(bf16)
            vc_ref[h] = v[:, h * DH:(h + 1) * DH].astype(bf16)

    nc = kc_ref.shape[1]
    nb = cov_ref.shape[0]
    q0 = t * tq
    qpos_r = q0 + lax.broadcasted_iota(jnp.int32, (tq, 1), 0)
    qpos_c = q0 + lax.broadcasted_iota(jnp.int32, (1, tq), 1)
    end_c = lax.broadcasted_iota(jnp.int32, (1, nc), 1) * CMP_STRIDE + (CMP_LEN - 1)
    end_r = lax.broadcasted_iota(jnp.int32, (nc, 1), 0) * CMP_STRIDE + (CMP_LEN - 1)
    mask = end_c <= qpos_r
    mask_t = end_r <= qpos_c
    gate = gate_ref[...]
    cov = cov_ref[...]

    jblk = lax.broadcasted_iota(jnp.int32, (nb, tq), 0)
    qblk = qpos_c // SLC_LEN
    valid = jblk <= qblk
    forced = (jblk == 0) | (jblk == qblk) | (jblk == qblk - 1)

    outs = []
    for kh in range(KVH):
        kc, vc = kc_ref[kh], vc_ref[kh]
        psum = jnp.zeros((nc, tq), f32)
        for g in range(G_NSA):
            h = kh * G_NSA + g
            qg = q_ref[0, h]
            p = _masked_softmax(_dot_nt(qg, kc), mask, axis=-1)
            outs.append(_dot(p.astype(bf16), vc) * gate[:, h:h + 1])
            psum = psum + _masked_softmax(_dot_nt(kc, qg), mask_t, axis=0)
        hi, lo = _split_bf16(psum)
        imp = _dot(cov, hi) + _dot(cov, lo)
        score = jnp.where(valid, jnp.where(forced, FORCE_SCORE, imp), -1.0)
        sel_t = jnp.where(valid, _topk_mask(score, SLC_TOP, axis=0), 0.0)
        if nb < LANES:
            sel_t = jnp.concatenate([sel_t, jnp.zeros((LANES - nb, tq), f32)], axis=0)
        sel_ref[0, kh] = sel_t.T.astype(bf16)
    o_ref[0] = jnp.concatenate(outs, axis=-1)


def _cmp_attn(part, bias, w2_big, q_hm, gate, tq):
    b, nc, _ = part.shape
    lq = q_hm.shape[2]
    nt = lq // tq
    n_slc = lq // SLC_LEN
    end_pos = jnp.arange(nc) * CMP_STRIDE + (CMP_LEN - 1)
    cos, sa, sb = _rope_tables(end_pos)
    cov = _cover_t(n_slc, nc)
    full = lambda a: pl.BlockSpec(a.shape, lambda i, t: (0,) * a.ndim)
    return pl.pallas_call(
        functools.partial(_cmp_attn_kernel, tq=tq),
        grid=(b, nt),
        in_specs=[pl.BlockSpec((1, nc, PART_W), lambda i, t: (i, 0, 0)),
                  full(bias), full(w2_big), full(cos), full(sa), full(sb),
                  pl.BlockSpec((1, H_NSA, tq, DH), lambda i, t: (i, 0, t, 0)),
                  pl.BlockSpec((tq, LANES), lambda i, t: (i * nt + t, 0)),
                  full(cov)],
        out_specs=(pl.BlockSpec((1, tq, NSA_WIDTH), lambda i, t: (i, t, 0)),
                   pl.BlockSpec((1, KVH, tq, LANES), lambda i, t: (i, 0, t, 0))),
        out_shape=(jax.ShapeDtypeStruct((b, lq, NSA_WIDTH), f32),
                   jax.ShapeDtypeStruct((b, KVH, lq, LANES), bf16)),
        scratch_shapes=[pltpu.VMEM((KVH, nc, DH), bf16), pltpu.VMEM((KVH, nc, DH), bf16)],
        compiler_params=_cparams(("parallel", "arbitrary")),
        name="cmp_attn",
    )(part, bias, w2_big, cos, sa, sb, q_hm, gate, cov)


def _block_expand(n_slc_pad, n_keys):
    e = (np.arange(n_keys)[None, :] // SLC_LEN) == np.arange(n_slc_pad)[:, None]
    return jnp.asarray(e.astype(np.float32), dtype=bf16)


def _flash_step(q_ref, k, v, allowed, m_ref, l_ref, acc_ref):
    for g in range(G_NSA):
        s = jnp.where(allowed, _dot_nt(q_ref[0, g], k), NEG)
        m_old = m_ref[g]
        m_new = jnp.maximum(m_old, jnp.max(s, axis=-1, keepdims=True))
        alpha = jnp.exp(m_old - m_new)
        p = jnp.where(allowed, jnp.exp(s - m_new), 0.0)
        l_ref[g] = alpha * l_ref[g] + jnp.sum(p, axis=-1, keepdims=True)
        acc_ref[g] = alpha * acc_ref[g] + _dot(p.astype(bf16), v)
        m_ref[g] = m_new


def _flash_reset(m_ref, l_ref, acc_ref):
    m_ref[...] = jnp.full(m_ref.shape, NEG, f32)
    l_ref[...] = jnp.zeros(l_ref.shape, f32)
    acc_ref[...] = jnp.zeros(acc_ref.shape, f32)


def _slc_win_kernel(q_ref, ks_ref, vs_ref, kw_ref, vw_ref, sel_ref, e_ref, gate_ref, ocmp_ref,
                    o_ref, m_ref, l_ref, acc_ref, *, tq, tk):
    kh = pl.program_id(1)
    t = pl.program_id(2)
    q0 = t * tq
    qpos = q0 + lax.broadcasted_iota(jnp.int32, (tq, 1), 0)
    sel = sel_ref[0, 0]

    _flash_reset(m_ref, l_ref, acc_ref)

    def slc_body(kt, carry):
        k0 = pl.multiple_of(kt * tk, tk)
        kpos = k0 + lax.broadcasted_iota(jnp.int32, (1, tk), 1)
        picked = _dot(sel, e_ref[:, pl.ds(k0, tk)])
        allowed = (picked > 0.5) & (kpos <= qpos)
        _flash_step(q_ref, ks_ref[0, 0, pl.ds(k0, tk), :], vs_ref[0, 0, pl.ds(k0, tk), :], allowed,
                    m_ref, l_ref, acc_ref)
        return carry

    lax.fori_loop(0, (q0 + tq + tk - 1) // tk, slc_body, 0)
    o_slc = [acc_ref[g] / l_ref[g] for g in range(G_NSA)]

    _flash_reset(m_ref, l_ref, acc_ref)

    def win_body(kt, carry):
        k0 = pl.multiple_of(kt * tq, tq)
        kpos = k0 + lax.broadcasted_iota(jnp.int32, (1, tq), 1)
        allowed = (kpos <= qpos) & (kpos > qpos - WINDOW)
        _flash_step(q_ref, kw_ref[0, 0, pl.ds(k0, tq), :], vw_ref[0, 0, pl.ds(k0, tq), :], allowed,
                    m_ref, l_ref, acc_ref)
        return carry

    lax.fori_loop(jnp.maximum(t - WINDOW // tq, 0), t + 1, win_body, 0)

    gate = gate_ref[...]
    outs = []
    for g in range(G_NSA):
        col = lambda c: jnp.where(kh == 0, gate[:, c * H_NSA + g:c * H_NSA + g + 1],
                                  gate[:, c * H_NSA + G_NSA + g:c * H_NSA + G_NSA + g + 1])
        outs.append(col(1) * o_slc[g] + col(2) * (acc_ref[g] / l_ref[g]))
    o_ref[0] = ocmp_ref[0] + jnp.concatenate(outs, axis=-1)


def _slc_win(q_hm, ks, vs, kw, vw, sel, gate, o_cmp, tq, tk):
    b, _, lq, _ = q_hm.shape
    nt = lq // tq
    e = _block_expand(LANES, lq)
    gw = G_NSA * DH
    kvspec = pl.BlockSpec((1, 1, lq, DH), lambda i, k, t: (i, k, 0, 0))
    return pl.pallas_call(
        functools.partial(_slc_win_kernel, tq=tq, tk=tk),
        grid=(b, KVH, nt),
        in_specs=[pl.BlockSpec((1, G_NSA, tq, DH), lambda i, k, t: (i, k, t, 0)),
                  kvspec, kvspec, kvspec, kvspec,
                  pl.BlockSpec((1, 1, tq, LANES), lambda i, k, t: (i, k, t, 0)),
                  pl.BlockSpec(e.shape, lambda i, k, t: (0, 0)),
                  pl.BlockSpec((tq, LANES), lambda i, k, t: (i * nt + t, 0)),
                  pl.BlockSpec((1, tq, gw), lambda i, k, t: (i, t, k))],
        out_specs=pl.BlockSpec((1, tq, gw), lambda i, k, t: (i, t, k)),
        out_shape=jax.ShapeDtypeStruct((b, lq, NSA_WIDTH), f32),
        scratch_shapes=[pltpu.VMEM((G_NSA, tq, 1), f32), pltpu.VMEM((G_NSA, tq, 1), f32),
                        pltpu.VMEM((G_NSA, tq, DH), f32)],
        compiler_params=_cparams(("parallel", "parallel", "parallel")),
        name="slc_win",
    )(q_hm, ks, vs, kw, vw, sel, e, gate, o_cmp)


def _rms_gain(o, gain):
    return o * lax.rsqrt(jnp.mean(o * o, axis=-1, keepdims=True) + EPS) * gain


def _gla_kernel(q_ref, k_ref, v_ref, la_ref, z_ref, gain_ref, y_ref, st_ref, s_scr, *, tl):
    t = pl.program_id(2)

    @pl.when(t == 0)
    def _():
        s_scr[...] = jnp.zeros(s_scr.shape, f32)

    c = GLA_CHUNK
    row = lax.broadcasted_iota(jnp.int32, (c, c), 0)
    col = lax.broadcasted_iota(jnp.int32, (c, c), 1)
    causal = col <= row
    tril = jnp.where(causal, 1.0, 0.0).astype(bf16)
    lane = lax.broadcasted_iota(jnp.int32, (1, LANES), 1)
    head_mask = [jnp.where(lane < DK, 1.0, 0.0), jnp.where(lane >= DK, 1.0, 0.0)]
    gain = gain_ref[...]

    st = s_scr[...]
    for ci in range(tl // c):
        rows = slice(ci * c, (ci + 1) * c)
        q, k, la = q_ref[rows, :], k_ref[rows, :], la_ref[rows, :]
        hi, rest = la.astype(bf16), la - la.astype(bf16).astype(f32)
        mid, lo = rest.astype(bf16), (rest - rest.astype(bf16).astype(f32)).astype(bf16)
        bc = _dot(tril, hi) + _dot(tril, mid) + _dot(tril, lo)
        ref_row = bc[c // 2 - 1:c // 2, :]
        b_last = bc[c - 1:c, :]
        q_a = q * jnp.exp(bc - ref_row)
        k_a = (k * jnp.exp(ref_row - bc)).astype(bf16)
        q_s = q * jnp.exp(bc)
        k_s = k * jnp.exp(b_last - bc)
        st_b = st.astype(bf16)
        upd = st * jnp.exp(b_last)
        for h in range(2):
            mh = head_mask[h]
            vh = v_ref[rows, h * DV:(h + 1) * DV].astype(bf16)
            a = jnp.where(causal, _dot_nt((q_a * mh).astype(bf16), k_a), 0.0)
            o = _dot(a.astype(bf16), vh) + _dot_nt((q_s * mh).astype(bf16), st_b)
            upd = upd + _dot_tn(vh, (k_s * mh).astype(bf16))
            y_ref[rows, h * DV:(h + 1) * DV] = _rms_gain(o, gain) * z_ref[rows, h * DV:(h + 1) * DV]
        st = upd
    s_scr[...] = st
    st_ref[0, 0] = st


def _gla(gq, gk, gv, la, zg, gain, b, tl):
    n = gq.shape[0]
    nt = n // b // tl
    pairs = H_GLA // 2
    qk = pl.BlockSpec((tl, LANES), lambda i, p, t: (i * nt + t, p))
    vz = pl.BlockSpec((tl, 2 * DV), lambda i, p, t: (i * nt + t, p))
    gain2 = gain.reshape(1, DV)
    y, st = pl.pallas_call(
        functools.partial(_gla_kernel, tl=tl),
        grid=(b, pairs, nt),
        in_specs=[qk, qk, vz, qk, vz, pl.BlockSpec((1, DV), lambda i, p, t: (0, 0))],
        out_specs=(vz, pl.BlockSpec((1, 1, DV, LANES), lambda i, p, t: (i, p, 0, 0))),
        out_shape=(jax.ShapeDtypeStruct((n, GLA_WIDTH), f32),
                   jax.ShapeDtypeStruct((b, pairs, DV, LANES), f32)),
        scratch_shapes=[pltpu.VMEM((DV, LANES), f32)],
        compiler_params=_cparams(("parallel", "parallel", "arbitrary")),
        name="gla",
    )(gq, gk, gv, la, zg, gain2)
    state = st.reshape(b, pairs, DV, 2, DK).transpose(0, 1, 3, 4, 2).reshape(b, H_GLA, DK, DV)
    return y, state


def _outproj_kernel(x_ref, on_ref, zn_ref, yg_ref, w_ref, g_ref, y_ref):
    y_nsa = (on_ref[...] * zn_ref[...]).astype(bf16)
    mix = _dot(y_nsa, w_ref[:NSA_WIDTH, :]) + _dot(yg_ref[...].astype(bf16), w_ref[NSA_WIDTH:, :])
    y_ref[...] = _rms_gain(x_ref[...] + mix, g_ref[...])


def _outproj(x, o_nsa, zn, y_gla, w_out, gain, tm):
    n = x.shape[0]
    tok = lambda w: pl.BlockSpec((tm, w), lambda i: (i, 0))
    g2 = gain.reshape(1, D_MODEL)
    return pl.pallas_call(
        _outproj_kernel,
        grid=(n // tm,),
        in_specs=[tok(D_MODEL), tok(NSA_WIDTH), tok(NSA_WIDTH), tok(GLA_WIDTH),
                  pl.BlockSpec(w_out.shape, lambda i: (0, 0)), pl.BlockSpec(g2.shape, lambda i: (0, 0))],
        out_specs=tok(D_MODEL),
        out_shape=jax.ShapeDtypeStruct((n, D_MODEL), f32),
        compiler_params=_cparams(("parallel",)),
        name="outproj",
    )(x, o_nsa, zn, y_gla, w_out, g2)


def _prompt_path(x, wts):
    (norm_g, w_pack, wa2_pad, ba, w1_big, cmp_bias, w2_big, gla_gain, w_out, out_gain) = wts
    b, lq, _ = x.shape
    (q_hm, kvc, kvs, kvw, ks, vs, kw, vw, gate, zn, gq, gk, gv, la, zg) = _inproj(
        x, jnp.arange(lq), norm_g, w_pack, wa2_pad, ba, tm=512)
    n_chunk = lq // CMP_STRIDE
    part = _cmp_part(kvc.reshape(b * n_chunk, CHUNK_W), w1_big, tr=n_chunk)
    o_cmp, sel = _cmp_attn(part.reshape(b, n_chunk, PART_W), cmp_bias, w2_big, q_hm, gate, tq=256)
    o_nsa = _slc_win(q_hm, ks, vs, kw, vw, sel, gate, o_cmp, tq=128, tk=256)
    y_gla, state = _gla(gq, gk, gv, la, zg, gla_gain, b, tl=512)
    y = _outproj(x.reshape(b * lq, D_MODEL), o_nsa.reshape(b * lq, NSA_WIDTH), zn, y_gla, w_out, out_gain, tm=512)
    kv5 = lambda a: a.reshape(b, lq, 2, KVH, DH)
    wlen = min(WINDOW, lq)
    return (y.reshape(b, lq, D_MODEL), kv5(kvc), kv5(kvs), kv5(kvw)[:, lq - wlen:], state)


def _start_page_gather(pt_ref, cache_ref, buf, sem, seq, slot, n_pages):
    for p in range(n_pages):
        pltpu.make_async_copy(cache_ref.at[pt_ref[seq, p]], buf.at[slot, p], sem.at[slot]).start()


def _wait_page_gather(cache_ref, buf, sem, slot, n_pages):
    for p in range(n_pages):
        pltpu.make_async_copy(cache_ref.at[0], buf.at[slot, p], sem.at[slot]).wait()


def _gathered_pages(pt_ref, cache_ref, buf, sem, n_pages):
    i = pl.program_id(0)
    slot = i % 2

    @pl.when(i == 0)
    def _():
        _start_page_gather(pt_ref, cache_ref, buf, sem, 0, 0, n_pages)

    @pl.when(i + 1 < pl.num_programs(0))
    def _():
        _start_page_gather(pt_ref, cache_ref, buf, sem, i + 1, 1 - slot, n_pages)

    _wait_page_gather(cache_ref, buf, sem, slot, n_pages)
    return slot


def _cmp_part_paged_kernel(pt_ref, cache_ref, w_ref, o_ref, buf, sem, *, n_pages):
    slot = _gathered_pages(pt_ref, cache_ref, buf, sem, n_pages)
    x = buf[slot].reshape(n_pages * (PAGE_SIZE // CMP_STRIDE), CHUNK_W)
    o_ref[0] = _dot(x.astype(bf16), w_ref[...])


def _cmp_part_paged(cache, page_table, w1_big):
    b, n_pages = page_table.shape
    cpp = PAGE_SIZE // CMP_STRIDE
    n_chunk = n_pages * cpp
    chunks = cache.reshape(cache.shape[0], cpp, CHUNK_W)
    grid_spec = pltpu.PrefetchScalarGridSpec(
        num_scalar_prefetch=1,
        grid=(b,),
        in_specs=[pl.BlockSpec(memory_space=pl.ANY),
                  pl.BlockSpec((CHUNK_W, PART_W), lambda i, pt: (0, 0))],
        out_specs=pl.BlockSpec((1, n_chunk, PART_W), lambda i, pt: (i, 0, 0)),
        scratch_shapes=[pltpu.VMEM((2, n_pages, cpp, CHUNK_W), f32), pltpu.SemaphoreType.DMA((2,))],
    )
    return pl.pallas_call(
        functools.partial(_cmp_part_paged_kernel, n_pages=n_pages),
        grid_spec=grid_spec,
        out_shape=jax.ShapeDtypeStruct((b, n_chunk, PART_W), f32),
        compiler_params=_cparams(("arbitrary",)),
        name="cmp_part_paged",
    )(page_table, chunks, w1_big)


def _group_sum_matrix(n_tok):
    r = np.arange(KVH * n_tok)[:, None]
    c = np.arange(H_NSA * n_tok)[None, :]
    m = ((c // (G_NSA * n_tok)) == (r // n_tok)) & ((c % n_tok) == (r % n_tok))
    return m.astype(np.float32)


def _cmp_attn_sample_kernel(part_ref, bias_ref, w2_ref, cos_ref, sa_ref, sb_ref, q_ref, gate_ref, cov_ref,
                            gs_ref, gst_ref, o_ref, sel_ref, *, n_tok, past_len):
    k, v = _compress_finish(part_ref[0], bias_ref[...], w2_ref[...], cos_ref[...], sa_ref[...], sb_ref[...])
    nc = k.shape[0]
    rows = H_NSA * n_tok
    qpos = past_len + lax.broadcasted_iota(jnp.int32, (rows, 1), 0) % n_tok
    end_c = lax.broadcasted_iota(jnp.int32, (1, nc), 1) * CMP_STRIDE + (CMP_LEN - 1)
    mask = end_c <= qpos
    p = _masked_softmax(_dot_nt(q_ref[0], k.astype(bf16)), mask, axis=-1)
    o_ref[0] = _dot(p.astype(bf16), v.astype(bf16)) * gate_ref[0][:, 0:1]
    hi, lo = _split_bf16(p)
    gs = gs_ref[...]
    ph, pl_ = _split_bf16(_dot(gs, hi) + _dot(gs, lo))
    imp = _dot(ph, cov_ref[...]) + _dot(pl_, cov_ref[...])
    n_blk = imp.shape[1]
    jblk = lax.broadcasted_iota(jnp.int32, imp.shape, 1)
    forced = (jblk == 0) | (jblk == n_blk - 1)
    sel = _topk_mask(jnp.where(forced, FORCE_SCORE, imp), SLC_TOP - 1, axis=1)
    sel_ref[0] = _dot(gst_ref[...], sel.astype(bf16)).astype(bf16)


def _cmp_attn_sample(part, bias, w2_big, q_blk, gate_rows, n_tok, past_len):
    b, nc, _ = part.shape
    rows = H_NSA * n_tok
    n_blk = past_len // SLC_LEN
    end_pos = jnp.arange(nc) * CMP_STRIDE + (CMP_LEN - 1)
    cos, sa, sb = _rope_tables(end_pos)
    cov = _cover_t(n_blk, nc).T
    gs = jnp.asarray(_group_sum_matrix(n_tok), dtype=bf16)
    gst = gs.T
    full = lambda a: pl.BlockSpec(a.shape, lambda i: (0,) * a.ndim)
    seq = lambda a: pl.BlockSpec((1,) + a.shape[1:], lambda i: (i,) + (0,) * (a.ndim - 1))
    return pl.pallas_call(
        functools.partial(_cmp_attn_sample_kernel, n_tok=n_tok, past_len=past_len),
        grid=(b,),
        in_specs=[seq(part), full(bias), full(w2_big), full(cos), full(sa), full(sb),
                  seq(q_blk), seq(gate_rows), full(cov), full(gs), full(gst)],
        out_specs=(pl.BlockSpec((1, rows, LANES), lambda i: (i, 0, 0)),
                   pl.BlockSpec((1, rows, n_blk), lambda i: (i, 0, 0))),
        out_shape=(jax.ShapeDtypeStruct((b, rows, LANES), f32),
                   jax.ShapeDtypeStruct((b, rows, n_blk), bf16)),
        compiler_params=_cparams(("parallel",)),
        name="cmp_attn_sample",
    )(part, bias, w2_big, cos, sa, sb, q_blk, gate_rows, cov, gs, gst)


def _slc_win_sample_kernel(pt_ref, cache_ref, q_ref, sel_ref, e_ref, snew_ref, cw_ref, wnew_ref, gate_ref,
                           ocmp_ref, o_ref, buf, sem, *, n_pages, n_tok, win_off):
    slot = _gathered_pages(pt_ref, cache_ref, buf, sem, n_pages)
    rows = H_NSA * n_tok
    q = q_ref[0]
    tok = lax.broadcasted_iota(jnp.int32, (rows, 1), 0) % n_tok
    new_i = lax.broadcasted_iota(jnp.int32, (1, snew_ref.shape[1]), 1)
    new_ok = (new_i <= tok) & (new_i < n_tok)

    def attend(keys, vals, allowed, new_ref):
        s = jnp.where(allowed, _dot_nt(q, keys), NEG)
        k_new = new_ref[0][:, :LANES].astype(bf16)
        v_new = new_ref[0][:, LANES:].astype(bf16)
        s_new = jnp.where(new_ok, _dot_nt(q, k_new), NEG)
        m = jnp.maximum(jnp.max(s, axis=-1, keepdims=True), jnp.max(s_new, axis=-1, keepdims=True))
        p = jnp.where(allowed, jnp.exp(s - m), 0.0)
        p_new = jnp.where(new_ok, jnp.exp(s_new - m), 0.0)
        l = jnp.sum(p, axis=-1, keepdims=True) + jnp.sum(p_new, axis=-1, keepdims=True)
        return (_dot(p.astype(bf16), vals) + _dot(p_new.astype(bf16), v_new)) / l

    pages = buf[slot].reshape(n_pages * PAGE_SIZE, KV_ROW)
    picked = _dot(sel_ref[0], e_ref[...]) > 0.5
    o_slc = attend(pages[:, :LANES].astype(bf16), pages[:, LANES:].astype(bf16), picked, snew_ref)

    cw = cw_ref[0]
    win_i = lax.broadcasted_iota(jnp.int32, (1, cw.shape[0]), 1)
    o_win = attend(cw[:, :LANES].astype(bf16), cw[:, LANES:].astype(bf16), win_i > tok + win_off, wnew_ref)

    gate = gate_ref[0]
    o = ocmp_ref[0] + gate[:, 1:2] * o_slc + gate[:, 2:3] * o_win
    second_kvh = lax.broadcasted_iota(jnp.int32, (rows, 1), 0) >= G_NSA * n_tok
    o_ref[0] = jnp.where(second_kvh, o[:, DH:], o[:, :DH])


def _slc_win_sample(cache_s, page_table, q_blk, sel_rows, kvs_new, cache_w, kvw_new, gate_rows, o_cmp, n_tok):
    b, n_pages = page_table.shape
    rows = H_NSA * n_tok
    past_len = n_pages * PAGE_SIZE
    wbuf = cache_w.shape[1]
    e = _block_expand(sel_rows.shape[-1], past_len)
    full = lambda a: pl.BlockSpec(a.shape, lambda i, pt: (0,) * a.ndim)
    seq = lambda a: pl.BlockSpec((1,) + a.shape[1:], lambda i, pt: (i,) + (0,) * (a.ndim - 1))
    grid_spec = pltpu.PrefetchScalarGridSpec(
        num_scalar_prefetch=1,
        grid=(b,),
        in_specs=[pl.BlockSpec(memory_space=pl.ANY), seq(q_blk), seq(sel_rows), full(e), seq(kvs_new),
                  seq(cache_w), seq(kvw_new), seq(gate_rows), seq(o_cmp)],
        out_specs=pl.BlockSpec((1, rows, DH), lambda i, pt: (i, 0, 0)),
        scratch_shapes=[pltpu.VMEM((2, n_pages, PAGE_SIZE, KV_ROW), f32), pltpu.SemaphoreType.DMA((2,))],
    )
    return pl.pallas_call(
        functools.partial(_slc_win_sample_kernel, n_pages=n_pages, n_tok=n_tok, win_off=wbuf - WINDOW),
        grid_spec=grid_spec,
        out_shape=jax.ShapeDtypeStruct((b, rows, DH), f32),
        compiler_params=_cparams(("arbitrary",)),
        name="slc_win_sample",
    )(page_table, cache_s, q_blk, sel_rows, e, kvs_new, cache_w, kvw_new, gate_rows, o_cmp)


def _gla_sample_kernel(qka_ref, v_ref, z_ref, s_ref, gain_ref, y_ref, so_ref, *, n_tok):
    gain = gain_ref[...]
    y_ref[...] = jnp.zeros(y_ref.shape, f32)
    for h in range(H_GLA):
        qka = qka_ref[0, h]
        s = s_ref[0, h]
        for t in range(n_tok):
            q_t = qka[:, t:t + 1]
            k_t = qka[:, n_tok + t:n_tok + t + 1]
            a_t = jnp.exp(qka[:, 2 * n_tok + t:2 * n_tok + t + 1])
            s = a_t * s + k_t * v_ref[0, t:t + 1, h * DV:(h + 1) * DV]
            o = jnp.sum(q_t * s, axis=0, keepdims=True)
            y_ref[0, t:t + 1, h * DV:(h + 1) * DV] = _rms_gain(o, gain) * z_ref[0, t:t + 1, h * DV:(h + 1) * DV]
        so_ref[0, h] = s


def _gla_sample(gq, gk, la, gv, zg, state, gain, b, n_tok):
    def cols(a):
        return a.reshape(b, n_tok, H_GLA, DK).transpose(0, 2, 3, 1)
    qka = jnp.concatenate([cols(gq), cols(gk), cols(la)], axis=-1)
    qka = jnp.pad(qka, ((0, 0), (0, 0), (0, 0), (0, LANES - 3 * n_tok)))
    pad_rows = lambda a: jnp.pad(a.reshape(b, n_tok, -1), ((0, 0), (0, 8 - n_tok), (0, 0)))
    v8, z8 = pad_rows(gv), pad_rows(zg)
    gain2 = gain.reshape(1, DV)
    seq = lambda a: pl.BlockSpec((1,) + a.shape[1:], lambda i: (i,) + (0,) * (a.ndim - 1))
    y, s_new = pl.pallas_call(
        functools.partial(_gla_sample_kernel, n_tok=n_tok),
        grid=(b,),
        in_specs=[seq(qka), seq(v8), seq(z8), seq(state), pl.BlockSpec((1, DV), lambda i: (0, 0))],
        out_specs=(seq(v8), seq(state)),
        out_shape=(jax.ShapeDtypeStruct(v8.shape, f32), jax.ShapeDtypeStruct(state.shape, f32)),
        compiler_params=_cparams(("parallel",)),
        name="gla_sample",
    )(qka, v8, z8, state, gain2)
    return y[:, :n_tok].reshape(b * n_tok, GLA_WIDTH), s_new


def _sample_path(x, cache_c, cache_s, cache_w, state, page_table, wts):
    (norm_g, w_pack, wa2_pad, ba, w1_big, cmp_bias, w2_big, gla_gain, w_out, out_gain) = wts
    b, n_tok, _ = x.shape
    n = b * n_tok
    n_pages = page_table.shape[1]
    past_len = n_pages * PAGE_SIZE
    assert n_tok <= 8 and past_len % SLC_LEN == 0 and past_len // SLC_LEN <= LANES
    pos = past_len + jnp.arange(n) % n_tok
    (q_hm, kvc, kvs, kvw, _, _, _, _, gate, zn, gq, gk, gv, la, zg) = _inproj(
        x.reshape(1, n, D_MODEL), pos, norm_g, w_pack, wa2_pad, ba, tm=n)

    rows = H_NSA * n_tok
    q_rows = q_hm[0].reshape(H_NSA, b, n_tok, DH).transpose(1, 0, 2, 3)
    zero = jnp.zeros_like(q_rows[:, :G_NSA])
    q_blk = jnp.concatenate([jnp.concatenate([q_rows[:, :G_NSA], zero], axis=-1),
                             jnp.concatenate([zero, q_rows[:, G_NSA:]], axis=-1)], axis=1).reshape(b, rows, LANES)
    gate_rows = gate[:, :3 * H_NSA].reshape(b, n_tok, 3, H_NSA).transpose(0, 3, 1, 2).reshape(b, rows, 3)
    gate_rows = jnp.pad(gate_rows, ((0, 0), (0, 0), (0, LANES - 3)))
    pad_new = lambda a: jnp.pad(a.reshape(b, n_tok, KV_ROW), ((0, 0), (0, 8 - n_tok), (0, 0)))

    part = _cmp_part_paged(cache_c, page_table, w1_big)
    o_cmp, sel_rows = _cmp_attn_sample(part, cmp_bias, w2_big, q_blk, gate_rows, n_tok, past_len)
    o_rows = _slc_win_sample(cache_s, page_table, q_blk, sel_rows, pad_new(kvs), cache_w, pad_new(kvw),
                             gate_rows, o_cmp, n_tok)
    o_nsa = o_rows.reshape(b, H_NSA, n_tok, DH).transpose(0, 2, 1, 3).reshape(n, NSA_WIDTH)

    y_gla, s_new = _gla_sample(gq, gk, la, gv, zg, state, gla_gain, b, n_tok)
    y = _outproj(x.reshape(n, D_MODEL), o_nsa, zn, y_gla, w_out, out_gain, tm=n)
    kv5 = lambda a: a.reshape(b, n_tok, 2, KVH, DH)
    win_new = jnp.concatenate([cache_w, kvw.reshape(b, n_tok, KV_ROW)], axis=1)[:, n_tok:]
    return (y.reshape(b, n_tok, D_MODEL), kv5(kvc), kv5(kvs), win_new.reshape(b, -1, 2, KVH, DH), s_new)


def kernel(x_prompt, x_sample, cache_cmp_kv, cache_slc_kv, cache_win_kv, state_gla, page_table,
           norm_in_gain, w_in, cmp_pe, cmp_w1, cmp_b1, cmp_w2, gla_wa2, gla_ba, gla_norm_gain,
           w_out, norm_out_gain):
    assert w_in.shape[0] == 1, "single-layer step"
    wts = (norm_in_gain[0], _pack_w_in(w_in[0]), _pad_wa2(gla_wa2[0]), gla_ba[0],
           _cmp_w1_big(cmp_w1[0]), _cmp_bias(cmp_pe[0], cmp_w1[0], cmp_b1[0]), _cmp_w2_big(cmp_w2[0]),
           gla_norm_gain[0], w_out[0].astype(bf16), norm_out_gain)
    n_pool = cache_cmp_kv.shape[1]
    yp, cmp_p, slc_p, win_p, gla_p = _prompt_path(x_prompt, wts)
    ys, cmp_s, slc_s, win_s, gla_s = _sample_path(
        x_sample, cache_cmp_kv[0].reshape(n_pool, PAGE_SIZE, KV_ROW), cache_slc_kv[0].reshape(n_pool, PAGE_SIZE, KV_ROW),
        cache_win_kv[0].reshape(cache_win_kv.shape[1], -1, KV_ROW), state_gla[0], page_table, wts)
    return (yp, ys, cmp_p[None], cmp_s[None], slc_p[None], slc_s[None], win_p[None], win_s[None],
            gla_p[None], gla_s[None])
```

```python
import functools

import numpy as np
import jax
import jax.numpy as jnp
from jax import lax
from jax.experimental import pallas as pl
from jax.experimental.pallas import tpu as pltpu

f32 = jnp.float32
bf16 = jnp.bfloat16

D_MODEL = 1024
DH = 64
H_NSA = 8
KVH = 2
G_NSA = 4
NSA_WIDTH = H_NSA * DH
CMP_LEN = 32
CMP_STRIDE = 16
CMP_HID = 128
SLC_LEN = 64
SLC_TOP = 16
WINDOW = 512
ROT_HALF = 8
ROPE_THETA = 500000.0
FORCE_SCORE = 1.0e4
H_GLA = 4
DK = 64
DV = 128
GLA_WIDTH = H_GLA * DV
GLA_LR = 16
GLA_TAU = 16.0
GLA_CHUNK = 64
GLA_SUB = 16
EPS = 1e-6
NEG = -1.0e30
PAGE_SIZE = 128
KV_ROW = 2 * KVH * DH

IN_SIZES = (H_NSA * DH, KV_ROW, KV_ROW, KV_ROW, 3 * H_NSA, NSA_WIDTH,
            H_GLA * DK, H_GLA * DK, H_GLA * DV, GLA_LR, GLA_WIDTH)
IN_OFFSETS = [0] + [int(v) for v in np.cumsum(IN_SIZES)]

LANES = 128
VMEM_LIMIT = 56 * 1024 * 1024

P_Q, P_KVC, P_KVS, P_KVW, P_ZN, P_GQ, P_GK, P_GV, P_ZG, P_MISC = (
    0, 512, 768, 1024, 1280, 1792, 2048, 2304, 2816, 3328)
D_PACK = P_MISC + LANES
MISC_GATE = 0
MISC_GLR = 32


def _cparams(sem):
    return pltpu.CompilerParams(dimension_semantics=sem, vmem_limit_bytes=VMEM_LIMIT)


def _sigmoid(x):
    return 1.0 / (1.0 + jnp.exp(-x))


def _silu(x):
    return x * _sigmoid(x)


def _log_sigmoid(x):
    return jnp.minimum(x, 0.0) - jnp.log1p(jnp.exp(-jnp.abs(x)))


def _dot(a, b):
    return jnp.dot(a, b, preferred_element_type=f32)


def _dot_nt(a, b):
    return lax.dot_general(a, b, (((1,), (1,)), ((), ())), preferred_element_type=f32)


def _dot_tn(a, b):
    return lax.dot_general(a, b, (((0,), (0,)), ((), ())), preferred_element_type=f32)


def _rope_tables(pos):
    n = pos.shape[0]
    inv = ROPE_THETA ** (-(jnp.arange(ROT_HALF, dtype=f32) / ROT_HALF))
    ang = pos.astype(f32)[:, None] * inv[None, :]
    cos, sin = jnp.cos(ang), jnp.sin(ang)
    z8 = jnp.zeros((n, ROT_HALF), f32)
    rest = DH - 2 * ROT_HALF
    c64 = jnp.concatenate([cos, cos, jnp.ones((n, rest), f32)], axis=-1)
    sa64 = jnp.concatenate([-sin, z8, jnp.zeros((n, rest), f32)], axis=-1)
    sb64 = jnp.concatenate([z8, sin, jnp.zeros((n, rest), f32)], axis=-1)
    tile = lambda t: jnp.concatenate([t, t], axis=-1)
    return tile(c64), tile(sa64), tile(sb64)


def _rope128(v, cos, sa, sb):
    return v * cos + pltpu.roll(v, LANES - ROT_HALF, 1) * sa + pltpu.roll(v, ROT_HALF, 1) * sb


def _inproj_kernel(x_ref, g_ref, w_ref, cos_ref, sa_ref, sb_ref, wa2_ref, ba_ref,
                   q_ref, kvc_ref, kvs_ref, kvw_ref, ks_ref, vs_ref, kw_ref, vw_ref,
                   gate_ref, zn_ref, gq_ref, gk_ref, gv_ref, la_ref, zg_ref):
    x = x_ref[...]
    ms = jnp.mean(x * x, axis=-1, keepdims=True)
    hn = (x * lax.rsqrt(ms + EPS) * g_ref[...]).astype(bf16)
    cos, sa, sb = cos_ref[...], sa_ref[...], sb_ref[...]

    def proj(off, width):
        return _dot(hn, w_ref[:, off:off + width])

    qp = proj(P_Q, NSA_WIDTH)
    for c in range(NSA_WIDTH // LANES):
        r = _rope128(qp[:, c * LANES:(c + 1) * LANES], cos, sa, sb) * (DH ** -0.5)
        q_ref[0, 2 * c] = r[:, :DH].astype(bf16)
        q_ref[0, 2 * c + 1] = r[:, DH:].astype(bf16)

    kvc_ref[...] = proj(P_KVC, KV_ROW)

    for off, kv_ref, k_ref, v_ref in ((P_KVS, kvs_ref, ks_ref, vs_ref), (P_KVW, kvw_ref, kw_ref, vw_ref)):
        p = proj(off, KV_ROW)
        k = _rope128(p[:, :LANES], cos, sa, sb)
        v = p[:, LANES:]
        kv_ref[:, :LANES] = k
        kv_ref[:, LANES:] = v
        for h in range(KVH):
            k_ref[0, h] = k[:, h * DH:(h + 1) * DH].astype(bf16)
            v_ref[0, h] = v[:, h * DH:(h + 1) * DH].astype(bf16)

    zn_ref[...] = _silu(proj(P_ZN, NSA_WIDTH))
    gq_ref[...] = proj(P_GQ, H_GLA * DK) * (DK ** -0.5)
    gk_ref[...] = proj(P_GK, H_GLA * DK)
    gv_ref[...] = proj(P_GV, H_GLA * DV)
    zg_ref[...] = _silu(proj(P_ZG, GLA_WIDTH))

    misc = proj(P_MISC, LANES)
    gate_ref[...] = _sigmoid(misc)
    xa = _dot(misc.astype(bf16), wa2_ref[...]) + ba_ref[...]
    la_ref[...] = _log_sigmoid(xa) / GLA_TAU


def _pack_w_in(w_in):
    o = IN_OFFSETS
    seg = lambda i: w_in[:, o[i]:o[i + 1]]
    misc = jnp.zeros((D_MODEL, LANES), w_in.dtype)
    misc = misc.at[:, MISC_GATE:MISC_GATE + 3 * H_NSA].set(seg(4))
    misc = misc.at[:, MISC_GLR:MISC_GLR + GLA_LR].set(seg(9))
    cols = [seg(0), seg(1), seg(2), seg(3), seg(5), seg(6), seg(7), seg(8), seg(10), misc]
    return jnp.concatenate(cols, axis=1).astype(bf16)


def _pad_wa2(wa2):
    pad = jnp.zeros((LANES, H_GLA * DK), wa2.dtype).at[MISC_GLR:MISC_GLR + GLA_LR].set(wa2)
    return pad.astype(bf16)


def _inproj(x, pos, norm_g, w_pack, wa2_pad, ba, tm):
    bk, lk, _ = x.shape
    n = bk * lk
    nt = lk // tm
    cos, sa, sb = _rope_tables(pos)
    tok = lambda w: pl.BlockSpec((tm, w), lambda i: (i, 0))
    tab = pl.BlockSpec((tm, LANES), lambda i: (i % nt, 0))
    full = lambda a: pl.BlockSpec(a.shape, lambda i: (0,) * a.ndim)
    hm = lambda h: pl.BlockSpec((1, h, tm, DH), lambda i: (i // nt, 0, i % nt, 0))
    g2 = norm_g.reshape(1, D_MODEL)
    ba2 = ba.reshape(1, H_GLA * DK)
    out_shape = (
        jax.ShapeDtypeStruct((bk, H_NSA, lk, DH), bf16),
        jax.ShapeDtypeStruct((n, KV_ROW), f32),
        jax.ShapeDtypeStruct((n, KV_ROW), f32),
        jax.ShapeDtypeStruct((n, KV_ROW), f32),
        jax.ShapeDtypeStruct((bk, KVH, lk, DH), bf16),
        jax.ShapeDtypeStruct((bk, KVH, lk, DH), bf16),
        jax.ShapeDtypeStruct((bk, KVH, lk, DH), bf16),
        jax.ShapeDtypeStruct((bk, KVH, lk, DH), bf16),
        jax.ShapeDtypeStruct((n, LANES), f32),
        jax.ShapeDtypeStruct((n, NSA_WIDTH), f32),
        jax.ShapeDtypeStruct((n, H_GLA * DK), f32),
        jax.ShapeDtypeStruct((n, H_GLA * DK), f32),
        jax.ShapeDtypeStruct((n, H_GLA * DV), f32),
        jax.ShapeDtypeStruct((n, H_GLA * DK), f32),
        jax.ShapeDtypeStruct((n, GLA_WIDTH), f32),
    )
    out_specs = (hm(H_NSA), tok(KV_ROW), tok(KV_ROW), tok(KV_ROW), hm(KVH), hm(KVH), hm(KVH), hm(KVH),
                 tok(LANES), tok(NSA_WIDTH), tok(H_GLA * DK), tok(H_GLA * DK), tok(H_GLA * DV),
                 tok(H_GLA * DK), tok(GLA_WIDTH))
    return pl.pallas_call(
        _inproj_kernel,
        grid=(n // tm,),
        in_specs=[tok(D_MODEL), full(g2), full(w_pack), tab, tab, tab, full(wa2_pad), full(ba2)],
        out_specs=out_specs,
        out_shape=out_shape,
        compiler_params=_cparams(("parallel",)),
        name="inproj",
    )(x.reshape(n, D_MODEL), g2, w_pack, cos, sa, sb, wa2_pad, ba2)


CHUNK_W = CMP_STRIDE * KV_ROW
PART_W = 2 * 2 * KVH * CMP_HID
HID_W = 2 * KVH * CMP_HID


def _cmp_w1_big(cmp_w1):
    w1r = cmp_w1.reshape(2, CMP_LEN // CMP_STRIDE, CMP_STRIDE, DH, CMP_HID)
    eye = jnp.eye(2, dtype=cmp_w1.dtype)
    big = jnp.einsum('crsdh,cC,kK->sCKdrckh', w1r, eye, eye)
    return big.reshape(CHUNK_W, PART_W).astype(bf16)


def _cmp_w2_big(cmp_w2):
    eye = jnp.eye(2, dtype=cmp_w2.dtype)
    big = jnp.einsum('chd,cC,kK->ckhCKd', cmp_w2, eye, eye)
    return big.reshape(HID_W, KV_ROW).astype(bf16)


def _cmp_part_kernel(x_ref, w_ref, o_ref):
    o_ref[...] = _dot(x_ref[...].astype(bf16), w_ref[...])


def _cmp_part(chunks, w1_big, tr):
    r = chunks.shape[0]
    return pl.pallas_call(
        _cmp_part_kernel,
        grid=(r // tr,),
        in_specs=[pl.BlockSpec((tr, CHUNK_W), lambda i: (i, 0)),
                  pl.BlockSpec((CHUNK_W, PART_W), lambda i: (0, 0))],
        out_specs=pl.BlockSpec((tr, PART_W), lambda i: (i, 0)),
        out_shape=jax.ShapeDtypeStruct((r, PART_W), f32),
        compiler_params=_cparams(("parallel",)),
        name="cmp_part",
    )(chunks, w1_big)


def _cmp_bias_kernel(pe_ref, w1_ref, b1_ref, o_ref):
    for c in range(2):
        o_ref[c] = _dot(pe_ref[c], w1_ref[c]) + b1_ref[c]


def _cmp_bias(cmp_pe, cmp_w1, cmp_b1):
    pe = jnp.broadcast_to(cmp_pe.reshape(2, 1, CMP_LEN * DH), (2, 8, CMP_LEN * DH))
    b1 = jnp.broadcast_to(cmp_b1.reshape(2, 1, CMP_HID), (2, 8, CMP_HID))
    out = pl.pallas_call(
        _cmp_bias_kernel,
        out_shape=jax.ShapeDtypeStruct((2, 8, CMP_HID), f32),
        name="cmp_bias",
    )(pe, cmp_w1, b1)
    row = out[:, 0, :]
    return jnp.broadcast_to(row[:, None, :], (2, KVH, CMP_HID)).reshape(1, HID_W)


def _compress_finish(part, bias, w2_big, cos, sa, sb):
    n = part.shape[0]
    pre = part[:, :HID_W] + pltpu.roll(part[:, HID_W:], n - 1, 0)
    h = _silu(pre + bias)
    out = _dot(h.astype(bf16), w2_big)
    return _rope128(out[:, :LANES], cos, sa, sb), out[:, LANES:]


def _masked_softmax(s, mask, axis):
    s = jnp.where(mask, s, NEG)
    e = jnp.exp(s - jnp.max(s, axis=axis, keepdims=True))
    return jnp.where(mask, e / jnp.sum(e, axis=axis, keepdims=True), 0.0)


def _split_bf16(x):
    hi = x.astype(bf16)
    return hi, (x - hi.astype(f32)).astype(bf16)


def _topk_mask(score, k, axis):
    n = score.shape[axis]
    idx = lax.broadcasted_iota(jnp.int32, score.shape, axis)
    sel = jnp.zeros(score.shape, f32)
    for _ in range(k):
        m = jnp.max(score, axis=axis, keepdims=True)
        first = jnp.min(jnp.where(score == m, idx, n), axis=axis, keepdims=True)
        pick = idx == first
        sel = jnp.where(pick, 1.0, sel)
        score = jnp.where(pick, NEG, score)
    return sel


def _cover_t(n_slc, n_chunk):
    start = np.arange(n_chunk)[None, :] * CMP_STRIDE
    j = np.arange(n_slc)[:, None]
    cov = (start < (j + 1) * SLC_LEN) & (start + CMP_LEN > j * SLC_LEN)
    return jnp.asarray(cov.astype(np.float32), dtype=bf16)


def _cmp_attn_kernel(part_ref, bias_ref, w2_ref, cos_ref, sa_ref, sb_ref, q_ref, gate_ref, cov_ref,
                     o_ref, sel_ref, kc_ref, vc_ref, *, tq):
    t = pl.program_id(1)

    @pl.when(t == 0)
    def _():
        k, v = _compress_finish(part_ref[0], bias_ref[...], w2_ref[...], cos_ref[...], sa_ref[...], sb_ref[...])
        for h in range(KVH):
            kc_ref[h] = k[:, h * DH:(h + 1) * DH].astype(bf16)
            vc_ref[h] = v[:, h * DH:(h + 1) * DH].astype(bf16)

    nc = kc_ref.shape[1]
    nb = cov_ref.shape[0]
    q0 = t * tq
    qpos_r = q0 + lax.broadcasted_iota(jnp.int32, (tq, 1), 0)
    qpos_c = q0 + lax.broadcasted_iota(jnp.int32, (1, tq), 1)
    end_c = lax.broadcasted_iota(jnp.int32, (1, nc), 1) * CMP_STRIDE + (CMP_LEN - 1)
    end_r = lax.broadcasted_iota(jnp.int32, (nc, 1), 0) * CMP_STRIDE + (CMP_LEN - 1)
    mask = end_c <= qpos_r
    mask_t = end_r <= qpos_c
    gate = gate_ref[...]
    cov = cov_ref[...]

    jblk = lax.broadcasted_iota(jnp.int32, (nb, tq), 0)
    qblk = qpos_c // SLC_LEN
    valid = jblk <= qblk
    forced = (jblk == 0) | (jblk == qblk) | (jblk == qblk - 1)

    outs = []
    for kh in range(KVH):
        kc, vc = kc_ref[kh], vc_ref[kh]
        psum = jnp.zeros((nc, tq), f32)
        for g in range(G_NSA):
            h = kh * G_NSA + g
            qg = q_ref[0, h]
            p = _masked_softmax(_dot_nt(qg, kc), mask, axis=-1)
            outs.append(_dot(p.astype(bf16), vc) * gate[:, h:h + 1])
            psum = psum + _masked_softmax(_dot_nt(kc, qg), mask_t, axis=0)
        hi, lo = _split_bf16(psum)
        imp = _dot(cov, hi) + _dot(cov, lo)
        score = jnp.where(valid, jnp.where(forced, FORCE_SCORE, imp), -1.0)
        sel_t = jnp.where(valid, _topk_mask(score, SLC_TOP, axis=0), 0.0)
        if nb < LANES:
            sel_t = jnp.concatenate([sel_t, jnp.zeros((LANES - nb, tq), f32)], axis=0)
        sel_ref[0, kh] = sel_t.T.astype(bf16)
    o_ref[0] = jnp.concatenate(outs, axis=-1)


def _cmp_attn(part, bias, w2_big, q_hm, gate, tq):
    b, nc, _ = part.shape
    lq = q_hm.shape[2]
    nt = lq // tq
    n_slc = lq // SLC_LEN
    end_pos = jnp.arange(nc) * CMP_STRIDE + (CMP_LEN - 1)
    cos, sa, sb = _rope_tables(end_pos)
    cov = _cover_t(n_slc, nc)
    full = lambda a: pl.BlockSpec(a.shape, lambda i, t: (0,) * a.ndim)
    return pl.pallas_call(
        functools.partial(_cmp_attn_kernel, tq=tq),
        grid=(b, nt),
        in_specs=[pl.BlockSpec((1, nc, PART_W), lambda i, t: (i, 0, 0)),
                  full(bias), full(w2_big), full(cos), full(sa), full(sb),
                  pl.BlockSpec((1, H_NSA, tq, DH), lambda i, t: (i, 0, t, 0)),
                  pl.BlockSpec((tq, LANES), lambda i, t: (i * nt + t, 0)),
                  full(cov)],
        out_specs=(pl.BlockSpec((1, tq, NSA_WIDTH), lambda i, t: (i, t, 0)),
                   pl.BlockSpec((1, KVH, tq, LANES), lambda i, t: (i, 0, t, 0))),
        out_shape=(jax.ShapeDtypeStruct((b, lq, NSA_WIDTH), f32),
                   jax.ShapeDtypeStruct((b, KVH, lq, LANES), bf16)),
        scratch_shapes=[pltpu.VMEM((KVH, nc, DH), bf16), pltpu.VMEM((KVH, nc, DH), bf16)],
        compiler_params=_cparams(("parallel", "arbitrary")),
        name="cmp_attn",
    )(part, bias, w2_big, cos, sa, sb, q_hm, gate, cov)


def _block_expand(n_slc_pad, n_keys):
    e = (np.arange(n_keys)[None, :] // SLC_LEN) == np.arange(n_slc_pad)[:, None]
    return jnp.asarray(e.astype(np.float32), dtype=bf16)


def _mask_bias(allowed):
    bias = jnp.where(allowed, 0.0, NEG)
    return jnp.concatenate([bias] * G_NSA, axis=0)


def _slc_win_kernel(q_ref, ks_ref, vs_ref, kw_ref, vw_ref, sel_ref, e_ref, gate_ref, ocmp_ref,
                    o_ref, s_ref, m_ref, l_ref, acc_ref, *, tq, tk):
    kh = pl.program_id(1)
    t = pl.program_id(2)
    q0 = t * tq
    qpos = q0 + lax.broadcasted_iota(jnp.int32, (tq, 1), 0)
    sel = sel_ref[0, 0]
    q = q_ref[0].reshape(G_NSA * tq, DH)

    n_kt = (q0 + tq + tk - 1) // tk
    m_ref[...] = jnp.full(m_ref.shape, NEG, f32)

    def lane_fold(x, op):
        out = x[:, :LANES]
        for c in range(1, x.shape[1] // LANES):
            out = op(out, x[:, c * LANES:(c + 1) * LANES])
        return out

    def score_body(kt, carry):
        k0 = pl.multiple_of(kt * tk, tk)
        kpos = k0 + lax.broadcasted_iota(jnp.int32, (1, tk), 1)
        picked = _dot(sel, e_ref[:, pl.ds(k0, tk)])
        s = _dot_nt(q, ks_ref[0, 0, pl.ds(k0, tk), :]) + _mask_bias((picked > 0.5) & (kpos <= qpos))
        s_ref[:, pl.ds(k0, tk)] = s
        m_ref[...] = jnp.maximum(m_ref[...], lane_fold(s, jnp.maximum))
        return carry

    lax.fori_loop(0, n_kt, score_body, 0)
    m = jnp.max(m_ref[...], axis=-1, keepdims=True)
    l_ref[...] = jnp.zeros(l_ref.shape, f32)
    acc_ref[...] = jnp.zeros(acc_ref.shape, f32)

    def pv_body(kt, carry):
        k0 = pl.multiple_of(kt * tk, tk)
        p = jnp.exp(s_ref[:, pl.ds(k0, tk)] - m)
        l_ref[...] += lane_fold(p, jnp.add)
        acc_ref[...] += _dot(p.astype(bf16), vs_ref[0, 0, pl.ds(k0, tk), :])
        return carry

    lax.fori_loop(0, n_kt, pv_body, 0)
    o_slc = acc_ref[...] / jnp.sum(l_ref[...], axis=-1, keepdims=True)

    band = WINDOW + tq
    w0 = pl.multiple_of(jnp.maximum(q0 - WINDOW, 0), tq)
    kpos = w0 + lax.broadcasted_iota(jnp.int32, (1, band), 1)
    s = _dot_nt(q, kw_ref[0, 0, pl.ds(w0, band), :]) + _mask_bias((kpos <= qpos) & (kpos > qpos - WINDOW))
    p = jnp.exp(s - jnp.max(s, axis=-1, keepdims=True))
    o_win = _dot(p.astype(bf16), vw_ref[0, 0, pl.ds(w0, band), :]) / jnp.sum(p, axis=-1, keepdims=True)

    gate = gate_ref[...]
    outs = []
    for g in range(G_NSA):
        rows = slice(g * tq, (g + 1) * tq)
        col = lambda c: jnp.where(kh == 0, gate[:, c * H_NSA + g:c * H_NSA + g + 1],
                                  gate[:, c * H_NSA + G_NSA + g:c * H_NSA + G_NSA + g + 1])
        outs.append(col(1) * o_slc[rows] + col(2) * o_win[rows])
    o_ref[0] = ocmp_ref[0] + jnp.concatenate(outs, axis=-1)


def _slc_win(q_hm, ks, vs, kw, vw, sel, gate, o_cmp, tq, tk):
    b, _, lq, _ = q_hm.shape
    nt = lq // tq
    e = _block_expand(LANES, lq)
    gw = G_NSA * DH
    kvspec = pl.BlockSpec((1, 1, lq, DH), lambda i, k, t: (i, k, 0, 0))
    return pl.pallas_call(
        functools.partial(_slc_win_kernel, tq=tq, tk=tk),
        grid=(b, KVH, nt),
        in_specs=[pl.BlockSpec((1, G_NSA, tq, DH), lambda i, k, t: (i, k, t, 0)),
                  kvspec, kvspec, kvspec, kvspec,
                  pl.BlockSpec((1, 1, tq, LANES), lambda i, k, t: (i, k, t, 0)),
                  pl.BlockSpec(e.shape, lambda i, k, t: (0, 0)),
                  pl.BlockSpec((tq, LANES), lambda i, k, t: (i * nt + t, 0)),
                  pl.BlockSpec((1, tq, gw), lambda i, k, t: (i, t, k))],
        out_specs=pl.BlockSpec((1, tq, gw), lambda i, k, t: (i, t, k)),
        out_shape=jax.ShapeDtypeStruct((b, lq, NSA_WIDTH), f32),
        scratch_shapes=[pltpu.VMEM((G_NSA * tq, lq), f32),
                        pltpu.VMEM((G_NSA * tq, LANES), f32),
                        pltpu.VMEM((G_NSA * tq, LANES), f32),
                        pltpu.VMEM((G_NSA * tq, DH), f32)],
        compiler_params=_cparams(("parallel", "parallel", "parallel")),
        name="slc_win",
    )(q_hm, ks, vs, kw, vw, sel, e, gate, o_cmp)


def _rms_gain(o, gain):
    return o * lax.rsqrt(jnp.mean(o * o, axis=-1, keepdims=True) + EPS) * gain


def _gla_kernel(q_ref, k_ref, v_ref, la_ref, z_ref, gain_ref, y_ref, st_ref, s_scr, *, tl):
    t = pl.program_id(2)

    @pl.when(t == 0)
    def _():
        s_scr[...] = jnp.zeros(s_scr.shape, f32)

    c = GLA_CHUNK
    row = lax.broadcasted_iota(jnp.int32, (c, c), 0)
    col = lax.broadcasted_iota(jnp.int32, (c, c), 1)
    causal = col <= row
    tril = jnp.where(causal, 1.0, 0.0).astype(bf16)
    lane = lax.broadcasted_iota(jnp.int32, (1, LANES), 1)
    head_mask = [jnp.where(lane < DK, 1.0, 0.0), jnp.where(lane >= DK, 1.0, 0.0)]
    gain = gain_ref[...]

    st = s_scr[...]
    for ci in range(tl // c):
        rows = slice(ci * c, (ci + 1) * c)
        q, k, la = q_ref[rows, :], k_ref[rows, :], la_ref[rows, :]
        hi, rest = la.astype(bf16), la - la.astype(bf16).astype(f32)
        mid, lo = rest.astype(bf16), (rest - rest.astype(bf16).astype(f32)).astype(bf16)
        bc = _dot(tril, hi) + _dot(tril, mid) + _dot(tril, lo)
        ref_row = bc[c // 2 - 1:c // 2, :]
        b_last = bc[c - 1:c, :]
        q_a = q * jnp.exp(bc - ref_row)
        k_a = (k * jnp.exp(ref_row - bc)).astype(bf16)
        q_s = q * jnp.exp(bc)
        k_s = k * jnp.exp(b_last - bc)
        st_b = st.astype(bf16)
        upd = st * jnp.exp(b_last)
        for h in range(2):
            mh = head_mask[h]
            vh = v_ref[rows, h * DV:(h + 1) * DV].astype(bf16)
            a = jnp.where(causal, _dot_nt((q_a * mh).astype(bf16), k_a), 0.0)
            o = _dot(a.astype(bf16), vh) + _dot_nt((q_s * mh).astype(bf16), st_b)
            upd = upd + _dot_tn(vh, (k_s * mh).astype(bf16))
            y_ref[rows, h * DV:(h + 1) * DV] = _rms_gain(o, gain) * z_ref[rows, h * DV:(h + 1) * DV]
        st = upd
    s_scr[...] = st
    st_ref[0, 0] = st


def _gla(gq, gk, gv, la, zg, gain, b, tl):
    n = gq.shape[0]
    nt = n // b // tl
    pairs = H_GLA // 2
    qk = pl.BlockSpec((tl, LANES), lambda i, p, t: (i * nt + t, p))
    vz = pl.BlockSpec((tl, 2 * DV), lambda i, p, t: (i * nt + t, p))
    gain2 = gain.reshape(1, DV)
    y, st = pl.pallas_call(
        functools.partial(_gla_kernel, tl=tl),
        grid=(b, pairs, nt),
        in_specs=[qk, qk, vz, qk, vz, pl.BlockSpec((1, DV), lambda i, p, t: (0, 0))],
        out_specs=(vz, pl.BlockSpec((1, 1, DV, LANES), lambda i, p, t: (i, p, 0, 0))),
        out_shape=(jax.ShapeDtypeStruct((n, GLA_WIDTH), f32),
                   jax.ShapeDtypeStruct((b, pairs, DV, LANES), f32)),
        scratch_shapes=[pltpu.VMEM((DV, LANES), f32)],
        compiler_params=_cparams(("parallel", "parallel", "arbitrary")),
        name="gla",
    )(gq, gk, gv, la, zg, gain2)
    state = st.reshape(b, pairs, DV, 2, DK).transpose(0, 1, 3, 4, 2).reshape(b, H_GLA, DK, DV)
    return y, state


def _outproj_kernel(x_ref, on_ref, zn_ref, yg_ref, w_ref, g_ref, y_ref):
    y_nsa = (on_ref[...] * zn_ref[...]).astype(bf16)
    mix = _dot(y_nsa, w_ref[:NSA_WIDTH, :]) + _dot(yg_ref[...].astype(bf16), w_ref[NSA_WIDTH:, :])
    y_ref[...] = _rms_gain(x_ref[...] + mix, g_ref[...])


def _outproj(x, o_nsa, zn, y_gla, w_out, gain, tm):
    n = x.shape[0]
    tok = lambda w: pl.BlockSpec((tm, w), lambda i: (i, 0))
    g2 = gain.reshape(1, D_MODEL)
    return pl.pallas_call(
        _outproj_kernel,
        grid=(n // tm,),
        in_specs=[tok(D_MODEL), tok(NSA_WIDTH), tok(NSA_WIDTH), tok(GLA_WIDTH),
                  pl.BlockSpec(w_out.shape, lambda i: (0, 0)), pl.BlockSpec(g2.shape, lambda i: (0, 0))],
        out_specs=tok(D_MODEL),
        out_shape=jax.ShapeDtypeStruct((n, D_MODEL), f32),
        compiler_params=_cparams(("parallel",)),
        name="outproj",
    )(x, o_nsa, zn, y_gla, w_out, g2)


def _prompt_path(x, wts):
    (norm_g, w_pack, wa2_pad, ba, w1_big, cmp_bias, w2_big, gla_gain, w_out, out_gain) = wts
    b, lq, _ = x.shape
    (q_hm, kvc, kvs, kvw, ks, vs, kw, vw, gate, zn, gq, gk, gv, la, zg) = _inproj(
        x, jnp.arange(lq), norm_g, w_pack, wa2_pad, ba, tm=512)
    n_chunk = lq // CMP_STRIDE
    part = _cmp_part(kvc.reshape(b * n_chunk, CHUNK_W), w1_big, tr=n_chunk)
    o_cmp, sel = _cmp_attn(part.reshape(b, n_chunk, PART_W), cmp_bias, w2_big, q_hm, gate, tq=256)
    o_nsa = _slc_win(q_hm, ks, vs, kw, vw, sel, gate, o_cmp, tq=256, tk=512)
    y_gla, state = _gla(gq, gk, gv, la, zg, gla_gain, b, tl=512)
    y = _outproj(x.reshape(b * lq, D_MODEL), o_nsa.reshape(b * lq, NSA_WIDTH), zn, y_gla, w_out, out_gain, tm=512)
    kv5 = lambda a: a.reshape(b, lq, 2, KVH, DH)
    wlen = min(WINDOW, lq)
    return (y.reshape(b, lq, D_MODEL), kv5(kvc), kv5(kvs), kv5(kvw)[:, lq - wlen:], state)


def _start_page_gather(pt_ref, cache_ref, buf, sem, seq, slot, n_pages):
    for p in range(n_pages):
        pltpu.make_async_copy(cache_ref.at[pt_ref[seq, p]], buf.at[slot, p], sem.at[slot]).start()


def _wait_page_gather(cache_ref, buf, sem, slot, n_pages):
    for p in range(n_pages):
        pltpu.make_async_copy(cache_ref.at[0], buf.at[slot, p], sem.at[slot]).wait()


def _gathered_pages(pt_ref, cache_ref, buf, sem, n_pages):
    i = pl.program_id(0)
    slot = i % 2

    @pl.when(i == 0)
    def _():
        _start_page_gather(pt_ref, cache_ref, buf, sem, 0, 0, n_pages)

    @pl.when(i + 1 < pl.num_programs(0))
    def _():
        _start_page_gather(pt_ref, cache_ref, buf, sem, i + 1, 1 - slot, n_pages)

    _wait_page_gather(cache_ref, buf, sem, slot, n_pages)
    return slot


def _cmp_part_paged_kernel(pt_ref, cache_ref, w_ref, o_ref, buf, sem, *, n_pages):
    slot = _gathered_pages(pt_ref, cache_ref, buf, sem, n_pages)
    x = buf[slot].reshape(n_pages * (PAGE_SIZE // CMP_STRIDE), CHUNK_W)
    o_ref[0] = _dot(x.astype(bf16), w_ref[...])


def _cmp_part_paged(cache, page_table, w1_big):
    b, n_pages = page_table.shape
    cpp = PAGE_SIZE // CMP_STRIDE
    n_chunk = n_pages * cpp
    chunks = cache.reshape(cache.shape[0], cpp, CHUNK_W)
    grid_spec = pltpu.PrefetchScalarGridSpec(
        num_scalar_prefetch=1,
        grid=(b,),
        in_specs=[pl.BlockSpec(memory_space=pl.ANY),
                  pl.BlockSpec((CHUNK_W, PART_W), lambda i, pt: (0, 0))],
        out_specs=pl.BlockSpec((1, n_chunk, PART_W), lambda i, pt: (i, 0, 0)),
        scratch_shapes=[pltpu.VMEM((2, n_pages, cpp, CHUNK_W), f32), pltpu.SemaphoreType.DMA((2,))],
    )
    return pl.pallas_call(
        functools.partial(_cmp_part_paged_kernel, n_pages=n_pages),
        grid_spec=grid_spec,
        out_shape=jax.ShapeDtypeStruct((b, n_chunk, PART_W), f32),
        compiler_params=_cparams(("arbitrary",)),
        name="cmp_part_paged",
    )(page_table, chunks, w1_big)


def _group_sum_matrix(n_tok):
    r = np.arange(KVH * n_tok)[:, None]
    c = np.arange(H_NSA * n_tok)[None, :]
    m = ((c // (G_NSA * n_tok)) == (r // n_tok)) & ((c % n_tok) == (r % n_tok))
    return m.astype(np.float32)


def _cmp_attn_sample_kernel(part_ref, bias_ref, w2_ref, cos_ref, sa_ref, sb_ref, q_ref, gate_ref, cov_ref,
                            gs_ref, gst_ref, o_ref, sel_ref, *, n_tok, past_len):
    k, v = _compress_finish(part_ref[0], bias_ref[...], w2_ref[...], cos_ref[...], sa_ref[...], sb_ref[...])
    nc = k.shape[0]
    rows = H_NSA * n_tok
    qpos = past_len + lax.broadcasted_iota(jnp.int32, (rows, 1), 0) % n_tok
    end_c = lax.broadcasted_iota(jnp.int32, (1, nc), 1) * CMP_STRIDE + (CMP_LEN - 1)
    mask = end_c <= qpos
    p = _masked_softmax(_dot_nt(q_ref[0], k.astype(bf16)), mask, axis=-1)
    o_ref[0] = _dot(p.astype(bf16), v.astype(bf16)) * gate_ref[0][:, 0:1]
    hi, lo = _split_bf16(p)
    gs = gs_ref[...]
    ph, pl_ = _split_bf16(_dot(gs, hi) + _dot(gs, lo))
    imp = _dot(ph, cov_ref[...]) + _dot(pl_, cov_ref[...])
    n_blk = imp.shape[1]
    jblk = lax.broadcasted_iota(jnp.int32, imp.shape, 1)
    forced = (jblk == 0) | (jblk == n_blk - 1)
    sel = _topk_mask(jnp.where(forced, FORCE_SCORE, imp), SLC_TOP - 1, axis=1)
    sel_ref[0] = _dot(gst_ref[...], sel.astype(bf16)).astype(bf16)


def _cmp_attn_sample(part, bias, w2_big, q_blk, gate_rows, n_tok, past_len):
    b, nc, _ = part.shape
    rows = H_NSA * n_tok
    n_blk = past_len // SLC_LEN
    end_pos = jnp.arange(nc) * CMP_STRIDE + (CMP_LEN - 1)
    cos, sa, sb = _rope_tables(end_pos)
    cov = _cover_t(n_blk, nc).T
    gs = jnp.asarray(_group_sum_matrix(n_tok), dtype=bf16)
    gst = gs.T
    full = lambda a: pl.BlockSpec(a.shape, lambda i: (0,) * a.ndim)
    seq = lambda a: pl.BlockSpec((1,) + a.shape[1:], lambda i: (i,) + (0,) * (a.ndim - 1))
    return pl.pallas_call(
        functools.partial(_cmp_attn_sample_kernel, n_tok=n_tok, past_len=past_len),
        grid=(b,),
        in_specs=[seq(part), full(bias), full(w2_big), full(cos), full(sa), full(sb),
                  seq(q_blk), seq(gate_rows), full(cov), full(gs), full(gst)],
        out_specs=(pl.BlockSpec((1, rows, LANES), lambda i: (i, 0, 0)),
                   pl.BlockSpec((1, rows, n_blk), lambda i: (i, 0, 0))),
        out_shape=(jax.ShapeDtypeStruct((b, rows, LANES), f32),
                   jax.ShapeDtypeStruct((b, rows, n_blk), bf16)),
        compiler_params=_cparams(("parallel",)),
        name="cmp_attn_sample",
    )(part, bias, w2_big, cos, sa, sb, q_blk, gate_rows, cov, gs, gst)


def _slc_win_sample_kernel(pt_ref, cache_ref, q_ref, sel_ref, e_ref, snew_ref, cw_ref, wnew_ref, gate_ref,
                           ocmp_ref, o_ref, buf, sem, *, n_pages, n_tok, win_off):
    slot = _gathered_pages(pt_ref, cache_ref, buf, sem, n_pages)
    rows = H_NSA * n_tok
    q = q_ref[0]
    tok = lax.broadcasted_iota(jnp.int32, (rows, 1), 0) % n_tok
    new_i = lax.broadcasted_iota(jnp.int32, (1, snew_ref.shape[1]), 1)
    new_ok = (new_i <= tok) & (new_i < n_tok)

    def attend(keys, vals, allowed, new_ref):
        s = jnp.where(allowed, _dot_nt(q, keys), NEG)
        k_new = new_ref[0][:, :LANES].astype(bf16)
        v_new = new_ref[0][:, LANES:].astype(bf16)
        s_new = jnp.where(new_ok, _dot_nt(q, k_new), NEG)
        m = jnp.maximum(jnp.max(s, axis=-1, keepdims=True), jnp.max(s_new, axis=-1, keepdims=True))
        p = jnp.where(allowed, jnp.exp(s - m), 0.0)
        p_new = jnp.where(new_ok, jnp.exp(s_new - m), 0.0)
        l = jnp.sum(p, axis=-1, keepdims=True) + jnp.sum(p_new, axis=-1, keepdims=True)
        return (_dot(p.astype(bf16), vals) + _dot(p_new.astype(bf16), v_new)) / l

    pages = buf[slot].reshape(n_pages * PAGE_SIZE, KV_ROW)
    picked = _dot(sel_ref[0], e_ref[...]) > 0.5
    o_slc = attend(pages[:, :LANES].astype(bf16), pages[:, LANES:].astype(bf16), picked, snew_ref)

    cw = cw_ref[0]
    win_i = lax.broadcasted_iota(jnp.int32, (1, cw.shape[0]), 1)
    o_win = attend(cw[:, :LANES].astype(bf16), cw[:, LANES:].astype(bf16), win_i > tok + win_off, wnew_ref)

    gate = gate_ref[0]
    o = ocmp_ref[0] + gate[:, 1:2] * o_slc + gate[:, 2:3] * o_win
    second_kvh = lax.broadcasted_iota(jnp.int32, (rows, 1), 0) >= G_NSA * n_tok
    o_ref[0] = jnp.where(second_kvh, o[:, DH:], o[:, :DH])


def _slc_win_sample(cache_s, page_table, q_blk, sel_rows, kvs_new, cache_w, kvw_new, gate_rows, o_cmp, n_tok):
    b, n_pages = page_table.shape
    rows = H_NSA * n_tok
    past_len = n_pages * PAGE_SIZE
    wbuf = cache_w.shape[1]
    e = _block_expand(sel_rows.shape[-1], past_len)
    full = lambda a: pl.BlockSpec(a.shape, lambda i, pt: (0,) * a.ndim)
    seq = lambda a: pl.BlockSpec((1,) + a.shape[1:], lambda i, pt: (i,) + (0,) * (a.ndim - 1))
    grid_spec = pltpu.PrefetchScalarGridSpec(
        num_scalar_prefetch=1,
        grid=(b,),
        in_specs=[pl.BlockSpec(memory_space=pl.ANY), seq(q_blk), seq(sel_rows), full(e), seq(kvs_new),
                  seq(cache_w), seq(kvw_new), seq(gate_rows), seq(o_cmp)],
        out_specs=pl.BlockSpec((1, rows, DH), lambda i, pt: (i, 0, 0)),
        scratch_shapes=[pltpu.VMEM((2, n_pages, PAGE_SIZE, KV_ROW), f32), pltpu.SemaphoreType.DMA((2,))],
    )
    return pl.pallas_call(
        functools.partial(_slc_win_sample_kernel, n_pages=n_pages, n_tok=n_tok, win_off=wbuf - WINDOW),
        grid_spec=grid_spec,
        out_shape=jax.ShapeDtypeStruct((b, rows, DH), f32),
        compiler_params=_cparams(("arbitrary",)),
        name="slc_win_sample",
    )(page_table, cache_s, q_blk, sel_rows, e, kvs_new, cache_w, kvw_new, gate_rows, o_cmp)


def _gla_sample_kernel(qka_ref, v_ref, z_ref, s_ref, gain_ref, y_ref, so_ref, *, n_tok):
    gain = gain_ref[...]
    y_ref[...] = jnp.zeros(y_ref.shape, f32)
    for h in range(H_GLA):
        qka = qka_ref[0, h]
        s = s_ref[0, h]
        for t in range(n_tok):
            q_t = qka[:, t:t + 1]
            k_t = qka[:, n_tok + t:n_tok + t + 1]
            a_t = jnp.exp(qka[:, 2 * n_tok + t:2 * n_tok + t + 1])
            s = a_t * s + k_t * v_ref[0, t:t + 1, h * DV:(h + 1) * DV]
            o = jnp.sum(q_t * s, axis=0, keepdims=True)
            y_ref[0, t:t + 1, h * DV:(h + 1) * DV] = _rms_gain(o, gain) * z_ref[0, t:t + 1, h * DV:(h + 1) * DV]
        so_ref[0, h] = s


def _gla_sample(gq, gk, la, gv, zg, state, gain, b, n_tok):
    def cols(a):
        return a.reshape(b, n_tok, H_GLA, DK).transpose(0, 2, 3, 1)
    qka = jnp.concatenate([cols(gq), cols(gk), cols(la)], axis=-1)
    qka = jnp.pad(qka, ((0, 0), (0, 0), (0, 0), (0, LANES - 3 * n_tok)))
    pad_rows = lambda a: jnp.pad(a.reshape(b, n_tok, -1), ((0, 0), (0, 8 - n_tok), (0, 0)))
    v8, z8 = pad_rows(gv), pad_rows(zg)
    gain2 = gain.reshape(1, DV)
    seq = lambda a: pl.BlockSpec((1,) + a.shape[1:], lambda i: (i,) + (0,) * (a.ndim - 1))
    y, s_new = pl.pallas_call(
        functools.partial(_gla_sample_kernel, n_tok=n_tok),
        grid=(b,),
        in_specs=[seq(qka), seq(v8), seq(z8), seq(state), pl.BlockSpec((1, DV), lambda i: (0, 0))],
        out_specs=(seq(v8), seq(state)),
        out_shape=(jax.ShapeDtypeStruct(v8.shape, f32), jax.ShapeDtypeStruct(state.shape, f32)),
        compiler_params=_cparams(("parallel",)),
        name="gla_sample",
    )(qka, v8, z8, state, gain2)
    return y[:, :n_tok].reshape(b * n_tok, GLA_WIDTH), s_new


def _sample_path(x, cache_c, cache_s, cache_w, state, page_table, wts):
    (norm_g, w_pack, wa2_pad, ba, w1_big, cmp_bias, w2_big, gla_gain, w_out, out_gain) = wts
    b, n_tok, _ = x.shape
    n = b * n_tok
    n_pages = page_table.shape[1]
    past_len = n_pages * PAGE_SIZE
    assert n_tok <= 8 and past_len % SLC_LEN == 0 and past_len // SLC_LEN <= LANES
    pos = past_len + jnp.arange(n) % n_tok
    (q_hm, kvc, kvs, kvw, _, _, _, _, gate, zn, gq, gk, gv, la, zg) = _inproj(
        x.reshape(1, n, D_MODEL), pos, norm_g, w_pack, wa2_pad, ba, tm=n)

    rows = H_NSA * n_tok
    q_rows = q_hm[0].reshape(H_NSA, b, n_tok, DH).transpose(1, 0, 2, 3)
    zero = jnp.zeros_like(q_rows[:, :G_NSA])
    q_blk = jnp.concatenate([jnp.concatenate([q_rows[:, :G_NSA], zero], axis=-1),
                             jnp.concatenate([zero, q_rows[:, G_NSA:]], axis=-1)], axis=1).reshape(b, rows, LANES)
    gate_rows = gate[:, :3 * H_NSA].reshape(b, n_tok, 3, H_NSA).transpose(0, 3, 1, 2).reshape(b, rows, 3)
    gate_rows = jnp.pad(gate_rows, ((0, 0), (0, 0), (0, LANES - 3)))
    pad_new = lambda a: jnp.pad(a.reshape(b, n_tok, KV_ROW), ((0, 0), (0, 8 - n_tok), (0, 0)))

    part = _cmp_part_paged(cache_c, page_table, w1_big)
    o_cmp, sel_rows = _cmp_attn_sample(part, cmp_bias, w2_big, q_blk, gate_rows, n_tok, past_len)
    o_rows = _slc_win_sample(cache_s, page_table, q_blk, sel_rows, pad_new(kvs), cache_w, pad_new(kvw),
                             gate_rows, o_cmp, n_tok)
    o_nsa = o_rows.reshape(b, H_NSA, n_tok, DH).transpose(0, 2, 1, 3).reshape(n, NSA_WIDTH)

    y_gla, s_new = _gla_sample(gq, gk, la, gv, zg, state, gla_gain, b, n_tok)
    y = _outproj(x.reshape(n, D_MODEL), o_nsa, zn, y_gla, w_out, out_gain, tm=n)
    kv5 = lambda a: a.reshape(b, n_tok, 2, KVH, DH)
    win_new = jnp.concatenate([cache_w, kvw.reshape(b, n_tok, KV_ROW)], axis=1)[:, n_tok:]
    return (y.reshape(b, n_tok, D_MODEL), kv5(kvc), kv5(kvs), win_new.reshape(b, -1, 2, KVH, DH), s_new)


def kernel(x_prompt, x_sample, cache_cmp_kv, cache_slc_kv, cache_win_kv, state_gla, page_table,
           norm_in_gain, w_in, cmp_pe, cmp_w1, cmp_b1, cmp_w2, gla_wa2, gla_ba, gla_norm_gain,
           w_out, norm_out_gain):
    assert w_in.shape[0] == 1, "single-layer step"
    wts = (norm_in_gain[0], _pack_w_in(w_in[0]), _pad_wa2(gla_wa2[0]), gla_ba[0],
           _cmp_w1_big(cmp_w1[0]), _cmp_bias(cmp_pe[0], cmp_w1[0], cmp_b1[0]), _cmp_w2_big(cmp_w2[0]),
           gla_norm_gain[0], w_out[0].astype(bf16), norm_out_gain)
    n_pool = cache_cmp_kv.shape[1]
    yp, cmp_p, slc_p, win_p, gla_p = _prompt_path(x_prompt, wts)
    ys, cmp_s, slc_s, win_s, gla_s = _sample_path(
        x_sample, cache_cmp_kv[0].reshape(n_pool, PAGE_SIZE, KV_ROW), cache_slc_kv[0].reshape(n_pool, PAGE_SIZE, KV_ROW),
        cache_win_kv[0].reshape(cache_win_kv.shape[1], -1, KV_ROW), state_gla[0], page_table, wts)
    return (yp, ys, cmp_p[None], cmp_s[None], slc_p[None], slc_s[None], win_p[None], win_s[None],
            gla_p[None], gla_s[None])
```

```python
import functools

import numpy as np
import jax
import jax.numpy as jnp
from jax import lax
from jax.experimental import pallas as pl
from jax.experimental.pallas import tpu as pltpu

f32 = jnp.float32
bf16 = jnp.bfloat16

D_MODEL = 1024
DH = 64
H_NSA = 8
KVH = 2
G_NSA = 4
NSA_WIDTH = H_NSA * DH
CMP_LEN = 32
CMP_STRIDE = 16
CMP_HID = 128
SLC_LEN = 64
SLC_TOP = 16
WINDOW = 512
ROT_HALF = 8
ROPE_THETA = 500000.0
FORCE_SCORE = 1.0e4
H_GLA = 4
DK = 64
DV = 128
GLA_WIDTH = H_GLA * DV
GLA_LR = 16
GLA_TAU = 16.0
GLA_CHUNK = 64
GLA_SUB = 16
EPS = 1e-6
NEG = -1.0e30
PAGE_SIZE = 128
KV_ROW = 2 * KVH * DH

IN_SIZES = (H_NSA * DH, KV_ROW, KV_ROW, KV_ROW, 3 * H_NSA, NSA_WIDTH,
            H_GLA * DK, H_GLA * DK, H_GLA * DV, GLA_LR, GLA_WIDTH)
IN_OFFSETS = [0] + [int(v) for v in np.cumsum(IN_SIZES)]

LANES = 128
VMEM_LIMIT = 56 * 1024 * 1024

P_Q, P_KVC, P_KVS, P_KVW, P_ZN, P_GQ, P_GK, P_GV, P_ZG, P_MISC = (
    0, 512, 768, 1024, 1280, 1792, 2048, 2304, 2816, 3328)
D_PACK = P_MISC + LANES
MISC_GATE = 0
MISC_GLR = 32


def _cparams(sem):
    return pltpu.CompilerParams(dimension_semantics=sem, vmem_limit_bytes=VMEM_LIMIT)


def _sigmoid(x):
    return 1.0 / (1.0 + jnp.exp(-x))


def _silu(x):
    return x * _sigmoid(x)


def _log_sigmoid(x):
    return jnp.minimum(x, 0.0) - jnp.log1p(jnp.exp(-jnp.abs(x)))


def _dot(a, b):
    return jnp.dot(a, b, preferred_element_type=f32)


def _dot_nt(a, b):
    return lax.dot_general(a, b, (((1,), (1,)), ((), ())), preferred_element_type=f32)


def _dot_tn(a, b):
    return lax.dot_general(a, b, (((0,), (0,)), ((), ())), preferred_element_type=f32)


def _rope_tables(pos):
    n = pos.shape[0]
    inv = ROPE_THETA ** (-(jnp.arange(ROT_HALF, dtype=f32) / ROT_HALF))
    ang = pos.astype(f32)[:, None] * inv[None, :]
    cos, sin = jnp.cos(ang), jnp.sin(ang)
    z8 = jnp.zeros((n, ROT_HALF), f32)
    rest = DH - 2 * ROT_HALF
    c64 = jnp.concatenate([cos, cos, jnp.ones((n, rest), f32)], axis=-1)
    sa64 = jnp.concatenate([-sin, z8, jnp.zeros((n, rest), f32)], axis=-1)
    sb64 = jnp.concatenate([z8, sin, jnp.zeros((n, rest), f32)], axis=-1)
    tile = lambda t: jnp.concatenate([t, t], axis=-1)
    return tile(c64), tile(sa64), tile(sb64)


def _rope128(v, cos, sa, sb):
    return v * cos + pltpu.roll(v, LANES - ROT_HALF, 1) * sa + pltpu.roll(v, ROT_HALF, 1) * sb


def _inproj_kernel(x_ref, g_ref, w_ref, cos_ref, sa_ref, sb_ref, wa2_ref, ba_ref,
                   q_ref, kvc_ref, kvs_ref, kvw_ref, ks_ref, vs_ref, kw_ref, vw_ref,
                   gate_ref, zn_ref, gq_ref, gk_ref, gv_ref, la_ref, zg_ref):
    x = x_ref[...]
    ms = jnp.mean(x * x, axis=-1, keepdims=True)
    hn = (x * lax.rsqrt(ms + EPS) * g_ref[...]).astype(bf16)
    cos, sa, sb = cos_ref[...], sa_ref[...], sb_ref[...]

    def proj(off, width):
        return _dot(hn, w_ref[:, off:off + width])

    qp = proj(P_Q, NSA_WIDTH)
    for c in range(NSA_WIDTH // LANES):
        r = _rope128(qp[:, c * LANES:(c + 1) * LANES], cos, sa, sb) * (DH ** -0.5)
        q_ref[0, 2 * c] = r[:, :DH].astype(bf16)
        q_ref[0, 2 * c + 1] = r[:, DH:].astype(bf16)

    kvc_ref[...] = proj(P_KVC, KV_ROW)

    for off, kv_ref, k_ref, v_ref in ((P_KVS, kvs_ref, ks_ref, vs_ref), (P_KVW, kvw_ref, kw_ref, vw_ref)):
        p = proj(off, KV_ROW)
        k = _rope128(p[:, :LANES], cos, sa, sb)
        v = p[:, LANES:]
        kv_ref[:, :LANES] = k
        kv_ref[:, LANES:] = v
        for h in range(KVH):
            k_ref[0, h] = k[:, h * DH:(h + 1) * DH].astype(bf16)
            v_ref[0, h] = v[:, h * DH:(h + 1) * DH].astype(bf16)

    zn_ref[...] = _silu(proj(P_ZN, NSA_WIDTH))
    gq_ref[...] = proj(P_GQ, H_GLA * DK) * (DK ** -0.5)
    gk_ref[...] = proj(P_GK, H_GLA * DK)
    gv_ref[...] = proj(P_GV, H_GLA * DV)
    zg_ref[...] = _silu(proj(P_ZG, GLA_WIDTH))

    misc = proj(P_MISC, LANES)
    gate_ref[...] = _sigmoid(misc)
    xa = _dot(misc.astype(bf16), wa2_ref[...]) + ba_ref[...]
    la_ref[...] = _log_sigmoid(xa) / GLA_TAU


def _pack_w_in(w_in):
    o = IN_OFFSETS
    seg = lambda i: w_in[:, o[i]:o[i + 1]]
    misc = jnp.zeros((D_MODEL, LANES), w_in.dtype)
    misc = misc.at[:, MISC_GATE:MISC_GATE + 3 * H_NSA].set(seg(4))
    misc = misc.at[:, MISC_GLR:MISC_GLR + GLA_LR].set(seg(9))
    cols = [seg(0), seg(1), seg(2), seg(3), seg(5), seg(6), seg(7), seg(8), seg(10), misc]
    return jnp.concatenate(cols, axis=1).astype(bf16)


def _pad_wa2(wa2):
    pad = jnp.zeros((LANES, H_GLA * DK), wa2.dtype).at[MISC_GLR:MISC_GLR + GLA_LR].set(wa2)
    return pad.astype(bf16)


def _inproj(x, pos, norm_g, w_pack, wa2_pad, ba, tm):
    bk, lk, _ = x.shape
    n = bk * lk
    nt = lk // tm
    cos, sa, sb = _rope_tables(pos)
    tok = lambda w: pl.BlockSpec((tm, w), lambda i: (i, 0))
    tab = pl.BlockSpec((tm, LANES), lambda i: (i % nt, 0))
    full = lambda a: pl.BlockSpec(a.shape, lambda i: (0,) * a.ndim)
    hm = lambda h: pl.BlockSpec((1, h, tm, DH), lambda i: (i // nt, 0, i % nt, 0))
    g2 = norm_g.reshape(1, D_MODEL)
    ba2 = ba.reshape(1, H_GLA * DK)
    out_shape = (
        jax.ShapeDtypeStruct((bk, H_NSA, lk, DH), bf16),
        jax.ShapeDtypeStruct((n, KV_ROW), f32),
        jax.ShapeDtypeStruct((n, KV_ROW), f32),
        jax.ShapeDtypeStruct((n, KV_ROW), f32),
        jax.ShapeDtypeStruct((bk, KVH, lk, DH), bf16),
        jax.ShapeDtypeStruct((bk, KVH, lk, DH), bf16),
        jax.ShapeDtypeStruct((bk, KVH, lk, DH), bf16),
        jax.ShapeDtypeStruct((bk, KVH, lk, DH), bf16),
        jax.ShapeDtypeStruct((n, LANES), f32),
        jax.ShapeDtypeStruct((n, NSA_WIDTH), f32),
        jax.ShapeDtypeStruct((n, H_GLA * DK), f32),
        jax.ShapeDtypeStruct((n, H_GLA * DK), f32),
        jax.ShapeDtypeStruct((n, H_GLA * DV), f32),
        jax.ShapeDtypeStruct((n, H_GLA * DK), f32),
        jax.ShapeDtypeStruct((n, GLA_WIDTH), f32),
    )
    out_specs = (hm(H_NSA), tok(KV_ROW), tok(KV_ROW), tok(KV_ROW), hm(KVH), hm(KVH), hm(KVH), hm(KVH),
                 tok(LANES), tok(NSA_WIDTH), tok(H_GLA * DK), tok(H_GLA * DK), tok(H_GLA * DV),
                 tok(H_GLA * DK), tok(GLA_WIDTH))
    return pl.pallas_call(
        _inproj_kernel,
        grid=(n // tm,),
        in_specs=[tok(D_MODEL), full(g2), full(w_pack), tab, tab, tab, full(wa2_pad), full(ba2)],
        out_specs=out_specs,
        out_shape=out_shape,
        compiler_params=_cparams(("parallel",)),
        name="inproj",
    )(x.reshape(n, D_MODEL), g2, w_pack, cos, sa, sb, wa2_pad, ba2)


CHUNK_W = CMP_STRIDE * KV_ROW
PART_W = 2 * 2 * KVH * CMP_HID
HID_W = 2 * KVH * CMP_HID


def _cmp_w1_big(cmp_w1):
    w1r = cmp_w1.reshape(2, CMP_LEN // CMP_STRIDE, CMP_STRIDE, DH, CMP_HID)
    eye = jnp.eye(2, dtype=cmp_w1.dtype)
    big = jnp.einsum('crsdh,cC,kK->sCKdrckh', w1r, eye, eye)
    return big.reshape(CHUNK_W, PART_W).astype(bf16)


def _cmp_w2_big(cmp_w2):
    eye = jnp.eye(2, dtype=cmp_w2.dtype)
    big = jnp.einsum('chd,cC,kK->ckhCKd', cmp_w2, eye, eye)
    return big.reshape(HID_W, KV_ROW).astype(bf16)


def _cmp_part_kernel(x_ref, w_ref, o_ref):
    o_ref[...] = _dot(x_ref[...].astype(bf16), w_ref[...])


def _cmp_part(chunks, w1_big, tr):
    r = chunks.shape[0]
    return pl.pallas_call(
        _cmp_part_kernel,
        grid=(r // tr,),
        in_specs=[pl.BlockSpec((tr, CHUNK_W), lambda i: (i, 0)),
                  pl.BlockSpec((CHUNK_W, PART_W), lambda i: (0, 0))],
        out_specs=pl.BlockSpec((tr, PART_W), lambda i: (i, 0)),
        out_shape=jax.ShapeDtypeStruct((r, PART_W), f32),
        compiler_params=_cparams(("parallel",)),
        name="cmp_part",
    )(chunks, w1_big)


def _cmp_bias_kernel(pe_ref, w1_ref, b1_ref, o_ref):
    for c in range(2):
        o_ref[c] = _dot(pe_ref[c], w1_ref[c]) + b1_ref[c]


def _cmp_bias(cmp_pe, cmp_w1, cmp_b1):
    pe = jnp.broadcast_to(cmp_pe.reshape(2, 1, CMP_LEN * DH), (2, 8, CMP_LEN * DH))
    b1 = jnp.broadcast_to(cmp_b1.reshape(2, 1, CMP_HID), (2, 8, CMP_HID))
    out = pl.pallas_call(
        _cmp_bias_kernel,
        out_shape=jax.ShapeDtypeStruct((2, 8, CMP_HID), f32),
        name="cmp_bias",
    )(pe, cmp_w1, b1)
    row = out[:, 0, :]
    return jnp.broadcast_to(row[:, None, :], (2, KVH, CMP_HID)).reshape(1, HID_W)


def _compress_finish(part, bias, w2_big, cos, sa, sb):
    n = part.shape[0]
    pre = part[:, :HID_W] + pltpu.roll(part[:, HID_W:], n - 1, 0)
    h = _silu(pre + bias)
    out = _dot(h.astype(bf16), w2_big)
    return _rope128(out[:, :LANES], cos, sa, sb), out[:, LANES:]


def _masked_softmax(s, mask, axis):
    s = jnp.where(mask, s, NEG)
    e = jnp.exp(s - jnp.max(s, axis=axis, keepdims=True))
    return jnp.where(mask, e / jnp.sum(e, axis=axis, keepdims=True), 0.0)


def _split_bf16(x):
    hi = x.astype(bf16)
    return hi, (x - hi.astype(f32)).astype(bf16)


def _topk_mask(score, k, axis):
    n = score.shape[axis]
    idx = lax.broadcasted_iota(jnp.int32, score.shape, axis)
    sel = jnp.zeros(score.shape, f32)
    for _ in range(k):
        m = jnp.max(score, axis=axis, keepdims=True)
        first = jnp.min(jnp.where(score == m, idx, n), axis=axis, keepdims=True)
        pick = idx == first
        sel = jnp.where(pick, 1.0, sel)
        score = jnp.where(pick, NEG, score)
    return sel


def _cover_t(n_slc, n_chunk):
    start = np.arange(n_chunk)[None, :] * CMP_STRIDE
    j = np.arange(n_slc)[:, None]
    cov = (start < (j + 1) * SLC_LEN) & (start + CMP_LEN > j * SLC_LEN)
    return jnp.asarray(cov.astype(np.float32), dtype=bf16)


def _cmp_attn_kernel(part_ref, bias_ref, w2_ref, cos_ref, sa_ref, sb_ref, q_ref, gate_ref, cov_ref,
                     o_ref, sel_ref, kc_ref, vc_ref, *, tq):
    t = pl.program_id(1)

    @pl.when(t == 0)
    def _():
        k, v = _compress_finish(part_ref[0], bias_ref[...], w2_ref[...], cos_ref[...], sa_ref[...], sb_ref[...])
        for h in range(KVH):
            kc_ref[h] = k[:, h * DH:(h + 1) * DH].astype(bf16)
            vc_ref[h] = v[:, h * DH:(h + 1) * DH].astype(bf16)

    nc = kc_ref.shape[1]
    nb = cov_ref.shape[0]
    q0 = t * tq
    qpos_r = q0 + lax.broadcasted_iota(jnp.int32, (tq, 1), 0)
    qpos_c = q0 + lax.broadcasted_iota(jnp.int32, (1, tq), 1)
    end_c = lax.broadcasted_iota(jnp.int32, (1, nc), 1) * CMP_STRIDE + (CMP_LEN - 1)
    end_r = lax.broadcasted_iota(jnp.int32, (nc, 1), 0) * CMP_STRIDE + (CMP_LEN - 1)
    mask = end_c <= qpos_r
    mask_t = end_r <= qpos_c
    gate = gate_ref[...]
    cov = cov_ref[...]

    jblk = lax.broadcasted_iota(jnp.int32, (nb, tq), 0)
    qblk = qpos_c // SLC_LEN
    valid = jblk <= qblk
    forced = (jblk == 0) | (jblk == qblk) | (jblk == qblk - 1)

    outs = []
    for kh in range(KVH):
        kc, vc = kc_ref[kh], vc_ref[kh]
        psum = jnp.zeros((nc, tq), f32)
        for g in range(G_NSA):
            h = kh * G_NSA + g
            qg = q_ref[0, h]
            p = _masked_softmax(_dot_nt(qg, kc), mask, axis=-1)
            outs.append(_dot(p.astype(bf16), vc) * gate[:, h:h + 1])
            psum = psum + _masked_softmax(_dot_nt(kc, qg), mask_t, axis=0)
        hi, lo = _split_bf16(psum)
        imp = _dot(cov, hi) + _dot(cov, lo)
        score = jnp.where(valid, jnp.where(forced, FORCE_SCORE, imp), -1.0)
        sel_t = jnp.where(valid, _topk_mask(score, SLC_TOP, axis=0), 0.0)
        if nb < LANES:
            sel_t = jnp.concatenate([sel_t, jnp.zeros((LANES - nb, tq), f32)], axis=0)
        sel_ref[0, kh] = sel_t.T.astype(bf16)
    o_ref[0] = jnp.concatenate(outs, axis=-1)


def _cmp_attn(part, bias, w2_big, q_hm, gate, tq):
    b, nc, _ = part.shape
    lq = q_hm.shape[2]
    nt = lq // tq
    n_slc = lq // SLC_LEN
    end_pos = jnp.arange(nc) * CMP_STRIDE + (CMP_LEN - 1)
    cos, sa, sb = _rope_tables(end_pos)
    cov = _cover_t(n_slc, nc)
    full = lambda a: pl.BlockSpec(a.shape, lambda i, t: (0,) * a.ndim)
    return pl.pallas_call(
        functools.partial(_cmp_attn_kernel, tq=tq),
        grid=(b, nt),
        in_specs=[pl.BlockSpec((1, nc, PART_W), lambda i, t: (i, 0, 0)),
                  full(bias), full(w2_big), full(cos), full(sa), full(sb),
                  pl.BlockSpec((1, H_NSA, tq, DH), lambda i, t: (i, 0, t, 0)),
                  pl.BlockSpec((tq, LANES), lambda i, t: (i * nt + t, 0)),
                  full(cov)],
        out_specs=(pl.BlockSpec((1, tq, NSA_WIDTH), lambda i, t: (i, t, 0)),
                   pl.BlockSpec((1, KVH, tq, LANES), lambda i, t: (i, 0, t, 0))),
        out_shape=(jax.ShapeDtypeStruct((b, lq, NSA_WIDTH), f32),
                   jax.ShapeDtypeStruct((b, KVH, lq, LANES), bf16)),
        scratch_shapes=[pltpu.VMEM((KVH, nc, DH), bf16), pltpu.VMEM((KVH, nc, DH), bf16)],
        compiler_params=_cparams(("parallel", "arbitrary")),
        name="cmp_attn",
    )(part, bias, w2_big, cos, sa, sb, q_hm, gate, cov)


def _block_expand(n_slc_pad, n_keys):
    e = (np.arange(n_keys)[None, :] // SLC_LEN) == np.arange(n_slc_pad)[:, None]
    return jnp.asarray(e.astype(np.float32), dtype=bf16)


def _mask_bias(allowed):
    bias = jnp.where(allowed, 0.0, NEG)
    return jnp.concatenate([bias] * G_NSA, axis=0)


def _slc_win_kernel(q_ref, ks_ref, vs_ref, kw_ref, vw_ref, sel_ref, e_ref, gate_ref, ocmp_ref,
                    o_ref, s_ref, m_ref, l_ref, acc_ref, *, tq, tk):
    kh = pl.program_id(1)
    t = pl.program_id(2)
    q0 = t * tq
    qpos = q0 + lax.broadcasted_iota(jnp.int32, (tq, 1), 0)
    sel = sel_ref[0, 0]
    q = q_ref[0].reshape(G_NSA * tq, DH)

    n_kt = (q0 + tq + tk - 1) // tk
    m_ref[...] = jnp.full(m_ref.shape, NEG, f32)

    def lane_fold(x, op):
        out = x[:, :LANES]
        for c in range(1, x.shape[1] // LANES):
            out = op(out, x[:, c * LANES:(c + 1) * LANES])
        return out

    def score_body(kt, carry):
        k0 = pl.multiple_of(kt * tk, tk)
        kpos = k0 + lax.broadcasted_iota(jnp.int32, (1, tk), 1)
        picked = _dot(sel, e_ref[:, pl.ds(k0, tk)])
        s = _dot_nt(q, ks_ref[0, 0, pl.ds(k0, tk), :]) + _mask_bias((picked > 0.5) & (kpos <= qpos))
        s_ref[:, pl.ds(k0, tk)] = s
        m_ref[...] = jnp.maximum(m_ref[...], lane_fold(s, jnp.maximum))
        return carry

    lax.fori_loop(0, n_kt, score_body, 0)
    m = jnp.max(m_ref[...], axis=-1, keepdims=True)
    l_ref[...] = jnp.zeros(l_ref.shape, f32)
    acc_ref[...] = jnp.zeros(acc_ref.shape, f32)

    def pv_body(kt, carry):
        k0 = pl.multiple_of(kt * tk, tk)
        p = jnp.exp(s_ref[:, pl.ds(k0, tk)] - m)
        l_ref[...] += lane_fold(p, jnp.add)
        acc_ref[...] += _dot(p.astype(bf16), vs_ref[0, 0, pl.ds(k0, tk), :])
        return carry

    lax.fori_loop(0, n_kt, pv_body, 0)
    o_slc = acc_ref[...] / jnp.sum(l_ref[...], axis=-1, keepdims=True)

    band = WINDOW + tq
    w0 = pl.multiple_of(jnp.maximum(q0 - WINDOW, 0), tq)
    kpos = w0 + lax.broadcasted_iota(jnp.int32, (1, band), 1)
    s = _dot_nt(q, kw_ref[0, 0, pl.ds(w0, band), :]) + _mask_bias((kpos <= qpos) & (kpos > qpos - WINDOW))
    p = jnp.exp(s - jnp.max(s, axis=-1, keepdims=True))
    o_win = _dot(p.astype(bf16), vw_ref[0, 0, pl.ds(w0, band), :]) / jnp.sum(p, axis=-1, keepdims=True)

    gate = gate_ref[...]
    outs = []
    for g in range(G_NSA):
        rows = slice(g * tq, (g + 1) * tq)
        col = lambda c: jnp.where(kh == 0, gate[:, c * H_NSA + g:c * H_NSA + g + 1],
                                  gate[:, c * H_NSA + G_NSA + g:c * H_NSA + G_NSA + g + 1])
        outs.append(col(1) * o_slc[rows] + col(2) * o_win[rows])
    o_ref[0] = ocmp_ref[0] + jnp.concatenate(outs, axis=-1)


def _slc_win(q_hm, ks, vs, kw, vw, sel, gate, o_cmp, tq, tk):
    b, _, lq, _ = q_hm.shape
    nt = lq // tq
    e = _block_expand(LANES, lq)
    gw = G_NSA * DH
    kvspec = pl.BlockSpec((1, 1, lq, DH), lambda i, k, t: (i, k, 0, 0))
    return pl.pallas_call(
        functools.partial(_slc_win_kernel, tq=tq, tk=tk),
        grid=(b, KVH, nt),
        in_specs=[pl.BlockSpec((1, G_NSA, tq, DH), lambda i, k, t: (i, k, t, 0)),
                  kvspec, kvspec, kvspec, kvspec,
                  pl.BlockSpec((1, 1, tq, LANES), lambda i, k, t: (i, k, t, 0)),
                  pl.BlockSpec(e.shape, lambda i, k, t: (0, 0)),
                  pl.BlockSpec((tq, LANES), lambda i, k, t: (i * nt + t, 0)),
                  pl.BlockSpec((1, tq, gw), lambda i, k, t: (i, t, k))],
        out_specs=pl.BlockSpec((1, tq, gw), lambda i, k, t: (i, t, k)),
        out_shape=jax.ShapeDtypeStruct((b, lq, NSA_WIDTH), f32),
        scratch_shapes=[pltpu.VMEM((G_NSA * tq, lq), f32),
                        pltpu.VMEM((G_NSA * tq, LANES), f32),
                        pltpu.VMEM((G_NSA * tq, LANES), f32),
                        pltpu.VMEM((G_NSA * tq, DH), f32)],
        compiler_params=_cparams(("parallel", "parallel", "parallel")),
        name="slc_win",
    )(q_hm, ks, vs, kw, vw, sel, e, gate, o_cmp)


def _rms_gain(o, gain):
    return o * lax.rsqrt(jnp.mean(o * o, axis=-1, keepdims=True) + EPS) * gain


def _gla_kernel(q_ref, k_ref, v_ref, la_ref, z_ref, gain_ref, y_ref, st_ref, s_scr, *, tl):
    t = pl.program_id(2)

    @pl.when(t == 0)
    def _():
        s_scr[...] = jnp.zeros(s_scr.shape, f32)

    c = GLA_CHUNK
    row = lax.broadcasted_iota(jnp.int32, (c, c), 0)
    col = lax.broadcasted_iota(jnp.int32, (c, c), 1)
    causal = col <= row
    tril = jnp.where(causal, 1.0, 0.0).astype(bf16)
    lane = lax.broadcasted_iota(jnp.int32, (1, LANES), 1)
    head_mask = [jnp.where(lane < DK, 1.0, 0.0), jnp.where(lane >= DK, 1.0, 0.0)]
    gain = gain_ref[...]

    st = s_scr[...]
    for ci in range(tl // c):
        rows = slice(ci * c, (ci + 1) * c)
        q, k, la = q_ref[rows, :], k_ref[rows, :], la_ref[rows, :]
        hi, rest = la.astype(bf16), la - la.astype(bf16).astype(f32)
        mid, lo = rest.astype(bf16), (rest - rest.astype(bf16).astype(f32)).astype(bf16)
        bc = _dot(tril, hi) + _dot(tril, mid) + _dot(tril, lo)
        ref_row = bc[c // 2 - 1:c // 2, :]
        b_last = bc[c - 1:c, :]
        q_a = q * jnp.exp(bc - ref_row)
        k_a = (k * jnp.exp(ref_row - bc)).astype(bf16)
        q_s = q * jnp.exp(bc)
        k_s = k * jnp.exp(b_last - bc)
        st_b = st.astype(bf16)
        upd = st * jnp.exp(b_last)
        for h in range(2):
            mh = head_mask[h]
            vh = v_ref[rows, h * DV:(h + 1) * DV].astype(bf16)
            a = jnp.where(causal, _dot_nt((q_a * mh).astype(bf16), k_a), 0.0)
            o = _dot(a.astype(bf16), vh) + _dot_nt((q_s * mh).astype(bf16), st_b)
            upd = upd + _dot_tn(vh, (k_s * mh).astype(bf16))
            y_ref[rows, h * DV:(h + 1) * DV] = _rms_gain(o, gain) * z_ref[rows, h * DV:(h + 1) * DV]
        st = upd
    s_scr[...] = st
    st_ref[0, 0] = st


def _gla(gq, gk, gv, la, zg, gain, b, tl):
    n = gq.shape[0]
    nt = n // b // tl
    pairs = H_GLA // 2
    qk = pl.BlockSpec((tl, LANES), lambda i, p, t: (i * nt + t, p))
    vz = pl.BlockSpec((tl, 2 * DV), lambda i, p, t: (i * nt + t, p))
    gain2 = gain.reshape(1, DV)
    y, st = pl.pallas_call(
        functools.partial(_gla_kernel, tl=tl),
        grid=(b, pairs, nt),
        in_specs=[qk, qk, vz, qk, vz, pl.BlockSpec((1, DV), lambda i, p, t: (0, 0))],
        out_specs=(vz, pl.BlockSpec((1, 1, DV, LANES), lambda i, p, t: (i, p, 0, 0))),
        out_shape=(jax.ShapeDtypeStruct((n, GLA_WIDTH), f32),
                   jax.ShapeDtypeStruct((b, pairs, DV, LANES), f32)),
        scratch_shapes=[pltpu.VMEM((DV, LANES), f32)],
        compiler_params=_cparams(("parallel", "parallel", "arbitrary")),
        name="gla",
    )(gq, gk, gv, la, zg, gain2)
    state = st.reshape(b, pairs, DV, 2, DK).transpose(0, 1, 3, 4, 2).reshape(b, H_GLA, DK, DV)
    return y, state


def _outproj_kernel(x_ref, on_ref, zn_ref, yg_ref, w_ref, g_ref, y_ref):
    y_nsa = (on_ref[...] * zn_ref[...]).astype(bf16)
    mix = _dot(y_nsa, w_ref[:NSA_WIDTH, :]) + _dot(yg_ref[...].astype(bf16), w_ref[NSA_WIDTH:, :])
    y_ref[...] = _rms_gain(x_ref[...] + mix, g_ref[...])


def _outproj(x, o_nsa, zn, y_gla, w_out, gain, tm):
    n = x.shape[0]
    tok = lambda w: pl.BlockSpec((tm, w), lambda i: (i, 0))
    g2 = gain.reshape(1, D_MODEL)
    return pl.pallas_call(
        _outproj_kernel,
        grid=(n // tm,),
        in_specs=[tok(D_MODEL), tok(NSA_WIDTH), tok(NSA_WIDTH), tok(GLA_WIDTH),
                  pl.BlockSpec(w_out.shape, lambda i: (0, 0)), pl.BlockSpec(g2.shape, lambda i: (0, 0))],
        out_specs=tok(D_MODEL),
        out_shape=jax.ShapeDtypeStruct((n, D_MODEL), f32),
        compiler_params=_cparams(("parallel",)),
        name="outproj",
    )(x, o_nsa, zn, y_gla, w_out, g2)


def _prompt_path(x, wts):
    (norm_g, w_pack, wa2_pad, ba, w1_big, cmp_bias, w2_big, gla_gain, w_out, out_gain) = wts
    b, lq, _ = x.shape
    (q_hm, kvc, kvs, kvw, ks, vs, kw, vw, gate, zn, gq, gk, gv, la, zg) = _inproj(
        x, jnp.arange(lq), norm_g, w_pack, wa2_pad, ba, tm=512)
    n_chunk = lq // CMP_STRIDE
    part = _cmp_part(kvc.reshape(b * n_chunk, CHUNK_W), w1_big, tr=n_chunk)
    o_cmp, sel = _cmp_attn(part.reshape(b, n_chunk, PART_W), cmp_bias, w2_big, q_hm, gate, tq=256)
    o_nsa = _slc_win(q_hm, ks, vs, kw, vw, sel, gate, o_cmp, tq=256, tk=512)
    y_gla, state = _gla(gq, gk, gv, la, zg, gla_gain, b, tl=512)
    y = _outproj(x.reshape(b * lq, D_MODEL), o_nsa.reshape(b * lq, NSA_WIDTH), zn, y_gla, w_out, out_gain, tm=512)
    kv5 = lambda a: a.reshape(b, lq, 2, KVH, DH)
    wlen = min(WINDOW, lq)
    return (y.reshape(b, lq, D_MODEL), kv5(kvc), kv5(kvs), kv5(kvw)[:, lq - wlen:], state)


def _pages_native(cache):
    n_pool = cache.shape[0]
    return cache.transpose(0, 2, 3, 4, 1).reshape(n_pool, 2, KVH * DH, cache.shape[1])


def _page_gather(make_copies):
    i = pl.program_id(0)
    slot = i % 2

    @pl.when(i == 0)
    def _():
        for c in make_copies(0, 0):
            c.start()

    @pl.when(i + 1 < pl.num_programs(0))
    def _():
        for c in make_copies(i + 1, 1 - slot):
            c.start()

    for c in make_copies(i, slot):
        c.wait()
    return slot


def _cmp_w1_pairs(cmp_w1):
    ratio = CMP_LEN // CMP_STRIDE
    w1r = cmp_w1.reshape(2, ratio, CMP_STRIDE // 2, 2, DH, CMP_HID)
    eye = jnp.eye(KVH, dtype=cmp_w1.dtype)
    big = jnp.einsum('crpjdh,kK->cpjKdrkh', w1r, eye)
    return big.reshape(2, CMP_STRIDE // 2, 2 * KVH * DH, ratio * KVH * CMP_HID).astype(bf16)


def _cmp_w2_heads(cmp_w2):
    eye = jnp.eye(KVH, dtype=cmp_w2.dtype)
    return jnp.einsum('chd,kK->ckhKd', cmp_w2, eye).reshape(2, KVH * CMP_HID, KVH * DH).astype(bf16)


def _group_sum_matrix(n_tok):
    r = np.arange(KVH * n_tok)[:, None]
    c = np.arange(H_NSA * n_tok)[None, :]
    m = ((c // (G_NSA * n_tok)) == (r // n_tok)) & ((c % n_tok) == (r % n_tok))
    return m.astype(np.float32)


def _cmp_attn_sample_kernel(pt_ref, cache_ref, w1_ref, bias_ref, w2_ref, cos_ref, sa_ref, sb_ref, q_ref,
                            gate_ref, cov_ref, gs_ref, gst_ref, o_ref, sel_ref, buf, rows_scr, sem,
                            *, n_pages, n_tok, past_len):
    def copies(seq, slot):
        return [pltpu.make_async_copy(cache_ref.at[pt_ref[seq, p]], buf.at[slot, p], sem.at[slot])
                for p in range(n_pages)]

    slot = _page_gather(copies)

    def to_rows(p, carry):
        r0 = pl.multiple_of(p * PAGE_SIZE, PAGE_SIZE)
        for c in range(2):
            rows_scr[c, pl.ds(r0, PAGE_SIZE), :] = buf[slot, p, c].T
        return carry

    lax.fori_loop(0, n_pages, to_rows, 0)

    nc = n_pages * (PAGE_SIZE // CMP_STRIDE)
    kv = []
    for c in range(2):
        part = jnp.zeros((nc, w1_ref.shape[-1]), f32)
        for pr in range(CMP_STRIDE // 2):
            lhs = jnp.concatenate([rows_scr[c, pl.ds(2 * pr + j, nc, stride=CMP_STRIDE), :] for j in range(2)],
                                  axis=-1)
            part = part + _dot(lhs.astype(bf16), w1_ref[c, pr])
        half = part.shape[1] // 2
        pre = part[:, :half] + pltpu.roll(part[:, half:], nc - 1, 0)
        hid = _silu(pre + bias_ref[:, c * half:(c + 1) * half])
        kv.append(_dot(hid.astype(bf16), w2_ref[c]))
    k = _rope128(kv[0], cos_ref[...], sa_ref[...], sb_ref[...])
    v = kv[1]
    rows = H_NSA * n_tok
    qpos = past_len + lax.broadcasted_iota(jnp.int32, (rows, 1), 0) % n_tok
    end_c = lax.broadcasted_iota(jnp.int32, (1, nc), 1) * CMP_STRIDE + (CMP_LEN - 1)
    mask = end_c <= qpos
    p = _masked_softmax(_dot_nt(q_ref[0], k.astype(bf16)), mask, axis=-1)
    o_ref[0] = _dot(p.astype(bf16), v.astype(bf16)) * gate_ref[0][:, 0:1]
    hi, lo = _split_bf16(p)
    gs = gs_ref[...]
    ph, pl_ = _split_bf16(_dot(gs, hi) + _dot(gs, lo))
    imp = _dot(ph, cov_ref[...]) + _dot(pl_, cov_ref[...])
    n_blk = imp.shape[1]
    jblk = lax.broadcasted_iota(jnp.int32, imp.shape, 1)
    forced = (jblk == 0) | (jblk == n_blk - 1)
    sel = _topk_mask(jnp.where(forced, FORCE_SCORE, imp), SLC_TOP - 1, axis=1)
    sel_ref[0] = _dot(gst_ref[...], sel.astype(bf16)).astype(bf16)


def _cmp_attn_sample(cache_t, page_table, w1_pairs, bias, w2_heads, q_blk, gate_rows, n_tok):
    b, n_pages = page_table.shape
    past_len = n_pages * PAGE_SIZE
    nc = past_len // CMP_STRIDE
    rows = H_NSA * n_tok
    n_blk = past_len // SLC_LEN
    end_pos = jnp.arange(nc) * CMP_STRIDE + (CMP_LEN - 1)
    cos, sa, sb = _rope_tables(end_pos)
    cov = _cover_t(n_blk, nc).T
    gs = jnp.asarray(_group_sum_matrix(n_tok), dtype=bf16)
    gst = gs.T
    full = lambda a: pl.BlockSpec(a.shape, lambda i, pt: (0,) * a.ndim)
    seq = lambda a: pl.BlockSpec((1,) + a.shape[1:], lambda i, pt: (i,) + (0,) * (a.ndim - 1))
    grid_spec = pltpu.PrefetchScalarGridSpec(
        num_scalar_prefetch=1,
        grid=(b,),
        in_specs=[pl.BlockSpec(memory_space=pl.ANY), full(w1_pairs), full(bias), full(w2_heads),
                  full(cos), full(sa), full(sb), seq(q_blk), seq(gate_rows), full(cov), full(gs), full(gst)],
        out_specs=(pl.BlockSpec((1, rows, LANES), lambda i, pt: (i, 0, 0)),
                   pl.BlockSpec((1, rows, n_blk), lambda i, pt: (i, 0, 0))),
        scratch_shapes=[pltpu.VMEM((2, n_pages) + cache_t.shape[1:], f32),
                        pltpu.VMEM((2, past_len, KVH * DH), f32),
                        pltpu.SemaphoreType.DMA((2,))],
    )
    return pl.pallas_call(
        functools.partial(_cmp_attn_sample_kernel, n_pages=n_pages, n_tok=n_tok, past_len=past_len),
        grid_spec=grid_spec,
        out_shape=(jax.ShapeDtypeStruct((b, rows, LANES), f32),
                   jax.ShapeDtypeStruct((b, rows, n_blk), bf16)),
        compiler_params=_cparams(("arbitrary",)),
        name="cmp_attn_sample",
    )(page_table, cache_t, w1_pairs, bias, w2_heads, cos, sa, sb, q_blk, gate_rows, cov, gs, gst)


def _slc_win_sample_kernel(pt_ref, cache_ref, q_ref, sel_ref, e_ref, snew_ref, cw_ref, wnew_ref, wnewt_ref,
                           gate_ref, ocmp_ref, o_ref, wout_ref, buf, sem, *, n_pages, n_tok, win_off):
    def copies(seq, slot):
        return [pltpu.make_async_copy(cache_ref.at[pt_ref[seq, p], c],
                                      buf.at[slot, c, :, pl.ds(p * PAGE_SIZE, PAGE_SIZE)], sem.at[slot])
                for p in range(n_pages) for c in range(2)]

    slot = _page_gather(copies)
    rows = H_NSA * n_tok
    q = q_ref[0]
    tok = lax.broadcasted_iota(jnp.int32, (rows, 1), 0) % n_tok
    new_i = lax.broadcasted_iota(jnp.int32, (1, snew_ref.shape[1]), 1)
    new_ok = (new_i <= tok) & (new_i < n_tok)

    def attend(keys_t, vals_t, allowed, new_ref):
        s = jnp.where(allowed, _dot(q, keys_t), NEG)
        k_new = new_ref[0][:, :LANES].astype(bf16)
        v_new = new_ref[0][:, LANES:].astype(bf16)
        s_new = jnp.where(new_ok, _dot_nt(q, k_new), NEG)
        m = jnp.maximum(jnp.max(s, axis=-1, keepdims=True), jnp.max(s_new, axis=-1, keepdims=True))
        p = jnp.where(allowed, jnp.exp(s - m), 0.0)
        p_new = jnp.where(new_ok, jnp.exp(s_new - m), 0.0)
        l = jnp.sum(p, axis=-1, keepdims=True) + jnp.sum(p_new, axis=-1, keepdims=True)
        return (_dot_nt(p.astype(bf16), vals_t) + _dot(p_new.astype(bf16), v_new)) / l

    picked = _dot(sel_ref[0], e_ref[...]) > 0.5
    o_slc = attend(buf[slot, 0].astype(bf16), buf[slot, 1].astype(bf16), picked, snew_ref)

    wbuf = cw_ref.shape[-1]
    win_i = lax.broadcasted_iota(jnp.int32, (1, wbuf), 1)
    o_win = attend(cw_ref[0, 0].astype(bf16), cw_ref[0, 1].astype(bf16), win_i > tok + win_off, wnew_ref)

    lane = lax.broadcasted_iota(jnp.int32, (1, LANES), 1)
    for c in range(2):
        shifted = pltpu.roll(cw_ref[0, c], wbuf - n_tok, 1)
        tail = jnp.where(lane >= LANES - n_tok, wnewt_ref[0, c], shifted[:, wbuf - LANES:])
        wout_ref[0, c] = jnp.concatenate([shifted[:, :wbuf - LANES], tail], axis=-1)

    gate = gate_ref[0]
    o = ocmp_ref[0] + gate[:, 1:2] * o_slc + gate[:, 2:3] * o_win
    second_kvh = lax.broadcasted_iota(jnp.int32, (rows, 1), 0) >= G_NSA * n_tok
    o_ref[0] = jnp.where(second_kvh, o[:, DH:], o[:, :DH])


def _slc_win_sample(cache_t, page_table, q_blk, sel_rows, kvs_new, win_t, kvw_new, kvw_new_t, gate_rows,
                    o_cmp, n_tok):
    b, n_pages = page_table.shape
    rows = H_NSA * n_tok
    past_len = n_pages * PAGE_SIZE
    wbuf = win_t.shape[-1]
    e = _block_expand(sel_rows.shape[-1], past_len)
    full = lambda a: pl.BlockSpec(a.shape, lambda i, pt: (0,) * a.ndim)
    seq = lambda a: pl.BlockSpec((1,) + a.shape[1:], lambda i, pt: (i,) + (0,) * (a.ndim - 1))
    grid_spec = pltpu.PrefetchScalarGridSpec(
        num_scalar_prefetch=1,
        grid=(b,),
        in_specs=[pl.BlockSpec(memory_space=pl.ANY), seq(q_blk), seq(sel_rows), full(e), seq(kvs_new),
                  seq(win_t), seq(kvw_new), seq(kvw_new_t), seq(gate_rows), seq(o_cmp)],
        out_specs=(pl.BlockSpec((1, rows, DH), lambda i, pt: (i, 0, 0)), seq(win_t)),
        scratch_shapes=[pltpu.VMEM((2, 2, KVH * DH, past_len), f32), pltpu.SemaphoreType.DMA((2,))],
    )
    return pl.pallas_call(
        functools.partial(_slc_win_sample_kernel, n_pages=n_pages, n_tok=n_tok, win_off=wbuf - WINDOW),
        grid_spec=grid_spec,
        out_shape=(jax.ShapeDtypeStruct((b, rows, DH), f32), jax.ShapeDtypeStruct(win_t.shape, f32)),
        compiler_params=_cparams(("arbitrary",)),
        name="slc_win_sample",
    )(page_table, cache_t, q_blk, sel_rows, e, kvs_new, win_t, kvw_new, kvw_new_t, gate_rows, o_cmp)


def _gla_sample_kernel(qka_ref, v_ref, z_ref, s_ref, gain_ref, y_ref, so_ref, *, n_tok):
    gain = gain_ref[...]
    y_ref[...] = jnp.zeros(y_ref.shape, f32)
    for h in range(H_GLA):
        qka = qka_ref[0, h]
        s = s_ref[0, h]
        for t in range(n_tok):
            q_t = qka[:, t:t + 1]
            k_t = qka[:, n_tok + t:n_tok + t + 1]
            a_t = jnp.exp(qka[:, 2 * n_tok + t:2 * n_tok + t + 1])
            s = a_t * s + k_t * v_ref[0, t:t + 1, h * DV:(h + 1) * DV]
            o = jnp.sum(q_t * s, axis=0, keepdims=True)
            y_ref[0, t:t + 1, h * DV:(h + 1) * DV] = _rms_gain(o, gain) * z_ref[0, t:t + 1, h * DV:(h + 1) * DV]
        so_ref[0, h] = s


def _gla_sample(gq, gk, la, gv, zg, state, gain, b, n_tok):
    def cols(a):
        return a.reshape(b, n_tok, H_GLA, DK).transpose(0, 2, 3, 1)
    qka = jnp.concatenate([cols(gq), cols(gk), cols(la)], axis=-1)
    qka = jnp.pad(qka, ((0, 0), (0, 0), (0, 0), (0, LANES - 3 * n_tok)))
    pad_rows = lambda a: jnp.pad(a.reshape(b, n_tok, -1), ((0, 0), (0, 8 - n_tok), (0, 0)))
    v8, z8 = pad_rows(gv), pad_rows(zg)
    gain2 = gain.reshape(1, DV)
    seq = lambda a: pl.BlockSpec((1,) + a.shape[1:], lambda i: (i,) + (0,) * (a.ndim - 1))
    y, s_new = pl.pallas_call(
        functools.partial(_gla_sample_kernel, n_tok=n_tok),
        grid=(b,),
        in_specs=[seq(qka), seq(v8), seq(z8), seq(state), pl.BlockSpec((1, DV), lambda i: (0, 0))],
        out_specs=(seq(v8), seq(state)),
        out_shape=(jax.ShapeDtypeStruct(v8.shape, f32), jax.ShapeDtypeStruct(state.shape, f32)),
        compiler_params=_cparams(("parallel",)),
        name="gla_sample",
    )(qka, v8, z8, state, gain2)
    return y[:, :n_tok].reshape(b * n_tok, GLA_WIDTH), s_new


def _sample_path(x, cache_c, cache_s, cache_w, state, page_table, wts, w1_pairs, w2_heads):
    (norm_g, w_pack, wa2_pad, ba, w1_big, cmp_bias, w2_big, gla_gain, w_out, out_gain) = wts
    b, n_tok, _ = x.shape
    n = b * n_tok
    n_pages = page_table.shape[1]
    past_len = n_pages * PAGE_SIZE
    assert n_tok <= 8 and past_len % SLC_LEN == 0 and past_len // SLC_LEN <= LANES
    pos = past_len + jnp.arange(n) % n_tok
    (q_hm, kvc, kvs, kvw, _, _, _, _, gate, zn, gq, gk, gv, la, zg) = _inproj(
        x.reshape(1, n, D_MODEL), pos, norm_g, w_pack, wa2_pad, ba, tm=n)

    rows = H_NSA * n_tok
    q_rows = q_hm[0].reshape(H_NSA, b, n_tok, DH).transpose(1, 0, 2, 3)
    zero = jnp.zeros_like(q_rows[:, :G_NSA])
    q_blk = jnp.concatenate([jnp.concatenate([q_rows[:, :G_NSA], zero], axis=-1),
                             jnp.concatenate([zero, q_rows[:, G_NSA:]], axis=-1)], axis=1).reshape(b, rows, LANES)
    gate_rows = gate[:, :3 * H_NSA].reshape(b, n_tok, 3, H_NSA).transpose(0, 3, 1, 2).reshape(b, rows, 3)
    gate_rows = jnp.pad(gate_rows, ((0, 0), (0, 0), (0, LANES - 3)))
    pad_new = lambda a: jnp.pad(a.reshape(b, n_tok, KV_ROW), ((0, 0), (0, 8 - n_tok), (0, 0)))

    kvw_new_t = kvw.reshape(b, n_tok, 2, KVH * DH).transpose(0, 2, 3, 1)
    kvw_new_t = jnp.pad(kvw_new_t, ((0, 0), (0, 0), (0, 0), (LANES - n_tok, 0)))

    o_cmp, sel_rows = _cmp_attn_sample(_pages_native(cache_c), page_table, w1_pairs, cmp_bias, w2_heads,
                                       q_blk, gate_rows, n_tok)
    o_rows, win_t = _slc_win_sample(_pages_native(cache_s), page_table, q_blk, sel_rows, pad_new(kvs),
                                    _pages_native(cache_w), pad_new(kvw), kvw_new_t, gate_rows, o_cmp, n_tok)
    o_nsa = o_rows.reshape(b, H_NSA, n_tok, DH).transpose(0, 2, 1, 3).reshape(n, NSA_WIDTH)

    y_gla, s_new = _gla_sample(gq, gk, la, gv, zg, state, gla_gain, b, n_tok)
    y = _outproj(x.reshape(n, D_MODEL), o_nsa, zn, y_gla, w_out, out_gain, tm=n)
    kv5 = lambda a: a.reshape(b, n_tok, 2, KVH, DH)
    win_new = win_t.reshape(b, 2, KVH, DH, -1).transpose(0, 4, 1, 2, 3)
    return (y.reshape(b, n_tok, D_MODEL), kv5(kvc), kv5(kvs), win_new, s_new)


def kernel(x_prompt, x_sample, cache_cmp_kv, cache_slc_kv, cache_win_kv, state_gla, page_table,
           norm_in_gain, w_in, cmp_pe, cmp_w1, cmp_b1, cmp_w2, gla_wa2, gla_ba, gla_norm_gain,
           w_out, norm_out_gain):
    assert w_in.shape[0] == 1, "single-layer step"
    wts = (norm_in_gain[0], _pack_w_in(w_in[0]), _pad_wa2(gla_wa2[0]), gla_ba[0],
           _cmp_w1_big(cmp_w1[0]), _cmp_bias(cmp_pe[0], cmp_w1[0], cmp_b1[0]), _cmp_w2_big(cmp_w2[0]),
           gla_norm_gain[0], w_out[0].astype(bf16), norm_out_gain)
    yp, cmp_p, slc_p, win_p, gla_p = _prompt_path(x_prompt, wts)
    ys, cmp_s, slc_s, win_s, gla_s = _sample_path(
        x_sample, cache_cmp_kv[0], cache_slc_kv[0], cache_win_kv[0], state_gla[0], page_table, wts,
        _cmp_w1_pairs(cmp_w1[0]), _cmp_w2_heads(cmp_w2[0]))
    return (yp, ys, cmp_p[None], cmp_s[None], slc_p[None], slc_s[None], win_p[None], win_s[None],
            gla_p[None], gla_s[None])
```

```python
import functools

import numpy as np
import jax
import jax.numpy as jnp
from jax import lax
from jax.experimental import pallas as pl
from jax.experimental.pallas import tpu as pltpu

f32 = jnp.float32
bf16 = jnp.bfloat16

D_MODEL = 1024
DH = 64
H_NSA = 8
KVH = 2
G_NSA = 4
NSA_WIDTH = H_NSA * DH
CMP_LEN = 32
CMP_STRIDE = 16
CMP_HID = 128
SLC_LEN = 64
SLC_TOP = 16
WINDOW = 512
ROT_HALF = 8
ROPE_THETA = 500000.0
FORCE_SCORE = 1.0e4
H_GLA = 4
DK = 64
DV = 128
GLA_WIDTH = H_GLA * DV
GLA_LR = 16
GLA_TAU = 16.0
GLA_CHUNK = 64
GLA_SUB = 16
EPS = 1e-6
NEG = -1.0e30
PAGE_SIZE = 128
KV_ROW = 2 * KVH * DH

IN_SIZES = (H_NSA * DH, KV_ROW, KV_ROW, KV_ROW, 3 * H_NSA, NSA_WIDTH,
            H_GLA * DK, H_GLA * DK, H_GLA * DV, GLA_LR, GLA_WIDTH)
IN_OFFSETS = [0] + [int(v) for v in np.cumsum(IN_SIZES)]

LANES = 128
VMEM_LIMIT = 56 * 1024 * 1024

P_Q, P_KVC, P_KVS, P_KVW, P_ZN, P_GQ, P_GK, P_GV, P_ZG, P_MISC = (
    0, 512, 768, 1024, 1280, 1792, 2048, 2304, 2816, 3328)
D_PACK = P_MISC + LANES
MISC_GATE = 0
MISC_GLR = 32


def _cparams(sem):
    return pltpu.CompilerParams(dimension_semantics=sem, vmem_limit_bytes=VMEM_LIMIT)


def _sigmoid(x):
    return 1.0 / (1.0 + jnp.exp(-x))


def _silu(x):
    return x * _sigmoid(x)


def _log_sigmoid(x):
    return jnp.minimum(x, 0.0) - jnp.log1p(jnp.exp(-jnp.abs(x)))


def _dot(a, b):
    return jnp.dot(a, b, preferred_element_type=f32)


def _dot_nt(a, b):
    return lax.dot_general(a, b, (((1,), (1,)), ((), ())), preferred_element_type=f32)


def _dot_tn(a, b):
    return lax.dot_general(a, b, (((0,), (0,)), ((), ())), preferred_element_type=f32)


def _rope_tables(pos):
    n = pos.shape[0]
    inv = ROPE_THETA ** (-(jnp.arange(ROT_HALF, dtype=f32) / ROT_HALF))
    ang = pos.astype(f32)[:, None] * inv[None, :]
    cos, sin = jnp.cos(ang), jnp.sin(ang)
    z8 = jnp.zeros((n, ROT_HALF), f32)
    rest = DH - 2 * ROT_HALF
    c64 = jnp.concatenate([cos, cos, jnp.ones((n, rest), f32)], axis=-1)
    sa64 = jnp.concatenate([-sin, z8, jnp.zeros((n, rest), f32)], axis=-1)
    sb64 = jnp.concatenate([z8, sin, jnp.zeros((n, rest), f32)], axis=-1)
    tile = lambda t: jnp.concatenate([t, t], axis=-1)
    return tile(c64), tile(sa64), tile(sb64)


def _rope128(v, cos, sa, sb):
    return v * cos + pltpu.roll(v, LANES - ROT_HALF, 1) * sa + pltpu.roll(v, ROT_HALF, 1) * sb


def _inproj_kernel(x_ref, g_ref, w_ref, cos_ref, sa_ref, sb_ref, wa2_ref, ba_ref,
                   q_ref, kvc_ref, kvs_ref, kvw_ref, ks_ref, vs_ref, kw_ref, vw_ref,
                   gate_ref, zn_ref, gq_ref, gk_ref, gv_ref, la_ref, zg_ref):
    x = x_ref[...]
    ms = jnp.mean(x * x, axis=-1, keepdims=True)
    hn = (x * lax.rsqrt(ms + EPS) * g_ref[...]).astype(bf16)
    cos, sa, sb = cos_ref[...], sa_ref[...], sb_ref[...]

    def proj(off, width):
        return _dot(hn, w_ref[:, off:off + width])

    qp = proj(P_Q, NSA_WIDTH)
    for c in range(NSA_WIDTH // LANES):
        r = _rope128(qp[:, c * LANES:(c + 1) * LANES], cos, sa, sb) * (DH ** -0.5)
        q_ref[0, 2 * c] = r[:, :DH].astype(bf16)
        q_ref[0, 2 * c + 1] = r[:, DH:].astype(bf16)

    kvc_ref[...] = proj(P_KVC, KV_ROW)

    for off, kv_ref, k_ref, v_ref in ((P_KVS, kvs_ref, ks_ref, vs_ref), (P_KVW, kvw_ref, kw_ref, vw_ref)):
        p = proj(off, KV_ROW)
        k = _rope128(p[:, :LANES], cos, sa, sb)
        v = p[:, LANES:]
        kv_ref[:, :LANES] = k
        kv_ref[:, LANES:] = v
        for h in range(KVH):
            k_ref[0, h] = k[:, h * DH:(h + 1) * DH].astype(bf16)
            v_ref[0, h] = v[:, h * DH:(h + 1) * DH].astype(bf16)

    zn_ref[...] = _silu(proj(P_ZN, NSA_WIDTH))
    gq_ref[...] = proj(P_GQ, H_GLA * DK) * (DK ** -0.5)
    gk_ref[...] = proj(P_GK, H_GLA * DK)
    gv_ref[...] = proj(P_GV, H_GLA * DV)
    zg_ref[...] = _silu(proj(P_ZG, GLA_WIDTH))

    misc = proj(P_MISC, LANES)
    gate_ref[...] = _sigmoid(misc)
    xa = _dot(misc.astype(bf16), wa2_ref[...]) + ba_ref[...]
    la_ref[...] = _log_sigmoid(xa) / GLA_TAU


def _pack_w_in(w_in):
    o = IN_OFFSETS
    seg = lambda i: w_in[:, o[i]:o[i + 1]]
    misc = jnp.zeros((D_MODEL, LANES), w_in.dtype)
    misc = misc.at[:, MISC_GATE:MISC_GATE + 3 * H_NSA].set(seg(4))
    misc = misc.at[:, MISC_GLR:MISC_GLR + GLA_LR].set(seg(9))
    cols = [seg(0), seg(1), seg(2), seg(3), seg(5), seg(6), seg(7), seg(8), seg(10), misc]
    return jnp.concatenate(cols, axis=1).astype(bf16)


def _pad_wa2(wa2):
    pad = jnp.zeros((LANES, H_GLA * DK), wa2.dtype).at[MISC_GLR:MISC_GLR + GLA_LR].set(wa2)
    return pad.astype(bf16)


def _inproj(x, pos, norm_g, w_pack, wa2_pad, ba, tm):
    bk, lk, _ = x.shape
    n = bk * lk
    nt = lk // tm
    cos, sa, sb = _rope_tables(pos)
    tok = lambda w: pl.BlockSpec((tm, w), lambda i: (i, 0))
    tab = pl.BlockSpec((tm, LANES), lambda i: (i % nt, 0))
    full = lambda a: pl.BlockSpec(a.shape, lambda i: (0,) * a.ndim)
    hm = lambda h: pl.BlockSpec((1, h, tm, DH), lambda i: (i // nt, 0, i % nt, 0))
    g2 = norm_g.reshape(1, D_MODEL)
    ba2 = ba.reshape(1, H_GLA * DK)
    out_shape = (
        jax.ShapeDtypeStruct((bk, H_NSA, lk, DH), bf16),
        jax.ShapeDtypeStruct((n, KV_ROW), f32),
        jax.ShapeDtypeStruct((n, KV_ROW), f32),
        jax.ShapeDtypeStruct((n, KV_ROW), f32),
        jax.ShapeDtypeStruct((bk, KVH, lk, DH), bf16),
        jax.ShapeDtypeStruct((bk, KVH, lk, DH), bf16),
        jax.ShapeDtypeStruct((bk, KVH, lk, DH), bf16),
        jax.ShapeDtypeStruct((bk, KVH, lk, DH), bf16),
        jax.ShapeDtypeStruct((n, LANES), f32),
        jax.ShapeDtypeStruct((n, NSA_WIDTH), f32),
        jax.ShapeDtypeStruct((n, H_GLA * DK), f32),
        jax.ShapeDtypeStruct((n, H_GLA * DK), f32),
        jax.ShapeDtypeStruct((n, H_GLA * DV), f32),
        jax.ShapeDtypeStruct((n, H_GLA * DK), f32),
        jax.ShapeDtypeStruct((n, GLA_WIDTH), f32),
    )
    out_specs = (hm(H_NSA), tok(KV_ROW), tok(KV_ROW), tok(KV_ROW), hm(KVH), hm(KVH), hm(KVH), hm(KVH),
                 tok(LANES), tok(NSA_WIDTH), tok(H_GLA * DK), tok(H_GLA * DK), tok(H_GLA * DV),
                 tok(H_GLA * DK), tok(GLA_WIDTH))
    return pl.pallas_call(
        _inproj_kernel,
        grid=(n // tm,),
        in_specs=[tok(D_MODEL), full(g2), full(w_pack), tab, tab, tab, full(wa2_pad), full(ba2)],
        out_specs=out_specs,
        out_shape=out_shape,
        compiler_params=_cparams(("parallel",)),
        name="inproj",
    )(x.reshape(n, D_MODEL), g2, w_pack, cos, sa, sb, wa2_pad, ba2)


PT_Q, PT_KVC, PT_KVS, PT_KVW, PT_GATE = 0, 512, 768, 1024, 1280
PT_ROWS = PT_GATE + 32
PK_ZN, PK_GQ, PK_GK, PK_GV, PK_ZG, PK_MISC = 0, 512, 768, 1024, 1536, 2048
PK_COLS = PK_MISC + LANES


def _inproj_prompt_kernel(x_ref, g_ref, wt_ref, w_ref, cos_ref, sin_ref, wa2_ref, ba_ref,
                          qt_ref, kvct_ref, kvst_ref, kvwt_ref, ks_ref, kw_ref, vst_ref, vwt_ref, kvc_ref,
                          gatet_ref, zn_ref, gq_ref, gk_ref, gv_ref, la_ref, zg_ref):
    x = x_ref[...]
    ms = jnp.mean(x * x, axis=-1, keepdims=True)
    hn = (x * lax.rsqrt(ms + EPS) * g_ref[...]).astype(bf16)
    cos_t, sin_t = cos_ref[...], sin_ref[...]

    def proj_t(off, rows):
        return _dot_nt(wt_ref[off:off + rows, :], hn)

    def rope_rows(v):
        x1, x2 = v[:ROT_HALF], v[ROT_HALF:2 * ROT_HALF]
        return jnp.concatenate([x1 * cos_t - x2 * sin_t, x2 * cos_t + x1 * sin_t, v[2 * ROT_HALF:]], axis=0)

    q_t = proj_t(PT_Q, NSA_WIDTH)
    for h in range(H_NSA):
        qt_ref[0, h] = (rope_rows(q_t[h * DH:(h + 1) * DH]) * (DH ** -0.5)).astype(bf16)

    kvc_t = proj_t(PT_KVC, KV_ROW)
    kvct_ref[0] = kvc_t
    kvc_ref[...] = kvc_t.T.astype(bf16)

    for off, kvt_ref, k_ref, vt_ref in ((PT_KVS, kvst_ref, ks_ref, vst_ref), (PT_KVW, kvwt_ref, kw_ref, vwt_ref)):
        t = proj_t(off, KV_ROW)
        k_t = jnp.concatenate([rope_rows(t[h * DH:(h + 1) * DH]) for h in range(KVH)], axis=0)
        kvt_ref[0, :LANES] = k_t
        kvt_ref[0, LANES:] = t[LANES:]
        k_tok = k_t.T
        for h in range(KVH):
            k_ref[0, h] = k_tok[:, h * DH:(h + 1) * DH].astype(bf16)
            vt_ref[0, h] = t[LANES + h * DH:LANES + (h + 1) * DH].astype(bf16)

    gatet_ref[0] = _sigmoid(proj_t(PT_GATE, PT_ROWS - PT_GATE))

    def proj(off, width):
        return _dot(hn, w_ref[:, off:off + width])

    zn_ref[...] = _silu(proj(PK_ZN, NSA_WIDTH))
    gq_ref[...] = proj(PK_GQ, H_GLA * DK) * (DK ** -0.5)
    gk_ref[...] = proj(PK_GK, H_GLA * DK)
    gv_ref[...] = proj(PK_GV, H_GLA * DV)
    zg_ref[...] = _silu(proj(PK_ZG, GLA_WIDTH))
    misc = proj(PK_MISC, LANES)
    xa = _dot(misc.astype(bf16), wa2_ref[...]) + ba_ref[...]
    la_ref[...] = _log_sigmoid(xa) / GLA_TAU


def _pack_w_in_prompt(w_in):
    o = IN_OFFSETS
    seg = lambda i: w_in[:, o[i]:o[i + 1]]
    gate_t = jnp.zeros((PT_ROWS - PT_GATE, D_MODEL), w_in.dtype).at[:3 * H_NSA].set(seg(4).T)
    w_t = jnp.concatenate([seg(0).T, seg(1).T, seg(2).T, seg(3).T, gate_t], axis=0).astype(bf16)
    misc = jnp.zeros((D_MODEL, LANES), w_in.dtype).at[:, MISC_GLR:MISC_GLR + GLA_LR].set(seg(9))
    w_tok = jnp.concatenate([seg(5), seg(6), seg(7), seg(8), seg(10), misc], axis=1).astype(bf16)
    return w_t, w_tok


def _inproj_prompt(x, norm_g, w_t, w_tok, wa2_pad, ba, tm):
    b, lq, _ = x.shape
    n = b * lq
    nt = lq // tm
    inv = ROPE_THETA ** (-(jnp.arange(ROT_HALF, dtype=f32) / ROT_HALF))
    ang = inv[:, None] * jnp.arange(lq).astype(f32)[None, :]
    cos_t, sin_t = jnp.cos(ang), jnp.sin(ang)
    tok = lambda w: pl.BlockSpec((tm, w), lambda i: (i, 0))
    tab = pl.BlockSpec((ROT_HALF, tm), lambda i: (0, i % nt))
    full = lambda a: pl.BlockSpec(a.shape, lambda i: (0,) * a.ndim)
    feat = lambda r: pl.BlockSpec((1, r, tm), lambda i: (i // nt, 0, i % nt))
    headf = lambda h: pl.BlockSpec((1, h, DH, tm), lambda i: (i // nt, 0, 0, i % nt))
    headt = lambda h: pl.BlockSpec((1, h, tm, DH), lambda i: (i // nt, 0, i % nt, 0))
    g2 = norm_g.reshape(1, D_MODEL)
    ba2 = ba.reshape(1, H_GLA * DK)
    sds = jax.ShapeDtypeStruct
    out_shape = (
        sds((b, H_NSA, DH, lq), bf16),
        sds((b, KV_ROW, lq), f32), sds((b, KV_ROW, lq), f32), sds((b, KV_ROW, lq), f32),
        sds((b, KVH, lq, DH), bf16), sds((b, KVH, lq, DH), bf16),
        sds((b, KVH, DH, lq), bf16), sds((b, KVH, DH, lq), bf16),
        sds((n, KV_ROW), bf16),
        sds((b, PT_ROWS - PT_GATE, lq), f32),
        sds((n, NSA_WIDTH), f32), sds((n, H_GLA * DK), f32), sds((n, H_GLA * DK), f32),
        sds((n, H_GLA * DV), f32), sds((n, H_GLA * DK), f32), sds((n, GLA_WIDTH), f32),
    )
    out_specs = (headf(H_NSA), feat(KV_ROW), feat(KV_ROW), feat(KV_ROW), headt(KVH), headt(KVH),
                 headf(KVH), headf(KVH), tok(KV_ROW), feat(PT_ROWS - PT_GATE),
                 tok(NSA_WIDTH), tok(H_GLA * DK), tok(H_GLA * DK), tok(H_GLA * DV), tok(H_GLA * DK),
                 tok(GLA_WIDTH))
    return pl.pallas_call(
        _inproj_prompt_kernel,
        grid=(n // tm,),
        in_specs=[tok(D_MODEL), full(g2), full(w_t), full(w_tok), tab, tab, full(wa2_pad), full(ba2)],
        out_specs=out_specs,
        out_shape=out_shape,
        compiler_params=_cparams(("parallel",)),
        name="inproj_prompt",
    )(x.reshape(n, D_MODEL), g2, w_t, w_tok, cos_t, sin_t, wa2_pad, ba2)


CHUNK_W = CMP_STRIDE * KV_ROW
PART_W = 2 * 2 * KVH * CMP_HID
HID_W = 2 * KVH * CMP_HID


def _cmp_w1_big(cmp_w1):
    w1r = cmp_w1.reshape(2, CMP_LEN // CMP_STRIDE, CMP_STRIDE, DH, CMP_HID)
    eye = jnp.eye(2, dtype=cmp_w1.dtype)
    big = jnp.einsum('crsdh,cC,kK->sCKdrckh', w1r, eye, eye)
    return big.reshape(CHUNK_W, PART_W).astype(bf16)


def _cmp_w2_big(cmp_w2):
    eye = jnp.eye(2, dtype=cmp_w2.dtype)
    big = jnp.einsum('chd,cC,kK->ckhCKd', cmp_w2, eye, eye)
    return big.reshape(HID_W, KV_ROW).astype(bf16)


def _cmp_part_kernel(x_ref, w_ref, o_ref):
    o_ref[...] = _dot(x_ref[...].astype(bf16), w_ref[...])


def _cmp_part(chunks, w1_big, tr):
    r = chunks.shape[0]
    return pl.pallas_call(
        _cmp_part_kernel,
        grid=(r // tr,),
        in_specs=[pl.BlockSpec((tr, CHUNK_W), lambda i: (i, 0)),
                  pl.BlockSpec((CHUNK_W, PART_W), lambda i: (0, 0))],
        out_specs=pl.BlockSpec((tr, PART_W), lambda i: (i, 0)),
        out_shape=jax.ShapeDtypeStruct((r, PART_W), f32),
        compiler_params=_cparams(("parallel",)),
        name="cmp_part",
    )(chunks, w1_big)


def _cmp_bias_kernel(pe_ref, w1_ref, b1_ref, o_ref):
    for c in range(2):
        o_ref[c] = _dot(pe_ref[c], w1_ref[c]) + b1_ref[c]


def _cmp_bias(cmp_pe, cmp_w1, cmp_b1):
    pe = jnp.broadcast_to(cmp_pe.reshape(2, 1, CMP_LEN * DH), (2, 8, CMP_LEN * DH))
    b1 = jnp.broadcast_to(cmp_b1.reshape(2, 1, CMP_HID), (2, 8, CMP_HID))
    out = pl.pallas_call(
        _cmp_bias_kernel,
        out_shape=jax.ShapeDtypeStruct((2, 8, CMP_HID), f32),
        name="cmp_bias",
    )(pe, cmp_w1, b1)
    row = out[:, 0, :]
    return jnp.broadcast_to(row[:, None, :], (2, KVH, CMP_HID)).reshape(1, HID_W)


def _compress_finish(part, bias, w2_big, cos, sa, sb):
    n = part.shape[0]
    pre = part[:, :HID_W] + pltpu.roll(part[:, HID_W:], n - 1, 0)
    h = _silu(pre + bias)
    out = _dot(h.astype(bf16), w2_big)
    return _rope128(out[:, :LANES], cos, sa, sb), out[:, LANES:]


def _masked_softmax(s, mask, axis):
    s = jnp.where(mask, s, NEG)
    e = jnp.exp(s - jnp.max(s, axis=axis, keepdims=True))
    return jnp.where(mask, e / jnp.sum(e, axis=axis, keepdims=True), 0.0)


def _split_bf16(x):
    hi = x.astype(bf16)
    return hi, (x - hi.astype(f32)).astype(bf16)


def _topk_mask(score, k, axis):
    n = score.shape[axis]
    idx = lax.broadcasted_iota(jnp.int32, score.shape, axis)
    sel = jnp.zeros(score.shape, f32)
    for _ in range(k):
        m = jnp.max(score, axis=axis, keepdims=True)
        first = jnp.min(jnp.where(score == m, idx, n), axis=axis, keepdims=True)
        pick = idx == first
        sel = jnp.where(pick, 1.0, sel)
        score = jnp.where(pick, NEG, score)
    return sel


def _topk_rows_by_rank(score, k):
    r, n = score.shape
    cols = jnp.concatenate([score, jnp.zeros((n - r, n), f32)], axis=0).T
    earlier = jnp.where(lax.broadcasted_iota(jnp.int32, (n, n), 0) < lax.broadcasted_iota(jnp.int32, (n, n), 1),
                        1.0, 0.0)
    rows = []
    for c in range(r):
        col, row = cols[:, c:c + 1], score[c:c + 1, :]
        before = jnp.where(col > row, 1.0, jnp.where(col == row, earlier, 0.0))
        rows.append(jnp.where(jnp.sum(before, axis=0, keepdims=True) < k, 1.0, 0.0))
    return jnp.concatenate(rows, axis=0)


def _cover_t(n_slc, n_chunk):
    start = np.arange(n_chunk)[None, :] * CMP_STRIDE
    j = np.arange(n_slc)[:, None]
    cov = (start < (j + 1) * SLC_LEN) & (start + CMP_LEN > j * SLC_LEN)
    return jnp.asarray(cov.astype(np.float32), dtype=bf16)


def _cmp_attn_kernel(part_ref, bias_ref, w2_ref, cos_ref, sa_ref, sb_ref, qt_ref, gatet_ref, cov_ref,
                     ot_ref, selt_ref, kc_ref, vct_ref, *, tq):
    t = pl.program_id(1)

    @pl.when(t == 0)
    def _():
        k, v = _compress_finish(part_ref[0], bias_ref[...], w2_ref[...], cos_ref[...], sa_ref[...], sb_ref[...])
        v_t = v.T
        for h in range(KVH):
            kc_ref[h] = k[:, h * DH:(h + 1) * DH].astype(bf16)
            vct_ref[h] = v_t[h * DH:(h + 1) * DH].astype(bf16)

    nc = kc_ref.shape[1]
    nb = cov_ref.shape[0]
    qpos = t * tq + lax.broadcasted_iota(jnp.int32, (1, tq), 1)
    end_pos = lax.broadcasted_iota(jnp.int32, (nc, 1), 0) * CMP_STRIDE + (CMP_LEN - 1)
    mask_t = end_pos <= qpos
    gate_t = gatet_ref[0]
    cov = cov_ref[...]

    jblk = lax.broadcasted_iota(jnp.int32, (nb, tq), 0)
    qblk = qpos // SLC_LEN
    valid = jblk <= qblk
    forced = (jblk == 0) | (jblk == qblk) | (jblk == qblk - 1)

    for kh in range(KVH):
        kc, vc_t = kc_ref[kh], vct_ref[kh]
        psum = jnp.zeros((nc, tq), f32)
        for g in range(G_NSA):
            h = kh * G_NSA + g
            p_t = _masked_softmax(_dot(kc, qt_ref[0, h]), mask_t, axis=0)
            ot_ref[0, h * DH:(h + 1) * DH] = _dot(vc_t, p_t.astype(bf16)) * gate_t[h:h + 1]
            psum = psum + p_t
        hi, lo = _split_bf16(psum)
        imp = _dot(cov, hi) + _dot(cov, lo)
        score = jnp.where(valid, jnp.where(forced, FORCE_SCORE, imp), -1.0)
        sel_t = jnp.where(valid, _topk_mask(score, SLC_TOP, axis=0), 0.0)
        if nb < LANES:
            sel_t = jnp.concatenate([sel_t, jnp.zeros((LANES - nb, tq), f32)], axis=0)
        selt_ref[0, kh] = sel_t.astype(bf16)


def _cmp_attn(part, bias, w2_big, q_t, gate_t, tq):
    b, nc, _ = part.shape
    lq = q_t.shape[3]
    n_slc = lq // SLC_LEN
    end_pos = jnp.arange(nc) * CMP_STRIDE + (CMP_LEN - 1)
    cos, sa, sb = _rope_tables(end_pos)
    cov = _cover_t(n_slc, nc)
    full = lambda a: pl.BlockSpec(a.shape, lambda i, t: (0,) * a.ndim)
    return pl.pallas_call(
        functools.partial(_cmp_attn_kernel, tq=tq),
        grid=(b, lq // tq),
        in_specs=[pl.BlockSpec((1, nc, PART_W), lambda i, t: (i, 0, 0)),
                  full(bias), full(w2_big), full(cos), full(sa), full(sb),
                  pl.BlockSpec((1, H_NSA, DH, tq), lambda i, t: (i, 0, 0, t)),
                  pl.BlockSpec((1, gate_t.shape[1], tq), lambda i, t: (i, 0, t)),
                  full(cov)],
        out_specs=(pl.BlockSpec((1, NSA_WIDTH, tq), lambda i, t: (i, 0, t)),
                   pl.BlockSpec((1, KVH, LANES, tq), lambda i, t: (i, 0, 0, t))),
        out_shape=(jax.ShapeDtypeStruct((b, NSA_WIDTH, lq), f32),
                   jax.ShapeDtypeStruct((b, KVH, LANES, lq), bf16)),
        scratch_shapes=[pltpu.VMEM((KVH, nc, DH), bf16), pltpu.VMEM((KVH, DH, nc), bf16)],
        compiler_params=_cparams(("parallel", "arbitrary")),
        name="cmp_attn",
    )(part, bias, w2_big, cos, sa, sb, q_t, gate_t, cov)


def _block_expand(n_slc_pad, n_keys):
    e = (np.arange(n_keys)[None, :] // SLC_LEN) == np.arange(n_slc_pad)[:, None]
    return jnp.asarray(e.astype(np.float32), dtype=bf16)


def _mask_bias_t(allowed):
    bias = jnp.where(allowed, 0.0, NEG)
    return jnp.concatenate([bias] * G_NSA, axis=1)


def _sublane_fold(x, op):
    return functools.reduce(op, [x[r:r + 8] for r in range(0, x.shape[0], 8)])


def _slc_win_kernel(qt_ref, ks_ref, vst_ref, kw_ref, vwt_ref, selt_ref, et_ref, gatet_ref, ocmpt_ref,
                    o_ref, s_ref, m_ref, l_ref, acc_ref, *, tq, tk):
    kh = pl.program_id(1)
    t = pl.program_id(2)
    q0 = t * tq
    qpos = q0 + lax.broadcasted_iota(jnp.int32, (1, tq), 1)
    sel_t = selt_ref[0, 0]
    q_t = jnp.concatenate([qt_ref[0, g] for g in range(G_NSA)], axis=1)

    n_kt = (q0 + tq + tk - 1) // tk
    m_ref[...] = jnp.full(m_ref.shape, NEG, f32)

    def score_body(kt, carry):
        k0 = pl.multiple_of(kt * tk, tk)
        kpos = k0 + lax.broadcasted_iota(jnp.int32, (tk, 1), 0)
        picked = _dot(et_ref[pl.ds(k0, tk), :], sel_t)
        s = _dot(ks_ref[0, 0, pl.ds(k0, tk), :], q_t) + _mask_bias_t((picked > 0.5) & (kpos <= qpos))
        s_ref[pl.ds(k0, tk), :] = s
        m_ref[...] = jnp.maximum(m_ref[...], _sublane_fold(s, jnp.maximum))
        return carry

    lax.fori_loop(0, n_kt, score_body, 0)
    m = jnp.max(m_ref[...], axis=0, keepdims=True)
    l_ref[...] = jnp.zeros(l_ref.shape, f32)
    acc_ref[...] = jnp.zeros(acc_ref.shape, f32)

    def pv_body(kt, carry):
        k0 = pl.multiple_of(kt * tk, tk)
        p = jnp.exp(s_ref[pl.ds(k0, tk), :] - m)
        l_ref[...] += _sublane_fold(p, jnp.add)
        acc_ref[...] += _dot(vst_ref[0, 0, :, pl.ds(k0, tk)], p.astype(bf16))
        return carry

    lax.fori_loop(0, n_kt, pv_body, 0)
    o_slc = acc_ref[...] / jnp.sum(l_ref[...], axis=0, keepdims=True)

    band = WINDOW + tq
    w0 = pl.multiple_of(jnp.maximum(q0 - WINDOW, 0), tq)
    kpos = w0 + lax.broadcasted_iota(jnp.int32, (band, 1), 0)
    s = _dot(kw_ref[0, 0, pl.ds(w0, band), :], q_t) + _mask_bias_t((kpos <= qpos) & (kpos > qpos - WINDOW))
    p = jnp.exp(s - jnp.max(s, axis=0, keepdims=True))
    o_win = _dot(vwt_ref[0, 0, :, pl.ds(w0, band)], p.astype(bf16)) / jnp.sum(p, axis=0, keepdims=True)

    gate_t = gatet_ref[0]
    row = lambda c, g: jnp.where(kh == 0, gate_t[c * H_NSA + g:c * H_NSA + g + 1],
                                 gate_t[c * H_NSA + G_NSA + g:c * H_NSA + G_NSA + g + 1])
    g_slc = jnp.concatenate([row(1, g) for g in range(G_NSA)], axis=1)
    g_win = jnp.concatenate([row(2, g) for g in range(G_NSA)], axis=1)
    o_t = g_slc * o_slc + g_win * o_win
    o_tok = jnp.concatenate([o_t, jnp.zeros((LANES - DH, o_t.shape[1]), f32)], axis=0).T
    ocmp_tok = ocmpt_ref[0].T
    o_ref[0] = ocmp_tok + jnp.concatenate([o_tok[g * tq:(g + 1) * tq, :DH] for g in range(G_NSA)], axis=1)


def _slc_win(q_t, ks, vs_t, kw, vw_t, sel_t, gate_t, o_cmp_t, tq, tk):
    b, _, _, lq = q_t.shape
    assert lq >= WINDOW + tq and lq % tk == 0
    e_t = _block_expand(LANES, lq).T
    gw = G_NSA * DH
    k_spec = pl.BlockSpec((1, 1, lq, DH), lambda i, k, t: (i, k, 0, 0))
    vt_spec = pl.BlockSpec((1, 1, DH, lq), lambda i, k, t: (i, k, 0, 0))
    lanes = G_NSA * tq
    return pl.pallas_call(
        functools.partial(_slc_win_kernel, tq=tq, tk=tk),
        grid=(b, KVH, lq // tq),
        in_specs=[pl.BlockSpec((1, G_NSA, DH, tq), lambda i, k, t: (i, k, 0, t)),
                  k_spec, vt_spec, k_spec, vt_spec,
                  pl.BlockSpec((1, 1, LANES, tq), lambda i, k, t: (i, k, 0, t)),
                  pl.BlockSpec(e_t.shape, lambda i, k, t: (0, 0)),
                  pl.BlockSpec((1, gate_t.shape[1], tq), lambda i, k, t: (i, 0, t)),
                  pl.BlockSpec((1, gw, tq), lambda i, k, t: (i, k, t))],
        out_specs=pl.BlockSpec((1, tq, gw), lambda i, k, t: (i, t, k)),
        out_shape=jax.ShapeDtypeStruct((b, lq, NSA_WIDTH), f32),
        scratch_shapes=[pltpu.VMEM((lq, lanes), f32),
                        pltpu.VMEM((8, lanes), f32),
                        pltpu.VMEM((8, lanes), f32),
                        pltpu.VMEM((DH, lanes), f32)],
        compiler_params=_cparams(("parallel", "parallel", "parallel")),
        name="slc_win",
    )(q_t, ks, vs_t, kw, vw_t, sel_t, e_t, gate_t, o_cmp_t)


def _rms_gain(o, gain):
    return o * lax.rsqrt(jnp.mean(o * o, axis=-1, keepdims=True) + EPS) * gain


def _gla_kernel(q_ref, k_ref, v_ref, la_ref, z_ref, gain_ref, y_ref, st_ref, s_scr, *, tl):
    t = pl.program_id(2)

    @pl.when(t == 0)
    def _():
        s_scr[...] = jnp.zeros(s_scr.shape, f32)

    c = GLA_CHUNK
    row = lax.broadcasted_iota(jnp.int32, (c, c), 0)
    col = lax.broadcasted_iota(jnp.int32, (c, c), 1)
    causal = col <= row
    tril = jnp.where(causal, 1.0, 0.0).astype(bf16)
    lane = lax.broadcasted_iota(jnp.int32, (1, LANES), 1)
    head_mask = [jnp.where(lane < DK, 1.0, 0.0), jnp.where(lane >= DK, 1.0, 0.0)]
    gain = gain_ref[...]

    chunks = []
    for ci in range(tl // c):
        rows = slice(ci * c, (ci + 1) * c)
        q, k, la = q_ref[rows, :], k_ref[rows, :], la_ref[rows, :]
        hi, rest = la.astype(bf16), la - la.astype(bf16).astype(f32)
        mid, lo = rest.astype(bf16), (rest - rest.astype(bf16).astype(f32)).astype(bf16)
        bc = _dot(tril, hi) + _dot(tril, mid) + _dot(tril, lo)
        ref_row = bc[c // 2 - 1:c // 2, :]
        b_last = bc[c - 1:c, :]
        q_a = q * jnp.exp(bc - ref_row)
        k_a = (k * jnp.exp(ref_row - bc)).astype(bf16)
        q_s = q * jnp.exp(bc)
        k_s = k * jnp.exp(b_last - bc)
        o_intra, q_inter, add = [], [], None
        for h in range(2):
            mh = head_mask[h]
            vh = v_ref[rows, h * DV:(h + 1) * DV].astype(bf16)
            a = jnp.where(causal, _dot_nt((q_a * mh).astype(bf16), k_a), 0.0)
            o_intra.append(_dot(a.astype(bf16), vh))
            q_inter.append((q_s * mh).astype(bf16))
            u = _dot_tn(vh, (k_s * mh).astype(bf16))
            add = u if add is None else add + u
        chunks.append((rows, jnp.exp(b_last), add, o_intra, q_inter))

    st = s_scr[...]
    for rows, decay, add, o_intra, q_inter in chunks:
        st_b = st.astype(bf16)
        for h in range(2):
            o = o_intra[h] + _dot_nt(q_inter[h], st_b)
            y_ref[rows, h * DV:(h + 1) * DV] = _rms_gain(o, gain) * z_ref[rows, h * DV:(h + 1) * DV]
        st = st * decay + add
    s_scr[...] = st
    st_ref[0, 0] = st


def _gla(gq, gk, gv, la, zg, gain, b, tl):
    n = gq.shape[0]
    nt = n // b // tl
    pairs = H_GLA // 2
    qk = pl.BlockSpec((tl, LANES), lambda i, p, t: (i * nt + t, p))
    vz = pl.BlockSpec((tl, 2 * DV), lambda i, p, t: (i * nt + t, p))
    gain2 = gain.reshape(1, DV)
    y, st = pl.pallas_call(
        functools.partial(_gla_kernel, tl=tl),
        grid=(b, pairs, nt),
        in_specs=[qk, qk, vz, qk, vz, pl.BlockSpec((1, DV), lambda i, p, t: (0, 0))],
        out_specs=(vz, pl.BlockSpec((1, 1, DV, LANES), lambda i, p, t: (i, p, 0, 0))),
        out_shape=(jax.ShapeDtypeStruct((n, GLA_WIDTH), f32),
                   jax.ShapeDtypeStruct((b, pairs, DV, LANES), f32)),
        scratch_shapes=[pltpu.VMEM((DV, LANES), f32)],
        compiler_params=_cparams(("parallel", "parallel", "arbitrary")),
        name="gla",
    )(gq, gk, gv, la, zg, gain2)
    state = st.reshape(b, pairs, DV, 2, DK).transpose(0, 1, 3, 4, 2).reshape(b, H_GLA, DK, DV)
    return y, state


def _outproj_kernel(x_ref, on_ref, zn_ref, yg_ref, w_ref, g_ref, y_ref):
    y_nsa = (on_ref[...] * zn_ref[...]).astype(bf16)
    mix = _dot(y_nsa, w_ref[:NSA_WIDTH, :]) + _dot(yg_ref[...].astype(bf16), w_ref[NSA_WIDTH:, :])
    y_ref[...] = _rms_gain(x_ref[...] + mix, g_ref[...])


def _outproj(x, o_nsa, zn, y_gla, w_out, gain, tm):
    n = x.shape[0]
    tok = lambda w: pl.BlockSpec((tm, w), lambda i: (i, 0))
    g2 = gain.reshape(1, D_MODEL)
    return pl.pallas_call(
        _outproj_kernel,
        grid=(n // tm,),
        in_specs=[tok(D_MODEL), tok(NSA_WIDTH), tok(NSA_WIDTH), tok(GLA_WIDTH),
                  pl.BlockSpec(w_out.shape, lambda i: (0, 0)), pl.BlockSpec(g2.shape, lambda i: (0, 0))],
        out_specs=tok(D_MODEL),
        out_shape=jax.ShapeDtypeStruct((n, D_MODEL), f32),
        compiler_params=_cparams(("parallel",)),
        name="outproj",
    )(x, o_nsa, zn, y_gla, w_out, g2)


def _prompt_path(x, wts, w_t, w_tok):
    (norm_g, _, wa2_pad, ba, w1_big, cmp_bias, w2_big, gla_gain, w_out, out_gain) = wts
    b, lq, _ = x.shape
    (q_t, kvc_t, kvs_t, kvw_t, ks, kw, vs_t, vw_t, kvc_tok, gate_t, zn, gq, gk, gv, la, zg) = _inproj_prompt(
        x, norm_g, w_t, w_tok, wa2_pad, ba, tm=512)
    n_chunk = lq // CMP_STRIDE
    part = _cmp_part(kvc_tok.reshape(b * n_chunk, CHUNK_W), w1_big, tr=n_chunk)
    o_cmp_t, sel_t = _cmp_attn(part.reshape(b, n_chunk, PART_W), cmp_bias, w2_big, q_t, gate_t, tq=256)
    o_nsa = _slc_win(q_t, ks, vs_t, kw, vw_t, sel_t, gate_t, o_cmp_t, tq=256, tk=512)
    y_gla, state = _gla(gq, gk, gv, la, zg, gla_gain, b, tl=512)
    y = _outproj(x.reshape(b * lq, D_MODEL), o_nsa.reshape(b * lq, NSA_WIDTH), zn, y_gla, w_out, out_gain, tm=512)
    kv5 = lambda a: a.reshape(b, 2, KVH, DH, -1).transpose(0, 4, 1, 2, 3)
    wlen = min(WINDOW, lq)
    return (y.reshape(b, lq, D_MODEL), kv5(kvc_t), kv5(kvs_t), kv5(kvw_t[:, :, lq - wlen:]), state)


def _pages_native(cache):
    n_pool = cache.shape[0]
    return cache.transpose(0, 2, 3, 4, 1).reshape(n_pool, 2, KVH * DH, cache.shape[1])


def _page_gather(make_copies):
    i = pl.program_id(0)
    slot = i % 2

    @pl.when(i == 0)
    def _():
        for c in make_copies(0, 0):
            c.start()

    @pl.when(i + 1 < pl.num_programs(0))
    def _():
        for c in make_copies(i + 1, 1 - slot):
            c.start()

    for c in make_copies(i, slot):
        c.wait()
    return slot


def _cmp_w1_pairs(cmp_w1):
    ratio = CMP_LEN // CMP_STRIDE
    w1r = cmp_w1.reshape(2, ratio, CMP_STRIDE // 2, 2, DH, CMP_HID)
    eye = jnp.eye(KVH, dtype=cmp_w1.dtype)
    big = jnp.einsum('crpjdh,kK->cpjKdrkh', w1r, eye)
    return big.reshape(2, CMP_STRIDE // 2, 2 * KVH * DH, ratio * KVH * CMP_HID).astype(bf16)


def _cmp_w2_heads(cmp_w2):
    eye = jnp.eye(KVH, dtype=cmp_w2.dtype)
    return jnp.einsum('chd,kK->ckhKd', cmp_w2, eye).reshape(2, KVH * CMP_HID, KVH * DH).astype(bf16)


def _group_sum_matrix(n_tok):
    r = np.arange(KVH * n_tok)[:, None]
    c = np.arange(H_NSA * n_tok)[None, :]
    m = ((c // (G_NSA * n_tok)) == (r // n_tok)) & ((c % n_tok) == (r % n_tok))
    return m.astype(np.float32)


def _cmp_attn_sample_kernel(pt_ref, cache_ref, perm_ref, w1_ref, bias_ref, w2_ref, cos_ref, sa_ref, sb_ref, q_ref,
                            gate_ref, cov_ref, gs_ref, gst_ref, o_ref, sel_ref, buf, rows_scr, sem,
                            *, n_pages, n_tok, past_len):
    def copies(seq, slot):
        return [pltpu.make_async_copy(cache_ref.at[pt_ref[seq, p]], buf.at[slot, p], sem.at[slot])
                for p in range(n_pages)]

    slot = _page_gather(copies)

    cpp = PAGE_SIZE // CMP_STRIDE
    perm = perm_ref[...]

    group = 4

    def to_rows(g, carry):
        for u in range(group):
            p = g * group + u
            r0 = pl.multiple_of(p * cpp, cpp)
            for c in range(2):
                t = _dot_nt(perm, buf[slot, p, c].astype(bf16))
                for s in range(CMP_STRIDE):
                    rows_scr[c, s, pl.ds(r0, cpp), :] = t[s * cpp:(s + 1) * cpp]
        return carry

    lax.fori_loop(0, n_pages // group, to_rows, 0)

    nc = n_pages * cpp
    kv = []
    for c in range(2):
        part = jnp.zeros((nc, w1_ref.shape[-1]), f32)
        for pr in range(CMP_STRIDE // 2):
            lhs = jnp.concatenate([rows_scr[c, 2 * pr], rows_scr[c, 2 * pr + 1]], axis=-1)
            part = part + _dot(lhs.astype(bf16), w1_ref[c, pr])
        half = part.shape[1] // 2
        pre = part[:, :half] + pltpu.roll(part[:, half:], nc - 1, 0)
        hid = _silu(pre + bias_ref[:, c * half:(c + 1) * half])
        kv.append(_dot(hid.astype(bf16), w2_ref[c]))
    k = _rope128(kv[0], cos_ref[...], sa_ref[...], sb_ref[...])
    v = kv[1]
    rows = H_NSA * n_tok
    qpos = past_len + lax.broadcasted_iota(jnp.int32, (rows, 1), 0) % n_tok
    end_c = lax.broadcasted_iota(jnp.int32, (1, nc), 1) * CMP_STRIDE + (CMP_LEN - 1)
    mask = end_c <= qpos
    p = _masked_softmax(_dot_nt(q_ref[0], k.astype(bf16)), mask, axis=-1)
    o_ref[0] = _dot(p.astype(bf16), v.astype(bf16)) * gate_ref[0][:, 0:1]
    hi, lo = _split_bf16(p)
    gs = gs_ref[...]
    ph, pl_ = _split_bf16(_dot(gs, hi) + _dot(gs, lo))
    imp = _dot(ph, cov_ref[...]) + _dot(pl_, cov_ref[...])
    n_blk = imp.shape[1]
    jblk = lax.broadcasted_iota(jnp.int32, imp.shape, 1)
    forced = (jblk == 0) | (jblk == n_blk - 1)
    sel = _topk_rows_by_rank(jnp.where(forced, FORCE_SCORE, imp), SLC_TOP - 1)
    sel_ref[0] = _dot(gst_ref[...], sel.astype(bf16)).astype(bf16)


def _cmp_attn_sample(cache_t, page_table, w1_pairs, bias, w2_heads, q_blk, gate_rows, n_tok):
    b, n_pages = page_table.shape
    past_len = n_pages * PAGE_SIZE
    nc = past_len // CMP_STRIDE
    rows = H_NSA * n_tok
    n_blk = past_len // SLC_LEN
    end_pos = jnp.arange(nc) * CMP_STRIDE + (CMP_LEN - 1)
    cos, sa, sb = _rope_tables(end_pos)
    cov = _cover_t(n_blk, nc).T
    gs = jnp.asarray(_group_sum_matrix(n_tok), dtype=bf16)
    gst = gs.T
    r = np.arange(PAGE_SIZE)
    cpp = PAGE_SIZE // CMP_STRIDE
    perm = jnp.asarray((r[None, :] == (r[:, None] % cpp) * CMP_STRIDE + r[:, None] // cpp).astype(np.float32),
                       dtype=bf16)
    full = lambda a: pl.BlockSpec(a.shape, lambda i, pt: (0,) * a.ndim)
    seq = lambda a: pl.BlockSpec((1,) + a.shape[1:], lambda i, pt: (i,) + (0,) * (a.ndim - 1))
    grid_spec = pltpu.PrefetchScalarGridSpec(
        num_scalar_prefetch=1,
        grid=(b,),
        in_specs=[pl.BlockSpec(memory_space=pl.ANY), full(perm), full(w1_pairs), full(bias), full(w2_heads),
                  full(cos), full(sa), full(sb), seq(q_blk), seq(gate_rows), full(cov), full(gs), full(gst)],
        out_specs=(pl.BlockSpec((1, rows, LANES), lambda i, pt: (i, 0, 0)),
                   pl.BlockSpec((1, rows, n_blk), lambda i, pt: (i, 0, 0))),
        scratch_shapes=[pltpu.VMEM((2, n_pages) + cache_t.shape[1:], f32),
                        pltpu.VMEM((2, CMP_STRIDE, nc, KVH * DH), f32),
                        pltpu.SemaphoreType.DMA((2,))],
    )
    return pl.pallas_call(
        functools.partial(_cmp_attn_sample_kernel, n_pages=n_pages, n_tok=n_tok, past_len=past_len),
        grid_spec=grid_spec,
        out_shape=(jax.ShapeDtypeStruct((b, rows, LANES), f32),
                   jax.ShapeDtypeStruct((b, rows, n_blk), bf16)),
        compiler_params=_cparams(("arbitrary",)),
        name="cmp_attn_sample",
    )(page_table, cache_t, perm, w1_pairs, bias, w2_heads, cos, sa, sb, q_blk, gate_rows, cov, gs, gst)


def _slc_win_sample_kernel(pt_ref, cache_ref, q_ref, sel_ref, e_ref, snew_ref, cw_ref, wnew_ref, wnewt_ref,
                           gate_ref, ocmp_ref, o_ref, wout_ref, buf, sem, *, n_pages, n_tok, win_off):
    def copies(seq, slot):
        return [pltpu.make_async_copy(cache_ref.at[pt_ref[seq, p], c],
                                      buf.at[slot, c, :, pl.ds(p * PAGE_SIZE, PAGE_SIZE)], sem.at[slot])
                for p in range(n_pages) for c in range(2)]

    slot = _page_gather(copies)
    rows = H_NSA * n_tok
    q = q_ref[0]
    tok = lax.broadcasted_iota(jnp.int32, (rows, 1), 0) % n_tok
    new_i = lax.broadcasted_iota(jnp.int32, (1, snew_ref.shape[1]), 1)
    new_ok = (new_i <= tok) & (new_i < n_tok)

    def attend(keys_t, vals_t, allowed, new_ref):
        s = jnp.where(allowed, _dot(q, keys_t), NEG)
        k_new = new_ref[0][:, :LANES].astype(bf16)
        v_new = new_ref[0][:, LANES:].astype(bf16)
        s_new = jnp.where(new_ok, _dot_nt(q, k_new), NEG)
        m = jnp.maximum(jnp.max(s, axis=-1, keepdims=True), jnp.max(s_new, axis=-1, keepdims=True))
        p = jnp.where(allowed, jnp.exp(s - m), 0.0)
        p_new = jnp.where(new_ok, jnp.exp(s_new - m), 0.0)
        l = jnp.sum(p, axis=-1, keepdims=True) + jnp.sum(p_new, axis=-1, keepdims=True)
        return (_dot_nt(p.astype(bf16), vals_t) + _dot(p_new.astype(bf16), v_new)) / l

    picked = _dot(sel_ref[0], e_ref[...]) > 0.5
    o_slc = attend(buf[slot, 0].astype(bf16), buf[slot, 1].astype(bf16), picked, snew_ref)

    wbuf = cw_ref.shape[-1]
    win_i = lax.broadcasted_iota(jnp.int32, (1, wbuf), 1)
    o_win = attend(cw_ref[0, 0].astype(bf16), cw_ref[0, 1].astype(bf16), win_i > tok + win_off, wnew_ref)

    lane = lax.broadcasted_iota(jnp.int32, (1, LANES), 1)
    for c in range(2):
        shifted = pltpu.roll(cw_ref[0, c], wbuf - n_tok, 1)
        tail = jnp.where(lane >= LANES - n_tok, wnewt_ref[0, c], shifted[:, wbuf - LANES:])
        wout_ref[0, c] = jnp.concatenate([shifted[:, :wbuf - LANES], tail], axis=-1)

    gate = gate_ref[0]
    o = ocmp_ref[0] + gate[:, 1:2] * o_slc + gate[:, 2:3] * o_win
    second_kvh = lax.broadcasted_iota(jnp.int32, (rows, 1), 0) >= G_NSA * n_tok
    o_ref[0] = jnp.where(second_kvh, o[:, DH:], o[:, :DH])


def _slc_win_sample(cache_t, page_table, q_blk, sel_rows, kvs_new, win_t, kvw_new, kvw_new_t, gate_rows,
                    o_cmp, n_tok):
    b, n_pages = page_table.shape
    rows = H_NSA * n_tok
    past_len = n_pages * PAGE_SIZE
    wbuf = win_t.shape[-1]
    e = _block_expand(sel_rows.shape[-1], past_len)
    full = lambda a: pl.BlockSpec(a.shape, lambda i, pt: (0,) * a.ndim)
    seq = lambda a: pl.BlockSpec((1,) + a.shape[1:], lambda i, pt: (i,) + (0,) * (a.ndim - 1))
    grid_spec = pltpu.PrefetchScalarGridSpec(
        num_scalar_prefetch=1,
        grid=(b,),
        in_specs=[pl.BlockSpec(memory_space=pl.ANY), seq(q_blk), seq(sel_rows), full(e), seq(kvs_new),
                  seq(win_t), seq(kvw_new), seq(kvw_new_t), seq(gate_rows), seq(o_cmp)],
        out_specs=(pl.BlockSpec((1, rows, DH), lambda i, pt: (i, 0, 0)), seq(win_t)),
        scratch_shapes=[pltpu.VMEM((2, 2, KVH * DH, past_len), f32), pltpu.SemaphoreType.DMA((2,))],
    )
    return pl.pallas_call(
        functools.partial(_slc_win_sample_kernel, n_pages=n_pages, n_tok=n_tok, win_off=wbuf - WINDOW),
        grid_spec=grid_spec,
        out_shape=(jax.ShapeDtypeStruct((b, rows, DH), f32), jax.ShapeDtypeStruct(win_t.shape, f32)),
        compiler_params=_cparams(("arbitrary",)),
        name="slc_win_sample",
    )(page_table, cache_t, q_blk, sel_rows, e, kvs_new, win_t, kvw_new, kvw_new_t, gate_rows, o_cmp)


def _gla_sample_kernel(qka_ref, v_ref, z_ref, s_ref, gain_ref, y_ref, so_ref, *, n_tok):
    gain = gain_ref[...]
    y_ref[...] = jnp.zeros(y_ref.shape, f32)
    for h in range(H_GLA):
        qka = qka_ref[0, h]
        s = s_ref[0, h]
        for t in range(n_tok):
            q_t = qka[:, t:t + 1]
            k_t = qka[:, n_tok + t:n_tok + t + 1]
            a_t = jnp.exp(qka[:, 2 * n_tok + t:2 * n_tok + t + 1])
            s = a_t * s + k_t * v_ref[0, t:t + 1, h * DV:(h + 1) * DV]
            o = jnp.sum(q_t * s, axis=0, keepdims=True)
            y_ref[0, t:t + 1, h * DV:(h + 1) * DV] = _rms_gain(o, gain) * z_ref[0, t:t + 1, h * DV:(h + 1) * DV]
        so_ref[0, h] = s


def _gla_sample(gq, gk, la, gv, zg, state, gain, b, n_tok):
    def cols(a):
        return a.reshape(b, n_tok, H_GLA, DK).transpose(0, 2, 3, 1)
    qka = jnp.concatenate([cols(gq), cols(gk), cols(la)], axis=-1)
    qka = jnp.pad(qka, ((0, 0), (0, 0), (0, 0), (0, LANES - 3 * n_tok)))
    pad_rows = lambda a: jnp.pad(a.reshape(b, n_tok, -1), ((0, 0), (0, 8 - n_tok), (0, 0)))
    v8, z8 = pad_rows(gv), pad_rows(zg)
    gain2 = gain.reshape(1, DV)
    seq = lambda a: pl.BlockSpec((1,) + a.shape[1:], lambda i: (i,) + (0,) * (a.ndim - 1))
    y, s_new = pl.pallas_call(
        functools.partial(_gla_sample_kernel, n_tok=n_tok),
        grid=(b,),
        in_specs=[seq(qka), seq(v8), seq(z8), seq(state), pl.BlockSpec((1, DV), lambda i: (0, 0))],
        out_specs=(seq(v8), seq(state)),
        out_shape=(jax.ShapeDtypeStruct(v8.shape, f32), jax.ShapeDtypeStruct(state.shape, f32)),
        compiler_params=_cparams(("parallel",)),
        name="gla_sample",
    )(qka, v8, z8, state, gain2)
    return y[:, :n_tok].reshape(b * n_tok, GLA_WIDTH), s_new


def _sample_path(x, cache_c, cache_s, cache_w, state, page_table, wts, w1_pairs, w2_heads):
    (norm_g, w_pack, wa2_pad, ba, w1_big, cmp_bias, w2_big, gla_gain, w_out, out_gain) = wts
    b, n_tok, _ = x.shape
    n = b * n_tok
    n_pages = page_table.shape[1]
    past_len = n_pages * PAGE_SIZE
    assert n_tok <= 8 and past_len % SLC_LEN == 0 and past_len // SLC_LEN <= LANES
    pos = past_len + jnp.arange(n) % n_tok
    (q_hm, kvc, kvs, kvw, _, _, _, _, gate, zn, gq, gk, gv, la, zg) = _inproj(
        x.reshape(1, n, D_MODEL), pos, norm_g, w_pack, wa2_pad, ba, tm=n)

    rows = H_NSA * n_tok
    q_rows = q_hm[0].reshape(H_NSA, b, n_tok, DH).transpose(1, 0, 2, 3)
    zero = jnp.zeros_like(q_rows[:, :G_NSA])
    q_blk = jnp.concatenate([jnp.concatenate([q_rows[:, :G_NSA], zero], axis=-1),
                             jnp.concatenate([zero, q_rows[:, G_NSA:]], axis=-1)], axis=1).reshape(b, rows, LANES)
    gate_rows = gate[:, :3 * H_NSA].reshape(b, n_tok, 3, H_NSA).transpose(0, 3, 1, 2).reshape(b, rows, 3)
    gate_rows = jnp.pad(gate_rows, ((0, 0), (0, 0), (0, LANES - 3)))
    pad_new = lambda a: jnp.pad(a.reshape(b, n_tok, KV_ROW), ((0, 0), (0, 8 - n_tok), (0, 0)))

    kvw_new_t = kvw.reshape(b, n_tok, 2, KVH * DH).transpose(0, 2, 3, 1)
    kvw_new_t = jnp.pad(kvw_new_t, ((0, 0), (0, 0), (0, 0), (LANES - n_tok, 0)))

    o_cmp, sel_rows = _cmp_attn_sample(_pages_native(cache_c), page_table, w1_pairs, cmp_bias, w2_heads,
                                       q_blk, gate_rows, n_tok)
    o_rows, win_t = _slc_win_sample(_pages_native(cache_s), page_table, q_blk, sel_rows, pad_new(kvs),
                                    _pages_native(cache_w), pad_new(kvw), kvw_new_t, gate_rows, o_cmp, n_tok)
    o_nsa = o_rows.reshape(b, H_NSA, n_tok, DH).transpose(0, 2, 1, 3).reshape(n, NSA_WIDTH)

    y_gla, s_new = _gla_sample(gq, gk, la, gv, zg, state, gla_gain, b, n_tok)
    y = _outproj(x.reshape(n, D_MODEL), o_nsa, zn, y_gla, w_out, out_gain, tm=n)
    kv5 = lambda a: a.reshape(b, n_tok, 2, KVH, DH)
    win_new = win_t.reshape(b, 2, KVH, DH, -1).transpose(0, 4, 1, 2, 3)
    return (y.reshape(b, n_tok, D_MODEL), kv5(kvc), kv5(kvs), win_new, s_new)


def kernel(x_prompt, x_sample, cache_cmp_kv, cache_slc_kv, cache_win_kv, state_gla, page_table,
           norm_in_gain, w_in, cmp_pe, cmp_w1, cmp_b1, cmp_w2, gla_wa2, gla_ba, gla_norm_gain,
           w_out, norm_out_gain):
    assert w_in.shape[0] == 1, "single-layer step"
    wts = (norm_in_gain[0], _pack_w_in(w_in[0]), _pad_wa2(gla_wa2[0]), gla_ba[0],
           _cmp_w1_big(cmp_w1[0]), _cmp_bias(cmp_pe[0], cmp_w1[0], cmp_b1[0]), _cmp_w2_big(cmp_w2[0]),
           gla_norm_gain[0], w_out[0].astype(bf16), norm_out_gain)
    yp, cmp_p, slc_p, win_p, gla_p = _prompt_path(x_prompt, wts, *_pack_w_in_prompt(w_in[0]))
    ys, cmp_s, slc_s, win_s, gla_s = _sample_path(
        x_sample, cache_cmp_kv[0], cache_slc_kv[0], cache_win_kv[0], state_gla[0], page_table, wts,
        _cmp_w1_pairs(cmp_w1[0]), _cmp_w2_heads(cmp_w2[0]))
    return (yp, ys, cmp_p[None], cmp_s[None], slc_p[None], slc_s[None], win_p[None], win_s[None],
            gla_p[None], gla_s[None])
```

```python
import functools

import numpy as np
import jax
import jax.numpy as jnp
from jax import lax
from jax.experimental import pallas as pl
from jax.experimental.pallas import tpu as pltpu

f32 = jnp.float32
bf16 = jnp.bfloat16

D_MODEL = 1024
DH = 64
H_NSA = 8
KVH = 2
G_NSA = 4
NSA_WIDTH = H_NSA * DH
CMP_LEN = 32
CMP_STRIDE = 16
CMP_HID = 128
SLC_LEN = 64
SLC_TOP = 16
WINDOW = 512
ROT_HALF = 8
ROPE_THETA = 500000.0
FORCE_SCORE = 1.0e4
H_GLA = 4
DK = 64
DV = 128
GLA_WIDTH = H_GLA * DV
GLA_LR = 16
GLA_TAU = 16.0
GLA_CHUNK = 64
GLA_SUB = 16
EPS = 1e-6
NEG = -1.0e30
LOG2E = 1.4426950408889634
PAGE_SIZE = 128
KV_ROW = 2 * KVH * DH

IN_SIZES = (H_NSA * DH, KV_ROW, KV_ROW, KV_ROW, 3 * H_NSA, NSA_WIDTH,
            H_GLA * DK, H_GLA * DK, H_GLA * DV, GLA_LR, GLA_WIDTH)
IN_OFFSETS = [0] + [int(v) for v in np.cumsum(IN_SIZES)]

LANES = 128
VMEM_LIMIT = 56 * 1024 * 1024

P_Q, P_KVC, P_KVS, P_KVW, P_ZN, P_GQ, P_GK, P_GV, P_ZG, P_MISC = (
    0, 512, 768, 1024, 1280, 1792, 2048, 2304, 2816, 3328)
D_PACK = P_MISC + LANES
MISC_GATE = 0
MISC_GLR = 32


def _cparams(sem):
    return pltpu.CompilerParams(dimension_semantics=sem, vmem_limit_bytes=VMEM_LIMIT)


def _sigmoid(x):
    return 1.0 / (1.0 + jnp.exp(-x))


def _silu(x):
    return x * _sigmoid(x)


def _log_sigmoid(x):
    return jnp.minimum(x, 0.0) - jnp.log1p(jnp.exp(-jnp.abs(x)))


def _dot(a, b):
    return jnp.dot(a, b, preferred_element_type=f32)


def _dot_nt(a, b):
    return lax.dot_general(a, b, (((1,), (1,)), ((), ())), preferred_element_type=f32)


def _dot_tn(a, b):
    return lax.dot_general(a, b, (((0,), (0,)), ((), ())), preferred_element_type=f32)


def _rope_tables(pos):
    n = pos.shape[0]
    inv = ROPE_THETA ** (-(jnp.arange(ROT_HALF, dtype=f32) / ROT_HALF))
    ang = pos.astype(f32)[:, None] * inv[None, :]
    cos, sin = jnp.cos(ang), jnp.sin(ang)
    z8 = jnp.zeros((n, ROT_HALF), f32)
    rest = DH - 2 * ROT_HALF
    c64 = jnp.concatenate([cos, cos, jnp.ones((n, rest), f32)], axis=-1)
    sa64 = jnp.concatenate([-sin, z8, jnp.zeros((n, rest), f32)], axis=-1)
    sb64 = jnp.concatenate([z8, sin, jnp.zeros((n, rest), f32)], axis=-1)
    tile = lambda t: jnp.concatenate([t, t], axis=-1)
    return tile(c64), tile(sa64), tile(sb64)


def _rope128(v, cos, sa, sb):
    return v * cos + pltpu.roll(v, LANES - ROT_HALF, 1) * sa + pltpu.roll(v, ROT_HALF, 1) * sb


def _inproj_kernel(x_ref, g_ref, w_ref, cos_ref, sa_ref, sb_ref, wa2_ref, ba_ref,
                   q_ref, kvc_ref, kvs_ref, kvw_ref, ks_ref, vs_ref, kw_ref, vw_ref,
                   gate_ref, zn_ref, gq_ref, gk_ref, gv_ref, la_ref, zg_ref):
    x = x_ref[...]
    ms = jnp.mean(x * x, axis=-1, keepdims=True)
    hn = (x * lax.rsqrt(ms + EPS) * g_ref[...]).astype(bf16)
    cos, sa, sb = cos_ref[...], sa_ref[...], sb_ref[...]

    def proj(off, width):
        return _dot(hn, w_ref[:, off:off + width])

    qp = proj(P_Q, NSA_WIDTH)
    for c in range(NSA_WIDTH // LANES):
        r = _rope128(qp[:, c * LANES:(c + 1) * LANES], cos, sa, sb) * (DH ** -0.5)
        q_ref[0, 2 * c] = r[:, :DH].astype(bf16)
        q_ref[0, 2 * c + 1] = r[:, DH:].astype(bf16)

    kvc_ref[...] = proj(P_KVC, KV_ROW)

    for off, kv_ref, k_ref, v_ref in ((P_KVS, kvs_ref, ks_ref, vs_ref), (P_KVW, kvw_ref, kw_ref, vw_ref)):
        p = proj(off, KV_ROW)
        k = _rope128(p[:, :LANES], cos, sa, sb)
        v = p[:, LANES:]
        kv_ref[:, :LANES] = k
        kv_ref[:, LANES:] = v
        for h in range(KVH):
            k_ref[0, h] = k[:, h * DH:(h + 1) * DH].astype(bf16)
            v_ref[0, h] = v[:, h * DH:(h + 1) * DH].astype(bf16)

    zn_ref[...] = _silu(proj(P_ZN, NSA_WIDTH))
    gq_ref[...] = proj(P_GQ, H_GLA * DK) * (DK ** -0.5)
    gk_ref[...] = proj(P_GK, H_GLA * DK)
    gv_ref[...] = proj(P_GV, H_GLA * DV)
    zg_ref[...] = _silu(proj(P_ZG, GLA_WIDTH))

    misc = proj(P_MISC, LANES)
    gate_ref[...] = _sigmoid(misc)
    xa = _dot(misc.astype(bf16), wa2_ref[...]) + ba_ref[...]
    la_ref[...] = _log_sigmoid(xa) / GLA_TAU


def _pack_w_in(w_in):
    o = IN_OFFSETS
    seg = lambda i: w_in[:, o[i]:o[i + 1]]
    misc = jnp.zeros((D_MODEL, LANES), w_in.dtype)
    misc = misc.at[:, MISC_GATE:MISC_GATE + 3 * H_NSA].set(seg(4))
    misc = misc.at[:, MISC_GLR:MISC_GLR + GLA_LR].set(seg(9))
    cols = [seg(0), seg(1), seg(2), seg(3), seg(5), seg(6), seg(7), seg(8), seg(10), misc]
    return jnp.concatenate(cols, axis=1).astype(bf16)


def _pad_wa2(wa2):
    pad = jnp.zeros((LANES, H_GLA * DK), wa2.dtype).at[MISC_GLR:MISC_GLR + GLA_LR].set(wa2)
    return pad.astype(bf16)


def _inproj(x, pos, norm_g, w_pack, wa2_pad, ba, tm):
    bk, lk, _ = x.shape
    n = bk * lk
    nt = lk // tm
    cos, sa, sb = _rope_tables(pos)
    tok = lambda w: pl.BlockSpec((tm, w), lambda i: (i, 0))
    tab = pl.BlockSpec((tm, LANES), lambda i: (i % nt, 0))
    full = lambda a: pl.BlockSpec(a.shape, lambda i: (0,) * a.ndim)
    hm = lambda h: pl.BlockSpec((1, h, tm, DH), lambda i: (i // nt, 0, i % nt, 0))
    g2 = norm_g.reshape(1, D_MODEL)
    ba2 = ba.reshape(1, H_GLA * DK)
    out_shape = (
        jax.ShapeDtypeStruct((bk, H_NSA, lk, DH), bf16),
        jax.ShapeDtypeStruct((n, KV_ROW), f32),
        jax.ShapeDtypeStruct((n, KV_ROW), f32),
        jax.ShapeDtypeStruct((n, KV_ROW), f32),
        jax.ShapeDtypeStruct((bk, KVH, lk, DH), bf16),
        jax.ShapeDtypeStruct((bk, KVH, lk, DH), bf16),
        jax.ShapeDtypeStruct((bk, KVH, lk, DH), bf16),
        jax.ShapeDtypeStruct((bk, KVH, lk, DH), bf16),
        jax.ShapeDtypeStruct((n, LANES), f32),
        jax.ShapeDtypeStruct((n, NSA_WIDTH), f32),
        jax.ShapeDtypeStruct((n, H_GLA * DK), f32),
        jax.ShapeDtypeStruct((n, H_GLA * DK), f32),
        jax.ShapeDtypeStruct((n, H_GLA * DV), f32),
        jax.ShapeDtypeStruct((n, H_GLA * DK), f32),
        jax.ShapeDtypeStruct((n, GLA_WIDTH), f32),
    )
    out_specs = (hm(H_NSA), tok(KV_ROW), tok(KV_ROW), tok(KV_ROW), hm(KVH), hm(KVH), hm(KVH), hm(KVH),
                 tok(LANES), tok(NSA_WIDTH), tok(H_GLA * DK), tok(H_GLA * DK), tok(H_GLA * DV),
                 tok(H_GLA * DK), tok(GLA_WIDTH))
    return pl.pallas_call(
        _inproj_kernel,
        grid=(n // tm,),
        in_specs=[tok(D_MODEL), full(g2), full(w_pack), tab, tab, tab, full(wa2_pad), full(ba2)],
        out_specs=out_specs,
        out_shape=out_shape,
        compiler_params=_cparams(("parallel",)),
        name="inproj",
    )(x.reshape(n, D_MODEL), g2, w_pack, cos, sa, sb, wa2_pad, ba2)


PT_Q, PT_KVC, PT_KVS, PT_KVW, PT_GATE = 0, 512, 768, 1024, 1280
PT_ROWS = PT_GATE + 32
PK_ZN, PK_GQ, PK_GK, PK_GV, PK_ZG, PK_MISC = 0, 512, 768, 1024, 1536, 2048
PK_COLS = PK_MISC + LANES


def _inproj_prompt_kernel(x_ref, g_ref, wt_ref, w_ref, cos_ref, sin_ref, wa2_ref, ba_ref,
                          qt_ref, kvct_ref, kvst_ref, kvwt_ref, ks_ref, kw_ref, vst_ref, vwt_ref, kvc_ref,
                          gatet_ref, zn_ref, gq_ref, gk_ref, gv_ref, la_ref, zg_ref):
    x = x_ref[...]
    ms = jnp.mean(x * x, axis=-1, keepdims=True)
    hn = (x * lax.rsqrt(ms + EPS) * g_ref[...]).astype(bf16)
    cos_t, sin_t = cos_ref[...], sin_ref[...]

    def proj_t(off, rows):
        return _dot_nt(wt_ref[off:off + rows, :], hn)

    def rope_rows(v):
        x1, x2 = v[:ROT_HALF], v[ROT_HALF:2 * ROT_HALF]
        return jnp.concatenate([x1 * cos_t - x2 * sin_t, x2 * cos_t + x1 * sin_t, v[2 * ROT_HALF:]], axis=0)

    q_t = proj_t(PT_Q, NSA_WIDTH)
    for h in range(H_NSA):
        qt_ref[0, h] = (rope_rows(q_t[h * DH:(h + 1) * DH]) * (DH ** -0.5 * LOG2E)).astype(bf16)

    kvc_t = proj_t(PT_KVC, KV_ROW)
    kvct_ref[0] = kvc_t
    kvc_ref[...] = kvc_t.T.astype(bf16)

    for off, kvt_ref, k_ref, vt_ref in ((PT_KVS, kvst_ref, ks_ref, vst_ref), (PT_KVW, kvwt_ref, kw_ref, vwt_ref)):
        t = proj_t(off, KV_ROW)
        k_t = jnp.concatenate([rope_rows(t[h * DH:(h + 1) * DH]) for h in range(KVH)], axis=0)
        kvt_ref[0, :LANES] = k_t
        kvt_ref[0, LANES:] = t[LANES:]
        k_tok = k_t.T
        for h in range(KVH):
            k_ref[0, h] = k_tok[:, h * DH:(h + 1) * DH].astype(bf16)
            vt_ref[0, h] = t[LANES + h * DH:LANES + (h + 1) * DH].astype(bf16)

    gatet_ref[0] = _sigmoid(proj_t(PT_GATE, PT_ROWS - PT_GATE))

    def proj(off, width):
        return _dot(hn, w_ref[:, off:off + width])

    zn_ref[...] = _silu(proj(PK_ZN, NSA_WIDTH))
    gq_ref[...] = proj(PK_GQ, H_GLA * DK) * (DK ** -0.5)
    gk_ref[...] = proj(PK_GK, H_GLA * DK)
    gv_ref[...] = proj(PK_GV, H_GLA * DV)
    zg_ref[...] = _silu(proj(PK_ZG, GLA_WIDTH))
    misc = proj(PK_MISC, LANES)
    xa = _dot(misc.astype(bf16), wa2_ref[...]) + ba_ref[...]
    la_ref[...] = _log_sigmoid(xa) / GLA_TAU


def _pack_w_in_prompt(w_in):
    o = IN_OFFSETS
    seg = lambda i: w_in[:, o[i]:o[i + 1]]
    gate_t = jnp.zeros((PT_ROWS - PT_GATE, D_MODEL), w_in.dtype).at[:3 * H_NSA].set(seg(4).T)
    w_t = jnp.concatenate([seg(0).T, seg(1).T, seg(2).T, seg(3).T, gate_t], axis=0).astype(bf16)
    misc = jnp.zeros((D_MODEL, LANES), w_in.dtype).at[:, MISC_GLR:MISC_GLR + GLA_LR].set(seg(9))
    w_tok = jnp.concatenate([seg(5), seg(6), seg(7), seg(8), seg(10), misc], axis=1).astype(bf16)
    return w_t, w_tok


def _inproj_prompt(x, norm_g, w_t, w_tok, wa2_pad, ba, tm):
    b, lq, _ = x.shape
    n = b * lq
    nt = lq // tm
    inv = ROPE_THETA ** (-(jnp.arange(ROT_HALF, dtype=f32) / ROT_HALF))
    ang = inv[:, None] * jnp.arange(lq).astype(f32)[None, :]
    cos_t, sin_t = jnp.cos(ang), jnp.sin(ang)
    tok = lambda w: pl.BlockSpec((tm, w), lambda i: (i, 0))
    tab = pl.BlockSpec((ROT_HALF, tm), lambda i: (0, i % nt))
    full = lambda a: pl.BlockSpec(a.shape, lambda i: (0,) * a.ndim)
    feat = lambda r: pl.BlockSpec((1, r, tm), lambda i: (i // nt, 0, i % nt))
    headf = lambda h: pl.BlockSpec((1, h, DH, tm), lambda i: (i // nt, 0, 0, i % nt))
    headt = lambda h: pl.BlockSpec((1, h, tm, DH), lambda i: (i // nt, 0, i % nt, 0))
    g2 = norm_g.reshape(1, D_MODEL)
    ba2 = ba.reshape(1, H_GLA * DK)
    sds = jax.ShapeDtypeStruct
    out_shape = (
        sds((b, H_NSA, DH, lq), bf16),
        sds((b, KV_ROW, lq), f32), sds((b, KV_ROW, lq), f32), sds((b, KV_ROW, lq), f32),
        sds((b, KVH, lq, DH), bf16), sds((b, KVH, lq, DH), bf16),
        sds((b, KVH, DH, lq), bf16), sds((b, KVH, DH, lq), bf16),
        sds((n, KV_ROW), bf16),
        sds((b, PT_ROWS - PT_GATE, lq), f32),
        sds((n, NSA_WIDTH), f32), sds((n, H_GLA * DK), f32), sds((n, H_GLA * DK), f32),
        sds((n, H_GLA * DV), f32), sds((n, H_GLA * DK), f32), sds((n, GLA_WIDTH), f32),
    )
    out_specs = (headf(H_NSA), feat(KV_ROW), feat(KV_ROW), feat(KV_ROW), headt(KVH), headt(KVH),
                 headf(KVH), headf(KVH), tok(KV_ROW), feat(PT_ROWS - PT_GATE),
                 tok(NSA_WIDTH), tok(H_GLA * DK), tok(H_GLA * DK), tok(H_GLA * DV), tok(H_GLA * DK),
                 tok(GLA_WIDTH))
    return pl.pallas_call(
        _inproj_prompt_kernel,
        grid=(n // tm,),
        in_specs=[tok(D_MODEL), full(g2), full(w_t), full(w_tok), tab, tab, full(wa2_pad), full(ba2)],
        out_specs=out_specs,
        out_shape=out_shape,
        compiler_params=_cparams(("parallel",)),
        name="inproj_prompt",
    )(x.reshape(n, D_MODEL), g2, w_t, w_tok, cos_t, sin_t, wa2_pad, ba2)


CHUNK_W = CMP_STRIDE * KV_ROW
PART_W = 2 * 2 * KVH * CMP_HID
HID_W = 2 * KVH * CMP_HID


def _cmp_w1_big(cmp_w1):
    w1r = cmp_w1.reshape(2, CMP_LEN // CMP_STRIDE, CMP_STRIDE, DH, CMP_HID)
    eye = jnp.eye(2, dtype=cmp_w1.dtype)
    big = jnp.einsum('crsdh,cC,kK->sCKdrckh', w1r, eye, eye)
    return big.reshape(CHUNK_W, PART_W).astype(bf16)


def _cmp_w2_big(cmp_w2):
    eye = jnp.eye(2, dtype=cmp_w2.dtype)
    big = jnp.einsum('chd,cC,kK->ckhCKd', cmp_w2, eye, eye)
    return big.reshape(HID_W, KV_ROW).astype(bf16)


def _cmp_part_kernel(x_ref, w_ref, o_ref):
    o_ref[...] = _dot(x_ref[...].astype(bf16), w_ref[...])


def _cmp_part(chunks, w1_big, tr):
    r = chunks.shape[0]
    return pl.pallas_call(
        _cmp_part_kernel,
        grid=(r // tr,),
        in_specs=[pl.BlockSpec((tr, CHUNK_W), lambda i: (i, 0)),
                  pl.BlockSpec((CHUNK_W, PART_W), lambda i: (0, 0))],
        out_specs=pl.BlockSpec((tr, PART_W), lambda i: (i, 0)),
        out_shape=jax.ShapeDtypeStruct((r, PART_W), f32),
        compiler_params=_cparams(("parallel",)),
        name="cmp_part",
    )(chunks, w1_big)


def _cmp_bias_kernel(pe_ref, w1_ref, b1_ref, o_ref):
    for c in range(2):
        o_ref[c] = _dot(pe_ref[c], w1_ref[c]) + b1_ref[c]


def _cmp_bias(cmp_pe, cmp_w1, cmp_b1):
    pe = jnp.broadcast_to(cmp_pe.reshape(2, 1, CMP_LEN * DH), (2, 8, CMP_LEN * DH))
    b1 = jnp.broadcast_to(cmp_b1.reshape(2, 1, CMP_HID), (2, 8, CMP_HID))
    out = pl.pallas_call(
        _cmp_bias_kernel,
        out_shape=jax.ShapeDtypeStruct((2, 8, CMP_HID), f32),
        name="cmp_bias",
    )(pe, cmp_w1, b1)
    row = out[:, 0, :]
    return jnp.broadcast_to(row[:, None, :], (2, KVH, CMP_HID)).reshape(1, HID_W)


def _compress_finish(part, bias, w2_big, cos, sa, sb):
    n = part.shape[0]
    pre = part[:, :HID_W] + pltpu.roll(part[:, HID_W:], n - 1, 0)
    h = _silu(pre + bias)
    out = _dot(h.astype(bf16), w2_big)
    return _rope128(out[:, :LANES], cos, sa, sb), out[:, LANES:]


def _masked_softmax(s, mask, axis, exp=jnp.exp):
    s = jnp.where(mask, s, NEG)
    e = exp(s - jnp.max(s, axis=axis, keepdims=True))
    return jnp.where(mask, e * (1.0 / jnp.sum(e, axis=axis, keepdims=True)), 0.0)


def _split_bf16(x):
    hi = x.astype(bf16)
    return hi, (x - hi.astype(f32)).astype(bf16)


def _topk_mask(score, k, axis):
    n = score.shape[axis]
    idx = lax.broadcasted_iota(jnp.int32, score.shape, axis)
    sel = jnp.zeros(score.shape, f32)
    for _ in range(k):
        m = jnp.max(score, axis=axis, keepdims=True)
        first = jnp.min(jnp.where(score == m, idx, n), axis=axis, keepdims=True)
        pick = idx == first
        sel = jnp.where(pick, 1.0, sel)
        score = jnp.where(pick, NEG, score)
    return sel


def _topk_rows_by_rank(score, k):
    r, n = score.shape
    cols = jnp.concatenate([score, jnp.zeros((n - r, n), f32)], axis=0).T
    earlier = jnp.where(lax.broadcasted_iota(jnp.int32, (n, n), 0) < lax.broadcasted_iota(jnp.int32, (n, n), 1),
                        1.0, 0.0)
    rows = []
    for c in range(r):
        col, row = cols[:, c:c + 1], score[c:c + 1, :]
        before = jnp.where(col > row, 1.0, jnp.where(col == row, earlier, 0.0))
        rows.append(jnp.where(jnp.sum(before, axis=0, keepdims=True) < k, 1.0, 0.0))
    return jnp.concatenate(rows, axis=0)


def _cover_t(n_slc, n_chunk):
    start = np.arange(n_chunk)[None, :] * CMP_STRIDE
    j = np.arange(n_slc)[:, None]
    cov = (start < (j + 1) * SLC_LEN) & (start + CMP_LEN > j * SLC_LEN)
    return jnp.asarray(cov.astype(np.float32), dtype=bf16)


def _cmp_attn_kernel(part_ref, bias_ref, w2_ref, cos_ref, sa_ref, sb_ref, qt_ref, gatet_ref, cov_ref,
                     ot_ref, selt_ref, kc_ref, vct_ref, *, tq):
    t = pl.program_id(1)

    @pl.when(t == 0)
    def _():
        k, v = _compress_finish(part_ref[0], bias_ref[...], w2_ref[...], cos_ref[...], sa_ref[...], sb_ref[...])
        v_t = v.T
        for h in range(KVH):
            kc_ref[h] = k[:, h * DH:(h + 1) * DH].astype(bf16)
            vct_ref[h] = v_t[h * DH:(h + 1) * DH].astype(bf16)

    nc = kc_ref.shape[1]
    nb = cov_ref.shape[0]
    qpos = t * tq + lax.broadcasted_iota(jnp.int32, (1, tq), 1)
    end_pos = lax.broadcasted_iota(jnp.int32, (nc, 1), 0) * CMP_STRIDE + (CMP_LEN - 1)
    mask_t = end_pos <= qpos
    gate_t = gatet_ref[0]
    cov = cov_ref[...]

    jblk = lax.broadcasted_iota(jnp.int32, (nb, tq), 0)
    qblk = qpos // SLC_LEN
    valid = jblk <= qblk
    forced = (jblk == 0) | (jblk == qblk) | (jblk == qblk - 1)

    scores = []
    for kh in range(KVH):
        kc, vc_t = kc_ref[kh], vct_ref[kh]
        psum = jnp.zeros((nc, tq), f32)
        for g in range(G_NSA):
            h = kh * G_NSA + g
            p_t = _masked_softmax(_dot(kc, qt_ref[0, h]), mask_t, axis=0, exp=jnp.exp2)
            ot_ref[0, h * DH:(h + 1) * DH] = _dot(vc_t, p_t.astype(bf16)) * gate_t[h:h + 1]
            psum = psum + p_t
        hi, lo = _split_bf16(psum)
        imp = _dot(cov, hi) + _dot(cov, lo)
        scores.append(jnp.where(valid, jnp.where(forced, FORCE_SCORE, imp), -1.0))
    picked = _topk_mask(jnp.concatenate(scores, axis=1), SLC_TOP, axis=0)
    for kh in range(KVH):
        sel_t = jnp.where(valid, picked[:, kh * tq:(kh + 1) * tq], 0.0)
        if nb < LANES:
            sel_t = jnp.concatenate([sel_t, jnp.zeros((LANES - nb, tq), f32)], axis=0)
        selt_ref[0, kh] = sel_t


def _cmp_attn(part, bias, w2_big, q_t, gate_t, tq):
    b, nc, _ = part.shape
    lq = q_t.shape[3]
    n_slc = lq // SLC_LEN
    end_pos = jnp.arange(nc) * CMP_STRIDE + (CMP_LEN - 1)
    cos, sa, sb = _rope_tables(end_pos)
    cov = _cover_t(n_slc, nc)
    full = lambda a: pl.BlockSpec(a.shape, lambda i, t: (0,) * a.ndim)
    return pl.pallas_call(
        functools.partial(_cmp_attn_kernel, tq=tq),
        grid=(b, lq // tq),
        in_specs=[pl.BlockSpec((1, nc, PART_W), lambda i, t: (i, 0, 0)),
                  full(bias), full(w2_big), full(cos), full(sa), full(sb),
                  pl.BlockSpec((1, H_NSA, DH, tq), lambda i, t: (i, 0, 0, t)),
                  pl.BlockSpec((1, gate_t.shape[1], tq), lambda i, t: (i, 0, t)),
                  full(cov)],
        out_specs=(pl.BlockSpec((1, NSA_WIDTH, tq), lambda i, t: (i, 0, t)),
                   pl.BlockSpec((1, KVH, LANES, tq), lambda i, t: (i, 0, 0, t))),
        out_shape=(jax.ShapeDtypeStruct((b, NSA_WIDTH, lq), f32),
                   jax.ShapeDtypeStruct((b, KVH, LANES, lq), f32)),
        scratch_shapes=[pltpu.VMEM((KVH, nc, DH), bf16), pltpu.VMEM((KVH, DH, nc), bf16)],
        compiler_params=_cparams(("parallel", "arbitrary")),
        name="cmp_attn",
    )(part, bias, w2_big, cos, sa, sb, q_t, gate_t, cov)


def _block_expand(n_slc_pad, n_keys):
    e = (np.arange(n_keys)[None, :] // SLC_LEN) == np.arange(n_slc_pad)[:, None]
    return jnp.asarray(e.astype(np.float32), dtype=bf16)


def _mask_bias_t(allowed):
    bias = jnp.where(allowed, 0.0, NEG)
    return jnp.concatenate([bias] * G_NSA, axis=1)


def _sublane_fold(x, op):
    return functools.reduce(op, [x[r:r + 8] for r in range(0, x.shape[0], 8)])


def _slc_win_kernel(qt_ref, ks_ref, vst_ref, kw_ref, vwt_ref, selt_ref, gatet_ref, ocmpt_ref,
                    o_ref, s_ref, m_ref, l_ref, acc_ref, *, tq, tk):
    kh = pl.program_id(1)
    t = pl.program_id(2)
    q0 = t * tq
    qpos = q0 + lax.broadcasted_iota(jnp.int32, (1, tq), 1)
    q_t = jnp.concatenate([qt_ref[0, g] for g in range(G_NSA)], axis=1)
    blocks_per_tile = tk // SLC_LEN

    n_kt = (q0 + tq + tk - 1) // tk
    m_ref[...] = jnp.full(m_ref.shape, NEG, f32)

    def score_body(kt, carry):
        k0 = pl.multiple_of(kt * tk, tk)
        kpos = k0 + lax.broadcasted_iota(jnp.int32, (tk, 1), 0)
        j0 = pl.multiple_of(kt * blocks_per_tile, blocks_per_tile)
        sel = selt_ref[0, 0, pl.ds(j0, blocks_per_tile), :]
        picked = jnp.concatenate([jnp.broadcast_to(sel[j:j + 1], (SLC_LEN, tq)) for j in range(blocks_per_tile)],
                                 axis=0)
        s = _dot(ks_ref[0, 0, pl.ds(k0, tk), :], q_t) + _mask_bias_t((picked > 0.5) & (kpos <= qpos))
        s_ref[pl.ds(k0, tk), :] = s
        m_ref[...] = jnp.maximum(m_ref[...], _sublane_fold(s, jnp.maximum))
        return carry

    lax.fori_loop(0, n_kt, score_body, 0)
    m = jnp.max(m_ref[...], axis=0, keepdims=True)
    l_ref[...] = jnp.zeros(l_ref.shape, f32)
    acc_ref[...] = jnp.zeros(acc_ref.shape, f32)

    def pv_body(kt, carry):
        k0 = pl.multiple_of(kt * tk, tk)
        p = jnp.exp2(s_ref[pl.ds(k0, tk), :] - m)
        l_ref[...] += _sublane_fold(p, jnp.add)
        acc_ref[...] += _dot(vst_ref[0, 0, :, pl.ds(k0, tk)], p.astype(bf16))
        return carry

    lax.fori_loop(0, n_kt, pv_body, 0)
    o_slc = acc_ref[...] / jnp.sum(l_ref[...], axis=0, keepdims=True)

    band = WINDOW + tq
    w0 = pl.multiple_of(jnp.maximum(q0 - WINDOW, 0), tq)
    kpos = w0 + lax.broadcasted_iota(jnp.int32, (band, 1), 0)
    s = _dot(kw_ref[0, 0, pl.ds(w0, band), :], q_t) + _mask_bias_t((kpos <= qpos) & (kpos > qpos - WINDOW))
    p = jnp.exp2(s - jnp.max(s, axis=0, keepdims=True))
    o_win =_dot(vwt_ref[0, 0, :, pl.ds(w0, band)], p.astype(bf16)) / jnp.sum(p, axis=0, keepdims=True)

    gate_t = gatet_ref[0]
    row = lambda c, g: jnp.where(kh == 0, gate_t[c * H_NSA + g:c * H_NSA + g + 1],
                                 gate_t[c * H_NSA + G_NSA + g:c * H_NSA + G_NSA + g + 1])
    g_slc = jnp.concatenate([row(1, g) for g in range(G_NSA)], axis=1)
    g_win = jnp.concatenate([row(2, g) for g in range(G_NSA)], axis=1)
    o_t = g_slc * o_slc + g_win * o_win
    o_tok = jnp.concatenate([o_t, jnp.zeros((LANES - DH, o_t.shape[1]), f32)], axis=0).T
    ocmp_tok = ocmpt_ref[0].T
    o_ref[0] = ocmp_tok + jnp.concatenate([o_tok[g * tq:(g + 1) * tq, :DH] for g in range(G_NSA)], axis=1)


def _slc_win(q_t, ks, vs_t, kw, vw_t, sel_t, gate_t, o_cmp_t, tq, tk):
    b, _, _, lq = q_t.shape
    assert lq >= WINDOW + tq and lq % tk == 0 and tk % (8 * SLC_LEN) == 0
    gw = G_NSA * DH
    k_spec = pl.BlockSpec((1, 1, lq, DH), lambda i, k, t: (i, k, 0, 0))
    vt_spec = pl.BlockSpec((1, 1, DH, lq), lambda i, k, t: (i, k, 0, 0))
    lanes = G_NSA * tq
    return pl.pallas_call(
        functools.partial(_slc_win_kernel, tq=tq, tk=tk),
        grid=(b, KVH, lq // tq),
        in_specs=[pl.BlockSpec((1, G_NSA, DH, tq), lambda i, k, t: (i, k, 0, t)),
                  k_spec, vt_spec, k_spec, vt_spec,
                  pl.BlockSpec((1, 1, LANES, tq), lambda i, k, t: (i, k, 0, t)),
                  pl.BlockSpec((1, gate_t.shape[1], tq), lambda i, k, t: (i, 0, t)),
                  pl.BlockSpec((1, gw, tq), lambda i, k, t: (i, k, t))],
        out_specs=pl.BlockSpec((1, tq, gw), lambda i, k, t: (i, t, k)),
        out_shape=jax.ShapeDtypeStruct((b, lq, NSA_WIDTH), f32),
        scratch_shapes=[pltpu.VMEM((lq, lanes), f32),
                        pltpu.VMEM((8, lanes), f32),
                        pltpu.VMEM((8, lanes), f32),
                        pltpu.VMEM((DH, lanes), f32)],
        compiler_params=_cparams(("parallel", "parallel", "parallel")),
        name="slc_win",
    )(q_t, ks, vs_t, kw, vw_t, sel_t, gate_t, o_cmp_t)


def _rms_gain(o, gain):
    return o * lax.rsqrt(jnp.mean(o * o, axis=-1, keepdims=True) + EPS) * gain


def _gla_kernel(q_ref, k_ref, v_ref, la_ref, z_ref, gain_ref, y_ref, st_ref, s_scr, *, tl):
    t = pl.program_id(2)

    @pl.when(t == 0)
    def _():
        s_scr[...] = jnp.zeros(s_scr.shape, f32)

    c = GLA_CHUNK
    row = lax.broadcasted_iota(jnp.int32, (c, c), 0)
    col = lax.broadcasted_iota(jnp.int32, (c, c), 1)
    causal = col <= row
    tril = jnp.where(causal, 1.0, 0.0).astype(bf16)
    lane = lax.broadcasted_iota(jnp.int32, (1, LANES), 1)
    head_mask = [jnp.where(lane < DK, 1.0, 0.0), jnp.where(lane >= DK, 1.0, 0.0)]
    gain = gain_ref[...]

    chunks = []
    for ci in range(tl // c):
        rows = slice(ci * c, (ci + 1) * c)
        q, k, la = q_ref[rows, :], k_ref[rows, :], la_ref[rows, :]
        hi, rest = la.astype(bf16), la - la.astype(bf16).astype(f32)
        mid, lo = rest.astype(bf16), (rest - rest.astype(bf16).astype(f32)).astype(bf16)
        bc = _dot(tril, hi) + _dot(tril, mid) + _dot(tril, lo)
        ref_row = bc[c // 2 - 1:c // 2, :]
        b_last = bc[c - 1:c, :]
        q_a = q * jnp.exp(bc - ref_row)
        k_a = (k * jnp.exp(ref_row - bc)).astype(bf16)
        q_s = q * jnp.exp(bc)
        k_s = k * jnp.exp(b_last - bc)
        o_intra, q_inter, add = [], [], None
        for h in range(2):
            mh = head_mask[h]
            vh = v_ref[rows, h * DV:(h + 1) * DV].astype(bf16)
            a = jnp.where(causal, _dot_nt((q_a * mh).astype(bf16), k_a), 0.0)
            o_intra.append(_dot(a.astype(bf16), vh))
            q_inter.append((q_s * mh).astype(bf16))
            u = _dot_tn(vh, (k_s * mh).astype(bf16))
            add = u if add is None else add + u
        chunks.append((rows, jnp.exp(b_last), add, o_intra, q_inter))

    st = s_scr[...]
    for rows, decay, add, o_intra, q_inter in chunks:
        st_b = st.astype(bf16)
        for h in range(2):
            o = o_intra[h] + _dot_nt(q_inter[h], st_b)
            y_ref[rows, h * DV:(h + 1) * DV] = _rms_gain(o, gain) * z_ref[rows, h * DV:(h + 1) * DV]
        st = st * decay + add
    s_scr[...] = st
    st_ref[0, 0] = st


def _gla(gq, gk, gv, la, zg, gain, b, tl):
    n = gq.shape[0]
    nt = n // b // tl
    pairs = H_GLA // 2
    qk = pl.BlockSpec((tl, LANES), lambda i, p, t: (i * nt + t, p))
    vz = pl.BlockSpec((tl, 2 * DV), lambda i, p, t: (i * nt + t, p))
    gain2 = gain.reshape(1, DV)
    y, st = pl.pallas_call(
        functools.partial(_gla_kernel, tl=tl),
        grid=(b, pairs, nt),
        in_specs=[qk, qk, vz, qk, vz, pl.BlockSpec((1, DV), lambda i, p, t: (0, 0))],
        out_specs=(vz, pl.BlockSpec((1, 1, DV, LANES), lambda i, p, t: (i, p, 0, 0))),
        out_shape=(jax.ShapeDtypeStruct((n, GLA_WIDTH), f32),
                   jax.ShapeDtypeStruct((b, pairs, DV, LANES), f32)),
        scratch_shapes=[pltpu.VMEM((DV, LANES), f32)],
        compiler_params=_cparams(("parallel", "parallel", "arbitrary")),
        name="gla",
    )(gq, gk, gv, la, zg, gain2)
    state = st.reshape(b, pairs, DV, 2, DK).transpose(0, 1, 3, 4, 2).reshape(b, H_GLA, DK, DV)
    return y, state


def _outproj_kernel(x_ref, on_ref, zn_ref, yg_ref, w_ref, g_ref, y_ref):
    y_nsa = (on_ref[...] * zn_ref[...]).astype(bf16)
    mix = _dot(y_nsa, w_ref[:NSA_WIDTH, :]) + _dot(yg_ref[...].astype(bf16), w_ref[NSA_WIDTH:, :])
    y_ref[...] = _rms_gain(x_ref[...] + mix, g_ref[...])


def _outproj(x, o_nsa, zn, y_gla, w_out, gain, tm):
    n = x.shape[0]
    tok = lambda w: pl.BlockSpec((tm, w), lambda i: (i, 0))
    g2 = gain.reshape(1, D_MODEL)
    return pl.pallas_call(
        _outproj_kernel,
        grid=(n // tm,),
        in_specs=[tok(D_MODEL), tok(NSA_WIDTH), tok(NSA_WIDTH), tok(GLA_WIDTH),
                  pl.BlockSpec(w_out.shape, lambda i: (0, 0)), pl.BlockSpec(g2.shape, lambda i: (0, 0))],
        out_specs=tok(D_MODEL),
        out_shape=jax.ShapeDtypeStruct((n, D_MODEL), f32),
        compiler_params=_cparams(("parallel",)),
        name="outproj",
    )(x, o_nsa, zn, y_gla, w_out, g2)


def _prompt_path(x, wts, w_t, w_tok):
    (norm_g, _, wa2_pad, ba, w1_big, cmp_bias, w2_big, gla_gain, w_out, out_gain) = wts
    b, lq, _ = x.shape
    (q_t, kvc_t, kvs_t, kvw_t, ks, kw, vs_t, vw_t, kvc_tok, gate_t, zn, gq, gk, gv, la, zg) = _inproj_prompt(
        x, norm_g, w_t, w_tok, wa2_pad, ba, tm=512)
    n_chunk = lq // CMP_STRIDE
    part = _cmp_part(kvc_tok.reshape(b * n_chunk, CHUNK_W), w1_big, tr=n_chunk)
    o_cmp_t, sel_t = _cmp_attn(part.reshape(b, n_chunk, PART_W), cmp_bias, w2_big, q_t, gate_t, tq=512)
    o_nsa = _slc_win(q_t, ks, vs_t, kw, vw_t, sel_t, gate_t, o_cmp_t, tq=256, tk=512)
    y_gla, state = _gla(gq, gk, gv, la, zg, gla_gain, b, tl=512)
    y = _outproj(x.reshape(b * lq, D_MODEL), o_nsa.reshape(b * lq, NSA_WIDTH), zn, y_gla, w_out, out_gain, tm=512)
    kv5 = lambda a: a.reshape(b, 2, KVH, DH, -1).transpose(0, 4, 1, 2, 3)
    wlen = min(WINDOW, lq)
    return (y.reshape(b, lq, D_MODEL), kv5(kvc_t), kv5(kvs_t), kv5(kvw_t[:, :, lq - wlen:]), state)


def _pages_native(cache):
    n_pool = cache.shape[0]
    return cache.transpose(0, 2, 3, 4, 1).reshape(n_pool, 2, KVH * DH, cache.shape[1])


def _page_gather(make_copies):
    i = pl.program_id(0)
    slot = i % 2

    @pl.when(i == 0)
    def _():
        for c in make_copies(0, 0):
            c.start()

    @pl.when(i + 1 < pl.num_programs(0))
    def _():
        for c in make_copies(i + 1, 1 - slot):
            c.start()

    for c in make_copies(i, slot):
        c.wait()
    return slot


def _cmp_w1_pairs(cmp_w1):
    ratio = CMP_LEN // CMP_STRIDE
    w1r = cmp_w1.reshape(2, ratio, CMP_STRIDE // 2, 2, DH, CMP_HID)
    eye = jnp.eye(KVH, dtype=cmp_w1.dtype)
    big = jnp.einsum('crpjdh,kK->cpjKdrkh', w1r, eye)
    return big.reshape(2, CMP_STRIDE // 2, 2 * KVH * DH, ratio * KVH * CMP_HID).astype(bf16)


def _cmp_w2_heads(cmp_w2):
    eye = jnp.eye(KVH, dtype=cmp_w2.dtype)
    return jnp.einsum('chd,kK->ckhKd', cmp_w2, eye).reshape(2, KVH * CMP_HID, KVH * DH).astype(bf16)


def _group_sum_matrix(n_tok):
    r = np.arange(KVH * n_tok)[:, None]
    c = np.arange(H_NSA * n_tok)[None, :]
    m = ((c // (G_NSA * n_tok)) == (r // n_tok)) & ((c % n_tok) == (r % n_tok))
    return m.astype(np.float32)


def _cmp_attn_sample_kernel(pt_ref, cache_ref, perm_ref, w1_ref, bias_ref, w2_ref, cos_ref, sa_ref, sb_ref, q_ref,
                            gate_ref, cov_ref, gs_ref, gst_ref, o_ref, sel_ref, buf, rows_scr, sem,
                            *, n_pages, n_tok, past_len):
    def copies(seq, slot):
        return [pltpu.make_async_copy(cache_ref.at[pt_ref[seq, p]], buf.at[slot, p], sem.at[slot])
                for p in range(n_pages)]

    slot = _page_gather(copies)

    cpp = PAGE_SIZE // CMP_STRIDE
    perm = perm_ref[...]

    group = 4

    def to_rows(g, carry):
        for u in range(group):
            p = g * group + u
            r0 = pl.multiple_of(p * cpp, cpp)
            for c in range(2):
                t = _dot_nt(perm, buf[slot, p, c].astype(bf16))
                for s in range(CMP_STRIDE):
                    rows_scr[c, s, pl.ds(r0, cpp), :] = t[s * cpp:(s + 1) * cpp]
        return carry

    lax.fori_loop(0, n_pages // group, to_rows, 0)

    nc = n_pages * cpp
    kv = []
    for c in range(2):
        part = jnp.zeros((nc, w1_ref.shape[-1]), f32)
        for pr in range(CMP_STRIDE // 2):
            lhs = jnp.concatenate([rows_scr[c, 2 * pr], rows_scr[c, 2 * pr + 1]], axis=-1)
            part = part + _dot(lhs.astype(bf16), w1_ref[c, pr])
        half = part.shape[1] // 2
        pre = part[:, :half] + pltpu.roll(part[:, half:], nc - 1, 0)
        hid = _silu(pre + bias_ref[:, c * half:(c + 1) * half])
        kv.append(_dot(hid.astype(bf16), w2_ref[c]))
    k = _rope128(kv[0], cos_ref[...], sa_ref[...], sb_ref[...])
    v = kv[1]
    rows = H_NSA * n_tok
    qpos = past_len + lax.broadcasted_iota(jnp.int32, (rows, 1), 0) % n_tok
    end_c = lax.broadcasted_iota(jnp.int32, (1, nc), 1) * CMP_STRIDE + (CMP_LEN - 1)
    mask = end_c <= qpos
    p = _masked_softmax(_dot_nt(q_ref[0], k.astype(bf16)), mask, axis=-1)
    o_ref[0] = _dot(p.astype(bf16), v.astype(bf16)) * gate_ref[0][:, 0:1]
    hi, lo = _split_bf16(p)
    gs = gs_ref[...]
    ph, pl_ = _split_bf16(_dot(gs, hi) + _dot(gs, lo))
    imp = _dot(ph, cov_ref[...]) + _dot(pl_, cov_ref[...])
    n_blk = imp.shape[1]
    jblk = lax.broadcasted_iota(jnp.int32, imp.shape, 1)
    forced = (jblk == 0) | (jblk == n_blk - 1)
    sel = _topk_rows_by_rank(jnp.where(forced, FORCE_SCORE, imp), SLC_TOP - 1)
    sel_ref[0] = _dot(gst_ref[...], sel.astype(bf16)).astype(bf16)


def _cmp_attn_sample(cache_t, page_table, w1_pairs, bias, w2_heads, q_blk, gate_rows, n_tok):
    b, n_pages = page_table.shape
    past_len = n_pages * PAGE_SIZE
    nc = past_len // CMP_STRIDE
    rows = H_NSA * n_tok
    n_blk = past_len // SLC_LEN
    end_pos = jnp.arange(nc) * CMP_STRIDE + (CMP_LEN - 1)
    cos, sa, sb = _rope_tables(end_pos)
    cov = _cover_t(n_blk, nc).T
    gs = jnp.asarray(_group_sum_matrix(n_tok), dtype=bf16)
    gst = gs.T
    r = np.arange(PAGE_SIZE)
    cpp = PAGE_SIZE // CMP_STRIDE
    perm = jnp.asarray((r[None, :] == (r[:, None] % cpp) * CMP_STRIDE + r[:, None] // cpp).astype(np.float32),
                       dtype=bf16)
    full = lambda a: pl.BlockSpec(a.shape, lambda i, pt: (0,) * a.ndim)
    seq = lambda a: pl.BlockSpec((1,) + a.shape[1:], lambda i, pt: (i,) + (0,) * (a.ndim - 1))
    grid_spec = pltpu.PrefetchScalarGridSpec(
        num_scalar_prefetch=1,
        grid=(b,),
        in_specs=[pl.BlockSpec(memory_space=pl.ANY), full(perm), full(w1_pairs), full(bias), full(w2_heads),
                  full(cos), full(sa), full(sb), seq(q_blk), seq(gate_rows), full(cov), full(gs), full(gst)],
        out_specs=(pl.BlockSpec((1, rows, LANES), lambda i, pt: (i, 0, 0)),
                   pl.BlockSpec((1, rows, n_blk), lambda i, pt: (i, 0, 0))),
        scratch_shapes=[pltpu.VMEM((2, n_pages) + cache_t.shape[1:], f32),
                        pltpu.VMEM((2, CMP_STRIDE, nc, KVH * DH), f32),
                        pltpu.SemaphoreType.DMA((2,))],
    )
    return pl.pallas_call(
        functools.partial(_cmp_attn_sample_kernel, n_pages=n_pages, n_tok=n_tok, past_len=past_len),
        grid_spec=grid_spec,
        out_shape=(jax.ShapeDtypeStruct((b, rows, LANES), f32),
                   jax.ShapeDtypeStruct((b, rows, n_blk), bf16)),
        compiler_params=_cparams(("arbitrary",)),
        name="cmp_attn_sample",
    )(page_table, cache_t, perm, w1_pairs, bias, w2_heads, cos, sa, sb, q_blk, gate_rows, cov, gs, gst)


def _slc_win_sample_kernel(pt_ref, cache_ref, q_ref, sel_ref, e_ref, snew_ref, cw_ref, wnew_ref, wnewt_ref,
                           gate_ref, ocmp_ref, o_ref, wout_ref, buf, sem, *, n_pages, n_tok, win_off):
    def copies(seq, slot):
        return [pltpu.make_async_copy(cache_ref.at[pt_ref[seq, p], c],
                                      buf.at[slot, c, :, pl.ds(p * PAGE_SIZE, PAGE_SIZE)], sem.at[slot])
                for p in range(n_pages) for c in range(2)]

    slot = _page_gather(copies)
    rows = H_NSA * n_tok
    q = q_ref[0]
    tok = lax.broadcasted_iota(jnp.int32, (rows, 1), 0) % n_tok
    new_i = lax.broadcasted_iota(jnp.int32, (1, snew_ref.shape[1]), 1)
    new_ok = (new_i <= tok) & (new_i < n_tok)

    def attend(keys_t, vals_t, allowed, new_ref):
        s = jnp.where(allowed, _dot(q, keys_t), NEG)
        k_new = new_ref[0][:, :LANES].astype(bf16)
        v_new = new_ref[0][:, LANES:].astype(bf16)
        s_new = jnp.where(new_ok, _dot_nt(q, k_new), NEG)
        m = jnp.maximum(jnp.max(s, axis=-1, keepdims=True), jnp.max(s_new, axis=-1, keepdims=True))
        p = jnp.where(allowed, jnp.exp(s - m), 0.0)
        p_new = jnp.where(new_ok, jnp.exp(s_new - m), 0.0)
        l = jnp.sum(p, axis=-1, keepdims=True) + jnp.sum(p_new, axis=-1, keepdims=True)
        return (_dot_nt(p.astype(bf16), vals_t) + _dot(p_new.astype(bf16), v_new)) / l

    picked = _dot(sel_ref[0], e_ref[...]) > 0.5
    o_slc = attend(buf[slot, 0].astype(bf16), buf[slot, 1].astype(bf16), picked, snew_ref)

    wbuf = cw_ref.shape[-1]
    win_i = lax.broadcasted_iota(jnp.int32, (1, wbuf), 1)
    o_win = attend(cw_ref[0, 0].astype(bf16), cw_ref[0, 1].astype(bf16), win_i > tok + win_off, wnew_ref)

    lane = lax.broadcasted_iota(jnp.int32, (1, LANES), 1)
    for c in range(2):
        shifted = pltpu.roll(cw_ref[0, c], wbuf - n_tok, 1)
        tail = jnp.where(lane >= LANES - n_tok, wnewt_ref[0, c], shifted[:, wbuf - LANES:])
        wout_ref[0, c] = jnp.concatenate([shifted[:, :wbuf - LANES], tail], axis=-1)

    gate = gate_ref[0]
    o = ocmp_ref[0] + gate[:, 1:2] * o_slc + gate[:, 2:3] * o_win
    second_kvh = lax.broadcasted_iota(jnp.int32, (rows, 1), 0) >= G_NSA * n_tok
    o_ref[0] = jnp.where(second_kvh, o[:, DH:], o[:, :DH])


def _slc_win_sample(cache_t, page_table, q_blk, sel_rows, kvs_new, win_t, kvw_new, kvw_new_t, gate_rows,
                    o_cmp, n_tok):
    b, n_pages = page_table.shape
    rows = H_NSA * n_tok
    past_len = n_pages * PAGE_SIZE
    wbuf = win_t.shape[-1]
    e = _block_expand(sel_rows.shape[-1], past_len)
    full = lambda a: pl.BlockSpec(a.shape, lambda i, pt: (0,) * a.ndim)
    seq = lambda a: pl.BlockSpec((1,) + a.shape[1:], lambda i, pt: (i,) + (0,) * (a.ndim - 1))
    grid_spec = pltpu.PrefetchScalarGridSpec(
        num_scalar_prefetch=1,
        grid=(b,),
        in_specs=[pl.BlockSpec(memory_space=pl.ANY), seq(q_blk), seq(sel_rows), full(e), seq(kvs_new),
                  seq(win_t), seq(kvw_new), seq(kvw_new_t), seq(gate_rows), seq(o_cmp)],
        out_specs=(pl.BlockSpec((1, rows, DH), lambda i, pt: (i, 0, 0)), seq(win_t)),
        scratch_shapes=[pltpu.VMEM((2, 2, KVH * DH, past_len), f32), pltpu.SemaphoreType.DMA((2,))],
    )
    return pl.pallas_call(
        functools.partial(_slc_win_sample_kernel, n_pages=n_pages, n_tok=n_tok, win_off=wbuf - WINDOW),
        grid_spec=grid_spec,
        out_shape=(jax.ShapeDtypeStruct((b, rows, DH), f32), jax.ShapeDtypeStruct(win_t.shape, f32)),
        compiler_params=_cparams(("arbitrary",)),
        name="slc_win_sample",
    )(page_table, cache_t, q_blk, sel_rows, e, kvs_new, win_t, kvw_new, kvw_new_t, gate_rows, o_cmp)


def _gla_sample_kernel(qka_ref, v_ref, z_ref, s_ref, gain_ref, y_ref, so_ref, *, n_tok):
    gain = gain_ref[...]
    y_ref[...] = jnp.zeros(y_ref.shape, f32)
    for i in range(qka_ref.shape[0]):
        for h in range(H_GLA):
            qka = qka_ref[i, h]
            s = s_ref[i, h]
            for t in range(n_tok):
                q_t = qka[:, t:t + 1]
                k_t = qka[:, n_tok + t:n_tok + t + 1]
                a_t = jnp.exp(qka[:, 2 * n_tok + t:2 * n_tok + t + 1])
                s = a_t * s + k_t * v_ref[i, t:t + 1, h * DV:(h + 1) * DV]
                o = jnp.sum(q_t * s, axis=0, keepdims=True)
                y_ref[i, t:t + 1, h * DV:(h + 1) * DV] = (_rms_gain(o, gain)
                                                          * z_ref[i, t:t + 1, h * DV:(h + 1) * DV])
            so_ref[i, h] = s


def _gla_sample(gq, gk, la, gv, zg, state, gain, b, n_tok):
    def cols(a):
        return a.reshape(b, n_tok, H_GLA, DK).transpose(0, 2, 3, 1)
    qka = jnp.concatenate([cols(gq), cols(gk), cols(la)], axis=-1)
    qka = jnp.pad(qka, ((0, 0), (0, 0), (0, 0), (0, LANES - 3 * n_tok)))
    pad_rows = lambda a: jnp.pad(a.reshape(b, n_tok, -1), ((0, 0), (0, 8 - n_tok), (0, 0)))
    v8, z8 = pad_rows(gv), pad_rows(zg)
    gain2 = gain.reshape(1, DV)
    per_step = 4 if b % 4 == 0 else 1
    seq = lambda a: pl.BlockSpec((per_step,) + a.shape[1:], lambda i: (i,) + (0,) * (a.ndim - 1))
    y, s_new = pl.pallas_call(
        functools.partial(_gla_sample_kernel, n_tok=n_tok),
        grid=(b // per_step,),
        in_specs=[seq(qka), seq(v8), seq(z8), seq(state), pl.BlockSpec((1, DV), lambda i: (0, 0))],
        out_specs=(seq(v8), seq(state)),
        out_shape=(jax.ShapeDtypeStruct(v8.shape, f32), jax.ShapeDtypeStruct(state.shape, f32)),
        compiler_params=_cparams(("parallel",)),
        name="gla_sample",
    )(qka, v8, z8, state, gain2)
    return y[:, :n_tok].reshape(b * n_tok, GLA_WIDTH), s_new


def _sample_path(x, cache_c, cache_s, cache_w, state, page_table, wts, w1_pairs, w2_heads):
    (norm_g, w_pack, wa2_pad, ba, w1_big, cmp_bias, w2_big, gla_gain, w_out, out_gain) = wts
    b, n_tok, _ = x.shape
    n = b * n_tok
    n_pages = page_table.shape[1]
    past_len = n_pages * PAGE_SIZE
    assert n_tok <= 8 and past_len % SLC_LEN == 0 and past_len // SLC_LEN <= LANES
    pos = past_len + jnp.arange(n) % n_tok
    (q_hm, kvc, kvs, kvw, _, _, _, _, gate, zn, gq, gk, gv, la, zg) = _inproj(
        x.reshape(1, n, D_MODEL), pos, norm_g, w_pack, wa2_pad, ba, tm=n)

    rows = H_NSA * n_tok
    q_rows = q_hm[0].reshape(H_NSA, b, n_tok, DH).transpose(1, 0, 2, 3)
    zero = jnp.zeros_like(q_rows[:, :G_NSA])
    q_blk = jnp.concatenate([jnp.concatenate([q_rows[:, :G_NSA], zero], axis=-1),
                             jnp.concatenate([zero, q_rows[:, G_NSA:]], axis=-1)], axis=1).reshape(b, rows, LANES)
    gate_rows = gate[:, :3 * H_NSA].reshape(b, n_tok, 3, H_NSA).transpose(0, 3, 1, 2).reshape(b, rows, 3)
    gate_rows = jnp.pad(gate_rows, ((0, 0), (0, 0), (0, LANES - 3)))
    pad_new = lambda a: jnp.pad(a.reshape(b, n_tok, KV_ROW), ((0, 0), (0, 8 - n_tok), (0, 0)))

    kvw_new_t = kvw.reshape(b, n_tok, 2, KVH * DH).transpose(0, 2, 3, 1)
    kvw_new_t = jnp.pad(kvw_new_t, ((0, 0), (0, 0), (0, 0), (LANES - n_tok, 0)))

    o_cmp, sel_rows = _cmp_attn_sample(_pages_native(cache_c), page_table, w1_pairs, cmp_bias, w2_heads,
                                       q_blk, gate_rows, n_tok)
    o_rows, win_t = _slc_win_sample(_pages_native(cache_s), page_table, q_blk, sel_rows, pad_new(kvs),
                                    _pages_native(cache_w), pad_new(kvw), kvw_new_t, gate_rows, o_cmp, n_tok)
    o_nsa = o_rows.reshape(b, H_NSA, n_tok, DH).transpose(0, 2, 1, 3).reshape(n, NSA_WIDTH)

    y_gla, s_new = _gla_sample(gq, gk, la, gv, zg, state, gla_gain, b, n_tok)
    y = _outproj(x.reshape(n, D_MODEL), o_nsa, zn, y_gla, w_out, out_gain, tm=n)
    kv5 = lambda a: a.reshape(b, n_tok, 2, KVH, DH)
    win_new = win_t.reshape(b, 2, KVH, DH, -1).transpose(0, 4, 1, 2, 3)
    return (y.reshape(b, n_tok, D_MODEL), kv5(kvc), kv5(kvs), win_new, s_new)


def kernel(x_prompt, x_sample, cache_cmp_kv, cache_slc_kv, cache_win_kv, state_gla, page_table,
           norm_in_gain, w_in, cmp_pe, cmp_w1, cmp_b1, cmp_w2, gla_wa2, gla_ba, gla_norm_gain,
           w_out, norm_out_gain):
    assert w_in.shape[0] == 1, "single-layer step"
    wts = (norm_in_gain[0], _pack_w_in(w_in[0]), _pad_wa2(gla_wa2[0]), gla_ba[0],
           _cmp_w1_big(cmp_w1[0]), _cmp_bias(cmp_pe[0], cmp_w1[0], cmp_b1[0]), _cmp_w2_big(cmp_w2[0]),
           gla_norm_gain[0], w_out[0].astype(bf16), norm_out_gain)
    yp, cmp_p, slc_p, win_p, gla_p = _prompt_path(x_prompt, wts, *_pack_w_in_prompt(w_in[0]))
    ys, cmp_s, slc_s, win_s, gla_s = _sample_path(
        x_sample, cache_cmp_kv[0], cache_slc_kv[0], cache_win_kv[0], state_gla[0], page_table, wts,
        _cmp_w1_pairs(cmp_w1[0]), _cmp_w2_heads(cmp_w2[0]))
    return (yp, ys, cmp_p[None], cmp_s[None], slc_p[None], slc_s[None], win_p[None], win_s[None],
            gla_p[None], gla_s[None])
```

```python
import functools

import numpy as np
import jax
import jax.numpy as jnp
from jax import lax
from jax.experimental import pallas as pl
from jax.experimental.pallas import tpu as pltpu

f32 = jnp.float32
bf16 = jnp.bfloat16

D_MODEL = 1024
DH = 64
H_NSA = 8
KVH = 2
G_NSA = 4
NSA_WIDTH = H_NSA * DH
CMP_LEN = 32
CMP_STRIDE = 16
CMP_HID = 128
SLC_LEN = 64
SLC_TOP = 16
WINDOW = 512
ROT_HALF = 8
ROPE_THETA = 500000.0
FORCE_SCORE = 1.0e4
H_GLA = 4
DK = 64
DV = 128
GLA_WIDTH = H_GLA * DV
GLA_LR = 16
GLA_TAU = 16.0
GLA_CHUNK = 64
GLA_SUB = 16
EPS = 1e-6
NEG = -1.0e30
LOG2E = 1.4426950408889634
PAGE_SIZE = 128
KV_ROW = 2 * KVH * DH

IN_SIZES = (H_NSA * DH, KV_ROW, KV_ROW, KV_ROW, 3 * H_NSA, NSA_WIDTH,
            H_GLA * DK, H_GLA * DK, H_GLA * DV, GLA_LR, GLA_WIDTH)
IN_OFFSETS = [0] + [int(v) for v in np.cumsum(IN_SIZES)]

LANES = 128
VMEM_LIMIT = 56 * 1024 * 1024

P_Q, P_KVC, P_KVS, P_KVW, P_ZN, P_GQ, P_GK, P_GV, P_ZG, P_MISC = (
    0, 512, 768, 1024, 1280, 1792, 2048, 2304, 2816, 3328)
D_PACK = P_MISC + LANES
MISC_GATE = 0
MISC_GLR = 32


def _cparams(sem):
    return pltpu.CompilerParams(dimension_semantics=sem, vmem_limit_bytes=VMEM_LIMIT)


def _sigmoid(x):
    return 1.0 / (1.0 + jnp.exp(-x))


def _silu(x):
    return x * _sigmoid(x)


def _log_sigmoid(x):
    return jnp.minimum(x, 0.0) - jnp.log1p(jnp.exp(-jnp.abs(x)))


def _dot(a, b):
    return jnp.dot(a, b, preferred_element_type=f32)


def _dot_nt(a, b):
    return lax.dot_general(a, b, (((1,), (1,)), ((), ())), preferred_element_type=f32)


def _dot_tn(a, b):
    return lax.dot_general(a, b, (((0,), (0,)), ((), ())), preferred_element_type=f32)


def _rope_tables(pos):
    n = pos.shape[0]
    inv = ROPE_THETA ** (-(jnp.arange(ROT_HALF, dtype=f32) / ROT_HALF))
    ang = pos.astype(f32)[:, None] * inv[None, :]
    cos, sin = jnp.cos(ang), jnp.sin(ang)
    z8 = jnp.zeros((n, ROT_HALF), f32)
    rest = DH - 2 * ROT_HALF
    c64 = jnp.concatenate([cos, cos, jnp.ones((n, rest), f32)], axis=-1)
    sa64 = jnp.concatenate([-sin, z8, jnp.zeros((n, rest), f32)], axis=-1)
    sb64 = jnp.concatenate([z8, sin, jnp.zeros((n, rest), f32)], axis=-1)
    tile = lambda t: jnp.concatenate([t, t], axis=-1)
    return tile(c64), tile(sa64), tile(sb64)


def _rope128(v, cos, sa, sb):
    return v * cos + pltpu.roll(v, LANES - ROT_HALF, 1) * sa + pltpu.roll(v, ROT_HALF, 1) * sb


def _inproj_kernel(x_ref, g_ref, w_ref, cos_ref, sa_ref, sb_ref, wa2_ref, ba_ref,
                   q_ref, kvc_ref, kvs_ref, kvw_ref, ks_ref, vs_ref, kw_ref, vw_ref,
                   gate_ref, zn_ref, gq_ref, gk_ref, gv_ref, la_ref, zg_ref):
    x = x_ref[...]
    ms = jnp.mean(x * x, axis=-1, keepdims=True)
    hn = (x * lax.rsqrt(ms + EPS) * g_ref[...]).astype(bf16)
    cos, sa, sb = cos_ref[...], sa_ref[...], sb_ref[...]

    def proj(off, width):
        return _dot(hn, w_ref[:, off:off + width])

    qp = proj(P_Q, NSA_WIDTH)
    for c in range(NSA_WIDTH // LANES):
        r = _rope128(qp[:, c * LANES:(c + 1) * LANES], cos, sa, sb) * (DH ** -0.5)
        q_ref[0, 2 * c] = r[:, :DH].astype(bf16)
        q_ref[0, 2 * c + 1] = r[:, DH:].astype(bf16)

    kvc_ref[...] = proj(P_KVC, KV_ROW)

    for off, kv_ref, k_ref, v_ref in ((P_KVS, kvs_ref, ks_ref, vs_ref), (P_KVW, kvw_ref, kw_ref, vw_ref)):
        p = proj(off, KV_ROW)
        k = _rope128(p[:, :LANES], cos, sa, sb)
        v = p[:, LANES:]
        kv_ref[:, :LANES] = k
        kv_ref[:, LANES:] = v
        for h in range(KVH):
            k_ref[0, h] = k[:, h * DH:(h + 1) * DH].astype(bf16)
            v_ref[0, h] = v[:, h * DH:(h + 1) * DH].astype(bf16)

    zn_ref[...] = _silu(proj(P_ZN, NSA_WIDTH))
    gq_ref[...] = proj(P_GQ, H_GLA * DK) * (DK ** -0.5)
    gk_ref[...] = proj(P_GK, H_GLA * DK)
    gv_ref[...] = proj(P_GV, H_GLA * DV)
    zg_ref[...] = _silu(proj(P_ZG, GLA_WIDTH))

    misc = proj(P_MISC, LANES)
    gate_ref[...] = _sigmoid(misc)
    xa = _dot(misc.astype(bf16), wa2_ref[...]) + ba_ref[...]
    la_ref[...] = _log_sigmoid(xa) / GLA_TAU


def _pack_w_in(w_in):
    o = IN_OFFSETS
    seg = lambda i: w_in[:, o[i]:o[i + 1]]
    misc = jnp.zeros((D_MODEL, LANES), w_in.dtype)
    misc = misc.at[:, MISC_GATE:MISC_GATE + 3 * H_NSA].set(seg(4))
    misc = misc.at[:, MISC_GLR:MISC_GLR + GLA_LR].set(seg(9))
    cols = [seg(0), seg(1), seg(2), seg(3), seg(5), seg(6), seg(7), seg(8), seg(10), misc]
    return jnp.concatenate(cols, axis=1).astype(bf16)


def _pad_wa2(wa2):
    pad = jnp.zeros((LANES, H_GLA * DK), wa2.dtype).at[MISC_GLR:MISC_GLR + GLA_LR].set(wa2)
    return pad.astype(bf16)


def _inproj(x, pos, norm_g, w_pack, wa2_pad, ba, tm):
    bk, lk, _ = x.shape
    n = bk * lk
    nt = lk // tm
    cos, sa, sb = _rope_tables(pos)
    tok = lambda w: pl.BlockSpec((tm, w), lambda i: (i, 0))
    tab = pl.BlockSpec((tm, LANES), lambda i: (i % nt, 0))
    full = lambda a: pl.BlockSpec(a.shape, lambda i: (0,) * a.ndim)
    hm = lambda h: pl.BlockSpec((1, h, tm, DH), lambda i: (i // nt, 0, i % nt, 0))
    g2 = norm_g.reshape(1, D_MODEL)
    ba2 = ba.reshape(1, H_GLA * DK)
    out_shape = (
        jax.ShapeDtypeStruct((bk, H_NSA, lk, DH), bf16),
        jax.ShapeDtypeStruct((n, KV_ROW), f32),
        jax.ShapeDtypeStruct((n, KV_ROW), f32),
        jax.ShapeDtypeStruct((n, KV_ROW), f32),
        jax.ShapeDtypeStruct((bk, KVH, lk, DH), bf16),
        jax.ShapeDtypeStruct((bk, KVH, lk, DH), bf16),
        jax.ShapeDtypeStruct((bk, KVH, lk, DH), bf16),
        jax.ShapeDtypeStruct((bk, KVH, lk, DH), bf16),
        jax.ShapeDtypeStruct((n, LANES), f32),
        jax.ShapeDtypeStruct((n, NSA_WIDTH), f32),
        jax.ShapeDtypeStruct((n, H_GLA * DK), f32),
        jax.ShapeDtypeStruct((n, H_GLA * DK), f32),
        jax.ShapeDtypeStruct((n, H_GLA * DV), f32),
        jax.ShapeDtypeStruct((n, H_GLA * DK), f32),
        jax.ShapeDtypeStruct((n, GLA_WIDTH), f32),
    )
    out_specs = (hm(H_NSA), tok(KV_ROW), tok(KV_ROW), tok(KV_ROW), hm(KVH), hm(KVH), hm(KVH), hm(KVH),
                 tok(LANES), tok(NSA_WIDTH), tok(H_GLA * DK), tok(H_GLA * DK), tok(H_GLA * DV),
                 tok(H_GLA * DK), tok(GLA_WIDTH))
    return pl.pallas_call(
        _inproj_kernel,
        grid=(n // tm,),
        in_specs=[tok(D_MODEL), full(g2), full(w_pack), tab, tab, tab, full(wa2_pad), full(ba2)],
        out_specs=out_specs,
        out_shape=out_shape,
        compiler_params=_cparams(("parallel",)),
        name="inproj",
    )(x.reshape(n, D_MODEL), g2, w_pack, cos, sa, sb, wa2_pad, ba2)


PT_Q, PT_KVC, PT_KVS, PT_KVW, PT_GATE = 0, 512, 768, 1024, 1280
PT_ROWS = PT_GATE + 32
PK_ZN, PK_GQ, PK_GK, PK_GV, PK_ZG, PK_MISC = 0, 512, 768, 1024, 1536, 2048
PK_COLS = PK_MISC + LANES


def _inproj_prompt_kernel(x_ref, g_ref, wt_ref, w_ref, cos_ref, sin_ref, wa2_ref, ba_ref,
                          qt_ref, kvct_ref, kvst_ref, kvwt_ref, ks_ref, kw_ref, vst_ref, vwt_ref, kvc_ref,
                          gatet_ref, zn_ref, gq_ref, gk_ref, gv_ref, la_ref, zg_ref):
    x = x_ref[...]
    ms = jnp.mean(x * x, axis=-1, keepdims=True)
    hn = (x * lax.rsqrt(ms + EPS) * g_ref[...]).astype(bf16)
    cos_t, sin_t = cos_ref[...], sin_ref[...]

    def proj_t(off, rows):
        return _dot_nt(wt_ref[off:off + rows, :], hn)

    def rope_rows(v):
        x1, x2 = v[:ROT_HALF], v[ROT_HALF:2 * ROT_HALF]
        return jnp.concatenate([x1 * cos_t - x2 * sin_t, x2 * cos_t + x1 * sin_t, v[2 * ROT_HALF:]], axis=0)

    q_t = proj_t(PT_Q, NSA_WIDTH)
    for h in range(H_NSA):
        qt_ref[0, h] = (rope_rows(q_t[h * DH:(h + 1) * DH]) * (DH ** -0.5 * LOG2E)).astype(bf16)

    kvc_t = proj_t(PT_KVC, KV_ROW)
    kvct_ref[0] = kvc_t
    kvc_ref[...] = kvc_t.T.astype(bf16)

    for off, kvt_ref, k_ref, vt_ref in ((PT_KVS, kvst_ref, ks_ref, vst_ref), (PT_KVW, kvwt_ref, kw_ref, vwt_ref)):
        t = proj_t(off, KV_ROW)
        k_t = jnp.concatenate([rope_rows(t[h * DH:(h + 1) * DH]) for h in range(KVH)], axis=0)
        kvt_ref[0, :LANES] = k_t
        kvt_ref[0, LANES:] = t[LANES:]
        k_tok = k_t.T
        for h in range(KVH):
            k_ref[0, h] = k_tok[:, h * DH:(h + 1) * DH].astype(bf16)
            vt_ref[0, h] = t[LANES + h * DH:LANES + (h + 1) * DH].astype(bf16)

    gatet_ref[0] = _sigmoid(proj_t(PT_GATE, PT_ROWS - PT_GATE))

    def proj(off, width):
        return _dot(hn, w_ref[:, off:off + width])

    zn_ref[...] = _silu(proj(PK_ZN, NSA_WIDTH))
    gq_ref[...] = proj(PK_GQ, H_GLA * DK) * (DK ** -0.5)
    gk_ref[...] = proj(PK_GK, H_GLA * DK)
    gv_ref[...] = proj(PK_GV, H_GLA * DV)
    zg_ref[...] = _silu(proj(PK_ZG, GLA_WIDTH))
    misc = proj(PK_MISC, LANES)
    xa = _dot(misc.astype(bf16), wa2_ref[...]) + ba_ref[...]
    la_ref[...] = _log_sigmoid(xa) / GLA_TAU


def _pack_w_in_prompt(w_in):
    o = IN_OFFSETS
    seg = lambda i: w_in[:, o[i]:o[i + 1]]
    gate_t = jnp.zeros((PT_ROWS - PT_GATE, D_MODEL), w_in.dtype).at[:3 * H_NSA].set(seg(4).T)
    w_t = jnp.concatenate([seg(0).T, seg(1).T, seg(2).T, seg(3).T, gate_t], axis=0).astype(bf16)
    misc = jnp.zeros((D_MODEL, LANES), w_in.dtype).at[:, MISC_GLR:MISC_GLR + GLA_LR].set(seg(9))
    w_tok = jnp.concatenate([seg(5), seg(6), seg(7), seg(8), seg(10), misc], axis=1).astype(bf16)
    return w_t, w_tok


def _inproj_prompt(x, norm_g, w_t, w_tok, wa2_pad, ba, tm):
    b, lq, _ = x.shape
    n = b * lq
    nt = lq // tm
    inv = ROPE_THETA ** (-(jnp.arange(ROT_HALF, dtype=f32) / ROT_HALF))
    ang = inv[:, None] * jnp.arange(lq).astype(f32)[None, :]
    cos_t, sin_t = jnp.cos(ang), jnp.sin(ang)
    tok = lambda w: pl.BlockSpec((tm, w), lambda i: (i, 0))
    tab = pl.BlockSpec((ROT_HALF, tm), lambda i: (0, i % nt))
    full = lambda a: pl.BlockSpec(a.shape, lambda i: (0,) * a.ndim)
    feat = lambda r: pl.BlockSpec((1, r, tm), lambda i: (i // nt, 0, i % nt))
    headf = lambda h: pl.BlockSpec((1, h, DH, tm), lambda i: (i // nt, 0, 0, i % nt))
    headt = lambda h: pl.BlockSpec((1, h, tm, DH), lambda i: (i // nt, 0, i % nt, 0))
    g2 = norm_g.reshape(1, D_MODEL)
    ba2 = ba.reshape(1, H_GLA * DK)
    sds = jax.ShapeDtypeStruct
    out_shape = (
        sds((b, H_NSA, DH, lq), bf16),
        sds((b, KV_ROW, lq), f32), sds((b, KV_ROW, lq), f32), sds((b, KV_ROW, lq), f32),
        sds((b, KVH, lq, DH), bf16), sds((b, KVH, lq, DH), bf16),
        sds((b, KVH, DH, lq), bf16), sds((b, KVH, DH, lq), bf16),
        sds((n, KV_ROW), bf16),
        sds((b, PT_ROWS - PT_GATE, lq), f32),
        sds((n, NSA_WIDTH), f32), sds((n, H_GLA * DK), f32), sds((n, H_GLA * DK), f32),
        sds((n, H_GLA * DV), f32), sds((n, H_GLA * DK), f32), sds((n, GLA_WIDTH), f32),
    )
    out_specs = (headf(H_NSA), feat(KV_ROW), feat(KV_ROW), feat(KV_ROW), headt(KVH), headt(KVH),
                 headf(KVH), headf(KVH), tok(KV_ROW), feat(PT_ROWS - PT_GATE),
                 tok(NSA_WIDTH), tok(H_GLA * DK), tok(H_GLA * DK), tok(H_GLA * DV), tok(H_GLA * DK),
                 tok(GLA_WIDTH))
    return pl.pallas_call(
        _inproj_prompt_kernel,
        grid=(n // tm,),
        in_specs=[tok(D_MODEL), full(g2), full(w_t), full(w_tok), tab, tab, full(wa2_pad), full(ba2)],
        out_specs=out_specs,
        out_shape=out_shape,
        compiler_params=_cparams(("parallel",)),
        name="inproj_prompt",
    )(x.reshape(n, D_MODEL), g2, w_t, w_tok, cos_t, sin_t, wa2_pad, ba2)


CHUNK_W = CMP_STRIDE * KV_ROW
PART_W = 2 * 2 * KVH * CMP_HID
HID_W = 2 * KVH * CMP_HID


def _cmp_w1_big(cmp_w1):
    w1r = cmp_w1.reshape(2, CMP_LEN // CMP_STRIDE, CMP_STRIDE, DH, CMP_HID)
    eye = jnp.eye(2, dtype=cmp_w1.dtype)
    big = jnp.einsum('crsdh,cC,kK->sCKdrckh', w1r, eye, eye)
    return big.reshape(CHUNK_W, PART_W).astype(bf16)


def _cmp_w2_big(cmp_w2):
    eye = jnp.eye(2, dtype=cmp_w2.dtype)
    big = jnp.einsum('chd,cC,kK->ckhCKd', cmp_w2, eye, eye)
    return big.reshape(HID_W, KV_ROW).astype(bf16)


def _cmp_part_kernel(x_ref, w_ref, o_ref):
    o_ref[...] = _dot(x_ref[...].astype(bf16), w_ref[...])


def _cmp_part(chunks, w1_big, tr):
    r = chunks.shape[0]
    return pl.pallas_call(
        _cmp_part_kernel,
        grid=(r // tr,),
        in_specs=[pl.BlockSpec((tr, CHUNK_W), lambda i: (i, 0)),
                  pl.BlockSpec((CHUNK_W, PART_W), lambda i: (0, 0))],
        out_specs=pl.BlockSpec((tr, PART_W), lambda i: (i, 0)),
        out_shape=jax.ShapeDtypeStruct((r, PART_W), f32),
        compiler_params=_cparams(("parallel",)),
        name="cmp_part",
    )(chunks, w1_big)


def _cmp_bias_kernel(pe_ref, w1_ref, b1_ref, o_ref):
    for c in range(2):
        o_ref[c] = _dot(pe_ref[c], w1_ref[c]) + b1_ref[c]


def _cmp_bias(cmp_pe, cmp_w1, cmp_b1):
    pe = jnp.broadcast_to(cmp_pe.reshape(2, 1, CMP_LEN * DH), (2, 8, CMP_LEN * DH))
    b1 = jnp.broadcast_to(cmp_b1.reshape(2, 1, CMP_HID), (2, 8, CMP_HID))
    out = pl.pallas_call(
        _cmp_bias_kernel,
        out_shape=jax.ShapeDtypeStruct((2, 8, CMP_HID), f32),
        name="cmp_bias",
    )(pe, cmp_w1, b1)
    row = out[:, 0, :]
    return jnp.broadcast_to(row[:, None, :], (2, KVH, CMP_HID)).reshape(1, HID_W)


def _compress_finish(part, bias, w2_big, cos, sa, sb):
    n = part.shape[0]
    pre = part[:, :HID_W] + pltpu.roll(part[:, HID_W:], n - 1, 0)
    h = _silu(pre + bias)
    out = _dot(h.astype(bf16), w2_big)
    return _rope128(out[:, :LANES], cos, sa, sb), out[:, LANES:]


def _masked_softmax(s, mask, axis, exp=jnp.exp):
    s = jnp.where(mask, s, NEG)
    e = exp(s - jnp.max(s, axis=axis, keepdims=True))
    return jnp.where(mask, e * (1.0 / jnp.sum(e, axis=axis, keepdims=True)), 0.0)


def _split_bf16(x):
    hi = x.astype(bf16)
    return hi, (x - hi.astype(f32)).astype(bf16)


def _topk_mask(score, k, axis):
    n = score.shape[axis]
    idx = lax.broadcasted_iota(jnp.int32, score.shape, axis)
    sel = jnp.zeros(score.shape, f32)
    for _ in range(k):
        m = jnp.max(score, axis=axis, keepdims=True)
        first = jnp.min(jnp.where(score == m, idx, n), axis=axis, keepdims=True)
        pick = idx == first
        sel = jnp.where(pick, 1.0, sel)
        score = jnp.where(pick, NEG, score)
    return sel


def _topk_rows_by_rank(score, k):
    r, n = score.shape
    cols = jnp.concatenate([score, jnp.zeros((n - r, n), f32)], axis=0).T
    earlier = jnp.where(lax.broadcasted_iota(jnp.int32, (n, n), 0) < lax.broadcasted_iota(jnp.int32, (n, n), 1),
                        1.0, 0.0)
    rows = []
    for c in range(r):
        col, row = cols[:, c:c + 1], score[c:c + 1, :]
        before = jnp.where(col > row, 1.0, jnp.where(col == row, earlier, 0.0))
        rows.append(jnp.where(jnp.sum(before, axis=0, keepdims=True) < k, 1.0, 0.0))
    return jnp.concatenate(rows, axis=0)


def _cover_t(n_slc, n_chunk):
    start = np.arange(n_chunk)[None, :] * CMP_STRIDE
    j = np.arange(n_slc)[:, None]
    cov = (start < (j + 1) * SLC_LEN) & (start + CMP_LEN > j * SLC_LEN)
    return jnp.asarray(cov.astype(np.float32), dtype=bf16)


def _cmp_attn_kernel(part_ref, bias_ref, w2_ref, cos_ref, sa_ref, sb_ref, qt_ref, gatet_ref, cov_ref,
                     ot_ref, selt_ref, kc_ref, vct_ref, *, tq):
    t = pl.program_id(1)

    @pl.when(t == 0)
    def _():
        k, v = _compress_finish(part_ref[0], bias_ref[...], w2_ref[...], cos_ref[...], sa_ref[...], sb_ref[...])
        v_t = v.T
        for h in range(KVH):
            kc_ref[h] = k[:, h * DH:(h + 1) * DH].astype(bf16)
            vct_ref[h] = v_t[h * DH:(h + 1) * DH].astype(bf16)

    nc = kc_ref.shape[1]
    nb = cov_ref.shape[0]
    qpos = t * tq + lax.broadcasted_iota(jnp.int32, (1, tq), 1)
    end_pos = lax.broadcasted_iota(jnp.int32, (nc, 1), 0) * CMP_STRIDE + (CMP_LEN - 1)
    mask_t = end_pos <= qpos
    gate_t = gatet_ref[0]
    cov = cov_ref[...]

    jblk = lax.broadcasted_iota(jnp.int32, (nb, tq), 0)
    qblk = qpos // SLC_LEN
    valid = jblk <= qblk
    forced = (jblk == 0) | (jblk == qblk) | (jblk == qblk - 1)

    scores = []
    for kh in range(KVH):
        kc, vc_t = kc_ref[kh], vct_ref[kh]
        psum = jnp.zeros((nc, tq), f32)
        for g in range(G_NSA):
            h = kh * G_NSA + g
            p_t = _masked_softmax(_dot(kc, qt_ref[0, h]), mask_t, axis=0, exp=jnp.exp2)
            ot_ref[0, h * DH:(h + 1) * DH] = _dot(vc_t, p_t.astype(bf16)) * gate_t[h:h + 1]
            psum = psum + p_t
        hi, lo = _split_bf16(psum)
        imp = _dot(cov, hi) + _dot(cov, lo)
        scores.append(jnp.where(valid, jnp.where(forced, FORCE_SCORE, imp), -1.0))
    picked = _topk_mask(jnp.concatenate(scores, axis=1), SLC_TOP, axis=0)
    for kh in range(KVH):
        sel_t = jnp.where(valid & (picked[:, kh * tq:(kh + 1) * tq] > 0.5), 0.0, NEG)
        if nb < LANES:
            sel_t = jnp.concatenate([sel_t, jnp.full((LANES - nb, tq), NEG, f32)], axis=0)
        selt_ref[0, kh] = sel_t


def _cmp_attn(part, bias, w2_big, q_t, gate_t, tq):
    b, nc, _ = part.shape
    lq = q_t.shape[3]
    n_slc = lq // SLC_LEN
    end_pos = jnp.arange(nc) * CMP_STRIDE + (CMP_LEN - 1)
    cos, sa, sb = _rope_tables(end_pos)
    cov = _cover_t(n_slc, nc)
    full = lambda a: pl.BlockSpec(a.shape, lambda i, t: (0,) * a.ndim)
    return pl.pallas_call(
        functools.partial(_cmp_attn_kernel, tq=tq),
        grid=(b, lq // tq),
        in_specs=[pl.BlockSpec((1, nc, PART_W), lambda i, t: (i, 0, 0)),
                  full(bias), full(w2_big), full(cos), full(sa), full(sb),
                  pl.BlockSpec((1, H_NSA, DH, tq), lambda i, t: (i, 0, 0, t)),
                  pl.BlockSpec((1, gate_t.shape[1], tq), lambda i, t: (i, 0, t)),
                  full(cov)],
        out_specs=(pl.BlockSpec((1, NSA_WIDTH, tq), lambda i, t: (i, 0, t)),
                   pl.BlockSpec((1, KVH, LANES, tq), lambda i, t: (i, 0, 0, t))),
        out_shape=(jax.ShapeDtypeStruct((b, NSA_WIDTH, lq), f32),
                   jax.ShapeDtypeStruct((b, KVH, LANES, lq), f32)),
        scratch_shapes=[pltpu.VMEM((KVH, nc, DH), bf16), pltpu.VMEM((KVH, DH, nc), bf16)],
        compiler_params=_cparams(("parallel", "arbitrary")),
        name="cmp_attn",
    )(part, bias, w2_big, cos, sa, sb, q_t, gate_t, cov)


def _block_expand(n_slc_pad, n_keys):
    e = (np.arange(n_keys)[None, :] // SLC_LEN) == np.arange(n_slc_pad)[:, None]
    return jnp.asarray(e.astype(np.float32), dtype=bf16)


def _mask_bias_t(allowed):
    bias = jnp.where(allowed, 0.0, NEG)
    return jnp.concatenate([bias] * G_NSA, axis=1)


def _sublane_fold(x, op):
    return functools.reduce(op, [x[r:r + 8] for r in range(0, x.shape[0], 8)])


def _slc_win_kernel(qt_ref, ks_ref, vst_ref, kw_ref, vwt_ref, selt_ref, gatet_ref, ocmpt_ref,
                    o_ref, s_a, s_b, m_a, m_b, ws_a, ws_b, wm_a, wm_b, acc_ref, wacc_ref, *, tq, tk, nt):
    t = pl.program_id(2)

    @pl.when(t == 0)
    def _():
        m_b[...] = jnp.full(m_b.shape, NEG, f32)
        ws_b[...] = jnp.zeros(ws_b.shape, f32)
        wm_b[...] = jnp.zeros(wm_b.shape, f32)

    step = functools.partial(_slc_win_step, qt_ref, ks_ref, vst_ref, kw_ref, vwt_ref, selt_ref, gatet_ref,
                             ocmpt_ref, o_ref, acc_ref, wacc_ref, tq=tq, tk=tk, nt=nt)

    @pl.when(t % 2 == 0)
    def _():
        step(s_a, s_b, m_a, m_b, ws_a, ws_b, wm_a, wm_b)

    @pl.when(t % 2 == 1)
    def _():
        step(s_b, s_a, m_b, m_a, ws_b, ws_a, wm_b, wm_a)


ONES_ROWS = 16


def _slc_win_step(qt_ref, ks_ref, vst_ref, kw_ref, vwt_ref, selt_ref, gatet_ref, ocmpt_ref, o_ref,
                  acc_ref, wacc_ref, s_cur, s_prev, m_cur, m_last, ws_cur, ws_prev, wm_cur, wm_last,
                  *, tq, tk, nt):
    kh = pl.program_id(1)
    t = pl.program_id(2)
    lq = ks_ref.shape[2]
    band = WINDOW + tq
    blocks_per_tile = tk // SLC_LEN
    key_tiles = lambda tile: (tile * tq + tq + tk - 1) // tk
    n_score = jnp.where(t < nt, key_tiles(t), 0)
    n_apply = jnp.where(t >= 1, key_tiles(t - 1), 0)
    n_plain = jnp.where(t < nt, (t * tq) // tk, 0)
    q0 = t * tq
    qpos = q0 + lax.broadcasted_iota(jnp.int32, (1, tq), 1)
    q_t = jnp.concatenate([qt_ref[0, g] for g in range(G_NSA)], axis=1)
    ones = jnp.ones((ONES_ROWS, tk), bf16)

    m_cur[...] = jnp.full(m_cur.shape, NEG, f32)
    m_prev = jnp.max(m_last[...], axis=0, keepdims=True)
    acc_ref[...] = jnp.zeros(acc_ref.shape, f32)

    def score(kt, causal):
        k0 = pl.multiple_of(kt * tk, tk)
        j0 = pl.multiple_of(kt * blocks_per_tile, blocks_per_tile)
        sel = selt_ref[0, 0, pl.ds(j0, blocks_per_tile), :]
        bias = jnp.concatenate([jnp.broadcast_to(sel[j:j + 1], (SLC_LEN, tq)) for j in range(blocks_per_tile)],
                               axis=0)
        if causal:
            kpos = k0 + lax.broadcasted_iota(jnp.int32, (tk, 1), 0)
            bias = jnp.where(kpos <= qpos, bias, NEG)
        s = _dot(ks_ref[0, 0, pl.ds(k0, tk), :], q_t) + jnp.concatenate([bias] * G_NSA, axis=1)
        s_cur[pl.ds(k0, tk), :] = s
        m_cur[...] = jnp.maximum(m_cur[...], _sublane_fold(s, jnp.maximum))

    def apply(kt):
        k0 = pl.multiple_of(kt * tk, tk)
        p = jnp.exp2(s_prev[pl.ds(k0, tk), :] - m_prev).astype(bf16)
        acc_ref[...] += _dot(jnp.concatenate([vst_ref[0, 0, :, pl.ds(k0, tk)], ones], axis=0), p)

    def both_body(kt, carry):
        score(kt, causal=False)
        apply(kt)
        return carry

    def score_body(kt, carry):
        score(kt, causal=True)
        return carry

    def apply_body(kt, carry):
        apply(kt)
        return carry

    n_both = jnp.minimum(n_plain, n_apply)
    lax.fori_loop(0, n_both, both_body, 0)
    lax.fori_loop(n_both, n_score, score_body, 0)
    lax.fori_loop(n_both, n_apply, apply_body, 0)

    w0 = pl.multiple_of(jnp.clip(q0 - WINDOW, 0, lq - band), tq)
    kpos = w0 + lax.broadcasted_iota(jnp.int32, (band, 1), 0)
    s = _dot(kw_ref[0, 0, pl.ds(w0, band), :], q_t) + _mask_bias_t((kpos <= qpos) & (kpos > qpos - WINDOW))
    ws_cur[...] = s
    wm_cur[...] = _sublane_fold(s, jnp.maximum)
    w0_prev = pl.multiple_of(jnp.clip(q0 - tq - WINDOW, 0, lq - band), tq)
    p = jnp.exp2(ws_prev[...] - jnp.max(wm_last[...], axis=0, keepdims=True)).astype(bf16)
    wacc_ref[...] = _dot(jnp.concatenate([vwt_ref[0, 0, :, pl.ds(w0_prev, band)],
                                          jnp.ones((ONES_ROWS, band), bf16)], axis=0), p)

    @pl.when(t >= 1)
    def _():
        o_slc = acc_ref[:DH] / acc_ref[DH:DH + 1]
        o_win = wacc_ref[:DH] / wacc_ref[DH:DH + 1]
        gate_t = gatet_ref[0]
        row = lambda c, g: jnp.where(kh == 0, gate_t[c * H_NSA + g:c * H_NSA + g + 1],
                                     gate_t[c * H_NSA + G_NSA + g:c * H_NSA + G_NSA + g + 1])
        g_slc = jnp.concatenate([row(1, g) for g in range(G_NSA)], axis=1)
        g_win = jnp.concatenate([row(2, g) for g in range(G_NSA)], axis=1)
        o_t = g_slc * o_slc + g_win * o_win
        o_tok = jnp.concatenate([o_t, jnp.zeros((LANES - DH, o_t.shape[1]), f32)], axis=0).T
        ocmp_tok = ocmpt_ref[0].T
        o_ref[0] = ocmp_tok + jnp.concatenate([o_tok[g * tq:(g + 1) * tq, :DH] for g in range(G_NSA)], axis=1)


def _slc_win(q_t, ks, vs_t, kw, vw_t, sel_t, gate_t, o_cmp_t, tq, tk):
    b, _, _, lq = q_t.shape
    assert lq >= WINDOW + tq and lq % tk == 0 and tk % (8 * SLC_LEN) == 0
    gw = G_NSA * DH
    k_spec = pl.BlockSpec((1, 1, lq, DH), lambda i, k, t: (i, k, 0, 0))
    vt_spec = pl.BlockSpec((1, 1, DH, lq), lambda i, k, t: (i, k, 0, 0))
    lanes = G_NSA * tq
    band = WINDOW + tq
    nt = lq // tq
    scored = lambda t: jnp.minimum(t, nt - 1)
    drained = lambda t: jnp.maximum(t - 1, 0)
    return pl.pallas_call(
        functools.partial(_slc_win_kernel, tq=tq, tk=tk, nt=nt),
        grid=(b, KVH, nt + 1),
        in_specs=[pl.BlockSpec((1, G_NSA, DH, tq), lambda i, k, t: (i, k, 0, scored(t))),
                  k_spec, vt_spec, k_spec, vt_spec,
                  pl.BlockSpec((1, 1, LANES, tq), lambda i, k, t: (i, k, 0, scored(t))),
                  pl.BlockSpec((1, gate_t.shape[1], tq), lambda i, k, t: (i, 0, drained(t))),
                  pl.BlockSpec((1, gw, tq), lambda i, k, t: (i, k, drained(t)))],
        out_specs=pl.BlockSpec((1, tq, gw), lambda i, k, t: (i, drained(t), k)),
        out_shape=jax.ShapeDtypeStruct((b, lq, NSA_WIDTH), f32),
        scratch_shapes=[pltpu.VMEM((lq, lanes), f32), pltpu.VMEM((lq, lanes), f32),
                        pltpu.VMEM((8, lanes), f32), pltpu.VMEM((8, lanes), f32),
                        pltpu.VMEM((band, lanes), f32), pltpu.VMEM((band, lanes), f32),
                        pltpu.VMEM((8, lanes), f32), pltpu.VMEM((8, lanes), f32),
                        pltpu.VMEM((DH + ONES_ROWS, lanes), f32),
                        pltpu.VMEM((DH + ONES_ROWS, lanes), f32)],
        compiler_params=_cparams(("parallel", "parallel", "arbitrary")),
        name="slc_win",
    )(q_t, ks, vs_t, kw, vw_t, sel_t, gate_t, o_cmp_t)


def _rms_gain(o, gain):
    return o * lax.rsqrt(jnp.mean(o * o, axis=-1, keepdims=True) + EPS) * gain


def _gla_kernel(q_ref, k_ref, v_ref, la_ref, z_ref, gain_ref, y_ref, st_ref, s_scr, *, tl):
    t = pl.program_id(2)

    @pl.when(t == 0)
    def _():
        s_scr[...] = jnp.zeros(s_scr.shape, f32)

    c = GLA_CHUNK
    row = lax.broadcasted_iota(jnp.int32, (c, c), 0)
    col = lax.broadcasted_iota(jnp.int32, (c, c), 1)
    causal = col <= row
    tril = jnp.where(causal, 1.0, 0.0).astype(bf16)
    lane = lax.broadcasted_iota(jnp.int32, (1, LANES), 1)
    head_mask = [jnp.where(lane < DK, 1.0, 0.0), jnp.where(lane >= DK, 1.0, 0.0)]
    gain = gain_ref[...]

    chunks = []
    for ci in range(tl // c):
        rows = slice(ci * c, (ci + 1) * c)
        q, k, la = q_ref[rows, :], k_ref[rows, :], la_ref[rows, :]
        hi, rest = la.astype(bf16), la - la.astype(bf16).astype(f32)
        mid, lo = rest.astype(bf16), (rest - rest.astype(bf16).astype(f32)).astype(bf16)
        bc = _dot(tril, hi) + _dot(tril, mid) + _dot(tril, lo)
        ref_row = bc[c // 2 - 1:c // 2, :]
        b_last = bc[c - 1:c, :]
        q_a = q * jnp.exp(bc - ref_row)
        k_a = (k * jnp.exp(ref_row - bc)).astype(bf16)
        q_s = q * jnp.exp(bc)
        k_s = k * jnp.exp(b_last - bc)
        o_intra, q_inter, add = [], [], None
        for h in range(2):
            mh = head_mask[h]
            vh = v_ref[rows, h * DV:(h + 1) * DV].astype(bf16)
            a = jnp.where(causal, _dot_nt((q_a * mh).astype(bf16), k_a), 0.0)
            o_intra.append(_dot(a.astype(bf16), vh))
            q_inter.append((q_s * mh).astype(bf16))
            u = _dot_tn(vh, (k_s * mh).astype(bf16))
            add = u if add is None else add + u
        chunks.append((rows, jnp.exp(b_last), add, o_intra, q_inter))

    st = s_scr[...]
    for rows, decay, add, o_intra, q_inter in chunks:
        st_b = st.astype(bf16)
        for h in range(2):
            o = o_intra[h] + _dot_nt(q_inter[h], st_b)
            y_ref[rows, h * DV:(h + 1) * DV] = (_rms_gain(o, gain)
                                                * z_ref[rows, h * DV:(h + 1) * DV]).astype(y_ref.dtype)
        st = st * decay + add
    s_scr[...] = st
    st_ref[0, 0] = st


def _gla(gq, gk, gv, la, zg, gain, b, tl):
    n = gq.shape[0]
    nt = n // b // tl
    pairs = H_GLA // 2
    qk = pl.BlockSpec((tl, LANES), lambda i, p, t: (i * nt + t, p))
    vz = pl.BlockSpec((tl, 2 * DV), lambda i, p, t: (i * nt + t, p))
    gain2 = gain.reshape(1, DV)
    y, st = pl.pallas_call(
        functools.partial(_gla_kernel, tl=tl),
        grid=(b, pairs, nt),
        in_specs=[qk, qk, vz, qk, vz, pl.BlockSpec((1, DV), lambda i, p, t: (0, 0))],
        out_specs=(vz, pl.BlockSpec((1, 1, DV, LANES), lambda i, p, t: (i, p, 0, 0))),
        out_shape=(jax.ShapeDtypeStruct((n, GLA_WIDTH), bf16),
                   jax.ShapeDtypeStruct((b, pairs, DV, LANES), f32)),
        scratch_shapes=[pltpu.VMEM((DV, LANES), f32)],
        compiler_params=_cparams(("parallel", "parallel", "arbitrary")),
        name="gla",
    )(gq, gk, gv, la, zg, gain2)
    state = st.reshape(b, pairs, DV, 2, DK).transpose(0, 1, 3, 4, 2).reshape(b, H_GLA, DK, DV)
    return y, state


def _outproj_kernel(x_ref, on_ref, zn_ref, yg_ref, w_ref, g_ref, y_ref):
    y_nsa = (on_ref[...] * zn_ref[...]).astype(bf16)
    mix = _dot(y_nsa, w_ref[:NSA_WIDTH, :]) + _dot(yg_ref[...].astype(bf16), w_ref[NSA_WIDTH:, :])
    y_ref[...] = _rms_gain(x_ref[...] + mix, g_ref[...])


def _outproj(x, o_nsa, zn, y_gla, w_out, gain, tm):
    n = x.shape[0]
    tok = lambda w: pl.BlockSpec((tm, w), lambda i: (i, 0))
    g2 = gain.reshape(1, D_MODEL)
    return pl.pallas_call(
        _outproj_kernel,
        grid=(n // tm,),
        in_specs=[tok(D_MODEL), tok(NSA_WIDTH), tok(NSA_WIDTH), tok(GLA_WIDTH),
                  pl.BlockSpec(w_out.shape, lambda i: (0, 0)), pl.BlockSpec(g2.shape, lambda i: (0, 0))],
        out_specs=tok(D_MODEL),
        out_shape=jax.ShapeDtypeStruct((n, D_MODEL), f32),
        compiler_params=_cparams(("parallel",)),
        name="outproj",
    )(x, o_nsa, zn, y_gla, w_out, g2)


def _prompt_path(x, wts, w_t, w_tok):
    (norm_g, _, wa2_pad, ba, w1_big, cmp_bias, w2_big, gla_gain, w_out, out_gain) = wts
    b, lq, _ = x.shape
    (q_t, kvc_t, kvs_t, kvw_t, ks, kw, vs_t, vw_t, kvc_tok, gate_t, zn, gq, gk, gv, la, zg) = _inproj_prompt(
        x, norm_g, w_t, w_tok, wa2_pad, ba, tm=512)
    n_chunk = lq // CMP_STRIDE
    part = _cmp_part(kvc_tok.reshape(b * n_chunk, CHUNK_W), w1_big, tr=n_chunk)
    o_cmp_t, sel_t = _cmp_attn(part.reshape(b, n_chunk, PART_W), cmp_bias, w2_big, q_t, gate_t, tq=512)
    o_nsa = _slc_win(q_t, ks, vs_t, kw, vw_t, sel_t, gate_t, o_cmp_t, tq=256, tk=512)
    y_gla, state = _gla(gq, gk, gv, la, zg, gla_gain, b, tl=512)
    y = _outproj(x.reshape(b * lq, D_MODEL), o_nsa.reshape(b * lq, NSA_WIDTH), zn, y_gla, w_out, out_gain, tm=512)
    kv5 = lambda a: a.reshape(b, 2, KVH, DH, -1).transpose(0, 4, 1, 2, 3)
    wlen = min(WINDOW, lq)
    return (y.reshape(b, lq, D_MODEL), kv5(kvc_t), kv5(kvs_t), kv5(kvw_t[:, :, lq - wlen:]), state)


def _pages_native(cache):
    n_pool = cache.shape[0]
    return cache.transpose(0, 2, 3, 4, 1).reshape(n_pool, 2, KVH * DH, cache.shape[1])


def _page_gather(make_copies):
    i = pl.program_id(0)
    slot = i % 2

    @pl.when(i == 0)
    def _():
        for c in make_copies(0, 0):
            c.start()

    @pl.when(i + 1 < pl.num_programs(0))
    def _():
        for c in make_copies(i + 1, 1 - slot):
            c.start()

    for c in make_copies(i, slot):
        c.wait()
    return slot


def _cmp_w1_pairs(cmp_w1):
    ratio = CMP_LEN // CMP_STRIDE
    w1r = cmp_w1.reshape(2, ratio, CMP_STRIDE // 2, 2, DH, CMP_HID)
    eye = jnp.eye(KVH, dtype=cmp_w1.dtype)
    big = jnp.einsum('crpjdh,kK->cpjKdrkh', w1r, eye)
    return big.reshape(2, CMP_STRIDE // 2, 2 * KVH * DH, ratio * KVH * CMP_HID).astype(bf16)


def _cmp_w2_heads(cmp_w2):
    eye = jnp.eye(KVH, dtype=cmp_w2.dtype)
    return jnp.einsum('chd,kK->ckhKd', cmp_w2, eye).reshape(2, KVH * CMP_HID, KVH * DH).astype(bf16)


def _group_sum_matrix(n_tok):
    r = np.arange(KVH * n_tok)[:, None]
    c = np.arange(H_NSA * n_tok)[None, :]
    m = ((c // (G_NSA * n_tok)) == (r // n_tok)) & ((c % n_tok) == (r % n_tok))
    return m.astype(np.float32)


def _cmp_attn_sample_kernel(pt_ref, cache_ref, perm_ref, w1_ref, bias_ref, w2_ref, cos_ref, sa_ref, sb_ref, q_ref,
                            gate_ref, cov_ref, gs_ref, gst_ref, o_ref, sel_ref, buf, rows_scr, sem,
                            *, n_pages, n_tok, past_len):
    def copies(seq, slot):
        return [pltpu.make_async_copy(cache_ref.at[pt_ref[seq, p]], buf.at[slot, p], sem.at[slot])
                for p in range(n_pages)]

    slot = _page_gather(copies)

    cpp = PAGE_SIZE // CMP_STRIDE
    perm = perm_ref[...]

    group = 4

    def to_rows(g, carry):
        for u in range(group):
            p = g * group + u
            r0 = pl.multiple_of(p * cpp, cpp)
            for c in range(2):
                t = _dot_nt(perm, buf[slot, p, c].astype(bf16))
                for s in range(CMP_STRIDE):
                    rows_scr[c, s, pl.ds(r0, cpp), :] = t[s * cpp:(s + 1) * cpp]
        return carry

    lax.fori_loop(0, n_pages // group, to_rows, 0)

    nc = n_pages * cpp
    kv = []
    for c in range(2):
        part = jnp.zeros((nc, w1_ref.shape[-1]), f32)
        for pr in range(CMP_STRIDE // 2):
            lhs = jnp.concatenate([rows_scr[c, 2 * pr], rows_scr[c, 2 * pr + 1]], axis=-1)
            part = part + _dot(lhs.astype(bf16), w1_ref[c, pr])
        half = part.shape[1] // 2
        pre = part[:, :half] + pltpu.roll(part[:, half:], nc - 1, 0)
        hid = _silu(pre + bias_ref[:, c * half:(c + 1) * half])
        kv.append(_dot(hid.astype(bf16), w2_ref[c]))
    k = _rope128(kv[0], cos_ref[...], sa_ref[...], sb_ref[...])
    v = kv[1]
    rows = H_NSA * n_tok
    qpos = past_len + lax.broadcasted_iota(jnp.int32, (rows, 1), 0) % n_tok
    end_c = lax.broadcasted_iota(jnp.int32, (1, nc), 1) * CMP_STRIDE + (CMP_LEN - 1)
    mask = end_c <= qpos
    p = _masked_softmax(_dot_nt(q_ref[0], k.astype(bf16)), mask, axis=-1)
    o_ref[0] = _dot(p.astype(bf16), v.astype(bf16)) * gate_ref[0][:, 0:1]
    hi, lo = _split_bf16(p)
    gs = gs_ref[...]
    ph, pl_ = _split_bf16(_dot(gs, hi) + _dot(gs, lo))
    imp = _dot(ph, cov_ref[...]) + _dot(pl_, cov_ref[...])
    n_blk = imp.shape[1]
    jblk = lax.broadcasted_iota(jnp.int32, imp.shape, 1)
    forced = (jblk == 0) | (jblk == n_blk - 1)
    sel = _topk_rows_by_rank(jnp.where(forced, FORCE_SCORE, imp), SLC_TOP - 1)
    sel_ref[0] = _dot(gst_ref[...], sel.astype(bf16)).astype(bf16)


def _cmp_attn_sample(cache_t, page_table, w1_pairs, bias, w2_heads, q_blk, gate_rows, n_tok):
    b, n_pages = page_table.shape
    past_len = n_pages * PAGE_SIZE
    nc = past_len // CMP_STRIDE
    rows = H_NSA * n_tok
    n_blk = past_len // SLC_LEN
    end_pos = jnp.arange(nc) * CMP_STRIDE + (CMP_LEN - 1)
    cos, sa, sb = _rope_tables(end_pos)
    cov = _cover_t(n_blk, nc).T
    gs = jnp.asarray(_group_sum_matrix(n_tok), dtype=bf16)
    gst = gs.T
    r = np.arange(PAGE_SIZE)
    cpp = PAGE_SIZE // CMP_STRIDE
    perm = jnp.asarray((r[None, :] == (r[:, None] % cpp) * CMP_STRIDE + r[:, None] // cpp).astype(np.float32),
                       dtype=bf16)
    full = lambda a: pl.BlockSpec(a.shape, lambda i, pt: (0,) * a.ndim)
    seq = lambda a: pl.BlockSpec((1,) + a.shape[1:], lambda i, pt: (i,) + (0,) * (a.ndim - 1))
    grid_spec = pltpu.PrefetchScalarGridSpec(
        num_scalar_prefetch=1,
        grid=(b,),
        in_specs=[pl.BlockSpec(memory_space=pl.ANY), full(perm), full(w1_pairs), full(bias), full(w2_heads),
                  full(cos), full(sa), full(sb), seq(q_blk), seq(gate_rows), full(cov), full(gs), full(gst)],
        out_specs=(pl.BlockSpec((1, rows, LANES), lambda i, pt: (i, 0, 0)),
                   pl.BlockSpec((1, rows, n_blk), lambda i, pt: (i, 0, 0))),
        scratch_shapes=[pltpu.VMEM((2, n_pages) + cache_t.shape[1:], f32),
                        pltpu.VMEM((2, CMP_STRIDE, nc, KVH * DH), f32),
                        pltpu.SemaphoreType.DMA((2,))],
    )
    return pl.pallas_call(
        functools.partial(_cmp_attn_sample_kernel, n_pages=n_pages, n_tok=n_tok, past_len=past_len),
        grid_spec=grid_spec,
        out_shape=(jax.ShapeDtypeStruct((b, rows, LANES), f32),
                   jax.ShapeDtypeStruct((b, rows, n_blk), bf16)),
        compiler_params=_cparams(("arbitrary",)),
        name="cmp_attn_sample",
    )(page_table, cache_t, perm, w1_pairs, bias, w2_heads, cos, sa, sb, q_blk, gate_rows, cov, gs, gst)


def _slc_win_sample_kernel(pt_ref, cache_ref, q_ref, sel_ref, e_ref, snew_ref, cw_ref, wnew_ref, wnewt_ref,
                           gate_ref, ocmp_ref, o_ref, wout_ref, buf, sem, *, n_pages, n_tok, win_off):
    def copies(seq, slot):
        return [pltpu.make_async_copy(cache_ref.at[pt_ref[seq, p], c],
                                      buf.at[slot, c, :, pl.ds(p * PAGE_SIZE, PAGE_SIZE)], sem.at[slot])
                for p in range(n_pages) for c in range(2)]

    slot = _page_gather(copies)
    rows = H_NSA * n_tok
    q = q_ref[0]
    tok = lax.broadcasted_iota(jnp.int32, (rows, 1), 0) % n_tok
    new_i = lax.broadcasted_iota(jnp.int32, (1, snew_ref.shape[1]), 1)
    new_ok = (new_i <= tok) & (new_i < n_tok)

    def attend(keys_t, vals_t, allowed, new_ref):
        s = jnp.where(allowed, _dot(q, keys_t), NEG)
        k_new = new_ref[0][:, :LANES].astype(bf16)
        v_new = new_ref[0][:, LANES:].astype(bf16)
        s_new = jnp.where(new_ok, _dot_nt(q, k_new), NEG)
        m = jnp.maximum(jnp.max(s, axis=-1, keepdims=True), jnp.max(s_new, axis=-1, keepdims=True))
        p = jnp.where(allowed, jnp.exp(s - m), 0.0)
        p_new = jnp.where(new_ok, jnp.exp(s_new - m), 0.0)
        l = jnp.sum(p, axis=-1, keepdims=True) + jnp.sum(p_new, axis=-1, keepdims=True)
        return (_dot_nt(p.astype(bf16), vals_t) + _dot(p_new.astype(bf16), v_new)) / l

    picked = _dot(sel_ref[0], e_ref[...]) > 0.5
    o_slc = attend(buf[slot, 0].astype(bf16), buf[slot, 1].astype(bf16), picked, snew_ref)

    wbuf = cw_ref.shape[-1]
    win_i = lax.broadcasted_iota(jnp.int32, (1, wbuf), 1)
    o_win = attend(cw_ref[0, 0].astype(bf16), cw_ref[0, 1].astype(bf16), win_i > tok + win_off, wnew_ref)

    lane = lax.broadcasted_iota(jnp.int32, (1, LANES), 1)
    for c in range(2):
        shifted = pltpu.roll(cw_ref[0, c], wbuf - n_tok, 1)
        tail = jnp.where(lane >= LANES - n_tok, wnewt_ref[0, c], shifted[:, wbuf - LANES:])
        wout_ref[0, c] = jnp.concatenate([shifted[:, :wbuf - LANES], tail], axis=-1)

    gate = gate_ref[0]
    o = ocmp_ref[0] + gate[:, 1:2] * o_slc + gate[:, 2:3] * o_win
    second_kvh = lax.broadcasted_iota(jnp.int32, (rows, 1), 0) >= G_NSA * n_tok
    o_ref[0] = jnp.where(second_kvh, o[:, DH:], o[:, :DH])


def _slc_win_sample(cache_t, page_table, q_blk, sel_rows, kvs_new, win_t, kvw_new, kvw_new_t, gate_rows,
                    o_cmp, n_tok):
    b, n_pages = page_table.shape
    rows = H_NSA * n_tok
    past_len = n_pages * PAGE_SIZE
    wbuf = win_t.shape[-1]
    e = _block_expand(sel_rows.shape[-1], past_len)
    full = lambda a: pl.BlockSpec(a.shape, lambda i, pt: (0,) * a.ndim)
    seq = lambda a: pl.BlockSpec((1,) + a.shape[1:], lambda i, pt: (i,) + (0,) * (a.ndim - 1))
    grid_spec = pltpu.PrefetchScalarGridSpec(
        num_scalar_prefetch=1,
        grid=(b,),
        in_specs=[pl.BlockSpec(memory_space=pl.ANY), seq(q_blk), seq(sel_rows), full(e), seq(kvs_new),
                  seq(win_t), seq(kvw_new), seq(kvw_new_t), seq(gate_rows), seq(o_cmp)],
        out_specs=(pl.BlockSpec((1, rows, DH), lambda i, pt: (i, 0, 0)), seq(win_t)),
        scratch_shapes=[pltpu.VMEM((2, 2, KVH * DH, past_len), f32), pltpu.SemaphoreType.DMA((2,))],
    )
    return pl.pallas_call(
        functools.partial(_slc_win_sample_kernel, n_pages=n_pages, n_tok=n_tok, win_off=wbuf - WINDOW),
        grid_spec=grid_spec,
        out_shape=(jax.ShapeDtypeStruct((b, rows, DH), f32), jax.ShapeDtypeStruct(win_t.shape, f32)),
        compiler_params=_cparams(("arbitrary",)),
        name="slc_win_sample",
    )(page_table, cache_t, q_blk, sel_rows, e, kvs_new, win_t, kvw_new, kvw_new_t, gate_rows, o_cmp)


def _gla_sample_kernel(qka_ref, v_ref, z_ref, s_ref, gain_ref, y_ref, so_ref, *, n_tok):
    gain = gain_ref[...]
    y_ref[...] = jnp.zeros(y_ref.shape, f32)
    for i in range(qka_ref.shape[0]):
        for h in range(H_GLA):
            qka = qka_ref[i, h]
            s = s_ref[i, h]
            for t in range(n_tok):
                q_t = qka[:, t:t + 1]
                k_t = qka[:, n_tok + t:n_tok + t + 1]
                a_t = jnp.exp(qka[:, 2 * n_tok + t:2 * n_tok + t + 1])
                s = a_t * s + k_t * v_ref[i, t:t + 1, h * DV:(h + 1) * DV]
                o = jnp.sum(q_t * s, axis=0, keepdims=True)
                y_ref[i, t:t + 1, h * DV:(h + 1) * DV] = (_rms_gain(o, gain)
                                                          * z_ref[i, t:t + 1, h * DV:(h + 1) * DV])
            so_ref[i, h] = s


def _gla_sample(gq, gk, la, gv, zg, state, gain, b, n_tok):
    def cols(a):
        return a.reshape(b, n_tok, H_GLA, DK).transpose(0, 2, 3, 1)
    qka = jnp.concatenate([cols(gq), cols(gk), cols(la)], axis=-1)
    qka = jnp.pad(qka, ((0, 0), (0, 0), (0, 0), (0, LANES - 3 * n_tok)))
    pad_rows = lambda a: jnp.pad(a.reshape(b, n_tok, -1), ((0, 0), (0, 8 - n_tok), (0, 0)))
    v8, z8 = pad_rows(gv), pad_rows(zg)
    gain2 = gain.reshape(1, DV)
    per_step = 4 if b % 4 == 0 else 1
    seq = lambda a: pl.BlockSpec((per_step,) + a.shape[1:], lambda i: (i,) + (0,) * (a.ndim - 1))
    y, s_new = pl.pallas_call(
        functools.partial(_gla_sample_kernel, n_tok=n_tok),
        grid=(b // per_step,),
        in_specs=[seq(qka), seq(v8), seq(z8), seq(state), pl.BlockSpec((1, DV), lambda i: (0, 0))],
        out_specs=(seq(v8), seq(state)),
        out_shape=(jax.ShapeDtypeStruct(v8.shape, f32), jax.ShapeDtypeStruct(state.shape, f32)),
        compiler_params=_cparams(("parallel",)),
        name="gla_sample",
    )(qka, v8, z8, state, gain2)
    return y[:, :n_tok].reshape(b * n_tok, GLA_WIDTH), s_new


def _sample_path(x, cache_c, cache_s, cache_w, state, page_table, wts, w1_pairs, w2_heads):
    (norm_g, w_pack, wa2_pad, ba, w1_big, cmp_bias, w2_big, gla_gain, w_out, out_gain) = wts
    b, n_tok, _ = x.shape
    n = b * n_tok
    n_pages = page_table.shape[1]
    past_len = n_pages * PAGE_SIZE
    assert n_tok <= 8 and past_len % SLC_LEN == 0 and past_len // SLC_LEN <= LANES
    pos = past_len + jnp.arange(n) % n_tok
    (q_hm, kvc, kvs, kvw, _, _, _, _, gate, zn, gq, gk, gv, la, zg) = _inproj(
        x.reshape(1, n, D_MODEL), pos, norm_g, w_pack, wa2_pad, ba, tm=n)

    rows = H_NSA * n_tok
    q_rows = q_hm[0].reshape(H_NSA, b, n_tok, DH).transpose(1, 0, 2, 3)
    zero = jnp.zeros_like(q_rows[:, :G_NSA])
    q_blk = jnp.concatenate([jnp.concatenate([q_rows[:, :G_NSA], zero], axis=-1),
                             jnp.concatenate([zero, q_rows[:, G_NSA:]], axis=-1)], axis=1).reshape(b, rows, LANES)
    gate_rows = gate[:, :3 * H_NSA].reshape(b, n_tok, 3, H_NSA).transpose(0, 3, 1, 2).reshape(b, rows, 3)
    gate_rows = jnp.pad(gate_rows, ((0, 0), (0, 0), (0, LANES - 3)))
    pad_new = lambda a: jnp.pad(a.reshape(b, n_tok, KV_ROW), ((0, 0), (0, 8 - n_tok), (0, 0)))

    kvw_new_t = kvw.reshape(b, n_tok, 2, KVH * DH).transpose(0, 2, 3, 1)
    kvw_new_t = jnp.pad(kvw_new_t, ((0, 0), (0, 0), (0, 0), (LANES - n_tok, 0)))

    o_cmp, sel_rows = _cmp_attn_sample(_pages_native(cache_c), page_table, w1_pairs, cmp_bias, w2_heads,
                                       q_blk, gate_rows, n_tok)
    o_rows, win_t = _slc_win_sample(_pages_native(cache_s), page_table, q_blk, sel_rows, pad_new(kvs),
                                    _pages_native(cache_w), pad_new(kvw), kvw_new_t, gate_rows, o_cmp, n_tok)
    o_nsa = o_rows.reshape(b, H_NSA, n_tok, DH).transpose(0, 2, 1, 3).reshape(n, NSA_WIDTH)

    y_gla, s_new = _gla_sample(gq, gk, la, gv, zg, state, gla_gain, b, n_tok)
    y = _outproj(x.reshape(n, D_MODEL), o_nsa, zn, y_gla, w_out, out_gain, tm=n)
    kv5 = lambda a: a.reshape(b, n_tok, 2, KVH, DH)
    win_new = win_t.reshape(b, 2, KVH, DH, -1).transpose(0, 4, 1, 2, 3)
    return (y.reshape(b, n_tok, D_MODEL), kv5(kvc), kv5(kvs), win_new, s_new)


def kernel(x_prompt, x_sample, cache_cmp_kv, cache_slc_kv, cache_win_kv, state_gla, page_table,
           norm_in_gain, w_in, cmp_pe, cmp_w1, cmp_b1, cmp_w2, gla_wa2, gla_ba, gla_norm_gain,
           w_out, norm_out_gain):
    assert w_in.shape[0] == 1, "single-layer step"
    wts = (norm_in_gain[0], _pack_w_in(w_in[0]), _pad_wa2(gla_wa2[0]), gla_ba[0],
           _cmp_w1_big(cmp_w1[0]), _cmp_bias(cmp_pe[0], cmp_w1[0], cmp_b1[0]), _cmp_w2_big(cmp_w2[0]),
           gla_norm_gain[0], w_out[0].astype(bf16), norm_out_gain)
    yp, cmp_p, slc_p, win_p, gla_p = _prompt_path(x_prompt, wts, *_pack_w_in_prompt(w_in[0]))
    ys, cmp_s, slc_s, win_s, gla_s = _sample_path(
        x_sample, cache_cmp_kv[0], cache_slc_kv[0], cache_win_kv[0], state_gla[0], page_table, wts,
        _cmp_w1_pairs(cmp_w1[0]), _cmp_w2_heads(cmp_w2[0]))
    return (yp, ys, cmp_p[None], cmp_s[None], slc_p[None], slc_s[None], win_p[None], win_s[None],
            gla_p[None], gla_s[None])
```

```python
import functools

import numpy as np
import jax
import jax.numpy as jnp
from jax import lax
from jax.experimental import pallas as pl
from jax.experimental.pallas import tpu as pltpu

f32 = jnp.float32
bf16 = jnp.bfloat16

D_MODEL = 1024
DH = 64
H_NSA = 8
KVH = 2
G_NSA = 4
NSA_WIDTH = H_NSA * DH
CMP_LEN = 32
CMP_STRIDE = 16
CMP_HID = 128
SLC_LEN = 64
SLC_TOP = 16
WINDOW = 512
ROT_HALF = 8
ROPE_THETA = 500000.0
FORCE_SCORE = 1.0e4
H_GLA = 4
DK = 64
DV = 128
GLA_WIDTH = H_GLA * DV
GLA_LR = 16
GLA_TAU = 16.0
GLA_CHUNK = 64
GLA_SUB = 16
EPS = 1e-6
NEG = -1.0e30
LOG2E = 1.4426950408889634
PAGE_SIZE = 128
KV_ROW = 2 * KVH * DH

IN_SIZES = (H_NSA * DH, KV_ROW, KV_ROW, KV_ROW, 3 * H_NSA, NSA_WIDTH,
            H_GLA * DK, H_GLA * DK, H_GLA * DV, GLA_LR, GLA_WIDTH)
IN_OFFSETS = [0] + [int(v) for v in np.cumsum(IN_SIZES)]

LANES = 128
VMEM_LIMIT = 56 * 1024 * 1024

P_Q, P_KVC, P_KVS, P_KVW, P_ZN, P_GQ, P_GK, P_GV, P_ZG, P_MISC = (
    0, 512, 768, 1024, 1280, 1792, 2048, 2304, 2816, 3328)
D_PACK = P_MISC + LANES
MISC_GATE = 0
MISC_GLR = 32


def _cparams(sem):
    return pltpu.CompilerParams(dimension_semantics=sem, vmem_limit_bytes=VMEM_LIMIT)


def _sigmoid(x):
    return 1.0 / (1.0 + jnp.exp(-x))


def _silu(x):
    return x * _sigmoid(x)


def _log_sigmoid(x):
    return jnp.minimum(x, 0.0) - jnp.log1p(jnp.exp(-jnp.abs(x)))


def _dot(a, b):
    return jnp.dot(a, b, preferred_element_type=f32)


def _dot_nt(a, b):
    return lax.dot_general(a, b, (((1,), (1,)), ((), ())), preferred_element_type=f32)


def _dot_tn(a, b):
    return lax.dot_general(a, b, (((0,), (0,)), ((), ())), preferred_element_type=f32)


def _rope_tables(pos):
    n = pos.shape[0]
    inv = ROPE_THETA ** (-(jnp.arange(ROT_HALF, dtype=f32) / ROT_HALF))
    ang = pos.astype(f32)[:, None] * inv[None, :]
    cos, sin = jnp.cos(ang), jnp.sin(ang)
    z8 = jnp.zeros((n, ROT_HALF), f32)
    rest = DH - 2 * ROT_HALF
    c64 = jnp.concatenate([cos, cos, jnp.ones((n, rest), f32)], axis=-1)
    sa64 = jnp.concatenate([-sin, z8, jnp.zeros((n, rest), f32)], axis=-1)
    sb64 = jnp.concatenate([z8, sin, jnp.zeros((n, rest), f32)], axis=-1)
    tile = lambda t: jnp.concatenate([t, t], axis=-1)
    return tile(c64), tile(sa64), tile(sb64)


def _rope128(v, cos, sa, sb):
    return v * cos + pltpu.roll(v, LANES - ROT_HALF, 1) * sa + pltpu.roll(v, ROT_HALF, 1) * sb


def _inproj_kernel(x_ref, g_ref, w_ref, cos_ref, sa_ref, sb_ref, wa2_ref, ba_ref,
                   q_ref, kvc_ref, kvs_ref, kvw_ref, ks_ref, vs_ref, kw_ref, vw_ref,
                   gate_ref, zn_ref, gq_ref, gk_ref, gv_ref, la_ref, zg_ref):
    x = x_ref[...]
    ms = jnp.mean(x * x, axis=-1, keepdims=True)
    hn = (x * lax.rsqrt(ms + EPS) * g_ref[...]).astype(bf16)
    cos, sa, sb = cos_ref[...], sa_ref[...], sb_ref[...]

    def proj(off, width):
        return _dot(hn, w_ref[:, off:off + width])

    qp = proj(P_Q, NSA_WIDTH)
    for c in range(NSA_WIDTH // LANES):
        r = _rope128(qp[:, c * LANES:(c + 1) * LANES], cos, sa, sb) * (DH ** -0.5)
        q_ref[0, 2 * c] = r[:, :DH].astype(bf16)
        q_ref[0, 2 * c + 1] = r[:, DH:].astype(bf16)

    kvc_ref[...] = proj(P_KVC, KV_ROW)

    for off, kv_ref, k_ref, v_ref in ((P_KVS, kvs_ref, ks_ref, vs_ref), (P_KVW, kvw_ref, kw_ref, vw_ref)):
        p = proj(off, KV_ROW)
        k = _rope128(p[:, :LANES], cos, sa, sb)
        v = p[:, LANES:]
        kv_ref[:, :LANES] = k
        kv_ref[:, LANES:] = v
        for h in range(KVH):
            k_ref[0, h] = k[:, h * DH:(h + 1) * DH].astype(bf16)
            v_ref[0, h] = v[:, h * DH:(h + 1) * DH].astype(bf16)

    zn_ref[...] = _silu(proj(P_ZN, NSA_WIDTH))
    gq_ref[...] = proj(P_GQ, H_GLA * DK) * (DK ** -0.5)
    gk_ref[...] = proj(P_GK, H_GLA * DK)
    gv_ref[...] = proj(P_GV, H_GLA * DV)
    zg_ref[...] = _silu(proj(P_ZG, GLA_WIDTH))

    misc = proj(P_MISC, LANES)
    gate_ref[...] = _sigmoid(misc)
    xa = _dot(misc.astype(bf16), wa2_ref[...]) + ba_ref[...]
    la_ref[...] = _log_sigmoid(xa) / GLA_TAU


def _pack_w_in(w_in):
    o = IN_OFFSETS
    seg = lambda i: w_in[:, o[i]:o[i + 1]]
    misc = jnp.zeros((D_MODEL, LANES), w_in.dtype)
    misc = misc.at[:, MISC_GATE:MISC_GATE + 3 * H_NSA].set(seg(4))
    misc = misc.at[:, MISC_GLR:MISC_GLR + GLA_LR].set(seg(9))
    cols = [seg(0), seg(1), seg(2), seg(3), seg(5), seg(6), seg(7), seg(8), seg(10), misc]
    return jnp.concatenate(cols, axis=1).astype(bf16)


def _pad_wa2(wa2):
    pad = jnp.zeros((LANES, H_GLA * DK), wa2.dtype).at[MISC_GLR:MISC_GLR + GLA_LR].set(wa2)
    return pad.astype(bf16)


def _inproj(x, pos, norm_g, w_pack, wa2_pad, ba, tm):
    bk, lk, _ = x.shape
    n = bk * lk
    nt = lk // tm
    cos, sa, sb = _rope_tables(pos)
    tok = lambda w: pl.BlockSpec((tm, w), lambda i: (i, 0))
    tab = pl.BlockSpec((tm, LANES), lambda i: (i % nt, 0))
    full = lambda a: pl.BlockSpec(a.shape, lambda i: (0,) * a.ndim)
    hm = lambda h: pl.BlockSpec((1, h, tm, DH), lambda i: (i // nt, 0, i % nt, 0))
    g2 = norm_g.reshape(1, D_MODEL)
    ba2 = ba.reshape(1, H_GLA * DK)
    out_shape = (
        jax.ShapeDtypeStruct((bk, H_NSA, lk, DH), bf16),
        jax.ShapeDtypeStruct((n, KV_ROW), f32),
        jax.ShapeDtypeStruct((n, KV_ROW), f32),
        jax.ShapeDtypeStruct((n, KV_ROW), f32),
        jax.ShapeDtypeStruct((bk, KVH, lk, DH), bf16),
        jax.ShapeDtypeStruct((bk, KVH, lk, DH), bf16),
        jax.ShapeDtypeStruct((bk, KVH, lk, DH), bf16),
        jax.ShapeDtypeStruct((bk, KVH, lk, DH), bf16),
        jax.ShapeDtypeStruct((n, LANES), f32),
        jax.ShapeDtypeStruct((n, NSA_WIDTH), f32),
        jax.ShapeDtypeStruct((n, H_GLA * DK), f32),
        jax.ShapeDtypeStruct((n, H_GLA * DK), f32),
        jax.ShapeDtypeStruct((n, H_GLA * DV), f32),
        jax.ShapeDtypeStruct((n, H_GLA * DK), f32),
        jax.ShapeDtypeStruct((n, GLA_WIDTH), f32),
    )
    out_specs = (hm(H_NSA), tok(KV_ROW), tok(KV_ROW), tok(KV_ROW), hm(KVH), hm(KVH), hm(KVH), hm(KVH),
                 tok(LANES), tok(NSA_WIDTH), tok(H_GLA * DK), tok(H_GLA * DK), tok(H_GLA * DV),
                 tok(H_GLA * DK), tok(GLA_WIDTH))
    return pl.pallas_call(
        _inproj_kernel,
        grid=(n // tm,),
        in_specs=[tok(D_MODEL), full(g2), full(w_pack), tab, tab, tab, full(wa2_pad), full(ba2)],
        out_specs=out_specs,
        out_shape=out_shape,
        compiler_params=_cparams(("parallel",)),
        name="inproj",
    )(x.reshape(n, D_MODEL), g2, w_pack, cos, sa, sb, wa2_pad, ba2)


PT_Q, PT_KVC, PT_KVS, PT_KVW, PT_GATE = 0, 512, 768, 1024, 1280
PT_ROWS = PT_GATE + 32
PK_ZN, PK_GQ, PK_GK, PK_GV, PK_ZG, PK_MISC = 0, 512, 768, 1024, 1536, 2048
PK_COLS = PK_MISC + LANES


def _inproj_prompt_kernel(x_ref, g_ref, wt_ref, w_ref, cos_ref, sin_ref, wa2_ref, ba_ref,
                          qt_ref, kvct_ref, kvst_ref, kvwt_ref, ks_ref, kw_ref, vst_ref, vwt_ref, kvc_ref,
                          gatet_ref, zn_ref, gq_ref, gk_ref, gv_ref, la_ref, zg_ref):
    x = x_ref[...]
    ms = jnp.mean(x * x, axis=-1, keepdims=True)
    hn = (x * lax.rsqrt(ms + EPS) * g_ref[...]).astype(bf16)
    cos_t, sin_t = cos_ref[...], sin_ref[...]

    def proj_t(off, rows):
        return _dot_nt(wt_ref[off:off + rows, :], hn)

    def rope_rows(v):
        x1, x2 = v[:ROT_HALF], v[ROT_HALF:2 * ROT_HALF]
        return jnp.concatenate([x1 * cos_t - x2 * sin_t, x2 * cos_t + x1 * sin_t, v[2 * ROT_HALF:]], axis=0)

    q_t = proj_t(PT_Q, NSA_WIDTH)
    for h in range(H_NSA):
        qt_ref[0, h] = (rope_rows(q_t[h * DH:(h + 1) * DH]) * (DH ** -0.5 * LOG2E)).astype(bf16)

    kvc_t = proj_t(PT_KVC, KV_ROW)
    kvct_ref[0] = kvc_t
    kvc_ref[...] = kvc_t.T.astype(bf16)

    for off, kvt_ref, k_ref, vt_ref in ((PT_KVS, kvst_ref, ks_ref, vst_ref), (PT_KVW, kvwt_ref, kw_ref, vwt_ref)):
        t = proj_t(off, KV_ROW)
        k_t = jnp.concatenate([rope_rows(t[h * DH:(h + 1) * DH]) for h in range(KVH)], axis=0)
        kvt_ref[0, :LANES] = k_t
        kvt_ref[0, LANES:] = t[LANES:]
        k_tok = k_t.T
        for h in range(KVH):
            k_ref[0, h] = k_tok[:, h * DH:(h + 1) * DH].astype(bf16)
            vt_ref[0, h] = t[LANES + h * DH:LANES + (h + 1) * DH].astype(bf16)

    gatet_ref[0] = _sigmoid(proj_t(PT_GATE, PT_ROWS - PT_GATE))

    def proj(off, width):
        return _dot(hn, w_ref[:, off:off + width])

    zn_ref[...] = _silu(proj(PK_ZN, NSA_WIDTH))
    gq_ref[...] = proj(PK_GQ, H_GLA * DK) * (DK ** -0.5)
    gk_ref[...] = proj(PK_GK, H_GLA * DK)
    gv_ref[...] = proj(PK_GV, H_GLA * DV)
    zg_ref[...] = _silu(proj(PK_ZG, GLA_WIDTH))
    misc = proj(PK_MISC, LANES)
    xa = _dot(misc.astype(bf16), wa2_ref[...]) + ba_ref[...]
    la_ref[...] = _log_sigmoid(xa) / GLA_TAU


def _pack_w_in_prompt(w_in):
    o = IN_OFFSETS
    seg = lambda i: w_in[:, o[i]:o[i + 1]]
    gate_t = jnp.zeros((PT_ROWS - PT_GATE, D_MODEL), w_in.dtype).at[:3 * H_NSA].set(seg(4).T)
    w_t = jnp.concatenate([seg(0).T, seg(1).T, seg(2).T, seg(3).T, gate_t], axis=0).astype(bf16)
    misc = jnp.zeros((D_MODEL, LANES), w_in.dtype).at[:, MISC_GLR:MISC_GLR + GLA_LR].set(seg(9))
    w_tok = jnp.concatenate([seg(5), seg(6), seg(7), seg(8), seg(10), misc], axis=1).astype(bf16)
    return w_t, w_tok


def _inproj_prompt(x, norm_g, w_t, w_tok, wa2_pad, ba, tm):
    b, lq, _ = x.shape
    n = b * lq
    nt = lq // tm
    inv = ROPE_THETA ** (-(jnp.arange(ROT_HALF, dtype=f32) / ROT_HALF))
    ang = inv[:, None] * jnp.arange(lq).astype(f32)[None, :]
    cos_t, sin_t = jnp.cos(ang), jnp.sin(ang)
    tok = lambda w: pl.BlockSpec((tm, w), lambda i: (i, 0))
    tab = pl.BlockSpec((ROT_HALF, tm), lambda i: (0, i % nt))
    full = lambda a: pl.BlockSpec(a.shape, lambda i: (0,) * a.ndim)
    feat = lambda r: pl.BlockSpec((1, r, tm), lambda i: (i // nt, 0, i % nt))
    headf = lambda h: pl.BlockSpec((1, h, DH, tm), lambda i: (i // nt, 0, 0, i % nt))
    headt = lambda h: pl.BlockSpec((1, h, tm, DH), lambda i: (i // nt, 0, i % nt, 0))
    g2 = norm_g.reshape(1, D_MODEL)
    ba2 = ba.reshape(1, H_GLA * DK)
    sds = jax.ShapeDtypeStruct
    out_shape = (
        sds((b, H_NSA, DH, lq), bf16),
        sds((b, KV_ROW, lq), f32), sds((b, KV_ROW, lq), f32), sds((b, KV_ROW, lq), f32),
        sds((b, KVH, lq, DH), bf16), sds((b, KVH, lq, DH), bf16),
        sds((b, KVH, DH, lq), bf16), sds((b, KVH, DH, lq), bf16),
        sds((n, KV_ROW), bf16),
        sds((b, PT_ROWS - PT_GATE, lq), f32),
        sds((n, NSA_WIDTH), f32), sds((n, H_GLA * DK), f32), sds((n, H_GLA * DK), f32),
        sds((n, H_GLA * DV), f32), sds((n, H_GLA * DK), f32), sds((n, GLA_WIDTH), f32),
    )
    out_specs = (headf(H_NSA), feat(KV_ROW), feat(KV_ROW), feat(KV_ROW), headt(KVH), headt(KVH),
                 headf(KVH), headf(KVH), tok(KV_ROW), feat(PT_ROWS - PT_GATE),
                 tok(NSA_WIDTH), tok(H_GLA * DK), tok(H_GLA * DK), tok(H_GLA * DV), tok(H_GLA * DK),
                 tok(GLA_WIDTH))
    return pl.pallas_call(
        _inproj_prompt_kernel,
        grid=(n // tm,),
        in_specs=[tok(D_MODEL), full(g2), full(w_t), full(w_tok), tab, tab, full(wa2_pad), full(ba2)],
        out_specs=out_specs,
        out_shape=out_shape,
        compiler_params=_cparams(("parallel",)),
        name="inproj_prompt",
    )(x.reshape(n, D_MODEL), g2, w_t, w_tok, cos_t, sin_t, wa2_pad, ba2)


CHUNK_W = CMP_STRIDE * KV_ROW
PART_W = 2 * 2 * KVH * CMP_HID
HID_W = 2 * KVH * CMP_HID


def _cmp_w1_big(cmp_w1):
    w1r = cmp_w1.reshape(2, CMP_LEN // CMP_STRIDE, CMP_STRIDE, DH, CMP_HID)
    eye = jnp.eye(2, dtype=cmp_w1.dtype)
    big = jnp.einsum('crsdh,cC,kK->sCKdrckh', w1r, eye, eye)
    return big.reshape(CHUNK_W, PART_W).astype(bf16)


def _cmp_w2_big(cmp_w2):
    eye = jnp.eye(2, dtype=cmp_w2.dtype)
    big = jnp.einsum('chd,cC,kK->ckhCKd', cmp_w2, eye, eye)
    return big.reshape(HID_W, KV_ROW).astype(bf16)


def _cmp_part_kernel(x_ref, w_ref, o_ref):
    o_ref[...] = _dot(x_ref[...].astype(bf16), w_ref[...])


def _cmp_part(chunks, w1_big, tr):
    r = chunks.shape[0]
    return pl.pallas_call(
        _cmp_part_kernel,
        grid=(r // tr,),
        in_specs=[pl.BlockSpec((tr, CHUNK_W), lambda i: (i, 0)),
                  pl.BlockSpec((CHUNK_W, PART_W), lambda i: (0, 0))],
        out_specs=pl.BlockSpec((tr, PART_W), lambda i: (i, 0)),
        out_shape=jax.ShapeDtypeStruct((r, PART_W), f32),
        compiler_params=_cparams(("parallel",)),
        name="cmp_part",
    )(chunks, w1_big)


def _cmp_bias_kernel(pe_ref, w1_ref, b1_ref, o_ref):
    for c in range(2):
        o_ref[c] = _dot(pe_ref[c], w1_ref[c]) + b1_ref[c]


def _cmp_bias(cmp_pe, cmp_w1, cmp_b1):
    pe = jnp.broadcast_to(cmp_pe.reshape(2, 1, CMP_LEN * DH), (2, 8, CMP_LEN * DH))
    b1 = jnp.broadcast_to(cmp_b1.reshape(2, 1, CMP_HID), (2, 8, CMP_HID))
    out = pl.pallas_call(
        _cmp_bias_kernel,
        out_shape=jax.ShapeDtypeStruct((2, 8, CMP_HID), f32),
        name="cmp_bias",
    )(pe, cmp_w1, b1)
    row = out[:, 0, :]
    return jnp.broadcast_to(row[:, None, :], (2, KVH, CMP_HID)).reshape(1, HID_W)


def _compress_finish(part, bias, w2_big, cos, sa, sb):
    n = part.shape[0]
    pre = part[:, :HID_W] + pltpu.roll(part[:, HID_W:], n - 1, 0)
    h = _silu(pre + bias)
    out = _dot(h.astype(bf16), w2_big)
    return _rope128(out[:, :LANES], cos, sa, sb), out[:, LANES:]


def _masked_softmax(s, mask, axis, exp=jnp.exp):
    s = jnp.where(mask, s, NEG)
    e = exp(s - jnp.max(s, axis=axis, keepdims=True))
    return jnp.where(mask, e * (1.0 / jnp.sum(e, axis=axis, keepdims=True)), 0.0)


def _split_bf16(x):
    hi = x.astype(bf16)
    return hi, (x - hi.astype(f32)).astype(bf16)


def _topk_mask(score, k, axis):
    n = score.shape[axis]
    idx = lax.broadcasted_iota(jnp.int32, score.shape, axis)
    sel = jnp.zeros(score.shape, f32)
    for _ in range(k):
        m = jnp.max(score, axis=axis, keepdims=True)
        first = jnp.min(jnp.where(score == m, idx, n), axis=axis, keepdims=True)
        pick = idx == first
        sel = jnp.where(pick, 1.0, sel)
        score = jnp.where(pick, NEG, score)
    return sel


def _topk_rows_by_rank(score, k):
    r, n = score.shape
    cols = jnp.concatenate([score, jnp.zeros((n - r, n), f32)], axis=0).T
    earlier = jnp.where(lax.broadcasted_iota(jnp.int32, (n, n), 0) < lax.broadcasted_iota(jnp.int32, (n, n), 1),
                        1.0, 0.0)
    rows = []
    for c in range(r):
        col, row = cols[:, c:c + 1], score[c:c + 1, :]
        before = jnp.where(col > row, 1.0, jnp.where(col == row, earlier, 0.0))
        rows.append(jnp.where(jnp.sum(before, axis=0, keepdims=True) < k, 1.0, 0.0))
    return jnp.concatenate(rows, axis=0)


def _cover_t(n_slc, n_chunk):
    start = np.arange(n_chunk)[None, :] * CMP_STRIDE
    j = np.arange(n_slc)[:, None]
    cov = (start < (j + 1) * SLC_LEN) & (start + CMP_LEN > j * SLC_LEN)
    return jnp.asarray(cov.astype(np.float32), dtype=bf16)


def _cmp_attn_kernel(part_ref, bias_ref, w2_ref, cos_ref, sa_ref, sb_ref, qt_ref, gatet_ref, cov_ref,
                     ot_ref, selt_ref, kc_ref, vct_ref, *, tq):
    t = pl.program_id(1)

    @pl.when(t == 0)
    def _():
        k, v = _compress_finish(part_ref[0], bias_ref[...], w2_ref[...], cos_ref[...], sa_ref[...], sb_ref[...])
        v_t = v.T
        for h in range(KVH):
            kc_ref[h] = k[:, h * DH:(h + 1) * DH].astype(bf16)
            vct_ref[h] = v_t[h * DH:(h + 1) * DH].astype(bf16)

    nc = kc_ref.shape[1]
    nb = cov_ref.shape[0]
    qpos = t * tq + lax.broadcasted_iota(jnp.int32, (1, tq), 1)
    end_pos = lax.broadcasted_iota(jnp.int32, (nc, 1), 0) * CMP_STRIDE + (CMP_LEN - 1)
    mask_t = end_pos <= qpos
    gate_t = gatet_ref[0]
    cov = cov_ref[...]

    jblk = lax.broadcasted_iota(jnp.int32, (nb, tq), 0)
    qblk = qpos // SLC_LEN
    valid = jblk <= qblk
    forced = (jblk == 0) | (jblk == qblk) | (jblk == qblk - 1)

    scores = []
    for kh in range(KVH):
        kc, vc_t = kc_ref[kh], vct_ref[kh]
        psum = jnp.zeros((nc, tq), f32)
        for g in range(G_NSA):
            h = kh * G_NSA + g
            p_t = _masked_softmax(_dot(kc, qt_ref[0, h]), mask_t, axis=0, exp=jnp.exp2)
            ot_ref[0, h * DH:(h + 1) * DH] = _dot(vc_t, p_t.astype(bf16)) * gate_t[h:h + 1]
            psum = psum + p_t
        hi, lo = _split_bf16(psum)
        imp = _dot(cov, hi) + _dot(cov, lo)
        scores.append(jnp.where(valid, jnp.where(forced, FORCE_SCORE, imp), -1.0))
    picked = _topk_mask(jnp.concatenate(scores, axis=1), SLC_TOP, axis=0)
    for kh in range(KVH):
        sel_t = jnp.where(valid & (picked[:, kh * tq:(kh + 1) * tq] > 0.5), 0.0, NEG)
        if nb < LANES:
            sel_t = jnp.concatenate([sel_t, jnp.full((LANES - nb, tq), NEG, f32)], axis=0)
        selt_ref[0, kh] = sel_t


def _cmp_attn(part, bias, w2_big, q_t, gate_t, tq):
    b, nc, _ = part.shape
    lq = q_t.shape[3]
    n_slc = lq // SLC_LEN
    end_pos = jnp.arange(nc) * CMP_STRIDE + (CMP_LEN - 1)
    cos, sa, sb = _rope_tables(end_pos)
    cov = _cover_t(n_slc, nc)
    full = lambda a: pl.BlockSpec(a.shape, lambda i, t: (0,) * a.ndim)
    return pl.pallas_call(
        functools.partial(_cmp_attn_kernel, tq=tq),
        grid=(b, lq // tq),
        in_specs=[pl.BlockSpec((1, nc, PART_W), lambda i, t: (i, 0, 0)),
                  full(bias), full(w2_big), full(cos), full(sa), full(sb),
                  pl.BlockSpec((1, H_NSA, DH, tq), lambda i, t: (i, 0, 0, t)),
                  pl.BlockSpec((1, gate_t.shape[1], tq), lambda i, t: (i, 0, t)),
                  full(cov)],
        out_specs=(pl.BlockSpec((1, NSA_WIDTH, tq), lambda i, t: (i, 0, t)),
                   pl.BlockSpec((1, KVH, LANES, tq), lambda i, t: (i, 0, 0, t))),
        out_shape=(jax.ShapeDtypeStruct((b, NSA_WIDTH, lq), f32),
                   jax.ShapeDtypeStruct((b, KVH, LANES, lq), f32)),
        scratch_shapes=[pltpu.VMEM((KVH, nc, DH), bf16), pltpu.VMEM((KVH, DH, nc), bf16)],
        compiler_params=_cparams(("parallel", "arbitrary")),
        name="cmp_attn",
    )(part, bias, w2_big, cos, sa, sb, q_t, gate_t, cov)


def _block_expand(n_slc_pad, n_keys):
    e = (np.arange(n_keys)[None, :] // SLC_LEN) == np.arange(n_slc_pad)[:, None]
    return jnp.asarray(e.astype(np.float32), dtype=bf16)


def _mask_bias_t(allowed):
    bias = jnp.where(allowed, 0.0, NEG)
    return jnp.concatenate([bias] * G_NSA, axis=1)


def _sublane_fold(x, op):
    return functools.reduce(op, [x[r:r + 8] for r in range(0, x.shape[0], 8)])


def _slc_win_kernel(qt_ref, ks_ref, vst_ref, kw_ref, vwt_ref, selt_ref, gatet_ref, ocmpt_ref,
                    o_ref, s_a, s_b, m_a, m_b, ws_a, ws_b, wm_a, wm_b, acc_ref, wacc_ref, *, tq, tk, nt):
    t = pl.program_id(2)

    @pl.when(t == 0)
    def _():
        m_b[...] = jnp.full(m_b.shape, NEG, f32)
        ws_b[...] = jnp.zeros(ws_b.shape, f32)
        wm_b[...] = jnp.zeros(wm_b.shape, f32)

    step = functools.partial(_slc_win_step, qt_ref, ks_ref, vst_ref, kw_ref, vwt_ref, selt_ref, gatet_ref,
                             ocmpt_ref, o_ref, acc_ref, wacc_ref, tq=tq, tk=tk, nt=nt)

    @pl.when(t % 2 == 0)
    def _():
        step(s_a, s_b, m_a, m_b, ws_a, ws_b, wm_a, wm_b)

    @pl.when(t % 2 == 1)
    def _():
        step(s_b, s_a, m_b, m_a, ws_b, ws_a, wm_b, wm_a)


ONES_ROWS = 16


def _slc_win_step(qt_ref, ks_ref, vst_ref, kw_ref, vwt_ref, selt_ref, gatet_ref, ocmpt_ref, o_ref,
                  acc_ref, wacc_ref, s_cur, s_prev, m_cur, m_last, ws_cur, ws_prev, wm_cur, wm_last,
                  *, tq, tk, nt):
    kh = pl.program_id(1)
    t = pl.program_id(2)
    lq = ks_ref.shape[2]
    band = WINDOW + tq
    blocks_per_tile = tk // SLC_LEN
    key_tiles = lambda tile: (tile * tq + tq + tk - 1) // tk
    n_score = jnp.where(t < nt, key_tiles(t), 0)
    n_apply = jnp.where(t >= 1, key_tiles(t - 1), 0)
    n_plain = jnp.where(t < nt, (t * tq) // tk, 0)
    q0 = t * tq
    qpos = q0 + lax.broadcasted_iota(jnp.int32, (1, tq), 1)
    q_t = jnp.concatenate([qt_ref[0, g] for g in range(G_NSA)], axis=1)
    ones = jnp.ones((ONES_ROWS, tk), bf16)

    m_cur[...] = jnp.full(m_cur.shape, NEG, f32)
    m_prev = jnp.max(m_last[...], axis=0, keepdims=True)
    acc_ref[...] = jnp.zeros(acc_ref.shape, f32)

    def score(kt, causal):
        k0 = pl.multiple_of(kt * tk, tk)
        j0 = pl.multiple_of(kt * blocks_per_tile, blocks_per_tile)
        sel = selt_ref[0, 0, pl.ds(j0, blocks_per_tile), :]
        bias = jnp.concatenate([jnp.broadcast_to(sel[j:j + 1], (SLC_LEN, tq)) for j in range(blocks_per_tile)],
                               axis=0)
        if causal:
            kpos = k0 + lax.broadcasted_iota(jnp.int32, (tk, 1), 0)
            bias = jnp.where(kpos <= qpos, bias, NEG)
        s = _dot(ks_ref[0, 0, pl.ds(k0, tk), :], q_t) + jnp.concatenate([bias] * G_NSA, axis=1)
        s_cur[pl.ds(k0, tk), :] = s
        m_cur[...] = jnp.maximum(m_cur[...], _sublane_fold(s, jnp.maximum))

    def apply(kt):
        k0 = pl.multiple_of(kt * tk, tk)
        p = jnp.exp2(s_prev[pl.ds(k0, tk), :] - m_prev).astype(bf16)
        acc_ref[...] += _dot(jnp.concatenate([vst_ref[0, 0, :, pl.ds(k0, tk)], ones], axis=0), p)

    def both_body(kt, carry):
        score(kt, causal=False)
        apply(kt)
        return carry

    def score_body(kt, carry):
        score(kt, causal=True)
        return carry

    def apply_body(kt, carry):
        apply(kt)
        return carry

    n_both = jnp.minimum(n_plain, n_apply)
    lax.fori_loop(0, n_both, both_body, 0)
    lax.fori_loop(n_both, n_score, score_body, 0)
    lax.fori_loop(n_both, n_apply, apply_body, 0)

    w0 = pl.multiple_of(jnp.clip(q0 - WINDOW, 0, lq - band), tq)
    kpos = w0 + lax.broadcasted_iota(jnp.int32, (band, 1), 0)
    s = _dot(kw_ref[0, 0, pl.ds(w0, band), :], q_t) + _mask_bias_t((kpos <= qpos) & (kpos > qpos - WINDOW))
    ws_cur[...] = s
    wm_cur[...] = _sublane_fold(s, jnp.maximum)
    w0_prev = pl.multiple_of(jnp.clip(q0 - tq - WINDOW, 0, lq - band), tq)
    p = jnp.exp2(ws_prev[...] - jnp.max(wm_last[...], axis=0, keepdims=True)).astype(bf16)
    wacc_ref[...] = _dot(jnp.concatenate([vwt_ref[0, 0, :, pl.ds(w0_prev, band)],
                                          jnp.ones((ONES_ROWS, band), bf16)], axis=0), p)

    @pl.when(t >= 1)
    def _():
        o_slc = acc_ref[:DH] / acc_ref[DH:DH + 1]
        o_win = wacc_ref[:DH] / wacc_ref[DH:DH + 1]
        gate_t = gatet_ref[0]
        row = lambda c, g: jnp.where(kh == 0, gate_t[c * H_NSA + g:c * H_NSA + g + 1],
                                     gate_t[c * H_NSA + G_NSA + g:c * H_NSA + G_NSA + g + 1])
        g_slc = jnp.concatenate([row(1, g) for g in range(G_NSA)], axis=1)
        g_win = jnp.concatenate([row(2, g) for g in range(G_NSA)], axis=1)
        o_t = g_slc * o_slc + g_win * o_win
        o_tok = jnp.concatenate([o_t, jnp.zeros((LANES - DH, o_t.shape[1]), f32)], axis=0).T
        ocmp_tok = ocmpt_ref[0].T
        o_ref[0] = ocmp_tok + jnp.concatenate([o_tok[g * tq:(g + 1) * tq, :DH] for g in range(G_NSA)], axis=1)


def _slc_win(q_t, ks, vs_t, kw, vw_t, sel_t, gate_t, o_cmp_t, tq, tk):
    b, _, _, lq = q_t.shape
    assert lq >= WINDOW + tq and lq % tk == 0 and tk % (8 * SLC_LEN) == 0
    gw = G_NSA * DH
    k_spec = pl.BlockSpec((1, 1, lq, DH), lambda i, k, t: (i, k, 0, 0))
    vt_spec = pl.BlockSpec((1, 1, DH, lq), lambda i, k, t: (i, k, 0, 0))
    lanes = G_NSA * tq
    band = WINDOW + tq
    nt = lq // tq
    scored = lambda t: jnp.minimum(t, nt - 1)
    drained = lambda t: jnp.maximum(t - 1, 0)
    return pl.pallas_call(
        functools.partial(_slc_win_kernel, tq=tq, tk=tk, nt=nt),
        grid=(b, KVH, nt + 1),
        in_specs=[pl.BlockSpec((1, G_NSA, DH, tq), lambda i, k, t: (i, k, 0, scored(t))),
                  k_spec, vt_spec, k_spec, vt_spec,
                  pl.BlockSpec((1, 1, LANES, tq), lambda i, k, t: (i, k, 0, scored(t))),
                  pl.BlockSpec((1, gate_t.shape[1], tq), lambda i, k, t: (i, 0, drained(t))),
                  pl.BlockSpec((1, gw, tq), lambda i, k, t: (i, k, drained(t)))],
        out_specs=pl.BlockSpec((1, tq, gw), lambda i, k, t: (i, drained(t), k)),
        out_shape=jax.ShapeDtypeStruct((b, lq, NSA_WIDTH), f32),
        scratch_shapes=[pltpu.VMEM((lq, lanes), f32), pltpu.VMEM((lq, lanes), f32),
                        pltpu.VMEM((8, lanes), f32), pltpu.VMEM((8, lanes), f32),
                        pltpu.VMEM((band, lanes), f32), pltpu.VMEM((band, lanes), f32),
                        pltpu.VMEM((8, lanes), f32), pltpu.VMEM((8, lanes), f32),
                        pltpu.VMEM((DH + ONES_ROWS, lanes), f32),
                        pltpu.VMEM((DH + ONES_ROWS, lanes), f32)],
        compiler_params=_cparams(("parallel", "parallel", "arbitrary")),
        name="slc_win",
    )(q_t, ks, vs_t, kw, vw_t, sel_t, gate_t, o_cmp_t)


def _rms_gain(o, gain):
    return o * lax.rsqrt(jnp.mean(o * o, axis=-1, keepdims=True) + EPS) * gain


def _gla_kernel(q_ref, k_ref, v_ref, la_ref, z_ref, gain_ref, s0_ref, y_ref, st_ref, s_scr, *, tl, chunk):
    t = pl.program_id(2)

    @pl.when(t == 0)
    def _():
        s_scr[...] = s0_ref[0, 0]

    c = chunk
    row = lax.broadcasted_iota(jnp.int32, (c, c), 0)
    col = lax.broadcasted_iota(jnp.int32, (c, c), 1)
    causal = col <= row
    tril = jnp.where(causal, 1.0, 0.0).astype(bf16)
    lane = lax.broadcasted_iota(jnp.int32, (1, LANES), 1)
    head_mask = [jnp.where(lane < DK, 1.0, 0.0), jnp.where(lane >= DK, 1.0, 0.0)]
    gain = gain_ref[...]

    chunks = []
    for ci in range(tl // c):
        rows = slice(ci * c, (ci + 1) * c)
        q, k, la = q_ref[rows, :], k_ref[rows, :], la_ref[rows, :]
        hi, rest = la.astype(bf16), la - la.astype(bf16).astype(f32)
        mid, lo = rest.astype(bf16), (rest - rest.astype(bf16).astype(f32)).astype(bf16)
        bc = _dot(tril, hi) + _dot(tril, mid) + _dot(tril, lo)
        ref_row = bc[c // 2 - 1:c // 2, :]
        b_last = bc[c - 1:c, :]
        q_a = q * jnp.exp(bc - ref_row)
        k_a = (k * jnp.exp(ref_row - bc)).astype(bf16)
        q_s = q * jnp.exp(bc)
        k_s = k * jnp.exp(b_last - bc)
        o_intra, q_inter, add = [], [], None
        for h in range(2):
            mh = head_mask[h]
            vh = v_ref[rows, h * DV:(h + 1) * DV].astype(bf16)
            a = jnp.where(causal, _dot_nt((q_a * mh).astype(bf16), k_a), 0.0)
            o_intra.append(_dot(a.astype(bf16), vh))
            q_inter.append((q_s * mh).astype(bf16))
            u = _dot_tn(vh, (k_s * mh).astype(bf16))
            add = u if add is None else add + u
        chunks.append((rows, jnp.exp(b_last), add, o_intra, q_inter))

    st = s_scr[...]
    for rows, decay, add, o_intra, q_inter in chunks:
        st_b = st.astype(bf16)
        for h in range(2):
            o = o_intra[h] + _dot_nt(q_inter[h], st_b)
            y_ref[rows, h * DV:(h + 1) * DV] = (_rms_gain(o, gain)
                                                * z_ref[rows, h * DV:(h + 1) * DV]).astype(y_ref.dtype)
        st = st * decay + add
    s_scr[...] = st
    st_ref[0, 0] = st


def _gla(gq, gk, gv, la, zg, gain, state0, b, tl, chunk):
    n = gq.shape[0]
    nt = n // b // tl
    pairs = H_GLA // 2
    qk = pl.BlockSpec((tl, LANES), lambda i, p, t: (i * nt + t, p))
    vz = pl.BlockSpec((tl, 2 * DV), lambda i, p, t: (i * nt + t, p))
    st_spec = pl.BlockSpec((1, 1, DV, LANES), lambda i, p, t: (i, p, 0, 0))
    gain2 = gain.reshape(1, DV)
    s0 = state0.reshape(b, pairs, 2, DK, DV).transpose(0, 1, 4, 2, 3).reshape(b, pairs, DV, LANES)
    y, st = pl.pallas_call(
        functools.partial(_gla_kernel, tl=tl, chunk=chunk),
        grid=(b, pairs, nt),
        in_specs=[qk, qk, vz, qk, vz, pl.BlockSpec((1, DV), lambda i, p, t: (0, 0)), st_spec],
        out_specs=(vz, st_spec),
        out_shape=(jax.ShapeDtypeStruct((n, GLA_WIDTH), bf16),
                   jax.ShapeDtypeStruct((b, pairs, DV, LANES), f32)),
        scratch_shapes=[pltpu.VMEM((DV, LANES), f32)],
        compiler_params=_cparams(("parallel", "parallel", "arbitrary")),
        name="gla",
    )(gq, gk, gv, la, zg, gain2, s0)
    state = st.reshape(b, pairs, DV, 2, DK).transpose(0, 1, 3, 4, 2).reshape(b, H_GLA, DK, DV)
    return y, state


def _outproj_kernel(x_ref, on_ref, zn_ref, yg_ref, w_ref, g_ref, y_ref):
    y_nsa = (on_ref[...] * zn_ref[...]).astype(bf16)
    mix = _dot(y_nsa, w_ref[:NSA_WIDTH, :]) + _dot(yg_ref[...].astype(bf16), w_ref[NSA_WIDTH:, :])
    y_ref[...] = _rms_gain(x_ref[...] + mix, g_ref[...])


def _outproj(x, o_nsa, zn, y_gla, w_out, gain, tm):
    n = x.shape[0]
    tok = lambda w: pl.BlockSpec((tm, w), lambda i: (i, 0))
    g2 = gain.reshape(1, D_MODEL)
    return pl.pallas_call(
        _outproj_kernel,
        grid=(n // tm,),
        in_specs=[tok(D_MODEL), tok(NSA_WIDTH), tok(NSA_WIDTH), tok(GLA_WIDTH),
                  pl.BlockSpec(w_out.shape, lambda i: (0, 0)), pl.BlockSpec(g2.shape, lambda i: (0, 0))],
        out_specs=tok(D_MODEL),
        out_shape=jax.ShapeDtypeStruct((n, D_MODEL), f32),
        compiler_params=_cparams(("parallel",)),
        name="outproj",
    )(x, o_nsa, zn, y_gla, w_out, g2)


def _prompt_path(x, wts, w_t, w_tok):
    (norm_g, _, wa2_pad, ba, w1_big, cmp_bias, w2_big, gla_gain, w_out, out_gain) = wts
    b, lq, _ = x.shape
    (q_t, kvc_t, kvs_t, kvw_t, ks, kw, vs_t, vw_t, kvc_tok, gate_t, zn, gq, gk, gv, la, zg) = _inproj_prompt(
        x, norm_g, w_t, w_tok, wa2_pad, ba, tm=512)
    n_chunk = lq // CMP_STRIDE
    part = _cmp_part(kvc_tok.reshape(b * n_chunk, CHUNK_W), w1_big, tr=n_chunk)
    o_cmp_t, sel_t = _cmp_attn(part.reshape(b, n_chunk, PART_W), cmp_bias, w2_big, q_t, gate_t, tq=512)
    o_nsa = _slc_win(q_t, ks, vs_t, kw, vw_t, sel_t, gate_t, o_cmp_t, tq=256, tk=512)
    y_gla, state = _gla(gq, gk, gv, la, zg, gla_gain, jnp.zeros((b, H_GLA, DK, DV), f32), b, tl=512,
                        chunk=GLA_CHUNK)
    y = _outproj(x.reshape(b * lq, D_MODEL), o_nsa.reshape(b * lq, NSA_WIDTH), zn, y_gla, w_out, out_gain, tm=512)
    kv5 = lambda a: a.reshape(b, 2, KVH, DH, -1).transpose(0, 4, 1, 2, 3)
    wlen = min(WINDOW, lq)
    return (y.reshape(b, lq, D_MODEL), kv5(kvc_t), kv5(kvs_t), kv5(kvw_t[:, :, lq - wlen:]), state)


def _pages_native(cache):
    n_pool = cache.shape[0]
    return cache.transpose(0, 2, 3, 4, 1).reshape(n_pool, 2, KVH * DH, cache.shape[1])


def _page_gather(make_copies):
    i = pl.program_id(0)
    slot = i % 2

    @pl.when(i == 0)
    def _():
        for c in make_copies(0, 0):
            c.start()

    @pl.when(i + 1 < pl.num_programs(0))
    def _():
        for c in make_copies(i + 1, 1 - slot):
            c.start()

    for c in make_copies(i, slot):
        c.wait()
    return slot


def _cmp_w1_pairs(cmp_w1):
    ratio = CMP_LEN // CMP_STRIDE
    w1r = cmp_w1.reshape(2, ratio, CMP_STRIDE // 2, 2, DH, CMP_HID)
    eye = jnp.eye(KVH, dtype=cmp_w1.dtype)
    big = jnp.einsum('crpjdh,kK->cpjKdrkh', w1r, eye)
    return big.reshape(2, CMP_STRIDE // 2, 2 * KVH * DH, ratio * KVH * CMP_HID).astype(bf16)


def _cmp_w2_heads(cmp_w2):
    eye = jnp.eye(KVH, dtype=cmp_w2.dtype)
    return jnp.einsum('chd,kK->ckhKd', cmp_w2, eye).reshape(2, KVH * CMP_HID, KVH * DH).astype(bf16)


def _group_sum_matrix(n_tok):
    r = np.arange(KVH * n_tok)[:, None]
    c = np.arange(H_NSA * n_tok)[None, :]
    m = ((c // (G_NSA * n_tok)) == (r // n_tok)) & ((c % n_tok) == (r % n_tok))
    return m.astype(np.float32)


def _cmp_attn_sample_kernel(pt_ref, cache_ref, perm_ref, w1_ref, bias_ref, w2_ref, cos_ref, sa_ref, sb_ref, q_ref,
                            gate_ref, cov_ref, gs_ref, gst_ref, o_ref, sel_ref, buf, rows_scr, sem,
                            *, n_pages, n_tok, past_len):
    def copies(seq, slot):
        return [pltpu.make_async_copy(cache_ref.at[pt_ref[seq, p]], buf.at[slot, p], sem.at[slot])
                for p in range(n_pages)]

    slot = _page_gather(copies)

    cpp = PAGE_SIZE // CMP_STRIDE
    perm = perm_ref[...]

    group = 8
    kd = KVH * DH

    def to_rows(g, carry):
        for u in range(group):
            p = g * group + u
            r0 = pl.multiple_of(p * cpp, cpp)
            t = _dot_nt(perm, buf[slot, p].reshape(2 * kd, PAGE_SIZE).astype(bf16))
            for c in range(2):
                for s in range(CMP_STRIDE):
                    rows_scr[c, s, pl.ds(r0, cpp), :] = t[s * cpp:(s + 1) * cpp, c * kd:(c + 1) * kd]
        return carry

    lax.fori_loop(0, n_pages // group, to_rows, 0)

    nc = n_pages * cpp
    kv = []
    for c in range(2):
        part = jnp.zeros((nc, w1_ref.shape[-1]), f32)
        for pr in range(CMP_STRIDE // 2):
            lhs = jnp.concatenate([rows_scr[c, 2 * pr], rows_scr[c, 2 * pr + 1]], axis=-1)
            part = part + _dot(lhs.astype(bf16), w1_ref[c, pr])
        half = part.shape[1] // 2
        pre = part[:, :half] + pltpu.roll(part[:, half:], nc - 1, 0)
        hid = _silu(pre + bias_ref[:, c * half:(c + 1) * half])
        kv.append(_dot(hid.astype(bf16), w2_ref[c]))
    k = _rope128(kv[0], cos_ref[...], sa_ref[...], sb_ref[...])
    v = kv[1]
    rows = H_NSA * n_tok
    qpos = past_len + lax.broadcasted_iota(jnp.int32, (rows, 1), 0) % n_tok
    end_c = lax.broadcasted_iota(jnp.int32, (1, nc), 1) * CMP_STRIDE + (CMP_LEN - 1)
    mask = end_c <= qpos
    p = _masked_softmax(_dot_nt(q_ref[0], k.astype(bf16)), mask, axis=-1)
    o_ref[0] = _dot(p.astype(bf16), v.astype(bf16)) * gate_ref[0][:, 0:1]
    hi, lo = _split_bf16(p)
    gs = gs_ref[...]
    ph, pl_ = _split_bf16(_dot(gs, hi) + _dot(gs, lo))
    imp = _dot(ph, cov_ref[...]) + _dot(pl_, cov_ref[...])
    n_blk = imp.shape[1]
    jblk = lax.broadcasted_iota(jnp.int32, imp.shape, 1)
    forced = (jblk == 0) | (jblk == n_blk - 1)
    sel = _topk_rows_by_rank(jnp.where(forced, FORCE_SCORE, imp), SLC_TOP - 1)
    sel_ref[0] = _dot(gst_ref[...], sel.astype(bf16)).astype(bf16)


def _cmp_attn_sample(cache_t, page_table, w1_pairs, bias, w2_heads, q_blk, gate_rows, n_tok):
    b, n_pages = page_table.shape
    past_len = n_pages * PAGE_SIZE
    nc = past_len // CMP_STRIDE
    rows = H_NSA * n_tok
    n_blk = past_len // SLC_LEN
    end_pos = jnp.arange(nc) * CMP_STRIDE + (CMP_LEN - 1)
    cos, sa, sb = _rope_tables(end_pos)
    cov = _cover_t(n_blk, nc).T
    gs = jnp.asarray(_group_sum_matrix(n_tok), dtype=bf16)
    gst = gs.T
    r = np.arange(PAGE_SIZE)
    cpp = PAGE_SIZE // CMP_STRIDE
    perm = jnp.asarray((r[None, :] == (r[:, None] % cpp) * CMP_STRIDE + r[:, None] // cpp).astype(np.float32),
                       dtype=bf16)
    full = lambda a: pl.BlockSpec(a.shape, lambda i, pt: (0,) * a.ndim)
    seq = lambda a: pl.BlockSpec((1,) + a.shape[1:], lambda i, pt: (i,) + (0,) * (a.ndim - 1))
    grid_spec = pltpu.PrefetchScalarGridSpec(
        num_scalar_prefetch=1,
        grid=(b,),
        in_specs=[pl.BlockSpec(memory_space=pl.ANY), full(perm), full(w1_pairs), full(bias), full(w2_heads),
                  full(cos), full(sa), full(sb), seq(q_blk), seq(gate_rows), full(cov), full(gs), full(gst)],
        out_specs=(pl.BlockSpec((1, rows, LANES), lambda i, pt: (i, 0, 0)),
                   pl.BlockSpec((1, rows, n_blk), lambda i, pt: (i, 0, 0))),
        scratch_shapes=[pltpu.VMEM((2, n_pages) + cache_t.shape[1:], f32),
                        pltpu.VMEM((2, CMP_STRIDE, nc, KVH * DH), f32),
                        pltpu.SemaphoreType.DMA((2,))],
    )
    return pl.pallas_call(
        functools.partial(_cmp_attn_sample_kernel, n_pages=n_pages, n_tok=n_tok, past_len=past_len),
        grid_spec=grid_spec,
        out_shape=(jax.ShapeDtypeStruct((b, rows, LANES), f32),
                   jax.ShapeDtypeStruct((b, rows, n_blk), bf16)),
        compiler_params=_cparams(("arbitrary",)),
        name="cmp_attn_sample",
    )(page_table, cache_t, perm, w1_pairs, bias, w2_heads, cos, sa, sb, q_blk, gate_rows, cov, gs, gst)


def _slc_win_sample_kernel(pt_ref, cache_ref, q_ref, sel_ref, e_ref, snew_ref, cw_ref, wnew_ref, wnewt_ref,
                           gate_ref, ocmp_ref, o_ref, wout_ref, buf, sem, *, n_pages, n_tok, win_off):
    def copies(seq, slot):
        return [pltpu.make_async_copy(cache_ref.at[pt_ref[seq, p], c],
                                      buf.at[slot, c, :, pl.ds(p * PAGE_SIZE, PAGE_SIZE)], sem.at[slot])
                for p in range(n_pages) for c in range(2)]

    slot = _page_gather(copies)
    rows = H_NSA * n_tok
    q = q_ref[0]
    tok = lax.broadcasted_iota(jnp.int32, (rows, 1), 0) % n_tok
    new_i = lax.broadcasted_iota(jnp.int32, (1, snew_ref.shape[1]), 1)
    new_ok = (new_i <= tok) & (new_i < n_tok)

    def attend(keys_t, vals_t, allowed, new_ref):
        s = jnp.where(allowed, _dot(q, keys_t), NEG)
        k_new = new_ref[0][:, :LANES].astype(bf16)
        v_new = new_ref[0][:, LANES:].astype(bf16)
        s_new = jnp.where(new_ok, _dot_nt(q, k_new), NEG)
        m = jnp.maximum(jnp.max(s, axis=-1, keepdims=True), jnp.max(s_new, axis=-1, keepdims=True))
        p = jnp.where(allowed, jnp.exp(s - m), 0.0)
        p_new = jnp.where(new_ok, jnp.exp(s_new - m), 0.0)
        l = jnp.sum(p, axis=-1, keepdims=True) + jnp.sum(p_new, axis=-1, keepdims=True)
        return (_dot_nt(p.astype(bf16), vals_t) + _dot(p_new.astype(bf16), v_new)) / l

    picked = _dot(sel_ref[0], e_ref[...]) > 0.5
    o_slc = attend(buf[slot, 0].astype(bf16), buf[slot, 1].astype(bf16), picked, snew_ref)

    wbuf = cw_ref.shape[-1]
    win_i = lax.broadcasted_iota(jnp.int32, (1, wbuf), 1)
    o_win = attend(cw_ref[0, 0].astype(bf16), cw_ref[0, 1].astype(bf16), win_i > tok + win_off, wnew_ref)

    lane = lax.broadcasted_iota(jnp.int32, (1, LANES), 1)
    for c in range(2):
        shifted = pltpu.roll(cw_ref[0, c], wbuf - n_tok, 1)
        tail = jnp.where(lane >= LANES - n_tok, wnewt_ref[0, c], shifted[:, wbuf - LANES:])
        wout_ref[0, c] = jnp.concatenate([shifted[:, :wbuf - LANES], tail], axis=-1)

    gate = gate_ref[0]
    o = ocmp_ref[0] + gate[:, 1:2] * o_slc + gate[:, 2:3] * o_win
    second_kvh = lax.broadcasted_iota(jnp.int32, (rows, 1), 0) >= G_NSA * n_tok
    o_ref[0] = jnp.where(second_kvh, o[:, DH:], o[:, :DH])


def _slc_win_sample(cache_t, page_table, q_blk, sel_rows, kvs_new, win_t, kvw_new, kvw_new_t, gate_rows,
                    o_cmp, n_tok):
    b, n_pages = page_table.shape
    rows = H_NSA * n_tok
    past_len = n_pages * PAGE_SIZE
    wbuf = win_t.shape[-1]
    e = _block_expand(sel_rows.shape[-1], past_len)
    full = lambda a: pl.BlockSpec(a.shape, lambda i, pt: (0,) * a.ndim)
    seq = lambda a: pl.BlockSpec((1,) + a.shape[1:], lambda i, pt: (i,) + (0,) * (a.ndim - 1))
    grid_spec = pltpu.PrefetchScalarGridSpec(
        num_scalar_prefetch=1,
        grid=(b,),
        in_specs=[pl.BlockSpec(memory_space=pl.ANY), seq(q_blk), seq(sel_rows), full(e), seq(kvs_new),
                  seq(win_t), seq(kvw_new), seq(kvw_new_t), seq(gate_rows), seq(o_cmp)],
        out_specs=(pl.BlockSpec((1, rows, DH), lambda i, pt: (i, 0, 0)), seq(win_t)),
        scratch_shapes=[pltpu.VMEM((2, 2, KVH * DH, past_len), f32), pltpu.SemaphoreType.DMA((2,))],
    )
    return pl.pallas_call(
        functools.partial(_slc_win_sample_kernel, n_pages=n_pages, n_tok=n_tok, win_off=wbuf - WINDOW),
        grid_spec=grid_spec,
        out_shape=(jax.ShapeDtypeStruct((b, rows, DH), f32), jax.ShapeDtypeStruct(win_t.shape, f32)),
        compiler_params=_cparams(("arbitrary",)),
        name="slc_win_sample",
    )(page_table, cache_t, q_blk, sel_rows, e, kvs_new, win_t, kvw_new, kvw_new_t, gate_rows, o_cmp)


def _gla_sample(gq, gk, la, gv, zg, state, gain, b, n_tok):
    rows = 8
    pad = lambda a: jnp.pad(a.reshape(b, n_tok, -1), ((0, 0), (0, rows - n_tok), (0, 0))).reshape(b * rows, -1)
    y, s_new = _gla(pad(gq), pad(gk), pad(gv), pad(la), pad(zg), gain, state, b, tl=rows, chunk=rows)
    return y.reshape(b, rows, GLA_WIDTH)[:, :n_tok].reshape(b * n_tok, GLA_WIDTH), s_new


def _sample_path(x, cache_c, cache_s, cache_w, state, page_table, wts, w1_pairs, w2_heads):
    (norm_g, w_pack, wa2_pad, ba, w1_big, cmp_bias, w2_big, gla_gain, w_out, out_gain) = wts
    b, n_tok, _ = x.shape
    n = b * n_tok
    n_pages = page_table.shape[1]
    past_len = n_pages * PAGE_SIZE
    assert n_tok <= 8 and past_len % SLC_LEN == 0 and past_len // SLC_LEN <= LANES
    pos = past_len + jnp.arange(n) % n_tok
    (q_hm, kvc, kvs, kvw, _, _, _, _, gate, zn, gq, gk, gv, la, zg) = _inproj(
        x.reshape(1, n, D_MODEL), pos, norm_g, w_pack, wa2_pad, ba, tm=n)

    rows = H_NSA * n_tok
    q_rows = q_hm[0].reshape(H_NSA, b, n_tok, DH).transpose(1, 0, 2, 3)
    zero = jnp.zeros_like(q_rows[:, :G_NSA])
    q_blk = jnp.concatenate([jnp.concatenate([q_rows[:, :G_NSA], zero], axis=-1),
                             jnp.concatenate([zero, q_rows[:, G_NSA:]], axis=-1)], axis=1).reshape(b, rows, LANES)
    gate_rows = gate[:, :3 * H_NSA].reshape(b, n_tok, 3, H_NSA).transpose(0, 3, 1, 2).reshape(b, rows, 3)
    gate_rows = jnp.pad(gate_rows, ((0, 0), (0, 0), (0, LANES - 3)))
    pad_new = lambda a: jnp.pad(a.reshape(b, n_tok, KV_ROW), ((0, 0), (0, 8 - n_tok), (0, 0)))

    kvw_new_t = kvw.reshape(b, n_tok, 2, KVH * DH).transpose(0, 2, 3, 1)
    kvw_new_t = jnp.pad(kvw_new_t, ((0, 0), (0, 0), (0, 0), (LANES - n_tok, 0)))

    o_cmp, sel_rows = _cmp_attn_sample(_pages_native(cache_c), page_table, w1_pairs, cmp_bias, w2_heads,
                                       q_blk, gate_rows, n_tok)
    o_rows, win_t = _slc_win_sample(_pages_native(cache_s), page_table, q_blk, sel_rows, pad_new(kvs),
                                    _pages_native(cache_w), pad_new(kvw), kvw_new_t, gate_rows, o_cmp, n_tok)
    o_nsa = o_rows.reshape(b, H_NSA, n_tok, DH).transpose(0, 2, 1, 3).reshape(n, NSA_WIDTH)

    y_gla, s_new = _gla_sample(gq, gk, la, gv, zg, state, gla_gain, b, n_tok)
    y = _outproj(x.reshape(n, D_MODEL), o_nsa, zn, y_gla, w_out, out_gain, tm=n)
    kv5 = lambda a: a.reshape(b, n_tok, 2, KVH, DH)
    win_new = win_t.reshape(b, 2, KVH, DH, -1).transpose(0, 4, 1, 2, 3)
    return (y.reshape(b, n_tok, D_MODEL), kv5(kvc), kv5(kvs), win_new, s_new)


def kernel(x_prompt, x_sample, cache_cmp_kv, cache_slc_kv, cache_win_kv, state_gla, page_table,
           norm_in_gain, w_in, cmp_pe, cmp_w1, cmp_b1, cmp_w2, gla_wa2, gla_ba, gla_norm_gain,
           w_out, norm_out_gain):
    assert w_in.shape[0] == 1, "single-layer step"
    wts = (norm_in_gain[0], _pack_w_in(w_in[0]), _pad_wa2(gla_wa2[0]), gla_ba[0],
           _cmp_w1_big(cmp_w1[0]), _cmp_bias(cmp_pe[0], cmp_w1[0], cmp_b1[0]), _cmp_w2_big(cmp_w2[0]),
           gla_norm_gain[0], w_out[0].astype(bf16), norm_out_gain)
    yp, cmp_p, slc_p, win_p, gla_p = _prompt_path(x_prompt, wts, *_pack_w_in_prompt(w_in[0]))
    ys, cmp_s, slc_s, win_s, gla_s = _sample_path(
        x_sample, cache_cmp_kv[0], cache_slc_kv[0], cache_win_kv[0], state_gla[0], page_table, wts,
        _cmp_w1_pairs(cmp_w1[0]), _cmp_w2_heads(cmp_w2[0]))
    return (yp, ys, cmp_p[None], cmp_s[None], slc_p[None], slc_s[None], win_p[None], win_s[None],
            gla_p[None], gla_s[None])
```

```python
import functools

import numpy as np
import jax
import jax.numpy as jnp
from jax import lax
from jax.experimental import pallas as pl
from jax.experimental.pallas import tpu as pltpu

f32 = jnp.float32
bf16 = jnp.bfloat16

D_MODEL = 1024
DH = 64
H_NSA = 8
KVH = 2
G_NSA = 4
NSA_WIDTH = H_NSA * DH
CMP_LEN = 32
CMP_STRIDE = 16
CMP_HID = 128
SLC_LEN = 64
SLC_TOP = 16
WINDOW = 512
ROT_HALF = 8
ROPE_THETA = 500000.0
FORCE_SCORE = 1.0e4
H_GLA = 4
DK = 64
DV = 128
GLA_WIDTH = H_GLA * DV
GLA_LR = 16
GLA_TAU = 16.0
GLA_CHUNK = 64
GLA_SUB = 16
EPS = 1e-6
NEG = -1.0e30
LOG2E = 1.4426950408889634
PAGE_SIZE = 128
KV_ROW = 2 * KVH * DH

IN_SIZES = (H_NSA * DH, KV_ROW, KV_ROW, KV_ROW, 3 * H_NSA, NSA_WIDTH,
            H_GLA * DK, H_GLA * DK, H_GLA * DV, GLA_LR, GLA_WIDTH)
IN_OFFSETS = [0] + [int(v) for v in np.cumsum(IN_SIZES)]

LANES = 128
VMEM_LIMIT = 56 * 1024 * 1024

P_Q, P_KVC, P_KVS, P_KVW, P_ZN, P_GQ, P_GK, P_GV, P_ZG, P_MISC = (
    0, 512, 768, 1024, 1280, 1792, 2048, 2304, 2816, 3328)
D_PACK = P_MISC + LANES
MISC_GATE = 0
MISC_GLR = 32


def _cparams(sem):
    return pltpu.CompilerParams(dimension_semantics=sem, vmem_limit_bytes=VMEM_LIMIT)


def _sigmoid(x):
    return 1.0 / (1.0 + jnp.exp(-x))


def _silu(x):
    return x * _sigmoid(x)


def _log_sigmoid(x):
    return jnp.minimum(x, 0.0) - jnp.log1p(jnp.exp(-jnp.abs(x)))


def _dot(a, b):
    return jnp.dot(a, b, preferred_element_type=f32)


def _dot_nt(a, b):
    return lax.dot_general(a, b, (((1,), (1,)), ((), ())), preferred_element_type=f32)


def _dot_tn(a, b):
    return lax.dot_general(a, b, (((0,), (0,)), ((), ())), preferred_element_type=f32)


def _rope_tables(pos):
    n = pos.shape[0]
    inv = ROPE_THETA ** (-(jnp.arange(ROT_HALF, dtype=f32) / ROT_HALF))
    ang = pos.astype(f32)[:, None] * inv[None, :]
    cos, sin = jnp.cos(ang), jnp.sin(ang)
    z8 = jnp.zeros((n, ROT_HALF), f32)
    rest = DH - 2 * ROT_HALF
    c64 = jnp.concatenate([cos, cos, jnp.ones((n, rest), f32)], axis=-1)
    sa64 = jnp.concatenate([-sin, z8, jnp.zeros((n, rest), f32)], axis=-1)
    sb64 = jnp.concatenate([z8, sin, jnp.zeros((n, rest), f32)], axis=-1)
    tile = lambda t: jnp.concatenate([t, t], axis=-1)
    return tile(c64), tile(sa64), tile(sb64)


def _rope128(v, cos, sa, sb):
    return v * cos + pltpu.roll(v, LANES - ROT_HALF, 1) * sa + pltpu.roll(v, ROT_HALF, 1) * sb


def _inproj_kernel(x_ref, g_ref, w_ref, cos_ref, sa_ref, sb_ref, wa2_ref, ba_ref,
                   q_ref, kvc_ref, kvs_ref, kvw_ref, ks_ref, vs_ref, kw_ref, vw_ref,
                   gate_ref, zn_ref, gq_ref, gk_ref, gv_ref, la_ref, zg_ref):
    x = x_ref[...]
    ms = jnp.mean(x * x, axis=-1, keepdims=True)
    hn = (x * lax.rsqrt(ms + EPS) * g_ref[...]).astype(bf16)
    cos, sa, sb = cos_ref[...], sa_ref[...], sb_ref[...]

    def proj(off, width):
        return _dot(hn, w_ref[:, off:off + width])

    qp = proj(P_Q, NSA_WIDTH)
    for c in range(NSA_WIDTH // LANES):
        r = _rope128(qp[:, c * LANES:(c + 1) * LANES], cos, sa, sb) * (DH ** -0.5)
        q_ref[0, 2 * c] = r[:, :DH].astype(bf16)
        q_ref[0, 2 * c + 1] = r[:, DH:].astype(bf16)

    kvc_ref[...] = proj(P_KVC, KV_ROW)

    for off, kv_ref, k_ref, v_ref in ((P_KVS, kvs_ref, ks_ref, vs_ref), (P_KVW, kvw_ref, kw_ref, vw_ref)):
        p = proj(off, KV_ROW)
        k = _rope128(p[:, :LANES], cos, sa, sb)
        v = p[:, LANES:]
        kv_ref[:, :LANES] = k
        kv_ref[:, LANES:] = v
        for h in range(KVH):
            k_ref[0, h] = k[:, h * DH:(h + 1) * DH].astype(bf16)
            v_ref[0, h] = v[:, h * DH:(h + 1) * DH].astype(bf16)

    zn_ref[...] = _silu(proj(P_ZN, NSA_WIDTH))
    gq_ref[...] = proj(P_GQ, H_GLA * DK) * (DK ** -0.5)
    gk_ref[...] = proj(P_GK, H_GLA * DK)
    gv_ref[...] = proj(P_GV, H_GLA * DV)
    zg_ref[...] = _silu(proj(P_ZG, GLA_WIDTH))

    misc = proj(P_MISC, LANES)
    gate_ref[...] = _sigmoid(misc)
    xa = _dot(misc.astype(bf16), wa2_ref[...]) + ba_ref[...]
    la_ref[...] = _log_sigmoid(xa) / GLA_TAU


def _pack_w_in(w_in):
    o = IN_OFFSETS
    seg = lambda i: w_in[:, o[i]:o[i + 1]]
    misc = jnp.zeros((D_MODEL, LANES), w_in.dtype)
    misc = misc.at[:, MISC_GATE:MISC_GATE + 3 * H_NSA].set(seg(4))
    misc = misc.at[:, MISC_GLR:MISC_GLR + GLA_LR].set(seg(9))
    cols = [seg(0), seg(1), seg(2), seg(3), seg(5), seg(6), seg(7), seg(8), seg(10), misc]
    return jnp.concatenate(cols, axis=1).astype(bf16)


def _pad_wa2(wa2):
    pad = jnp.zeros((LANES, H_GLA * DK), wa2.dtype).at[MISC_GLR:MISC_GLR + GLA_LR].set(wa2)
    return pad.astype(bf16)


def _inproj(x, pos, norm_g, w_pack, wa2_pad, ba, tm):
    bk, lk, _ = x.shape
    n = bk * lk
    nt = lk // tm
    cos, sa, sb = _rope_tables(pos)
    tok = lambda w: pl.BlockSpec((tm, w), lambda i: (i, 0))
    tab = pl.BlockSpec((tm, LANES), lambda i: (i % nt, 0))
    full = lambda a: pl.BlockSpec(a.shape, lambda i: (0,) * a.ndim)
    hm = lambda h: pl.BlockSpec((1, h, tm, DH), lambda i: (i // nt, 0, i % nt, 0))
    g2 = norm_g.reshape(1, D_MODEL)
    ba2 = ba.reshape(1, H_GLA * DK)
    out_shape = (
        jax.ShapeDtypeStruct((bk, H_NSA, lk, DH), bf16),
        jax.ShapeDtypeStruct((n, KV_ROW), f32),
        jax.ShapeDtypeStruct((n, KV_ROW), f32),
        jax.ShapeDtypeStruct((n, KV_ROW), f32),
        jax.ShapeDtypeStruct((bk, KVH, lk, DH), bf16),
        jax.ShapeDtypeStruct((bk, KVH, lk, DH), bf16),
        jax.ShapeDtypeStruct((bk, KVH, lk, DH), bf16),
        jax.ShapeDtypeStruct((bk, KVH, lk, DH), bf16),
        jax.ShapeDtypeStruct((n, LANES), f32),
        jax.ShapeDtypeStruct((n, NSA_WIDTH), f32),
        jax.ShapeDtypeStruct((n, H_GLA * DK), f32),
        jax.ShapeDtypeStruct((n, H_GLA * DK), f32),
        jax.ShapeDtypeStruct((n, H_GLA * DV), f32),
        jax.ShapeDtypeStruct((n, H_GLA * DK), f32),
        jax.ShapeDtypeStruct((n, GLA_WIDTH), f32),
    )
    out_specs = (hm(H_NSA), tok(KV_ROW), tok(KV_ROW), tok(KV_ROW), hm(KVH), hm(KVH), hm(KVH), hm(KVH),
                 tok(LANES), tok(NSA_WIDTH), tok(H_GLA * DK), tok(H_GLA * DK), tok(H_GLA * DV),
                 tok(H_GLA * DK), tok(GLA_WIDTH))
    return pl.pallas_call(
        _inproj_kernel,
        grid=(n // tm,),
        in_specs=[tok(D_MODEL), full(g2), full(w_pack), tab, tab, tab, full(wa2_pad), full(ba2)],
        out_specs=out_specs,
        out_shape=out_shape,
        compiler_params=_cparams(("parallel",)),
        name="inproj",
    )(x.reshape(n, D_MODEL), g2, w_pack, cos, sa, sb, wa2_pad, ba2)


PT_Q, PT_KVC, PT_KVS, PT_KVW, PT_GATE = 0, 512, 768, 1024, 1280
PT_ROWS = PT_GATE + 32
PK_ZN, PK_GQ, PK_GK, PK_GV, PK_ZG, PK_MISC = 0, 512, 768, 1024, 1536, 2048
PK_COLS = PK_MISC + LANES


def _inproj_prompt_kernel(x_ref, g_ref, wt_ref, w_ref, cos_ref, sin_ref, wa2_ref, ba_ref,
                          qt_ref, kvct_ref, kvst_ref, kvwt_ref, ks_ref, kw_ref, vst_ref, vwt_ref, kvc_ref,
                          gatet_ref, zn_ref, gq_ref, gk_ref, gv_ref, la_ref, zg_ref):
    x = x_ref[...]
    ms = jnp.mean(x * x, axis=-1, keepdims=True)
    hn = (x * lax.rsqrt(ms + EPS) * g_ref[...]).astype(bf16)
    cos_t, sin_t = cos_ref[...], sin_ref[...]

    def proj_t(off, rows):
        return _dot_nt(wt_ref[off:off + rows, :], hn)

    def rope_rows(v):
        x1, x2 = v[:ROT_HALF], v[ROT_HALF:2 * ROT_HALF]
        return jnp.concatenate([x1 * cos_t - x2 * sin_t, x2 * cos_t + x1 * sin_t, v[2 * ROT_HALF:]], axis=0)

    q_t = proj_t(PT_Q, NSA_WIDTH)
    for h in range(H_NSA):
        qt_ref[0, h] = (rope_rows(q_t[h * DH:(h + 1) * DH]) * (DH ** -0.5 * LOG2E)).astype(bf16)

    kvc_t = proj_t(PT_KVC, KV_ROW)
    kvct_ref[0] = kvc_t
    kvc_ref[...] = kvc_t.T.astype(bf16)

    for off, kvt_ref, k_ref, vt_ref in ((PT_KVS, kvst_ref, ks_ref, vst_ref), (PT_KVW, kvwt_ref, kw_ref, vwt_ref)):
        t = proj_t(off, KV_ROW)
        k_t = jnp.concatenate([rope_rows(t[h * DH:(h + 1) * DH]) for h in range(KVH)], axis=0)
        kvt_ref[0, :LANES] = k_t
        kvt_ref[0, LANES:] = t[LANES:]
        k_tok = k_t.T
        for h in range(KVH):
            k_ref[0, h] = k_tok[:, h * DH:(h + 1) * DH].astype(bf16)
            vt_ref[0, h] = t[LANES + h * DH:LANES + (h + 1) * DH].astype(bf16)

    gatet_ref[0] = _sigmoid(proj_t(PT_GATE, PT_ROWS - PT_GATE))

    def proj(off, width):
        return _dot(hn, w_ref[:, off:off + width])

    zn_ref[...] = _silu(proj(PK_ZN, NSA_WIDTH))
    gq_ref[...] = proj(PK_GQ, H_GLA * DK) * (DK ** -0.5)
    gk_ref[...] = proj(PK_GK, H_GLA * DK)
    gv_ref[...] = proj(PK_GV, H_GLA * DV)
    zg_ref[...] = _silu(proj(PK_ZG, GLA_WIDTH))
    misc = proj(PK_MISC, LANES)
    xa = _dot(misc.astype(bf16), wa2_ref[...]) + ba_ref[...]
    la_ref[...] = _log_sigmoid(xa) / GLA_TAU


def _pack_w_in_prompt(w_in):
    o = IN_OFFSETS
    seg = lambda i: w_in[:, o[i]:o[i + 1]]
    gate_t = jnp.zeros((PT_ROWS - PT_GATE, D_MODEL), w_in.dtype).at[:3 * H_NSA].set(seg(4).T)
    w_t = jnp.concatenate([seg(0).T, seg(1).T, seg(2).T, seg(3).T, gate_t], axis=0).astype(bf16)
    misc = jnp.zeros((D_MODEL, LANES), w_in.dtype).at[:, MISC_GLR:MISC_GLR + GLA_LR].set(seg(9))
    w_tok = jnp.concatenate([seg(5), seg(6), seg(7), seg(8), seg(10), misc], axis=1).astype(bf16)
    return w_t, w_tok


def _inproj_prompt(x, norm_g, w_t, w_tok, wa2_pad, ba, tm):
    b, lq, _ = x.shape
    n = b * lq
    nt = lq // tm
    inv = ROPE_THETA ** (-(jnp.arange(ROT_HALF, dtype=f32) / ROT_HALF))
    ang = inv[:, None] * jnp.arange(lq).astype(f32)[None, :]
    cos_t, sin_t = jnp.cos(ang), jnp.sin(ang)
    tok = lambda w: pl.BlockSpec((tm, w), lambda i: (i, 0))
    tab = pl.BlockSpec((ROT_HALF, tm), lambda i: (0, i % nt))
    full = lambda a: pl.BlockSpec(a.shape, lambda i: (0,) * a.ndim)
    feat = lambda r: pl.BlockSpec((1, r, tm), lambda i: (i // nt, 0, i % nt))
    headf = lambda h: pl.BlockSpec((1, h, DH, tm), lambda i: (i // nt, 0, 0, i % nt))
    headt = lambda h: pl.BlockSpec((1, h, tm, DH), lambda i: (i // nt, 0, i % nt, 0))
    g2 = norm_g.reshape(1, D_MODEL)
    ba2 = ba.reshape(1, H_GLA * DK)
    sds = jax.ShapeDtypeStruct
    out_shape = (
        sds((b, H_NSA, DH, lq), bf16),
        sds((b, KV_ROW, lq), f32), sds((b, KV_ROW, lq), f32), sds((b, KV_ROW, lq), f32),
        sds((b, KVH, lq, DH), bf16), sds((b, KVH, lq, DH), bf16),
        sds((b, KVH, DH, lq), bf16), sds((b, KVH, DH, lq), bf16),
        sds((n, KV_ROW), bf16),
        sds((b, PT_ROWS - PT_GATE, lq), f32),
        sds((n, NSA_WIDTH), f32), sds((n, H_GLA * DK), f32), sds((n, H_GLA * DK), f32),
        sds((n, H_GLA * DV), f32), sds((n, H_GLA * DK), f32), sds((n, GLA_WIDTH), f32),
    )
    out_specs = (headf(H_NSA), feat(KV_ROW), feat(KV_ROW), feat(KV_ROW), headt(KVH), headt(KVH),
                 headf(KVH), headf(KVH), tok(KV_ROW), feat(PT_ROWS - PT_GATE),
                 tok(NSA_WIDTH), tok(H_GLA * DK), tok(H_GLA * DK), tok(H_GLA * DV), tok(H_GLA * DK),
                 tok(GLA_WIDTH))
    return pl.pallas_call(
        _inproj_prompt_kernel,
        grid=(n // tm,),
        in_specs=[tok(D_MODEL), full(g2), full(w_t), full(w_tok), tab, tab, full(wa2_pad), full(ba2)],
        out_specs=out_specs,
        out_shape=out_shape,
        compiler_params=_cparams(("parallel",)),
        name="inproj_prompt",
    )(x.reshape(n, D_MODEL), g2, w_t, w_tok, cos_t, sin_t, wa2_pad, ba2)


CHUNK_W = CMP_STRIDE * KV_ROW
PART_W = 2 * 2 * KVH * CMP_HID
HID_W = 2 * KVH * CMP_HID


def _cmp_w1_big(cmp_w1):
    w1r = cmp_w1.reshape(2, CMP_LEN // CMP_STRIDE, CMP_STRIDE, DH, CMP_HID)
    eye = jnp.eye(2, dtype=cmp_w1.dtype)
    big = jnp.einsum('crsdh,cC,kK->sCKdrckh', w1r, eye, eye)
    return big.reshape(CHUNK_W, PART_W).astype(bf16)


def _cmp_w2_big(cmp_w2):
    eye = jnp.eye(2, dtype=cmp_w2.dtype)
    big = jnp.einsum('chd,cC,kK->ckhCKd', cmp_w2, eye, eye)
    return big.reshape(HID_W, KV_ROW).astype(bf16)


def _cmp_part_kernel(x_ref, w_ref, o_ref):
    o_ref[...] = _dot(x_ref[...].astype(bf16), w_ref[...])


def _cmp_part(chunks, w1_big, tr):
    r = chunks.shape[0]
    return pl.pallas_call(
        _cmp_part_kernel,
        grid=(r // tr,),
        in_specs=[pl.BlockSpec((tr, CHUNK_W), lambda i: (i, 0)),
                  pl.BlockSpec((CHUNK_W, PART_W), lambda i: (0, 0))],
        out_specs=pl.BlockSpec((tr, PART_W), lambda i: (i, 0)),
        out_shape=jax.ShapeDtypeStruct((r, PART_W), f32),
        compiler_params=_cparams(("parallel",)),
        name="cmp_part",
    )(chunks, w1_big)


def _cmp_bias_kernel(pe_ref, w1_ref, b1_ref, o_ref):
    for c in range(2):
        o_ref[c] = _dot(pe_ref[c], w1_ref[c]) + b1_ref[c]


def _cmp_bias(cmp_pe, cmp_w1, cmp_b1):
    pe = jnp.broadcast_to(cmp_pe.reshape(2, 1, CMP_LEN * DH), (2, 8, CMP_LEN * DH))
    b1 = jnp.broadcast_to(cmp_b1.reshape(2, 1, CMP_HID), (2, 8, CMP_HID))
    out = pl.pallas_call(
        _cmp_bias_kernel,
        out_shape=jax.ShapeDtypeStruct((2, 8, CMP_HID), f32),
        name="cmp_bias",
    )(pe, cmp_w1, b1)
    row = out[:, 0, :]
    return jnp.broadcast_to(row[:, None, :], (2, KVH, CMP_HID)).reshape(1, HID_W)


def _compress_finish(part, bias, w2_big, cos, sa, sb):
    n = part.shape[0]
    pre = part[:, :HID_W] + pltpu.roll(part[:, HID_W:], n - 1, 0)
    h = _silu(pre + bias)
    out = _dot(h.astype(bf16), w2_big)
    return _rope128(out[:, :LANES], cos, sa, sb), out[:, LANES:]


def _masked_softmax(s, mask, axis, exp=jnp.exp):
    s = jnp.where(mask, s, NEG)
    e = exp(s - jnp.max(s, axis=axis, keepdims=True))
    return jnp.where(mask, e * (1.0 / jnp.sum(e, axis=axis, keepdims=True)), 0.0)


def _split_bf16(x):
    hi = x.astype(bf16)
    return hi, (x - hi.astype(f32)).astype(bf16)


def _topk_mask(score, k, axis):
    n = score.shape[axis]
    idx = lax.broadcasted_iota(jnp.int32, score.shape, axis)
    sel = jnp.zeros(score.shape, f32)
    for _ in range(k):
        m = jnp.max(score, axis=axis, keepdims=True)
        first = jnp.min(jnp.where(score == m, idx, n), axis=axis, keepdims=True)
        pick = idx == first
        sel = jnp.where(pick, 1.0, sel)
        score = jnp.where(pick, NEG, score)
    return sel


def _topk_rows_by_rank(score, k):
    r, n = score.shape
    cols = jnp.concatenate([score, jnp.zeros((n - r, n), f32)], axis=0).T
    earlier = jnp.where(lax.broadcasted_iota(jnp.int32, (n, n), 0) < lax.broadcasted_iota(jnp.int32, (n, n), 1),
                        1.0, 0.0)
    rows = []
    for c in range(r):
        col, row = cols[:, c:c + 1], score[c:c + 1, :]
        before = jnp.where(col > row, 1.0, jnp.where(col == row, earlier, 0.0))
        rows.append(jnp.where(jnp.sum(before, axis=0, keepdims=True) < k, 1.0, 0.0))
    return jnp.concatenate(rows, axis=0)


def _cover_t(n_slc, n_chunk):
    start = np.arange(n_chunk)[None, :] * CMP_STRIDE
    j = np.arange(n_slc)[:, None]
    cov = (start < (j + 1) * SLC_LEN) & (start + CMP_LEN > j * SLC_LEN)
    return jnp.asarray(cov.astype(np.float32), dtype=bf16)


def _cmp_attn_kernel(part_ref, bias_ref, w2_ref, cos_ref, sa_ref, sb_ref, qt_ref, gatet_ref, cov_ref,
                     ot_ref, selt_ref, kc_ref, vct_ref, *, tq):
    t = pl.program_id(1)

    @pl.when(t == 0)
    def _():
        k, v = _compress_finish(part_ref[0], bias_ref[...], w2_ref[...], cos_ref[...], sa_ref[...], sb_ref[...])
        v_t = v.T
        for h in range(KVH):
            kc_ref[h] = k[:, h * DH:(h + 1) * DH].astype(bf16)
            vct_ref[h] = v_t[h * DH:(h + 1) * DH].astype(bf16)

    nc = kc_ref.shape[1]
    nb = cov_ref.shape[0]
    qpos = t * tq + lax.broadcasted_iota(jnp.int32, (1, tq), 1)
    end_pos = lax.broadcasted_iota(jnp.int32, (nc, 1), 0) * CMP_STRIDE + (CMP_LEN - 1)
    mask_t = end_pos <= qpos
    gate_t = gatet_ref[0]
    cov = cov_ref[...]

    jblk = lax.broadcasted_iota(jnp.int32, (nb, tq), 0)
    qblk = qpos // SLC_LEN
    valid = jblk <= qblk
    forced = (jblk == 0) | (jblk == qblk) | (jblk == qblk - 1)

    scores = []
    for kh in range(KVH):
        kc, vc_t = kc_ref[kh], vct_ref[kh]
        psum = jnp.zeros((nc, tq), f32)
        for g in range(G_NSA):
            h = kh * G_NSA + g
            p_t = _masked_softmax(_dot(kc, qt_ref[0, h]), mask_t, axis=0, exp=jnp.exp2)
            ot_ref[0, h * DH:(h + 1) * DH] = _dot(vc_t, p_t.astype(bf16)) * gate_t[h:h + 1]
            psum = psum + p_t
        hi, lo = _split_bf16(psum)
        imp = _dot(cov, hi) + _dot(cov, lo)
        scores.append(jnp.where(valid, jnp.where(forced, FORCE_SCORE, imp), -1.0))
    picked = _topk_mask(jnp.concatenate(scores, axis=1), SLC_TOP, axis=0)
    for kh in range(KVH):
        sel_t = jnp.where(valid & (picked[:, kh * tq:(kh + 1) * tq] > 0.5), 0.0, NEG)
        if nb < LANES:
            sel_t = jnp.concatenate([sel_t, jnp.full((LANES - nb, tq), NEG, f32)], axis=0)
        selt_ref[0, kh] = sel_t


def _cmp_attn(part, bias, w2_big, q_t, gate_t, tq):
    b, nc, _ = part.shape
    lq = q_t.shape[3]
    n_slc = lq // SLC_LEN
    end_pos = jnp.arange(nc) * CMP_STRIDE + (CMP_LEN - 1)
    cos, sa, sb = _rope_tables(end_pos)
    cov = _cover_t(n_slc, nc)
    full = lambda a: pl.BlockSpec(a.shape, lambda i, t: (0,) * a.ndim)
    return pl.pallas_call(
        functools.partial(_cmp_attn_kernel, tq=tq),
        grid=(b, lq // tq),
        in_specs=[pl.BlockSpec((1, nc, PART_W), lambda i, t: (i, 0, 0)),
                  full(bias), full(w2_big), full(cos), full(sa), full(sb),
                  pl.BlockSpec((1, H_NSA, DH, tq), lambda i, t: (i, 0, 0, t)),
                  pl.BlockSpec((1, gate_t.shape[1], tq), lambda i, t: (i, 0, t)),
                  full(cov)],
        out_specs=(pl.BlockSpec((1, NSA_WIDTH, tq), lambda i, t: (i, 0, t)),
                   pl.BlockSpec((1, KVH, LANES, tq), lambda i, t: (i, 0, 0, t))),
        out_shape=(jax.ShapeDtypeStruct((b, NSA_WIDTH, lq), f32),
                   jax.ShapeDtypeStruct((b, KVH, LANES, lq), f32)),
        scratch_shapes=[pltpu.VMEM((KVH, nc, DH), bf16), pltpu.VMEM((KVH, DH, nc), bf16)],
        compiler_params=_cparams(("parallel", "arbitrary")),
        name="cmp_attn",
    )(part, bias, w2_big, cos, sa, sb, q_t, gate_t, cov)


def _block_expand(n_slc_pad, n_keys):
    e = (np.arange(n_keys)[None, :] // SLC_LEN) == np.arange(n_slc_pad)[:, None]
    return jnp.asarray(e.astype(np.float32), dtype=bf16)


def _mask_bias_t(allowed):
    bias = jnp.where(allowed, 0.0, NEG)
    return jnp.concatenate([bias] * G_NSA, axis=1)


def _sublane_fold(x, op):
    return functools.reduce(op, [x[r:r + 8] for r in range(0, x.shape[0], 8)])


def _slc_win_kernel(qt_ref, ks_ref, vst_ref, kw_ref, vwt_ref, selt_ref, gatet_ref, ocmpt_ref, zn_ref,
                    o_ref, s_a, s_b, m_a, m_b, ws_a, ws_b, wm_a, wm_b, acc_ref, wacc_ref, *, tq, tk, nt):
    t = pl.program_id(2)

    @pl.when(t == 0)
    def _():
        m_b[...] = jnp.full(m_b.shape, NEG, f32)
        ws_b[...] = jnp.zeros(ws_b.shape, f32)
        wm_b[...] = jnp.zeros(wm_b.shape, f32)

    step = functools.partial(_slc_win_step, qt_ref, ks_ref, vst_ref, kw_ref, vwt_ref, selt_ref, gatet_ref,
                             ocmpt_ref, zn_ref, o_ref, acc_ref, wacc_ref, tq=tq, tk=tk, nt=nt)

    @pl.when(t % 2 == 0)
    def _():
        step(s_a, s_b, m_a, m_b, ws_a, ws_b, wm_a, wm_b)

    @pl.when(t % 2 == 1)
    def _():
        step(s_b, s_a, m_b, m_a, ws_b, ws_a, wm_b, wm_a)


ONES_ROWS = 16


def _slc_win_step(qt_ref, ks_ref, vst_ref, kw_ref, vwt_ref, selt_ref, gatet_ref, ocmpt_ref, zn_ref, o_ref,
                  acc_ref, wacc_ref, s_cur, s_prev, m_cur, m_last, ws_cur, ws_prev, wm_cur, wm_last,
                  *, tq, tk, nt):
    kh = pl.program_id(1)
    t = pl.program_id(2)
    lq = ks_ref.shape[2]
    band = WINDOW + tq
    blocks_per_tile = tk // SLC_LEN
    key_tiles = lambda tile: (tile * tq + tq + tk - 1) // tk
    n_score = jnp.where(t < nt, key_tiles(t), 0)
    n_apply = jnp.where(t >= 1, key_tiles(t - 1), 0)
    n_plain = jnp.where(t < nt, (t * tq) // tk, 0)
    q0 = t * tq
    qpos = q0 + lax.broadcasted_iota(jnp.int32, (1, tq), 1)
    q_t = jnp.concatenate([qt_ref[0, g] for g in range(G_NSA)], axis=1)
    ones = jnp.ones((ONES_ROWS, tk), bf16)

    m_cur[...] = jnp.full(m_cur.shape, NEG, f32)
    m_prev = jnp.max(m_last[...], axis=0, keepdims=True)
    acc_ref[...] = jnp.zeros(acc_ref.shape, f32)

    def score(kt, causal):
        k0 = pl.multiple_of(kt * tk, tk)
        j0 = pl.multiple_of(kt * blocks_per_tile, blocks_per_tile)
        sel = selt_ref[0, 0, pl.ds(j0, blocks_per_tile), :]
        bias = jnp.concatenate([jnp.broadcast_to(sel[j:j + 1], (SLC_LEN, tq)) for j in range(blocks_per_tile)],
                               axis=0)
        if causal:
            kpos = k0 + lax.broadcasted_iota(jnp.int32, (tk, 1), 0)
            bias = jnp.where(kpos <= qpos, bias, NEG)
        s = _dot(ks_ref[0, 0, pl.ds(k0, tk), :], q_t) + jnp.concatenate([bias] * G_NSA, axis=1)
        s_cur[pl.ds(k0, tk), :] = s
        m_cur[...] = jnp.maximum(m_cur[...], _sublane_fold(s, jnp.maximum))

    def apply(kt):
        k0 = pl.multiple_of(kt * tk, tk)
        p = jnp.exp2(s_prev[pl.ds(k0, tk), :] - m_prev).astype(bf16)
        acc_ref[...] += _dot(jnp.concatenate([vst_ref[0, 0, :, pl.ds(k0, tk)], ones], axis=0), p)

    def both_body(kt, carry):
        score(kt, causal=False)
        apply(kt)
        return carry

    def score_body(kt, carry):
        score(kt, causal=True)
        return carry

    def apply_body(kt, carry):
        apply(kt)
        return carry

    n_both = jnp.minimum(n_plain, n_apply)
    lax.fori_loop(0, n_both, both_body, 0)
    lax.fori_loop(n_both, n_score, score_body, 0)
    lax.fori_loop(n_both, n_apply, apply_body, 0)

    w0 = pl.multiple_of(jnp.clip(q0 - WINDOW, 0, lq - band), tq)
    w0_prev = pl.multiple_of(jnp.clip(q0 - tq - WINDOW, 0, lq - band), tq)
    wm_prev = jnp.max(wm_last[...], axis=0, keepdims=True)
    ones_w = jnp.ones((ONES_ROWS, tq), bf16)
    wmax, wacc = None, None
    for j in range(band // tq):
        rows = pl.ds(j * tq, tq)
        kpos = w0 + j * tq + lax.broadcasted_iota(jnp.int32, (tq, 1), 0)
        s = (_dot(kw_ref[0, 0, pl.ds(pl.multiple_of(w0 + j * tq, tq), tq), :], q_t)
             + _mask_bias_t((kpos <= qpos) & (kpos > qpos - WINDOW)))
        ws_cur[rows, :] = s
        fold = _sublane_fold(s, jnp.maximum)
        wmax = fold if wmax is None else jnp.maximum(wmax, fold)
        p = jnp.exp2(ws_prev[rows, :] - wm_prev).astype(bf16)
        v_ext = jnp.concatenate([vwt_ref[0, 0, :, pl.ds(pl.multiple_of(w0_prev + j * tq, tq), tq)], ones_w], axis=0)
        part = _dot(v_ext, p)
        wacc = part if wacc is None else wacc + part
    wm_cur[...] = wmax
    wacc_ref[...] = wacc

    @pl.when(t >= 1)
    def _():
        o_slc = acc_ref[:DH] / acc_ref[DH:DH + 1]
        o_win = wacc_ref[:DH] / wacc_ref[DH:DH + 1]
        gate_t = gatet_ref[0]
        row = lambda c, g: jnp.where(kh == 0, gate_t[c * H_NSA + g:c * H_NSA + g + 1],
                                     gate_t[c * H_NSA + G_NSA + g:c * H_NSA + G_NSA + g + 1])
        g_slc = jnp.concatenate([row(1, g) for g in range(G_NSA)], axis=1)
        g_win = jnp.concatenate([row(2, g) for g in range(G_NSA)], axis=1)
        o_t = g_slc * o_slc + g_win * o_win
        o_heads = jnp.concatenate([o_t[:, g * tq:(g + 1) * tq] for g in range(G_NSA)], axis=0)
        o_ref[0] = ((ocmpt_ref[0] + o_heads).T * zn_ref[...]).astype(o_ref.dtype)


def _slc_win(q_t, ks, vs_t, kw, vw_t, sel_t, gate_t, o_cmp_t, zn, tq, tk):
    b, _, _, lq = q_t.shape
    assert lq >= WINDOW + tq and lq % tk == 0 and tk % (8 * SLC_LEN) == 0
    gw = G_NSA * DH
    k_spec = pl.BlockSpec((1, 1, lq, DH), lambda i, k, t: (i, k, 0, 0))
    vt_spec = pl.BlockSpec((1, 1, DH, lq), lambda i, k, t: (i, k, 0, 0))
    lanes = G_NSA * tq
    band = WINDOW + tq
    nt = lq // tq
    scored = lambda t: jnp.minimum(t, nt - 1)
    drained = lambda t: jnp.maximum(t - 1, 0)
    return pl.pallas_call(
        functools.partial(_slc_win_kernel, tq=tq, tk=tk, nt=nt),
        grid=(b, KVH, nt + 1),
        in_specs=[pl.BlockSpec((1, G_NSA, DH, tq), lambda i, k, t: (i, k, 0, scored(t))),
                  k_spec, vt_spec, k_spec, vt_spec,
                  pl.BlockSpec((1, 1, LANES, tq), lambda i, k, t: (i, k, 0, scored(t))),
                  pl.BlockSpec((1, gate_t.shape[1], tq), lambda i, k, t: (i, 0, drained(t))),
                  pl.BlockSpec((1, gw, tq), lambda i, k, t: (i, k, drained(t))),
                  pl.BlockSpec((tq, gw), lambda i, k, t: (i * nt + drained(t), k))],
        out_specs=pl.BlockSpec((1, tq, gw), lambda i, k, t: (i, drained(t), k)),
        out_shape=jax.ShapeDtypeStruct((b, lq, NSA_WIDTH), bf16),
        scratch_shapes=[pltpu.VMEM((lq, lanes), f32), pltpu.VMEM((lq, lanes), f32),
                        pltpu.VMEM((8, lanes), f32), pltpu.VMEM((8, lanes), f32),
                        pltpu.VMEM((band, lanes), f32), pltpu.VMEM((band, lanes), f32),
                        pltpu.VMEM((8, lanes), f32), pltpu.VMEM((8, lanes), f32),
                        pltpu.VMEM((DH + ONES_ROWS, lanes), f32),
                        pltpu.VMEM((DH + ONES_ROWS, lanes), f32)],
        compiler_params=_cparams(("parallel", "parallel", "arbitrary")),
        name="slc_win",
    )(q_t, ks, vs_t, kw, vw_t, sel_t, gate_t, o_cmp_t, zn)


def _rms_gain(o, gain):
    return o * lax.rsqrt(jnp.mean(o * o, axis=-1, keepdims=True) + EPS) * gain


def _gla_kernel(q_ref, k_ref, v_ref, la_ref, z_ref, gain_ref, s0_ref, y_ref, st_ref, s_scr, *, tl, chunk):
    t = pl.program_id(1)

    @pl.when(t == 0)
    def _():
        s_scr[...] = s0_ref[0]

    c = chunk
    pairs = s_scr.shape[0]
    row = lax.broadcasted_iota(jnp.int32, (c, c), 0)
    col = lax.broadcasted_iota(jnp.int32, (c, c), 1)
    causal = col <= row
    tril = jnp.where(causal, 1.0, 0.0).astype(bf16)
    lane = lax.broadcasted_iota(jnp.int32, (1, LANES), 1)
    head_mask = [jnp.where(lane < DK, 1.0, 0.0), jnp.where(lane >= DK, 1.0, 0.0)]
    gain = gain_ref[...]

    chunks = []
    for ci, pr in [(ci, pr) for ci in range(tl // c) for pr in range(pairs)]:
        rows = slice(ci * c, (ci + 1) * c)
        grp = slice(pr * LANES, (pr + 1) * LANES)
        q, k, la = q_ref[rows, grp], k_ref[rows, grp], la_ref[rows, grp]
        hi, rest = la.astype(bf16), la - la.astype(bf16).astype(f32)
        mid, lo = rest.astype(bf16), (rest - rest.astype(bf16).astype(f32)).astype(bf16)
        bc = _dot(tril, hi) + _dot(tril, mid) + _dot(tril, lo)
        ref_row = bc[c // 2 - 1:c // 2, :]
        b_last = bc[c - 1:c, :]
        q_a = q * jnp.exp(bc - ref_row)
        k_a = (k * jnp.exp(ref_row - bc)).astype(bf16)
        q_s = q * jnp.exp(bc)
        k_s = k * jnp.exp(b_last - bc)
        o_intra, q_inter, add = [], [], None
        for h in range(2):
            mh = head_mask[h]
            vh = v_ref[rows, (2 * pr + h) * DV:(2 * pr + h + 1) * DV].astype(bf16)
            a = jnp.where(causal, _dot_nt((q_a * mh).astype(bf16), k_a), 0.0)
            o_intra.append(_dot(a.astype(bf16), vh))
            q_inter.append((q_s * mh).astype(bf16))
            u = _dot_tn(vh, (k_s * mh).astype(bf16))
            add = u if add is None else add + u
        chunks.append((rows, pr, jnp.exp(b_last), add, o_intra, q_inter))

    st = [s_scr[pr] for pr in range(pairs)]
    for rows, pr, decay, add, o_intra, q_inter in chunks:
        st_b = st[pr].astype(bf16)
        for h in range(2):
            cols = slice((2 * pr + h) * DV, (2 * pr + h + 1) * DV)
            o = o_intra[h] + _dot_nt(q_inter[h], st_b)
            y_ref[rows, cols] = (_rms_gain(o, gain) * z_ref[rows, cols]).astype(y_ref.dtype)
        st[pr] = st[pr] * decay + add
    for pr in range(pairs):
        s_scr[pr] = st[pr]
        st_ref[0, pr] = st[pr]


def _gla(gq, gk, gv, la, zg, gain, state0, b, tl, chunk):
    n = gq.shape[0]
    nt = n // b // tl
    pairs = H_GLA // 2
    qk = pl.BlockSpec((tl, H_GLA * DK), lambda i, t: (i * nt + t, 0))
    vz = pl.BlockSpec((tl, GLA_WIDTH), lambda i, t: (i * nt + t, 0))
    st_spec = pl.BlockSpec((1, pairs, DV, LANES), lambda i, t: (i, 0, 0, 0))
    gain2 = gain.reshape(1, DV)
    s0 = state0.reshape(b, pairs, 2, DK, DV).transpose(0, 1, 4, 2, 3).reshape(b, pairs, DV, LANES)
    y, st = pl.pallas_call(
        functools.partial(_gla_kernel, tl=tl, chunk=chunk),
        grid=(b, nt),
        in_specs=[qk, qk, vz, qk, vz, pl.BlockSpec((1, DV), lambda i, t: (0, 0)), st_spec],
        out_specs=(vz, st_spec),
        out_shape=(jax.ShapeDtypeStruct((n, GLA_WIDTH), bf16),
                   jax.ShapeDtypeStruct((b, pairs, DV, LANES), f32)),
        scratch_shapes=[pltpu.VMEM((pairs, DV, LANES), f32)],
        compiler_params=_cparams(("parallel", "arbitrary")),
        name="gla",
    )(gq, gk, gv, la, zg, gain2, s0)
    state = st.reshape(b, pairs, DV, 2, DK).transpose(0, 1, 3, 4, 2).reshape(b, H_GLA, DK, DV)
    return y, state


def _outproj_kernel(x_ref, yn_ref, yg_ref, w_ref, g_ref, y_ref):
    mix = _dot(yn_ref[...].astype(bf16), w_ref[:NSA_WIDTH, :]) + _dot(yg_ref[...].astype(bf16), w_ref[NSA_WIDTH:, :])
    y_ref[...] = _rms_gain(x_ref[...] + mix, g_ref[...])


def _outproj(x, y_nsa, y_gla, w_out, gain, tm):
    n = x.shape[0]
    tok = lambda w: pl.BlockSpec((tm, w), lambda i: (i, 0))
    g2 = gain.reshape(1, D_MODEL)
    return pl.pallas_call(
        _outproj_kernel,
        grid=(n // tm,),
        in_specs=[tok(D_MODEL), tok(NSA_WIDTH), tok(GLA_WIDTH),
                  pl.BlockSpec(w_out.shape, lambda i: (0, 0)), pl.BlockSpec(g2.shape, lambda i: (0, 0))],
        out_specs=tok(D_MODEL),
        out_shape=jax.ShapeDtypeStruct((n, D_MODEL), f32),
        compiler_params=_cparams(("parallel",)),
        name="outproj",
    )(x, y_nsa, y_gla, w_out, g2)


def _prompt_path(x, wts, w_t, w_tok):
    (norm_g, _, wa2_pad, ba, w1_big, cmp_bias, w2_big, gla_gain, w_out, out_gain) = wts
    b, lq, _ = x.shape
    (q_t, kvc_t, kvs_t, kvw_t, ks, kw, vs_t, vw_t, kvc_tok, gate_t, zn, gq, gk, gv, la, zg) = _inproj_prompt(
        x, norm_g, w_t, w_tok, wa2_pad, ba, tm=512)
    n_chunk = lq // CMP_STRIDE
    part = _cmp_part(kvc_tok.reshape(b * n_chunk, CHUNK_W), w1_big, tr=n_chunk)
    o_cmp_t, sel_t = _cmp_attn(part.reshape(b, n_chunk, PART_W), cmp_bias, w2_big, q_t, gate_t, tq=512)
    y_nsa = _slc_win(q_t, ks, vs_t, kw, vw_t, sel_t, gate_t, o_cmp_t, zn, tq=256, tk=512)
    y_gla, state = _gla(gq, gk, gv, la, zg, gla_gain, jnp.zeros((b, H_GLA, DK, DV), f32), b, tl=512,
                        chunk=GLA_CHUNK)
    y = _outproj(x.reshape(b * lq, D_MODEL), y_nsa.reshape(b * lq, NSA_WIDTH), y_gla, w_out, out_gain, tm=512)
    kv5 = lambda a: a.reshape(b, 2, KVH, DH, -1).transpose(0, 4, 1, 2, 3)
    wlen = min(WINDOW, lq)
    return (y.reshape(b, lq, D_MODEL), kv5(kvc_t), kv5(kvs_t), kv5(kvw_t[:, :, lq - wlen:]), state)


def _pages_native(cache):
    n_pool = cache.shape[0]
    return cache.transpose(0, 2, 3, 4, 1).reshape(n_pool, 2, KVH * DH, cache.shape[1])


def _page_gather(make_copies):
    i = pl.program_id(0)
    slot = i % 2

    @pl.when(i == 0)
    def _():
        for c in make_copies(0, 0):
            c.start()

    @pl.when(i + 1 < pl.num_programs(0))
    def _():
        for c in make_copies(i + 1, 1 - slot):
            c.start()

    for c in make_copies(i, slot):
        c.wait()
    return slot


def _cmp_w1_pairs(cmp_w1):
    ratio = CMP_LEN // CMP_STRIDE
    w1r = cmp_w1.reshape(2, ratio, CMP_STRIDE // 2, 2, DH, CMP_HID)
    eye = jnp.eye(KVH, dtype=cmp_w1.dtype)
    big = jnp.einsum('crpjdh,kK->cpjKdrkh', w1r, eye)
    return big.reshape(2, CMP_STRIDE // 2, 2 * KVH * DH, ratio * KVH * CMP_HID).astype(bf16)


def _cmp_w2_heads(cmp_w2):
    eye = jnp.eye(KVH, dtype=cmp_w2.dtype)
    return jnp.einsum('chd,kK->ckhKd', cmp_w2, eye).reshape(2, KVH * CMP_HID, KVH * DH).astype(bf16)


def _group_sum_matrix(n_tok):
    r = np.arange(KVH * n_tok)[:, None]
    c = np.arange(H_NSA * n_tok)[None, :]
    m = ((c // (G_NSA * n_tok)) == (r // n_tok)) & ((c % n_tok) == (r % n_tok))
    return m.astype(np.float32)


def _cmp_attn_sample_kernel(pt_ref, cache_ref, perm_ref, w1_ref, bias_ref, w2_ref, cos_ref, sa_ref, sb_ref, q_ref,
                            gate_ref, cov_ref, gs_ref, gst_ref, o_ref, sel_ref, buf, rows_scr, sem,
                            *, n_pages, n_tok, past_len):
    def copies(seq, slot):
        return [pltpu.make_async_copy(cache_ref.at[pt_ref[seq, p]], buf.at[slot, p], sem.at[slot])
                for p in range(n_pages)]

    slot = _page_gather(copies)

    cpp = PAGE_SIZE // CMP_STRIDE
    perm = perm_ref[...]

    group = 8
    kd = KVH * DH

    def to_rows(g, carry):
        for u in range(group):
            p = g * group + u
            r0 = pl.multiple_of(p * cpp, cpp)
            t = _dot_nt(perm, buf[slot, p].reshape(2 * kd, PAGE_SIZE).astype(bf16))
            for c in range(2):
                for s in range(CMP_STRIDE):
                    rows_scr[c, s, pl.ds(r0, cpp), :] = t[s * cpp:(s + 1) * cpp, c * kd:(c + 1) * kd]
        return carry

    lax.fori_loop(0, n_pages // group, to_rows, 0)

    nc = n_pages * cpp
    kv = []
    for c in range(2):
        part = jnp.zeros((nc, w1_ref.shape[-1]), f32)
        for pr in range(CMP_STRIDE // 2):
            lhs = jnp.concatenate([rows_scr[c, 2 * pr], rows_scr[c, 2 * pr + 1]], axis=-1)
            part = part + _dot(lhs.astype(bf16), w1_ref[c, pr])
        half = part.shape[1] // 2
        pre = part[:, :half] + pltpu.roll(part[:, half:], nc - 1, 0)
        hid = _silu(pre + bias_ref[:, c * half:(c + 1) * half])
        kv.append(_dot(hid.astype(bf16), w2_ref[c]))
    k = _rope128(kv[0], cos_ref[...], sa_ref[...], sb_ref[...])
    v = kv[1]
    rows = H_NSA * n_tok
    qpos = past_len + lax.broadcasted_iota(jnp.int32, (rows, 1), 0) % n_tok
    end_c = lax.broadcasted_iota(jnp.int32, (1, nc), 1) * CMP_STRIDE + (CMP_LEN - 1)
    mask = end_c <= qpos
    p = _masked_softmax(_dot_nt(q_ref[0], k.astype(bf16)), mask, axis=-1)
    o_ref[0] = _dot(p.astype(bf16), v.astype(bf16)) * gate_ref[0][:, 0:1]
    hi, lo = _split_bf16(p)
    gs = gs_ref[...]
    ph, pl_ = _split_bf16(_dot(gs, hi) + _dot(gs, lo))
    imp = _dot(ph, cov_ref[...]) + _dot(pl_, cov_ref[...])
    n_blk = imp.shape[1]
    jblk = lax.broadcasted_iota(jnp.int32, imp.shape, 1)
    forced = (jblk == 0) | (jblk == n_blk - 1)
    sel = _topk_rows_by_rank(jnp.where(forced, FORCE_SCORE, imp), SLC_TOP - 1)
    sel_ref[0] = _dot(gst_ref[...], sel.astype(bf16)).astype(bf16)


def _cmp_attn_sample(cache_t, page_table, w1_pairs, bias, w2_heads, q_blk, gate_rows, n_tok):
    b, n_pages = page_table.shape
    past_len = n_pages * PAGE_SIZE
    nc = past_len // CMP_STRIDE
    rows = H_NSA * n_tok
    n_blk = past_len // SLC_LEN
    end_pos = jnp.arange(nc) * CMP_STRIDE + (CMP_LEN - 1)
    cos, sa, sb = _rope_tables(end_pos)
    cov = _cover_t(n_blk, nc).T
    gs = jnp.asarray(_group_sum_matrix(n_tok), dtype=bf16)
    gst = gs.T
    r = np.arange(PAGE_SIZE)
    cpp = PAGE_SIZE // CMP_STRIDE
    perm = jnp.asarray((r[None, :] == (r[:, None] % cpp) * CMP_STRIDE + r[:, None] // cpp).astype(np.float32),
                       dtype=bf16)
    full = lambda a: pl.BlockSpec(a.shape, lambda i, pt: (0,) * a.ndim)
    seq = lambda a: pl.BlockSpec((1,) + a.shape[1:], lambda i, pt: (i,) + (0,) * (a.ndim - 1))
    grid_spec = pltpu.PrefetchScalarGridSpec(
        num_scalar_prefetch=1,
        grid=(b,),
        in_specs=[pl.BlockSpec(memory_space=pl.ANY), full(perm), full(w1_pairs), full(bias), full(w2_heads),
                  full(cos), full(sa), full(sb), seq(q_blk), seq(gate_rows), full(cov), full(gs), full(gst)],
        out_specs=(pl.BlockSpec((1, rows, LANES), lambda i, pt: (i, 0, 0)),
                   pl.BlockSpec((1, rows, n_blk), lambda i, pt: (i, 0, 0))),
        scratch_shapes=[pltpu.VMEM((2, n_pages) + cache_t.shape[1:], f32),
                        pltpu.VMEM((2, CMP_STRIDE, nc, KVH * DH), f32),
                        pltpu.SemaphoreType.DMA((2,))],
    )
    return pl.pallas_call(
        functools.partial(_cmp_attn_sample_kernel, n_pages=n_pages, n_tok=n_tok, past_len=past_len),
        grid_spec=grid_spec,
        out_shape=(jax.ShapeDtypeStruct((b, rows, LANES), f32),
                   jax.ShapeDtypeStruct((b, rows, n_blk), bf16)),
        compiler_params=_cparams(("arbitrary",)),
        name="cmp_attn_sample",
    )(page_table, cache_t, perm, w1_pairs, bias, w2_heads, cos, sa, sb, q_blk, gate_rows, cov, gs, gst)


def _slc_win_sample_kernel(pt_ref, cache_ref, q_ref, sel_ref, e_ref, snew_ref, cw_ref, wnew_ref, wnewt_ref,
                           gate_ref, ocmp_ref, zn_ref, o_ref, wout_ref, buf, sem, *, n_pages, n_tok, win_off):
    def copies(seq, slot):
        return [pltpu.make_async_copy(cache_ref.at[pt_ref[seq, p], c],
                                      buf.at[slot, c, :, pl.ds(p * PAGE_SIZE, PAGE_SIZE)], sem.at[slot])
                for p in range(n_pages) for c in range(2)]

    slot = _page_gather(copies)
    rows = H_NSA * n_tok
    q = q_ref[0]
    tok = lax.broadcasted_iota(jnp.int32, (rows, 1), 0) % n_tok
    new_i = lax.broadcasted_iota(jnp.int32, (1, snew_ref.shape[1]), 1)
    new_ok = (new_i <= tok) & (new_i < n_tok)

    def attend(keys_t, vals_t, allowed, new_ref):
        s = jnp.where(allowed, _dot(q, keys_t), NEG)
        k_new = new_ref[0][:, :LANES].astype(bf16)
        v_new = new_ref[0][:, LANES:].astype(bf16)
        s_new = jnp.where(new_ok, _dot_nt(q, k_new), NEG)
        m = jnp.maximum(jnp.max(s, axis=-1, keepdims=True), jnp.max(s_new, axis=-1, keepdims=True))
        p = jnp.where(allowed, jnp.exp(s - m), 0.0)
        p_new = jnp.where(new_ok, jnp.exp(s_new - m), 0.0)
        l = jnp.sum(p, axis=-1, keepdims=True) + jnp.sum(p_new, axis=-1, keepdims=True)
        return (_dot_nt(p.astype(bf16), vals_t) + _dot(p_new.astype(bf16), v_new)) / l

    picked = _dot(sel_ref[0], e_ref[...]) > 0.5
    o_slc = attend(buf[slot, 0].astype(bf16), buf[slot, 1].astype(bf16), picked, snew_ref)

    wbuf = cw_ref.shape[-1]
    win_i = lax.broadcasted_iota(jnp.int32, (1, wbuf), 1)
    o_win = attend(cw_ref[0, 0].astype(bf16), cw_ref[0, 1].astype(bf16), win_i > tok + win_off, wnew_ref)

    lane = lax.broadcasted_iota(jnp.int32, (1, LANES), 1)
    for c in range(2):
        shifted = pltpu.roll(cw_ref[0, c], wbuf - n_tok, 1)
        tail = jnp.where(lane >= LANES - n_tok, wnewt_ref[0, c], shifted[:, wbuf - LANES:])
        wout_ref[0, c] = jnp.concatenate([shifted[:, :wbuf - LANES], tail], axis=-1)

    gate = gate_ref[0]
    o = ocmp_ref[0] + gate[:, 1:2] * o_slc + gate[:, 2:3] * o_win
    second_kvh = lax.broadcasted_iota(jnp.int32, (rows, 1), 0) >= G_NSA * n_tok
    o_ref[0] = jnp.where(second_kvh, o[:, DH:], o[:, :DH]) * zn_ref[0]


def _slc_win_sample(cache_t, page_table, q_blk, sel_rows, kvs_new, win_t, kvw_new, kvw_new_t, gate_rows,
                    o_cmp, zn_rows, n_tok):
    b, n_pages = page_table.shape
    rows = H_NSA * n_tok
    past_len = n_pages * PAGE_SIZE
    wbuf = win_t.shape[-1]
    e = _block_expand(sel_rows.shape[-1], past_len)
    full = lambda a: pl.BlockSpec(a.shape, lambda i, pt: (0,) * a.ndim)
    seq = lambda a: pl.BlockSpec((1,) + a.shape[1:], lambda i, pt: (i,) + (0,) * (a.ndim - 1))
    grid_spec = pltpu.PrefetchScalarGridSpec(
        num_scalar_prefetch=1,
        grid=(b,),
        in_specs=[pl.BlockSpec(memory_space=pl.ANY), seq(q_blk), seq(sel_rows), full(e), seq(kvs_new),
                  seq(win_t), seq(kvw_new), seq(kvw_new_t), seq(gate_rows), seq(o_cmp), seq(zn_rows)],
        out_specs=(pl.BlockSpec((1, rows, DH), lambda i, pt: (i, 0, 0)), seq(win_t)),
        scratch_shapes=[pltpu.VMEM((2, 2, KVH * DH, past_len), f32), pltpu.SemaphoreType.DMA((2,))],
    )
    return pl.pallas_call(
        functools.partial(_slc_win_sample_kernel, n_pages=n_pages, n_tok=n_tok, win_off=wbuf - WINDOW),
        grid_spec=grid_spec,
        out_shape=(jax.ShapeDtypeStruct((b, rows, DH), f32), jax.ShapeDtypeStruct(win_t.shape, f32)),
        compiler_params=_cparams(("arbitrary",)),
        name="slc_win_sample",
    )(page_table, cache_t, q_blk, sel_rows, e, kvs_new, win_t, kvw_new, kvw_new_t, gate_rows, o_cmp, zn_rows)


def _gla_sample(gq, gk, la, gv, zg, state, gain, b, n_tok):
    rows = 8
    pad = lambda a: jnp.pad(a.reshape(b, n_tok, -1), ((0, 0), (0, rows - n_tok), (0, 0))).reshape(b * rows, -1)
    y, s_new = _gla(pad(gq), pad(gk), pad(gv), pad(la), pad(zg), gain, state, b, tl=rows, chunk=rows)
    return y.reshape(b, rows, GLA_WIDTH)[:, :n_tok].reshape(b * n_tok, GLA_WIDTH), s_new


def _sample_path(x, cache_c, cache_s, cache_w, state, page_table, wts, w1_pairs, w2_heads):
    (norm_g, w_pack, wa2_pad, ba, w1_big, cmp_bias, w2_big, gla_gain, w_out, out_gain) = wts
    b, n_tok, _ = x.shape
    n = b * n_tok
    n_pages = page_table.shape[1]
    past_len = n_pages * PAGE_SIZE
    assert n_tok <= 8 and past_len % SLC_LEN == 0 and past_len // SLC_LEN <= LANES
    pos = past_len + jnp.arange(n) % n_tok
    (q_hm, kvc, kvs, kvw, _, _, _, _, gate, zn, gq, gk, gv, la, zg) = _inproj(
        x.reshape(1, n, D_MODEL), pos, norm_g, w_pack, wa2_pad, ba, tm=n)

    rows = H_NSA * n_tok
    q_rows = q_hm[0].reshape(H_NSA, b, n_tok, DH).transpose(1, 0, 2, 3)
    zero = jnp.zeros_like(q_rows[:, :G_NSA])
    q_blk = jnp.concatenate([jnp.concatenate([q_rows[:, :G_NSA], zero], axis=-1),
                             jnp.concatenate([zero, q_rows[:, G_NSA:]], axis=-1)], axis=1).reshape(b, rows, LANES)
    gate_rows = gate[:, :3 * H_NSA].reshape(b, n_tok, 3, H_NSA).transpose(0, 3, 1, 2).reshape(b, rows, 3)
    gate_rows = jnp.pad(gate_rows, ((0, 0), (0, 0), (0, LANES - 3)))
    pad_new = lambda a: jnp.pad(a.reshape(b, n_tok, KV_ROW), ((0, 0), (0, 8 - n_tok), (0, 0)))

    kvw_new_t = kvw.reshape(b, n_tok, 2, KVH * DH).transpose(0, 2, 3, 1)
    kvw_new_t = jnp.pad(kvw_new_t, ((0, 0), (0, 0), (0, 0), (LANES - n_tok, 0)))

    o_cmp, sel_rows = _cmp_attn_sample(_pages_native(cache_c), page_table, w1_pairs, cmp_bias, w2_heads,
                                       q_blk, gate_rows, n_tok)
    o_rows, win_t = _slc_win_sample(_pages_native(cache_s), page_table, q_blk, sel_rows, pad_new(kvs),
                                    _pages_native(cache_w), pad_new(kvw), kvw_new_t, gate_rows, o_cmp,
                                    zn.reshape(b, n_tok, H_NSA, DH).transpose(0, 2, 1, 3).reshape(b, rows, DH),
                                    n_tok)
    y_nsa = o_rows.reshape(b, H_NSA, n_tok, DH).transpose(0, 2, 1, 3).reshape(n, NSA_WIDTH)

    y_gla, s_new = _gla_sample(gq, gk, la, gv, zg, state, gla_gain, b, n_tok)
    y = _outproj(x.reshape(n, D_MODEL), y_nsa, y_gla, w_out, out_gain, tm=n)
    kv5 = lambda a: a.reshape(b, n_tok, 2, KVH, DH)
    win_new = win_t.reshape(b, 2, KVH, DH, -1).transpose(0, 4, 1, 2, 3)
    return (y.reshape(b, n_tok, D_MODEL), kv5(kvc), kv5(kvs), win_new, s_new)


def kernel(x_prompt, x_sample, cache_cmp_kv, cache_slc_kv, cache_win_kv, state_gla, page_table,
           norm_in_gain, w_in, cmp_pe, cmp_w1, cmp_b1, cmp_w2, gla_wa2, gla_ba, gla_norm_gain,
           w_out, norm_out_gain):
    assert w_in.shape[0] == 1, "single-layer step"
    wts = (norm_in_gain[0], _pack_w_in(w_in[0]), _pad_wa2(gla_wa2[0]), gla_ba[0],
           _cmp_w1_big(cmp_w1[0]), _cmp_bias(cmp_pe[0], cmp_w1[0], cmp_b1[0]), _cmp_w2_big(cmp_w2[0]),
           gla_norm_gain[0], w_out[0].astype(bf16), norm_out_gain)
    yp, cmp_p, slc_p, win_p, gla_p = _prompt_path(x_prompt, wts, *_pack_w_in_prompt(w_in[0]))
    ys, cmp_s, slc_s, win_s, gla_s = _sample_path(
        x_sample, cache_cmp_kv[0], cache_slc_kv[0], cache_win_kv[0], state_gla[0], page_table, wts,
        _cmp_w1_pairs(cmp_w1[0]), _cmp_w2_heads(cmp_w2[0]))
    return (yp, ys, cmp_p[None], cmp_s[None], slc_p[None], slc_s[None], win_p[None], win_s[None],
            gla_p[None], gla_s[None])
```

```python
import functools

import numpy as np
import jax
import jax.numpy as jnp
from jax import lax
from jax.experimental import pallas as pl
from jax.experimental.pallas import tpu as pltpu

f32 = jnp.float32
bf16 = jnp.bfloat16

D_MODEL = 1024
DH = 64
H_NSA = 8
KVH = 2
G_NSA = 4
NSA_WIDTH = H_NSA * DH
CMP_LEN = 32
CMP_STRIDE = 16
CMP_HID = 128
SLC_LEN = 64
SLC_TOP = 16
WINDOW = 512
ROT_HALF = 8
ROPE_THETA = 500000.0
FORCE_SCORE = 1.0e4
H_GLA = 4
DK = 64
DV = 128
GLA_WIDTH = H_GLA * DV
GLA_LR = 16
GLA_TAU = 16.0
GLA_CHUNK = 64
GLA_SUB = 16
EPS = 1e-6
NEG = -1.0e30
LOG2E = 1.4426950408889634
PAGE_SIZE = 128
KV_ROW = 2 * KVH * DH

IN_SIZES = (H_NSA * DH, KV_ROW, KV_ROW, KV_ROW, 3 * H_NSA, NSA_WIDTH,
            H_GLA * DK, H_GLA * DK, H_GLA * DV, GLA_LR, GLA_WIDTH)
IN_OFFSETS = [0] + [int(v) for v in np.cumsum(IN_SIZES)]

LANES = 128
VMEM_LIMIT = 56 * 1024 * 1024

P_Q, P_KVC, P_KVS, P_KVW, P_ZN, P_GQ, P_GK, P_GV, P_ZG, P_MISC = (
    0, 512, 768, 1024, 1280, 1792, 2048, 2304, 2816, 3328)
D_PACK = P_MISC + LANES
MISC_GATE = 0
MISC_GLR = 32


def _cparams(sem):
    return pltpu.CompilerParams(dimension_semantics=sem, vmem_limit_bytes=VMEM_LIMIT)


def _sigmoid(x):
    return 1.0 / (1.0 + jnp.exp(-x))


def _silu(x):
    return x * _sigmoid(x)


def _log_sigmoid(x):
    return jnp.minimum(x, 0.0) - jnp.log1p(jnp.exp(-jnp.abs(x)))


def _dot(a, b):
    return jnp.dot(a, b, preferred_element_type=f32)


def _dot_nt(a, b):
    return lax.dot_general(a, b, (((1,), (1,)), ((), ())), preferred_element_type=f32)


def _dot_tn(a, b):
    return lax.dot_general(a, b, (((0,), (0,)), ((), ())), preferred_element_type=f32)


def _rope_tables(pos):
    n = pos.shape[0]
    inv = ROPE_THETA ** (-(jnp.arange(ROT_HALF, dtype=f32) / ROT_HALF))
    ang = pos.astype(f32)[:, None] * inv[None, :]
    cos, sin = jnp.cos(ang), jnp.sin(ang)
    z8 = jnp.zeros((n, ROT_HALF), f32)
    rest = DH - 2 * ROT_HALF
    c64 = jnp.concatenate([cos, cos, jnp.ones((n, rest), f32)], axis=-1)
    sa64 = jnp.concatenate([-sin, z8, jnp.zeros((n, rest), f32)], axis=-1)
    sb64 = jnp.concatenate([z8, sin, jnp.zeros((n, rest), f32)], axis=-1)
    tile = lambda t: jnp.concatenate([t, t], axis=-1)
    return tile(c64), tile(sa64), tile(sb64)


def _rope128(v, cos, sa, sb):
    return v * cos + pltpu.roll(v, LANES - ROT_HALF, 1) * sa + pltpu.roll(v, ROT_HALF, 1) * sb


def _inproj_kernel(x_ref, g_ref, w_ref, cos_ref, sa_ref, sb_ref, wa2_ref, ba_ref,
                   q_ref, kvc_ref, kvs_ref, kvw_ref, ks_ref, vs_ref, kw_ref, vw_ref,
                   gate_ref, zn_ref, gq_ref, gk_ref, gv_ref, la_ref, zg_ref):
    x = x_ref[...]
    ms = jnp.mean(x * x, axis=-1, keepdims=True)
    hn = (x * lax.rsqrt(ms + EPS) * g_ref[...]).astype(bf16)
    cos, sa, sb = cos_ref[...], sa_ref[...], sb_ref[...]

    def proj(off, width):
        return _dot(hn, w_ref[:, off:off + width])

    qp = proj(P_Q, NSA_WIDTH)
    for c in range(NSA_WIDTH // LANES):
        r = _rope128(qp[:, c * LANES:(c + 1) * LANES], cos, sa, sb) * (DH ** -0.5)
        q_ref[0, 2 * c] = r[:, :DH].astype(bf16)
        q_ref[0, 2 * c + 1] = r[:, DH:].astype(bf16)

    kvc_ref[...] = proj(P_KVC, KV_ROW)

    for off, kv_ref, k_ref, v_ref in ((P_KVS, kvs_ref, ks_ref, vs_ref), (P_KVW, kvw_ref, kw_ref, vw_ref)):
        p = proj(off, KV_ROW)
        k = _rope128(p[:, :LANES], cos, sa, sb)
        v = p[:, LANES:]
        kv_ref[:, :LANES] = k
        kv_ref[:, LANES:] = v
        for h in range(KVH):
            k_ref[0, h] = k[:, h * DH:(h + 1) * DH].astype(bf16)
            v_ref[0, h] = v[:, h * DH:(h + 1) * DH].astype(bf16)

    zn_ref[...] = _silu(proj(P_ZN, NSA_WIDTH))
    gq_ref[...] = proj(P_GQ, H_GLA * DK) * (DK ** -0.5)
    gk_ref[...] = proj(P_GK, H_GLA * DK)
    gv_ref[...] = proj(P_GV, H_GLA * DV)
    zg_ref[...] = _silu(proj(P_ZG, GLA_WIDTH))

    misc = proj(P_MISC, LANES)
    gate_ref[...] = _sigmoid(misc)
    xa = _dot(misc.astype(bf16), wa2_ref[...]) + ba_ref[...]
    la_ref[...] = _log_sigmoid(xa) / GLA_TAU


def _pack_w_in(w_in):
    o = IN_OFFSETS
    seg = lambda i: w_in[:, o[i]:o[i + 1]]
    misc = jnp.zeros((D_MODEL, LANES), w_in.dtype)
    misc = misc.at[:, MISC_GATE:MISC_GATE + 3 * H_NSA].set(seg(4))
    misc = misc.at[:, MISC_GLR:MISC_GLR + GLA_LR].set(seg(9))
    cols = [seg(0), seg(1), seg(2), seg(3), seg(5), seg(6), seg(7), seg(8), seg(10), misc]
    return jnp.concatenate(cols, axis=1).astype(bf16)


def _pad_wa2(wa2):
    pad = jnp.zeros((LANES, H_GLA * DK), wa2.dtype).at[MISC_GLR:MISC_GLR + GLA_LR].set(wa2)
    return pad.astype(bf16)


def _inproj(x, pos, norm_g, w_pack, wa2_pad, ba, tm):
    bk, lk, _ = x.shape
    n = bk * lk
    nt = lk // tm
    cos, sa, sb = _rope_tables(pos)
    tok = lambda w: pl.BlockSpec((tm, w), lambda i: (i, 0))
    tab = pl.BlockSpec((tm, LANES), lambda i: (i % nt, 0))
    full = lambda a: pl.BlockSpec(a.shape, lambda i: (0,) * a.ndim)
    hm = lambda h: pl.BlockSpec((1, h, tm, DH), lambda i: (i // nt, 0, i % nt, 0))
    g2 = norm_g.reshape(1, D_MODEL)
    ba2 = ba.reshape(1, H_GLA * DK)
    out_shape = (
        jax.ShapeDtypeStruct((bk, H_NSA, lk, DH), bf16),
        jax.ShapeDtypeStruct((n, KV_ROW), f32),
        jax.ShapeDtypeStruct((n, KV_ROW), f32),
        jax.ShapeDtypeStruct((n, KV_ROW), f32),
        jax.ShapeDtypeStruct((bk, KVH, lk, DH), bf16),
        jax.ShapeDtypeStruct((bk, KVH, lk, DH), bf16),
        jax.ShapeDtypeStruct((bk, KVH, lk, DH), bf16),
        jax.ShapeDtypeStruct((bk, KVH, lk, DH), bf16),
        jax.ShapeDtypeStruct((n, LANES), f32),
        jax.ShapeDtypeStruct((n, NSA_WIDTH), f32),
        jax.ShapeDtypeStruct((n, H_GLA * DK), f32),
        jax.ShapeDtypeStruct((n, H_GLA * DK), f32),
        jax.ShapeDtypeStruct((n, H_GLA * DV), f32),
        jax.ShapeDtypeStruct((n, H_GLA * DK), f32),
        jax.ShapeDtypeStruct((n, GLA_WIDTH), f32),
    )
    out_specs = (hm(H_NSA), tok(KV_ROW), tok(KV_ROW), tok(KV_ROW), hm(KVH), hm(KVH), hm(KVH), hm(KVH),
                 tok(LANES), tok(NSA_WIDTH), tok(H_GLA * DK), tok(H_GLA * DK), tok(H_GLA * DV),
                 tok(H_GLA * DK), tok(GLA_WIDTH))
    return pl.pallas_call(
        _inproj_kernel,
        grid=(n // tm,),
        in_specs=[tok(D_MODEL), full(g2), full(w_pack), tab, tab, tab, full(wa2_pad), full(ba2)],
        out_specs=out_specs,
        out_shape=out_shape,
        compiler_params=_cparams(("parallel",)),
        name="inproj",
    )(x.reshape(n, D_MODEL), g2, w_pack, cos, sa, sb, wa2_pad, ba2)


PT_Q, PT_KVC, PT_KVS, PT_KVW, PT_GATE = 0, 512, 768, 1024, 1280
PT_ROWS = PT_GATE + 32
PK_ZN, PK_GQ, PK_GK, PK_GV, PK_ZG, PK_MISC = 0, 512, 768, 1024, 1536, 2048
PK_COLS = PK_MISC + LANES


def _inproj_prompt_kernel(x_ref, g_ref, wt_ref, w_ref, cos_ref, sin_ref, wa2_ref, ba_ref,
                          qt_ref, kvct_ref, kvst_ref, kvwt_ref, ks_ref, kw_ref, vst_ref, vwt_ref, kvc_ref,
                          gatet_ref, zn_ref, gq_ref, gk_ref, gv_ref, la_ref, zg_ref):
    x = x_ref[...]
    ms = jnp.mean(x * x, axis=-1, keepdims=True)
    hn = (x * lax.rsqrt(ms + EPS) * g_ref[...]).astype(bf16)
    cos_t, sin_t = cos_ref[...], sin_ref[...]

    def proj_t(off, rows):
        return _dot_nt(wt_ref[off:off + rows, :], hn)

    def rope_rows(v):
        x1, x2 = v[:ROT_HALF], v[ROT_HALF:2 * ROT_HALF]
        return jnp.concatenate([x1 * cos_t - x2 * sin_t, x2 * cos_t + x1 * sin_t, v[2 * ROT_HALF:]], axis=0)

    q_t = proj_t(PT_Q, NSA_WIDTH)
    for h in range(H_NSA):
        qt_ref[0, h] = (rope_rows(q_t[h * DH:(h + 1) * DH]) * (DH ** -0.5 * LOG2E)).astype(bf16)

    kvc_t = proj_t(PT_KVC, KV_ROW)
    kvct_ref[0] = kvc_t
    kvc_ref[...] = kvc_t.T.astype(bf16)

    for off, kvt_ref, k_ref, vt_ref in ((PT_KVS, kvst_ref, ks_ref, vst_ref), (PT_KVW, kvwt_ref, kw_ref, vwt_ref)):
        t = proj_t(off, KV_ROW)
        k_t = jnp.concatenate([rope_rows(t[h * DH:(h + 1) * DH]) for h in range(KVH)], axis=0)
        kvt_ref[0, :LANES] = k_t
        kvt_ref[0, LANES:] = t[LANES:]
        k_tok = k_t.T
        for h in range(KVH):
            k_ref[0, h] = k_tok[:, h * DH:(h + 1) * DH].astype(bf16)
            vt_ref[0, h] = t[LANES + h * DH:LANES + (h + 1) * DH].astype(bf16)

    gatet_ref[0] = _sigmoid(proj_t(PT_GATE, PT_ROWS - PT_GATE))

    def proj(off, width):
        return _dot(hn, w_ref[:, off:off + width])

    zn_ref[...] = _silu(proj(PK_ZN, NSA_WIDTH))
    gq_ref[...] = proj(PK_GQ, H_GLA * DK) * (DK ** -0.5)
    gk_ref[...] = proj(PK_GK, H_GLA * DK)
    gv_ref[...] = proj(PK_GV, H_GLA * DV)
    zg_ref[...] = _silu(proj(PK_ZG, GLA_WIDTH))
    misc = proj(PK_MISC, LANES)
    xa = _dot(misc.astype(bf16), wa2_ref[...]) + ba_ref[...]
    la_ref[...] = _log_sigmoid(xa) / GLA_TAU


def _pack_w_in_prompt(w_in):
    o = IN_OFFSETS
    seg = lambda i: w_in[:, o[i]:o[i + 1]]
    gate_t = jnp.zeros((PT_ROWS - PT_GATE, D_MODEL), w_in.dtype).at[:3 * H_NSA].set(seg(4).T)
    w_t = jnp.concatenate([seg(0).T, seg(1).T, seg(2).T, seg(3).T, gate_t], axis=0).astype(bf16)
    misc = jnp.zeros((D_MODEL, LANES), w_in.dtype).at[:, MISC_GLR:MISC_GLR + GLA_LR].set(seg(9))
    w_tok = jnp.concatenate([seg(5), seg(6), seg(7), seg(8), seg(10), misc], axis=1).astype(bf16)
    return w_t, w_tok


def _inproj_prompt(x, norm_g, w_t, w_tok, wa2_pad, ba, tm):
    b, lq, _ = x.shape
    n = b * lq
    nt = lq // tm
    inv = ROPE_THETA ** (-(jnp.arange(ROT_HALF, dtype=f32) / ROT_HALF))
    ang = inv[:, None] * jnp.arange(lq).astype(f32)[None, :]
    cos_t, sin_t = jnp.cos(ang), jnp.sin(ang)
    tok = lambda w: pl.BlockSpec((tm, w), lambda i: (i, 0))
    tab = pl.BlockSpec((ROT_HALF, tm), lambda i: (0, i % nt))
    full = lambda a: pl.BlockSpec(a.shape, lambda i: (0,) * a.ndim)
    feat = lambda r: pl.BlockSpec((1, r, tm), lambda i: (i // nt, 0, i % nt))
    headf = lambda h: pl.BlockSpec((1, h, DH, tm), lambda i: (i // nt, 0, 0, i % nt))
    headt = lambda h: pl.BlockSpec((1, h, tm, DH), lambda i: (i // nt, 0, i % nt, 0))
    g2 = norm_g.reshape(1, D_MODEL)
    ba2 = ba.reshape(1, H_GLA * DK)
    sds = jax.ShapeDtypeStruct
    out_shape = (
        sds((b, H_NSA, DH, lq), bf16),
        sds((b, KV_ROW, lq), f32), sds((b, KV_ROW, lq), f32), sds((b, KV_ROW, lq), f32),
        sds((b, KVH, lq, DH), bf16), sds((b, KVH, lq, DH), bf16),
        sds((b, KVH, DH, lq), bf16), sds((b, KVH, DH, lq), bf16),
        sds((n, KV_ROW), bf16),
        sds((b, PT_ROWS - PT_GATE, lq), f32),
        sds((n, NSA_WIDTH), f32), sds((n, H_GLA * DK), f32), sds((n, H_GLA * DK), f32),
        sds((n, H_GLA * DV), f32), sds((n, H_GLA * DK), f32), sds((n, GLA_WIDTH), f32),
    )
    out_specs = (headf(H_NSA), feat(KV_ROW), feat(KV_ROW), feat(KV_ROW), headt(KVH), headt(KVH),
                 headf(KVH), headf(KVH), tok(KV_ROW), feat(PT_ROWS - PT_GATE),
                 tok(NSA_WIDTH), tok(H_GLA * DK), tok(H_GLA * DK), tok(H_GLA * DV), tok(H_GLA * DK),
                 tok(GLA_WIDTH))
    return pl.pallas_call(
        _inproj_prompt_kernel,
        grid=(n // tm,),
        in_specs=[tok(D_MODEL), full(g2), full(w_t), full(w_tok), tab, tab, full(wa2_pad), full(ba2)],
        out_specs=out_specs,
        out_shape=out_shape,
        compiler_params=_cparams(("parallel",)),
        name="inproj_prompt",
    )(x.reshape(n, D_MODEL), g2, w_t, w_tok, cos_t, sin_t, wa2_pad, ba2)


CHUNK_W = CMP_STRIDE * KV_ROW
PART_W = 2 * 2 * KVH * CMP_HID
HID_W = 2 * KVH * CMP_HID


def _cmp_w1_big(cmp_w1):
    w1r = cmp_w1.reshape(2, CMP_LEN // CMP_STRIDE, CMP_STRIDE, DH, CMP_HID)
    eye = jnp.eye(2, dtype=cmp_w1.dtype)
    big = jnp.einsum('crsdh,cC,kK->sCKdrckh', w1r, eye, eye)
    return big.reshape(CHUNK_W, PART_W).astype(bf16)


def _cmp_w2_big(cmp_w2):
    eye = jnp.eye(2, dtype=cmp_w2.dtype)
    big = jnp.einsum('chd,cC,kK->ckhCKd', cmp_w2, eye, eye)
    return big.reshape(HID_W, KV_ROW).astype(bf16)


def _cmp_part_kernel(x_ref, w_ref, o_ref):
    o_ref[...] = _dot(x_ref[...].astype(bf16), w_ref[...])


def _cmp_part(chunks, w1_big, tr):
    r = chunks.shape[0]
    return pl.pallas_call(
        _cmp_part_kernel,
        grid=(r // tr,),
        in_specs=[pl.BlockSpec((tr, CHUNK_W), lambda i: (i, 0)),
                  pl.BlockSpec((CHUNK_W, PART_W), lambda i: (0, 0))],
        out_specs=pl.BlockSpec((tr, PART_W), lambda i: (i, 0)),
        out_shape=jax.ShapeDtypeStruct((r, PART_W), f32),
        compiler_params=_cparams(("parallel",)),
        name="cmp_part",
    )(chunks, w1_big)


def _cmp_bias_kernel(pe_ref, w1_ref, b1_ref, o_ref):
    for c in range(2):
        o_ref[c] = _dot(pe_ref[c], w1_ref[c]) + b1_ref[c]


def _cmp_bias(cmp_pe, cmp_w1, cmp_b1):
    pe = jnp.broadcast_to(cmp_pe.reshape(2, 1, CMP_LEN * DH), (2, 8, CMP_LEN * DH))
    b1 = jnp.broadcast_to(cmp_b1.reshape(2, 1, CMP_HID), (2, 8, CMP_HID))
    out = pl.pallas_call(
        _cmp_bias_kernel,
        out_shape=jax.ShapeDtypeStruct((2, 8, CMP_HID), f32),
        name="cmp_bias",
    )(pe, cmp_w1, b1)
    row = out[:, 0, :]
    return jnp.broadcast_to(row[:, None, :], (2, KVH, CMP_HID)).reshape(1, HID_W)


def _compress_finish(part, bias, w2_big, cos, sa, sb):
    n = part.shape[0]
    pre = part[:, :HID_W] + pltpu.roll(part[:, HID_W:], n - 1, 0)
    h = _silu(pre + bias)
    out = _dot(h.astype(bf16), w2_big)
    return _rope128(out[:, :LANES], cos, sa, sb), out[:, LANES:]


def _masked_softmax(s, mask, axis, exp=jnp.exp):
    s = jnp.where(mask, s, NEG)
    e = exp(s - jnp.max(s, axis=axis, keepdims=True))
    return jnp.where(mask, e * (1.0 / jnp.sum(e, axis=axis, keepdims=True)), 0.0)


def _split_bf16(x):
    hi = x.astype(bf16)
    return hi, (x - hi.astype(f32)).astype(bf16)


def _topk_mask(score, k, axis):
    n = score.shape[axis]
    idx = lax.broadcasted_iota(jnp.int32, score.shape, axis)
    sel = jnp.zeros(score.shape, f32)
    for _ in range(k):
        m = jnp.max(score, axis=axis, keepdims=True)
        first = jnp.min(jnp.where(score == m, idx, n), axis=axis, keepdims=True)
        pick = idx == first
        sel = jnp.where(pick, 1.0, sel)
        score = jnp.where(pick, NEG, score)
    return sel


def _topk_rows_by_rank(score, k):
    r, n = score.shape
    cols = jnp.concatenate([score, jnp.zeros((n - r, n), f32)], axis=0).T
    earlier = jnp.where(lax.broadcasted_iota(jnp.int32, (n, n), 0) < lax.broadcasted_iota(jnp.int32, (n, n), 1),
                        1.0, 0.0)
    rows = []
    for c in range(r):
        col, row = cols[:, c:c + 1], score[c:c + 1, :]
        before = jnp.where(col > row, 1.0, jnp.where(col == row, earlier, 0.0))
        rows.append(jnp.where(jnp.sum(before, axis=0, keepdims=True) < k, 1.0, 0.0))
    return jnp.concatenate(rows, axis=0)


def _cover_t(n_slc, n_chunk):
    start = np.arange(n_chunk)[None, :] * CMP_STRIDE
    j = np.arange(n_slc)[:, None]
    cov = (start < (j + 1) * SLC_LEN) & (start + CMP_LEN > j * SLC_LEN)
    return jnp.asarray(cov.astype(np.float32), dtype=bf16)


def _cmp_attn_kernel(part_ref, bias_ref, w2_ref, cos_ref, sa_ref, sb_ref, qt_ref, gatet_ref, cov_ref,
                     ot_ref, selt_ref, kc_ref, vct_ref, *, tq):
    t = pl.program_id(1)

    @pl.when(t == 0)
    def _():
        k, v = _compress_finish(part_ref[0], bias_ref[...], w2_ref[...], cos_ref[...], sa_ref[...], sb_ref[...])
        v_t = v.T
        for h in range(KVH):
            kc_ref[h] = k[:, h * DH:(h + 1) * DH].astype(bf16)
            vct_ref[h] = v_t[h * DH:(h + 1) * DH].astype(bf16)

    nc = kc_ref.shape[1]
    nb = cov_ref.shape[0]
    qpos = t * tq + lax.broadcasted_iota(jnp.int32, (1, tq), 1)
    end_pos = lax.broadcasted_iota(jnp.int32, (nc, 1), 0) * CMP_STRIDE + (CMP_LEN - 1)
    mask_t = end_pos <= qpos
    gate_t = gatet_ref[0]
    cov = cov_ref[...]

    jblk = lax.broadcasted_iota(jnp.int32, (nb, tq), 0)
    qblk = qpos // SLC_LEN
    valid = jblk <= qblk
    forced = (jblk == 0) | (jblk == qblk) | (jblk == qblk - 1)

    scores = []
    for kh in range(KVH):
        kc, vc_t = kc_ref[kh], vct_ref[kh]
        psum = jnp.zeros((nc, tq), f32)
        for g in range(G_NSA):
            h = kh * G_NSA + g
            p_t = _masked_softmax(_dot(kc, qt_ref[0, h]), mask_t, axis=0, exp=jnp.exp2)
            ot_ref[0, h * DH:(h + 1) * DH] = _dot(vc_t, p_t.astype(bf16)) * gate_t[h:h + 1]
            psum = psum + p_t
        hi, lo = _split_bf16(psum)
        imp = _dot(cov, hi) + _dot(cov, lo)
        scores.append(jnp.where(valid, jnp.where(forced, FORCE_SCORE, imp), -1.0))
    picked = _topk_mask(jnp.concatenate(scores, axis=1), SLC_TOP, axis=0)
    for kh in range(KVH):
        sel_t = jnp.where(valid & (picked[:, kh * tq:(kh + 1) * tq] > 0.5), 0.0, NEG)
        if nb < LANES:
            sel_t = jnp.concatenate([sel_t, jnp.full((LANES - nb, tq), NEG, f32)], axis=0)
        selt_ref[0, kh] = sel_t


def _cmp_attn(part, bias, w2_big, q_t, gate_t, tq):
    b, nc, _ = part.shape
    lq = q_t.shape[3]
    n_slc = lq // SLC_LEN
    end_pos = jnp.arange(nc) * CMP_STRIDE + (CMP_LEN - 1)
    cos, sa, sb = _rope_tables(end_pos)
    cov = _cover_t(n_slc, nc)
    full = lambda a: pl.BlockSpec(a.shape, lambda i, t: (0,) * a.ndim)
    return pl.pallas_call(
        functools.partial(_cmp_attn_kernel, tq=tq),
        grid=(b, lq // tq),
        in_specs=[pl.BlockSpec((1, nc, PART_W), lambda i, t: (i, 0, 0)),
                  full(bias), full(w2_big), full(cos), full(sa), full(sb),
                  pl.BlockSpec((1, H_NSA, DH, tq), lambda i, t: (i, 0, 0, t)),
                  pl.BlockSpec((1, gate_t.shape[1], tq), lambda i, t: (i, 0, t)),
                  full(cov)],
        out_specs=(pl.BlockSpec((1, NSA_WIDTH, tq), lambda i, t: (i, 0, t)),
                   pl.BlockSpec((1, KVH, LANES, tq), lambda i, t: (i, 0, 0, t))),
        out_shape=(jax.ShapeDtypeStruct((b, NSA_WIDTH, lq), f32),
                   jax.ShapeDtypeStruct((b, KVH, LANES, lq), f32)),
        scratch_shapes=[pltpu.VMEM((KVH, nc, DH), bf16), pltpu.VMEM((KVH, DH, nc), bf16)],
        compiler_params=_cparams(("parallel", "arbitrary")),
        name="cmp_attn",
    )(part, bias, w2_big, cos, sa, sb, q_t, gate_t, cov)


def _block_expand(n_slc_pad, n_keys):
    e = (np.arange(n_keys)[None, :] // SLC_LEN) == np.arange(n_slc_pad)[:, None]
    return jnp.asarray(e.astype(np.float32), dtype=bf16)


def _mask_bias_t(allowed):
    bias = jnp.where(allowed, 0.0, NEG)
    return jnp.concatenate([bias] * G_NSA, axis=1)


def _sublane_fold(x, op):
    return functools.reduce(op, [x[r:r + 8] for r in range(0, x.shape[0], 8)])


def _slc_win_kernel(qt_ref, ks_ref, vst_ref, kw_ref, vwt_ref, selt_ref, gatet_ref, ocmpt_ref, zn_ref,
                    o_ref, s_a, s_b, m_a, m_b, ws_a, ws_b, wm_a, wm_b, acc_ref, wacc_ref, *, tq, tk, nt):
    t = pl.program_id(2)

    @pl.when(t == 0)
    def _():
        m_b[...] = jnp.full(m_b.shape, NEG, f32)
        ws_b[...] = jnp.zeros(ws_b.shape, f32)
        wm_b[...] = jnp.zeros(wm_b.shape, f32)

    step = functools.partial(_slc_win_step, qt_ref, ks_ref, vst_ref, kw_ref, vwt_ref, selt_ref, gatet_ref,
                             ocmpt_ref, zn_ref, o_ref, acc_ref, wacc_ref, tq=tq, tk=tk, nt=nt)

    @pl.when(t % 2 == 0)
    def _():
        step(s_a, s_b, m_a, m_b, ws_a, ws_b, wm_a, wm_b)

    @pl.when(t % 2 == 1)
    def _():
        step(s_b, s_a, m_b, m_a, ws_b, ws_a, wm_b, wm_a)


ONES_ROWS = 16


def _slc_win_step(qt_ref, ks_ref, vst_ref, kw_ref, vwt_ref, selt_ref, gatet_ref, ocmpt_ref, zn_ref, o_ref,
                  acc_ref, wacc_ref, s_cur, s_prev, m_cur, m_last, ws_cur, ws_prev, wm_cur, wm_last,
                  *, tq, tk, nt):
    kh = pl.program_id(1)
    t = pl.program_id(2)
    lq = ks_ref.shape[2]
    band = WINDOW + tq
    blocks_per_tile = tk // SLC_LEN
    key_tiles = lambda tile: (tile * tq + tq + tk - 1) // tk
    n_score = jnp.where(t < nt, key_tiles(t), 0)
    n_apply = jnp.where(t >= 1, key_tiles(t - 1), 0)
    n_plain = jnp.where(t < nt, (t * tq) // tk, 0)
    q0 = t * tq
    qpos = q0 + lax.broadcasted_iota(jnp.int32, (1, tq), 1)
    q_t = jnp.concatenate([qt_ref[0, g] for g in range(G_NSA)], axis=1)
    ones = jnp.ones((ONES_ROWS, tk), bf16)

    m_cur[...] = jnp.full(m_cur.shape, NEG, f32)
    m_prev = jnp.max(m_last[...], axis=0, keepdims=True)
    acc_ref[...] = jnp.zeros(acc_ref.shape, f32)

    def score(kt, causal):
        k0 = pl.multiple_of(kt * tk, tk)
        j0 = pl.multiple_of(kt * blocks_per_tile, blocks_per_tile)
        sel = selt_ref[0, 0, pl.ds(j0, blocks_per_tile), :]
        bias = jnp.concatenate([jnp.broadcast_to(sel[j:j + 1], (SLC_LEN, tq)) for j in range(blocks_per_tile)],
                               axis=0)
        if causal:
            kpos = k0 + lax.broadcasted_iota(jnp.int32, (tk, 1), 0)
            bias = jnp.where(kpos <= qpos, bias, NEG)
        s = _dot(ks_ref[0, 0, pl.ds(k0, tk), :], q_t) + jnp.concatenate([bias] * G_NSA, axis=1)
        s_cur[pl.ds(k0, tk), :] = s
        m_cur[...] = jnp.maximum(m_cur[...], _sublane_fold(s, jnp.maximum))

    def apply(kt):
        k0 = pl.multiple_of(kt * tk, tk)
        p = jnp.exp2(s_prev[pl.ds(k0, tk), :] - m_prev).astype(bf16)
        acc_ref[...] += _dot(jnp.concatenate([vst_ref[0, 0, :, pl.ds(k0, tk)], ones], axis=0), p)

    def both_body(kt, carry):
        score(kt, causal=False)
        apply(kt)
        return carry

    def score_body(kt, carry):
        score(kt, causal=True)
        return carry

    def apply_body(kt, carry):
        apply(kt)
        return carry

    def both2_body(i, carry):
        for u in range(2):
            score(2 * i + u, causal=False)
            apply(2 * i + u)
        return carry

    n_both = jnp.minimum(n_plain, n_apply)
    lax.fori_loop(0, n_both // 2, both2_body, 0)
    lax.fori_loop(n_both // 2 * 2, n_both, both_body, 0)
    lax.fori_loop(n_both, n_score, score_body, 0)
    lax.fori_loop(n_both, n_apply, apply_body, 0)

    w0 = pl.multiple_of(jnp.clip(q0 - WINDOW, 0, lq - band), tq)
    w0_prev = pl.multiple_of(jnp.clip(q0 - tq - WINDOW, 0, lq - band), tq)
    wm_prev = jnp.max(wm_last[...], axis=0, keepdims=True)
    ones_w = jnp.ones((ONES_ROWS, tq), bf16)
    wmax, wacc = None, None
    for j in range(band // tq):
        rows = pl.ds(j * tq, tq)
        kpos = w0 + j * tq + lax.broadcasted_iota(jnp.int32, (tq, 1), 0)
        s = (_dot(kw_ref[0, 0, pl.ds(pl.multiple_of(w0 + j * tq, tq), tq), :], q_t)
             + _mask_bias_t((kpos <= qpos) & (kpos > qpos - WINDOW)))
        ws_cur[rows, :] = s
        fold = _sublane_fold(s, jnp.maximum)
        wmax = fold if wmax is None else jnp.maximum(wmax, fold)
        p = jnp.exp2(ws_prev[rows, :] - wm_prev).astype(bf16)
        v_ext = jnp.concatenate([vwt_ref[0, 0, :, pl.ds(pl.multiple_of(w0_prev + j * tq, tq), tq)], ones_w], axis=0)
        part = _dot(v_ext, p)
        wacc = part if wacc is None else wacc + part
    wm_cur[...] = wmax
    wacc_ref[...] = wacc

    @pl.when(t >= 1)
    def _():
        o_slc = acc_ref[:DH] / acc_ref[DH:DH + 1]
        o_win = wacc_ref[:DH] / wacc_ref[DH:DH + 1]
        gate_t = gatet_ref[0]
        row = lambda c, g: jnp.where(kh == 0, gate_t[c * H_NSA + g:c * H_NSA + g + 1],
                                     gate_t[c * H_NSA + G_NSA + g:c * H_NSA + G_NSA + g + 1])
        g_slc = jnp.concatenate([row(1, g) for g in range(G_NSA)], axis=1)
        g_win = jnp.concatenate([row(2, g) for g in range(G_NSA)], axis=1)
        o_t = g_slc * o_slc + g_win * o_win
        o_heads = jnp.concatenate([o_t[:, g * tq:(g + 1) * tq] for g in range(G_NSA)], axis=0)
        o_ref[0] = ((ocmpt_ref[0] + o_heads).T * zn_ref[...]).astype(o_ref.dtype)


def _slc_win(q_t, ks, vs_t, kw, vw_t, sel_t, gate_t, o_cmp_t, zn, tq, tk):
    b, _, _, lq = q_t.shape
    assert lq >= WINDOW + tq and lq % tk == 0 and tk % (8 * SLC_LEN) == 0
    gw = G_NSA * DH
    k_spec = pl.BlockSpec((1, 1, lq, DH), lambda i, k, t: (i, k, 0, 0))
    vt_spec = pl.BlockSpec((1, 1, DH, lq), lambda i, k, t: (i, k, 0, 0))
    lanes = G_NSA * tq
    band = WINDOW + tq
    nt = lq // tq
    scored = lambda t: jnp.minimum(t, nt - 1)
    drained = lambda t: jnp.maximum(t - 1, 0)
    return pl.pallas_call(
        functools.partial(_slc_win_kernel, tq=tq, tk=tk, nt=nt),
        grid=(b, KVH, nt + 1),
        in_specs=[pl.BlockSpec((1, G_NSA, DH, tq), lambda i, k, t: (i, k, 0, scored(t))),
                  k_spec, vt_spec, k_spec, vt_spec,
                  pl.BlockSpec((1, 1, LANES, tq), lambda i, k, t: (i, k, 0, scored(t))),
                  pl.BlockSpec((1, gate_t.shape[1], tq), lambda i, k, t: (i, 0, drained(t))),
                  pl.BlockSpec((1, gw, tq), lambda i, k, t: (i, k, drained(t))),
                  pl.BlockSpec((tq, gw), lambda i, k, t: (i * nt + drained(t), k))],
        out_specs=pl.BlockSpec((1, tq, gw), lambda i, k, t: (i, drained(t), k)),
        out_shape=jax.ShapeDtypeStruct((b, lq, NSA_WIDTH), bf16),
        scratch_shapes=[pltpu.VMEM((lq, lanes), f32), pltpu.VMEM((lq, lanes), f32),
                        pltpu.VMEM((8, lanes), f32), pltpu.VMEM((8, lanes), f32),
                        pltpu.VMEM((band, lanes), f32), pltpu.VMEM((band, lanes), f32),
                        pltpu.VMEM((8, lanes), f32), pltpu.VMEM((8, lanes), f32),
                        pltpu.VMEM((DH + ONES_ROWS, lanes), f32),
                        pltpu.VMEM((DH + ONES_ROWS, lanes), f32)],
        compiler_params=_cparams(("parallel", "parallel", "arbitrary")),
        name="slc_win",
    )(q_t, ks, vs_t, kw, vw_t, sel_t, gate_t, o_cmp_t, zn)


def _rms_gain(o, gain):
    return o * lax.rsqrt(jnp.mean(o * o, axis=-1, keepdims=True) + EPS) * gain


def _gla_kernel(q_ref, k_ref, v_ref, la_ref, z_ref, gain_ref, s0_ref, y_ref, st_ref, s_scr, *, tl, chunk):
    t = pl.program_id(1)

    @pl.when(t == 0)
    def _():
        s_scr[...] = s0_ref[0]

    c = chunk
    pairs = s_scr.shape[0]
    row = lax.broadcasted_iota(jnp.int32, (c, c), 0)
    col = lax.broadcasted_iota(jnp.int32, (c, c), 1)
    causal = col <= row
    tril = jnp.where(causal, 1.0, 0.0).astype(bf16)
    lane = lax.broadcasted_iota(jnp.int32, (1, LANES), 1)
    head_mask = [jnp.where(lane < DK, 1.0, 0.0), jnp.where(lane >= DK, 1.0, 0.0)]
    gain = gain_ref[...]

    chunks = []
    for ci, pr in [(ci, pr) for ci in range(tl // c) for pr in range(pairs)]:
        rows = slice(ci * c, (ci + 1) * c)
        grp = slice(pr * LANES, (pr + 1) * LANES)
        q, k, la = q_ref[rows, grp], k_ref[rows, grp], la_ref[rows, grp]
        hi, rest = la.astype(bf16), la - la.astype(bf16).astype(f32)
        mid, lo = rest.astype(bf16), (rest - rest.astype(bf16).astype(f32)).astype(bf16)
        bc = _dot(tril, hi) + _dot(tril, mid) + _dot(tril, lo)
        ref_row = bc[c // 2 - 1:c // 2, :]
        b_last = bc[c - 1:c, :]
        q_a = q * jnp.exp(bc - ref_row)
        k_a = (k * jnp.exp(ref_row - bc)).astype(bf16)
        q_s = q * jnp.exp(bc)
        k_s = k * jnp.exp(b_last - bc)
        o_intra, q_inter, add = [], [], None
        for h in range(2):
            mh = head_mask[h]
            vh = v_ref[rows, (2 * pr + h) * DV:(2 * pr + h + 1) * DV].astype(bf16)
            a = jnp.where(causal, _dot_nt((q_a * mh).astype(bf16), k_a), 0.0)
            o_intra.append(_dot(a.astype(bf16), vh))
            q_inter.append((q_s * mh).astype(bf16))
            u = _dot_tn(vh, (k_s * mh).astype(bf16))
            add = u if add is None else add + u
        chunks.append((rows, pr, jnp.exp(b_last), add, o_intra, q_inter))

    st = [s_scr[pr] for pr in range(pairs)]
    for rows, pr, decay, add, o_intra, q_inter in chunks:
        st_b = st[pr].astype(bf16)
        for h in range(2):
            cols = slice((2 * pr + h) * DV, (2 * pr + h + 1) * DV)
            o = o_intra[h] + _dot_nt(q_inter[h], st_b)
            y_ref[rows, cols] = (_rms_gain(o, gain) * z_ref[rows, cols]).astype(y_ref.dtype)
        st[pr] = st[pr] * decay + add
    for pr in range(pairs):
        s_scr[pr] = st[pr]
        st_ref[0, pr] = st[pr]


def _gla(gq, gk, gv, la, zg, gain, state0, b, tl, chunk):
    n = gq.shape[0]
    nt = n // b // tl
    pairs = H_GLA // 2
    qk = pl.BlockSpec((tl, H_GLA * DK), lambda i, t: (i * nt + t, 0))
    vz = pl.BlockSpec((tl, GLA_WIDTH), lambda i, t: (i * nt + t, 0))
    st_spec = pl.BlockSpec((1, pairs, DV, LANES), lambda i, t: (i, 0, 0, 0))
    gain2 = gain.reshape(1, DV)
    s0 = state0.reshape(b, pairs, 2, DK, DV).transpose(0, 1, 4, 2, 3).reshape(b, pairs, DV, LANES)
    y, st = pl.pallas_call(
        functools.partial(_gla_kernel, tl=tl, chunk=chunk),
        grid=(b, nt),
        in_specs=[qk, qk, vz, qk, vz, pl.BlockSpec((1, DV), lambda i, t: (0, 0)), st_spec],
        out_specs=(vz, st_spec),
        out_shape=(jax.ShapeDtypeStruct((n, GLA_WIDTH), bf16),
                   jax.ShapeDtypeStruct((b, pairs, DV, LANES), f32)),
        scratch_shapes=[pltpu.VMEM((pairs, DV, LANES), f32)],
        compiler_params=_cparams(("parallel", "arbitrary")),
        name="gla",
    )(gq, gk, gv, la, zg, gain2, s0)
    state = st.reshape(b, pairs, DV, 2, DK).transpose(0, 1, 3, 4, 2).reshape(b, H_GLA, DK, DV)
    return y, state


def _outproj_kernel(x_ref, yn_ref, yg_ref, w_ref, g_ref, y_ref):
    mix = _dot(yn_ref[...].astype(bf16), w_ref[:NSA_WIDTH, :]) + _dot(yg_ref[...].astype(bf16), w_ref[NSA_WIDTH:, :])
    y_ref[...] = _rms_gain(x_ref[...] + mix, g_ref[...])


def _outproj(x, y_nsa, y_gla, w_out, gain, tm):
    n = x.shape[0]
    tok = lambda w: pl.BlockSpec((tm, w), lambda i: (i, 0))
    g2 = gain.reshape(1, D_MODEL)
    return pl.pallas_call(
        _outproj_kernel,
        grid=(n // tm,),
        in_specs=[tok(D_MODEL), tok(NSA_WIDTH), tok(GLA_WIDTH),
                  pl.BlockSpec(w_out.shape, lambda i: (0, 0)), pl.BlockSpec(g2.shape, lambda i: (0, 0))],
        out_specs=tok(D_MODEL),
        out_shape=jax.ShapeDtypeStruct((n, D_MODEL), f32),
        compiler_params=_cparams(("parallel",)),
        name="outproj",
    )(x, y_nsa, y_gla, w_out, g2)


def _prompt_path(x, wts, w_t, w_tok):
    (norm_g, _, wa2_pad, ba, w1_big, cmp_bias, w2_big, gla_gain, w_out, out_gain) = wts
    b, lq, _ = x.shape
    (q_t, kvc_t, kvs_t, kvw_t, ks, kw, vs_t, vw_t, kvc_tok, gate_t, zn, gq, gk, gv, la, zg) = _inproj_prompt(
        x, norm_g, w_t, w_tok, wa2_pad, ba, tm=512)
    n_chunk = lq // CMP_STRIDE
    part = _cmp_part(kvc_tok.reshape(b * n_chunk, CHUNK_W), w1_big, tr=n_chunk)
    o_cmp_t, sel_t = _cmp_attn(part.reshape(b, n_chunk, PART_W), cmp_bias, w2_big, q_t, gate_t, tq=512)
    y_nsa = _slc_win(q_t, ks, vs_t, kw, vw_t, sel_t, gate_t, o_cmp_t, zn, tq=256, tk=512)
    y_gla, state = _gla(gq, gk, gv, la, zg, gla_gain, jnp.zeros((b, H_GLA, DK, DV), f32), b, tl=512,
                        chunk=GLA_CHUNK)
    y = _outproj(x.reshape(b * lq, D_MODEL), y_nsa.reshape(b * lq, NSA_WIDTH), y_gla, w_out, out_gain, tm=1024)
    kv5 = lambda a: a.reshape(b, 2, KVH, DH, -1).transpose(0, 4, 1, 2, 3)
    wlen = min(WINDOW, lq)
    return (y.reshape(b, lq, D_MODEL), kv5(kvc_t), kv5(kvs_t), kv5(kvw_t[:, :, lq - wlen:]), state)


def _pages_native(cache):
    n_pool = cache.shape[0]
    return cache.transpose(0, 2, 3, 4, 1).reshape(n_pool, 2, KVH * DH, cache.shape[1])


def _page_gather(make_copies):
    i = pl.program_id(0)
    slot = i % 2

    @pl.when(i == 0)
    def _():
        for c in make_copies(0, 0):
            c.start()

    @pl.when(i + 1 < pl.num_programs(0))
    def _():
        for c in make_copies(i + 1, 1 - slot):
            c.start()

    for c in make_copies(i, slot):
        c.wait()
    return slot


def _cmp_w1_pairs(cmp_w1):
    ratio = CMP_LEN // CMP_STRIDE
    w1r = cmp_w1.reshape(2, ratio, CMP_STRIDE // 2, 2, DH, CMP_HID)
    eye = jnp.eye(KVH, dtype=cmp_w1.dtype)
    big = jnp.einsum('crpjdh,kK->cpjKdrkh', w1r, eye)
    return big.reshape(2, CMP_STRIDE // 2, 2 * KVH * DH, ratio * KVH * CMP_HID).astype(bf16)


def _cmp_w2_heads(cmp_w2):
    eye = jnp.eye(KVH, dtype=cmp_w2.dtype)
    return jnp.einsum('chd,kK->ckhKd', cmp_w2, eye).reshape(2, KVH * CMP_HID, KVH * DH).astype(bf16)


def _group_sum_matrix(n_tok):
    r = np.arange(KVH * n_tok)[:, None]
    c = np.arange(H_NSA * n_tok)[None, :]
    m = ((c // (G_NSA * n_tok)) == (r // n_tok)) & ((c % n_tok) == (r % n_tok))
    return m.astype(np.float32)


def _cmp_attn_sample_kernel(pt_ref, cache_ref, perm_ref, w1_ref, bias_ref, w2_ref, cos_ref, sa_ref, sb_ref, q_ref,
                            gate_ref, cov_ref, gs_ref, gst_ref, o_ref, sel_ref, buf, rows_scr, sem,
                            *, n_pages, n_tok, past_len):
    def copies(seq, slot):
        return [pltpu.make_async_copy(cache_ref.at[pt_ref[seq, p]], buf.at[slot, p], sem.at[slot])
                for p in range(n_pages)]

    slot = _page_gather(copies)

    cpp = PAGE_SIZE // CMP_STRIDE
    perm = perm_ref[...]

    group = 8
    kd = KVH * DH

    def to_rows(g, carry):
        for u in range(group):
            p = g * group + u
            r0 = pl.multiple_of(p * cpp, cpp)
            t = _dot_nt(perm, buf[slot, p].reshape(2 * kd, PAGE_SIZE).astype(bf16))
            for c in range(2):
                for s in range(CMP_STRIDE):
                    rows_scr[c, s, pl.ds(r0, cpp), :] = t[s * cpp:(s + 1) * cpp, c * kd:(c + 1) * kd]
        return carry

    lax.fori_loop(0, n_pages // group, to_rows, 0)

    nc = n_pages * cpp
    kv = []
    for c in range(2):
        part = jnp.zeros((nc, w1_ref.shape[-1]), f32)
        for pr in range(CMP_STRIDE // 2):
            lhs = jnp.concatenate([rows_scr[c, 2 * pr], rows_scr[c, 2 * pr + 1]], axis=-1)
            part = part + _dot(lhs.astype(bf16), w1_ref[c, pr])
        half = part.shape[1] // 2
        pre = part[:, :half] + pltpu.roll(part[:, half:], nc - 1, 0)
        hid = _silu(pre + bias_ref[:, c * half:(c + 1) * half])
        kv.append(_dot(hid.astype(bf16), w2_ref[c]))
    k = _rope128(kv[0], cos_ref[...], sa_ref[...], sb_ref[...])
    v = kv[1]
    rows = H_NSA * n_tok
    qpos = past_len + lax.broadcasted_iota(jnp.int32, (rows, 1), 0) % n_tok
    end_c = lax.broadcasted_iota(jnp.int32, (1, nc), 1) * CMP_STRIDE + (CMP_LEN - 1)
    mask = end_c <= qpos
    p = _masked_softmax(_dot_nt(q_ref[0], k.astype(bf16)), mask, axis=-1)
    o_ref[0] = _dot(p.astype(bf16), v.astype(bf16)) * gate_ref[0][:, 0:1]
    hi, lo = _split_bf16(p)
    gs = gs_ref[...]
    ph, pl_ = _split_bf16(_dot(gs, hi) + _dot(gs, lo))
    imp = _dot(ph, cov_ref[...]) + _dot(pl_, cov_ref[...])
    n_blk = imp.shape[1]
    jblk = lax.broadcasted_iota(jnp.int32, imp.shape, 1)
    forced = (jblk == 0) | (jblk == n_blk - 1)
    sel = _topk_rows_by_rank(jnp.where(forced, FORCE_SCORE, imp), SLC_TOP - 1)
    sel_ref[0] = _dot(gst_ref[...], sel.astype(bf16)).astype(bf16)


def _cmp_attn_sample(cache_t, page_table, w1_pairs, bias, w2_heads, q_blk, gate_rows, n_tok):
    b, n_pages = page_table.shape
    past_len = n_pages * PAGE_SIZE
    nc = past_len // CMP_STRIDE
    rows = H_NSA * n_tok
    n_blk = past_len // SLC_LEN
    end_pos = jnp.arange(nc) * CMP_STRIDE + (CMP_LEN - 1)
    cos, sa, sb = _rope_tables(end_pos)
    cov = _cover_t(n_blk, nc).T
    gs = jnp.asarray(_group_sum_matrix(n_tok), dtype=bf16)
    gst = gs.T
    r = np.arange(PAGE_SIZE)
    cpp = PAGE_SIZE // CMP_STRIDE
    perm = jnp.asarray((r[None, :] == (r[:, None] % cpp) * CMP_STRIDE + r[:, None] // cpp).astype(np.float32),
                       dtype=bf16)
    full = lambda a: pl.BlockSpec(a.shape, lambda i, pt: (0,) * a.ndim)
    seq = lambda a: pl.BlockSpec((1,) + a.shape[1:], lambda i, pt: (i,) + (0,) * (a.ndim - 1))
    grid_spec = pltpu.PrefetchScalarGridSpec(
        num_scalar_prefetch=1,
        grid=(b,),
        in_specs=[pl.BlockSpec(memory_space=pl.ANY), full(perm), full(w1_pairs), full(bias), full(w2_heads),
                  full(cos), full(sa), full(sb), seq(q_blk), seq(gate_rows), full(cov), full(gs), full(gst)],
        out_specs=(pl.BlockSpec((1, rows, LANES), lambda i, pt: (i, 0, 0)),
                   pl.BlockSpec((1, rows, n_blk), lambda i, pt: (i, 0, 0))),
        scratch_shapes=[pltpu.VMEM((2, n_pages) + cache_t.shape[1:], f32),
                        pltpu.VMEM((2, CMP_STRIDE, nc, KVH * DH), f32),
                        pltpu.SemaphoreType.DMA((2,))],
    )
    return pl.pallas_call(
        functools.partial(_cmp_attn_sample_kernel, n_pages=n_pages, n_tok=n_tok, past_len=past_len),
        grid_spec=grid_spec,
        out_shape=(jax.ShapeDtypeStruct((b, rows, LANES), f32),
                   jax.ShapeDtypeStruct((b, rows, n_blk), bf16)),
        compiler_params=_cparams(("arbitrary",)),
        name="cmp_attn_sample",
    )(page_table, cache_t, perm, w1_pairs, bias, w2_heads, cos, sa, sb, q_blk, gate_rows, cov, gs, gst)


def _slc_win_sample_kernel(pt_ref, cache_ref, q_ref, sel_ref, e_ref, snew_ref, cw_ref, wnew_ref, wnewt_ref,
                           gate_ref, ocmp_ref, zn_ref, o_ref, wout_ref, buf, sem, *, n_pages, n_tok, win_off):
    def copies(seq, slot):
        return [pltpu.make_async_copy(cache_ref.at[pt_ref[seq, p], c],
                                      buf.at[slot, c, :, pl.ds(p * PAGE_SIZE, PAGE_SIZE)], sem.at[slot])
                for p in range(n_pages) for c in range(2)]

    slot = _page_gather(copies)
    rows = H_NSA * n_tok
    q = q_ref[0]
    tok = lax.broadcasted_iota(jnp.int32, (rows, 1), 0) % n_tok
    new_i = lax.broadcasted_iota(jnp.int32, (1, snew_ref.shape[1]), 1)
    new_ok = (new_i <= tok) & (new_i < n_tok)

    def attend(keys_t, vals_t, allowed, new_ref):
        s = jnp.where(allowed, _dot(q, keys_t), NEG)
        k_new = new_ref[0][:, :LANES].astype(bf16)
        v_new = new_ref[0][:, LANES:].astype(bf16)
        s_new = jnp.where(new_ok, _dot_nt(q, k_new), NEG)
        m = jnp.maximum(jnp.max(s, axis=-1, keepdims=True), jnp.max(s_new, axis=-1, keepdims=True))
        p = jnp.where(allowed, jnp.exp(s - m), 0.0)
        p_new = jnp.where(new_ok, jnp.exp(s_new - m), 0.0)
        l = jnp.sum(p, axis=-1, keepdims=True) + jnp.sum(p_new, axis=-1, keepdims=True)
        return (_dot_nt(p.astype(bf16), vals_t) + _dot(p_new.astype(bf16), v_new)) / l

    picked = _dot(sel_ref[0], e_ref[...]) > 0.5
    o_slc = attend(buf[slot, 0].astype(bf16), buf[slot, 1].astype(bf16), picked, snew_ref)

    wbuf = cw_ref.shape[-1]
    win_i = lax.broadcasted_iota(jnp.int32, (1, wbuf), 1)
    o_win = attend(cw_ref[0, 0].astype(bf16), cw_ref[0, 1].astype(bf16), win_i > tok + win_off, wnew_ref)

    lane = lax.broadcasted_iota(jnp.int32, (1, LANES), 1)
    for c in range(2):
        shifted = pltpu.roll(cw_ref[0, c], wbuf - n_tok, 1)
        tail = jnp.where(lane >= LANES - n_tok, wnewt_ref[0, c], shifted[:, wbuf - LANES:])
        wout_ref[0, c] = jnp.concatenate([shifted[:, :wbuf - LANES], tail], axis=-1)

    gate = gate_ref[0]
    o = ocmp_ref[0] + gate[:, 1:2] * o_slc + gate[:, 2:3] * o_win
    second_kvh = lax.broadcasted_iota(jnp.int32, (rows, 1), 0) >= G_NSA * n_tok
    o_ref[0] = jnp.where(second_kvh, o[:, DH:], o[:, :DH]) * zn_ref[0]


def _slc_win_sample(cache_t, page_table, q_blk, sel_rows, kvs_new, win_t, kvw_new, kvw_new_t, gate_rows,
                    o_cmp, zn_rows, n_tok):
    b, n_pages = page_table.shape
    rows = H_NSA * n_tok
    past_len = n_pages * PAGE_SIZE
    wbuf = win_t.shape[-1]
    e = _block_expand(sel_rows.shape[-1], past_len)
    full = lambda a: pl.BlockSpec(a.shape, lambda i, pt: (0,) * a.ndim)
    seq = lambda a: pl.BlockSpec((1,) + a.shape[1:], lambda i, pt: (i,) + (0,) * (a.ndim - 1))
    grid_spec = pltpu.PrefetchScalarGridSpec(
        num_scalar_prefetch=1,
        grid=(b,),
        in_specs=[pl.BlockSpec(memory_space=pl.ANY), seq(q_blk), seq(sel_rows), full(e), seq(kvs_new),
                  seq(win_t), seq(kvw_new), seq(kvw_new_t), seq(gate_rows), seq(o_cmp), seq(zn_rows)],
        out_specs=(pl.BlockSpec((1, rows, DH), lambda i, pt: (i, 0, 0)), seq(win_t)),
        scratch_shapes=[pltpu.VMEM((2, 2, KVH * DH, past_len), f32), pltpu.SemaphoreType.DMA((2,))],
    )
    return pl.pallas_call(
        functools.partial(_slc_win_sample_kernel, n_pages=n_pages, n_tok=n_tok, win_off=wbuf - WINDOW),
        grid_spec=grid_spec,
        out_shape=(jax.ShapeDtypeStruct((b, rows, DH), f32), jax.ShapeDtypeStruct(win_t.shape, f32)),
        compiler_params=_cparams(("arbitrary",)),
        name="slc_win_sample",
    )(page_table, cache_t, q_blk, sel_rows, e, kvs_new, win_t, kvw_new, kvw_new_t, gate_rows, o_cmp, zn_rows)


def _gla_sample(gq, gk, la, gv, zg, state, gain, b, n_tok):
    rows = 8
    pad = lambda a: jnp.pad(a.reshape(b, n_tok, -1), ((0, 0), (0, rows - n_tok), (0, 0))).reshape(b * rows, -1)
    y, s_new = _gla(pad(gq), pad(gk), pad(gv), pad(la), pad(zg), gain, state, b, tl=rows, chunk=rows)
    return y.reshape(b, rows, GLA_WIDTH)[:, :n_tok].reshape(b * n_tok, GLA_WIDTH), s_new


def _sample_path(x, cache_c, cache_s, cache_w, state, page_table, wts, w1_pairs, w2_heads):
    (norm_g, w_pack, wa2_pad, ba, w1_big, cmp_bias, w2_big, gla_gain, w_out, out_gain) = wts
    b, n_tok, _ = x.shape
    n = b * n_tok
    n_pages = page_table.shape[1]
    past_len = n_pages * PAGE_SIZE
    assert n_tok <= 8 and past_len % SLC_LEN == 0 and past_len // SLC_LEN <= LANES
    pos = past_len + jnp.arange(n) % n_tok
    (q_hm, kvc, kvs, kvw, _, _, _, _, gate, zn, gq, gk, gv, la, zg) = _inproj(
        x.reshape(1, n, D_MODEL), pos, norm_g, w_pack, wa2_pad, ba, tm=n)

    rows = H_NSA * n_tok
    q_rows = q_hm[0].reshape(H_NSA, b, n_tok, DH).transpose(1, 0, 2, 3)
    zero = jnp.zeros_like(q_rows[:, :G_NSA])
    q_blk = jnp.concatenate([jnp.concatenate([q_rows[:, :G_NSA], zero], axis=-1),
                             jnp.concatenate([zero, q_rows[:, G_NSA:]], axis=-1)], axis=1).reshape(b, rows, LANES)
    gate_rows = gate[:, :3 * H_NSA].reshape(b, n_tok, 3, H_NSA).transpose(0, 3, 1, 2).reshape(b, rows, 3)
    gate_rows = jnp.pad(gate_rows, ((0, 0), (0, 0), (0, LANES - 3)))
    pad_new = lambda a: jnp.pad(a.reshape(b, n_tok, KV_ROW), ((0, 0), (0, 8 - n_tok), (0, 0)))

    kvw_new_t = kvw.reshape(b, n_tok, 2, KVH * DH).transpose(0, 2, 3, 1)
    kvw_new_t = jnp.pad(kvw_new_t, ((0, 0), (0, 0), (0, 0), (LANES - n_tok, 0)))

    o_cmp, sel_rows = _cmp_attn_sample(_pages_native(cache_c), page_table, w1_pairs, cmp_bias, w2_heads,
                                       q_blk, gate_rows, n_tok)
    o_rows, win_t = _slc_win_sample(_pages_native(cache_s), page_table, q_blk, sel_rows, pad_new(kvs),
                                    _pages_native(cache_w), pad_new(kvw), kvw_new_t, gate_rows, o_cmp,
                                    zn.reshape(b, n_tok, H_NSA, DH).transpose(0, 2, 1, 3).reshape(b, rows, DH),
                                    n_tok)
    y_nsa = o_rows.reshape(b, H_NSA, n_tok, DH).transpose(0, 2, 1, 3).reshape(n, NSA_WIDTH)

    y_gla, s_new = _gla_sample(gq, gk, la, gv, zg, state, gla_gain, b, n_tok)
    y = _outproj(x.reshape(n, D_MODEL), y_nsa, y_gla, w_out, out_gain, tm=n)
    kv5 = lambda a: a.reshape(b, n_tok, 2, KVH, DH)
    win_new = win_t.reshape(b, 2, KVH, DH, -1).transpose(0, 4, 1, 2, 3)
    return (y.reshape(b, n_tok, D_MODEL), kv5(kvc), kv5(kvs), win_new, s_new)


def kernel(x_prompt, x_sample, cache_cmp_kv, cache_slc_kv, cache_win_kv, state_gla, page_table,
           norm_in_gain, w_in, cmp_pe, cmp_w1, cmp_b1, cmp_w2, gla_wa2, gla_ba, gla_norm_gain,
           w_out, norm_out_gain):
    assert w_in.shape[0] == 1, "single-layer step"
    wts = (norm_in_gain[0], _pack_w_in(w_in[0]), _pad_wa2(gla_wa2[0]), gla_ba[0],
           _cmp_w1_big(cmp_w1[0]), _cmp_bias(cmp_pe[0], cmp_w1[0], cmp_b1[0]), _cmp_w2_big(cmp_w2[0]),
           gla_norm_gain[0], w_out[0].astype(bf16), norm_out_gain)
    yp, cmp_p, slc_p, win_p, gla_p = _prompt_path(x_prompt, wts, *_pack_w_in_prompt(w_in[0]))
    ys, cmp_s, slc_s, win_s, gla_s = _sample_path(
        x_sample, cache_cmp_kv[0], cache_slc_kv[0], cache_win_kv[0], state_gla[0], page_table, wts,
        _cmp_w1_pairs(cmp_w1[0]), _cmp_w2_heads(cmp_w2[0]))
    return (yp, ys, cmp_p[None], cmp_s[None], slc_p[None], slc_s[None], win_p[None], win_s[None],
            gla_p[None], gla_s[None])
```

```python
import functools

import numpy as np
import jax
import jax.numpy as jnp
from jax import lax
from jax.experimental import pallas as pl
from jax.experimental.pallas import tpu as pltpu

f32 = jnp.float32
bf16 = jnp.bfloat16

D_MODEL = 1024
DH = 64
H_NSA = 8
KVH = 2
G_NSA = 4
NSA_WIDTH = H_NSA * DH
CMP_LEN = 32
CMP_STRIDE = 16
CMP_HID = 128
SLC_LEN = 64
SLC_TOP = 16
WINDOW = 512
ROT_HALF = 8
ROPE_THETA = 500000.0
FORCE_SCORE = 1.0e4
H_GLA = 4
DK = 64
DV = 128
GLA_WIDTH = H_GLA * DV
GLA_LR = 16
GLA_TAU = 16.0
GLA_CHUNK = 64
GLA_SUB = 16
EPS = 1e-6
NEG = -1.0e30
LOG2E = 1.4426950408889634
PAGE_SIZE = 128
KV_ROW = 2 * KVH * DH

IN_SIZES = (H_NSA * DH, KV_ROW, KV_ROW, KV_ROW, 3 * H_NSA, NSA_WIDTH,
            H_GLA * DK, H_GLA * DK, H_GLA * DV, GLA_LR, GLA_WIDTH)
IN_OFFSETS = [0] + [int(v) for v in np.cumsum(IN_SIZES)]

LANES = 128
VMEM_LIMIT = 56 * 1024 * 1024

P_Q, P_KVC, P_KVS, P_KVW, P_ZN, P_GQ, P_GK, P_GV, P_ZG, P_MISC = (
    0, 512, 768, 1024, 1280, 1792, 2048, 2304, 2816, 3328)
D_PACK = P_MISC + LANES
MISC_GATE = 0
MISC_GLR = 32


def _cparams(sem):
    return pltpu.CompilerParams(dimension_semantics=sem, vmem_limit_bytes=VMEM_LIMIT)


def _sigmoid(x):
    return 1.0 / (1.0 + jnp.exp(-x))


def _silu(x):
    return x * _sigmoid(x)


def _log_sigmoid(x):
    return jnp.minimum(x, 0.0) - jnp.log1p(jnp.exp(-jnp.abs(x)))


def _dot(a, b):
    return jnp.dot(a, b, preferred_element_type=f32)


def _dot_nt(a, b):
    return lax.dot_general(a, b, (((1,), (1,)), ((), ())), preferred_element_type=f32)


def _dot_tn(a, b):
    return lax.dot_general(a, b, (((0,), (0,)), ((), ())), preferred_element_type=f32)


def _rope_tables(pos):
    n = pos.shape[0]
    inv = ROPE_THETA ** (-(jnp.arange(ROT_HALF, dtype=f32) / ROT_HALF))
    ang = pos.astype(f32)[:, None] * inv[None, :]
    cos, sin = jnp.cos(ang), jnp.sin(ang)
    z8 = jnp.zeros((n, ROT_HALF), f32)
    rest = DH - 2 * ROT_HALF
    c64 = jnp.concatenate([cos, cos, jnp.ones((n, rest), f32)], axis=-1)
    sa64 = jnp.concatenate([-sin, z8, jnp.zeros((n, rest), f32)], axis=-1)
    sb64 = jnp.concatenate([z8, sin, jnp.zeros((n, rest), f32)], axis=-1)
    tile = lambda t: jnp.concatenate([t, t], axis=-1)
    return tile(c64), tile(sa64), tile(sb64)


def _rope128(v, cos, sa, sb):
    return v * cos + pltpu.roll(v, LANES - ROT_HALF, 1) * sa + pltpu.roll(v, ROT_HALF, 1) * sb


def _inproj_kernel(x_ref, g_ref, w_ref, cos_ref, sa_ref, sb_ref, wa2_ref, ba_ref,
                   q_ref, kvc_ref, kvs_ref, kvw_ref, ks_ref, vs_ref, kw_ref, vw_ref,
                   gate_ref, zn_ref, gq_ref, gk_ref, gv_ref, la_ref, zg_ref):
    x = x_ref[...]
    ms = jnp.mean(x * x, axis=-1, keepdims=True)
    hn = (x * lax.rsqrt(ms + EPS) * g_ref[...]).astype(bf16)
    cos, sa, sb = cos_ref[...], sa_ref[...], sb_ref[...]

    def proj(off, width):
        return _dot(hn, w_ref[:, off:off + width])

    qp = proj(P_Q, NSA_WIDTH)
    for c in range(NSA_WIDTH // LANES):
        r = _rope128(qp[:, c * LANES:(c + 1) * LANES], cos, sa, sb) * (DH ** -0.5)
        q_ref[0, 2 * c] = r[:, :DH].astype(bf16)
        q_ref[0, 2 * c + 1] = r[:, DH:].astype(bf16)

    kvc_ref[...] = proj(P_KVC, KV_ROW)

    for off, kv_ref, k_ref, v_ref in ((P_KVS, kvs_ref, ks_ref, vs_ref), (P_KVW, kvw_ref, kw_ref, vw_ref)):
        p = proj(off, KV_ROW)
        k = _rope128(p[:, :LANES], cos, sa, sb)
        v = p[:, LANES:]
        kv_ref[:, :LANES] = k
        kv_ref[:, LANES:] = v
        for h in range(KVH):
            k_ref[0, h] = k[:, h * DH:(h + 1) * DH].astype(bf16)
            v_ref[0, h] = v[:, h * DH:(h + 1) * DH].astype(bf16)

    zn_ref[...] = _silu(proj(P_ZN, NSA_WIDTH))
    gq_ref[...] = proj(P_GQ, H_GLA * DK) * (DK ** -0.5)
    gk_ref[...] = proj(P_GK, H_GLA * DK)
    gv_ref[...] = proj(P_GV, H_GLA * DV)
    zg_ref[...] = _silu(proj(P_ZG, GLA_WIDTH))

    misc = proj(P_MISC, LANES)
    gate_ref[...] = _sigmoid(misc)
    xa = _dot(misc.astype(bf16), wa2_ref[...]) + ba_ref[...]
    la_ref[...] = _log_sigmoid(xa) / GLA_TAU


def _pack_w_in(w_in):
    o = IN_OFFSETS
    seg = lambda i: w_in[:, o[i]:o[i + 1]]
    misc = jnp.zeros((D_MODEL, LANES), w_in.dtype)
    misc = misc.at[:, MISC_GATE:MISC_GATE + 3 * H_NSA].set(seg(4))
    misc = misc.at[:, MISC_GLR:MISC_GLR + GLA_LR].set(seg(9))
    cols = [seg(0), seg(1), seg(2), seg(3), seg(5), seg(6), seg(7), seg(8), seg(10), misc]
    return jnp.concatenate(cols, axis=1).astype(bf16)


def _pad_wa2(wa2):
    pad = jnp.zeros((LANES, H_GLA * DK), wa2.dtype).at[MISC_GLR:MISC_GLR + GLA_LR].set(wa2)
    return pad.astype(bf16)


def _inproj(x, pos, norm_g, w_pack, wa2_pad, ba, tm):
    bk, lk, _ = x.shape
    n = bk * lk
    nt = lk // tm
    cos, sa, sb = _rope_tables(pos)
    tok = lambda w: pl.BlockSpec((tm, w), lambda i: (i, 0))
    tab = pl.BlockSpec((tm, LANES), lambda i: (i % nt, 0))
    full = lambda a: pl.BlockSpec(a.shape, lambda i: (0,) * a.ndim)
    hm = lambda h: pl.BlockSpec((1, h, tm, DH), lambda i: (i // nt, 0, i % nt, 0))
    g2 = norm_g.reshape(1, D_MODEL)
    ba2 = ba.reshape(1, H_GLA * DK)
    out_shape = (
        jax.ShapeDtypeStruct((bk, H_NSA, lk, DH), bf16),
        jax.ShapeDtypeStruct((n, KV_ROW), f32),
        jax.ShapeDtypeStruct((n, KV_ROW), f32),
        jax.ShapeDtypeStruct((n, KV_ROW), f32),
        jax.ShapeDtypeStruct((bk, KVH, lk, DH), bf16),
        jax.ShapeDtypeStruct((bk, KVH, lk, DH), bf16),
        jax.ShapeDtypeStruct((bk, KVH, lk, DH), bf16),
        jax.ShapeDtypeStruct((bk, KVH, lk, DH), bf16),
        jax.ShapeDtypeStruct((n, LANES), f32),
        jax.ShapeDtypeStruct((n, NSA_WIDTH), f32),
        jax.ShapeDtypeStruct((n, H_GLA * DK), f32),
        jax.ShapeDtypeStruct((n, H_GLA * DK), f32),
        jax.ShapeDtypeStruct((n, H_GLA * DV), f32),
        jax.ShapeDtypeStruct((n, H_GLA * DK), f32),
        jax.ShapeDtypeStruct((n, GLA_WIDTH), f32),
    )
    out_specs = (hm(H_NSA), tok(KV_ROW), tok(KV_ROW), tok(KV_ROW), hm(KVH), hm(KVH), hm(KVH), hm(KVH),
                 tok(LANES), tok(NSA_WIDTH), tok(H_GLA * DK), tok(H_GLA * DK), tok(H_GLA * DV),
                 tok(H_GLA * DK), tok(GLA_WIDTH))
    return pl.pallas_call(
        _inproj_kernel,
        grid=(n // tm,),
        in_specs=[tok(D_MODEL), full(g2), full(w_pack), tab, tab, tab, full(wa2_pad), full(ba2)],
        out_specs=out_specs,
        out_shape=out_shape,
        compiler_params=_cparams(("parallel",)),
        name="inproj",
    )(x.reshape(n, D_MODEL), g2, w_pack, cos, sa, sb, wa2_pad, ba2)


PT_Q, PT_KVC, PT_KVS, PT_KVW, PT_GATE = 0, 512, 768, 1024, 1280
PT_ROWS = PT_GATE + 32
PK_ZN, PK_GQ, PK_GK, PK_GV, PK_ZG, PK_MISC = 0, 512, 768, 1024, 1536, 2048
PK_COLS = PK_MISC + LANES


def _inproj_prompt_kernel(x_ref, g_ref, wt_ref, w_ref, cos_ref, sin_ref, wa2_ref, ba_ref,
                          qt_ref, kvct_ref, kvst_ref, kvwt_ref, ks_ref, kw_ref, vst_ref, vwt_ref,
                          gatet_ref, zn_ref, gq_ref, gk_ref, gv_ref, la_ref, zg_ref):
    x = x_ref[...]
    ms = jnp.mean(x * x, axis=-1, keepdims=True)
    hn = (x * lax.rsqrt(ms + EPS) * g_ref[...]).astype(bf16)
    cos_t, sin_t = cos_ref[...], sin_ref[...]

    def proj_t(off, rows):
        return _dot_nt(wt_ref[off:off + rows, :], hn)

    def rope_rows(v):
        x1, x2 = v[:ROT_HALF], v[ROT_HALF:2 * ROT_HALF]
        return jnp.concatenate([x1 * cos_t - x2 * sin_t, x2 * cos_t + x1 * sin_t, v[2 * ROT_HALF:]], axis=0)

    q_t = proj_t(PT_Q, NSA_WIDTH)
    for h in range(H_NSA):
        qt_ref[0, h] = (rope_rows(q_t[h * DH:(h + 1) * DH]) * (DH ** -0.5 * LOG2E)).astype(bf16)

    kvct_ref[0] = proj_t(PT_KVC, KV_ROW)

    for off, kvt_ref, k_ref, vt_ref in ((PT_KVS, kvst_ref, ks_ref, vst_ref), (PT_KVW, kvwt_ref, kw_ref, vwt_ref)):
        t = proj_t(off, KV_ROW)
        k_t = jnp.concatenate([rope_rows(t[h * DH:(h + 1) * DH]) for h in range(KVH)], axis=0)
        kvt_ref[0, :LANES] = k_t
        kvt_ref[0, LANES:] = t[LANES:]
        k_tok = k_t.T
        for h in range(KVH):
            k_ref[0, h] = k_tok[:, h * DH:(h + 1) * DH].astype(bf16)
            vt_ref[0, h] = t[LANES + h * DH:LANES + (h + 1) * DH].astype(bf16)

    gatet_ref[0] = _sigmoid(proj_t(PT_GATE, PT_ROWS - PT_GATE))

    def proj(off, width):
        return _dot(hn, w_ref[:, off:off + width])

    zn_ref[...] = _silu(proj(PK_ZN, NSA_WIDTH))
    gq_ref[...] = proj(PK_GQ, H_GLA * DK) * (DK ** -0.5)
    gk_ref[...] = proj(PK_GK, H_GLA * DK)
    gv_ref[...] = proj(PK_GV, H_GLA * DV)
    zg_ref[...] = _silu(proj(PK_ZG, GLA_WIDTH))
    misc = proj(PK_MISC, LANES)
    xa = _dot(misc.astype(bf16), wa2_ref[...]) + ba_ref[...]
    la_ref[...] = _log_sigmoid(xa) / GLA_TAU


def _pack_w_in_prompt(w_in):
    o = IN_OFFSETS
    seg = lambda i: w_in[:, o[i]:o[i + 1]]
    gate_t = jnp.zeros((PT_ROWS - PT_GATE, D_MODEL), w_in.dtype).at[:3 * H_NSA].set(seg(4).T)
    w_t = jnp.concatenate([seg(0).T, seg(1).T, seg(2).T, seg(3).T, gate_t], axis=0).astype(bf16)
    misc = jnp.zeros((D_MODEL, LANES), w_in.dtype).at[:, MISC_GLR:MISC_GLR + GLA_LR].set(seg(9))
    w_tok = jnp.concatenate([seg(5), seg(6), seg(7), seg(8), seg(10), misc], axis=1).astype(bf16)
    return w_t, w_tok


def _inproj_prompt(x, norm_g, w_t, w_tok, wa2_pad, ba, tm):
    b, lq, _ = x.shape
    n = b * lq
    nt = lq // tm
    inv = ROPE_THETA ** (-(jnp.arange(ROT_HALF, dtype=f32) / ROT_HALF))
    ang = inv[:, None] * jnp.arange(lq).astype(f32)[None, :]
    cos_t, sin_t = jnp.cos(ang), jnp.sin(ang)
    tok = lambda w: pl.BlockSpec((tm, w), lambda i: (i, 0))
    tab = pl.BlockSpec((ROT_HALF, tm), lambda i: (0, i % nt))
    full = lambda a: pl.BlockSpec(a.shape, lambda i: (0,) * a.ndim)
    feat = lambda r: pl.BlockSpec((1, r, tm), lambda i: (i // nt, 0, i % nt))
    headf = lambda h: pl.BlockSpec((1, h, DH, tm), lambda i: (i // nt, 0, 0, i % nt))
    headt = lambda h: pl.BlockSpec((1, h, tm, DH), lambda i: (i // nt, 0, i % nt, 0))
    g2 = norm_g.reshape(1, D_MODEL)
    ba2 = ba.reshape(1, H_GLA * DK)
    sds = jax.ShapeDtypeStruct
    out_shape = (
        sds((b, H_NSA, DH, lq), bf16),
        sds((b, KV_ROW, lq), f32), sds((b, KV_ROW, lq), f32), sds((b, KV_ROW, lq), f32),
        sds((b, KVH, lq, DH), bf16), sds((b, KVH, lq, DH), bf16),
        sds((b, KVH, DH, lq), bf16), sds((b, KVH, DH, lq), bf16),
        sds((b, PT_ROWS - PT_GATE, lq), f32),
        sds((n, NSA_WIDTH), f32), sds((n, H_GLA * DK), f32), sds((n, H_GLA * DK), f32),
        sds((n, H_GLA * DV), f32), sds((n, H_GLA * DK), f32), sds((n, GLA_WIDTH), f32),
    )
    out_specs = (headf(H_NSA), feat(KV_ROW), feat(KV_ROW), feat(KV_ROW), headt(KVH), headt(KVH),
                 headf(KVH), headf(KVH), feat(PT_ROWS - PT_GATE),
                 tok(NSA_WIDTH), tok(H_GLA * DK), tok(H_GLA * DK), tok(H_GLA * DV), tok(H_GLA * DK),
                 tok(GLA_WIDTH))
    return pl.pallas_call(
        _inproj_prompt_kernel,
        grid=(n // tm,),
        in_specs=[tok(D_MODEL), full(g2), full(w_t), full(w_tok), tab, tab, full(wa2_pad), full(ba2)],
        out_specs=out_specs,
        out_shape=out_shape,
        compiler_params=_cparams(("parallel",)),
        name="inproj_prompt",
    )(x.reshape(n, D_MODEL), g2, w_t, w_tok, cos_t, sin_t, wa2_pad, ba2)


HID_W = 2 * KVH * CMP_HID
CHUNKS_PER_PAGE = PAGE_SIZE // CMP_STRIDE
PAGE_GROUP = 8


def _cmp_w1_pairs(cmp_w1):
    ratio = CMP_LEN // CMP_STRIDE
    w1r = cmp_w1.reshape(2, ratio, CMP_STRIDE // 2, 2, DH, CMP_HID)
    eye = jnp.eye(KVH, dtype=cmp_w1.dtype)
    big = jnp.einsum('crpjdh,kK->cpjKdrkh', w1r, eye)
    return big.reshape(2, CMP_STRIDE // 2, 2 * KVH * DH, ratio * KVH * CMP_HID).astype(bf16)


def _cmp_w2_heads(cmp_w2):
    eye = jnp.eye(KVH, dtype=cmp_w2.dtype)
    return jnp.einsum('chd,kK->ckhKd', cmp_w2, eye).reshape(2, KVH * CMP_HID, KVH * DH).astype(bf16)


def _row_gather_matrix():
    r = np.arange(PAGE_SIZE)
    src = (r % CHUNKS_PER_PAGE) * CMP_STRIDE + r // CHUNKS_PER_PAGE
    return jnp.asarray((r[None, :] == src[:, None]).astype(np.float32), dtype=bf16)


def _compress_pages(page_at, n_pages, perm, rows_scr, w1_ref, bias_ref, w2_ref):
    cpp = CHUNKS_PER_PAGE
    kd = KVH * DH

    def to_rows(g, carry):
        for u in range(PAGE_GROUP):
            p = g * PAGE_GROUP + u
            r0 = pl.multiple_of(p * cpp, cpp)
            t = _dot_nt(perm, page_at(p).astype(bf16))
            for c in range(2):
                for s in range(CMP_STRIDE):
                    rows_scr[c, s, pl.ds(r0, cpp), :] = t[s * cpp:(s + 1) * cpp, c * kd:(c + 1) * kd]
        return carry

    lax.fori_loop(0, n_pages // PAGE_GROUP, to_rows, 0)

    nc = n_pages * cpp
    kv = []
    for c in range(2):
        part = jnp.zeros((nc, w1_ref.shape[-1]), f32)
        for pr in range(CMP_STRIDE // 2):
            lhs = jnp.concatenate([rows_scr[c, 2 * pr], rows_scr[c, 2 * pr + 1]], axis=-1)
            part = part + _dot(lhs.astype(bf16), w1_ref[c, pr])
        half = part.shape[1] // 2
        pre = part[:, :half] + pltpu.roll(part[:, half:], nc - 1, 0)
        hid = _silu(pre + bias_ref[:, c * half:(c + 1) * half])
        kv.append(_dot(hid.astype(bf16), w2_ref[c]))
    return kv


def _cmp_bias_kernel(pe_ref, w1_ref, b1_ref, o_ref):
    for c in range(2):
        o_ref[c] = _dot(pe_ref[c], w1_ref[c]) + b1_ref[c]


def _cmp_bias(cmp_pe, cmp_w1, cmp_b1):
    pe = jnp.broadcast_to(cmp_pe.reshape(2, 1, CMP_LEN * DH), (2, 8, CMP_LEN * DH))
    b1 = jnp.broadcast_to(cmp_b1.reshape(2, 1, CMP_HID), (2, 8, CMP_HID))
    out = pl.pallas_call(
        _cmp_bias_kernel,
        out_shape=jax.ShapeDtypeStruct((2, 8, CMP_HID), f32),
        name="cmp_bias",
    )(pe, cmp_w1, b1)
    row = out[:, 0, :]
    return jnp.broadcast_to(row[:, None, :], (2, KVH, CMP_HID)).reshape(1, HID_W)


def _masked_softmax(s, mask, axis, exp=jnp.exp):
    s = jnp.where(mask, s, NEG)
    e = exp(s - jnp.max(s, axis=axis, keepdims=True))
    return jnp.where(mask, e * (1.0 / jnp.sum(e, axis=axis, keepdims=True)), 0.0)


def _split_bf16(x):
    hi = x.astype(bf16)
    return hi, (x - hi.astype(f32)).astype(bf16)


def _topk_mask(score, k, axis):
    n = score.shape[axis]
    idx = lax.broadcasted_iota(jnp.int32, score.shape, axis)
    sel = jnp.zeros(score.shape, f32)
    for _ in range(k):
        m = jnp.max(score, axis=axis, keepdims=True)
        first = jnp.min(jnp.where(score == m, idx, n), axis=axis, keepdims=True)
        pick = idx == first
        sel = jnp.where(pick, 1.0, sel)
        score = jnp.where(pick, NEG, score)
    return sel


def _topk_rows_by_rank(score, k):
    r, n = score.shape
    cols = jnp.concatenate([score, jnp.zeros((n - r, n), f32)], axis=0).T
    earlier = jnp.where(lax.broadcasted_iota(jnp.int32, (n, n), 0) < lax.broadcasted_iota(jnp.int32, (n, n), 1),
                        1.0, 0.0)
    rows = []
    for c in range(r):
        col, row = cols[:, c:c + 1], score[c:c + 1, :]
        before = jnp.where(col > row, 1.0, jnp.where(col == row, earlier, 0.0))
        rows.append(jnp.where(jnp.sum(before, axis=0, keepdims=True) < k, 1.0, 0.0))
    return jnp.concatenate(rows, axis=0)


def _cover_t(n_slc, n_chunk):
    start = np.arange(n_chunk)[None, :] * CMP_STRIDE
    j = np.arange(n_slc)[:, None]
    cov = (start < (j + 1) * SLC_LEN) & (start + CMP_LEN > j * SLC_LEN)
    return jnp.asarray(cov.astype(np.float32), dtype=bf16)


def _cmp_attn_kernel(kvct_ref, perm_ref, w1_ref, bias_ref, w2_ref, cos_ref, sa_ref, sb_ref, qt_ref, gatet_ref,
                     cov_ref, ot_ref, selt_ref, rows_scr, kc_ref, vct_ref, *, tq):
    t = pl.program_id(1)

    @pl.when(t == 0)
    def _():
        page_at = lambda p: kvct_ref[0, :, pl.ds(pl.multiple_of(p * PAGE_SIZE, PAGE_SIZE), PAGE_SIZE)]
        k, v = _compress_pages(page_at, kvct_ref.shape[2] // PAGE_SIZE, perm_ref[...], rows_scr,
                               w1_ref, bias_ref, w2_ref)
        k = _rope128(k, cos_ref[...], sa_ref[...], sb_ref[...])
        v_t = v.T
        for h in range(KVH):
            kc_ref[h] = k[:, h * DH:(h + 1) * DH].astype(bf16)
            vct_ref[h] = v_t[h * DH:(h + 1) * DH].astype(bf16)

    nc = kc_ref.shape[1]
    nb = cov_ref.shape[0]
    qpos = t * tq + lax.broadcasted_iota(jnp.int32, (1, tq), 1)
    end_pos = lax.broadcasted_iota(jnp.int32, (nc, 1), 0) * CMP_STRIDE + (CMP_LEN - 1)
    mask_t = end_pos <= qpos
    gate_t = gatet_ref[0]
    cov = cov_ref[...]

    jblk = lax.broadcasted_iota(jnp.int32, (nb, tq), 0)
    qblk = qpos // SLC_LEN
    valid = jblk <= qblk
    forced = (jblk == 0) | (jblk == qblk) | (jblk == qblk - 1)

    scores = []
    for kh in range(KVH):
        kc, vc_t = kc_ref[kh], vct_ref[kh]
        psum = jnp.zeros((nc, tq), f32)
        for g in range(G_NSA):
            h = kh * G_NSA + g
            p_t = _masked_softmax(_dot(kc, qt_ref[0, h]), mask_t, axis=0, exp=jnp.exp2)
            ot_ref[0, h * DH:(h + 1) * DH] = _dot(vc_t, p_t.astype(bf16)) * gate_t[h:h + 1]
            psum = psum + p_t
        hi, lo = _split_bf16(psum)
        imp = _dot(cov, hi) + _dot(cov, lo)
        scores.append(jnp.where(valid, jnp.where(forced, FORCE_SCORE, imp), -1.0))
    picked = _topk_mask(jnp.concatenate(scores, axis=1), SLC_TOP, axis=0)
    for kh in range(KVH):
        sel_t = jnp.where(valid & (picked[:, kh * tq:(kh + 1) * tq] > 0.5), 0.0, NEG)
        if nb < LANES:
            sel_t = jnp.concatenate([sel_t, jnp.full((LANES - nb, tq), NEG, f32)], axis=0)
        selt_ref[0, kh] = sel_t


def _cmp_attn(kvc_t, w1_pairs, bias, w2_heads, q_t, gate_t, tq):
    b, _, lq = kvc_t.shape
    assert lq % (PAGE_SIZE * PAGE_GROUP) == 0
    nc = lq // CMP_STRIDE
    n_slc = lq // SLC_LEN
    perm = _row_gather_matrix()
    end_pos = jnp.arange(nc) * CMP_STRIDE + (CMP_LEN - 1)
    cos, sa, sb = _rope_tables(end_pos)
    cov = _cover_t(n_slc, nc)
    full = lambda a: pl.BlockSpec(a.shape, lambda i, t: (0,) * a.ndim)
    return pl.pallas_call(
        functools.partial(_cmp_attn_kernel, tq=tq),
        grid=(b, lq // tq),
        in_specs=[pl.BlockSpec((1, KV_ROW, lq), lambda i, t: (i, 0, 0)),
                  full(perm), full(w1_pairs), full(bias), full(w2_heads), full(cos), full(sa), full(sb),
                  pl.BlockSpec((1, H_NSA, DH, tq), lambda i, t: (i, 0, 0, t)),
                  pl.BlockSpec((1, gate_t.shape[1], tq), lambda i, t: (i, 0, t)),
                  full(cov)],
        out_specs=(pl.BlockSpec((1, NSA_WIDTH, tq), lambda i, t: (i, 0, t)),
                   pl.BlockSpec((1, KVH, LANES, tq), lambda i, t: (i, 0, 0, t))),
        out_shape=(jax.ShapeDtypeStruct((b, NSA_WIDTH, lq), f32),
                   jax.ShapeDtypeStruct((b, KVH, LANES, lq), f32)),
        scratch_shapes=[pltpu.VMEM((2, CMP_STRIDE, nc, KVH * DH), f32),
                        pltpu.VMEM((KVH, nc, DH), bf16), pltpu.VMEM((KVH, DH, nc), bf16)],
        compiler_params=_cparams(("parallel", "arbitrary")),
        name="cmp_attn",
    )(kvc_t, perm, w1_pairs, bias, w2_heads, cos, sa, sb, q_t, gate_t, cov)


def _block_expand(n_slc_pad, n_keys):
    e = (np.arange(n_keys)[None, :] // SLC_LEN) == np.arange(n_slc_pad)[:, None]
    return jnp.asarray(e.astype(np.float32), dtype=bf16)


def _mask_bias_t(allowed):
    bias = jnp.where(allowed, 0.0, NEG)
    return jnp.concatenate([bias] * G_NSA, axis=1)


def _sublane_fold(x, op):
    return functools.reduce(op, [x[r:r + 8] for r in range(0, x.shape[0], 8)])


def _slc_win_kernel(qt_ref, ks_ref, vst_ref, kw_ref, vwt_ref, selt_ref, gatet_ref, ocmpt_ref, zn_ref,
                    o_ref, s_a, s_b, m_a, m_b, ws_a, ws_b, wm_a, wm_b, acc_ref, wacc_ref, *, tq, tk, nt):
    t = pl.program_id(2)

    @pl.when(t == 0)
    def _():
        m_b[...] = jnp.full(m_b.shape, NEG, f32)
        ws_b[...] = jnp.zeros(ws_b.shape, f32)
        wm_b[...] = jnp.zeros(wm_b.shape, f32)

    step = functools.partial(_slc_win_step, qt_ref, ks_ref, vst_ref, kw_ref, vwt_ref, selt_ref, gatet_ref,
                             ocmpt_ref, zn_ref, o_ref, acc_ref, wacc_ref, tq=tq, tk=tk, nt=nt)

    @pl.when(t % 2 == 0)
    def _():
        step(s_a, s_b, m_a, m_b, ws_a, ws_b, wm_a, wm_b)

    @pl.when(t % 2 == 1)
    def _():
        step(s_b, s_a, m_b, m_a, ws_b, ws_a, wm_b, wm_a)


ONES_ROWS = 16


def _slc_win_step(qt_ref, ks_ref, vst_ref, kw_ref, vwt_ref, selt_ref, gatet_ref, ocmpt_ref, zn_ref, o_ref,
                  acc_ref, wacc_ref, s_cur, s_prev, m_cur, m_last, ws_cur, ws_prev, wm_cur, wm_last,
                  *, tq, tk, nt):
    kh = pl.program_id(1)
    t = pl.program_id(2)
    lq = ks_ref.shape[2]
    band = WINDOW + tq
    blocks_per_tile = tk // SLC_LEN
    key_tiles = lambda tile: (tile * tq + tq + tk - 1) // tk
    n_score = jnp.where(t < nt, key_tiles(t), 0)
    n_apply = jnp.where(t >= 1, key_tiles(t - 1), 0)
    n_plain = jnp.where(t < nt, (t * tq) // tk, 0)
    q0 = t * tq
    qpos = q0 + lax.broadcasted_iota(jnp.int32, (1, tq), 1)
    q_t = jnp.concatenate([qt_ref[0, g] for g in range(G_NSA)], axis=1)
    ones = jnp.ones((ONES_ROWS, tk), bf16)

    m_cur[...] = jnp.full(m_cur.shape, NEG, f32)
    m_prev = jnp.max(m_last[...], axis=0, keepdims=True)
    acc_ref[...] = jnp.zeros(acc_ref.shape, f32)

    def score(kt, causal):
        k0 = pl.multiple_of(kt * tk, tk)
        j0 = pl.multiple_of(kt * blocks_per_tile, blocks_per_tile)
        sel = selt_ref[0, 0, pl.ds(j0, blocks_per_tile), :]
        bias = jnp.concatenate([jnp.broadcast_to(sel[j:j + 1], (SLC_LEN, tq)) for j in range(blocks_per_tile)],
                               axis=0)
        if causal:
            kpos = k0 + lax.broadcasted_iota(jnp.int32, (tk, 1), 0)
            bias = jnp.where(kpos <= qpos, bias, NEG)
        s = _dot(ks_ref[0, 0, pl.ds(k0, tk), :], q_t) + jnp.concatenate([bias] * G_NSA, axis=1)
        s_cur[pl.ds(k0, tk), :] = s
        m_cur[...] = jnp.maximum(m_cur[...], _sublane_fold(s, jnp.maximum))

    def apply(kt):
        k0 = pl.multiple_of(kt * tk, tk)
        p = jnp.exp2(s_prev[pl.ds(k0, tk), :] - m_prev).astype(bf16)
        acc_ref[...] += _dot(jnp.concatenate([vst_ref[0, 0, :, pl.ds(k0, tk)], ones], axis=0), p)

    def both_body(kt, carry):
        score(kt, causal=False)
        apply(kt)
        return carry

    def score_body(kt, carry):
        score(kt, causal=True)
        return carry

    def apply_body(kt, carry):
        apply(kt)
        return carry

    def both2_body(i, carry):
        for u in range(2):
            score(2 * i + u, causal=False)
            apply(2 * i + u)
        return carry

    n_both = jnp.minimum(n_plain, n_apply)
    lax.fori_loop(0, n_both // 2, both2_body, 0)
    lax.fori_loop(n_both // 2 * 2, n_both, both_body, 0)
    lax.fori_loop(n_both, n_score, score_body, 0)
    lax.fori_loop(n_both, n_apply, apply_body, 0)

    w0 = pl.multiple_of(jnp.clip(q0 - WINDOW, 0, lq - band), tq)
    w0_prev = pl.multiple_of(jnp.clip(q0 - tq - WINDOW, 0, lq - band), tq)
    wm_prev = jnp.max(wm_last[...], axis=0, keepdims=True)
    ones_w = jnp.ones((ONES_ROWS, tq), bf16)
    wmax, wacc = None, None
    for j in range(band // tq):
        rows = pl.ds(j * tq, tq)
        kpos = w0 + j * tq + lax.broadcasted_iota(jnp.int32, (tq, 1), 0)
        s = (_dot(kw_ref[0, 0, pl.ds(pl.multiple_of(w0 + j * tq, tq), tq), :], q_t)
             + _mask_bias_t((kpos <= qpos) & (kpos > qpos - WINDOW)))
        ws_cur[rows, :] = s
        fold = _sublane_fold(s, jnp.maximum)
        wmax = fold if wmax is None else jnp.maximum(wmax, fold)
        p = jnp.exp2(ws_prev[rows, :] - wm_prev).astype(bf16)
        v_ext = jnp.concatenate([vwt_ref[0, 0, :, pl.ds(pl.multiple_of(w0_prev + j * tq, tq), tq)], ones_w], axis=0)
        part = _dot(v_ext, p)
        wacc = part if wacc is None else wacc + part
    wm_cur[...] = wmax
    wacc_ref[...] = wacc

    @pl.when(t >= 1)
    def _():
        o_slc = acc_ref[:DH] / acc_ref[DH:DH + 1]
        o_win = wacc_ref[:DH] / wacc_ref[DH:DH + 1]
        gate_t = gatet_ref[0]
        row = lambda c, g: jnp.where(kh == 0, gate_t[c * H_NSA + g:c * H_NSA + g + 1],
                                     gate_t[c * H_NSA + G_NSA + g:c * H_NSA + G_NSA + g + 1])
        g_slc = jnp.concatenate([row(1, g) for g in range(G_NSA)], axis=1)
        g_win = jnp.concatenate([row(2, g) for g in range(G_NSA)], axis=1)
        o_t = g_slc * o_slc + g_win * o_win
        o_heads = jnp.concatenate([o_t[:, g * tq:(g + 1) * tq] for g in range(G_NSA)], axis=0)
        o_ref[0] = ((ocmpt_ref[0] + o_heads).T * zn_ref[...]).astype(o_ref.dtype)


def _slc_win(q_t, ks, vs_t, kw, vw_t, sel_t, gate_t, o_cmp_t, zn, tq, tk):
    b, _, _, lq = q_t.shape
    assert lq >= WINDOW + tq and lq % tk == 0 and tk % (8 * SLC_LEN) == 0
    gw = G_NSA * DH
    k_spec = pl.BlockSpec((1, 1, lq, DH), lambda i, k, t: (i, k, 0, 0))
    vt_spec = pl.BlockSpec((1, 1, DH, lq), lambda i, k, t: (i, k, 0, 0))
    lanes = G_NSA * tq
    band = WINDOW + tq
    nt = lq // tq
    scored = lambda t: jnp.minimum(t, nt - 1)
    drained = lambda t: jnp.maximum(t - 1, 0)
    return pl.pallas_call(
        functools.partial(_slc_win_kernel, tq=tq, tk=tk, nt=nt),
        grid=(b, KVH, nt + 1),
        in_specs=[pl.BlockSpec((1, G_NSA, DH, tq), lambda i, k, t: (i, k, 0, scored(t))),
                  k_spec, vt_spec, k_spec, vt_spec,
                  pl.BlockSpec((1, 1, LANES, tq), lambda i, k, t: (i, k, 0, scored(t))),
                  pl.BlockSpec((1, gate_t.shape[1], tq), lambda i, k, t: (i, 0, drained(t))),
                  pl.BlockSpec((1, gw, tq), lambda i, k, t: (i, k, drained(t))),
                  pl.BlockSpec((tq, gw), lambda i, k, t: (i * nt + drained(t), k))],
        out_specs=pl.BlockSpec((1, tq, gw), lambda i, k, t: (i, drained(t), k)),
        out_shape=jax.ShapeDtypeStruct((b, lq, NSA_WIDTH), bf16),
        scratch_shapes=[pltpu.VMEM((lq, lanes), f32), pltpu.VMEM((lq, lanes), f32),
                        pltpu.VMEM((8, lanes), f32), pltpu.VMEM((8, lanes), f32),
                        pltpu.VMEM((band, lanes), f32), pltpu.VMEM((band, lanes), f32),
                        pltpu.VMEM((8, lanes), f32), pltpu.VMEM((8, lanes), f32),
                        pltpu.VMEM((DH + ONES_ROWS, lanes), f32),
                        pltpu.VMEM((DH + ONES_ROWS, lanes), f32)],
        compiler_params=_cparams(("parallel", "parallel", "arbitrary")),
        name="slc_win",
    )(q_t, ks, vs_t, kw, vw_t, sel_t, gate_t, o_cmp_t, zn)


def _rms_gain(o, gain):
    return o * lax.rsqrt(jnp.mean(o * o, axis=-1, keepdims=True) + EPS) * gain


def _gla_kernel(q_ref, k_ref, v_ref, la_ref, z_ref, gain_ref, s0_ref, y_ref, st_ref, s_scr, *, tl, chunk):
    t = pl.program_id(1)

    @pl.when(t == 0)
    def _():
        s_scr[...] = s0_ref[0]

    c = chunk
    pairs = s_scr.shape[0]
    row = lax.broadcasted_iota(jnp.int32, (c, c), 0)
    col = lax.broadcasted_iota(jnp.int32, (c, c), 1)
    causal = col <= row
    tril = jnp.where(causal, 1.0, 0.0).astype(bf16)
    lane = lax.broadcasted_iota(jnp.int32, (1, LANES), 1)
    head_mask = [jnp.where(lane < DK, 1.0, 0.0), jnp.where(lane >= DK, 1.0, 0.0)]
    gain = gain_ref[...]

    chunks = []
    for ci, pr in [(ci, pr) for ci in range(tl // c) for pr in range(pairs)]:
        rows = slice(ci * c, (ci + 1) * c)
        grp = slice(pr * LANES, (pr + 1) * LANES)
        q, k, la = q_ref[rows, grp], k_ref[rows, grp], la_ref[rows, grp]
        hi, rest = la.astype(bf16), la - la.astype(bf16).astype(f32)
        mid, lo = rest.astype(bf16), (rest - rest.astype(bf16).astype(f32)).astype(bf16)
        bc = _dot(tril, hi) + _dot(tril, mid) + _dot(tril, lo)
        ref_row = bc[c // 2 - 1:c // 2, :]
        b_last = bc[c - 1:c, :]
        q_a = q * jnp.exp(bc - ref_row)
        k_a = (k * jnp.exp(ref_row - bc)).astype(bf16)
        q_s = q * jnp.exp(bc)
        k_s = k * jnp.exp(b_last - bc)
        o_intra, q_inter, add = [], [], None
        for h in range(2):
            mh = head_mask[h]
            vh = v_ref[rows, (2 * pr + h) * DV:(2 * pr + h + 1) * DV].astype(bf16)
            a = jnp.where(causal, _dot_nt((q_a * mh).astype(bf16), k_a), 0.0)
            o_intra.append(_dot(a.astype(bf16), vh))
            q_inter.append((q_s * mh).astype(bf16))
            u = _dot_tn(vh, (k_s * mh).astype(bf16))
            add = u if add is None else add + u
        chunks.append((rows, pr, jnp.exp(b_last), add, o_intra, q_inter))

    st = [s_scr[pr] for pr in range(pairs)]
    for rows, pr, decay, add, o_intra, q_inter in chunks:
        st_b = st[pr].astype(bf16)
        for h in range(2):
            cols = slice((2 * pr + h) * DV, (2 * pr + h + 1) * DV)
            o = o_intra[h] + _dot_nt(q_inter[h], st_b)
            y_ref[rows, cols] = (_rms_gain(o, gain) * z_ref[rows, cols]).astype(y_ref.dtype)
        st[pr] = st[pr] * decay + add
    for pr in range(pairs):
        s_scr[pr] = st[pr]
        st_ref[0, pr] = st[pr]


def _gla(gq, gk, gv, la, zg, gain, state0, b, tl, chunk):
    n = gq.shape[0]
    nt = n // b // tl
    pairs = H_GLA // 2
    qk = pl.BlockSpec((tl, H_GLA * DK), lambda i, t: (i * nt + t, 0))
    vz = pl.BlockSpec((tl, GLA_WIDTH), lambda i, t: (i * nt + t, 0))
    st_spec = pl.BlockSpec((1, pairs, DV, LANES), lambda i, t: (i, 0, 0, 0))
    gain2 = gain.reshape(1, DV)
    s0 = state0.reshape(b, pairs, 2, DK, DV).transpose(0, 1, 4, 2, 3).reshape(b, pairs, DV, LANES)
    y, st = pl.pallas_call(
        functools.partial(_gla_kernel, tl=tl, chunk=chunk),
        grid=(b, nt),
        in_specs=[qk, qk, vz, qk, vz, pl.BlockSpec((1, DV), lambda i, t: (0, 0)), st_spec],
        out_specs=(vz, st_spec),
        out_shape=(jax.ShapeDtypeStruct((n, GLA_WIDTH), bf16),
                   jax.ShapeDtypeStruct((b, pairs, DV, LANES), f32)),
        scratch_shapes=[pltpu.VMEM((pairs, DV, LANES), f32)],
        compiler_params=_cparams(("parallel", "arbitrary")),
        name="gla",
    )(gq, gk, gv, la, zg, gain2, s0)
    state = st.reshape(b, pairs, DV, 2, DK).transpose(0, 1, 3, 4, 2).reshape(b, H_GLA, DK, DV)
    return y, state


def _outproj_kernel(x_ref, yn_ref, yg_ref, w_ref, g_ref, y_ref):
    mix = _dot(yn_ref[...].astype(bf16), w_ref[:NSA_WIDTH, :]) + _dot(yg_ref[...].astype(bf16), w_ref[NSA_WIDTH:, :])
    y_ref[...] = _rms_gain(x_ref[...] + mix, g_ref[...])


def _outproj(x, y_nsa, y_gla, w_out, gain, tm):
    n = x.shape[0]
    tok = lambda w: pl.BlockSpec((tm, w), lambda i: (i, 0))
    g2 = gain.reshape(1, D_MODEL)
    return pl.pallas_call(
        _outproj_kernel,
        grid=(n // tm,),
        in_specs=[tok(D_MODEL), tok(NSA_WIDTH), tok(GLA_WIDTH),
                  pl.BlockSpec(w_out.shape, lambda i: (0, 0)), pl.BlockSpec(g2.shape, lambda i: (0, 0))],
        out_specs=tok(D_MODEL),
        out_shape=jax.ShapeDtypeStruct((n, D_MODEL), f32),
        compiler_params=_cparams(("parallel",)),
        name="outproj",
    )(x, y_nsa, y_gla, w_out, g2)


def _prompt_path(x, wts, w_t, w_tok):
    (norm_g, _, wa2_pad, ba, w1_pairs, cmp_bias, w2_heads, gla_gain, w_out, out_gain) = wts
    b, lq, _ = x.shape
    (q_t, kvc_t, kvs_t, kvw_t, ks, kw, vs_t, vw_t, gate_t, zn, gq, gk, gv, la, zg) = _inproj_prompt(
        x, norm_g, w_t, w_tok, wa2_pad, ba, tm=512)
    o_cmp_t, sel_t = _cmp_attn(kvc_t, w1_pairs, cmp_bias, w2_heads, q_t, gate_t, tq=512)
    y_nsa = _slc_win(q_t, ks, vs_t, kw, vw_t, sel_t, gate_t, o_cmp_t, zn, tq=256, tk=512)
    y_gla, state = _gla(gq, gk, gv, la, zg, gla_gain, jnp.zeros((b, H_GLA, DK, DV), f32), b, tl=512,
                        chunk=GLA_CHUNK)
    y = _outproj(x.reshape(b * lq, D_MODEL), y_nsa.reshape(b * lq, NSA_WIDTH), y_gla, w_out, out_gain, tm=1024)
    kv5 = lambda a: a.reshape(b, 2, KVH, DH, -1).transpose(0, 4, 1, 2, 3)
    wlen = min(WINDOW, lq)
    return (y.reshape(b, lq, D_MODEL), kv5(kvc_t), kv5(kvs_t), kv5(kvw_t[:, :, lq - wlen:]), state)


def _pages_native(cache):
    n_pool = cache.shape[0]
    return cache.transpose(0, 2, 3, 4, 1).reshape(n_pool, 2, KVH * DH, cache.shape[1])


def _page_gather(make_copies):
    i = pl.program_id(0)
    slot = i % 2

    @pl.when(i == 0)
    def _():
        for c in make_copies(0, 0):
            c.start()

    @pl.when(i + 1 < pl.num_programs(0))
    def _():
        for c in make_copies(i + 1, 1 - slot):
            c.start()

    for c in make_copies(i, slot):
        c.wait()
    return slot


def _group_sum_matrix(n_tok):
    r = np.arange(KVH * n_tok)[:, None]
    c = np.arange(H_NSA * n_tok)[None, :]
    m = ((c // (G_NSA * n_tok)) == (r // n_tok)) & ((c % n_tok) == (r % n_tok))
    return m.astype(np.float32)


def _cmp_attn_sample_kernel(pt_ref, cache_ref, perm_ref, w1_ref, bias_ref, w2_ref, cos_ref, sa_ref, sb_ref, q_ref,
                            gate_ref, cov_ref, gs_ref, gst_ref, o_ref, sel_ref, buf, rows_scr, sem,
                            *, n_pages, n_tok, past_len):
    def copies(seq, slot):
        return [pltpu.make_async_copy(cache_ref.at[pt_ref[seq, p]], buf.at[slot, p], sem.at[slot])
                for p in range(n_pages)]

    slot = _page_gather(copies)

    page_at = lambda p: buf[slot, p].reshape(2 * KVH * DH, PAGE_SIZE)
    k, v = _compress_pages(page_at, n_pages, perm_ref[...], rows_scr, w1_ref, bias_ref, w2_ref)
    k = _rope128(k, cos_ref[...], sa_ref[...], sb_ref[...])
    nc = n_pages * CHUNKS_PER_PAGE
    rows = H_NSA * n_tok
    qpos = past_len + lax.broadcasted_iota(jnp.int32, (rows, 1), 0) % n_tok
    end_c = lax.broadcasted_iota(jnp.int32, (1, nc), 1) * CMP_STRIDE + (CMP_LEN - 1)
    mask = end_c <= qpos
    p = _masked_softmax(_dot_nt(q_ref[0], k.astype(bf16)), mask, axis=-1)
    o_ref[0] = _dot(p.astype(bf16), v.astype(bf16)) * gate_ref[0][:, 0:1]
    hi, lo = _split_bf16(p)
    gs = gs_ref[...]
    ph, pl_ = _split_bf16(_dot(gs, hi) + _dot(gs, lo))
    imp = _dot(ph, cov_ref[...]) + _dot(pl_, cov_ref[...])
    n_blk = imp.shape[1]
    jblk = lax.broadcasted_iota(jnp.int32, imp.shape, 1)
    forced = (jblk == 0) | (jblk == n_blk - 1)
    sel = _topk_rows_by_rank(jnp.where(forced, FORCE_SCORE, imp), SLC_TOP - 1)
    sel_ref[0] = _dot(gst_ref[...], sel.astype(bf16)).astype(bf16)


def _cmp_attn_sample(cache_t, page_table, w1_pairs, bias, w2_heads, q_blk, gate_rows, n_tok):
    b, n_pages = page_table.shape
    past_len = n_pages * PAGE_SIZE
    nc = past_len // CMP_STRIDE
    rows = H_NSA * n_tok
    n_blk = past_len // SLC_LEN
    end_pos = jnp.arange(nc) * CMP_STRIDE + (CMP_LEN - 1)
    cos, sa, sb = _rope_tables(end_pos)
    cov = _cover_t(n_blk, nc).T
    gs = jnp.asarray(_group_sum_matrix(n_tok), dtype=bf16)
    gst = gs.T
    perm = _row_gather_matrix()
    full = lambda a: pl.BlockSpec(a.shape, lambda i, pt: (0,) * a.ndim)
    seq = lambda a: pl.BlockSpec((1,) + a.shape[1:], lambda i, pt: (i,) + (0,) * (a.ndim - 1))
    grid_spec = pltpu.PrefetchScalarGridSpec(
        num_scalar_prefetch=1,
        grid=(b,),
        in_specs=[pl.BlockSpec(memory_space=pl.ANY), full(perm), full(w1_pairs), full(bias), full(w2_heads),
                  full(cos), full(sa), full(sb), seq(q_blk), seq(gate_rows), full(cov), full(gs), full(gst)],
        out_specs=(pl.BlockSpec((1, rows, LANES), lambda i, pt: (i, 0, 0)),
                   pl.BlockSpec((1, rows, n_blk), lambda i, pt: (i, 0, 0))),
        scratch_shapes=[pltpu.VMEM((2, n_pages) + cache_t.shape[1:], f32),
                        pltpu.VMEM((2, CMP_STRIDE, nc, KVH * DH), f32),
                        pltpu.SemaphoreType.DMA((2,))],
    )
    return pl.pallas_call(
        functools.partial(_cmp_attn_sample_kernel, n_pages=n_pages, n_tok=n_tok, past_len=past_len),
        grid_spec=grid_spec,
        out_shape=(jax.ShapeDtypeStruct((b, rows, LANES), f32),
                   jax.ShapeDtypeStruct((b, rows, n_blk), bf16)),
        compiler_params=_cparams(("arbitrary",)),
        name="cmp_attn_sample",
    )(page_table, cache_t, perm, w1_pairs, bias, w2_heads, cos, sa, sb, q_blk, gate_rows, cov, gs, gst)


def _slc_win_sample_kernel(pt_ref, cache_ref, q_ref, sel_ref, e_ref, snew_ref, cw_ref, wnew_ref, wnewt_ref,
                           gate_ref, ocmp_ref, zn_ref, o_ref, wout_ref, buf, sem, *, n_pages, n_tok, win_off):
    def copies(seq, slot):
        return [pltpu.make_async_copy(cache_ref.at[pt_ref[seq, p], c],
                                      buf.at[slot, c, :, pl.ds(p * PAGE_SIZE, PAGE_SIZE)], sem.at[slot])
                for p in range(n_pages) for c in range(2)]

    slot = _page_gather(copies)
    rows = H_NSA * n_tok
    q = q_ref[0]
    tok = lax.broadcasted_iota(jnp.int32, (rows, 1), 0) % n_tok
    new_i = lax.broadcasted_iota(jnp.int32, (1, snew_ref.shape[1]), 1)
    new_ok = (new_i <= tok) & (new_i < n_tok)

    def attend(keys_t, vals_t, allowed, new_ref):
        s = jnp.where(allowed, _dot(q, keys_t), NEG)
        k_new = new_ref[0][:, :LANES].astype(bf16)
        v_new = new_ref[0][:, LANES:].astype(bf16)
        s_new = jnp.where(new_ok, _dot_nt(q, k_new), NEG)
        m = jnp.maximum(jnp.max(s, axis=-1, keepdims=True), jnp.max(s_new, axis=-1, keepdims=True))
        p = jnp.where(allowed, jnp.exp(s - m), 0.0)
        p_new = jnp.where(new_ok, jnp.exp(s_new - m), 0.0)
        l = jnp.sum(p, axis=-1, keepdims=True) + jnp.sum(p_new, axis=-1, keepdims=True)
        return (_dot_nt(p.astype(bf16), vals_t) + _dot(p_new.astype(bf16), v_new)) / l

    picked = _dot(sel_ref[0], e_ref[...]) > 0.5
    o_slc = attend(buf[slot, 0].astype(bf16), buf[slot, 1].astype(bf16), picked, snew_ref)

    wbuf = cw_ref.shape[-1]
    win_i = lax.broadcasted_iota(jnp.int32, (1, wbuf), 1)
    o_win = attend(cw_ref[0, 0].astype(bf16), cw_ref[0, 1].astype(bf16), win_i > tok + win_off, wnew_ref)

    lane = lax.broadcasted_iota(jnp.int32, (1, LANES), 1)
    for c in range(2):
        shifted = pltpu.roll(cw_ref[0, c], wbuf - n_tok, 1)
        tail = jnp.where(lane >= LANES - n_tok, wnewt_ref[0, c], shifted[:, wbuf - LANES:])
        wout_ref[0, c] = jnp.concatenate([shifted[:, :wbuf - LANES], tail], axis=-1)

    gate = gate_ref[0]
    o = ocmp_ref[0] + gate[:, 1:2] * o_slc + gate[:, 2:3] * o_win
    second_kvh = lax.broadcasted_iota(jnp.int32, (rows, 1), 0) >= G_NSA * n_tok
    o_ref[0] = jnp.where(second_kvh, o[:, DH:], o[:, :DH]) * zn_ref[0]


def _slc_win_sample(cache_t, page_table, q_blk, sel_rows, kvs_new, win_t, kvw_new, kvw_new_t, gate_rows,
                    o_cmp, zn_rows, n_tok):
    b, n_pages = page_table.shape
    rows = H_NSA * n_tok
    past_len = n_pages * PAGE_SIZE
    wbuf = win_t.shape[-1]
    e = _block_expand(sel_rows.shape[-1], past_len)
    full = lambda a: pl.BlockSpec(a.shape, lambda i, pt: (0,) * a.ndim)
    seq = lambda a: pl.BlockSpec((1,) + a.shape[1:], lambda i, pt: (i,) + (0,) * (a.ndim - 1))
    grid_spec = pltpu.PrefetchScalarGridSpec(
        num_scalar_prefetch=1,
        grid=(b,),
        in_specs=[pl.BlockSpec(memory_space=pl.ANY), seq(q_blk), seq(sel_rows), full(e), seq(kvs_new),
                  seq(win_t), seq(kvw_new), seq(kvw_new_t), seq(gate_rows), seq(o_cmp), seq(zn_rows)],
        out_specs=(pl.BlockSpec((1, rows, DH), lambda i, pt: (i, 0, 0)), seq(win_t)),
        scratch_shapes=[pltpu.VMEM((2, 2, KVH * DH, past_len), f32), pltpu.SemaphoreType.DMA((2,))],
    )
    return pl.pallas_call(
        functools.partial(_slc_win_sample_kernel, n_pages=n_pages, n_tok=n_tok, win_off=wbuf - WINDOW),
        grid_spec=grid_spec,
        out_shape=(jax.ShapeDtypeStruct((b, rows, DH), f32), jax.ShapeDtypeStruct(win_t.shape, f32)),
        compiler_params=_cparams(("arbitrary",)),
        name="slc_win_sample",
    )(page_table, cache_t, q_blk, sel_rows, e, kvs_new, win_t, kvw_new, kvw_new_t, gate_rows, o_cmp, zn_rows)


def _gla_sample(gq, gk, la, gv, zg, state, gain, b, n_tok):
    rows = 8
    pad = lambda a: jnp.pad(a.reshape(b, n_tok, -1), ((0, 0), (0, rows - n_tok), (0, 0))).reshape(b * rows, -1)
    y, s_new = _gla(pad(gq), pad(gk), pad(gv), pad(la), pad(zg), gain, state, b, tl=rows, chunk=rows)
    return y.reshape(b, rows, GLA_WIDTH)[:, :n_tok].reshape(b * n_tok, GLA_WIDTH), s_new


def _sample_path(x, cache_c, cache_s, cache_w, state, page_table, wts):
    (norm_g, w_pack, wa2_pad, ba, w1_pairs, cmp_bias, w2_heads, gla_gain, w_out, out_gain) = wts
    b, n_tok, _ = x.shape
    n = b * n_tok
    n_pages = page_table.shape[1]
    past_len = n_pages * PAGE_SIZE
    assert n_tok <= 8 and past_len % SLC_LEN == 0 and past_len // SLC_LEN <= LANES
    pos = past_len + jnp.arange(n) % n_tok
    (q_hm, kvc, kvs, kvw, _, _, _, _, gate, zn, gq, gk, gv, la, zg) = _inproj(
        x.reshape(1, n, D_MODEL), pos, norm_g, w_pack, wa2_pad, ba, tm=n)

    rows = H_NSA * n_tok
    q_rows = q_hm[0].reshape(H_NSA, b, n_tok, DH).transpose(1, 0, 2, 3)
    zero = jnp.zeros_like(q_rows[:, :G_NSA])
    q_blk = jnp.concatenate([jnp.concatenate([q_rows[:, :G_NSA], zero], axis=-1),
                             jnp.concatenate([zero, q_rows[:, G_NSA:]], axis=-1)], axis=1).reshape(b, rows, LANES)
    gate_rows = gate[:, :3 * H_NSA].reshape(b, n_tok, 3, H_NSA).transpose(0, 3, 1, 2).reshape(b, rows, 3)
    gate_rows = jnp.pad(gate_rows, ((0, 0), (0, 0), (0, LANES - 3)))
    pad_new = lambda a: jnp.pad(a.reshape(b, n_tok, KV_ROW), ((0, 0), (0, 8 - n_tok), (0, 0)))

    kvw_new_t = kvw.reshape(b, n_tok, 2, KVH * DH).transpose(0, 2, 3, 1)
    kvw_new_t = jnp.pad(kvw_new_t, ((0, 0), (0, 0), (0, 0), (LANES - n_tok, 0)))

    o_cmp, sel_rows = _cmp_attn_sample(_pages_native(cache_c), page_table, w1_pairs, cmp_bias, w2_heads,
                                       q_blk, gate_rows, n_tok)
    o_rows, win_t = _slc_win_sample(_pages_native(cache_s), page_table, q_blk, sel_rows, pad_new(kvs),
                                    _pages_native(cache_w), pad_new(kvw), kvw_new_t, gate_rows, o_cmp,
                                    zn.reshape(b, n_tok, H_NSA, DH).transpose(0, 2, 1, 3).reshape(b, rows, DH),
                                    n_tok)
    y_nsa = o_rows.reshape(b, H_NSA, n_tok, DH).transpose(0, 2, 1, 3).reshape(n, NSA_WIDTH)

    y_gla, s_new = _gla_sample(gq, gk, la, gv, zg, state, gla_gain, b, n_tok)
    y = _outproj(x.reshape(n, D_MODEL), y_nsa, y_gla, w_out, out_gain, tm=n)
    kv5 = lambda a: a.reshape(b, n_tok, 2, KVH, DH)
    win_new = win_t.reshape(b, 2, KVH, DH, -1).transpose(0, 4, 1, 2, 3)
    return (y.reshape(b, n_tok, D_MODEL), kv5(kvc), kv5(kvs), win_new, s_new)


def kernel(x_prompt, x_sample, cache_cmp_kv, cache_slc_kv, cache_win_kv, state_gla, page_table,
           norm_in_gain, w_in, cmp_pe, cmp_w1, cmp_b1, cmp_w2, gla_wa2, gla_ba, gla_norm_gain,
           w_out, norm_out_gain):
    assert w_in.shape[0] == 1, "single-layer step"
    wts = (norm_in_gain[0], _pack_w_in(w_in[0]), _pad_wa2(gla_wa2[0]), gla_ba[0],
           _cmp_w1_pairs(cmp_w1[0]), _cmp_bias(cmp_pe[0], cmp_w1[0], cmp_b1[0]), _cmp_w2_heads(cmp_w2[0]),
           gla_norm_gain[0], w_out[0].astype(bf16), norm_out_gain)
    yp, cmp_p, slc_p, win_p, gla_p = _prompt_path(x_prompt, wts, *_pack_w_in_prompt(w_in[0]))
    ys, cmp_s, slc_s, win_s, gla_s = _sample_path(
        x_sample, cache_cmp_kv[0], cache_slc_kv[0], cache_win_kv[0], state_gla[0], page_table, wts)
    return (yp, ys, cmp_p[None], cmp_s[None], slc_p[None], slc_s[None], win_p[None], win_s[None],
            gla_p[None], gla_s[None])
```

```python
import functools

import numpy as np
import jax
import jax.numpy as jnp
from jax import lax
from jax.experimental import pallas as pl
from jax.experimental.pallas import tpu as pltpu

f32 = jnp.float32
bf16 = jnp.bfloat16

D_MODEL = 1024
DH = 64
H_NSA = 8
KVH = 2
G_NSA = 4
NSA_WIDTH = H_NSA * DH
CMP_LEN = 32
CMP_STRIDE = 16
CMP_HID = 128
SLC_LEN = 64
SLC_TOP = 16
WINDOW = 512
ROT_HALF = 8
ROPE_THETA = 500000.0
FORCE_SCORE = 1.0e4
H_GLA = 4
DK = 64
DV = 128
GLA_WIDTH = H_GLA * DV
GLA_LR = 16
GLA_TAU = 16.0
GLA_CHUNK = 64
GLA_SUB = 16
EPS = 1e-6
NEG = -1.0e30
LOG2E = 1.4426950408889634
PAGE_SIZE = 128
KV_ROW = 2 * KVH * DH

IN_SIZES = (H_NSA * DH, KV_ROW, KV_ROW, KV_ROW, 3 * H_NSA, NSA_WIDTH,
            H_GLA * DK, H_GLA * DK, H_GLA * DV, GLA_LR, GLA_WIDTH)
IN_OFFSETS = [0] + [int(v) for v in np.cumsum(IN_SIZES)]

LANES = 128
VMEM_LIMIT = 56 * 1024 * 1024

P_Q, P_KVC, P_KVS, P_KVW, P_ZN, P_GQ, P_GK, P_GV, P_ZG, P_MISC = (
    0, 512, 768, 1024, 1280, 1792, 2048, 2304, 2816, 3328)
D_PACK = P_MISC + LANES
MISC_GATE = 0
MISC_GLR = 32


def _cparams(sem):
    return pltpu.CompilerParams(dimension_semantics=sem, vmem_limit_bytes=VMEM_LIMIT)


def _sigmoid(x):
    return 1.0 / (1.0 + jnp.exp(-x))


def _silu(x):
    return x * _sigmoid(x)


def _log_sigmoid(x):
    return jnp.minimum(x, 0.0) - jnp.log1p(jnp.exp(-jnp.abs(x)))


def _dot(a, b):
    return jnp.dot(a, b, preferred_element_type=f32)


def _dot_nt(a, b):
    return lax.dot_general(a, b, (((1,), (1,)), ((), ())), preferred_element_type=f32)


def _dot_tn(a, b):
    return lax.dot_general(a, b, (((0,), (0,)), ((), ())), preferred_element_type=f32)


def _rope_tables(pos):
    n = pos.shape[0]
    inv = ROPE_THETA ** (-(jnp.arange(ROT_HALF, dtype=f32) / ROT_HALF))
    ang = pos.astype(f32)[:, None] * inv[None, :]
    cos, sin = jnp.cos(ang), jnp.sin(ang)
    z8 = jnp.zeros((n, ROT_HALF), f32)
    rest = DH - 2 * ROT_HALF
    c64 = jnp.concatenate([cos, cos, jnp.ones((n, rest), f32)], axis=-1)
    sa64 = jnp.concatenate([-sin, z8, jnp.zeros((n, rest), f32)], axis=-1)
    sb64 = jnp.concatenate([z8, sin, jnp.zeros((n, rest), f32)], axis=-1)
    tile = lambda t: jnp.concatenate([t, t], axis=-1)
    return tile(c64), tile(sa64), tile(sb64)


def _rope128(v, cos, sa, sb):
    return v * cos + pltpu.roll(v, LANES - ROT_HALF, 1) * sa + pltpu.roll(v, ROT_HALF, 1) * sb


def _inproj_kernel(x_ref, g_ref, w_ref, cos_ref, sa_ref, sb_ref, wa2_ref, ba_ref,
                   q_ref, kvc_ref, kvs_ref, kvw_ref, ks_ref, vs_ref, kw_ref, vw_ref,
                   gate_ref, zn_ref, gq_ref, gk_ref, gv_ref, la_ref, zg_ref):
    x = x_ref[...]
    ms = jnp.mean(x * x, axis=-1, keepdims=True)
    hn = (x * lax.rsqrt(ms + EPS) * g_ref[...]).astype(bf16)
    cos, sa, sb = cos_ref[...], sa_ref[...], sb_ref[...]

    def proj(off, width):
        return _dot(hn, w_ref[:, off:off + width])

    qp = proj(P_Q, NSA_WIDTH)
    for c in range(NSA_WIDTH // LANES):
        r = _rope128(qp[:, c * LANES:(c + 1) * LANES], cos, sa, sb) * (DH ** -0.5)
        q_ref[0, 2 * c] = r[:, :DH].astype(bf16)
        q_ref[0, 2 * c + 1] = r[:, DH:].astype(bf16)

    kvc_ref[...] = proj(P_KVC, KV_ROW)

    for off, kv_ref, k_ref, v_ref in ((P_KVS, kvs_ref, ks_ref, vs_ref), (P_KVW, kvw_ref, kw_ref, vw_ref)):
        p = proj(off, KV_ROW)
        k = _rope128(p[:, :LANES], cos, sa, sb)
        v = p[:, LANES:]
        kv_ref[:, :LANES] = k
        kv_ref[:, LANES:] = v
        for h in range(KVH):
            k_ref[0, h] = k[:, h * DH:(h + 1) * DH].astype(bf16)
            v_ref[0, h] = v[:, h * DH:(h + 1) * DH].astype(bf16)

    zn_ref[...] = _silu(proj(P_ZN, NSA_WIDTH))
    gq_ref[...] = proj(P_GQ, H_GLA * DK) * (DK ** -0.5)
    gk_ref[...] = proj(P_GK, H_GLA * DK)
    gv_ref[...] = proj(P_GV, H_GLA * DV)
    zg_ref[...] = _silu(proj(P_ZG, GLA_WIDTH))

    misc = proj(P_MISC, LANES)
    gate_ref[...] = _sigmoid(misc)
    xa = _dot(misc.astype(bf16), wa2_ref[...]) + ba_ref[...]
    la_ref[...] = _log_sigmoid(xa) / GLA_TAU


def _pack_w_in(w_in):
    o = IN_OFFSETS
    seg = lambda i: w_in[:, o[i]:o[i + 1]]
    misc = jnp.zeros((D_MODEL, LANES), w_in.dtype)
    misc = misc.at[:, MISC_GATE:MISC_GATE + 3 * H_NSA].set(seg(4))
    misc = misc.at[:, MISC_GLR:MISC_GLR + GLA_LR].set(seg(9))
    cols = [seg(0), seg(1), seg(2), seg(3), seg(5), seg(6), seg(7), seg(8), seg(10), misc]
    return jnp.concatenate(cols, axis=1).astype(bf16)


def _pad_wa2(wa2):
    pad = jnp.zeros((LANES, H_GLA * DK), wa2.dtype).at[MISC_GLR:MISC_GLR + GLA_LR].set(wa2)
    return pad.astype(bf16)


def _inproj(x, pos, norm_g, w_pack, wa2_pad, ba, tm):
    bk, lk, _ = x.shape
    n = bk * lk
    nt = lk // tm
    cos, sa, sb = _rope_tables(pos)
    tok = lambda w: pl.BlockSpec((tm, w), lambda i: (i, 0))
    tab = pl.BlockSpec((tm, LANES), lambda i: (i % nt, 0))
    full = lambda a: pl.BlockSpec(a.shape, lambda i: (0,) * a.ndim)
    hm = lambda h: pl.BlockSpec((1, h, tm, DH), lambda i: (i // nt, 0, i % nt, 0))
    g2 = norm_g.reshape(1, D_MODEL)
    ba2 = ba.reshape(1, H_GLA * DK)
    out_shape = (
        jax.ShapeDtypeStruct((bk, H_NSA, lk, DH), bf16),
        jax.ShapeDtypeStruct((n, KV_ROW), f32),
        jax.ShapeDtypeStruct((n, KV_ROW), f32),
        jax.ShapeDtypeStruct((n, KV_ROW), f32),
        jax.ShapeDtypeStruct((bk, KVH, lk, DH), bf16),
        jax.ShapeDtypeStruct((bk, KVH, lk, DH), bf16),
        jax.ShapeDtypeStruct((bk, KVH, lk, DH), bf16),
        jax.ShapeDtypeStruct((bk, KVH, lk, DH), bf16),
        jax.ShapeDtypeStruct((n, LANES), f32),
        jax.ShapeDtypeStruct((n, NSA_WIDTH), f32),
        jax.ShapeDtypeStruct((n, H_GLA * DK), f32),
        jax.ShapeDtypeStruct((n, H_GLA * DK), f32),
        jax.ShapeDtypeStruct((n, H_GLA * DV), f32),
        jax.ShapeDtypeStruct((n, H_GLA * DK), f32),
        jax.ShapeDtypeStruct((n, GLA_WIDTH), f32),
    )
    out_specs = (hm(H_NSA), tok(KV_ROW), tok(KV_ROW), tok(KV_ROW), hm(KVH), hm(KVH), hm(KVH), hm(KVH),
                 tok(LANES), tok(NSA_WIDTH), tok(H_GLA * DK), tok(H_GLA * DK), tok(H_GLA * DV),
                 tok(H_GLA * DK), tok(GLA_WIDTH))
    return pl.pallas_call(
        _inproj_kernel,
        grid=(n // tm,),
        in_specs=[tok(D_MODEL), full(g2), full(w_pack), tab, tab, tab, full(wa2_pad), full(ba2)],
        out_specs=out_specs,
        out_shape=out_shape,
        compiler_params=_cparams(("parallel",)),
        name="inproj",
    )(x.reshape(n, D_MODEL), g2, w_pack, cos, sa, sb, wa2_pad, ba2)


PT_Q, PT_KVC, PT_KVS, PT_KVW, PT_GATE = 0, 512, 768, 1024, 1280
PT_ROWS = PT_GATE + 32
PK_ZN, PK_GQ, PK_GK, PK_GV, PK_ZG, PK_MISC = 0, 512, 768, 1024, 1536, 2048
PK_COLS = PK_MISC + LANES


def _inproj_prompt_kernel(x_ref, g_ref, wt_ref, w_ref, cos_ref, sin_ref, wa2_ref, ba_ref,
                          qt_ref, kvct_ref, kvst_ref, kvwt_ref, ks_ref, kw_ref, vst_ref, vwt_ref,
                          gatet_ref, zn_ref, gq_ref, gk_ref, gv_ref, la_ref, zg_ref):
    x = x_ref[...]
    ms = jnp.mean(x * x, axis=-1, keepdims=True)
    hn = (x * lax.rsqrt(ms + EPS) * g_ref[...]).astype(bf16)
    cos_t, sin_t = cos_ref[...], sin_ref[...]

    all_t = _dot_nt(wt_ref[...], hn)

    def proj_t(off, rows):
        return all_t[off:off + rows]

    def rope_rows(v):
        x1, x2 = v[:ROT_HALF], v[ROT_HALF:2 * ROT_HALF]
        return jnp.concatenate([x1 * cos_t - x2 * sin_t, x2 * cos_t + x1 * sin_t, v[2 * ROT_HALF:]], axis=0)

    q_t = proj_t(PT_Q, NSA_WIDTH)
    for h in range(H_NSA):
        qt_ref[0, h] = (rope_rows(q_t[h * DH:(h + 1) * DH]) * (DH ** -0.5 * LOG2E)).astype(bf16)

    kvct_ref[0] = proj_t(PT_KVC, KV_ROW)

    for off, kvt_ref, k_ref, vt_ref in ((PT_KVS, kvst_ref, ks_ref, vst_ref), (PT_KVW, kvwt_ref, kw_ref, vwt_ref)):
        t = proj_t(off, KV_ROW)
        k_t = jnp.concatenate([rope_rows(t[h * DH:(h + 1) * DH]) for h in range(KVH)], axis=0)
        kvt_ref[0, :LANES] = k_t
        kvt_ref[0, LANES:] = t[LANES:]
        k_tok = k_t.T
        for h in range(KVH):
            k_ref[0, h] = k_tok[:, h * DH:(h + 1) * DH].astype(bf16)
            vt_ref[0, h] = t[LANES + h * DH:LANES + (h + 1) * DH].astype(bf16)

    gatet_ref[0] = _sigmoid(proj_t(PT_GATE, PT_ROWS - PT_GATE))

    def proj(off, width):
        return _dot(hn, w_ref[:, off:off + width])

    zn_ref[...] = _silu(proj(PK_ZN, NSA_WIDTH))
    gq_ref[...] = proj(PK_GQ, H_GLA * DK) * (DK ** -0.5)
    gk_ref[...] = proj(PK_GK, H_GLA * DK)
    gv_ref[...] = proj(PK_GV, H_GLA * DV)
    zg_ref[...] = _silu(proj(PK_ZG, GLA_WIDTH))
    misc = proj(PK_MISC, LANES)
    xa = _dot(misc.astype(bf16), wa2_ref[...]) + ba_ref[...]
    la_ref[...] = _log_sigmoid(xa) / GLA_TAU


def _pack_w_in_prompt(w_in):
    o = IN_OFFSETS
    seg = lambda i: w_in[:, o[i]:o[i + 1]]
    gate_t = jnp.zeros((PT_ROWS - PT_GATE, D_MODEL), w_in.dtype).at[:3 * H_NSA].set(seg(4).T)
    w_t = jnp.concatenate([seg(0).T, seg(1).T, seg(2).T, seg(3).T, gate_t], axis=0).astype(bf16)
    misc = jnp.zeros((D_MODEL, LANES), w_in.dtype).at[:, MISC_GLR:MISC_GLR + GLA_LR].set(seg(9))
    w_tok = jnp.concatenate([seg(5), seg(6), seg(7), seg(8), seg(10), misc], axis=1).astype(bf16)
    return w_t, w_tok


def _inproj_prompt(x, norm_g, w_t, w_tok, wa2_pad, ba, tm):
    b, lq, _ = x.shape
    n = b * lq
    nt = lq // tm
    inv = ROPE_THETA ** (-(jnp.arange(ROT_HALF, dtype=f32) / ROT_HALF))
    ang = inv[:, None] * jnp.arange(lq).astype(f32)[None, :]
    cos_t, sin_t = jnp.cos(ang), jnp.sin(ang)
    tok = lambda w: pl.BlockSpec((tm, w), lambda i: (i, 0))
    tab = pl.BlockSpec((ROT_HALF, tm), lambda i: (0, i % nt))
    full = lambda a: pl.BlockSpec(a.shape, lambda i: (0,) * a.ndim)
    feat = lambda r: pl.BlockSpec((1, r, tm), lambda i: (i // nt, 0, i % nt))
    headf = lambda h: pl.BlockSpec((1, h, DH, tm), lambda i: (i // nt, 0, 0, i % nt))
    headt = lambda h: pl.BlockSpec((1, h, tm, DH), lambda i: (i // nt, 0, i % nt, 0))
    g2 = norm_g.reshape(1, D_MODEL)
    ba2 = ba.reshape(1, H_GLA * DK)
    sds = jax.ShapeDtypeStruct
    out_shape = (
        sds((b, H_NSA, DH, lq), bf16),
        sds((b, KV_ROW, lq), f32), sds((b, KV_ROW, lq), f32), sds((b, KV_ROW, lq), f32),
        sds((b, KVH, lq, DH), bf16), sds((b, KVH, lq, DH), bf16),
        sds((b, KVH, DH, lq), bf16), sds((b, KVH, DH, lq), bf16),
        sds((b, PT_ROWS - PT_GATE, lq), f32),
        sds((n, NSA_WIDTH), f32), sds((n, H_GLA * DK), f32), sds((n, H_GLA * DK), f32),
        sds((n, H_GLA * DV), f32), sds((n, H_GLA * DK), f32), sds((n, GLA_WIDTH), f32),
    )
    out_specs = (headf(H_NSA), feat(KV_ROW), feat(KV_ROW), feat(KV_ROW), headt(KVH), headt(KVH),
                 headf(KVH), headf(KVH), feat(PT_ROWS - PT_GATE),
                 tok(NSA_WIDTH), tok(H_GLA * DK), tok(H_GLA * DK), tok(H_GLA * DV), tok(H_GLA * DK),
                 tok(GLA_WIDTH))
    return pl.pallas_call(
        _inproj_prompt_kernel,
        grid=(n // tm,),
        in_specs=[tok(D_MODEL), full(g2), full(w_t), full(w_tok), tab, tab, full(wa2_pad), full(ba2)],
        out_specs=out_specs,
        out_shape=out_shape,
        compiler_params=_cparams(("parallel",)),
        name="inproj_prompt",
    )(x.reshape(n, D_MODEL), g2, w_t, w_tok, cos_t, sin_t, wa2_pad, ba2)


HID_W = 2 * KVH * CMP_HID
CHUNKS_PER_PAGE = PAGE_SIZE // CMP_STRIDE
PAGE_GROUP = 8


def _cmp_w1_pairs(cmp_w1):
    ratio = CMP_LEN // CMP_STRIDE
    w1r = cmp_w1.reshape(2, ratio, CMP_STRIDE // 2, 2, DH, CMP_HID)
    eye = jnp.eye(KVH, dtype=cmp_w1.dtype)
    big = jnp.einsum('crpjdh,kK->cpjKdrkh', w1r, eye)
    return big.reshape(2, CMP_STRIDE // 2, 2 * KVH * DH, ratio * KVH * CMP_HID).astype(bf16)


def _cmp_w2_heads(cmp_w2):
    eye = jnp.eye(KVH, dtype=cmp_w2.dtype)
    return jnp.einsum('chd,kK->ckhKd', cmp_w2, eye).reshape(2, KVH * CMP_HID, KVH * DH).astype(bf16)


def _row_gather_matrix():
    r = np.arange(PAGE_SIZE)
    src = (r % CHUNKS_PER_PAGE) * CMP_STRIDE + r // CHUNKS_PER_PAGE
    return jnp.asarray((r[None, :] == src[:, None]).astype(np.float32), dtype=bf16)


def _compress_pages(page_at, n_pages, perm, rows_scr, w1_ref, bias_ref, w2_ref):
    cpp = CHUNKS_PER_PAGE
    kd = KVH * DH

    def to_rows(g, carry):
        for u in range(PAGE_GROUP):
            p = g * PAGE_GROUP + u
            r0 = pl.multiple_of(p * cpp, cpp)
            t = _dot_nt(perm, page_at(p).astype(bf16))
            for c in range(2):
                for s in range(CMP_STRIDE):
                    rows_scr[c, s, pl.ds(r0, cpp), :] = t[s * cpp:(s + 1) * cpp, c * kd:(c + 1) * kd]
        return carry

    lax.fori_loop(0, n_pages // PAGE_GROUP, to_rows, 0)

    nc = n_pages * cpp
    kv = []
    for c in range(2):
        lhs = jnp.concatenate([rows_scr[c, s] for s in range(CMP_STRIDE)], axis=-1).astype(bf16)
        part = _dot(lhs, w1_ref[c].reshape(CMP_STRIDE * kd, w1_ref.shape[-1]))
        half = part.shape[1] // 2
        pre = part[:, :half] + pltpu.roll(part[:, half:], nc - 1, 0)
        hid = _silu(pre + bias_ref[:, c * half:(c + 1) * half])
        kv.append(_dot(hid.astype(bf16), w2_ref[c]))
    return kv


def _cmp_bias_kernel(pe_ref, w1_ref, b1_ref, o_ref):
    for c in range(2):
        o_ref[c] = _dot(pe_ref[c], w1_ref[c]) + b1_ref[c]


def _cmp_bias(cmp_pe, cmp_w1, cmp_b1):
    pe = jnp.broadcast_to(cmp_pe.reshape(2, 1, CMP_LEN * DH), (2, 8, CMP_LEN * DH))
    b1 = jnp.broadcast_to(cmp_b1.reshape(2, 1, CMP_HID), (2, 8, CMP_HID))
    out = pl.pallas_call(
        _cmp_bias_kernel,
        out_shape=jax.ShapeDtypeStruct((2, 8, CMP_HID), f32),
        name="cmp_bias",
    )(pe, cmp_w1, b1)
    row = out[:, 0, :]
    return jnp.broadcast_to(row[:, None, :], (2, KVH, CMP_HID)).reshape(1, HID_W)


def _masked_softmax(s, mask, axis, exp=jnp.exp):
    s = jnp.where(mask, s, NEG)
    e = exp(s - jnp.max(s, axis=axis, keepdims=True))
    return jnp.where(mask, e * (1.0 / jnp.sum(e, axis=axis, keepdims=True)), 0.0)


def _split_bf16(x):
    hi = x.astype(bf16)
    return hi, (x - hi.astype(f32)).astype(bf16)


def _topk_mask(score, k, axis):
    n = score.shape[axis]
    idx = lax.broadcasted_iota(jnp.int32, score.shape, axis)
    sel = jnp.zeros(score.shape, f32)
    for _ in range(k):
        m = jnp.max(score, axis=axis, keepdims=True)
        first = jnp.min(jnp.where(score == m, idx, n), axis=axis, keepdims=True)
        pick = idx == first
        sel = jnp.where(pick, 1.0, sel)
        score = jnp.where(pick, NEG, score)
    return sel


def _topk_rows_by_rank(score, k):
    r, n = score.shape
    cols = jnp.concatenate([score, jnp.zeros((n - r, n), f32)], axis=0).T
    earlier = jnp.where(lax.broadcasted_iota(jnp.int32, (n, n), 0) < lax.broadcasted_iota(jnp.int32, (n, n), 1),
                        1.0, 0.0)
    rows = []
    for c in range(r):
        col, row = cols[:, c:c + 1], score[c:c + 1, :]
        before = jnp.where(col > row, 1.0, jnp.where(col == row, earlier, 0.0))
        rows.append(jnp.where(jnp.sum(before, axis=0, keepdims=True) < k, 1.0, 0.0))
    return jnp.concatenate(rows, axis=0)


def _cover_t(n_slc, n_chunk):
    start = np.arange(n_chunk)[None, :] * CMP_STRIDE
    j = np.arange(n_slc)[:, None]
    cov = (start < (j + 1) * SLC_LEN) & (start + CMP_LEN > j * SLC_LEN)
    return jnp.asarray(cov.astype(np.float32), dtype=bf16)


def _cmp_attn_kernel(kvct_ref, perm_ref, w1_ref, bias_ref, w2_ref, cos_ref, sa_ref, sb_ref, qt_ref, gatet_ref,
                     cov_ref, ot_ref, selt_ref, rows_scr, kc_ref, vct_ref, *, tq):
    t = pl.program_id(1)

    @pl.when(t == 0)
    def _():
        page_at = lambda p: kvct_ref[0, :, pl.ds(pl.multiple_of(p * PAGE_SIZE, PAGE_SIZE), PAGE_SIZE)]
        k, v = _compress_pages(page_at, kvct_ref.shape[2] // PAGE_SIZE, perm_ref[...], rows_scr,
                               w1_ref, bias_ref, w2_ref)
        k = _rope128(k, cos_ref[...], sa_ref[...], sb_ref[...])
        v_t = v.T
        for h in range(KVH):
            kc_ref[h] = k[:, h * DH:(h + 1) * DH].astype(bf16)
            vct_ref[h] = v_t[h * DH:(h + 1) * DH].astype(bf16)

    nc = kc_ref.shape[1]
    nb = cov_ref.shape[0]
    qpos = t * tq + lax.broadcasted_iota(jnp.int32, (1, tq), 1)
    end_pos = lax.broadcasted_iota(jnp.int32, (nc, 1), 0) * CMP_STRIDE + (CMP_LEN - 1)
    mask_t = end_pos <= qpos
    gate_t = gatet_ref[0]
    cov = cov_ref[...]

    jblk = lax.broadcasted_iota(jnp.int32, (nb, tq), 0)
    qblk = qpos // SLC_LEN
    valid = jblk <= qblk
    forced = (jblk == 0) | (jblk == qblk) | (jblk == qblk - 1)

    scores = []
    for kh in range(KVH):
        kc, vc_t = kc_ref[kh], vct_ref[kh]
        psum = jnp.zeros((nc, tq), f32)
        for g in range(G_NSA):
            h = kh * G_NSA + g
            p_t = _masked_softmax(_dot(kc, qt_ref[0, h]), mask_t, axis=0, exp=jnp.exp2)
            ot_ref[0, h * DH:(h + 1) * DH] = _dot(vc_t, p_t.astype(bf16)) * gate_t[h:h + 1]
            psum = psum + p_t
        hi, lo = _split_bf16(psum)
        imp = _dot(cov, hi) + _dot(cov, lo)
        scores.append(jnp.where(valid, jnp.where(forced, FORCE_SCORE, imp), -1.0))
    picked = _topk_mask(jnp.concatenate(scores, axis=1), SLC_TOP, axis=0)
    for kh in range(KVH):
        sel_t = jnp.where(valid & (picked[:, kh * tq:(kh + 1) * tq] > 0.5), 0.0, NEG)
        if nb < LANES:
            sel_t = jnp.concatenate([sel_t, jnp.full((LANES - nb, tq), NEG, f32)], axis=0)
        selt_ref[0, kh] = sel_t


def _cmp_attn(kvc_t, w1_pairs, bias, w2_heads, q_t, gate_t, tq):
    b, _, lq = kvc_t.shape
    assert lq % (PAGE_SIZE * PAGE_GROUP) == 0
    nc = lq // CMP_STRIDE
    n_slc = lq // SLC_LEN
    perm = _row_gather_matrix()
    end_pos = jnp.arange(nc) * CMP_STRIDE + (CMP_LEN - 1)
    cos, sa, sb = _rope_tables(end_pos)
    cov = _cover_t(n_slc, nc)
    full = lambda a: pl.BlockSpec(a.shape, lambda i, t: (0,) * a.ndim)
    return pl.pallas_call(
        functools.partial(_cmp_attn_kernel, tq=tq),
        grid=(b, lq // tq),
        in_specs=[pl.BlockSpec((1, KV_ROW, lq), lambda i, t: (i, 0, 0)),
                  full(perm), full(w1_pairs), full(bias), full(w2_heads), full(cos), full(sa), full(sb),
                  pl.BlockSpec((1, H_NSA, DH, tq), lambda i, t: (i, 0, 0, t)),
                  pl.BlockSpec((1, gate_t.shape[1], tq), lambda i, t: (i, 0, t)),
                  full(cov)],
        out_specs=(pl.BlockSpec((1, NSA_WIDTH, tq), lambda i, t: (i, 0, t)),
                   pl.BlockSpec((1, KVH, LANES, tq), lambda i, t: (i, 0, 0, t))),
        out_shape=(jax.ShapeDtypeStruct((b, NSA_WIDTH, lq), f32),
                   jax.ShapeDtypeStruct((b, KVH, LANES, lq), f32)),
        scratch_shapes=[pltpu.VMEM((2, CMP_STRIDE, nc, KVH * DH), f32),
                        pltpu.VMEM((KVH, nc, DH), bf16), pltpu.VMEM((KVH, DH, nc), bf16)],
        compiler_params=_cparams(("parallel", "arbitrary")),
        name="cmp_attn",
    )(kvc_t, perm, w1_pairs, bias, w2_heads, cos, sa, sb, q_t, gate_t, cov)


def _block_expand(n_slc_pad, n_keys):
    e = (np.arange(n_keys)[None, :] // SLC_LEN) == np.arange(n_slc_pad)[:, None]
    return jnp.asarray(e.astype(np.float32), dtype=bf16)


def _mask_bias_t(allowed):
    bias = jnp.where(allowed, 0.0, NEG)
    return jnp.concatenate([bias] * G_NSA, axis=1)


def _sublane_fold(x, op):
    return functools.reduce(op, [x[r:r + 8] for r in range(0, x.shape[0], 8)])


def _slc_win_kernel(qt_ref, ks_ref, vst_ref, kw_ref, vwt_ref, selt_ref, gatet_ref, ocmpt_ref, zn_ref,
                    o_ref, s_a, s_b, m_a, m_b, ws_a, ws_b, wm_a, wm_b, acc_ref, wacc_ref, *, tq, tk, nt):
    t = pl.program_id(2)

    @pl.when(t == 0)
    def _():
        m_b[...] = jnp.full(m_b.shape, NEG, f32)
        ws_b[...] = jnp.zeros(ws_b.shape, f32)
        wm_b[...] = jnp.zeros(wm_b.shape, f32)

    step = functools.partial(_slc_win_step, qt_ref, ks_ref, vst_ref, kw_ref, vwt_ref, selt_ref, gatet_ref,
                             ocmpt_ref, zn_ref, o_ref, acc_ref, wacc_ref, tq=tq, tk=tk, nt=nt)

    @pl.when(t % 2 == 0)
    def _():
        step(s_a, s_b, m_a, m_b, ws_a, ws_b, wm_a, wm_b)

    @pl.when(t % 2 == 1)
    def _():
        step(s_b, s_a, m_b, m_a, ws_b, ws_a, wm_b, wm_a)


ONES_ROWS = 16


def _slc_win_step(qt_ref, ks_ref, vst_ref, kw_ref, vwt_ref, selt_ref, gatet_ref, ocmpt_ref, zn_ref, o_ref,
                  acc_ref, wacc_ref, s_cur, s_prev, m_cur, m_last, ws_cur, ws_prev, wm_cur, wm_last,
                  *, tq, tk, nt):
    kh = pl.program_id(1)
    t = pl.program_id(2)
    lq = ks_ref.shape[2]
    band = WINDOW + tq
    blocks_per_tile = tk // SLC_LEN
    key_tiles = lambda tile: (tile * tq + tq + tk - 1) // tk
    n_apply = jnp.where(t >= 1, key_tiles(t - 1), 0)
    n_plain = jnp.where(t < nt, (t * tq) // tk, 0)
    q0 = t * tq
    qpos = q0 + lax.broadcasted_iota(jnp.int32, (1, tq), 1)
    q_t = jnp.concatenate([qt_ref[0, g] for g in range(G_NSA)], axis=1)
    ones = jnp.ones((ONES_ROWS, tk), bf16)

    m_cur[...] = jnp.full(m_cur.shape, NEG, f32)
    m_prev = jnp.max(m_last[...], axis=0, keepdims=True)
    acc_ref[...] = jnp.zeros(acc_ref.shape, f32)

    def score(kt, causal):
        k0 = pl.multiple_of(kt * tk, tk)
        j0 = pl.multiple_of(kt * blocks_per_tile, blocks_per_tile)
        sel = selt_ref[0, 0, pl.ds(j0, blocks_per_tile), :]
        bias = jnp.concatenate([jnp.broadcast_to(sel[j:j + 1], (SLC_LEN, tq)) for j in range(blocks_per_tile)],
                               axis=0)
        if causal:
            kpos = k0 + lax.broadcasted_iota(jnp.int32, (tk, 1), 0)
            bias = jnp.where(kpos <= qpos, bias, NEG)
        s = _dot(ks_ref[0, 0, pl.ds(k0, tk), :], q_t) + jnp.concatenate([bias] * G_NSA, axis=1)
        s_cur[pl.ds(k0, tk), :] = s
        m_cur[...] = jnp.maximum(m_cur[...], _sublane_fold(s, jnp.maximum))

    def apply(kt):
        k0 = pl.multiple_of(kt * tk, tk)
        p = jnp.exp2(s_prev[pl.ds(k0, tk), :] - m_prev).astype(bf16)
        acc_ref[...] += _dot(jnp.concatenate([vst_ref[0, 0, :, pl.ds(k0, tk)], ones], axis=0), p)

    def both_body(kt, carry):
        score(kt, causal=False)
        apply(kt)
        return carry

    def apply_body(kt, carry):
        apply(kt)
        return carry

    def both2_body(i, carry):
        for u in range(2):
            score(2 * i + u, causal=False)
            apply(2 * i + u)
        return carry

    n_both = jnp.minimum(n_plain, n_apply)
    lax.fori_loop(0, n_both // 2, both2_body, 0)
    lax.fori_loop(n_both // 2 * 2, n_both, both_body, 0)
    lax.fori_loop(n_both, n_apply, apply_body, 0)
    score(jnp.minimum((t * tq) // tk, lq // tk - 1), causal=True)

    w0 = pl.multiple_of(jnp.clip(q0 - WINDOW, 0, lq - band), tq)
    w0_prev = pl.multiple_of(jnp.clip(q0 - tq - WINDOW, 0, lq - band), tq)
    wm_prev = jnp.max(wm_last[...], axis=0, keepdims=True)
    ones_w = jnp.ones((ONES_ROWS, tq), bf16)
    wmax, wacc = None, None
    for j in range(band // tq):
        rows = pl.ds(j * tq, tq)
        kpos = w0 + j * tq + lax.broadcasted_iota(jnp.int32, (tq, 1), 0)
        s = (_dot(kw_ref[0, 0, pl.ds(pl.multiple_of(w0 + j * tq, tq), tq), :], q_t)
             + _mask_bias_t((kpos <= qpos) & (kpos > qpos - WINDOW)))
        ws_cur[rows, :] = s
        fold = _sublane_fold(s, jnp.maximum)
        wmax = fold if wmax is None else jnp.maximum(wmax, fold)
        p = jnp.exp2(ws_prev[rows, :] - wm_prev).astype(bf16)
        v_ext = jnp.concatenate([vwt_ref[0, 0, :, pl.ds(pl.multiple_of(w0_prev + j * tq, tq), tq)], ones_w], axis=0)
        part = _dot(v_ext, p)
        wacc = part if wacc is None else wacc + part
    wm_cur[...] = wmax
    wacc_ref[...] = wacc

    @pl.when(t >= 1)
    def _():
        o_slc = acc_ref[:DH] / acc_ref[DH:DH + 1]
        o_win = wacc_ref[:DH] / wacc_ref[DH:DH + 1]
        gate_t = gatet_ref[0]
        row = lambda c, g: jnp.where(kh == 0, gate_t[c * H_NSA + g:c * H_NSA + g + 1],
                                     gate_t[c * H_NSA + G_NSA + g:c * H_NSA + G_NSA + g + 1])
        g_slc = jnp.concatenate([row(1, g) for g in range(G_NSA)], axis=1)
        g_win = jnp.concatenate([row(2, g) for g in range(G_NSA)], axis=1)
        o_t = g_slc * o_slc + g_win * o_win
        o_heads = jnp.concatenate([o_t[:, g * tq:(g + 1) * tq] for g in range(G_NSA)], axis=0)
        o_ref[0] = ((ocmpt_ref[0] + o_heads).T * zn_ref[...]).astype(o_ref.dtype)


def _slc_win(q_t, ks, vs_t, kw, vw_t, sel_t, gate_t, o_cmp_t, zn, tq, tk):
    b, _, _, lq = q_t.shape
    assert lq >= WINDOW + tq and lq % tk == 0 and tk % (8 * SLC_LEN) == 0
    gw = G_NSA * DH
    k_spec = pl.BlockSpec((1, 1, lq, DH), lambda i, k, t: (i, k, 0, 0))
    vt_spec = pl.BlockSpec((1, 1, DH, lq), lambda i, k, t: (i, k, 0, 0))
    lanes = G_NSA * tq
    band = WINDOW + tq
    nt = lq // tq
    scored = lambda t: jnp.minimum(t, nt - 1)
    drained = lambda t: jnp.maximum(t - 1, 0)
    return pl.pallas_call(
        functools.partial(_slc_win_kernel, tq=tq, tk=tk, nt=nt),
        grid=(b, KVH, nt + 1),
        in_specs=[pl.BlockSpec((1, G_NSA, DH, tq), lambda i, k, t: (i, k, 0, scored(t))),
                  k_spec, vt_spec, k_spec, vt_spec,
                  pl.BlockSpec((1, 1, LANES, tq), lambda i, k, t: (i, k, 0, scored(t))),
                  pl.BlockSpec((1, gate_t.shape[1], tq), lambda i, k, t: (i, 0, drained(t))),
                  pl.BlockSpec((1, gw, tq), lambda i, k, t: (i, k, drained(t))),
                  pl.BlockSpec((tq, gw), lambda i, k, t: (i * nt + drained(t), k))],
        out_specs=pl.BlockSpec((1, tq, gw), lambda i, k, t: (i, drained(t), k)),
        out_shape=jax.ShapeDtypeStruct((b, lq, NSA_WIDTH), bf16),
        scratch_shapes=[pltpu.VMEM((lq, lanes), f32), pltpu.VMEM((lq, lanes), f32),
                        pltpu.VMEM((8, lanes), f32), pltpu.VMEM((8, lanes), f32),
                        pltpu.VMEM((band, lanes), f32), pltpu.VMEM((band, lanes), f32),
                        pltpu.VMEM((8, lanes), f32), pltpu.VMEM((8, lanes), f32),
                        pltpu.VMEM((DH + ONES_ROWS, lanes), f32),
                        pltpu.VMEM((DH + ONES_ROWS, lanes), f32)],
        compiler_params=_cparams(("parallel", "parallel", "arbitrary")),
        name="slc_win",
    )(q_t, ks, vs_t, kw, vw_t, sel_t, gate_t, o_cmp_t, zn)


def _rms_gain(o, gain):
    return o * lax.rsqrt(jnp.mean(o * o, axis=-1, keepdims=True) + EPS) * gain


def _gla_kernel(q_ref, k_ref, v_ref, la_ref, z_ref, gain_ref, s0_ref, y_ref, st_ref, s_scr, *, tl, chunk):
    t = pl.program_id(1)

    @pl.when(t == 0)
    def _():
        s_scr[...] = s0_ref[0]

    c = chunk
    pairs = s_scr.shape[0]
    row = lax.broadcasted_iota(jnp.int32, (c, c), 0)
    col = lax.broadcasted_iota(jnp.int32, (c, c), 1)
    causal = col <= row
    tril = jnp.where(causal, 1.0, 0.0).astype(bf16)
    lane = lax.broadcasted_iota(jnp.int32, (1, LANES), 1)
    head_mask = [jnp.where(lane < DK, 1.0, 0.0), jnp.where(lane >= DK, 1.0, 0.0)]
    gain = gain_ref[...]

    chunks = []
    for ci, pr in [(ci, pr) for ci in range(tl // c) for pr in range(pairs)]:
        rows = slice(ci * c, (ci + 1) * c)
        grp = slice(pr * LANES, (pr + 1) * LANES)
        q, k, la = q_ref[rows, grp], k_ref[rows, grp], la_ref[rows, grp]
        hi, lo = _split_bf16(la)
        bc = _dot(tril, hi) + _dot(tril, lo)
        ref_row = bc[c // 2 - 1:c // 2, :]
        b_last = bc[c - 1:c, :]
        q_a = q * jnp.exp(bc - ref_row)
        k_a = (k * jnp.exp(ref_row - bc)).astype(bf16)
        q_s = q * jnp.exp(bc)
        k_s = k * jnp.exp(b_last - bc)
        o_intra, q_inter, add = [], [], None
        for h in range(2):
            mh = head_mask[h]
            vh = v_ref[rows, (2 * pr + h) * DV:(2 * pr + h + 1) * DV].astype(bf16)
            a = jnp.where(causal, _dot_nt((q_a * mh).astype(bf16), k_a), 0.0)
            o_intra.append(_dot(a.astype(bf16), vh))
            q_inter.append((q_s * mh).astype(bf16))
            u = _dot_tn(vh, (k_s * mh).astype(bf16))
            add = u if add is None else add + u
        chunks.append((rows, pr, jnp.exp(b_last), add, o_intra, q_inter))

    st = [s_scr[pr] for pr in range(pairs)]
    for rows, pr, decay, add, o_intra, q_inter in chunks:
        st_b = st[pr].astype(bf16)
        for h in range(2):
            cols = slice((2 * pr + h) * DV, (2 * pr + h + 1) * DV)
            o = o_intra[h] + _dot_nt(q_inter[h], st_b)
            y_ref[rows, cols] = (_rms_gain(o, gain) * z_ref[rows, cols]).astype(y_ref.dtype)
        st[pr] = st[pr] * decay + add
    for pr in range(pairs):
        s_scr[pr] = st[pr]
        st_ref[0, pr] = st[pr]


def _gla(gq, gk, gv, la, zg, gain, state0, b, tl, chunk):
    n = gq.shape[0]
    nt = n // b // tl
    pairs = H_GLA // 2
    qk = pl.BlockSpec((tl, H_GLA * DK), lambda i, t: (i * nt + t, 0))
    vz = pl.BlockSpec((tl, GLA_WIDTH), lambda i, t: (i * nt + t, 0))
    st_spec = pl.BlockSpec((1, pairs, DV, LANES), lambda i, t: (i, 0, 0, 0))
    gain2 = gain.reshape(1, DV)
    s0 = state0.reshape(b, pairs, 2, DK, DV).transpose(0, 1, 4, 2, 3).reshape(b, pairs, DV, LANES)
    y, st = pl.pallas_call(
        functools.partial(_gla_kernel, tl=tl, chunk=chunk),
        grid=(b, nt),
        in_specs=[qk, qk, vz, qk, vz, pl.BlockSpec((1, DV), lambda i, t: (0, 0)), st_spec],
        out_specs=(vz, st_spec),
        out_shape=(jax.ShapeDtypeStruct((n, GLA_WIDTH), bf16),
                   jax.ShapeDtypeStruct((b, pairs, DV, LANES), f32)),
        scratch_shapes=[pltpu.VMEM((pairs, DV, LANES), f32)],
        compiler_params=_cparams(("parallel", "arbitrary")),
        name="gla",
    )(gq, gk, gv, la, zg, gain2, s0)
    state = st.reshape(b, pairs, DV, 2, DK).transpose(0, 1, 3, 4, 2).reshape(b, H_GLA, DK, DV)
    return y, state


def _outproj_kernel(x_ref, yn_ref, yg_ref, w_ref, g_ref, y_ref):
    mix = _dot(yn_ref[...].astype(bf16), w_ref[:NSA_WIDTH, :]) + _dot(yg_ref[...].astype(bf16), w_ref[NSA_WIDTH:, :])
    y_ref[...] = _rms_gain(x_ref[...] + mix, g_ref[...])


def _outproj(x, y_nsa, y_gla, w_out, gain, tm):
    n = x.shape[0]
    tok = lambda w: pl.BlockSpec((tm, w), lambda i: (i, 0))
    g2 = gain.reshape(1, D_MODEL)
    return pl.pallas_call(
        _outproj_kernel,
        grid=(n // tm,),
        in_specs=[tok(D_MODEL), tok(NSA_WIDTH), tok(GLA_WIDTH),
                  pl.BlockSpec(w_out.shape, lambda i: (0, 0)), pl.BlockSpec(g2.shape, lambda i: (0, 0))],
        out_specs=tok(D_MODEL),
        out_shape=jax.ShapeDtypeStruct((n, D_MODEL), f32),
        compiler_params=_cparams(("parallel",)),
        name="outproj",
    )(x, y_nsa, y_gla, w_out, g2)


def _prompt_path(x, wts, w_t, w_tok):
    (norm_g, _, wa2_pad, ba, w1_pairs, cmp_bias, w2_heads, gla_gain, w_out, out_gain) = wts
    b, lq, _ = x.shape
    (q_t, kvc_t, kvs_t, kvw_t, ks, kw, vs_t, vw_t, gate_t, zn, gq, gk, gv, la, zg) = _inproj_prompt(
        x, norm_g, w_t, w_tok, wa2_pad, ba, tm=512)
    o_cmp_t, sel_t = _cmp_attn(kvc_t, w1_pairs, cmp_bias, w2_heads, q_t, gate_t, tq=512)
    y_nsa = _slc_win(q_t, ks, vs_t, kw, vw_t, sel_t, gate_t, o_cmp_t, zn, tq=256, tk=512)
    y_gla, state = _gla(gq, gk, gv, la, zg, gla_gain, jnp.zeros((b, H_GLA, DK, DV), f32), b, tl=512,
                        chunk=GLA_CHUNK)
    y = _outproj(x.reshape(b * lq, D_MODEL), y_nsa.reshape(b * lq, NSA_WIDTH), y_gla, w_out, out_gain, tm=1024)
    kv5 = lambda a: a.reshape(b, 2, KVH, DH, -1).transpose(0, 4, 1, 2, 3)
    wlen = min(WINDOW, lq)
    return (y.reshape(b, lq, D_MODEL), kv5(kvc_t), kv5(kvs_t), kv5(kvw_t[:, :, lq - wlen:]), state)


def _pages_native(cache):
    n_pool = cache.shape[0]
    return cache.transpose(0, 2, 3, 4, 1).reshape(n_pool, 2, KVH * DH, cache.shape[1])


def _page_gather(make_copies):
    i = pl.program_id(0)
    slot = i % 2

    @pl.when(i == 0)
    def _():
        for c in make_copies(0, 0):
            c.start()

    @pl.when(i + 1 < pl.num_programs(0))
    def _():
        for c in make_copies(i + 1, 1 - slot):
            c.start()

    for c in make_copies(i, slot):
        c.wait()
    return slot


def _group_sum_matrix(n_tok):
    r = np.arange(KVH * n_tok)[:, None]
    c = np.arange(H_NSA * n_tok)[None, :]
    m = ((c // (G_NSA * n_tok)) == (r // n_tok)) & ((c % n_tok) == (r % n_tok))
    return m.astype(np.float32)


def _cmp_attn_sample_kernel(pt_ref, cache_ref, perm_ref, w1_ref, bias_ref, w2_ref, cos_ref, sa_ref, sb_ref, q_ref,
                            gate_ref, cov_ref, gs_ref, gst_ref, o_ref, sel_ref, buf, rows_scr, sem,
                            *, n_pages, n_tok, past_len):
    def copies(seq, slot):
        return [pltpu.make_async_copy(cache_ref.at[pt_ref[seq, p]], buf.at[slot, p], sem.at[slot])
                for p in range(n_pages)]

    slot = _page_gather(copies)

    page_at = lambda p: buf[slot, p].reshape(2 * KVH * DH, PAGE_SIZE)
    k, v = _compress_pages(page_at, n_pages, perm_ref[...], rows_scr, w1_ref, bias_ref, w2_ref)
    k = _rope128(k, cos_ref[...], sa_ref[...], sb_ref[...])
    nc = n_pages * CHUNKS_PER_PAGE
    rows = H_NSA * n_tok
    qpos = past_len + lax.broadcasted_iota(jnp.int32, (rows, 1), 0) % n_tok
    end_c = lax.broadcasted_iota(jnp.int32, (1, nc), 1) * CMP_STRIDE + (CMP_LEN - 1)
    mask = end_c <= qpos
    p = _masked_softmax(_dot_nt(q_ref[0], k.astype(bf16)), mask, axis=-1)
    o_ref[0] = _dot(p.astype(bf16), v.astype(bf16)) * gate_ref[0][:, 0:1]
    hi, lo = _split_bf16(p)
    gs = gs_ref[...]
    ph, pl_ = _split_bf16(_dot(gs, hi) + _dot(gs, lo))
    imp = _dot(ph, cov_ref[...]) + _dot(pl_, cov_ref[...])
    n_blk = imp.shape[1]
    jblk = lax.broadcasted_iota(jnp.int32, imp.shape, 1)
    forced = (jblk == 0) | (jblk == n_blk - 1)
    sel = _topk_rows_by_rank(jnp.where(forced, FORCE_SCORE, imp), SLC_TOP - 1)
    sel_ref[0] = _dot(gst_ref[...], sel.astype(bf16)).astype(bf16)


def _cmp_attn_sample(cache_t, page_table, w1_pairs, bias, w2_heads, q_blk, gate_rows, n_tok):
    b, n_pages = page_table.shape
    past_len = n_pages * PAGE_SIZE
    nc = past_len // CMP_STRIDE
    rows = H_NSA * n_tok
    n_blk = past_len // SLC_LEN
    end_pos = jnp.arange(nc) * CMP_STRIDE + (CMP_LEN - 1)
    cos, sa, sb = _rope_tables(end_pos)
    cov = _cover_t(n_blk, nc).T
    gs = jnp.asarray(_group_sum_matrix(n_tok), dtype=bf16)
    gst = gs.T
    perm = _row_gather_matrix()
    full = lambda a: pl.BlockSpec(a.shape, lambda i, pt: (0,) * a.ndim)
    seq = lambda a: pl.BlockSpec((1,) + a.shape[1:], lambda i, pt: (i,) + (0,) * (a.ndim - 1))
    grid_spec = pltpu.PrefetchScalarGridSpec(
        num_scalar_prefetch=1,
        grid=(b,),
        in_specs=[pl.BlockSpec(memory_space=pl.ANY), full(perm), full(w1_pairs), full(bias), full(w2_heads),
                  full(cos), full(sa), full(sb), seq(q_blk), seq(gate_rows), full(cov), full(gs), full(gst)],
        out_specs=(pl.BlockSpec((1, rows, LANES), lambda i, pt: (i, 0, 0)),
                   pl.BlockSpec((1, rows, n_blk), lambda i, pt: (i, 0, 0))),
        scratch_shapes=[pltpu.VMEM((2, n_pages) + cache_t.shape[1:], f32),
                        pltpu.VMEM((2, CMP_STRIDE, nc, KVH * DH), f32),
                        pltpu.SemaphoreType.DMA((2,))],
    )
    return pl.pallas_call(
        functools.partial(_cmp_attn_sample_kernel, n_pages=n_pages, n_tok=n_tok, past_len=past_len),
        grid_spec=grid_spec,
        out_shape=(jax.ShapeDtypeStruct((b, rows, LANES), f32),
                   jax.ShapeDtypeStruct((b, rows, n_blk), bf16)),
        compiler_params=_cparams(("arbitrary",)),
        name="cmp_attn_sample",
    )(page_table, cache_t, perm, w1_pairs, bias, w2_heads, cos, sa, sb, q_blk, gate_rows, cov, gs, gst)


def _slc_win_sample_kernel(pt_ref, cache_ref, q_ref, sel_ref, e_ref, snew_ref, cw_ref, wnew_ref, wnewt_ref,
                           gate_ref, ocmp_ref, zn_ref, o_ref, wout_ref, buf, sem, *, n_pages, n_tok, win_off):
    def copies(seq, slot):
        return [pltpu.make_async_copy(cache_ref.at[pt_ref[seq, p], c],
                                      buf.at[slot, c, :, pl.ds(p * PAGE_SIZE, PAGE_SIZE)], sem.at[slot])
                for p in range(n_pages) for c in range(2)]

    slot = _page_gather(copies)
    rows = H_NSA * n_tok
    q = q_ref[0]
    tok = lax.broadcasted_iota(jnp.int32, (rows, 1), 0) % n_tok
    new_i = lax.broadcasted_iota(jnp.int32, (1, snew_ref.shape[1]), 1)
    new_ok = (new_i <= tok) & (new_i < n_tok)

    def attend(keys_t, vals_t, allowed, new_ref):
        s = jnp.where(allowed, _dot(q, keys_t), NEG)
        k_new = new_ref[0][:, :LANES].astype(bf16)
        v_new = new_ref[0][:, LANES:].astype(bf16)
        s_new = jnp.where(new_ok, _dot_nt(q, k_new), NEG)
        m = jnp.maximum(jnp.max(s, axis=-1, keepdims=True), jnp.max(s_new, axis=-1, keepdims=True))
        p = jnp.where(allowed, jnp.exp(s - m), 0.0)
        p_new = jnp.where(new_ok, jnp.exp(s_new - m), 0.0)
        l = jnp.sum(p, axis=-1, keepdims=True) + jnp.sum(p_new, axis=-1, keepdims=True)
        return (_dot_nt(p.astype(bf16), vals_t) + _dot(p_new.astype(bf16), v_new)) / l

    picked = _dot(sel_ref[0], e_ref[...]) > 0.5
    o_slc = attend(buf[slot, 0].astype(bf16), buf[slot, 1].astype(bf16), picked, snew_ref)

    wbuf = cw_ref.shape[-1]
    win_i = lax.broadcasted_iota(jnp.int32, (1, wbuf), 1)
    o_win = attend(cw_ref[0, 0].astype(bf16), cw_ref[0, 1].astype(bf16), win_i > tok + win_off, wnew_ref)

    lane = lax.broadcasted_iota(jnp.int32, (1, LANES), 1)
    for c in range(2):
        shifted = pltpu.roll(cw_ref[0, c], wbuf - n_tok, 1)
        tail = jnp.where(lane >= LANES - n_tok, wnewt_ref[0, c], shifted[:, wbuf - LANES:])
        wout_ref[0, c] = jnp.concatenate([shifted[:, :wbuf - LANES], tail], axis=-1)

    gate = gate_ref[0]
    o = ocmp_ref[0] + gate[:, 1:2] * o_slc + gate[:, 2:3] * o_win
    second_kvh = lax.broadcasted_iota(jnp.int32, (rows, 1), 0) >= G_NSA * n_tok
    o_ref[0] = jnp.where(second_kvh, o[:, DH:], o[:, :DH]) * zn_ref[0]


def _slc_win_sample(cache_t, page_table, q_blk, sel_rows, kvs_new, win_t, kvw_new, kvw_new_t, gate_rows,
                    o_cmp, zn_rows, n_tok):
    b, n_pages = page_table.shape
    rows = H_NSA * n_tok
    past_len = n_pages * PAGE_SIZE
    wbuf = win_t.shape[-1]
    e = _block_expand(sel_rows.shape[-1], past_len)
    full = lambda a: pl.BlockSpec(a.shape, lambda i, pt: (0,) * a.ndim)
    seq = lambda a: pl.BlockSpec((1,) + a.shape[1:], lambda i, pt: (i,) + (0,) * (a.ndim - 1))
    grid_spec = pltpu.PrefetchScalarGridSpec(
        num_scalar_prefetch=1,
        grid=(b,),
        in_specs=[pl.BlockSpec(memory_space=pl.ANY), seq(q_blk), seq(sel_rows), full(e), seq(kvs_new),
                  seq(win_t), seq(kvw_new), seq(kvw_new_t), seq(gate_rows), seq(o_cmp), seq(zn_rows)],
        out_specs=(pl.BlockSpec((1, rows, DH), lambda i, pt: (i, 0, 0)), seq(win_t)),
        scratch_shapes=[pltpu.VMEM((2, 2, KVH * DH, past_len), f32), pltpu.SemaphoreType.DMA((2,))],
    )
    return pl.pallas_call(
        functools.partial(_slc_win_sample_kernel, n_pages=n_pages, n_tok=n_tok, win_off=wbuf - WINDOW),
        grid_spec=grid_spec,
        out_shape=(jax.ShapeDtypeStruct((b, rows, DH), f32), jax.ShapeDtypeStruct(win_t.shape, f32)),
        compiler_params=_cparams(("arbitrary",)),
        name="slc_win_sample",
    )(page_table, cache_t, q_blk, sel_rows, e, kvs_new, win_t, kvw_new, kvw_new_t, gate_rows, o_cmp, zn_rows)


def _gla_sample(gq, gk, la, gv, zg, state, gain, b, n_tok):
    rows = 8
    pad = lambda a: jnp.pad(a.reshape(b, n_tok, -1), ((0, 0), (0, rows - n_tok), (0, 0))).reshape(b * rows, -1)
    y, s_new = _gla(pad(gq), pad(gk), pad(gv), pad(la), pad(zg), gain, state, b, tl=rows, chunk=rows)
    return y.reshape(b, rows, GLA_WIDTH)[:, :n_tok].reshape(b * n_tok, GLA_WIDTH), s_new


def _sample_path(x, cache_c, cache_s, cache_w, state, page_table, wts):
    (norm_g, w_pack, wa2_pad, ba, w1_pairs, cmp_bias, w2_heads, gla_gain, w_out, out_gain) = wts
    b, n_tok, _ = x.shape
    n = b * n_tok
    n_pages = page_table.shape[1]
    past_len = n_pages * PAGE_SIZE
    assert n_tok <= 8 and past_len % SLC_LEN == 0 and past_len // SLC_LEN <= LANES
    pos = past_len + jnp.arange(n) % n_tok
    (q_hm, kvc, kvs, kvw, _, _, _, _, gate, zn, gq, gk, gv, la, zg) = _inproj(
        x.reshape(1, n, D_MODEL), pos, norm_g, w_pack, wa2_pad, ba, tm=n)

    rows = H_NSA * n_tok
    q_rows = q_hm[0].reshape(H_NSA, b, n_tok, DH).transpose(1, 0, 2, 3)
    zero = jnp.zeros_like(q_rows[:, :G_NSA])
    q_blk = jnp.concatenate([jnp.concatenate([q_rows[:, :G_NSA], zero], axis=-1),
                             jnp.concatenate([zero, q_rows[:, G_NSA:]], axis=-1)], axis=1).reshape(b, rows, LANES)
    gate_rows = gate[:, :3 * H_NSA].reshape(b, n_tok, 3, H_NSA).transpose(0, 3, 1, 2).reshape(b, rows, 3)
    gate_rows = jnp.pad(gate_rows, ((0, 0), (0, 0), (0, LANES - 3)))
    pad_new = lambda a: jnp.pad(a.reshape(b, n_tok, KV_ROW), ((0, 0), (0, 8 - n_tok), (0, 0)))

    kvw_new_t = kvw.reshape(b, n_tok, 2, KVH * DH).transpose(0, 2, 3, 1)
    kvw_new_t = jnp.pad(kvw_new_t, ((0, 0), (0, 0), (0, 0), (LANES - n_tok, 0)))

    o_cmp, sel_rows = _cmp_attn_sample(_pages_native(cache_c), page_table, w1_pairs, cmp_bias, w2_heads,
                                       q_blk, gate_rows, n_tok)
    o_rows, win_t = _slc_win_sample(_pages_native(cache_s), page_table, q_blk, sel_rows, pad_new(kvs),
                                    _pages_native(cache_w), pad_new(kvw), kvw_new_t, gate_rows, o_cmp,
                                    zn.reshape(b, n_tok, H_NSA, DH).transpose(0, 2, 1, 3).reshape(b, rows, DH),
                                    n_tok)
    y_nsa = o_rows.reshape(b, H_NSA, n_tok, DH).transpose(0, 2, 1, 3).reshape(n, NSA_WIDTH)

    y_gla, s_new = _gla_sample(gq, gk, la, gv, zg, state, gla_gain, b, n_tok)
    y = _outproj(x.reshape(n, D_MODEL), y_nsa, y_gla, w_out, out_gain, tm=n)
    kv5 = lambda a: a.reshape(b, n_tok, 2, KVH, DH)
    win_new = win_t.reshape(b, 2, KVH, DH, -1).transpose(0, 4, 1, 2, 3)
    return (y.reshape(b, n_tok, D_MODEL), kv5(kvc), kv5(kvs), win_new, s_new)


def kernel(x_prompt, x_sample, cache_cmp_kv, cache_slc_kv, cache_win_kv, state_gla, page_table,
           norm_in_gain, w_in, cmp_pe, cmp_w1, cmp_b1, cmp_w2, gla_wa2, gla_ba, gla_norm_gain,
           w_out, norm_out_gain):
    assert w_in.shape[0] == 1, "single-layer step"
    wts = (norm_in_gain[0], _pack_w_in(w_in[0]), _pad_wa2(gla_wa2[0]), gla_ba[0],
           _cmp_w1_pairs(cmp_w1[0]), _cmp_bias(cmp_pe[0], cmp_w1[0], cmp_b1[0]), _cmp_w2_heads(cmp_w2[0]),
           gla_norm_gain[0], w_out[0].astype(bf16), norm_out_gain)
    yp, cmp_p, slc_p, win_p, gla_p = _prompt_path(x_prompt, wts, *_pack_w_in_prompt(w_in[0]))
    ys, cmp_s, slc_s, win_s, gla_s = _sample_path(
        x_sample, cache_cmp_kv[0], cache_slc_kv[0], cache_win_kv[0], state_gla[0], page_table, wts)
    return (yp, ys, cmp_p[None], cmp_s[None], slc_p[None], slc_s[None], win_p[None], win_s[None],
            gla_p[None], gla_s[None])
```

```python
import functools

import numpy as np
import jax
import jax.numpy as jnp
from jax import lax
from jax.experimental import pallas as pl
from jax.experimental.pallas import tpu as pltpu

f32 = jnp.float32
bf16 = jnp.bfloat16

D_MODEL = 1024
DH = 64
H_NSA = 8
KVH = 2
G_NSA = 4
NSA_WIDTH = H_NSA * DH
CMP_LEN = 32
CMP_STRIDE = 16
CMP_HID = 128
SLC_LEN = 64
SLC_TOP = 16
WINDOW = 512
ROT_HALF = 8
ROPE_THETA = 500000.0
FORCE_SCORE = 1.0e4
H_GLA = 4
DK = 64
DV = 128
GLA_WIDTH = H_GLA * DV
GLA_LR = 16
GLA_TAU = 16.0
GLA_CHUNK = 64
GLA_SUB = 16
EPS = 1e-6
NEG = -1.0e30
LOG2E = 1.4426950408889634
PAGE_SIZE = 128
KV_ROW = 2 * KVH * DH

IN_SIZES = (H_NSA * DH, KV_ROW, KV_ROW, KV_ROW, 3 * H_NSA, NSA_WIDTH,
            H_GLA * DK, H_GLA * DK, H_GLA * DV, GLA_LR, GLA_WIDTH)
IN_OFFSETS = [0] + [int(v) for v in np.cumsum(IN_SIZES)]

LANES = 128
VMEM_LIMIT = 56 * 1024 * 1024

P_Q, P_KVC, P_KVS, P_KVW, P_ZN, P_GQ, P_GK, P_GV, P_ZG, P_MISC = (
    0, 512, 768, 1024, 1280, 1792, 2048, 2304, 2816, 3328)
D_PACK = P_MISC + LANES
MISC_GATE = 0
MISC_GLR = 32


def _cparams(sem):
    return pltpu.CompilerParams(dimension_semantics=sem, vmem_limit_bytes=VMEM_LIMIT)


def _sigmoid(x):
    return 1.0 / (1.0 + jnp.exp(-x))


def _silu(x):
    return x * _sigmoid(x)


def _log_sigmoid(x):
    return jnp.minimum(x, 0.0) - jnp.log1p(jnp.exp(-jnp.abs(x)))


def _dot(a, b):
    return jnp.dot(a, b, preferred_element_type=f32)


def _dot_nt(a, b):
    return lax.dot_general(a, b, (((1,), (1,)), ((), ())), preferred_element_type=f32)


def _dot_tn(a, b):
    return lax.dot_general(a, b, (((0,), (0,)), ((), ())), preferred_element_type=f32)


def _rope_tables(pos):
    n = pos.shape[0]
    inv = ROPE_THETA ** (-(jnp.arange(ROT_HALF, dtype=f32) / ROT_HALF))
    ang = pos.astype(f32)[:, None] * inv[None, :]
    cos, sin = jnp.cos(ang), jnp.sin(ang)
    z8 = jnp.zeros((n, ROT_HALF), f32)
    rest = DH - 2 * ROT_HALF
    c64 = jnp.concatenate([cos, cos, jnp.ones((n, rest), f32)], axis=-1)
    sa64 = jnp.concatenate([-sin, z8, jnp.zeros((n, rest), f32)], axis=-1)
    sb64 = jnp.concatenate([z8, sin, jnp.zeros((n, rest), f32)], axis=-1)
    tile = lambda t: jnp.concatenate([t, t], axis=-1)
    return tile(c64), tile(sa64), tile(sb64)


def _rope128(v, cos, sa, sb):
    return v * cos + pltpu.roll(v, LANES - ROT_HALF, 1) * sa + pltpu.roll(v, ROT_HALF, 1) * sb


def _inproj_kernel(x_ref, g_ref, w_ref, cos_ref, sa_ref, sb_ref, wa2_ref, ba_ref,
                   q_ref, kvc_ref, kvs_ref, kvw_ref, ks_ref, vs_ref, kw_ref, vw_ref,
                   gate_ref, zn_ref, gq_ref, gk_ref, gv_ref, la_ref, zg_ref):
    x = x_ref[...]
    ms = jnp.mean(x * x, axis=-1, keepdims=True)
    hn = (x * lax.rsqrt(ms + EPS) * g_ref[...]).astype(bf16)
    cos, sa, sb = cos_ref[...], sa_ref[...], sb_ref[...]

    def proj(off, width):
        return _dot(hn, w_ref[:, off:off + width])

    qp = proj(P_Q, NSA_WIDTH)
    for c in range(NSA_WIDTH // LANES):
        r = _rope128(qp[:, c * LANES:(c + 1) * LANES], cos, sa, sb) * (DH ** -0.5)
        q_ref[0, 2 * c] = r[:, :DH].astype(bf16)
        q_ref[0, 2 * c + 1] = r[:, DH:].astype(bf16)

    kvc_ref[...] = proj(P_KVC, KV_ROW)

    for off, kv_ref, k_ref, v_ref in ((P_KVS, kvs_ref, ks_ref, vs_ref), (P_KVW, kvw_ref, kw_ref, vw_ref)):
        p = proj(off, KV_ROW)
        k = _rope128(p[:, :LANES], cos, sa, sb)
        v = p[:, LANES:]
        kv_ref[:, :LANES] = k
        kv_ref[:, LANES:] = v
        for h in range(KVH):
            k_ref[0, h] = k[:, h * DH:(h + 1) * DH].astype(bf16)
            v_ref[0, h] = v[:, h * DH:(h + 1) * DH].astype(bf16)

    zn_ref[...] = _silu(proj(P_ZN, NSA_WIDTH))
    gq_ref[...] = proj(P_GQ, H_GLA * DK) * (DK ** -0.5)
    gk_ref[...] = proj(P_GK, H_GLA * DK)
    gv_ref[...] = proj(P_GV, H_GLA * DV)
    zg_ref[...] = _silu(proj(P_ZG, GLA_WIDTH))

    misc = proj(P_MISC, LANES)
    gate_ref[...] = _sigmoid(misc)
    xa = _dot(misc.astype(bf16), wa2_ref[...]) + ba_ref[...]
    la_ref[...] = _log_sigmoid(xa) / GLA_TAU


def _pack_w_in(w_in):
    o = IN_OFFSETS
    seg = lambda i: w_in[:, o[i]:o[i + 1]]
    misc = jnp.zeros((D_MODEL, LANES), w_in.dtype)
    misc = misc.at[:, MISC_GATE:MISC_GATE + 3 * H_NSA].set(seg(4))
    misc = misc.at[:, MISC_GLR:MISC_GLR + GLA_LR].set(seg(9))
    cols = [seg(0), seg(1), seg(2), seg(3), seg(5), seg(6), seg(7), seg(8), seg(10), misc]
    return jnp.concatenate(cols, axis=1).astype(bf16)


def _pad_wa2(wa2):
    pad = jnp.zeros((LANES, H_GLA * DK), wa2.dtype).at[MISC_GLR:MISC_GLR + GLA_LR].set(wa2)
    return pad.astype(bf16)


def _inproj(x, pos, norm_g, w_pack, wa2_pad, ba, tm):
    bk, lk, _ = x.shape
    n = bk * lk
    nt = lk // tm
    cos, sa, sb = _rope_tables(pos)
    tok = lambda w: pl.BlockSpec((tm, w), lambda i: (i, 0))
    tab = pl.BlockSpec((tm, LANES), lambda i: (i % nt, 0))
    full = lambda a: pl.BlockSpec(a.shape, lambda i: (0,) * a.ndim)
    hm = lambda h: pl.BlockSpec((1, h, tm, DH), lambda i: (i // nt, 0, i % nt, 0))
    g2 = norm_g.reshape(1, D_MODEL)
    ba2 = ba.reshape(1, H_GLA * DK)
    out_shape = (
        jax.ShapeDtypeStruct((bk, H_NSA, lk, DH), bf16),
        jax.ShapeDtypeStruct((n, KV_ROW), f32),
        jax.ShapeDtypeStruct((n, KV_ROW), f32),
        jax.ShapeDtypeStruct((n, KV_ROW), f32),
        jax.ShapeDtypeStruct((bk, KVH, lk, DH), bf16),
        jax.ShapeDtypeStruct((bk, KVH, lk, DH), bf16),
        jax.ShapeDtypeStruct((bk, KVH, lk, DH), bf16),
        jax.ShapeDtypeStruct((bk, KVH, lk, DH), bf16),
        jax.ShapeDtypeStruct((n, LANES), f32),
        jax.ShapeDtypeStruct((n, NSA_WIDTH), f32),
        jax.ShapeDtypeStruct((n, H_GLA * DK), f32),
        jax.ShapeDtypeStruct((n, H_GLA * DK), f32),
        jax.ShapeDtypeStruct((n, H_GLA * DV), f32),
        jax.ShapeDtypeStruct((n, H_GLA * DK), f32),
        jax.ShapeDtypeStruct((n, GLA_WIDTH), f32),
    )
    out_specs = (hm(H_NSA), tok(KV_ROW), tok(KV_ROW), tok(KV_ROW), hm(KVH), hm(KVH), hm(KVH), hm(KVH),
                 tok(LANES), tok(NSA_WIDTH), tok(H_GLA * DK), tok(H_GLA * DK), tok(H_GLA * DV),
                 tok(H_GLA * DK), tok(GLA_WIDTH))
    return pl.pallas_call(
        _inproj_kernel,
        grid=(n // tm,),
        in_specs=[tok(D_MODEL), full(g2), full(w_pack), tab, tab, tab, full(wa2_pad), full(ba2)],
        out_specs=out_specs,
        out_shape=out_shape,
        compiler_params=_cparams(("parallel",)),
        name="inproj",
    )(x.reshape(n, D_MODEL), g2, w_pack, cos, sa, sb, wa2_pad, ba2)


PT_Q, PT_KVC, PT_KVS, PT_KVW, PT_GATE = 0, 512, 768, 1024, 1280
PT_ROWS = PT_GATE + 32
PK_ZN, PK_GQ, PK_GK, PK_GV, PK_ZG, PK_MISC = 0, 512, 768, 1024, 1536, 2048
PK_COLS = PK_MISC + LANES


def _inproj_prompt_kernel(x_ref, g_ref, wt_ref, w_ref, cos_ref, sin_ref, wa2_ref, ba_ref,
                          qt_ref, kvct_ref, kvst_ref, kvwt_ref, ks_ref, kw_ref, vst_ref, vwt_ref,
                          gatet_ref, zn_ref, gq_ref, gk_ref, gv_ref, la_ref, zg_ref):
    x = x_ref[...]
    ms = jnp.mean(x * x, axis=-1, keepdims=True)
    hn = (x * lax.rsqrt(ms + EPS) * g_ref[...]).astype(bf16)
    cos_t, sin_t = cos_ref[...], sin_ref[...]

    all_t = _dot_nt(wt_ref[...], hn)

    def proj_t(off, rows):
        return all_t[off:off + rows]

    def rope_rows(v):
        x1, x2 = v[:ROT_HALF], v[ROT_HALF:2 * ROT_HALF]
        return jnp.concatenate([x1 * cos_t - x2 * sin_t, x2 * cos_t + x1 * sin_t, v[2 * ROT_HALF:]], axis=0)

    q_t = proj_t(PT_Q, NSA_WIDTH)
    for h in range(H_NSA):
        qt_ref[0, h] = (rope_rows(q_t[h * DH:(h + 1) * DH]) * (DH ** -0.5 * LOG2E)).astype(bf16)

    kvct_ref[0] = proj_t(PT_KVC, KV_ROW)

    for off, kvt_ref, k_ref, vt_ref in ((PT_KVS, kvst_ref, ks_ref, vst_ref), (PT_KVW, kvwt_ref, kw_ref, vwt_ref)):
        t = proj_t(off, KV_ROW)
        k_t = jnp.concatenate([rope_rows(t[h * DH:(h + 1) * DH]) for h in range(KVH)], axis=0)
        kvt_ref[0, :LANES] = k_t
        kvt_ref[0, LANES:] = t[LANES:]
        k_tok = k_t.T
        for h in range(KVH):
            k_ref[0, h] = k_tok[:, h * DH:(h + 1) * DH].astype(bf16)
            vt_ref[0, h] = t[LANES + h * DH:LANES + (h + 1) * DH].astype(bf16)

    gatet_ref[0] = _sigmoid(proj_t(PT_GATE, PT_ROWS - PT_GATE))

    def proj(off, width):
        return _dot(hn, w_ref[:, off:off + width])

    zn_ref[...] = _silu(proj(PK_ZN, NSA_WIDTH))
    gq_ref[...] = proj(PK_GQ, H_GLA * DK) * (DK ** -0.5)
    gk_ref[...] = proj(PK_GK, H_GLA * DK)
    gv_ref[...] = proj(PK_GV, H_GLA * DV)
    zg_ref[...] = _silu(proj(PK_ZG, GLA_WIDTH))
    misc = proj(PK_MISC, LANES)
    xa = _dot(misc.astype(bf16), wa2_ref[...]) + ba_ref[...]
    la_ref[...] = _log_sigmoid(xa) / GLA_TAU


def _pack_w_in_prompt(w_in):
    o = IN_OFFSETS
    seg = lambda i: w_in[:, o[i]:o[i + 1]]
    gate_t = jnp.zeros((PT_ROWS - PT_GATE, D_MODEL), w_in.dtype).at[:3 * H_NSA].set(seg(4).T)
    w_t = jnp.concatenate([seg(0).T, seg(1).T, seg(2).T, seg(3).T, gate_t], axis=0).astype(bf16)
    misc = jnp.zeros((D_MODEL, LANES), w_in.dtype).at[:, MISC_GLR:MISC_GLR + GLA_LR].set(seg(9))
    w_tok = jnp.concatenate([seg(5), seg(6), seg(7), seg(8), seg(10), misc], axis=1).astype(bf16)
    return w_t, w_tok


def _inproj_prompt(x, norm_g, w_t, w_tok, wa2_pad, ba, tm):
    b, lq, _ = x.shape
    n = b * lq
    nt = lq // tm
    inv = ROPE_THETA ** (-(jnp.arange(ROT_HALF, dtype=f32) / ROT_HALF))
    ang = inv[:, None] * jnp.arange(lq).astype(f32)[None, :]
    cos_t, sin_t = jnp.cos(ang), jnp.sin(ang)
    tok = lambda w: pl.BlockSpec((tm, w), lambda i: (i, 0))
    tab = pl.BlockSpec((ROT_HALF, tm), lambda i: (0, i % nt))
    full = lambda a: pl.BlockSpec(a.shape, lambda i: (0,) * a.ndim)
    feat = lambda r: pl.BlockSpec((1, r, tm), lambda i: (i // nt, 0, i % nt))
    headf = lambda h: pl.BlockSpec((1, h, DH, tm), lambda i: (i // nt, 0, 0, i % nt))
    headt = lambda h: pl.BlockSpec((1, h, tm, DH), lambda i: (i // nt, 0, i % nt, 0))
    g2 = norm_g.reshape(1, D_MODEL)
    ba2 = ba.reshape(1, H_GLA * DK)
    sds = jax.ShapeDtypeStruct
    out_shape = (
        sds((b, H_NSA, DH, lq), bf16),
        sds((b, KV_ROW, lq), f32), sds((b, KV_ROW, lq), f32), sds((b, KV_ROW, lq), f32),
        sds((b, KVH, lq, DH), bf16), sds((b, KVH, lq, DH), bf16),
        sds((b, KVH, DH, lq), bf16), sds((b, KVH, DH, lq), bf16),
        sds((b, PT_ROWS - PT_GATE, lq), f32),
        sds((n, NSA_WIDTH), f32), sds((n, H_GLA * DK), f32), sds((n, H_GLA * DK), f32),
        sds((n, H_GLA * DV), f32), sds((n, H_GLA * DK), f32), sds((n, GLA_WIDTH), f32),
    )
    out_specs = (headf(H_NSA), feat(KV_ROW), feat(KV_ROW), feat(KV_ROW), headt(KVH), headt(KVH),
                 headf(KVH), headf(KVH), feat(PT_ROWS - PT_GATE),
                 tok(NSA_WIDTH), tok(H_GLA * DK), tok(H_GLA * DK), tok(H_GLA * DV), tok(H_GLA * DK),
                 tok(GLA_WIDTH))
    return pl.pallas_call(
        _inproj_prompt_kernel,
        grid=(n // tm,),
        in_specs=[tok(D_MODEL), full(g2), full(w_t), full(w_tok), tab, tab, full(wa2_pad), full(ba2)],
        out_specs=out_specs,
        out_shape=out_shape,
        compiler_params=_cparams(("parallel",)),
        name="inproj_prompt",
    )(x.reshape(n, D_MODEL), g2, w_t, w_tok, cos_t, sin_t, wa2_pad, ba2)


HID_W = 2 * KVH * CMP_HID
CHUNKS_PER_PAGE = PAGE_SIZE // CMP_STRIDE
PAGE_GROUP = 8


def _cmp_w1_pairs(cmp_w1):
    ratio = CMP_LEN // CMP_STRIDE
    w1r = cmp_w1.reshape(2, ratio, CMP_STRIDE // 2, 2, DH, CMP_HID)
    eye = jnp.eye(KVH, dtype=cmp_w1.dtype)
    big = jnp.einsum('crpjdh,kK->cpjKdrkh', w1r, eye)
    return big.reshape(2, CMP_STRIDE // 2, 2 * KVH * DH, ratio * KVH * CMP_HID).astype(bf16)


def _cmp_w2_heads(cmp_w2):
    eye = jnp.eye(KVH, dtype=cmp_w2.dtype)
    return jnp.einsum('chd,kK->ckhKd', cmp_w2, eye).reshape(2, KVH * CMP_HID, KVH * DH).astype(bf16)


def _row_gather_matrix():
    r = np.arange(PAGE_SIZE)
    src = (r % CHUNKS_PER_PAGE) * CMP_STRIDE + r // CHUNKS_PER_PAGE
    return jnp.asarray((r[None, :] == src[:, None]).astype(np.float32), dtype=bf16)


def _compress_pages(page_at, n_pages, perm, rows_scr, w1_ref, bias_ref, w2_ref):
    cpp = CHUNKS_PER_PAGE
    kd = KVH * DH

    def to_rows(g, carry):
        for u in range(PAGE_GROUP):
            p = g * PAGE_GROUP + u
            r0 = pl.multiple_of(p * cpp, cpp)
            t = _dot_nt(perm, page_at(p).astype(bf16))
            for c in range(2):
                for s in range(CMP_STRIDE):
                    rows_scr[c, s, pl.ds(r0, cpp), :] = t[s * cpp:(s + 1) * cpp, c * kd:(c + 1) * kd]
        return carry

    lax.fori_loop(0, n_pages // PAGE_GROUP, to_rows, 0)

    nc = n_pages * cpp
    kv = []
    for c in range(2):
        lhs = jnp.concatenate([rows_scr[c, s] for s in range(CMP_STRIDE)], axis=-1).astype(bf16)
        part = _dot(lhs, w1_ref[c].reshape(CMP_STRIDE * kd, w1_ref.shape[-1]))
        half = part.shape[1] // 2
        pre = part[:, :half] + pltpu.roll(part[:, half:], nc - 1, 0)
        hid = _silu(pre + bias_ref[:, c * half:(c + 1) * half])
        kv.append(_dot(hid.astype(bf16), w2_ref[c]))
    return kv


def _cmp_bias_kernel(pe_ref, w1_ref, b1_ref, o_ref):
    for c in range(2):
        o_ref[c] = _dot(pe_ref[c], w1_ref[c]) + b1_ref[c]


def _cmp_bias(cmp_pe, cmp_w1, cmp_b1):
    pe = jnp.broadcast_to(cmp_pe.reshape(2, 1, CMP_LEN * DH), (2, 8, CMP_LEN * DH))
    b1 = jnp.broadcast_to(cmp_b1.reshape(2, 1, CMP_HID), (2, 8, CMP_HID))
    out = pl.pallas_call(
        _cmp_bias_kernel,
        out_shape=jax.ShapeDtypeStruct((2, 8, CMP_HID), f32),
        name="cmp_bias",
    )(pe, cmp_w1, b1)
    row = out[:, 0, :]
    return jnp.broadcast_to(row[:, None, :], (2, KVH, CMP_HID)).reshape(1, HID_W)


def _masked_softmax(s, mask, axis, exp=jnp.exp):
    s = jnp.where(mask, s, NEG)
    e = exp(s - jnp.max(s, axis=axis, keepdims=True))
    return jnp.where(mask, e * (1.0 / jnp.sum(e, axis=axis, keepdims=True)), 0.0)


def _split_bf16(x):
    hi = x.astype(bf16)
    return hi, (x - hi.astype(f32)).astype(bf16)


def _topk_mask(score, k, axis):
    n = score.shape[axis]
    idx = lax.broadcasted_iota(jnp.int32, score.shape, axis)
    sel = jnp.zeros(score.shape, f32)
    for _ in range(k):
        m = jnp.max(score, axis=axis, keepdims=True)
        first = jnp.min(jnp.where(score == m, idx, n), axis=axis, keepdims=True)
        pick = idx == first
        sel = jnp.where(pick, 1.0, sel)
        score = jnp.where(pick, NEG, score)
    return sel


def _topk_rows_by_rank(score, k):
    r, n = score.shape
    cols = jnp.concatenate([score, jnp.zeros((n - r, n), f32)], axis=0).T
    earlier = jnp.where(lax.broadcasted_iota(jnp.int32, (n, n), 0) < lax.broadcasted_iota(jnp.int32, (n, n), 1),
                        1.0, 0.0)
    rows = []
    for c in range(r):
        col, row = cols[:, c:c + 1], score[c:c + 1, :]
        before = jnp.where(col > row, 1.0, jnp.where(col == row, earlier, 0.0))
        rows.append(jnp.where(jnp.sum(before, axis=0, keepdims=True) < k, 1.0, 0.0))
    return jnp.concatenate(rows, axis=0)


def _cover_t(n_slc, n_chunk):
    start = np.arange(n_chunk)[None, :] * CMP_STRIDE
    j = np.arange(n_slc)[:, None]
    cov = (start < (j + 1) * SLC_LEN) & (start + CMP_LEN > j * SLC_LEN)
    return jnp.asarray(cov.astype(np.float32), dtype=bf16)


def _cmp_attn_kernel(kvct_ref, perm_ref, w1_ref, bias_ref, w2_ref, cos_ref, sa_ref, sb_ref, qt_ref, gatet_ref,
                     cov_ref, ot_ref, selt_ref, rows_scr, kc_ref, vct_ref, *, tq):
    t = pl.program_id(1)

    @pl.when(t == 0)
    def _():
        page_at = lambda p: kvct_ref[0, :, pl.ds(pl.multiple_of(p * PAGE_SIZE, PAGE_SIZE), PAGE_SIZE)]
        k, v = _compress_pages(page_at, kvct_ref.shape[2] // PAGE_SIZE, perm_ref[...], rows_scr,
                               w1_ref, bias_ref, w2_ref)
        k = _rope128(k, cos_ref[...], sa_ref[...], sb_ref[...])
        v_t = v.T
        for h in range(KVH):
            kc_ref[h] = k[:, h * DH:(h + 1) * DH].astype(bf16)
            vct_ref[h] = v_t[h * DH:(h + 1) * DH].astype(bf16)

    nc = kc_ref.shape[1]
    nb = cov_ref.shape[0]
    qpos = t * tq + lax.broadcasted_iota(jnp.int32, (1, tq), 1)
    end_pos = lax.broadcasted_iota(jnp.int32, (nc, 1), 0) * CMP_STRIDE + (CMP_LEN - 1)
    mask_t = end_pos <= qpos
    gate_t = gatet_ref[0]
    cov = cov_ref[...]

    jblk = lax.broadcasted_iota(jnp.int32, (nb, tq), 0)
    qblk = qpos // SLC_LEN
    valid = jblk <= qblk
    forced = (jblk == 0) | (jblk == qblk) | (jblk == qblk - 1)

    heads = range(H_NSA)
    s_t = [_dot(kc_ref[h // G_NSA], qt_ref[0, h]) for h in heads]
    p_t = [_masked_softmax(s, mask_t, axis=0, exp=jnp.exp2) for s in s_t]
    for h in heads:
        ot_ref[0, h * DH:(h + 1) * DH] = _dot(vct_ref[h // G_NSA], p_t[h].astype(bf16)) * gate_t[h:h + 1]
    scores = []
    for kh in range(KVH):
        psum = functools.reduce(lambda a, b: a + b, p_t[kh * G_NSA:(kh + 1) * G_NSA])
        hi, lo = _split_bf16(psum)
        imp = _dot(cov, hi) + _dot(cov, lo)
        scores.append(jnp.where(valid, jnp.where(forced, FORCE_SCORE, imp), -1.0))
    picked = _topk_mask(jnp.concatenate(scores, axis=1), SLC_TOP, axis=0)
    for kh in range(KVH):
        sel_t = jnp.where(valid & (picked[:, kh * tq:(kh + 1) * tq] > 0.5), 0.0, NEG)
        if nb < LANES:
            sel_t = jnp.concatenate([sel_t, jnp.full((LANES - nb, tq), NEG, f32)], axis=0)
        selt_ref[0, kh] = sel_t


def _cmp_attn(kvc_t, w1_pairs, bias, w2_heads, q_t, gate_t, tq):
    b, _, lq = kvc_t.shape
    assert lq % (PAGE_SIZE * PAGE_GROUP) == 0
    nc = lq // CMP_STRIDE
    n_slc = lq // SLC_LEN
    perm = _row_gather_matrix()
    end_pos = jnp.arange(nc) * CMP_STRIDE + (CMP_LEN - 1)
    cos, sa, sb = _rope_tables(end_pos)
    cov = _cover_t(n_slc, nc)
    full = lambda a: pl.BlockSpec(a.shape, lambda i, t: (0,) * a.ndim)
    return pl.pallas_call(
        functools.partial(_cmp_attn_kernel, tq=tq),
        grid=(b, lq // tq),
        in_specs=[pl.BlockSpec((1, KV_ROW, lq), lambda i, t: (i, 0, 0)),
                  full(perm), full(w1_pairs), full(bias), full(w2_heads), full(cos), full(sa), full(sb),
                  pl.BlockSpec((1, H_NSA, DH, tq), lambda i, t: (i, 0, 0, t)),
                  pl.BlockSpec((1, gate_t.shape[1], tq), lambda i, t: (i, 0, t)),
                  full(cov)],
        out_specs=(pl.BlockSpec((1, NSA_WIDTH, tq), lambda i, t: (i, 0, t)),
                   pl.BlockSpec((1, KVH, LANES, tq), lambda i, t: (i, 0, 0, t))),
        out_shape=(jax.ShapeDtypeStruct((b, NSA_WIDTH, lq), f32),
                   jax.ShapeDtypeStruct((b, KVH, LANES, lq), f32)),
        scratch_shapes=[pltpu.VMEM((2, CMP_STRIDE, nc, KVH * DH), f32),
                        pltpu.VMEM((KVH, nc, DH), bf16), pltpu.VMEM((KVH, DH, nc), bf16)],
        compiler_params=_cparams(("parallel", "arbitrary")),
        name="cmp_attn",
    )(kvc_t, perm, w1_pairs, bias, w2_heads, cos, sa, sb, q_t, gate_t, cov)


def _block_expand(n_slc_pad, n_keys):
    e = (np.arange(n_keys)[None, :] // SLC_LEN) == np.arange(n_slc_pad)[:, None]
    return jnp.asarray(e.astype(np.float32), dtype=bf16)


def _mask_bias_t(allowed):
    bias = jnp.where(allowed, 0.0, NEG)
    return jnp.concatenate([bias] * G_NSA, axis=1)


def _sublane_fold(x, op):
    return functools.reduce(op, [x[r:r + 8] for r in range(0, x.shape[0], 8)])


def _slc_win_kernel(qt_ref, ks_ref, vst_ref, kw_ref, vwt_ref, selt_ref, gatet_ref, ocmpt_ref, zn_ref,
                    o_ref, s_a, s_b, m_a, m_b, ws_a, ws_b, wm_a, wm_b, acc_ref, wacc_ref, *, tq, tk, nt):
    t = pl.program_id(2)

    @pl.when(t == 0)
    def _():
        m_b[...] = jnp.full(m_b.shape, NEG, f32)
        ws_b[...] = jnp.zeros(ws_b.shape, f32)
        wm_b[...] = jnp.zeros(wm_b.shape, f32)

    step = functools.partial(_slc_win_step, qt_ref, ks_ref, vst_ref, kw_ref, vwt_ref, selt_ref, gatet_ref,
                             ocmpt_ref, zn_ref, o_ref, acc_ref, wacc_ref, tq=tq, tk=tk, nt=nt)

    @pl.when(t % 2 == 0)
    def _():
        step(s_a, s_b, m_a, m_b, ws_a, ws_b, wm_a, wm_b)

    @pl.when(t % 2 == 1)
    def _():
        step(s_b, s_a, m_b, m_a, ws_b, ws_a, wm_b, wm_a)


ONES_ROWS = 16


def _slc_win_step(qt_ref, ks_ref, vst_ref, kw_ref, vwt_ref, selt_ref, gatet_ref, ocmpt_ref, zn_ref, o_ref,
                  acc_ref, wacc_ref, s_cur, s_prev, m_cur, m_last, ws_cur, ws_prev, wm_cur, wm_last,
                  *, tq, tk, nt):
    kh = pl.program_id(1)
    t = pl.program_id(2)
    lq = ks_ref.shape[2]
    band = WINDOW + tq
    blocks_per_tile = tk // SLC_LEN
    key_tiles = lambda tile: (tile * tq + tq + tk - 1) // tk
    n_apply = jnp.where(t >= 1, key_tiles(t - 1), 0)
    n_plain = jnp.where(t < nt, (t * tq) // tk, 0)
    q0 = t * tq
    qpos = q0 + lax.broadcasted_iota(jnp.int32, (1, tq), 1)
    q_t = jnp.concatenate([qt_ref[0, g] for g in range(G_NSA)], axis=1)
    ones = jnp.ones((ONES_ROWS, tk), bf16)

    m_cur[...] = jnp.full(m_cur.shape, NEG, f32)
    m_prev = jnp.max(m_last[...], axis=0, keepdims=True)
    acc_ref[...] = jnp.zeros(acc_ref.shape, f32)

    def score(kt, causal):
        k0 = pl.multiple_of(kt * tk, tk)
        j0 = pl.multiple_of(kt * blocks_per_tile, blocks_per_tile)
        sel = selt_ref[0, 0, pl.ds(j0, blocks_per_tile), :]
        bias = jnp.concatenate([jnp.broadcast_to(sel[j:j + 1], (SLC_LEN, tq)) for j in range(blocks_per_tile)],
                               axis=0)
        if causal:
            kpos = k0 + lax.broadcasted_iota(jnp.int32, (tk, 1), 0)
            bias = jnp.where(kpos <= qpos, bias, NEG)
        s = _dot(ks_ref[0, 0, pl.ds(k0, tk), :], q_t) + jnp.concatenate([bias] * G_NSA, axis=1)
        s_cur[pl.ds(k0, tk), :] = s
        m_cur[...] = jnp.maximum(m_cur[...], _sublane_fold(s, jnp.maximum))

    def apply(kt):
        k0 = pl.multiple_of(kt * tk, tk)
        p = jnp.exp2(s_prev[pl.ds(k0, tk), :] - m_prev).astype(bf16)
        acc_ref[...] += _dot(jnp.concatenate([vst_ref[0, 0, :, pl.ds(k0, tk)], ones], axis=0), p)

    def both_body(kt, carry):
        score(kt, causal=False)
        apply(kt)
        return carry

    def apply_body(kt, carry):
        apply(kt)
        return carry

    def both2_body(i, carry):
        for u in range(2):
            score(2 * i + u, causal=False)
            apply(2 * i + u)
        return carry

    n_both = jnp.minimum(n_plain, n_apply)
    lax.fori_loop(0, n_both // 2, both2_body, 0)
    lax.fori_loop(n_both // 2 * 2, n_both, both_body, 0)
    lax.fori_loop(n_both, n_apply, apply_body, 0)
    score(jnp.minimum((t * tq) // tk, lq // tk - 1), causal=True)

    w0 = pl.multiple_of(jnp.clip(q0 - WINDOW, 0, lq - band), tq)
    w0_prev = pl.multiple_of(jnp.clip(q0 - tq - WINDOW, 0, lq - band), tq)
    wm_prev = jnp.max(wm_last[...], axis=0, keepdims=True)
    ones_w = jnp.ones((ONES_ROWS, tq), bf16)
    wmax, wacc = None, None
    for j in range(band // tq):
        rows = pl.ds(j * tq, tq)
        kpos = w0 + j * tq + lax.broadcasted_iota(jnp.int32, (tq, 1), 0)
        s = (_dot(kw_ref[0, 0, pl.ds(pl.multiple_of(w0 + j * tq, tq), tq), :], q_t)
             + _mask_bias_t((kpos <= qpos) & (kpos > qpos - WINDOW)))
        ws_cur[rows, :] = s
        fold = _sublane_fold(s, jnp.maximum)
        wmax = fold if wmax is None else jnp.maximum(wmax, fold)
        p = jnp.exp2(ws_prev[rows, :] - wm_prev).astype(bf16)
        v_ext = jnp.concatenate([vwt_ref[0, 0, :, pl.ds(pl.multiple_of(w0_prev + j * tq, tq), tq)], ones_w], axis=0)
        part = _dot(v_ext, p)
        wacc = part if wacc is None else wacc + part
    wm_cur[...] = wmax
    wacc_ref[...] = wacc

    @pl.when(t >= 1)
    def _():
        o_slc = acc_ref[:DH] / acc_ref[DH:DH + 1]
        o_win = wacc_ref[:DH] / wacc_ref[DH:DH + 1]
        gate_t = gatet_ref[0]
        row = lambda c, g: jnp.where(kh == 0, gate_t[c * H_NSA + g:c * H_NSA + g + 1],
                                     gate_t[c * H_NSA + G_NSA + g:c * H_NSA + G_NSA + g + 1])
        g_slc = jnp.concatenate([row(1, g) for g in range(G_NSA)], axis=1)
        g_win = jnp.concatenate([row(2, g) for g in range(G_NSA)], axis=1)
        o_t = g_slc * o_slc + g_win * o_win
        o_heads = jnp.concatenate([o_t[:, g * tq:(g + 1) * tq] for g in range(G_NSA)], axis=0)
        o_ref[0] = ((ocmpt_ref[0] + o_heads).T * zn_ref[...]).astype(o_ref.dtype)


def _slc_win(q_t, ks, vs_t, kw, vw_t, sel_t, gate_t, o_cmp_t, zn, tq, tk):
    b, _, _, lq = q_t.shape
    assert lq >= WINDOW + tq and lq % tk == 0 and tk % (8 * SLC_LEN) == 0
    gw = G_NSA * DH
    k_spec = pl.BlockSpec((1, 1, lq, DH), lambda i, k, t: (i, k, 0, 0))
    vt_spec = pl.BlockSpec((1, 1, DH, lq), lambda i, k, t: (i, k, 0, 0))
    lanes = G_NSA * tq
    band = WINDOW + tq
    nt = lq // tq
    scored = lambda t: jnp.minimum(t, nt - 1)
    drained = lambda t: jnp.maximum(t - 1, 0)
    return pl.pallas_call(
        functools.partial(_slc_win_kernel, tq=tq, tk=tk, nt=nt),
        grid=(b, KVH, nt + 1),
        in_specs=[pl.BlockSpec((1, G_NSA, DH, tq), lambda i, k, t: (i, k, 0, scored(t))),
                  k_spec, vt_spec, k_spec, vt_spec,
                  pl.BlockSpec((1, 1, LANES, tq), lambda i, k, t: (i, k, 0, scored(t))),
                  pl.BlockSpec((1, gate_t.shape[1], tq), lambda i, k, t: (i, 0, drained(t))),
                  pl.BlockSpec((1, gw, tq), lambda i, k, t: (i, k, drained(t))),
                  pl.BlockSpec((tq, gw), lambda i, k, t: (i * nt + drained(t), k))],
        out_specs=pl.BlockSpec((1, tq, gw), lambda i, k, t: (i, drained(t), k)),
        out_shape=jax.ShapeDtypeStruct((b, lq, NSA_WIDTH), bf16),
        scratch_shapes=[pltpu.VMEM((lq, lanes), f32), pltpu.VMEM((lq, lanes), f32),
                        pltpu.VMEM((8, lanes), f32), pltpu.VMEM((8, lanes), f32),
                        pltpu.VMEM((band, lanes), f32), pltpu.VMEM((band, lanes), f32),
                        pltpu.VMEM((8, lanes), f32), pltpu.VMEM((8, lanes), f32),
                        pltpu.VMEM((DH + ONES_ROWS, lanes), f32),
                        pltpu.VMEM((DH + ONES_ROWS, lanes), f32)],
        compiler_params=_cparams(("parallel", "parallel", "arbitrary")),
        name="slc_win",
    )(q_t, ks, vs_t, kw, vw_t, sel_t, gate_t, o_cmp_t, zn)


def _rms_gain(o, gain):
    return o * lax.rsqrt(jnp.mean(o * o, axis=-1, keepdims=True) + EPS) * gain


def _gla_kernel(q_ref, k_ref, v_ref, la_ref, z_ref, gain_ref, s0_ref, y_ref, st_ref, s_scr, *, tl, chunk):
    t = pl.program_id(1)

    @pl.when(t == 0)
    def _():
        s_scr[...] = s0_ref[0]

    c = chunk
    pairs = s_scr.shape[0]
    row = lax.broadcasted_iota(jnp.int32, (c, c), 0)
    col = lax.broadcasted_iota(jnp.int32, (c, c), 1)
    causal = col <= row
    tril = jnp.where(causal, 1.0, 0.0).astype(bf16)
    lane = lax.broadcasted_iota(jnp.int32, (1, LANES), 1)
    head_mask = [jnp.where(lane < DK, 1.0, 0.0), jnp.where(lane >= DK, 1.0, 0.0)]
    gain = gain_ref[...]

    units = [(ci, pr) for ci in range(tl // c) for pr in range(pairs)]
    rows_of = lambda ci: slice(ci * c, (ci + 1) * c)
    grp_of = lambda pr: slice(pr * LANES, (pr + 1) * LANES)
    bcs = []
    for ci, pr in units:
        hi, lo = _split_bf16(la_ref[rows_of(ci), grp_of(pr)])
        bcs.append(_dot(tril, hi) + _dot(tril, lo))
    scaled = []
    for (ci, pr), bc in zip(units, bcs):
        q, k = q_ref[rows_of(ci), grp_of(pr)], k_ref[rows_of(ci), grp_of(pr)]
        ref_row = bc[c // 2 - 1:c // 2, :]
        b_last = bc[c - 1:c, :]
        scaled.append((q * jnp.exp(bc - ref_row), (k * jnp.exp(ref_row - bc)).astype(bf16),
                       q * jnp.exp(bc), k * jnp.exp(b_last - bc), jnp.exp(b_last)))
    attn = []
    for (ci, pr), (q_a, k_a, _, _, _) in zip(units, scaled):
        attn.append([jnp.where(causal, _dot_nt((q_a * head_mask[h]).astype(bf16), k_a), 0.0).astype(bf16)
                     for h in range(2)])
    chunks = []
    for (ci, pr), (_, _, q_s, k_s, decay), a in zip(units, scaled, attn):
        rows = rows_of(ci)
        o_intra, q_inter, add = [], [], None
        for h in range(2):
            vh = v_ref[rows, (2 * pr + h) * DV:(2 * pr + h + 1) * DV].astype(bf16)
            o_intra.append(_dot(a[h], vh))
            q_inter.append((q_s * head_mask[h]).astype(bf16))
            u = _dot_tn(vh, (k_s * head_mask[h]).astype(bf16))
            add = u if add is None else add + u
        chunks.append((rows, pr, decay, add, o_intra, q_inter))

    st = [s_scr[pr] for pr in range(pairs)]
    for rows, pr, decay, add, o_intra, q_inter in chunks:
        st_b = st[pr].astype(bf16)
        for h in range(2):
            cols = slice((2 * pr + h) * DV, (2 * pr + h + 1) * DV)
            o = o_intra[h] + _dot_nt(q_inter[h], st_b)
            y_ref[rows, cols] = (_rms_gain(o, gain) * z_ref[rows, cols]).astype(y_ref.dtype)
        st[pr] = st[pr] * decay + add
    for pr in range(pairs):
        s_scr[pr] = st[pr]
        st_ref[0, pr] = st[pr]


def _gla(gq, gk, gv, la, zg, gain, state0, b, tl, chunk):
    n = gq.shape[0]
    nt = n // b // tl
    pairs = H_GLA // 2
    qk = pl.BlockSpec((tl, H_GLA * DK), lambda i, t: (i * nt + t, 0))
    vz = pl.BlockSpec((tl, GLA_WIDTH), lambda i, t: (i * nt + t, 0))
    st_spec = pl.BlockSpec((1, pairs, DV, LANES), lambda i, t: (i, 0, 0, 0))
    gain2 = gain.reshape(1, DV)
    s0 = state0.reshape(b, pairs, 2, DK, DV).transpose(0, 1, 4, 2, 3).reshape(b, pairs, DV, LANES)
    y, st = pl.pallas_call(
        functools.partial(_gla_kernel, tl=tl, chunk=chunk),
        grid=(b, nt),
        in_specs=[qk, qk, vz, qk, vz, pl.BlockSpec((1, DV), lambda i, t: (0, 0)), st_spec],
        out_specs=(vz, st_spec),
        out_shape=(jax.ShapeDtypeStruct((n, GLA_WIDTH), bf16),
                   jax.ShapeDtypeStruct((b, pairs, DV, LANES), f32)),
        scratch_shapes=[pltpu.VMEM((pairs, DV, LANES), f32)],
        compiler_params=_cparams(("parallel", "arbitrary")),
        name="gla",
    )(gq, gk, gv, la, zg, gain2, s0)
    state = st.reshape(b, pairs, DV, 2, DK).transpose(0, 1, 3, 4, 2).reshape(b, H_GLA, DK, DV)
    return y, state


def _outproj_kernel(x_ref, yn_ref, yg_ref, w_ref, g_ref, y_ref):
    mix = _dot(yn_ref[...].astype(bf16), w_ref[:NSA_WIDTH, :]) + _dot(yg_ref[...].astype(bf16), w_ref[NSA_WIDTH:, :])
    y_ref[...] = _rms_gain(x_ref[...] + mix, g_ref[...])


def _outproj(x, y_nsa, y_gla, w_out, gain, tm):
    n = x.shape[0]
    tok = lambda w: pl.BlockSpec((tm, w), lambda i: (i, 0))
    g2 = gain.reshape(1, D_MODEL)
    return pl.pallas_call(
        _outproj_kernel,
        grid=(n // tm,),
        in_specs=[tok(D_MODEL), tok(NSA_WIDTH), tok(GLA_WIDTH),
                  pl.BlockSpec(w_out.shape, lambda i: (0, 0)), pl.BlockSpec(g2.shape, lambda i: (0, 0))],
        out_specs=tok(D_MODEL),
        out_shape=jax.ShapeDtypeStruct((n, D_MODEL), f32),
        compiler_params=_cparams(("parallel",)),
        name="outproj",
    )(x, y_nsa, y_gla, w_out, g2)


def _prompt_path(x, wts, w_t, w_tok):
    (norm_g, _, wa2_pad, ba, w1_pairs, cmp_bias, w2_heads, gla_gain, w_out, out_gain) = wts
    b, lq, _ = x.shape
    (q_t, kvc_t, kvs_t, kvw_t, ks, kw, vs_t, vw_t, gate_t, zn, gq, gk, gv, la, zg) = _inproj_prompt(
        x, norm_g, w_t, w_tok, wa2_pad, ba, tm=512)
    o_cmp_t, sel_t = _cmp_attn(kvc_t, w1_pairs, cmp_bias, w2_heads, q_t, gate_t, tq=512)
    y_nsa = _slc_win(q_t, ks, vs_t, kw, vw_t, sel_t, gate_t, o_cmp_t, zn, tq=256, tk=512)
    y_gla, state = _gla(gq, gk, gv, la, zg, gla_gain, jnp.zeros((b, H_GLA, DK, DV), f32), b, tl=512,
                        chunk=GLA_CHUNK)
    y = _outproj(x.reshape(b * lq, D_MODEL), y_nsa.reshape(b * lq, NSA_WIDTH), y_gla, w_out, out_gain, tm=1024)
    kv5 = lambda a: a.reshape(b, 2, KVH, DH, -1).transpose(0, 4, 1, 2, 3)
    wlen = min(WINDOW, lq)
    return (y.reshape(b, lq, D_MODEL), kv5(kvc_t), kv5(kvs_t), kv5(kvw_t[:, :, lq - wlen:]), state)


def _pages_native(cache):
    n_pool = cache.shape[0]
    return cache.transpose(0, 2, 3, 4, 1).reshape(n_pool, 2, KVH * DH, cache.shape[1])


def _page_gather(make_copies):
    i = pl.program_id(0)
    slot = i % 2

    @pl.when(i == 0)
    def _():
        for c in make_copies(0, 0):
            c.start()

    @pl.when(i + 1 < pl.num_programs(0))
    def _():
        for c in make_copies(i + 1, 1 - slot):
            c.start()

    for c in make_copies(i, slot):
        c.wait()
    return slot


def _group_sum_matrix(n_tok):
    r = np.arange(KVH * n_tok)[:, None]
    c = np.arange(H_NSA * n_tok)[None, :]
    m = ((c // (G_NSA * n_tok)) == (r // n_tok)) & ((c % n_tok) == (r % n_tok))
    return m.astype(np.float32)


def _cmp_attn_sample_kernel(pt_ref, cache_ref, perm_ref, w1_ref, bias_ref, w2_ref, cos_ref, sa_ref, sb_ref, q_ref,
                            gate_ref, cov_ref, gs_ref, gst_ref, o_ref, sel_ref, buf, rows_scr, sem,
                            *, n_pages, n_tok, past_len):
    def copies(seq, slot):
        return [pltpu.make_async_copy(cache_ref.at[pt_ref[seq, p]], buf.at[slot, p], sem.at[slot])
                for p in range(n_pages)]

    slot = _page_gather(copies)

    page_at = lambda p: buf[slot, p].reshape(2 * KVH * DH, PAGE_SIZE)
    k, v = _compress_pages(page_at, n_pages, perm_ref[...], rows_scr, w1_ref, bias_ref, w2_ref)
    k = _rope128(k, cos_ref[...], sa_ref[...], sb_ref[...])
    nc = n_pages * CHUNKS_PER_PAGE
    rows = H_NSA * n_tok
    qpos = past_len + lax.broadcasted_iota(jnp.int32, (rows, 1), 0) % n_tok
    end_c = lax.broadcasted_iota(jnp.int32, (1, nc), 1) * CMP_STRIDE + (CMP_LEN - 1)
    mask = end_c <= qpos
    p = _masked_softmax(_dot_nt(q_ref[0], k.astype(bf16)), mask, axis=-1)
    o_ref[0] = _dot(p.astype(bf16), v.astype(bf16)) * gate_ref[0][:, 0:1]
    hi, lo = _split_bf16(p)
    gs = gs_ref[...]
    ph, pl_ = _split_bf16(_dot(gs, hi) + _dot(gs, lo))
    imp = _dot(ph, cov_ref[...]) + _dot(pl_, cov_ref[...])
    n_blk = imp.shape[1]
    jblk = lax.broadcasted_iota(jnp.int32, imp.shape, 1)
    forced = (jblk == 0) | (jblk == n_blk - 1)
    sel = _topk_rows_by_rank(jnp.where(forced, FORCE_SCORE, imp), SLC_TOP - 1)
    sel_ref[0] = _dot(gst_ref[...], sel.astype(bf16)).astype(bf16)


def _cmp_attn_sample(cache_t, page_table, w1_pairs, bias, w2_heads, q_blk, gate_rows, n_tok):
    b, n_pages = page_table.shape
    past_len = n_pages * PAGE_SIZE
    nc = past_len // CMP_STRIDE
    rows = H_NSA * n_tok
    n_blk = past_len // SLC_LEN
    end_pos = jnp.arange(nc) * CMP_STRIDE + (CMP_LEN - 1)
    cos, sa, sb = _rope_tables(end_pos)
    cov = _cover_t(n_blk, nc).T
    gs = jnp.asarray(_group_sum_matrix(n_tok), dtype=bf16)
    gst = gs.T
    perm = _row_gather_matrix()
    full = lambda a: pl.BlockSpec(a.shape, lambda i, pt: (0,) * a.ndim)
    seq = lambda a: pl.BlockSpec((1,) + a.shape[1:], lambda i, pt: (i,) + (0,) * (a.ndim - 1))
    grid_spec = pltpu.PrefetchScalarGridSpec(
        num_scalar_prefetch=1,
        grid=(b,),
        in_specs=[pl.BlockSpec(memory_space=pl.ANY), full(perm), full(w1_pairs), full(bias), full(w2_heads),
                  full(cos), full(sa), full(sb), seq(q_blk), seq(gate_rows), full(cov), full(gs), full(gst)],
        out_specs=(pl.BlockSpec((1, rows, LANES), lambda i, pt: (i, 0, 0)),
                   pl.BlockSpec((1, rows, n_blk), lambda i, pt: (i, 0, 0))),
        scratch_shapes=[pltpu.VMEM((2, n_pages) + cache_t.shape[1:], f32),
                        pltpu.VMEM((2, CMP_STRIDE, nc, KVH * DH), f32),
                        pltpu.SemaphoreType.DMA((2,))],
    )
    return pl.pallas_call(
        functools.partial(_cmp_attn_sample_kernel, n_pages=n_pages, n_tok=n_tok, past_len=past_len),
        grid_spec=grid_spec,
        out_shape=(jax.ShapeDtypeStruct((b, rows, LANES), f32),
                   jax.ShapeDtypeStruct((b, rows, n_blk), bf16)),
        compiler_params=_cparams(("arbitrary",)),
        name="cmp_attn_sample",
    )(page_table, cache_t, perm, w1_pairs, bias, w2_heads, cos, sa, sb, q_blk, gate_rows, cov, gs, gst)


def _slc_win_sample_kernel(pt_ref, cache_ref, q_ref, sel_ref, e_ref, snew_ref, cw_ref, wnew_ref, wnewt_ref,
                           gate_ref, ocmp_ref, zn_ref, o_ref, wout_ref, buf, sem, *, n_pages, n_tok, win_off):
    def copies(seq, slot):
        return [pltpu.make_async_copy(cache_ref.at[pt_ref[seq, p], c],
                                      buf.at[slot, c, :, pl.ds(p * PAGE_SIZE, PAGE_SIZE)], sem.at[slot])
                for p in range(n_pages) for c in range(2)]

    slot = _page_gather(copies)
    rows = H_NSA * n_tok
    q = q_ref[0]
    tok = lax.broadcasted_iota(jnp.int32, (rows, 1), 0) % n_tok
    new_i = lax.broadcasted_iota(jnp.int32, (1, snew_ref.shape[1]), 1)
    new_ok = (new_i <= tok) & (new_i < n_tok)

    def attend(keys_t, vals_t, allowed, new_ref):
        s = jnp.where(allowed, _dot(q, keys_t), NEG)
        k_new = new_ref[0][:, :LANES].astype(bf16)
        v_new = new_ref[0][:, LANES:].astype(bf16)
        s_new = jnp.where(new_ok, _dot_nt(q, k_new), NEG)
        m = jnp.maximum(jnp.max(s, axis=-1, keepdims=True), jnp.max(s_new, axis=-1, keepdims=True))
        p = jnp.where(allowed, jnp.exp(s - m), 0.0)
        p_new = jnp.where(new_ok, jnp.exp(s_new - m), 0.0)
        l = jnp.sum(p, axis=-1, keepdims=True) + jnp.sum(p_new, axis=-1, keepdims=True)
        return (_dot_nt(p.astype(bf16), vals_t) + _dot(p_new.astype(bf16), v_new)) / l

    picked = _dot(sel_ref[0], e_ref[...]) > 0.5
    o_slc = attend(buf[slot, 0].astype(bf16), buf[slot, 1].astype(bf16), picked, snew_ref)

    wbuf = cw_ref.shape[-1]
    win_i = lax.broadcasted_iota(jnp.int32, (1, wbuf), 1)
    o_win = attend(cw_ref[0, 0].astype(bf16), cw_ref[0, 1].astype(bf16), win_i > tok + win_off, wnew_ref)

    lane = lax.broadcasted_iota(jnp.int32, (1, LANES), 1)
    for c in range(2):
        shifted = pltpu.roll(cw_ref[0, c], wbuf - n_tok, 1)
        tail = jnp.where(lane >= LANES - n_tok, wnewt_ref[0, c], shifted[:, wbuf - LANES:])
        wout_ref[0, c] = jnp.concatenate([shifted[:, :wbuf - LANES], tail], axis=-1)

    gate = gate_ref[0]
    o = ocmp_ref[0] + gate[:, 1:2] * o_slc + gate[:, 2:3] * o_win
    second_kvh = lax.broadcasted_iota(jnp.int32, (rows, 1), 0) >= G_NSA * n_tok
    o_ref[0] = jnp.where(second_kvh, o[:, DH:], o[:, :DH]) * zn_ref[0]


def _slc_win_sample(cache_t, page_table, q_blk, sel_rows, kvs_new, win_t, kvw_new, kvw_new_t, gate_rows,
                    o_cmp, zn_rows, n_tok):
    b, n_pages = page_table.shape
    rows = H_NSA * n_tok
    past_len = n_pages * PAGE_SIZE
    wbuf = win_t.shape[-1]
    e = _block_expand(sel_rows.shape[-1], past_len)
    full = lambda a: pl.BlockSpec(a.shape, lambda i, pt: (0,) * a.ndim)
    seq = lambda a: pl.BlockSpec((1,) + a.shape[1:], lambda i, pt: (i,) + (0,) * (a.ndim - 1))
    grid_spec = pltpu.PrefetchScalarGridSpec(
        num_scalar_prefetch=1,
        grid=(b,),
        in_specs=[pl.BlockSpec(memory_space=pl.ANY), seq(q_blk), seq(sel_rows), full(e), seq(kvs_new),
                  seq(win_t), seq(kvw_new), seq(kvw_new_t), seq(gate_rows), seq(o_cmp), seq(zn_rows)],
        out_specs=(pl.BlockSpec((1, rows, DH), lambda i, pt: (i, 0, 0)), seq(win_t)),
        scratch_shapes=[pltpu.VMEM((2, 2, KVH * DH, past_len), f32), pltpu.SemaphoreType.DMA((2,))],
    )
    return pl.pallas_call(
        functools.partial(_slc_win_sample_kernel, n_pages=n_pages, n_tok=n_tok, win_off=wbuf - WINDOW),
        grid_spec=grid_spec,
        out_shape=(jax.ShapeDtypeStruct((b, rows, DH), f32), jax.ShapeDtypeStruct(win_t.shape, f32)),
        compiler_params=_cparams(("arbitrary",)),
        name="slc_win_sample",
    )(page_table, cache_t, q_blk, sel_rows, e, kvs_new, win_t, kvw_new, kvw_new_t, gate_rows, o_cmp, zn_rows)


def _gla_sample(gq, gk, la, gv, zg, state, gain, b, n_tok):
    rows = 8
    pad = lambda a: jnp.pad(a.reshape(b, n_tok, -1), ((0, 0), (0, rows - n_tok), (0, 0))).reshape(b * rows, -1)
    y, s_new = _gla(pad(gq), pad(gk), pad(gv), pad(la), pad(zg), gain, state, b, tl=rows, chunk=rows)
    return y.reshape(b, rows, GLA_WIDTH)[:, :n_tok].reshape(b * n_tok, GLA_WIDTH), s_new


def _sample_path(x, cache_c, cache_s, cache_w, state, page_table, wts):
    (norm_g, w_pack, wa2_pad, ba, w1_pairs, cmp_bias, w2_heads, gla_gain, w_out, out_gain) = wts
    b, n_tok, _ = x.shape
    n = b * n_tok
    n_pages = page_table.shape[1]
    past_len = n_pages * PAGE_SIZE
    assert n_tok <= 8 and past_len % SLC_LEN == 0 and past_len // SLC_LEN <= LANES
    pos = past_len + jnp.arange(n) % n_tok
    (q_hm, kvc, kvs, kvw, _, _, _, _, gate, zn, gq, gk, gv, la, zg) = _inproj(
        x.reshape(1, n, D_MODEL), pos, norm_g, w_pack, wa2_pad, ba, tm=n)

    rows = H_NSA * n_tok
    q_rows = q_hm[0].reshape(H_NSA, b, n_tok, DH).transpose(1, 0, 2, 3)
    zero = jnp.zeros_like(q_rows[:, :G_NSA])
    q_blk = jnp.concatenate([jnp.concatenate([q_rows[:, :G_NSA], zero], axis=-1),
                             jnp.concatenate([zero, q_rows[:, G_NSA:]], axis=-1)], axis=1).reshape(b, rows, LANES)
    gate_rows = gate[:, :3 * H_NSA].reshape(b, n_tok, 3, H_NSA).transpose(0, 3, 1, 2).reshape(b, rows, 3)
    gate_rows = jnp.pad(gate_rows, ((0, 0), (0, 0), (0, LANES - 3)))
    pad_new = lambda a: jnp.pad(a.reshape(b, n_tok, KV_ROW), ((0, 0), (0, 8 - n_tok), (0, 0)))

    kvw_new_t = kvw.reshape(b, n_tok, 2, KVH * DH).transpose(0, 2, 3, 1)
    kvw_new_t = jnp.pad(kvw_new_t, ((0, 0), (0, 0), (0, 0), (LANES - n_tok, 0)))

    o_cmp, sel_rows = _cmp_attn_sample(_pages_native(cache_c), page_table, w1_pairs, cmp_bias, w2_heads,
                                       q_blk, gate_rows, n_tok)
    o_rows, win_t = _slc_win_sample(_pages_native(cache_s), page_table, q_blk, sel_rows, pad_new(kvs),
                                    _pages_native(cache_w), pad_new(kvw), kvw_new_t, gate_rows, o_cmp,
                                    zn.reshape(b, n_tok, H_NSA, DH).transpose(0, 2, 1, 3).reshape(b, rows, DH),
                                    n_tok)
    y_nsa = o_rows.reshape(b, H_NSA, n_tok, DH).transpose(0, 2, 1, 3).reshape(n, NSA_WIDTH)

    y_gla, s_new = _gla_sample(gq, gk, la, gv, zg, state, gla_gain, b, n_tok)
    y = _outproj(x.reshape(n, D_MODEL), y_nsa, y_gla, w_out, out_gain, tm=n)
    kv5 = lambda a: a.reshape(b, n_tok, 2, KVH, DH)
    win_new = win_t.reshape(b, 2, KVH, DH, -1).transpose(0, 4, 1, 2, 3)
    return (y.reshape(b, n_tok, D_MODEL), kv5(kvc), kv5(kvs), win_new, s_new)


def kernel(x_prompt, x_sample, cache_cmp_kv, cache_slc_kv, cache_win_kv, state_gla, page_table,
           norm_in_gain, w_in, cmp_pe, cmp_w1, cmp_b1, cmp_w2, gla_wa2, gla_ba, gla_norm_gain,
           w_out, norm_out_gain):
    assert w_in.shape[0] == 1, "single-layer step"
    wts = (norm_in_gain[0], _pack_w_in(w_in[0]), _pad_wa2(gla_wa2[0]), gla_ba[0],
           _cmp_w1_pairs(cmp_w1[0]), _cmp_bias(cmp_pe[0], cmp_w1[0], cmp_b1[0]), _cmp_w2_heads(cmp_w2[0]),
           gla_norm_gain[0], w_out[0].astype(bf16), norm_out_gain)
    yp, cmp_p, slc_p, win_p, gla_p = _prompt_path(x_prompt, wts, *_pack_w_in_prompt(w_in[0]))
    ys, cmp_s, slc_s, win_s, gla_s = _sample_path(
        x_sample, cache_cmp_kv[0], cache_slc_kv[0], cache_win_kv[0], state_gla[0], page_table, wts)
    return (yp, ys, cmp_p[None], cmp_s[None], slc_p[None], slc_s[None], win_p[None], win_s[None],
            gla_p[None], gla_s[None])
```

```python
import functools

import numpy as np
import jax
import jax.numpy as jnp
from jax import lax
from jax.experimental import pallas as pl
from jax.experimental.pallas import tpu as pltpu

f32 = jnp.float32
bf16 = jnp.bfloat16

D_MODEL = 1024
DH = 64
H_NSA = 8
KVH = 2
G_NSA = 4
NSA_WIDTH = H_NSA * DH
CMP_LEN = 32
CMP_STRIDE = 16
CMP_HID = 128
SLC_LEN = 64
SLC_TOP = 16
WINDOW = 512
ROT_HALF = 8
ROPE_THETA = 500000.0
FORCE_SCORE = 1.0e4
H_GLA = 4
DK = 64
DV = 128
GLA_WIDTH = H_GLA * DV
GLA_LR = 16
GLA_TAU = 16.0
GLA_CHUNK = 64
EPS = 1e-6
NEG = -1.0e30
LOG2E = 1.4426950408889634
PAGE_SIZE = 128
KV_ROW = 2 * KVH * DH

IN_SIZES = (H_NSA * DH, KV_ROW, KV_ROW, KV_ROW, 3 * H_NSA, NSA_WIDTH,
            H_GLA * DK, H_GLA * DK, H_GLA * DV, GLA_LR, GLA_WIDTH)
IN_OFFSETS = [0] + [int(v) for v in np.cumsum(IN_SIZES)]

LANES = 128
VMEM_LIMIT = 56 * 1024 * 1024

P_Q, P_KVC, P_KVS, P_KVW, P_ZN, P_GQ, P_GK, P_GV, P_ZG, P_MISC = (
    0, 512, 768, 1024, 1280, 1792, 2048, 2304, 2816, 3328)
D_PACK = P_MISC + LANES
MISC_GATE = 0
MISC_GLR = 32


def _cparams(sem):
    return pltpu.CompilerParams(dimension_semantics=sem, vmem_limit_bytes=VMEM_LIMIT)


def _sigmoid(x):
    return 1.0 / (1.0 + jnp.exp(-x))


def _silu(x):
    return x * _sigmoid(x)


def _log_sigmoid(x):
    return jnp.minimum(x, 0.0) - jnp.log1p(jnp.exp(-jnp.abs(x)))


def _dot(a, b):
    return jnp.dot(a, b, preferred_element_type=f32)


def _dot_nt(a, b):
    return lax.dot_general(a, b, (((1,), (1,)), ((), ())), preferred_element_type=f32)


def _dot_tn(a, b):
    return lax.dot_general(a, b, (((0,), (0,)), ((), ())), preferred_element_type=f32)


def _rope_tables(pos):
    n = pos.shape[0]
    inv = ROPE_THETA ** (-(jnp.arange(ROT_HALF, dtype=f32) / ROT_HALF))
    ang = pos.astype(f32)[:, None] * inv[None, :]
    cos, sin = jnp.cos(ang), jnp.sin(ang)
    z8 = jnp.zeros((n, ROT_HALF), f32)
    rest = DH - 2 * ROT_HALF
    c64 = jnp.concatenate([cos, cos, jnp.ones((n, rest), f32)], axis=-1)
    sa64 = jnp.concatenate([-sin, z8, jnp.zeros((n, rest), f32)], axis=-1)
    sb64 = jnp.concatenate([z8, sin, jnp.zeros((n, rest), f32)], axis=-1)
    tile = lambda t: jnp.concatenate([t, t], axis=-1)
    return tile(c64), tile(sa64), tile(sb64)


def _rope128(v, cos, sa, sb):
    return v * cos + pltpu.roll(v, LANES - ROT_HALF, 1) * sa + pltpu.roll(v, ROT_HALF, 1) * sb


def _inproj_kernel(x_ref, g_ref, w_ref, cos_ref, sa_ref, sb_ref, wa2_ref, ba_ref,
                   q_ref, kvc_ref, kvs_ref, kvw_ref, gate_ref, zn_ref, gq_ref, gk_ref, gv_ref, la_ref, zg_ref):
    x = x_ref[...]
    ms = jnp.mean(x * x, axis=-1, keepdims=True)
    hn = (x * lax.rsqrt(ms + EPS) * g_ref[...]).astype(bf16)
    cos, sa, sb = cos_ref[...], sa_ref[...], sb_ref[...]

    def proj(off, width):
        return _dot(hn, w_ref[:, off:off + width])

    qp = proj(P_Q, NSA_WIDTH)
    for c in range(NSA_WIDTH // LANES):
        r = _rope128(qp[:, c * LANES:(c + 1) * LANES], cos, sa, sb) * (DH ** -0.5)
        q_ref[0, 2 * c] = r[:, :DH].astype(bf16)
        q_ref[0, 2 * c + 1] = r[:, DH:].astype(bf16)

    kvc_ref[...] = proj(P_KVC, KV_ROW)

    for off, kv_ref in ((P_KVS, kvs_ref), (P_KVW, kvw_ref)):
        p = proj(off, KV_ROW)
        kv_ref[:, :LANES] = _rope128(p[:, :LANES], cos, sa, sb)
        kv_ref[:, LANES:] = p[:, LANES:]

    zn_ref[...] = _silu(proj(P_ZN, NSA_WIDTH))
    gq_ref[...] = proj(P_GQ, H_GLA * DK) * (DK ** -0.5)
    gk_ref[...] = proj(P_GK, H_GLA * DK)
    gv_ref[...] = proj(P_GV, H_GLA * DV)
    zg_ref[...] = _silu(proj(P_ZG, GLA_WIDTH))

    misc = proj(P_MISC, LANES)
    gate_ref[...] = _sigmoid(misc)
    xa = _dot(misc.astype(bf16), wa2_ref[...]) + ba_ref[...]
    la_ref[...] = _log_sigmoid(xa) / GLA_TAU


def _pack_w_in(w_in):
    o = IN_OFFSETS
    seg = lambda i: w_in[:, o[i]:o[i + 1]]
    misc = jnp.zeros((D_MODEL, LANES), w_in.dtype)
    misc = misc.at[:, MISC_GATE:MISC_GATE + 3 * H_NSA].set(seg(4))
    misc = misc.at[:, MISC_GLR:MISC_GLR + GLA_LR].set(seg(9))
    cols = [seg(0), seg(1), seg(2), seg(3), seg(5), seg(6), seg(7), seg(8), seg(10), misc]
    return jnp.concatenate(cols, axis=1).astype(bf16)


def _pad_wa2(wa2):
    pad = jnp.zeros((LANES, H_GLA * DK), wa2.dtype).at[MISC_GLR:MISC_GLR + GLA_LR].set(wa2)
    return pad.astype(bf16)


def _inproj(x, pos, norm_g, w_pack, wa2_pad, ba, tm):
    bk, lk, _ = x.shape
    n = bk * lk
    nt = lk // tm
    cos, sa, sb = _rope_tables(pos)
    tok = lambda w: pl.BlockSpec((tm, w), lambda i: (i, 0))
    tab = pl.BlockSpec((tm, LANES), lambda i: (i % nt, 0))
    full = lambda a: pl.BlockSpec(a.shape, lambda i: (0,) * a.ndim)
    hm = lambda h: pl.BlockSpec((1, h, tm, DH), lambda i: (i // nt, 0, i % nt, 0))
    g2 = norm_g.reshape(1, D_MODEL)
    ba2 = ba.reshape(1, H_GLA * DK)
    out_shape = (
        jax.ShapeDtypeStruct((bk, H_NSA, lk, DH), bf16),
        jax.ShapeDtypeStruct((n, KV_ROW), f32),
        jax.ShapeDtypeStruct((n, KV_ROW), f32),
        jax.ShapeDtypeStruct((n, KV_ROW), f32),
        jax.ShapeDtypeStruct((n, LANES), f32),
        jax.ShapeDtypeStruct((n, NSA_WIDTH), f32),
        jax.ShapeDtypeStruct((n, H_GLA * DK), f32),
        jax.ShapeDtypeStruct((n, H_GLA * DK), f32),
        jax.ShapeDtypeStruct((n, H_GLA * DV), f32),
        jax.ShapeDtypeStruct((n, H_GLA * DK), f32),
        jax.ShapeDtypeStruct((n, GLA_WIDTH), f32),
    )
    out_specs = (hm(H_NSA), tok(KV_ROW), tok(KV_ROW), tok(KV_ROW),
                 tok(LANES), tok(NSA_WIDTH), tok(H_GLA * DK), tok(H_GLA * DK), tok(H_GLA * DV),
                 tok(H_GLA * DK), tok(GLA_WIDTH))
    return pl.pallas_call(
        _inproj_kernel,
        grid=(n // tm,),
        in_specs=[tok(D_MODEL), full(g2), full(w_pack), tab, tab, tab, full(wa2_pad), full(ba2)],
        out_specs=out_specs,
        out_shape=out_shape,
        compiler_params=_cparams(("parallel",)),
        name="inproj",
    )(x.reshape(n, D_MODEL), g2, w_pack, cos, sa, sb, wa2_pad, ba2)


PT_Q, PT_KVC, PT_KVS, PT_KVW, PT_GATE = 0, 512, 768, 1024, 1280
PT_ROWS = PT_GATE + 32
PK_ZN, PK_GQ, PK_GK, PK_GV, PK_ZG, PK_MISC = 0, 512, 768, 1024, 1536, 2048
PK_COLS = PK_MISC + LANES


def _inproj_prompt_kernel(x_ref, g_ref, wt_ref, w_ref, cos_ref, sin_ref, wa2_ref, ba_ref,
                          qt_ref, kvct_ref, kvst_ref, kvwt_ref, ks_ref, kw_ref, vst_ref, vwt_ref,
                          gatet_ref, zn_ref, gq_ref, gk_ref, gv_ref, la_ref, zg_ref):
    x = x_ref[...]
    ms = jnp.mean(x * x, axis=-1, keepdims=True)
    hn = (x * lax.rsqrt(ms + EPS) * g_ref[...]).astype(bf16)
    cos_t, sin_t = cos_ref[...], sin_ref[...]

    all_t = _dot_nt(wt_ref[...], hn)

    def proj_t(off, rows):
        return all_t[off:off + rows]

    def rope_rows(v):
        x1, x2 = v[:ROT_HALF], v[ROT_HALF:2 * ROT_HALF]
        return jnp.concatenate([x1 * cos_t - x2 * sin_t, x2 * cos_t + x1 * sin_t, v[2 * ROT_HALF:]], axis=0)

    q_t = proj_t(PT_Q, NSA_WIDTH)
    for h in range(H_NSA):
        qt_ref[0, h] = (rope_rows(q_t[h * DH:(h + 1) * DH]) * (DH ** -0.5 * LOG2E)).astype(bf16)

    kvct_ref[0] = proj_t(PT_KVC, KV_ROW)

    for off, kvt_ref, k_ref, vt_ref in ((PT_KVS, kvst_ref, ks_ref, vst_ref), (PT_KVW, kvwt_ref, kw_ref, vwt_ref)):
        t = proj_t(off, KV_ROW)
        k_t = jnp.concatenate([rope_rows(t[h * DH:(h + 1) * DH]) for h in range(KVH)], axis=0)
        kvt_ref[0, :LANES] = k_t
        kvt_ref[0, LANES:] = t[LANES:]
        k_tok = k_t.T
        for h in range(KVH):
            k_ref[0, h] = k_tok[:, h * DH:(h + 1) * DH].astype(bf16)
            vt_ref[0, h] = t[LANES + h * DH:LANES + (h + 1) * DH].astype(bf16)

    gatet_ref[0] = _sigmoid(proj_t(PT_GATE, PT_ROWS - PT_GATE))

    def proj(off, width):
        return _dot(hn, w_ref[:, off:off + width])

    zn_ref[...] = _silu(proj(PK_ZN, NSA_WIDTH))
    gq_ref[...] = proj(PK_GQ, H_GLA * DK) * (DK ** -0.5)
    gk_ref[...] = proj(PK_GK, H_GLA * DK)
    gv_ref[...] = proj(PK_GV, H_GLA * DV)
    zg_ref[...] = _silu(proj(PK_ZG, GLA_WIDTH))
    misc = proj(PK_MISC, LANES)
    xa = _dot(misc.astype(bf16), wa2_ref[...]) + ba_ref[...]
    la_ref[...] = _log_sigmoid(xa) / GLA_TAU


def _pack_w_in_prompt(w_in):
    o = IN_OFFSETS
    seg = lambda i: w_in[:, o[i]:o[i + 1]]
    gate_t = jnp.zeros((PT_ROWS - PT_GATE, D_MODEL), w_in.dtype).at[:3 * H_NSA].set(seg(4).T)
    w_t = jnp.concatenate([seg(0).T, seg(1).T, seg(2).T, seg(3).T, gate_t], axis=0).astype(bf16)
    misc = jnp.zeros((D_MODEL, LANES), w_in.dtype).at[:, MISC_GLR:MISC_GLR + GLA_LR].set(seg(9))
    w_tok = jnp.concatenate([seg(5), seg(6), seg(7), seg(8), seg(10), misc], axis=1).astype(bf16)
    return w_t, w_tok


def _inproj_prompt(x, norm_g, w_t, w_tok, wa2_pad, ba, tm):
    b, lq, _ = x.shape
    n = b * lq
    nt = lq // tm
    inv = ROPE_THETA ** (-(jnp.arange(ROT_HALF, dtype=f32) / ROT_HALF))
    ang = inv[:, None] * jnp.arange(lq).astype(f32)[None, :]
    cos_t, sin_t = jnp.cos(ang), jnp.sin(ang)
    tok = lambda w: pl.BlockSpec((tm, w), lambda i: (i, 0))
    tab = pl.BlockSpec((ROT_HALF, tm), lambda i: (0, i % nt))
    full = lambda a: pl.BlockSpec(a.shape, lambda i: (0,) * a.ndim)
    feat = lambda r: pl.BlockSpec((1, r, tm), lambda i: (i // nt, 0, i % nt))
    headf = lambda h: pl.BlockSpec((1, h, DH, tm), lambda i: (i // nt, 0, 0, i % nt))
    headt = lambda h: pl.BlockSpec((1, h, tm, DH), lambda i: (i // nt, 0, i % nt, 0))
    g2 = norm_g.reshape(1, D_MODEL)
    ba2 = ba.reshape(1, H_GLA * DK)
    sds = jax.ShapeDtypeStruct
    out_shape = (
        sds((b, H_NSA, DH, lq), bf16),
        sds((b, KV_ROW, lq), f32), sds((b, KV_ROW, lq), f32), sds((b, KV_ROW, lq), f32),
        sds((b, KVH, lq, DH), bf16), sds((b, KVH, lq, DH), bf16),
        sds((b, KVH, DH, lq), bf16), sds((b, KVH, DH, lq), bf16),
        sds((b, PT_ROWS - PT_GATE, lq), f32),
        sds((n, NSA_WIDTH), f32), sds((n, H_GLA * DK), f32), sds((n, H_GLA * DK), f32),
        sds((n, H_GLA * DV), f32), sds((n, H_GLA * DK), f32), sds((n, GLA_WIDTH), f32),
    )
    out_specs = (headf(H_NSA), feat(KV_ROW), feat(KV_ROW), feat(KV_ROW), headt(KVH), headt(KVH),
                 headf(KVH), headf(KVH), feat(PT_ROWS - PT_GATE),
                 tok(NSA_WIDTH), tok(H_GLA * DK), tok(H_GLA * DK), tok(H_GLA * DV), tok(H_GLA * DK),
                 tok(GLA_WIDTH))
    return pl.pallas_call(
        _inproj_prompt_kernel,
        grid=(n // tm,),
        in_specs=[tok(D_MODEL), full(g2), full(w_t), full(w_tok), tab, tab, full(wa2_pad), full(ba2)],
        out_specs=out_specs,
        out_shape=out_shape,
        compiler_params=_cparams(("parallel",)),
        name="inproj_prompt",
    )(x.reshape(n, D_MODEL), g2, w_t, w_tok, cos_t, sin_t, wa2_pad, ba2)


HID_W = 2 * KVH * CMP_HID
CHUNKS_PER_PAGE = PAGE_SIZE // CMP_STRIDE
PAGE_GROUP = 8


def _cmp_w1_pairs(cmp_w1):
    ratio = CMP_LEN // CMP_STRIDE
    w1r = cmp_w1.reshape(2, ratio, CMP_STRIDE // 2, 2, DH, CMP_HID)
    eye = jnp.eye(KVH, dtype=cmp_w1.dtype)
    big = jnp.einsum('crpjdh,kK->cpjKdrkh', w1r, eye)
    return big.reshape(2, CMP_STRIDE // 2, 2 * KVH * DH, ratio * KVH * CMP_HID).astype(bf16)


def _cmp_w2_heads(cmp_w2):
    eye = jnp.eye(KVH, dtype=cmp_w2.dtype)
    return jnp.einsum('chd,kK->ckhKd', cmp_w2, eye).reshape(2, KVH * CMP_HID, KVH * DH).astype(bf16)


def _row_gather_matrix():
    r = np.arange(PAGE_SIZE)
    src = (r % CHUNKS_PER_PAGE) * CMP_STRIDE + r // CHUNKS_PER_PAGE
    return jnp.asarray((r[None, :] == src[:, None]).astype(np.float32), dtype=bf16)


def _compress_pages(page_at, n_pages, perm, rows_scr, w1_ref, bias_ref, w2_ref):
    cpp = CHUNKS_PER_PAGE
    kd = KVH * DH

    def to_rows(g, carry):
        pages = [g * PAGE_GROUP + u for u in range(PAGE_GROUP)]
        gathered = [_dot_nt(perm, page_at(p).astype(bf16)) for p in pages]
        for p, t in zip(pages, gathered):
            r0 = pl.multiple_of(p * cpp, cpp)
            for c in range(2):
                for s in range(CMP_STRIDE):
                    rows_scr[c, s, pl.ds(r0, cpp), :] = t[s * cpp:(s + 1) * cpp, c * kd:(c + 1) * kd]
        return carry

    lax.fori_loop(0, n_pages // PAGE_GROUP, to_rows, 0)

    nc = n_pages * cpp
    kv = []
    for c in range(2):
        lhs = jnp.concatenate([rows_scr[c, s] for s in range(CMP_STRIDE)], axis=-1).astype(bf16)
        part = _dot(lhs, w1_ref[c].reshape(CMP_STRIDE * kd, w1_ref.shape[-1]))
        half = part.shape[1] // 2
        pre = part[:, :half] + pltpu.roll(part[:, half:], nc - 1, 0)
        hid = _silu(pre + bias_ref[:, c * half:(c + 1) * half])
        kv.append(_dot(hid.astype(bf16), w2_ref[c]))
    return kv


def _cmp_bias_kernel(pe_ref, w1_ref, b1_ref, o_ref):
    for c in range(2):
        o_ref[c] = _dot(pe_ref[c], w1_ref[c]) + b1_ref[c]


def _cmp_bias(cmp_pe, cmp_w1, cmp_b1):
    pe = jnp.broadcast_to(cmp_pe.reshape(2, 1, CMP_LEN * DH), (2, 8, CMP_LEN * DH))
    b1 = jnp.broadcast_to(cmp_b1.reshape(2, 1, CMP_HID), (2, 8, CMP_HID))
    out = pl.pallas_call(
        _cmp_bias_kernel,
        out_shape=jax.ShapeDtypeStruct((2, 8, CMP_HID), f32),
        name="cmp_bias",
    )(pe, cmp_w1, b1)
    row = out[:, 0, :]
    return jnp.broadcast_to(row[:, None, :], (2, KVH, CMP_HID)).reshape(1, HID_W)


def _masked_softmax(s, mask, axis, exp=jnp.exp):
    s = jnp.where(mask, s, NEG)
    e = exp(s - jnp.max(s, axis=axis, keepdims=True))
    return jnp.where(mask, e * (1.0 / jnp.sum(e, axis=axis, keepdims=True)), 0.0)


def _split_bf16(x):
    hi = x.astype(bf16)
    return hi, (x - hi.astype(f32)).astype(bf16)


def _topk_mask(score, k, axis):
    n = score.shape[axis]
    idx = lax.broadcasted_iota(jnp.int32, score.shape, axis)
    sel = jnp.zeros(score.shape, f32)
    for _ in range(k):
        m = jnp.max(score, axis=axis, keepdims=True)
        first = jnp.min(jnp.where(score == m, idx, n), axis=axis, keepdims=True)
        pick = idx == first
        sel = jnp.where(pick, 1.0, sel)
        score = jnp.where(pick, NEG, score)
    return sel


def _topk_rows_by_rank(score, k):
    r, n = score.shape
    cols = jnp.concatenate([score, jnp.zeros((n - r, n), f32)], axis=0).T
    earlier = jnp.where(lax.broadcasted_iota(jnp.int32, (n, n), 0) < lax.broadcasted_iota(jnp.int32, (n, n), 1),
                        1.0, 0.0)
    rows = []
    for c in range(r):
        col, row = cols[:, c:c + 1], score[c:c + 1, :]
        before = jnp.where(col > row, 1.0, jnp.where(col == row, earlier, 0.0))
        rows.append(jnp.where(jnp.sum(before, axis=0, keepdims=True) < k, 1.0, 0.0))
    return jnp.concatenate(rows, axis=0)


def _cover_t(n_slc, n_chunk):
    start = np.arange(n_chunk)[None, :] * CMP_STRIDE
    j = np.arange(n_slc)[:, None]
    cov = (start < (j + 1) * SLC_LEN) & (start + CMP_LEN > j * SLC_LEN)
    return jnp.asarray(cov.astype(np.float32), dtype=bf16)


def _cmp_attn_kernel(kvct_ref, perm_ref, w1_ref, bias_ref, w2_ref, cos_ref, sa_ref, sb_ref, qt_ref, gatet_ref,
                     cov_ref, ot_ref, selt_ref, rows_scr, kc_ref, vct_ref, *, tq):
    t = pl.program_id(1)

    @pl.when(t == 0)
    def _():
        page_at = lambda p: kvct_ref[0, :, pl.ds(pl.multiple_of(p * PAGE_SIZE, PAGE_SIZE), PAGE_SIZE)]
        k, v = _compress_pages(page_at, kvct_ref.shape[2] // PAGE_SIZE, perm_ref[...], rows_scr,
                               w1_ref, bias_ref, w2_ref)
        k = _rope128(k, cos_ref[...], sa_ref[...], sb_ref[...])
        v_t = v.T
        for h in range(KVH):
            kc_ref[h] = k[:, h * DH:(h + 1) * DH].astype(bf16)
            vct_ref[h] = v_t[h * DH:(h + 1) * DH].astype(bf16)

    nc = kc_ref.shape[1]
    nb = cov_ref.shape[0]
    qpos = t * tq + lax.broadcasted_iota(jnp.int32, (1, tq), 1)
    end_pos = lax.broadcasted_iota(jnp.int32, (nc, 1), 0) * CMP_STRIDE + (CMP_LEN - 1)
    mask_t = end_pos <= qpos
    gate_t = gatet_ref[0]
    cov = cov_ref[...]

    jblk = lax.broadcasted_iota(jnp.int32, (nb, tq), 0)
    qblk = qpos // SLC_LEN
    valid = jblk <= qblk
    forced = (jblk == 0) | (jblk == qblk) | (jblk == qblk - 1)

    heads = range(H_NSA)
    s_t = [_dot(kc_ref[h // G_NSA], qt_ref[0, h]) for h in heads]
    p_t = [_masked_softmax(s, mask_t, axis=0, exp=jnp.exp2) for s in s_t]
    for h in heads:
        ot_ref[0, h * DH:(h + 1) * DH] = _dot(vct_ref[h // G_NSA], p_t[h].astype(bf16)) * gate_t[h:h + 1]
    scores = []
    for kh in range(KVH):
        psum = functools.reduce(lambda a, b: a + b, p_t[kh * G_NSA:(kh + 1) * G_NSA])
        hi, lo = _split_bf16(psum)
        imp = _dot(cov, hi) + _dot(cov, lo)
        scores.append(jnp.where(valid, jnp.where(forced, FORCE_SCORE, imp), -1.0))
    picked = _topk_mask(jnp.concatenate(scores, axis=1), SLC_TOP, axis=0)
    for kh in range(KVH):
        sel_t = jnp.where(valid & (picked[:, kh * tq:(kh + 1) * tq] > 0.5), 0.0, NEG)
        if nb < LANES:
            sel_t = jnp.concatenate([sel_t, jnp.full((LANES - nb, tq), NEG, f32)], axis=0)
        selt_ref[0, kh] = sel_t


def _cmp_attn(kvc_t, w1_pairs, bias, w2_heads, q_t, gate_t, tq):
    b, _, lq = kvc_t.shape
    assert lq % (PAGE_SIZE * PAGE_GROUP) == 0
    nc = lq // CMP_STRIDE
    n_slc = lq // SLC_LEN
    perm = _row_gather_matrix()
    end_pos = jnp.arange(nc) * CMP_STRIDE + (CMP_LEN - 1)
    cos, sa, sb = _rope_tables(end_pos)
    cov = _cover_t(n_slc, nc)
    full = lambda a: pl.BlockSpec(a.shape, lambda i, t: (0,) * a.ndim)
    return pl.pallas_call(
        functools.partial(_cmp_attn_kernel, tq=tq),
        grid=(b, lq // tq),
        in_specs=[pl.BlockSpec((1, KV_ROW, lq), lambda i, t: (i, 0, 0)),
                  full(perm), full(w1_pairs), full(bias), full(w2_heads), full(cos), full(sa), full(sb),
                  pl.BlockSpec((1, H_NSA, DH, tq), lambda i, t: (i, 0, 0, t)),
                  pl.BlockSpec((1, gate_t.shape[1], tq), lambda i, t: (i, 0, t)),
                  full(cov)],
        out_specs=(pl.BlockSpec((1, NSA_WIDTH, tq), lambda i, t: (i, 0, t)),
                   pl.BlockSpec((1, KVH, LANES, tq), lambda i, t: (i, 0, 0, t))),
        out_shape=(jax.ShapeDtypeStruct((b, NSA_WIDTH, lq), f32),
                   jax.ShapeDtypeStruct((b, KVH, LANES, lq), f32)),
        scratch_shapes=[pltpu.VMEM((2, CMP_STRIDE, nc, KVH * DH), f32),
                        pltpu.VMEM((KVH, nc, DH), bf16), pltpu.VMEM((KVH, DH, nc), bf16)],
        compiler_params=_cparams(("parallel", "arbitrary")),
        name="cmp_attn",
    )(kvc_t, perm, w1_pairs, bias, w2_heads, cos, sa, sb, q_t, gate_t, cov)


def _block_expand(n_slc_pad, n_keys):
    e = (np.arange(n_keys)[None, :] // SLC_LEN) == np.arange(n_slc_pad)[:, None]
    return jnp.asarray(e.astype(np.float32), dtype=bf16)


def _mask_bias_t(allowed):
    bias = jnp.where(allowed, 0.0, NEG)
    return jnp.concatenate([bias] * G_NSA, axis=1)


def _sublane_fold(x, op):
    return functools.reduce(op, [x[r:r + 8] for r in range(0, x.shape[0], 8)])


def _slc_win_kernel(qt_ref, ks_ref, vst_ref, kw_ref, vwt_ref, selt_ref, gatet_ref, ocmpt_ref, zn_ref,
                    o_ref, s_a, s_b, m_a, m_b, ws_a, ws_b, wm_a, wm_b, acc_ref, wacc_ref, *, tq, tk, nt):
    t = pl.program_id(2)

    @pl.when(t == 0)
    def _():
        m_b[...] = jnp.full(m_b.shape, NEG, f32)
        ws_b[...] = jnp.zeros(ws_b.shape, f32)
        wm_b[...] = jnp.zeros(wm_b.shape, f32)

    step = functools.partial(_slc_win_step, qt_ref, ks_ref, vst_ref, kw_ref, vwt_ref, selt_ref, gatet_ref,
                             ocmpt_ref, zn_ref, o_ref, acc_ref, wacc_ref, tq=tq, tk=tk, nt=nt)

    @pl.when(t % 2 == 0)
    def _():
        step(s_a, s_b, m_a, m_b, ws_a, ws_b, wm_a, wm_b)

    @pl.when(t % 2 == 1)
    def _():
        step(s_b, s_a, m_b, m_a, ws_b, ws_a, wm_b, wm_a)


ONES_ROWS = 16


def _slc_win_step(qt_ref, ks_ref, vst_ref, kw_ref, vwt_ref, selt_ref, gatet_ref, ocmpt_ref, zn_ref, o_ref,
                  acc_ref, wacc_ref, s_cur, s_prev, m_cur, m_last, ws_cur, ws_prev, wm_cur, wm_last,
                  *, tq, tk, nt):
    kh = pl.program_id(1)
    t = pl.program_id(2)
    lq = ks_ref.shape[2]
    band = WINDOW + tq
    blocks_per_tile = tk // SLC_LEN
    key_tiles = lambda tile: (tile * tq + tq + tk - 1) // tk
    n_apply = jnp.where(t >= 1, key_tiles(t - 1), 0)
    n_plain = jnp.where(t < nt, (t * tq) // tk, 0)
    q0 = t * tq
    qpos = q0 + lax.broadcasted_iota(jnp.int32, (1, tq), 1)
    q_t = jnp.concatenate([qt_ref[0, g] for g in range(G_NSA)], axis=1)
    ones = jnp.ones((ONES_ROWS, tk), bf16)

    m_cur[...] = jnp.full(m_cur.shape, NEG, f32)
    m_prev = jnp.max(m_last[...], axis=0, keepdims=True)
    acc_ref[...] = jnp.zeros(acc_ref.shape, f32)

    def score(kt, causal):
        k0 = pl.multiple_of(kt * tk, tk)
        j0 = pl.multiple_of(kt * blocks_per_tile, blocks_per_tile)
        sel = selt_ref[0, 0, pl.ds(j0, blocks_per_tile), :]
        bias = jnp.concatenate([jnp.broadcast_to(sel[j:j + 1], (SLC_LEN, tq)) for j in range(blocks_per_tile)],
                               axis=0)
        if causal:
            kpos = k0 + lax.broadcasted_iota(jnp.int32, (tk, 1), 0)
            bias = jnp.where(kpos <= qpos, bias, NEG)
        s = _dot(ks_ref[0, 0, pl.ds(k0, tk), :], q_t) + jnp.concatenate([bias] * G_NSA, axis=1)
        s_cur[pl.ds(k0, tk), :] = s
        m_cur[...] = jnp.maximum(m_cur[...], _sublane_fold(s, jnp.maximum))

    def apply(kt):
        k0 = pl.multiple_of(kt * tk, tk)
        p = jnp.exp2(s_prev[pl.ds(k0, tk), :] - m_prev).astype(bf16)
        acc_ref[...] += _dot(jnp.concatenate([vst_ref[0, 0, :, pl.ds(k0, tk)], ones], axis=0), p)

    def both_body(kt, carry):
        score(kt, causal=False)
        apply(kt)
        return carry

    def apply_body(kt, carry):
        apply(kt)
        return carry

    def both2_body(i, carry):
        for u in range(2):
            score(2 * i + u, causal=False)
            apply(2 * i + u)
        return carry

    n_both = jnp.minimum(n_plain, n_apply)
    lax.fori_loop(0, n_both // 2, both2_body, 0)
    lax.fori_loop(n_both // 2 * 2, n_both, both_body, 0)
    lax.fori_loop(n_both, n_apply, apply_body, 0)
    score(jnp.minimum((t * tq) // tk, lq // tk - 1), causal=True)

    w0 = pl.multiple_of(jnp.clip(q0 - WINDOW, 0, lq - band), tq)
    w0_prev = pl.multiple_of(jnp.clip(q0 - tq - WINDOW, 0, lq - band), tq)
    wm_prev = jnp.max(wm_last[...], axis=0, keepdims=True)
    ones_w = jnp.ones((ONES_ROWS, tq), bf16)
    wmax, wacc = None, None
    for j in range(band // tq):
        rows = pl.ds(j * tq, tq)
        kpos = w0 + j * tq + lax.broadcasted_iota(jnp.int32, (tq, 1), 0)
        s = (_dot(kw_ref[0, 0, pl.ds(pl.multiple_of(w0 + j * tq, tq), tq), :], q_t)
             + _mask_bias_t((kpos <= qpos) & (kpos > qpos - WINDOW)))
        ws_cur[rows, :] = s
        fold = _sublane_fold(s, jnp.maximum)
        wmax = fold if wmax is None else jnp.maximum(wmax, fold)
        p = jnp.exp2(ws_prev[rows, :] - wm_prev).astype(bf16)
        v_ext = jnp.concatenate([vwt_ref[0, 0, :, pl.ds(pl.multiple_of(w0_prev + j * tq, tq), tq)], ones_w], axis=0)
        part = _dot(v_ext, p)
        wacc = part if wacc is None else wacc + part
    wm_cur[...] = wmax
    wacc_ref[...] = wacc

    @pl.when(t >= 1)
    def _():
        o_slc = acc_ref[:DH] / acc_ref[DH:DH + 1]
        o_win = wacc_ref[:DH] / wacc_ref[DH:DH + 1]
        gate_t = gatet_ref[0]
        row = lambda c, g: jnp.where(kh == 0, gate_t[c * H_NSA + g:c * H_NSA + g + 1],
                                     gate_t[c * H_NSA + G_NSA + g:c * H_NSA + G_NSA + g + 1])
        g_slc = jnp.concatenate([row(1, g) for g in range(G_NSA)], axis=1)
        g_win = jnp.concatenate([row(2, g) for g in range(G_NSA)], axis=1)
        o_t = g_slc * o_slc + g_win * o_win
        o_heads = jnp.concatenate([o_t[:, g * tq:(g + 1) * tq] for g in range(G_NSA)], axis=0)
        o_ref[0] = ((ocmpt_ref[0] + o_heads).T * zn_ref[...]).astype(o_ref.dtype)


def _slc_win(q_t, ks, vs_t, kw, vw_t, sel_t, gate_t, o_cmp_t, zn, tq, tk):
    b, _, _, lq = q_t.shape
    assert lq >= WINDOW + tq and lq % tk == 0 and tk % (8 * SLC_LEN) == 0
    gw = G_NSA * DH
    k_spec = pl.BlockSpec((1, 1, lq, DH), lambda i, k, t: (i, k, 0, 0))
    vt_spec = pl.BlockSpec((1, 1, DH, lq), lambda i, k, t: (i, k, 0, 0))
    lanes = G_NSA * tq
    band = WINDOW + tq
    nt = lq // tq
    scored = lambda t: jnp.minimum(t, nt - 1)
    drained = lambda t: jnp.maximum(t - 1, 0)
    return pl.pallas_call(
        functools.partial(_slc_win_kernel, tq=tq, tk=tk, nt=nt),
        grid=(b, KVH, nt + 1),
        in_specs=[pl.BlockSpec((1, G_NSA, DH, tq), lambda i, k, t: (i, k, 0, scored(t))),
                  k_spec, vt_spec, k_spec, vt_spec,
                  pl.BlockSpec((1, 1, LANES, tq), lambda i, k, t: (i, k, 0, scored(t))),
                  pl.BlockSpec((1, gate_t.shape[1], tq), lambda i, k, t: (i, 0, drained(t))),
                  pl.BlockSpec((1, gw, tq), lambda i, k, t: (i, k, drained(t))),
                  pl.BlockSpec((tq, gw), lambda i, k, t: (i * nt + drained(t), k))],
        out_specs=pl.BlockSpec((1, tq, gw), lambda i, k, t: (i, drained(t), k)),
        out_shape=jax.ShapeDtypeStruct((b, lq, NSA_WIDTH), bf16),
        scratch_shapes=[pltpu.VMEM((lq, lanes), f32), pltpu.VMEM((lq, lanes), f32),
                        pltpu.VMEM((8, lanes), f32), pltpu.VMEM((8, lanes), f32),
                        pltpu.VMEM((band, lanes), f32), pltpu.VMEM((band, lanes), f32),
                        pltpu.VMEM((8, lanes), f32), pltpu.VMEM((8, lanes), f32),
                        pltpu.VMEM((DH + ONES_ROWS, lanes), f32),
                        pltpu.VMEM((DH + ONES_ROWS, lanes), f32)],
        compiler_params=_cparams(("parallel", "parallel", "arbitrary")),
        name="slc_win",
    )(q_t, ks, vs_t, kw, vw_t, sel_t, gate_t, o_cmp_t, zn)


def _rms_gain(o, gain):
    return o * lax.rsqrt(jnp.mean(o * o, axis=-1, keepdims=True) + EPS) * gain


def _gla_kernel(q_ref, k_ref, v_ref, la_ref, z_ref, gain_ref, s0_ref, y_ref, st_ref, s_scr, *, tl, chunk):
    t = pl.program_id(1)

    @pl.when(t == 0)
    def _():
        s_scr[...] = s0_ref[0]

    c = chunk
    pairs = s_scr.shape[0]
    row = lax.broadcasted_iota(jnp.int32, (c, c), 0)
    col = lax.broadcasted_iota(jnp.int32, (c, c), 1)
    causal = col <= row
    tril = jnp.where(causal, 1.0, 0.0).astype(bf16)
    lane = lax.broadcasted_iota(jnp.int32, (1, LANES), 1)
    head_mask = [jnp.where(lane < DK, 1.0, 0.0), jnp.where(lane >= DK, 1.0, 0.0)]
    gain = gain_ref[...]

    units = [(ci, pr) for ci in range(tl // c) for pr in range(pairs)]
    rows_of = lambda ci: slice(ci * c, (ci + 1) * c)
    grp_of = lambda pr: slice(pr * LANES, (pr + 1) * LANES)
    bcs = []
    for ci, pr in units:
        hi, lo = _split_bf16(la_ref[rows_of(ci), grp_of(pr)])
        bcs.append(_dot(tril, hi) + _dot(tril, lo))
    scaled = []
    for (ci, pr), bc in zip(units, bcs):
        q, k = q_ref[rows_of(ci), grp_of(pr)], k_ref[rows_of(ci), grp_of(pr)]
        ref_row = bc[c // 2 - 1:c // 2, :]
        b_last = bc[c - 1:c, :]
        scaled.append((q * jnp.exp(bc - ref_row), (k * jnp.exp(ref_row - bc)).astype(bf16),
                       q * jnp.exp(bc), k * jnp.exp(b_last - bc), jnp.exp(b_last)))
    attn = []
    for (ci, pr), (q_a, k_a, _, _, _) in zip(units, scaled):
        attn.append([jnp.where(causal, _dot_nt((q_a * head_mask[h]).astype(bf16), k_a), 0.0).astype(bf16)
                     for h in range(2)])
    chunks = []
    for (ci, pr), (_, _, q_s, k_s, decay), a in zip(units, scaled, attn):
        rows = rows_of(ci)
        o_intra, q_inter, add = [], [], None
        for h in range(2):
            vh = v_ref[rows, (2 * pr + h) * DV:(2 * pr + h + 1) * DV].astype(bf16)
            o_intra.append(_dot(a[h], vh))
            q_inter.append((q_s * head_mask[h]).astype(bf16))
            u = _dot_tn(vh, (k_s * head_mask[h]).astype(bf16))
            add = u if add is None else add + u
        chunks.append((rows, pr, decay, add, o_intra, q_inter))

    st = [s_scr[pr] for pr in range(pairs)]
    for rows, pr, decay, add, o_intra, q_inter in chunks:
        st_b = st[pr].astype(bf16)
        for h in range(2):
            cols = slice((2 * pr + h) * DV, (2 * pr + h + 1) * DV)
            o = o_intra[h] + _dot_nt(q_inter[h], st_b)
            y_ref[rows, cols] = (_rms_gain(o, gain) * z_ref[rows, cols]).astype(y_ref.dtype)
        st[pr] = st[pr] * decay + add
    for pr in range(pairs):
        s_scr[pr] = st[pr]
        st_ref[0, pr] = st[pr]


def _gla(gq, gk, gv, la, zg, gain, state0, b, tl, chunk):
    n = gq.shape[0]
    nt = n // b // tl
    pairs = H_GLA // 2
    qk = pl.BlockSpec((tl, H_GLA * DK), lambda i, t: (i * nt + t, 0))
    vz = pl.BlockSpec((tl, GLA_WIDTH), lambda i, t: (i * nt + t, 0))
    st_spec = pl.BlockSpec((1, pairs, DV, LANES), lambda i, t: (i, 0, 0, 0))
    gain2 = gain.reshape(1, DV)
    s0 = state0.reshape(b, pairs, 2, DK, DV).transpose(0, 1, 4, 2, 3).reshape(b, pairs, DV, LANES)
    y, st = pl.pallas_call(
        functools.partial(_gla_kernel, tl=tl, chunk=chunk),
        grid=(b, nt),
        in_specs=[qk, qk, vz, qk, vz, pl.BlockSpec((1, DV), lambda i, t: (0, 0)), st_spec],
        out_specs=(vz, st_spec),
        out_shape=(jax.ShapeDtypeStruct((n, GLA_WIDTH), bf16),
                   jax.ShapeDtypeStruct((b, pairs, DV, LANES), f32)),
        scratch_shapes=[pltpu.VMEM((pairs, DV, LANES), f32)],
        compiler_params=_cparams(("parallel", "arbitrary")),
        name="gla",
    )(gq, gk, gv, la, zg, gain2, s0)
    state = st.reshape(b, pairs, DV, 2, DK).transpose(0, 1, 3, 4, 2).reshape(b, H_GLA, DK, DV)
    return y, state


def _outproj_kernel(x_ref, yn_ref, yg_ref, w_ref, g_ref, y_ref):
    mix = _dot(yn_ref[...].astype(bf16), w_ref[:NSA_WIDTH, :]) + _dot(yg_ref[...].astype(bf16), w_ref[NSA_WIDTH:, :])
    y_ref[...] = _rms_gain(x_ref[...] + mix, g_ref[...])


def _outproj(x, y_nsa, y_gla, w_out, gain, tm):
    n = x.shape[0]
    tok = lambda w: pl.BlockSpec((tm, w), lambda i: (i, 0))
    g2 = gain.reshape(1, D_MODEL)
    return pl.pallas_call(
        _outproj_kernel,
        grid=(n // tm,),
        in_specs=[tok(D_MODEL), tok(NSA_WIDTH), tok(GLA_WIDTH),
                  pl.BlockSpec(w_out.shape, lambda i: (0, 0)), pl.BlockSpec(g2.shape, lambda i: (0, 0))],
        out_specs=tok(D_MODEL),
        out_shape=jax.ShapeDtypeStruct((n, D_MODEL), f32),
        compiler_params=_cparams(("parallel",)),
        name="outproj",
    )(x, y_nsa, y_gla, w_out, g2)


def _prompt_path(x, wts, w_t, w_tok):
    (norm_g, _, wa2_pad, ba, w1_pairs, cmp_bias, w2_heads, gla_gain, w_out, out_gain) = wts
    b, lq, _ = x.shape
    (q_t, kvc_t, kvs_t, kvw_t, ks, kw, vs_t, vw_t, gate_t, zn, gq, gk, gv, la, zg) = _inproj_prompt(
        x, norm_g, w_t, w_tok, wa2_pad, ba, tm=512)
    o_cmp_t, sel_t = _cmp_attn(kvc_t, w1_pairs, cmp_bias, w2_heads, q_t, gate_t, tq=512)
    y_nsa = _slc_win(q_t, ks, vs_t, kw, vw_t, sel_t, gate_t, o_cmp_t, zn, tq=256, tk=512)
    y_gla, state = _gla(gq, gk, gv, la, zg, gla_gain, jnp.zeros((b, H_GLA, DK, DV), f32), b, tl=512,
                        chunk=GLA_CHUNK)
    y = _outproj(x.reshape(b * lq, D_MODEL), y_nsa.reshape(b * lq, NSA_WIDTH), y_gla, w_out, out_gain, tm=1024)
    kv5 = lambda a: a.reshape(b, 2, KVH, DH, -1).transpose(0, 4, 1, 2, 3)
    wlen = min(WINDOW, lq)
    return (y.reshape(b, lq, D_MODEL), kv5(kvc_t), kv5(kvs_t), kv5(kvw_t[:, :, lq - wlen:]), state)


def _pages_native(cache):
    n_pool = cache.shape[0]
    return cache.transpose(0, 2, 3, 4, 1).reshape(n_pool, 2, KVH * DH, cache.shape[1])


def _page_gather(make_copies):
    i = pl.program_id(0)
    slot = i % 2

    @pl.when(i == 0)
    def _():
        for c in make_copies(0, 0):
            c.start()

    @pl.when(i + 1 < pl.num_programs(0))
    def _():
        for c in make_copies(i + 1, 1 - slot):
            c.start()

    for c in make_copies(i, slot):
        c.wait()
    return slot


def _group_sum_matrix(n_tok):
    r = np.arange(KVH * n_tok)[:, None]
    c = np.arange(H_NSA * n_tok)[None, :]
    m = ((c // (G_NSA * n_tok)) == (r // n_tok)) & ((c % n_tok) == (r % n_tok))
    return m.astype(np.float32)


def _cmp_attn_sample_kernel(pt_ref, cache_ref, perm_ref, w1_ref, bias_ref, w2_ref, cos_ref, sa_ref, sb_ref, q_ref,
                            gate_ref, cov_ref, gs_ref, gst_ref, o_ref, sel_ref, buf, rows_scr, sem,
                            *, n_pages, n_tok, past_len):
    def copies(seq, slot):
        return [pltpu.make_async_copy(cache_ref.at[pt_ref[seq, p]], buf.at[slot, p], sem.at[slot])
                for p in range(n_pages)]

    slot = _page_gather(copies)

    page_at = lambda p: buf[slot, p].reshape(2 * KVH * DH, PAGE_SIZE)
    k, v = _compress_pages(page_at, n_pages, perm_ref[...], rows_scr, w1_ref, bias_ref, w2_ref)
    k = _rope128(k, cos_ref[...], sa_ref[...], sb_ref[...])
    nc = n_pages * CHUNKS_PER_PAGE
    rows = H_NSA * n_tok
    qpos = past_len + lax.broadcasted_iota(jnp.int32, (rows, 1), 0) % n_tok
    end_c = lax.broadcasted_iota(jnp.int32, (1, nc), 1) * CMP_STRIDE + (CMP_LEN - 1)
    mask = end_c <= qpos
    p = _masked_softmax(_dot_nt(q_ref[0], k.astype(bf16)), mask, axis=-1)
    o_ref[0] = _dot(p.astype(bf16), v.astype(bf16)) * gate_ref[0][:, 0:1]
    hi, lo = _split_bf16(p)
    gs = gs_ref[...]
    ph, pl_ = _split_bf16(_dot(gs, hi) + _dot(gs, lo))
    imp = _dot(ph, cov_ref[...]) + _dot(pl_, cov_ref[...])
    n_blk = imp.shape[1]
    jblk = lax.broadcasted_iota(jnp.int32, imp.shape, 1)
    forced = (jblk == 0) | (jblk == n_blk - 1)
    sel = _topk_rows_by_rank(jnp.where(forced, FORCE_SCORE, imp), SLC_TOP - 1)
    sel_ref[0] = _dot(gst_ref[...], sel.astype(bf16)).astype(bf16)


def _cmp_attn_sample(cache_t, page_table, w1_pairs, bias, w2_heads, q_blk, gate_rows, n_tok):
    b, n_pages = page_table.shape
    past_len = n_pages * PAGE_SIZE
    nc = past_len // CMP_STRIDE
    rows = H_NSA * n_tok
    n_blk = past_len // SLC_LEN
    end_pos = jnp.arange(nc) * CMP_STRIDE + (CMP_LEN - 1)
    cos, sa, sb = _rope_tables(end_pos)
    cov = _cover_t(n_blk, nc).T
    gs = jnp.asarray(_group_sum_matrix(n_tok), dtype=bf16)
    gst = gs.T
    perm = _row_gather_matrix()
    full = lambda a: pl.BlockSpec(a.shape, lambda i, pt: (0,) * a.ndim)
    seq = lambda a: pl.BlockSpec((1,) + a.shape[1:], lambda i, pt: (i,) + (0,) * (a.ndim - 1))
    grid_spec = pltpu.PrefetchScalarGridSpec(
        num_scalar_prefetch=1,
        grid=(b,),
        in_specs=[pl.BlockSpec(memory_space=pl.ANY), full(perm), full(w1_pairs), full(bias), full(w2_heads),
                  full(cos), full(sa), full(sb), seq(q_blk), seq(gate_rows), full(cov), full(gs), full(gst)],
        out_specs=(pl.BlockSpec((1, rows, LANES), lambda i, pt: (i, 0, 0)),
                   pl.BlockSpec((1, rows, n_blk), lambda i, pt: (i, 0, 0))),
        scratch_shapes=[pltpu.VMEM((2, n_pages) + cache_t.shape[1:], f32),
                        pltpu.VMEM((2, CMP_STRIDE, nc, KVH * DH), f32),
                        pltpu.SemaphoreType.DMA((2,))],
    )
    return pl.pallas_call(
        functools.partial(_cmp_attn_sample_kernel, n_pages=n_pages, n_tok=n_tok, past_len=past_len),
        grid_spec=grid_spec,
        out_shape=(jax.ShapeDtypeStruct((b, rows, LANES), f32),
                   jax.ShapeDtypeStruct((b, rows, n_blk), bf16)),
        compiler_params=_cparams(("arbitrary",)),
        name="cmp_attn_sample",
    )(page_table, cache_t, perm, w1_pairs, bias, w2_heads, cos, sa, sb, q_blk, gate_rows, cov, gs, gst)


def _slc_win_sample_kernel(pt_ref, cache_ref, q_ref, sel_ref, e_ref, snew_ref, cw_ref, wnew_ref, wnewt_ref,
                           gate_ref, ocmp_ref, zn_ref, o_ref, wout_ref, buf, sem, *, n_pages, n_tok, win_off):
    def copies(seq, slot):
        return [pltpu.make_async_copy(cache_ref.at[pt_ref[seq, p], c],
                                      buf.at[slot, c, :, pl.ds(p * PAGE_SIZE, PAGE_SIZE)], sem.at[slot])
                for p in range(n_pages) for c in range(2)]

    slot = _page_gather(copies)
    rows = H_NSA * n_tok
    q = q_ref[0]
    tok = lax.broadcasted_iota(jnp.int32, (rows, 1), 0) % n_tok
    new_i = lax.broadcasted_iota(jnp.int32, (1, snew_ref.shape[1]), 1)
    new_ok = (new_i <= tok) & (new_i < n_tok)

    def attend(keys_t, vals_t, allowed, new_ref):
        s = jnp.where(allowed, _dot(q, keys_t), NEG)
        k_new = new_ref[0][:, :LANES].astype(bf16)
        v_new = new_ref[0][:, LANES:].astype(bf16)
        s_new = jnp.where(new_ok, _dot_nt(q, k_new), NEG)
        m = jnp.maximum(jnp.max(s, axis=-1, keepdims=True), jnp.max(s_new, axis=-1, keepdims=True))
        p = jnp.where(allowed, jnp.exp(s - m), 0.0)
        p_new = jnp.where(new_ok, jnp.exp(s_new - m), 0.0)
        l = jnp.sum(p, axis=-1, keepdims=True) + jnp.sum(p_new, axis=-1, keepdims=True)
        return (_dot_nt(p.astype(bf16), vals_t) + _dot(p_new.astype(bf16), v_new)) / l

    picked = _dot(sel_ref[0], e_ref[...]) > 0.5
    o_slc = attend(buf[slot, 0].astype(bf16), buf[slot, 1].astype(bf16), picked, snew_ref)

    wbuf = cw_ref.shape[-1]
    win_i = lax.broadcasted_iota(jnp.int32, (1, wbuf), 1)
    o_win = attend(cw_ref[0, 0].astype(bf16), cw_ref[0, 1].astype(bf16), win_i > tok + win_off, wnew_ref)

    lane = lax.broadcasted_iota(jnp.int32, (1, LANES), 1)
    for c in range(2):
        shifted = pltpu.roll(cw_ref[0, c], wbuf - n_tok, 1)
        tail = jnp.where(lane >= LANES - n_tok, wnewt_ref[0, c], shifted[:, wbuf - LANES:])
        wout_ref[0, c] = jnp.concatenate([shifted[:, :wbuf - LANES], tail], axis=-1)

    gate = gate_ref[0]
    o = ocmp_ref[0] + gate[:, 1:2] * o_slc + gate[:, 2:3] * o_win
    second_kvh = lax.broadcasted_iota(jnp.int32, (rows, 1), 0) >= G_NSA * n_tok
    o_ref[0] = jnp.where(second_kvh, o[:, DH:], o[:, :DH]) * zn_ref[0]


def _slc_win_sample(cache_t, page_table, q_blk, sel_rows, kvs_new, win_t, kvw_new, kvw_new_t, gate_rows,
                    o_cmp, zn_rows, n_tok):
    b, n_pages = page_table.shape
    rows = H_NSA * n_tok
    past_len = n_pages * PAGE_SIZE
    wbuf = win_t.shape[-1]
    e = _block_expand(sel_rows.shape[-1], past_len)
    full = lambda a: pl.BlockSpec(a.shape, lambda i, pt: (0,) * a.ndim)
    seq = lambda a: pl.BlockSpec((1,) + a.shape[1:], lambda i, pt: (i,) + (0,) * (a.ndim - 1))
    grid_spec = pltpu.PrefetchScalarGridSpec(
        num_scalar_prefetch=1,
        grid=(b,),
        in_specs=[pl.BlockSpec(memory_space=pl.ANY), seq(q_blk), seq(sel_rows), full(e), seq(kvs_new),
                  seq(win_t), seq(kvw_new), seq(kvw_new_t), seq(gate_rows), seq(o_cmp), seq(zn_rows)],
        out_specs=(pl.BlockSpec((1, rows, DH), lambda i, pt: (i, 0, 0)), seq(win_t)),
        scratch_shapes=[pltpu.VMEM((2, 2, KVH * DH, past_len), f32), pltpu.SemaphoreType.DMA((2,))],
    )
    return pl.pallas_call(
        functools.partial(_slc_win_sample_kernel, n_pages=n_pages, n_tok=n_tok, win_off=wbuf - WINDOW),
        grid_spec=grid_spec,
        out_shape=(jax.ShapeDtypeStruct((b, rows, DH), f32), jax.ShapeDtypeStruct(win_t.shape, f32)),
        compiler_params=_cparams(("arbitrary",)),
        name="slc_win_sample",
    )(page_table, cache_t, q_blk, sel_rows, e, kvs_new, win_t, kvw_new, kvw_new_t, gate_rows, o_cmp, zn_rows)


def _gla_sample(gq, gk, la, gv, zg, state, gain, b, n_tok):
    rows = 8
    pad = lambda a: jnp.pad(a.reshape(b, n_tok, -1), ((0, 0), (0, rows - n_tok), (0, 0))).reshape(b * rows, -1)
    y, s_new = _gla(pad(gq), pad(gk), pad(gv), pad(la), pad(zg), gain, state, b, tl=rows, chunk=rows)
    return y.reshape(b, rows, GLA_WIDTH)[:, :n_tok].reshape(b * n_tok, GLA_WIDTH), s_new


def _sample_path(x, cache_c, cache_s, cache_w, state, page_table, wts):
    (norm_g, w_pack, wa2_pad, ba, w1_pairs, cmp_bias, w2_heads, gla_gain, w_out, out_gain) = wts
    b, n_tok, _ = x.shape
    n = b * n_tok
    n_pages = page_table.shape[1]
    past_len = n_pages * PAGE_SIZE
    assert n_tok <= 8 and past_len % SLC_LEN == 0 and past_len // SLC_LEN <= LANES
    pos = past_len + jnp.arange(n) % n_tok
    (q_hm, kvc, kvs, kvw, gate, zn, gq, gk, gv, la, zg) = _inproj(
        x.reshape(1, n, D_MODEL), pos, norm_g, w_pack, wa2_pad, ba, tm=n)

    rows = H_NSA * n_tok
    q_rows = q_hm[0].reshape(H_NSA, b, n_tok, DH).transpose(1, 0, 2, 3)
    zero = jnp.zeros_like(q_rows[:, :G_NSA])
    q_blk = jnp.concatenate([jnp.concatenate([q_rows[:, :G_NSA], zero], axis=-1),
                             jnp.concatenate([zero, q_rows[:, G_NSA:]], axis=-1)], axis=1).reshape(b, rows, LANES)
    gate_rows = gate[:, :3 * H_NSA].reshape(b, n_tok, 3, H_NSA).transpose(0, 3, 1, 2).reshape(b, rows, 3)
    gate_rows = jnp.pad(gate_rows, ((0, 0), (0, 0), (0, LANES - 3)))
    pad_new = lambda a: jnp.pad(a.reshape(b, n_tok, KV_ROW), ((0, 0), (0, 8 - n_tok), (0, 0)))

    kvw_new_t = kvw.reshape(b, n_tok, 2, KVH * DH).transpose(0, 2, 3, 1)
    kvw_new_t = jnp.pad(kvw_new_t, ((0, 0), (0, 0), (0, 0), (LANES - n_tok, 0)))

    o_cmp, sel_rows = _cmp_attn_sample(_pages_native(cache_c), page_table, w1_pairs, cmp_bias, w2_heads,
                                       q_blk, gate_rows, n_tok)
    o_rows, win_t = _slc_win_sample(_pages_native(cache_s), page_table, q_blk, sel_rows, pad_new(kvs),
                                    _pages_native(cache_w), pad_new(kvw), kvw_new_t, gate_rows, o_cmp,
                                    zn.reshape(b, n_tok, H_NSA, DH).transpose(0, 2, 1, 3).reshape(b, rows, DH),
                                    n_tok)
    y_nsa = o_rows.reshape(b, H_NSA, n_tok, DH).transpose(0, 2, 1, 3).reshape(n, NSA_WIDTH)

    y_gla, s_new = _gla_sample(gq, gk, la, gv, zg, state, gla_gain, b, n_tok)
    y = _outproj(x.reshape(n, D_MODEL), y_nsa, y_gla, w_out, out_gain, tm=n)
    kv5 = lambda a: a.reshape(b, n_tok, 2, KVH, DH)
    win_new = win_t.reshape(b, 2, KVH, DH, -1).transpose(0, 4, 1, 2, 3)
    return (y.reshape(b, n_tok, D_MODEL), kv5(kvc), kv5(kvs), win_new, s_new)


def kernel(x_prompt, x_sample, cache_cmp_kv, cache_slc_kv, cache_win_kv, state_gla, page_table,
           norm_in_gain, w_in, cmp_pe, cmp_w1, cmp_b1, cmp_w2, gla_wa2, gla_ba, gla_norm_gain,
           w_out, norm_out_gain):
    assert w_in.shape[0] == 1, "single-layer step"
    wts = (norm_in_gain[0], _pack_w_in(w_in[0]), _pad_wa2(gla_wa2[0]), gla_ba[0],
           _cmp_w1_pairs(cmp_w1[0]), _cmp_bias(cmp_pe[0], cmp_w1[0], cmp_b1[0]), _cmp_w2_heads(cmp_w2[0]),
           gla_norm_gain[0], w_out[0].astype(bf16), norm_out_gain)
    yp, cmp_p, slc_p, win_p, gla_p = _prompt_path(x_prompt, wts, *_pack_w_in_prompt(w_in[0]))
    ys, cmp_s, slc_s, win_s, gla_s = _sample_path(
        x_sample, cache_cmp_kv[0], cache_slc_kv[0], cache_win_kv[0], state_gla[0], page_table, wts)
    return (yp, ys, cmp_p[None], cmp_s[None], slc_p[None], slc_s[None], win_p[None], win_s[None],
            gla_p[None], gla_s[None])
```

```python
import functools

import numpy as np
import jax
import jax.numpy as jnp
from jax import lax
from jax.experimental import pallas as pl
from jax.experimental.pallas import tpu as pltpu

f32 = jnp.float32
bf16 = jnp.bfloat16

D_MODEL = 1024
DH = 64
H_NSA = 8
KVH = 2
G_NSA = 4
NSA_WIDTH = H_NSA * DH
CMP_LEN = 32
CMP_STRIDE = 16
CMP_HID = 128
SLC_LEN = 64
SLC_TOP = 16
WINDOW = 512
ROT_HALF = 8
ROPE_THETA = 500000.0
FORCE_SCORE = 1.0e4
H_GLA = 4
DK = 64
DV = 128
GLA_WIDTH = H_GLA * DV
GLA_LR = 16
GLA_TAU = 16.0
GLA_CHUNK = 64
EPS = 1e-6
NEG = -1.0e30
LOG2E = 1.4426950408889634
PAGE_SIZE = 128
KV_ROW = 2 * KVH * DH

IN_SIZES = (H_NSA * DH, KV_ROW, KV_ROW, KV_ROW, 3 * H_NSA, NSA_WIDTH,
            H_GLA * DK, H_GLA * DK, H_GLA * DV, GLA_LR, GLA_WIDTH)
IN_OFFSETS = [0] + [int(v) for v in np.cumsum(IN_SIZES)]

LANES = 128
VMEM_LIMIT = 56 * 1024 * 1024

P_Q, P_KVC, P_KVS, P_KVW, P_ZN, P_GQ, P_GK, P_GV, P_ZG, P_MISC = (
    0, 512, 768, 1024, 1280, 1792, 2048, 2304, 2816, 3328)
D_PACK = P_MISC + LANES
MISC_GATE = 0
MISC_GLR = 32


def _cparams(sem):
    return pltpu.CompilerParams(dimension_semantics=sem, vmem_limit_bytes=VMEM_LIMIT)


def _sigmoid(x):
    return 1.0 / (1.0 + jnp.exp(-x))


def _silu(x):
    return x * _sigmoid(x)


def _log_sigmoid(x):
    return jnp.minimum(x, 0.0) - jnp.log1p(jnp.exp(-jnp.abs(x)))


def _dot(a, b):
    return jnp.dot(a, b, preferred_element_type=f32)


def _dot_nt(a, b):
    return lax.dot_general(a, b, (((1,), (1,)), ((), ())), preferred_element_type=f32)


def _dot_tn(a, b):
    return lax.dot_general(a, b, (((0,), (0,)), ((), ())), preferred_element_type=f32)


def _rope_tables(pos):
    n = pos.shape[0]
    inv = ROPE_THETA ** (-(jnp.arange(ROT_HALF, dtype=f32) / ROT_HALF))
    ang = pos.astype(f32)[:, None] * inv[None, :]
    cos, sin = jnp.cos(ang), jnp.sin(ang)
    z8 = jnp.zeros((n, ROT_HALF), f32)
    rest = DH - 2 * ROT_HALF
    c64 = jnp.concatenate([cos, cos, jnp.ones((n, rest), f32)], axis=-1)
    sa64 = jnp.concatenate([-sin, z8, jnp.zeros((n, rest), f32)], axis=-1)
    sb64 = jnp.concatenate([z8, sin, jnp.zeros((n, rest), f32)], axis=-1)
    tile = lambda t: jnp.concatenate([t, t], axis=-1)
    return tile(c64), tile(sa64), tile(sb64)


def _rope128(v, cos, sa, sb):
    return v * cos + pltpu.roll(v, LANES - ROT_HALF, 1) * sa + pltpu.roll(v, ROT_HALF, 1) * sb


def _inproj_kernel(x_ref, g_ref, w_ref, cos_ref, sa_ref, sb_ref, wa2_ref, ba_ref,
                   q_ref, kvc_ref, kvs_ref, kvw_ref, gate_ref, zn_ref, gq_ref, gk_ref, gv_ref, la_ref, zg_ref):
    x = x_ref[...]
    ms = jnp.mean(x * x, axis=-1, keepdims=True)
    hn = (x * lax.rsqrt(ms + EPS) * g_ref[...]).astype(bf16)
    cos, sa, sb = cos_ref[...], sa_ref[...], sb_ref[...]

    def proj(off, width):
        return _dot(hn, w_ref[:, off:off + width])

    qp = proj(P_Q, NSA_WIDTH)
    for c in range(NSA_WIDTH // LANES):
        r = _rope128(qp[:, c * LANES:(c + 1) * LANES], cos, sa, sb) * (DH ** -0.5)
        q_ref[0, 2 * c] = r[:, :DH].astype(bf16)
        q_ref[0, 2 * c + 1] = r[:, DH:].astype(bf16)

    kvc_ref[...] = proj(P_KVC, KV_ROW)

    for off, kv_ref in ((P_KVS, kvs_ref), (P_KVW, kvw_ref)):
        p = proj(off, KV_ROW)
        kv_ref[:, :LANES] = _rope128(p[:, :LANES], cos, sa, sb)
        kv_ref[:, LANES:] = p[:, LANES:]

    zn_ref[...] = _silu(proj(P_ZN, NSA_WIDTH))
    gq_ref[...] = proj(P_GQ, H_GLA * DK) * (DK ** -0.5)
    gk_ref[...] = proj(P_GK, H_GLA * DK)
    gv_ref[...] = proj(P_GV, H_GLA * DV)
    zg_ref[...] = _silu(proj(P_ZG, GLA_WIDTH))

    misc = proj(P_MISC, LANES)
    gate_ref[...] = _sigmoid(misc)
    xa = _dot(misc.astype(bf16), wa2_ref[...]) + ba_ref[...]
    la_ref[...] = _log_sigmoid(xa) / GLA_TAU


def _pack_w_in(w_in):
    o = IN_OFFSETS
    seg = lambda i: w_in[:, o[i]:o[i + 1]]
    misc = jnp.zeros((D_MODEL, LANES), w_in.dtype)
    misc = misc.at[:, MISC_GATE:MISC_GATE + 3 * H_NSA].set(seg(4))
    misc = misc.at[:, MISC_GLR:MISC_GLR + GLA_LR].set(seg(9))
    cols = [seg(0), seg(1), seg(2), seg(3), seg(5), seg(6), seg(7), seg(8), seg(10), misc]
    return jnp.concatenate(cols, axis=1).astype(bf16)


def _pad_wa2(wa2):
    pad = jnp.zeros((LANES, H_GLA * DK), wa2.dtype).at[MISC_GLR:MISC_GLR + GLA_LR].set(wa2)
    return pad.astype(bf16)


def _inproj(x, pos, norm_g, w_pack, wa2_pad, ba, tm):
    bk, lk, _ = x.shape
    n = bk * lk
    nt = lk // tm
    cos, sa, sb = _rope_tables(pos)
    tok = lambda w: pl.BlockSpec((tm, w), lambda i: (i, 0))
    tab = pl.BlockSpec((tm, LANES), lambda i: (i % nt, 0))
    full = lambda a: pl.BlockSpec(a.shape, lambda i: (0,) * a.ndim)
    hm = lambda h: pl.BlockSpec((1, h, tm, DH), lambda i: (i // nt, 0, i % nt, 0))
    g2 = norm_g.reshape(1, D_MODEL)
    ba2 = ba.reshape(1, H_GLA * DK)
    out_shape = (
        jax.ShapeDtypeStruct((bk, H_NSA, lk, DH), bf16),
        jax.ShapeDtypeStruct((n, KV_ROW), f32),
        jax.ShapeDtypeStruct((n, KV_ROW), f32),
        jax.ShapeDtypeStruct((n, KV_ROW), f32),
        jax.ShapeDtypeStruct((n, LANES), f32),
        jax.ShapeDtypeStruct((n, NSA_WIDTH), f32),
        jax.ShapeDtypeStruct((n, H_GLA * DK), f32),
        jax.ShapeDtypeStruct((n, H_GLA * DK), f32),
        jax.ShapeDtypeStruct((n, H_GLA * DV), f32),
        jax.ShapeDtypeStruct((n, H_GLA * DK), f32),
        jax.ShapeDtypeStruct((n, GLA_WIDTH), f32),
    )
    out_specs = (hm(H_NSA), tok(KV_ROW), tok(KV_ROW), tok(KV_ROW),
                 tok(LANES), tok(NSA_WIDTH), tok(H_GLA * DK), tok(H_GLA * DK), tok(H_GLA * DV),
                 tok(H_GLA * DK), tok(GLA_WIDTH))
    return pl.pallas_call(
        _inproj_kernel,
        grid=(n // tm,),
        in_specs=[tok(D_MODEL), full(g2), full(w_pack), tab, tab, tab, full(wa2_pad), full(ba2)],
        out_specs=out_specs,
        out_shape=out_shape,
        compiler_params=_cparams(("parallel",)),
        name="inproj",
    )(x.reshape(n, D_MODEL), g2, w_pack, cos, sa, sb, wa2_pad, ba2)


PT_Q, PT_KVC, PT_KVS, PT_KVW, PT_GATE = 0, 512, 768, 1024, 1280
PT_ROWS = PT_GATE + 32
PK_ZN, PK_GQ, PK_GK, PK_GV, PK_ZG, PK_MISC = 0, 512, 768, 1024, 1536, 2048
PK_COLS = PK_MISC + LANES


def _inproj_prompt_kernel(x_ref, g_ref, wt_ref, w_ref, cos_ref, sin_ref, wa2_ref, ba_ref,
                          qt_ref, kvct_ref, kvst_ref, kvwt_ref, ks_ref, kw_ref, vst_ref, vwt_ref,
                          gatet_ref, zn_ref, gq_ref, gk_ref, gv_ref, la_ref, zg_ref):
    x = x_ref[...]
    ms = jnp.mean(x * x, axis=-1, keepdims=True)
    hn = (x * lax.rsqrt(ms + EPS) * g_ref[...]).astype(bf16)
    cos_t, sin_t = cos_ref[...], sin_ref[...]

    all_t = _dot_nt(wt_ref[...], hn)

    def proj_t(off, rows):
        return all_t[off:off + rows]

    def rope_rows(v):
        x1, x2 = v[:ROT_HALF], v[ROT_HALF:2 * ROT_HALF]
        return jnp.concatenate([x1 * cos_t - x2 * sin_t, x2 * cos_t + x1 * sin_t, v[2 * ROT_HALF:]], axis=0)

    q_t = proj_t(PT_Q, NSA_WIDTH)
    for h in range(H_NSA):
        qt_ref[0, h] = (rope_rows(q_t[h * DH:(h + 1) * DH]) * (DH ** -0.5 * LOG2E)).astype(bf16)

    kvct_ref[0] = proj_t(PT_KVC, KV_ROW)

    for off, kvt_ref, k_ref, vt_ref in ((PT_KVS, kvst_ref, ks_ref, vst_ref), (PT_KVW, kvwt_ref, kw_ref, vwt_ref)):
        t = proj_t(off, KV_ROW)
        k_t = jnp.concatenate([rope_rows(t[h * DH:(h + 1) * DH]) for h in range(KVH)], axis=0)
        kvt_ref[0, :LANES] = k_t
        kvt_ref[0, LANES:] = t[LANES:]
        k_tok = k_t.T
        for h in range(KVH):
            k_ref[0, h] = k_tok[:, h * DH:(h + 1) * DH].astype(bf16)
            vt_ref[0, h] = t[LANES + h * DH:LANES + (h + 1) * DH].astype(bf16)

    gatet_ref[0] = _sigmoid(proj_t(PT_GATE, PT_ROWS - PT_GATE))

    def proj(off, width):
        return _dot(hn, w_ref[:, off:off + width])

    zn_ref[...] = _silu(proj(PK_ZN, NSA_WIDTH))
    gq_ref[...] = proj(PK_GQ, H_GLA * DK) * (DK ** -0.5)
    gk_ref[...] = proj(PK_GK, H_GLA * DK)
    gv_ref[...] = proj(PK_GV, H_GLA * DV)
    zg_ref[...] = _silu(proj(PK_ZG, GLA_WIDTH))
    misc = proj(PK_MISC, LANES)
    xa = _dot(misc.astype(bf16), wa2_ref[...]) + ba_ref[...]
    la_ref[...] = _log_sigmoid(xa) / GLA_TAU


def _pack_w_in_prompt(w_in):
    o = IN_OFFSETS
    seg = lambda i: w_in[:, o[i]:o[i + 1]]
    gate_t = jnp.zeros((PT_ROWS - PT_GATE, D_MODEL), w_in.dtype).at[:3 * H_NSA].set(seg(4).T)
    w_t = jnp.concatenate([seg(0).T, seg(1).T, seg(2).T, seg(3).T, gate_t], axis=0).astype(bf16)
    misc = jnp.zeros((D_MODEL, LANES), w_in.dtype).at[:, MISC_GLR:MISC_GLR + GLA_LR].set(seg(9))
    w_tok = jnp.concatenate([seg(5), seg(6), seg(7), seg(8), seg(10), misc], axis=1).astype(bf16)
    return w_t, w_tok


def _inproj_prompt(x, norm_g, w_t, w_tok, wa2_pad, ba, tm):
    b, lq, _ = x.shape
    n = b * lq
    nt = lq // tm
    inv = ROPE_THETA ** (-(jnp.arange(ROT_HALF, dtype=f32) / ROT_HALF))
    ang = inv[:, None] * jnp.arange(lq).astype(f32)[None, :]
    cos_t, sin_t = jnp.cos(ang), jnp.sin(ang)
    tok = lambda w: pl.BlockSpec((tm, w), lambda i: (i, 0))
    tab = pl.BlockSpec((ROT_HALF, tm), lambda i: (0, i % nt))
    full = lambda a: pl.BlockSpec(a.shape, lambda i: (0,) * a.ndim)
    feat = lambda r: pl.BlockSpec((1, r, tm), lambda i: (i // nt, 0, i % nt))
    headf = lambda h: pl.BlockSpec((1, h, DH, tm), lambda i: (i // nt, 0, 0, i % nt))
    headt = lambda h: pl.BlockSpec((1, h, tm, DH), lambda i: (i // nt, 0, i % nt, 0))
    g2 = norm_g.reshape(1, D_MODEL)
    ba2 = ba.reshape(1, H_GLA * DK)
    sds = jax.ShapeDtypeStruct
    out_shape = (
        sds((b, H_NSA, DH, lq), bf16),
        sds((b, KV_ROW, lq), f32), sds((b, KV_ROW, lq), f32), sds((b, KV_ROW, lq), f32),
        sds((b, KVH, lq, DH), bf16), sds((b, KVH, lq, DH), bf16),
        sds((b, KVH, DH, lq), bf16), sds((b, KVH, DH, lq), bf16),
        sds((b, PT_ROWS - PT_GATE, lq), f32),
        sds((n, NSA_WIDTH), f32), sds((n, H_GLA * DK), f32), sds((n, H_GLA * DK), f32),
        sds((n, H_GLA * DV), f32), sds((n, H_GLA * DK), f32), sds((n, GLA_WIDTH), f32),
    )
    out_specs = (headf(H_NSA), feat(KV_ROW), feat(KV_ROW), feat(KV_ROW), headt(KVH), headt(KVH),
                 headf(KVH), headf(KVH), feat(PT_ROWS - PT_GATE),
                 tok(NSA_WIDTH), tok(H_GLA * DK), tok(H_GLA * DK), tok(H_GLA * DV), tok(H_GLA * DK),
                 tok(GLA_WIDTH))
    return pl.pallas_call(
        _inproj_prompt_kernel,
        grid=(n // tm,),
        in_specs=[tok(D_MODEL), full(g2), full(w_t), full(w_tok), tab, tab, full(wa2_pad), full(ba2)],
        out_specs=out_specs,
        out_shape=out_shape,
        compiler_params=_cparams(("parallel",)),
        name="inproj_prompt",
    )(x.reshape(n, D_MODEL), g2, w_t, w_tok, cos_t, sin_t, wa2_pad, ba2)


HID_W = 2 * KVH * CMP_HID
CHUNKS_PER_PAGE = PAGE_SIZE // CMP_STRIDE
PAGE_GROUP = 16


def _cmp_w1_pairs(cmp_w1):
    ratio = CMP_LEN // CMP_STRIDE
    w1r = cmp_w1.reshape(2, ratio, CMP_STRIDE // 2, 2, DH, CMP_HID)
    eye = jnp.eye(KVH, dtype=cmp_w1.dtype)
    big = jnp.einsum('crpjdh,kK->cpjKdrkh', w1r, eye)
    return big.reshape(2, CMP_STRIDE // 2, 2 * KVH * DH, ratio * KVH * CMP_HID).astype(bf16)


def _cmp_w2_heads(cmp_w2):
    eye = jnp.eye(KVH, dtype=cmp_w2.dtype)
    return jnp.einsum('chd,kK->ckhKd', cmp_w2, eye).reshape(2, KVH * CMP_HID, KVH * DH).astype(bf16)


def _row_gather_matrix():
    r = np.arange(PAGE_SIZE)
    src = (r % CHUNKS_PER_PAGE) * CMP_STRIDE + r // CHUNKS_PER_PAGE
    return jnp.asarray((r[None, :] == src[:, None]).astype(np.float32), dtype=bf16)


def _compress_pages(page_at, n_pages, perm, rows_scr, w1_ref, bias_ref, w2_ref):
    cpp = CHUNKS_PER_PAGE
    kd = KVH * DH

    def to_rows(g, carry):
        pages = [g * PAGE_GROUP + u for u in range(PAGE_GROUP)]
        gathered = [_dot_nt(perm, page_at(p).astype(bf16)) for p in pages]
        for p, t in zip(pages, gathered):
            r0 = pl.multiple_of(p * cpp, cpp)
            for c in range(2):
                for s in range(CMP_STRIDE):
                    rows_scr[c, s, pl.ds(r0, cpp), :] = t[s * cpp:(s + 1) * cpp, c * kd:(c + 1) * kd]
        return carry

    lax.fori_loop(0, n_pages // PAGE_GROUP, to_rows, 0)

    nc = n_pages * cpp
    kv = []
    for c in range(2):
        lhs = jnp.concatenate([rows_scr[c, s] for s in range(CMP_STRIDE)], axis=-1).astype(bf16)
        part = _dot(lhs, w1_ref[c].reshape(CMP_STRIDE * kd, w1_ref.shape[-1]))
        half = part.shape[1] // 2
        pre = part[:, :half] + pltpu.roll(part[:, half:], nc - 1, 0)
        hid = _silu(pre + bias_ref[:, c * half:(c + 1) * half])
        kv.append(_dot(hid.astype(bf16), w2_ref[c]))
    return kv


def _cmp_bias_kernel(pe_ref, w1_ref, b1_ref, o_ref):
    for c in range(2):
        o_ref[c] = _dot(pe_ref[c], w1_ref[c]) + b1_ref[c]


def _cmp_bias(cmp_pe, cmp_w1, cmp_b1):
    pe = jnp.broadcast_to(cmp_pe.reshape(2, 1, CMP_LEN * DH), (2, 8, CMP_LEN * DH))
    b1 = jnp.broadcast_to(cmp_b1.reshape(2, 1, CMP_HID), (2, 8, CMP_HID))
    out = pl.pallas_call(
        _cmp_bias_kernel,
        out_shape=jax.ShapeDtypeStruct((2, 8, CMP_HID), f32),
        name="cmp_bias",
    )(pe, cmp_w1, b1)
    row = out[:, 0, :]
    return jnp.broadcast_to(row[:, None, :], (2, KVH, CMP_HID)).reshape(1, HID_W)


def _masked_softmax(s, mask, axis, exp=jnp.exp):
    s = jnp.where(mask, s, NEG)
    e = exp(s - jnp.max(s, axis=axis, keepdims=True))
    return jnp.where(mask, e * (1.0 / jnp.sum(e, axis=axis, keepdims=True)), 0.0)


def _split_bf16(x):
    hi = x.astype(bf16)
    return hi, (x - hi.astype(f32)).astype(bf16)


def _topk_mask(score, k, axis):
    n = score.shape[axis]
    idx = lax.broadcasted_iota(jnp.int32, score.shape, axis)
    sel = jnp.zeros(score.shape, f32)
    for _ in range(k):
        m = jnp.max(score, axis=axis, keepdims=True)
        first = jnp.min(jnp.where(score == m, idx, n), axis=axis, keepdims=True)
        pick = idx == first
        sel = jnp.where(pick, 1.0, sel)
        score = jnp.where(pick, NEG, score)
    return sel


def _topk_rows_by_rank(score, k):
    r, n = score.shape
    cols = jnp.concatenate([score, jnp.zeros((n - r, n), f32)], axis=0).T
    earlier = jnp.where(lax.broadcasted_iota(jnp.int32, (n, n), 0) < lax.broadcasted_iota(jnp.int32, (n, n), 1),
                        1.0, 0.0)
    rows = []
    for c in range(r):
        col, row = cols[:, c:c + 1], score[c:c + 1, :]
        before = jnp.where(col > row, 1.0, jnp.where(col == row, earlier, 0.0))
        rows.append(jnp.where(jnp.sum(before, axis=0, keepdims=True) < k, 1.0, 0.0))
    return jnp.concatenate(rows, axis=0)


def _cover_t(n_slc, n_chunk):
    start = np.arange(n_chunk)[None, :] * CMP_STRIDE
    j = np.arange(n_slc)[:, None]
    cov = (start < (j + 1) * SLC_LEN) & (start + CMP_LEN > j * SLC_LEN)
    return jnp.asarray(cov.astype(np.float32), dtype=bf16)


def _cmp_attn_kernel(kvct_ref, perm_ref, w1_ref, bias_ref, w2_ref, cos_ref, sa_ref, sb_ref, qt_ref, gatet_ref,
                     cov_ref, ot_ref, selt_ref, rows_scr, kc_ref, vct_ref, *, tq):
    t = pl.program_id(1)

    @pl.when(t == 0)
    def _():
        page_at = lambda p: kvct_ref[0, :, pl.ds(pl.multiple_of(p * PAGE_SIZE, PAGE_SIZE), PAGE_SIZE)]
        k, v = _compress_pages(page_at, kvct_ref.shape[2] // PAGE_SIZE, perm_ref[...], rows_scr,
                               w1_ref, bias_ref, w2_ref)
        k = _rope128(k, cos_ref[...], sa_ref[...], sb_ref[...])
        v_t = v.T
        for h in range(KVH):
            kc_ref[h] = k[:, h * DH:(h + 1) * DH].astype(bf16)
            vct_ref[h] = v_t[h * DH:(h + 1) * DH].astype(bf16)

    nc = kc_ref.shape[1]
    nb = cov_ref.shape[0]
    qpos = t * tq + lax.broadcasted_iota(jnp.int32, (1, tq), 1)
    end_pos = lax.broadcasted_iota(jnp.int32, (nc, 1), 0) * CMP_STRIDE + (CMP_LEN - 1)
    mask_t = end_pos <= qpos
    gate_t = gatet_ref[0]
    cov = cov_ref[...]

    jblk = lax.broadcasted_iota(jnp.int32, (nb, tq), 0)
    qblk = qpos // SLC_LEN
    valid = jblk <= qblk
    forced = (jblk == 0) | (jblk == qblk) | (jblk == qblk - 1)

    heads = range(H_NSA)
    s_t = [_dot(kc_ref[h // G_NSA], qt_ref[0, h]) for h in heads]
    p_t = [_masked_softmax(s, mask_t, axis=0, exp=jnp.exp2) for s in s_t]
    for h in heads:
        ot_ref[0, h * DH:(h + 1) * DH] = _dot(vct_ref[h // G_NSA], p_t[h].astype(bf16)) * gate_t[h:h + 1]
    scores = []
    for kh in range(KVH):
        psum = functools.reduce(lambda a, b: a + b, p_t[kh * G_NSA:(kh + 1) * G_NSA])
        hi, lo = _split_bf16(psum)
        imp = _dot(cov, hi) + _dot(cov, lo)
        scores.append(jnp.where(valid, jnp.where(forced, FORCE_SCORE, imp), -1.0))
    picked = _topk_mask(jnp.concatenate(scores, axis=1), SLC_TOP, axis=0)
    for kh in range(KVH):
        sel_t = jnp.where(valid & (picked[:, kh * tq:(kh + 1) * tq] > 0.5), 0.0, NEG)
        if nb < LANES:
            sel_t = jnp.concatenate([sel_t, jnp.full((LANES - nb, tq), NEG, f32)], axis=0)
        selt_ref[0, kh] = sel_t


def _cmp_attn(kvc_t, w1_pairs, bias, w2_heads, q_t, gate_t, tq):
    b, _, lq = kvc_t.shape
    assert lq % (PAGE_SIZE * PAGE_GROUP) == 0
    nc = lq // CMP_STRIDE
    n_slc = lq // SLC_LEN
    perm = _row_gather_matrix()
    end_pos = jnp.arange(nc) * CMP_STRIDE + (CMP_LEN - 1)
    cos, sa, sb = _rope_tables(end_pos)
    cov = _cover_t(n_slc, nc)
    full = lambda a: pl.BlockSpec(a.shape, lambda i, t: (0,) * a.ndim)
    return pl.pallas_call(
        functools.partial(_cmp_attn_kernel, tq=tq),
        grid=(b, lq // tq),
        in_specs=[pl.BlockSpec((1, KV_ROW, lq), lambda i, t: (i, 0, 0)),
                  full(perm), full(w1_pairs), full(bias), full(w2_heads), full(cos), full(sa), full(sb),
                  pl.BlockSpec((1, H_NSA, DH, tq), lambda i, t: (i, 0, 0, t)),
                  pl.BlockSpec((1, gate_t.shape[1], tq), lambda i, t: (i, 0, t)),
                  full(cov)],
        out_specs=(pl.BlockSpec((1, NSA_WIDTH, tq), lambda i, t: (i, 0, t)),
                   pl.BlockSpec((1, KVH, LANES, tq), lambda i, t: (i, 0, 0, t))),
        out_shape=(jax.ShapeDtypeStruct((b, NSA_WIDTH, lq), f32),
                   jax.ShapeDtypeStruct((b, KVH, LANES, lq), f32)),
        scratch_shapes=[pltpu.VMEM((2, CMP_STRIDE, nc, KVH * DH), f32),
                        pltpu.VMEM((KVH, nc, DH), bf16), pltpu.VMEM((KVH, DH, nc), bf16)],
        compiler_params=_cparams(("parallel", "arbitrary")),
        name="cmp_attn",
    )(kvc_t, perm, w1_pairs, bias, w2_heads, cos, sa, sb, q_t, gate_t, cov)


def _block_expand(n_slc_pad, n_keys):
    e = (np.arange(n_keys)[None, :] // SLC_LEN) == np.arange(n_slc_pad)[:, None]
    return jnp.asarray(e.astype(np.float32), dtype=bf16)


def _mask_bias_t(allowed):
    bias = jnp.where(allowed, 0.0, NEG)
    return jnp.concatenate([bias] * G_NSA, axis=1)


def _sublane_fold(x, op):
    return functools.reduce(op, [x[r:r + 8] for r in range(0, x.shape[0], 8)])


def _slc_win_kernel(qt_ref, ks_ref, vst_ref, kw_ref, vwt_ref, selt_ref, gatet_ref, ocmpt_ref, zn_ref,
                    o_ref, s_a, s_b, m_a, m_b, ws_a, ws_b, wm_a, wm_b, acc_ref, wacc_ref, *, tq, tk, nt):
    t = pl.program_id(2)

    @pl.when(t == 0)
    def _():
        m_b[...] = jnp.full(m_b.shape, NEG, f32)
        ws_b[...] = jnp.zeros(ws_b.shape, f32)
        wm_b[...] = jnp.zeros(wm_b.shape, f32)

    step = functools.partial(_slc_win_step, qt_ref, ks_ref, vst_ref, kw_ref, vwt_ref, selt_ref, gatet_ref,
                             ocmpt_ref, zn_ref, o_ref, acc_ref, wacc_ref, tq=tq, tk=tk, nt=nt)

    @pl.when(t % 2 == 0)
    def _():
        step(s_a, s_b, m_a, m_b, ws_a, ws_b, wm_a, wm_b)

    @pl.when(t % 2 == 1)
    def _():
        step(s_b, s_a, m_b, m_a, ws_b, ws_a, wm_b, wm_a)


ONES_ROWS = 16


def _slc_win_step(qt_ref, ks_ref, vst_ref, kw_ref, vwt_ref, selt_ref, gatet_ref, ocmpt_ref, zn_ref, o_ref,
                  acc_ref, wacc_ref, s_cur, s_prev, m_cur, m_last, ws_cur, ws_prev, wm_cur, wm_last,
                  *, tq, tk, nt):
    kh = pl.program_id(1)
    t = pl.program_id(2)
    lq = ks_ref.shape[2]
    band = WINDOW + tq
    blocks_per_tile = tk // SLC_LEN
    key_tiles = lambda tile: (tile * tq + tq + tk - 1) // tk
    n_apply = jnp.where(t >= 1, key_tiles(t - 1), 0)
    n_plain = jnp.where(t < nt, (t * tq) // tk, 0)
    q0 = t * tq
    qpos = q0 + lax.broadcasted_iota(jnp.int32, (1, tq), 1)
    q_t = jnp.concatenate([qt_ref[0, g] for g in range(G_NSA)], axis=1)
    ones = jnp.ones((ONES_ROWS, tk), bf16)

    m_cur[...] = jnp.full(m_cur.shape, NEG, f32)
    m_prev = jnp.max(m_last[...], axis=0, keepdims=True)
    acc_ref[...] = jnp.zeros(acc_ref.shape, f32)

    def score(kt, causal):
        k0 = pl.multiple_of(kt * tk, tk)
        j0 = pl.multiple_of(kt * blocks_per_tile, blocks_per_tile)
        sel = selt_ref[0, 0, pl.ds(j0, blocks_per_tile), :]
        bias = jnp.concatenate([jnp.broadcast_to(sel[j:j + 1], (SLC_LEN, tq)) for j in range(blocks_per_tile)],
                               axis=0)
        if causal:
            kpos = k0 + lax.broadcasted_iota(jnp.int32, (tk, 1), 0)
            bias = jnp.where(kpos <= qpos, bias, NEG)
        s = _dot(ks_ref[0, 0, pl.ds(k0, tk), :], q_t) + jnp.concatenate([bias] * G_NSA, axis=1)
        s_cur[pl.ds(k0, tk), :] = s
        m_cur[...] = jnp.maximum(m_cur[...], _sublane_fold(s, jnp.maximum))

    def apply(kt):
        k0 = pl.multiple_of(kt * tk, tk)
        p = jnp.exp2(s_prev[pl.ds(k0, tk), :] - m_prev).astype(bf16)
        acc_ref[...] += _dot(jnp.concatenate([vst_ref[0, 0, :, pl.ds(k0, tk)], ones], axis=0), p)

    def both_body(kt, carry):
        score(kt, causal=False)
        apply(kt)
        return carry

    def apply_body(kt, carry):
        apply(kt)
        return carry

    def both2_body(i, carry):
        for u in range(2):
            score(2 * i + u, causal=False)
            apply(2 * i + u)
        return carry

    n_both = jnp.minimum(n_plain, n_apply)
    lax.fori_loop(0, n_both // 2, both2_body, 0)
    lax.fori_loop(n_both // 2 * 2, n_both, both_body, 0)
    lax.fori_loop(n_both, n_apply, apply_body, 0)
    score(jnp.minimum((t * tq) // tk, lq // tk - 1), causal=True)

    w0 = pl.multiple_of(jnp.clip(q0 - WINDOW, 0, lq - band), tq)
    w0_prev = pl.multiple_of(jnp.clip(q0 - tq - WINDOW, 0, lq - band), tq)
    wm_prev = jnp.max(wm_last[...], axis=0, keepdims=True)
    ones_w = jnp.ones((ONES_ROWS, tq), bf16)
    wmax, wacc = None, None
    for j in range(band // tq):
        rows = pl.ds(j * tq, tq)
        kpos = w0 + j * tq + lax.broadcasted_iota(jnp.int32, (tq, 1), 0)
        s = (_dot(kw_ref[0, 0, pl.ds(pl.multiple_of(w0 + j * tq, tq), tq), :], q_t)
             + _mask_bias_t((kpos <= qpos) & (kpos > qpos - WINDOW)))
        ws_cur[rows, :] = s
        fold = _sublane_fold(s, jnp.maximum)
        wmax = fold if wmax is None else jnp.maximum(wmax, fold)
        p = jnp.exp2(ws_prev[rows, :] - wm_prev).astype(bf16)
        v_ext = jnp.concatenate([vwt_ref[0, 0, :, pl.ds(pl.multiple_of(w0_prev + j * tq, tq), tq)], ones_w], axis=0)
        part = _dot(v_ext, p)
        wacc = part if wacc is None else wacc + part
    wm_cur[...] = wmax
    wacc_ref[...] = wacc

    @pl.when(t >= 1)
    def _():
        o_slc = acc_ref[:DH] / acc_ref[DH:DH + 1]
        o_win = wacc_ref[:DH] / wacc_ref[DH:DH + 1]
        gate_t = gatet_ref[0]
        row = lambda c, g: jnp.where(kh == 0, gate_t[c * H_NSA + g:c * H_NSA + g + 1],
                                     gate_t[c * H_NSA + G_NSA + g:c * H_NSA + G_NSA + g + 1])
        g_slc = jnp.concatenate([row(1, g) for g in range(G_NSA)], axis=1)
        g_win = jnp.concatenate([row(2, g) for g in range(G_NSA)], axis=1)
        o_t = g_slc * o_slc + g_win * o_win
        o_heads = jnp.concatenate([o_t[:, g * tq:(g + 1) * tq] for g in range(G_NSA)], axis=0)
        o_ref[0] = ((ocmpt_ref[0] + o_heads).T * zn_ref[...]).astype(o_ref.dtype)


def _slc_win(q_t, ks, vs_t, kw, vw_t, sel_t, gate_t, o_cmp_t, zn, tq, tk):
    b, _, _, lq = q_t.shape
    assert lq >= WINDOW + tq and lq % tk == 0 and tk % (8 * SLC_LEN) == 0
    gw = G_NSA * DH
    k_spec = pl.BlockSpec((1, 1, lq, DH), lambda i, k, t: (i, k, 0, 0))
    vt_spec = pl.BlockSpec((1, 1, DH, lq), lambda i, k, t: (i, k, 0, 0))
    lanes = G_NSA * tq
    band = WINDOW + tq
    nt = lq // tq
    scored = lambda t: jnp.minimum(t, nt - 1)
    drained = lambda t: jnp.maximum(t - 1, 0)
    return pl.pallas_call(
        functools.partial(_slc_win_kernel, tq=tq, tk=tk, nt=nt),
        grid=(b, KVH, nt + 1),
        in_specs=[pl.BlockSpec((1, G_NSA, DH, tq), lambda i, k, t: (i, k, 0, scored(t))),
                  k_spec, vt_spec, k_spec, vt_spec,
                  pl.BlockSpec((1, 1, LANES, tq), lambda i, k, t: (i, k, 0, scored(t))),
                  pl.BlockSpec((1, gate_t.shape[1], tq), lambda i, k, t: (i, 0, drained(t))),
                  pl.BlockSpec((1, gw, tq), lambda i, k, t: (i, k, drained(t))),
                  pl.BlockSpec((tq, gw), lambda i, k, t: (i * nt + drained(t), k))],
        out_specs=pl.BlockSpec((1, tq, gw), lambda i, k, t: (i, drained(t), k)),
        out_shape=jax.ShapeDtypeStruct((b, lq, NSA_WIDTH), bf16),
        scratch_shapes=[pltpu.VMEM((lq, lanes), f32), pltpu.VMEM((lq, lanes), f32),
                        pltpu.VMEM((8, lanes), f32), pltpu.VMEM((8, lanes), f32),
                        pltpu.VMEM((band, lanes), f32), pltpu.VMEM((band, lanes), f32),
                        pltpu.VMEM((8, lanes), f32), pltpu.VMEM((8, lanes), f32),
                        pltpu.VMEM((DH + ONES_ROWS, lanes), f32),
                        pltpu.VMEM((DH + ONES_ROWS, lanes), f32)],
        compiler_params=_cparams(("parallel", "parallel", "arbitrary")),
        name="slc_win",
    )(q_t, ks, vs_t, kw, vw_t, sel_t, gate_t, o_cmp_t, zn)


def _rms_gain(o, gain):
    return o * lax.rsqrt(jnp.mean(o * o, axis=-1, keepdims=True) + EPS) * gain


def _gla_kernel(q_ref, k_ref, v_ref, la_ref, z_ref, gain_ref, s0_ref, y_ref, st_ref, s_scr, *, tl, chunk):
    t = pl.program_id(1)

    @pl.when(t == 0)
    def _():
        s_scr[...] = s0_ref[0]

    c = chunk
    pairs = s_scr.shape[0]
    row = lax.broadcasted_iota(jnp.int32, (c, c), 0)
    col = lax.broadcasted_iota(jnp.int32, (c, c), 1)
    causal = col <= row
    tril = jnp.where(causal, 1.0, 0.0).astype(bf16)
    lane = lax.broadcasted_iota(jnp.int32, (1, LANES), 1)
    head_mask = [jnp.where(lane < DK, 1.0, 0.0), jnp.where(lane >= DK, 1.0, 0.0)]
    gain = gain_ref[...]

    units = [(ci, pr) for ci in range(tl // c) for pr in range(pairs)]
    rows_of = lambda ci: slice(ci * c, (ci + 1) * c)
    grp_of = lambda pr: slice(pr * LANES, (pr + 1) * LANES)
    bcs = []
    for ci, pr in units:
        hi, lo = _split_bf16(la_ref[rows_of(ci), grp_of(pr)])
        bcs.append(_dot(tril, hi) + _dot(tril, lo))
    scaled = []
    for (ci, pr), bc in zip(units, bcs):
        q, k = q_ref[rows_of(ci), grp_of(pr)], k_ref[rows_of(ci), grp_of(pr)]
        ref_row = bc[c // 2 - 1:c // 2, :]
        b_last = bc[c - 1:c, :]
        scaled.append((q * jnp.exp(bc - ref_row), (k * jnp.exp(ref_row - bc)).astype(bf16),
                       q * jnp.exp(bc), k * jnp.exp(b_last - bc), jnp.exp(b_last)))
    attn = []
    for (ci, pr), (q_a, k_a, _, _, _) in zip(units, scaled):
        attn.append([jnp.where(causal, _dot_nt((q_a * head_mask[h]).astype(bf16), k_a), 0.0).astype(bf16)
                     for h in range(2)])
    chunks = []
    for (ci, pr), (_, _, q_s, k_s, decay), a in zip(units, scaled, attn):
        rows = rows_of(ci)
        o_intra, q_inter, add = [], [], None
        for h in range(2):
            vh = v_ref[rows, (2 * pr + h) * DV:(2 * pr + h + 1) * DV].astype(bf16)
            o_intra.append(_dot(a[h], vh))
            q_inter.append((q_s * head_mask[h]).astype(bf16))
            u = _dot_tn(vh, (k_s * head_mask[h]).astype(bf16))
            add = u if add is None else add + u
        chunks.append((rows, pr, decay, add, o_intra, q_inter))

    st = [s_scr[pr] for pr in range(pairs)]
    for rows, pr, decay, add, o_intra, q_inter in chunks:
        st_b = st[pr].astype(bf16)
        for h in range(2):
            cols = slice((2 * pr + h) * DV, (2 * pr + h + 1) * DV)
            o = o_intra[h] + _dot_nt(q_inter[h], st_b)
            y_ref[rows, cols] = (_rms_gain(o, gain) * z_ref[rows, cols]).astype(y_ref.dtype)
        st[pr] = st[pr] * decay + add
    for pr in range(pairs):
        s_scr[pr] = st[pr]
        st_ref[0, pr] = st[pr]


def _gla(gq, gk, gv, la, zg, gain, state0, b, tl, chunk):
    n = gq.shape[0]
    nt = n // b // tl
    pairs = H_GLA // 2
    qk = pl.BlockSpec((tl, H_GLA * DK), lambda i, t: (i * nt + t, 0))
    vz = pl.BlockSpec((tl, GLA_WIDTH), lambda i, t: (i * nt + t, 0))
    st_spec = pl.BlockSpec((1, pairs, DV, LANES), lambda i, t: (i, 0, 0, 0))
    gain2 = gain.reshape(1, DV)
    s0 = state0.reshape(b, pairs, 2, DK, DV).transpose(0, 1, 4, 2, 3).reshape(b, pairs, DV, LANES)
    y, st = pl.pallas_call(
        functools.partial(_gla_kernel, tl=tl, chunk=chunk),
        grid=(b, nt),
        in_specs=[qk, qk, vz, qk, vz, pl.BlockSpec((1, DV), lambda i, t: (0, 0)), st_spec],
        out_specs=(vz, st_spec),
        out_shape=(jax.ShapeDtypeStruct((n, GLA_WIDTH), bf16),
                   jax.ShapeDtypeStruct((b, pairs, DV, LANES), f32)),
        scratch_shapes=[pltpu.VMEM((pairs, DV, LANES), f32)],
        compiler_params=_cparams(("parallel", "arbitrary")),
        name="gla",
    )(gq, gk, gv, la, zg, gain2, s0)
    state = st.reshape(b, pairs, DV, 2, DK).transpose(0, 1, 3, 4, 2).reshape(b, H_GLA, DK, DV)
    return y, state


def _outproj_kernel(x_ref, yn_ref, yg_ref, w_ref, g_ref, y_ref):
    mix = _dot(yn_ref[...].astype(bf16), w_ref[:NSA_WIDTH, :]) + _dot(yg_ref[...].astype(bf16), w_ref[NSA_WIDTH:, :])
    y_ref[...] = _rms_gain(x_ref[...] + mix, g_ref[...])


def _outproj(x, y_nsa, y_gla, w_out, gain, tm):
    n = x.shape[0]
    tok = lambda w: pl.BlockSpec((tm, w), lambda i: (i, 0))
    g2 = gain.reshape(1, D_MODEL)
    return pl.pallas_call(
        _outproj_kernel,
        grid=(n // tm,),
        in_specs=[tok(D_MODEL), tok(NSA_WIDTH), tok(GLA_WIDTH),
                  pl.BlockSpec(w_out.shape, lambda i: (0, 0)), pl.BlockSpec(g2.shape, lambda i: (0, 0))],
        out_specs=tok(D_MODEL),
        out_shape=jax.ShapeDtypeStruct((n, D_MODEL), f32),
        compiler_params=_cparams(("parallel",)),
        name="outproj",
    )(x, y_nsa, y_gla, w_out, g2)


def _prompt_path(x, wts, w_t, w_tok):
    (norm_g, _, wa2_pad, ba, w1_pairs, cmp_bias, w2_heads, gla_gain, w_out, out_gain) = wts
    b, lq, _ = x.shape
    (q_t, kvc_t, kvs_t, kvw_t, ks, kw, vs_t, vw_t, gate_t, zn, gq, gk, gv, la, zg) = _inproj_prompt(
        x, norm_g, w_t, w_tok, wa2_pad, ba, tm=512)
    o_cmp_t, sel_t = _cmp_attn(kvc_t, w1_pairs, cmp_bias, w2_heads, q_t, gate_t, tq=512)
    y_nsa = _slc_win(q_t, ks, vs_t, kw, vw_t, sel_t, gate_t, o_cmp_t, zn, tq=256, tk=512)
    y_gla, state = _gla(gq, gk, gv, la, zg, gla_gain, jnp.zeros((b, H_GLA, DK, DV), f32), b, tl=1024,
                        chunk=GLA_CHUNK)
    y = _outproj(x.reshape(b * lq, D_MODEL), y_nsa.reshape(b * lq, NSA_WIDTH), y_gla, w_out, out_gain, tm=1024)
    kv5 = lambda a: a.reshape(b, 2, KVH, DH, -1).transpose(0, 4, 1, 2, 3)
    wlen = min(WINDOW, lq)
    return (y.reshape(b, lq, D_MODEL), kv5(kvc_t), kv5(kvs_t), kv5(kvw_t[:, :, lq - wlen:]), state)


def _pages_native(cache):
    n_pool = cache.shape[0]
    return cache.transpose(0, 2, 3, 4, 1).reshape(n_pool, 2, KVH * DH, cache.shape[1])


def _page_gather(make_copies):
    i = pl.program_id(0)
    slot = i % 2

    @pl.when(i == 0)
    def _():
        for c in make_copies(0, 0):
            c.start()

    @pl.when(i + 1 < pl.num_programs(0))
    def _():
        for c in make_copies(i + 1, 1 - slot):
            c.start()

    for c in make_copies(i, slot):
        c.wait()
    return slot


def _group_sum_matrix(n_tok):
    r = np.arange(KVH * n_tok)[:, None]
    c = np.arange(H_NSA * n_tok)[None, :]
    m = ((c // (G_NSA * n_tok)) == (r // n_tok)) & ((c % n_tok) == (r % n_tok))
    return m.astype(np.float32)


def _cmp_attn_sample_kernel(pt_ref, cache_ref, perm_ref, w1_ref, bias_ref, w2_ref, cos_ref, sa_ref, sb_ref, q_ref,
                            gate_ref, cov_ref, gs_ref, gst_ref, o_ref, sel_ref, buf, rows_scr, sem,
                            *, n_pages, n_tok, past_len):
    def copies(seq, slot):
        return [pltpu.make_async_copy(cache_ref.at[pt_ref[seq, p]], buf.at[slot, p], sem.at[slot])
                for p in range(n_pages)]

    slot = _page_gather(copies)

    page_at = lambda p: buf[slot, p].reshape(2 * KVH * DH, PAGE_SIZE)
    k, v = _compress_pages(page_at, n_pages, perm_ref[...], rows_scr, w1_ref, bias_ref, w2_ref)
    k = _rope128(k, cos_ref[...], sa_ref[...], sb_ref[...])
    nc = n_pages * CHUNKS_PER_PAGE
    rows = H_NSA * n_tok
    qpos = past_len + lax.broadcasted_iota(jnp.int32, (rows, 1), 0) % n_tok
    end_c = lax.broadcasted_iota(jnp.int32, (1, nc), 1) * CMP_STRIDE + (CMP_LEN - 1)
    mask = end_c <= qpos
    p = _masked_softmax(_dot_nt(q_ref[0], k.astype(bf16)), mask, axis=-1)
    o_ref[0] = _dot(p.astype(bf16), v.astype(bf16)) * gate_ref[0][:, 0:1]
    hi, lo = _split_bf16(p)
    gs = gs_ref[...]
    ph, pl_ = _split_bf16(_dot(gs, hi) + _dot(gs, lo))
    imp = _dot(ph, cov_ref[...]) + _dot(pl_, cov_ref[...])
    n_blk = imp.shape[1]
    jblk = lax.broadcasted_iota(jnp.int32, imp.shape, 1)
    forced = (jblk == 0) | (jblk == n_blk - 1)
    sel = _topk_rows_by_rank(jnp.where(forced, FORCE_SCORE, imp), SLC_TOP - 1)
    sel_ref[0] = _dot(gst_ref[...], sel.astype(bf16)).astype(bf16)


def _cmp_attn_sample(cache_t, page_table, w1_pairs, bias, w2_heads, q_blk, gate_rows, n_tok):
    b, n_pages = page_table.shape
    past_len = n_pages * PAGE_SIZE
    nc = past_len // CMP_STRIDE
    rows = H_NSA * n_tok
    n_blk = past_len // SLC_LEN
    end_pos = jnp.arange(nc) * CMP_STRIDE + (CMP_LEN - 1)
    cos, sa, sb = _rope_tables(end_pos)
    cov = _cover_t(n_blk, nc).T
    gs = jnp.asarray(_group_sum_matrix(n_tok), dtype=bf16)
    gst = gs.T
    perm = _row_gather_matrix()
    full = lambda a: pl.BlockSpec(a.shape, lambda i, pt: (0,) * a.ndim)
    seq = lambda a: pl.BlockSpec((1,) + a.shape[1:], lambda i, pt: (i,) + (0,) * (a.ndim - 1))
    grid_spec = pltpu.PrefetchScalarGridSpec(
        num_scalar_prefetch=1,
        grid=(b,),
        in_specs=[pl.BlockSpec(memory_space=pl.ANY), full(perm), full(w1_pairs), full(bias), full(w2_heads),
                  full(cos), full(sa), full(sb), seq(q_blk), seq(gate_rows), full(cov), full(gs), full(gst)],
        out_specs=(pl.BlockSpec((1, rows, LANES), lambda i, pt: (i, 0, 0)),
                   pl.BlockSpec((1, rows, n_blk), lambda i, pt: (i, 0, 0))),
        scratch_shapes=[pltpu.VMEM((2, n_pages) + cache_t.shape[1:], f32),
                        pltpu.VMEM((2, CMP_STRIDE, nc, KVH * DH), f32),
                        pltpu.SemaphoreType.DMA((2,))],
    )
    return pl.pallas_call(
        functools.partial(_cmp_attn_sample_kernel, n_pages=n_pages, n_tok=n_tok, past_len=past_len),
        grid_spec=grid_spec,
        out_shape=(jax.ShapeDtypeStruct((b, rows, LANES), f32),
                   jax.ShapeDtypeStruct((b, rows, n_blk), bf16)),
        compiler_params=_cparams(("arbitrary",)),
        name="cmp_attn_sample",
    )(page_table, cache_t, perm, w1_pairs, bias, w2_heads, cos, sa, sb, q_blk, gate_rows, cov, gs, gst)


def _slc_win_sample_kernel(pt_ref, cache_ref, q_ref, sel_ref, e_ref, snew_ref, cw_ref, wnew_ref, wnewt_ref,
                           gate_ref, ocmp_ref, zn_ref, o_ref, wout_ref, buf, sem, *, n_pages, n_tok, win_off):
    def copies(seq, slot):
        return [pltpu.make_async_copy(cache_ref.at[pt_ref[seq, p], c],
                                      buf.at[slot, c, :, pl.ds(p * PAGE_SIZE, PAGE_SIZE)], sem.at[slot])
                for p in range(n_pages) for c in range(2)]

    slot = _page_gather(copies)
    rows = H_NSA * n_tok
    q = q_ref[0]
    tok = lax.broadcasted_iota(jnp.int32, (rows, 1), 0) % n_tok
    new_i = lax.broadcasted_iota(jnp.int32, (1, snew_ref.shape[1]), 1)
    new_ok = (new_i <= tok) & (new_i < n_tok)

    def attend(keys_t, vals_t, allowed, new_ref):
        s = jnp.where(allowed, _dot(q, keys_t), NEG)
        k_new = new_ref[0][:, :LANES].astype(bf16)
        v_new = new_ref[0][:, LANES:].astype(bf16)
        s_new = jnp.where(new_ok, _dot_nt(q, k_new), NEG)
        m = jnp.maximum(jnp.max(s, axis=-1, keepdims=True), jnp.max(s_new, axis=-1, keepdims=True))
        p = jnp.where(allowed, jnp.exp(s - m), 0.0)
        p_new = jnp.where(new_ok, jnp.exp(s_new - m), 0.0)
        l = jnp.sum(p, axis=-1, keepdims=True) + jnp.sum(p_new, axis=-1, keepdims=True)
        return (_dot_nt(p.astype(bf16), vals_t) + _dot(p_new.astype(bf16), v_new)) / l

    picked = _dot(sel_ref[0], e_ref[...]) > 0.5
    o_slc = attend(buf[slot, 0].astype(bf16), buf[slot, 1].astype(bf16), picked, snew_ref)

    wbuf = cw_ref.shape[-1]
    win_i = lax.broadcasted_iota(jnp.int32, (1, wbuf), 1)
    o_win = attend(cw_ref[0, 0].astype(bf16), cw_ref[0, 1].astype(bf16), win_i > tok + win_off, wnew_ref)

    lane = lax.broadcasted_iota(jnp.int32, (1, LANES), 1)
    for c in range(2):
        shifted = pltpu.roll(cw_ref[0, c], wbuf - n_tok, 1)
        tail = jnp.where(lane >= LANES - n_tok, wnewt_ref[0, c], shifted[:, wbuf - LANES:])
        wout_ref[0, c] = jnp.concatenate([shifted[:, :wbuf - LANES], tail], axis=-1)

    gate = gate_ref[0]
    o = ocmp_ref[0] + gate[:, 1:2] * o_slc + gate[:, 2:3] * o_win
    second_kvh = lax.broadcasted_iota(jnp.int32, (rows, 1), 0) >= G_NSA * n_tok
    o_ref[0] = jnp.where(second_kvh, o[:, DH:], o[:, :DH]) * zn_ref[0]


def _slc_win_sample(cache_t, page_table, q_blk, sel_rows, kvs_new, win_t, kvw_new, kvw_new_t, gate_rows,
                    o_cmp, zn_rows, n_tok):
    b, n_pages = page_table.shape
    rows = H_NSA * n_tok
    past_len = n_pages * PAGE_SIZE
    wbuf = win_t.shape[-1]
    e = _block_expand(sel_rows.shape[-1], past_len)
    full = lambda a: pl.BlockSpec(a.shape, lambda i, pt: (0,) * a.ndim)
    seq = lambda a: pl.BlockSpec((1,) + a.shape[1:], lambda i, pt: (i,) + (0,) * (a.ndim - 1))
    grid_spec = pltpu.PrefetchScalarGridSpec(
        num_scalar_prefetch=1,
        grid=(b,),
        in_specs=[pl.BlockSpec(memory_space=pl.ANY), seq(q_blk), seq(sel_rows), full(e), seq(kvs_new),
                  seq(win_t), seq(kvw_new), seq(kvw_new_t), seq(gate_rows), seq(o_cmp), seq(zn_rows)],
        out_specs=(pl.BlockSpec((1, rows, DH), lambda i, pt: (i, 0, 0)), seq(win_t)),
        scratch_shapes=[pltpu.VMEM((2, 2, KVH * DH, past_len), f32), pltpu.SemaphoreType.DMA((2,))],
    )
    return pl.pallas_call(
        functools.partial(_slc_win_sample_kernel, n_pages=n_pages, n_tok=n_tok, win_off=wbuf - WINDOW),
        grid_spec=grid_spec,
        out_shape=(jax.ShapeDtypeStruct((b, rows, DH), f32), jax.ShapeDtypeStruct(win_t.shape, f32)),
        compiler_params=_cparams(("arbitrary",)),
        name="slc_win_sample",
    )(page_table, cache_t, q_blk, sel_rows, e, kvs_new, win_t, kvw_new, kvw_new_t, gate_rows, o_cmp, zn_rows)


def _gla_sample(gq, gk, la, gv, zg, state, gain, b, n_tok):
    rows = 8
    pad = lambda a: jnp.pad(a.reshape(b, n_tok, -1), ((0, 0), (0, rows - n_tok), (0, 0))).reshape(b * rows, -1)
    y, s_new = _gla(pad(gq), pad(gk), pad(gv), pad(la), pad(zg), gain, state, b, tl=rows, chunk=rows)
    return y.reshape(b, rows, GLA_WIDTH)[:, :n_tok].reshape(b * n_tok, GLA_WIDTH), s_new


def _sample_path(x, cache_c, cache_s, cache_w, state, page_table, wts):
    (norm_g, w_pack, wa2_pad, ba, w1_pairs, cmp_bias, w2_heads, gla_gain, w_out, out_gain) = wts
    b, n_tok, _ = x.shape
    n = b * n_tok
    n_pages = page_table.shape[1]
    past_len = n_pages * PAGE_SIZE
    assert n_tok <= 8 and past_len % SLC_LEN == 0 and past_len // SLC_LEN <= LANES
    pos = past_len + jnp.arange(n) % n_tok
    (q_hm, kvc, kvs, kvw, gate, zn, gq, gk, gv, la, zg) = _inproj(
        x.reshape(1, n, D_MODEL), pos, norm_g, w_pack, wa2_pad, ba, tm=n)

    rows = H_NSA * n_tok
    q_rows = q_hm[0].reshape(H_NSA, b, n_tok, DH).transpose(1, 0, 2, 3)
    zero = jnp.zeros_like(q_rows[:, :G_NSA])
    q_blk = jnp.concatenate([jnp.concatenate([q_rows[:, :G_NSA], zero], axis=-1),
                             jnp.concatenate([zero, q_rows[:, G_NSA:]], axis=-1)], axis=1).reshape(b, rows, LANES)
    gate_rows = gate[:, :3 * H_NSA].reshape(b, n_tok, 3, H_NSA).transpose(0, 3, 1, 2).reshape(b, rows, 3)
    gate_rows = jnp.pad(gate_rows, ((0, 0), (0, 0), (0, LANES - 3)))
    pad_new = lambda a: jnp.pad(a.reshape(b, n_tok, KV_ROW), ((0, 0), (0, 8 - n_tok), (0, 0)))

    kvw_new_t = kvw.reshape(b, n_tok, 2, KVH * DH).transpose(0, 2, 3, 1)
    kvw_new_t = jnp.pad(kvw_new_t, ((0, 0), (0, 0), (0, 0), (LANES - n_tok, 0)))

    o_cmp, sel_rows = _cmp_attn_sample(_pages_native(cache_c), page_table, w1_pairs, cmp_bias, w2_heads,
                                       q_blk, gate_rows, n_tok)
    o_rows, win_t = _slc_win_sample(_pages_native(cache_s), page_table, q_blk, sel_rows, pad_new(kvs),
                                    _pages_native(cache_w), pad_new(kvw), kvw_new_t, gate_rows, o_cmp,
                                    zn.reshape(b, n_tok, H_NSA, DH).transpose(0, 2, 1, 3).reshape(b, rows, DH),
                                    n_tok)
    y_nsa = o_rows.reshape(b, H_NSA, n_tok, DH).transpose(0, 2, 1, 3).reshape(n, NSA_WIDTH)

    y_gla, s_new = _gla_sample(gq, gk, la, gv, zg, state, gla_gain, b, n_tok)
    y = _outproj(x.reshape(n, D_MODEL), y_nsa, y_gla, w_out, out_gain, tm=n)
    kv5 = lambda a: a.reshape(b, n_tok, 2, KVH, DH)
    win_new = win_t.reshape(b, 2, KVH, DH, -1).transpose(0, 4, 1, 2, 3)
    return (y.reshape(b, n_tok, D_MODEL), kv5(kvc), kv5(kvs), win_new, s_new)


def kernel(x_prompt, x_sample, cache_cmp_kv, cache_slc_kv, cache_win_kv, state_gla, page_table,
           norm_in_gain, w_in, cmp_pe, cmp_w1, cmp_b1, cmp_w2, gla_wa2, gla_ba, gla_norm_gain,
           w_out, norm_out_gain):
    assert w_in.shape[0] == 1, "single-layer step"
    wts = (norm_in_gain[0], _pack_w_in(w_in[0]), _pad_wa2(gla_wa2[0]), gla_ba[0],
           _cmp_w1_pairs(cmp_w1[0]), _cmp_bias(cmp_pe[0], cmp_w1[0], cmp_b1[0]), _cmp_w2_heads(cmp_w2[0]),
           gla_norm_gain[0], w_out[0].astype(bf16), norm_out_gain)
    yp, cmp_p, slc_p, win_p, gla_p = _prompt_path(x_prompt, wts, *_pack_w_in_prompt(w_in[0]))
    ys, cmp_s, slc_s, win_s, gla_s = _sample_path(
        x_sample, cache_cmp_kv[0], cache_slc_kv[0], cache_win_kv[0], state_gla[0], page_table, wts)
    return (yp, ys, cmp_p[None], cmp_s[None], slc_p[None], slc_s[None], win_p[None], win_s[None],
            gla_p[None], gla_s[None])
```

```python
import functools

import numpy as np
import jax
import jax.numpy as jnp
from jax import lax
from jax.experimental import pallas as pl
from jax.experimental.pallas import tpu as pltpu

f32 = jnp.float32
bf16 = jnp.bfloat16

D_MODEL = 1024
DH = 64
H_NSA = 8
KVH = 2
G_NSA = 4
NSA_WIDTH = H_NSA * DH
CMP_LEN = 32
CMP_STRIDE = 16
CMP_HID = 128
SLC_LEN = 64
SLC_TOP = 16
WINDOW = 512
ROT_HALF = 8
ROPE_THETA = 500000.0
FORCE_SCORE = 1.0e4
H_GLA = 4
DK = 64
DV = 128
GLA_WIDTH = H_GLA * DV
GLA_LR = 16
GLA_TAU = 16.0
GLA_CHUNK = 64
EPS = 1e-6
NEG = -1.0e30
LOG2E = 1.4426950408889634
PAGE_SIZE = 128
KV_ROW = 2 * KVH * DH

IN_SIZES = (H_NSA * DH, KV_ROW, KV_ROW, KV_ROW, 3 * H_NSA, NSA_WIDTH,
            H_GLA * DK, H_GLA * DK, H_GLA * DV, GLA_LR, GLA_WIDTH)
IN_OFFSETS = [0] + [int(v) for v in np.cumsum(IN_SIZES)]

LANES = 128
VMEM_LIMIT = 56 * 1024 * 1024

P_Q, P_KVC, P_KVS, P_KVW, P_ZN, P_GQ, P_GK, P_GV, P_ZG, P_MISC = (
    0, 512, 768, 1024, 1280, 1792, 2048, 2304, 2816, 3328)
D_PACK = P_MISC + LANES
MISC_GATE = 0
MISC_GLR = 32


def _cparams(sem):
    return pltpu.CompilerParams(dimension_semantics=sem, vmem_limit_bytes=VMEM_LIMIT)


def _sigmoid(x):
    return 1.0 / (1.0 + jnp.exp(-x))


def _silu(x):
    return x * _sigmoid(x)


def _log_sigmoid(x):
    return jnp.minimum(x, 0.0) - jnp.log1p(jnp.exp(-jnp.abs(x)))


def _dot(a, b):
    return jnp.dot(a, b, preferred_element_type=f32)


def _dot_nt(a, b):
    return lax.dot_general(a, b, (((1,), (1,)), ((), ())), preferred_element_type=f32)


def _dot_tn(a, b):
    return lax.dot_general(a, b, (((0,), (0,)), ((), ())), preferred_element_type=f32)


def _rope_tables(pos):
    n = pos.shape[0]
    inv = ROPE_THETA ** (-(jnp.arange(ROT_HALF, dtype=f32) / ROT_HALF))
    ang = pos.astype(f32)[:, None] * inv[None, :]
    cos, sin = jnp.cos(ang), jnp.sin(ang)
    z8 = jnp.zeros((n, ROT_HALF), f32)
    rest = DH - 2 * ROT_HALF
    c64 = jnp.concatenate([cos, cos, jnp.ones((n, rest), f32)], axis=-1)
    sa64 = jnp.concatenate([-sin, z8, jnp.zeros((n, rest), f32)], axis=-1)
    sb64 = jnp.concatenate([z8, sin, jnp.zeros((n, rest), f32)], axis=-1)
    tile = lambda t: jnp.concatenate([t, t], axis=-1)
    return tile(c64), tile(sa64), tile(sb64)


def _rope128(v, cos, sa, sb):
    return v * cos + pltpu.roll(v, LANES - ROT_HALF, 1) * sa + pltpu.roll(v, ROT_HALF, 1) * sb


def _inproj_kernel(x_ref, g_ref, w_ref, cos_ref, sa_ref, sb_ref, wa2_ref, ba_ref,
                   q_ref, kvc_ref, kvs_ref, kvw_ref, gate_ref, zn_ref, gq_ref, gk_ref, gv_ref, la_ref, zg_ref):
    x = x_ref[...]
    ms = jnp.mean(x * x, axis=-1, keepdims=True)
    hn = (x * lax.rsqrt(ms + EPS) * g_ref[...]).astype(bf16)
    cos, sa, sb = cos_ref[...], sa_ref[...], sb_ref[...]

    def proj(off, width):
        return _dot(hn, w_ref[:, off:off + width])

    qp = proj(P_Q, NSA_WIDTH)
    for c in range(NSA_WIDTH // LANES):
        r = _rope128(qp[:, c * LANES:(c + 1) * LANES], cos, sa, sb) * (DH ** -0.5)
        q_ref[0, 2 * c] = r[:, :DH].astype(bf16)
        q_ref[0, 2 * c + 1] = r[:, DH:].astype(bf16)

    kvc_ref[...] = proj(P_KVC, KV_ROW)

    for off, kv_ref in ((P_KVS, kvs_ref), (P_KVW, kvw_ref)):
        p = proj(off, KV_ROW)
        kv_ref[:, :LANES] = _rope128(p[:, :LANES], cos, sa, sb)
        kv_ref[:, LANES:] = p[:, LANES:]

    zn_ref[...] = _silu(proj(P_ZN, NSA_WIDTH))
    gq_ref[...] = proj(P_GQ, H_GLA * DK) * (DK ** -0.5)
    gk_ref[...] = proj(P_GK, H_GLA * DK)
    gv_ref[...] = proj(P_GV, H_GLA * DV)
    zg_ref[...] = _silu(proj(P_ZG, GLA_WIDTH))

    misc = proj(P_MISC, LANES)
    gate_ref[...] = _sigmoid(misc)
    xa = _dot(misc.astype(bf16), wa2_ref[...]) + ba_ref[...]
    la_ref[...] = _log_sigmoid(xa) / GLA_TAU


def _pack_w_in(w_in):
    o = IN_OFFSETS
    seg = lambda i: w_in[:, o[i]:o[i + 1]]
    misc = jnp.zeros((D_MODEL, LANES), w_in.dtype)
    misc = misc.at[:, MISC_GATE:MISC_GATE + 3 * H_NSA].set(seg(4))
    misc = misc.at[:, MISC_GLR:MISC_GLR + GLA_LR].set(seg(9))
    cols = [seg(0), seg(1), seg(2), seg(3), seg(5), seg(6), seg(7), seg(8), seg(10), misc]
    return jnp.concatenate(cols, axis=1).astype(bf16)


def _pad_wa2(wa2):
    pad = jnp.zeros((LANES, H_GLA * DK), wa2.dtype).at[MISC_GLR:MISC_GLR + GLA_LR].set(wa2)
    return pad.astype(bf16)


def _inproj(x, pos, norm_g, w_pack, wa2_pad, ba, tm):
    bk, lk, _ = x.shape
    n = bk * lk
    nt = lk // tm
    cos, sa, sb = _rope_tables(pos)
    tok = lambda w: pl.BlockSpec((tm, w), lambda i: (i, 0))
    tab = pl.BlockSpec((tm, LANES), lambda i: (i % nt, 0))
    full = lambda a: pl.BlockSpec(a.shape, lambda i: (0,) * a.ndim)
    hm = lambda h: pl.BlockSpec((1, h, tm, DH), lambda i: (i // nt, 0, i % nt, 0))
    g2 = norm_g.reshape(1, D_MODEL)
    ba2 = ba.reshape(1, H_GLA * DK)
    out_shape = (
        jax.ShapeDtypeStruct((bk, H_NSA, lk, DH), bf16),
        jax.ShapeDtypeStruct((n, KV_ROW), f32),
        jax.ShapeDtypeStruct((n, KV_ROW), f32),
        jax.ShapeDtypeStruct((n, KV_ROW), f32),
        jax.ShapeDtypeStruct((n, LANES), f32),
        jax.ShapeDtypeStruct((n, NSA_WIDTH), f32),
        jax.ShapeDtypeStruct((n, H_GLA * DK), f32),
        jax.ShapeDtypeStruct((n, H_GLA * DK), f32),
        jax.ShapeDtypeStruct((n, H_GLA * DV), f32),
        jax.ShapeDtypeStruct((n, H_GLA * DK), f32),
        jax.ShapeDtypeStruct((n, GLA_WIDTH), f32),
    )
    out_specs = (hm(H_NSA), tok(KV_ROW), tok(KV_ROW), tok(KV_ROW),
                 tok(LANES), tok(NSA_WIDTH), tok(H_GLA * DK), tok(H_GLA * DK), tok(H_GLA * DV),
                 tok(H_GLA * DK), tok(GLA_WIDTH))
    return pl.pallas_call(
        _inproj_kernel,
        grid=(n // tm,),
        in_specs=[tok(D_MODEL), full(g2), full(w_pack), tab, tab, tab, full(wa2_pad), full(ba2)],
        out_specs=out_specs,
        out_shape=out_shape,
        compiler_params=_cparams(("parallel",)),
        name="inproj",
    )(x.reshape(n, D_MODEL), g2, w_pack, cos, sa, sb, wa2_pad, ba2)


PT_Q, PT_KVC, PT_KVS, PT_KVW, PT_GATE = 0, 512, 768, 1024, 1280
PT_ROWS = PT_GATE + 32
PK_ZN, PK_GQ, PK_GK, PK_GV, PK_ZG, PK_MISC = 0, 512, 768, 1024, 1536, 2048
PK_COLS = PK_MISC + LANES


def _inproj_prompt_kernel(x_ref, g_ref, wt_ref, w_ref, cos_ref, sin_ref, wa2_ref, ba_ref,
                          qt_ref, kvct_ref, kvst_ref, kvwt_ref, ks_ref, kw_ref, vst_ref, vwt_ref,
                          gatet_ref, zn_ref, gq_ref, gk_ref, gv_ref, la_ref, zg_ref):
    x = x_ref[...]
    ms = jnp.mean(x * x, axis=-1, keepdims=True)
    hn = (x * lax.rsqrt(ms + EPS) * g_ref[...]).astype(bf16)
    cos_t, sin_t = cos_ref[...], sin_ref[...]

    all_t = _dot_nt(wt_ref[...], hn)

    def proj_t(off, rows):
        return all_t[off:off + rows]

    def rope_rows(v):
        x1, x2 = v[:ROT_HALF], v[ROT_HALF:2 * ROT_HALF]
        return jnp.concatenate([x1 * cos_t - x2 * sin_t, x2 * cos_t + x1 * sin_t, v[2 * ROT_HALF:]], axis=0)

    q_t = proj_t(PT_Q, NSA_WIDTH)
    for h in range(H_NSA):
        qt_ref[0, h] = (rope_rows(q_t[h * DH:(h + 1) * DH]) * (DH ** -0.5 * LOG2E)).astype(bf16)

    kvct_ref[0] = proj_t(PT_KVC, KV_ROW)

    for off, kvt_ref, k_ref, vt_ref in ((PT_KVS, kvst_ref, ks_ref, vst_ref), (PT_KVW, kvwt_ref, kw_ref, vwt_ref)):
        t = proj_t(off, KV_ROW)
        k_t = jnp.concatenate([rope_rows(t[h * DH:(h + 1) * DH]) for h in range(KVH)], axis=0)
        kvt_ref[0, :LANES] = k_t
        kvt_ref[0, LANES:] = t[LANES:]
        k_tok = k_t.T
        for h in range(KVH):
            k_ref[0, h] = k_tok[:, h * DH:(h + 1) * DH].astype(bf16)
            vt_ref[0, h] = t[LANES + h * DH:LANES + (h + 1) * DH].astype(bf16)

    gatet_ref[0] = _sigmoid(proj_t(PT_GATE, PT_ROWS - PT_GATE))

    def proj(off, width):
        return _dot(hn, w_ref[:, off:off + width])

    zn_ref[...] = _silu(proj(PK_ZN, NSA_WIDTH))
    gq_ref[...] = proj(PK_GQ, H_GLA * DK) * (DK ** -0.5)
    gk_ref[...] = proj(PK_GK, H_GLA * DK)
    gv_ref[...] = proj(PK_GV, H_GLA * DV)
    zg_ref[...] = _silu(proj(PK_ZG, GLA_WIDTH))
    misc = proj(PK_MISC, LANES)
    xa = _dot(misc.astype(bf16), wa2_ref[...]) + ba_ref[...]
    la_ref[...] = _log_sigmoid(xa) / GLA_TAU


def _pack_w_in_prompt(w_in):
    o = IN_OFFSETS
    seg = lambda i: w_in[:, o[i]:o[i + 1]]
    gate_t = jnp.zeros((PT_ROWS - PT_GATE, D_MODEL), w_in.dtype).at[:3 * H_NSA].set(seg(4).T)
    w_t = jnp.concatenate([seg(0).T, seg(1).T, seg(2).T, seg(3).T, gate_t], axis=0).astype(bf16)
    misc = jnp.zeros((D_MODEL, LANES), w_in.dtype).at[:, MISC_GLR:MISC_GLR + GLA_LR].set(seg(9))
    w_tok = jnp.concatenate([seg(5), seg(6), seg(7), seg(8), seg(10), misc], axis=1).astype(bf16)
    return w_t, w_tok


def _inproj_prompt(x, norm_g, w_t, w_tok, wa2_pad, ba, tm):
    b, lq, _ = x.shape
    n = b * lq
    nt = lq // tm
    inv = ROPE_THETA ** (-(jnp.arange(ROT_HALF, dtype=f32) / ROT_HALF))
    ang = inv[:, None] * jnp.arange(lq).astype(f32)[None, :]
    cos_t, sin_t = jnp.cos(ang), jnp.sin(ang)
    tok = lambda w: pl.BlockSpec((tm, w), lambda i: (i, 0))
    tab = pl.BlockSpec((ROT_HALF, tm), lambda i: (0, i % nt))
    full = lambda a: pl.BlockSpec(a.shape, lambda i: (0,) * a.ndim)
    feat = lambda r: pl.BlockSpec((1, r, tm), lambda i: (i // nt, 0, i % nt))
    headf = lambda h: pl.BlockSpec((1, h, DH, tm), lambda i: (i // nt, 0, 0, i % nt))
    headt = lambda h: pl.BlockSpec((1, h, tm, DH), lambda i: (i // nt, 0, i % nt, 0))
    g2 = norm_g.reshape(1, D_MODEL)
    ba2 = ba.reshape(1, H_GLA * DK)
    sds = jax.ShapeDtypeStruct
    out_shape = (
        sds((b, H_NSA, DH, lq), bf16),
        sds((b, KV_ROW, lq), f32), sds((b, KV_ROW, lq), f32), sds((b, KV_ROW, lq), f32),
        sds((b, KVH, lq, DH), bf16), sds((b, KVH, lq, DH), bf16),
        sds((b, KVH, DH, lq), bf16), sds((b, KVH, DH, lq), bf16),
        sds((b, PT_ROWS - PT_GATE, lq), f32),
        sds((n, NSA_WIDTH), f32), sds((n, H_GLA * DK), f32), sds((n, H_GLA * DK), f32),
        sds((n, H_GLA * DV), f32), sds((n, H_GLA * DK), f32), sds((n, GLA_WIDTH), f32),
    )
    out_specs = (headf(H_NSA), feat(KV_ROW), feat(KV_ROW), feat(KV_ROW), headt(KVH), headt(KVH),
                 headf(KVH), headf(KVH), feat(PT_ROWS - PT_GATE),
                 tok(NSA_WIDTH), tok(H_GLA * DK), tok(H_GLA * DK), tok(H_GLA * DV), tok(H_GLA * DK),
                 tok(GLA_WIDTH))
    return pl.pallas_call(
        _inproj_prompt_kernel,
        grid=(n // tm,),
        in_specs=[tok(D_MODEL), full(g2), full(w_t), full(w_tok), tab, tab, full(wa2_pad), full(ba2)],
        out_specs=out_specs,
        out_shape=out_shape,
        compiler_params=_cparams(("parallel",)),
        name="inproj_prompt",
    )(x.reshape(n, D_MODEL), g2, w_t, w_tok, cos_t, sin_t, wa2_pad, ba2)


HID_W = 2 * KVH * CMP_HID
CHUNKS_PER_PAGE = PAGE_SIZE // CMP_STRIDE
PAGE_GROUP = 32


def _cmp_w1_pairs(cmp_w1):
    ratio = CMP_LEN // CMP_STRIDE
    w1r = cmp_w1.reshape(2, ratio, CMP_STRIDE // 2, 2, DH, CMP_HID)
    eye = jnp.eye(KVH, dtype=cmp_w1.dtype)
    big = jnp.einsum('crpjdh,kK->cpjKdrkh', w1r, eye)
    return big.reshape(2, CMP_STRIDE // 2, 2 * KVH * DH, ratio * KVH * CMP_HID).astype(bf16)


def _cmp_w2_heads(cmp_w2):
    eye = jnp.eye(KVH, dtype=cmp_w2.dtype)
    return jnp.einsum('chd,kK->ckhKd', cmp_w2, eye).reshape(2, KVH * CMP_HID, KVH * DH).astype(bf16)


def _row_gather_matrix():
    r = np.arange(PAGE_SIZE)
    src = (r % CHUNKS_PER_PAGE) * CMP_STRIDE + r // CHUNKS_PER_PAGE
    return jnp.asarray((r[None, :] == src[:, None]).astype(np.float32), dtype=bf16)


def _compress_pages(page_at, n_pages, perm, rows_scr, w1_ref, bias_ref, w2_ref):
    cpp = CHUNKS_PER_PAGE
    kd = KVH * DH

    def to_rows(g, carry):
        pages = [g * PAGE_GROUP + u for u in range(PAGE_GROUP)]
        gathered = [_dot_nt(perm, page_at(p).astype(bf16)) for p in pages]
        for p, t in zip(pages, gathered):
            r0 = pl.multiple_of(p * cpp, cpp)
            for c in range(2):
                for s in range(CMP_STRIDE):
                    rows_scr[c, s, pl.ds(r0, cpp), :] = t[s * cpp:(s + 1) * cpp, c * kd:(c + 1) * kd]
        return carry

    lax.fori_loop(0, n_pages // PAGE_GROUP, to_rows, 0)

    nc = n_pages * cpp
    kv = []
    for c in range(2):
        lhs = jnp.concatenate([rows_scr[c, s] for s in range(CMP_STRIDE)], axis=-1).astype(bf16)
        part = _dot(lhs, w1_ref[c].reshape(CMP_STRIDE * kd, w1_ref.shape[-1]))
        half = part.shape[1] // 2
        pre = part[:, :half] + pltpu.roll(part[:, half:], nc - 1, 0)
        hid = _silu(pre + bias_ref[:, c * half:(c + 1) * half])
        kv.append(_dot(hid.astype(bf16), w2_ref[c]))
    return kv


def _cmp_bias_kernel(pe_ref, w1_ref, b1_ref, o_ref):
    for c in range(2):
        o_ref[c] = _dot(pe_ref[c], w1_ref[c]) + b1_ref[c]


def _cmp_bias(cmp_pe, cmp_w1, cmp_b1):
    pe = jnp.broadcast_to(cmp_pe.reshape(2, 1, CMP_LEN * DH), (2, 8, CMP_LEN * DH))
    b1 = jnp.broadcast_to(cmp_b1.reshape(2, 1, CMP_HID), (2, 8, CMP_HID))
    out = pl.pallas_call(
        _cmp_bias_kernel,
        out_shape=jax.ShapeDtypeStruct((2, 8, CMP_HID), f32),
        name="cmp_bias",
    )(pe, cmp_w1, b1)
    row = out[:, 0, :]
    return jnp.broadcast_to(row[:, None, :], (2, KVH, CMP_HID)).reshape(1, HID_W)


def _masked_softmax(s, mask, axis, exp=jnp.exp):
    s = jnp.where(mask, s, NEG)
    e = exp(s - jnp.max(s, axis=axis, keepdims=True))
    return jnp.where(mask, e * (1.0 / jnp.sum(e, axis=axis, keepdims=True)), 0.0)


def _split_bf16(x):
    hi = x.astype(bf16)
    return hi, (x - hi.astype(f32)).astype(bf16)


def _topk_mask(score, k, axis):
    n = score.shape[axis]
    idx = lax.broadcasted_iota(jnp.int32, score.shape, axis)
    sel = jnp.zeros(score.shape, f32)
    for _ in range(k):
        m = jnp.max(score, axis=axis, keepdims=True)
        first = jnp.min(jnp.where(score == m, idx, n), axis=axis, keepdims=True)
        pick = idx == first
        sel = jnp.where(pick, 1.0, sel)
        score = jnp.where(pick, NEG, score)
    return sel


def _topk_rows_by_rank(score, k):
    r, n = score.shape
    cols = jnp.concatenate([score, jnp.zeros((n - r, n), f32)], axis=0).T
    earlier = jnp.where(lax.broadcasted_iota(jnp.int32, (n, n), 0) < lax.broadcasted_iota(jnp.int32, (n, n), 1),
                        1.0, 0.0)
    rows = []
    for c in range(r):
        col, row = cols[:, c:c + 1], score[c:c + 1, :]
        before = jnp.where(col > row, 1.0, jnp.where(col == row, earlier, 0.0))
        rows.append(jnp.where(jnp.sum(before, axis=0, keepdims=True) < k, 1.0, 0.0))
    return jnp.concatenate(rows, axis=0)


def _cover_t(n_slc, n_chunk):
    start = np.arange(n_chunk)[None, :] * CMP_STRIDE
    j = np.arange(n_slc)[:, None]
    cov = (start < (j + 1) * SLC_LEN) & (start + CMP_LEN > j * SLC_LEN)
    return jnp.asarray(cov.astype(np.float32), dtype=bf16)


def _cmp_attn_kernel(kvct_ref, perm_ref, w1_ref, bias_ref, w2_ref, cos_ref, sa_ref, sb_ref, qt_ref, gatet_ref,
                     cov_ref, ot_ref, selt_ref, rows_scr, kc_ref, vct_ref, *, tq):
    t = pl.program_id(1)

    @pl.when(t == 0)
    def _():
        page_at = lambda p: kvct_ref[0, :, pl.ds(pl.multiple_of(p * PAGE_SIZE, PAGE_SIZE), PAGE_SIZE)]
        k, v = _compress_pages(page_at, kvct_ref.shape[2] // PAGE_SIZE, perm_ref[...], rows_scr,
                               w1_ref, bias_ref, w2_ref)
        k = _rope128(k, cos_ref[...], sa_ref[...], sb_ref[...])
        v_t = v.T
        for h in range(KVH):
            kc_ref[h] = k[:, h * DH:(h + 1) * DH].astype(bf16)
            vct_ref[h] = v_t[h * DH:(h + 1) * DH].astype(bf16)

    nc = kc_ref.shape[1]
    nb = cov_ref.shape[0]
    qpos = t * tq + lax.broadcasted_iota(jnp.int32, (1, tq), 1)
    end_pos = lax.broadcasted_iota(jnp.int32, (nc, 1), 0) * CMP_STRIDE + (CMP_LEN - 1)
    mask_t = end_pos <= qpos
    gate_t = gatet_ref[0]
    cov = cov_ref[...]

    jblk = lax.broadcasted_iota(jnp.int32, (nb, tq), 0)
    qblk = qpos // SLC_LEN
    valid = jblk <= qblk
    forced = (jblk == 0) | (jblk == qblk) | (jblk == qblk - 1)

    heads = range(H_NSA)
    s_t = [_dot(kc_ref[h // G_NSA], qt_ref[0, h]) for h in heads]
    p_t = [_masked_softmax(s, mask_t, axis=0, exp=jnp.exp2) for s in s_t]
    for h in heads:
        ot_ref[0, h * DH:(h + 1) * DH] = _dot(vct_ref[h // G_NSA], p_t[h].astype(bf16)) * gate_t[h:h + 1]
    scores = []
    for kh in range(KVH):
        psum = functools.reduce(lambda a, b: a + b, p_t[kh * G_NSA:(kh + 1) * G_NSA])
        hi, lo = _split_bf16(psum)
        imp = _dot(cov, hi) + _dot(cov, lo)
        scores.append(jnp.where(valid, jnp.where(forced, FORCE_SCORE, imp), -1.0))
    picked = _topk_mask(jnp.concatenate(scores, axis=1), SLC_TOP, axis=0)
    for kh in range(KVH):
        sel_t = jnp.where(valid & (picked[:, kh * tq:(kh + 1) * tq] > 0.5), 0.0, NEG)
        if nb < LANES:
            sel_t = jnp.concatenate([sel_t, jnp.full((LANES - nb, tq), NEG, f32)], axis=0)
        selt_ref[0, kh] = sel_t


def _cmp_attn(kvc_t, w1_pairs, bias, w2_heads, q_t, gate_t, tq):
    b, _, lq = kvc_t.shape
    assert lq % (PAGE_SIZE * PAGE_GROUP) == 0
    nc = lq // CMP_STRIDE
    n_slc = lq // SLC_LEN
    perm = _row_gather_matrix()
    end_pos = jnp.arange(nc) * CMP_STRIDE + (CMP_LEN - 1)
    cos, sa, sb = _rope_tables(end_pos)
    cov = _cover_t(n_slc, nc)
    full = lambda a: pl.BlockSpec(a.shape, lambda i, t: (0,) * a.ndim)
    return pl.pallas_call(
        functools.partial(_cmp_attn_kernel, tq=tq),
        grid=(b, lq // tq),
        in_specs=[pl.BlockSpec((1, KV_ROW, lq), lambda i, t: (i, 0, 0)),
                  full(perm), full(w1_pairs), full(bias), full(w2_heads), full(cos), full(sa), full(sb),
                  pl.BlockSpec((1, H_NSA, DH, tq), lambda i, t: (i, 0, 0, t)),
                  pl.BlockSpec((1, gate_t.shape[1], tq), lambda i, t: (i, 0, t)),
                  full(cov)],
        out_specs=(pl.BlockSpec((1, NSA_WIDTH, tq), lambda i, t: (i, 0, t)),
                   pl.BlockSpec((1, KVH, LANES, tq), lambda i, t: (i, 0, 0, t))),
        out_shape=(jax.ShapeDtypeStruct((b, NSA_WIDTH, lq), f32),
                   jax.ShapeDtypeStruct((b, KVH, LANES, lq), f32)),
        scratch_shapes=[pltpu.VMEM((2, CMP_STRIDE, nc, KVH * DH), f32),
                        pltpu.VMEM((KVH, nc, DH), bf16), pltpu.VMEM((KVH, DH, nc), bf16)],
        compiler_params=_cparams(("parallel", "arbitrary")),
        name="cmp_attn",
    )(kvc_t, perm, w1_pairs, bias, w2_heads, cos, sa, sb, q_t, gate_t, cov)


def _block_expand(n_slc_pad, n_keys):
    e = (np.arange(n_keys)[None, :] // SLC_LEN) == np.arange(n_slc_pad)[:, None]
    return jnp.asarray(e.astype(np.float32), dtype=bf16)


def _mask_bias_t(allowed):
    bias = jnp.where(allowed, 0.0, NEG)
    return jnp.concatenate([bias] * G_NSA, axis=1)


def _sublane_fold(x, op):
    return functools.reduce(op, [x[r:r + 8] for r in range(0, x.shape[0], 8)])


def _slc_win_kernel(qt_ref, ks_ref, vst_ref, kw_ref, vwt_ref, selt_ref, gatet_ref, ocmpt_ref, zn_ref,
                    o_ref, s_a, s_b, m_a, m_b, ws_a, ws_b, wm_a, wm_b, acc_ref, wacc_ref, *, tq, tk, nt):
    t = pl.program_id(2)

    @pl.when(t == 0)
    def _():
        m_b[...] = jnp.full(m_b.shape, NEG, f32)
        ws_b[...] = jnp.zeros(ws_b.shape, f32)
        wm_b[...] = jnp.zeros(wm_b.shape, f32)

    step = functools.partial(_slc_win_step, qt_ref, ks_ref, vst_ref, kw_ref, vwt_ref, selt_ref, gatet_ref,
                             ocmpt_ref, zn_ref, o_ref, acc_ref, wacc_ref, tq=tq, tk=tk, nt=nt)

    @pl.when(t % 2 == 0)
    def _():
        step(s_a, s_b, m_a, m_b, ws_a, ws_b, wm_a, wm_b)

    @pl.when(t % 2 == 1)
    def _():
        step(s_b, s_a, m_b, m_a, ws_b, ws_a, wm_b, wm_a)


ONES_ROWS = 16


def _slc_win_step(qt_ref, ks_ref, vst_ref, kw_ref, vwt_ref, selt_ref, gatet_ref, ocmpt_ref, zn_ref, o_ref,
                  acc_ref, wacc_ref, s_cur, s_prev, m_cur, m_last, ws_cur, ws_prev, wm_cur, wm_last,
                  *, tq, tk, nt):
    kh = pl.program_id(1)
    t = pl.program_id(2)
    lq = ks_ref.shape[2]
    band = WINDOW + tq
    blocks_per_tile = tk // SLC_LEN
    key_tiles = lambda tile: (tile * tq + tq + tk - 1) // tk
    n_apply = jnp.where(t >= 1, key_tiles(t - 1), 0)
    n_plain = jnp.where(t < nt, (t * tq) // tk, 0)
    q0 = t * tq
    qpos = q0 + lax.broadcasted_iota(jnp.int32, (1, tq), 1)
    q_t = jnp.concatenate([qt_ref[0, g] for g in range(G_NSA)], axis=1)
    ones = jnp.ones((ONES_ROWS, tk), bf16)

    m_cur[...] = jnp.full(m_cur.shape, NEG, f32)
    m_prev = jnp.max(m_last[...], axis=0, keepdims=True)
    acc_ref[...] = jnp.zeros(acc_ref.shape, f32)

    def score(kt, causal):
        k0 = pl.multiple_of(kt * tk, tk)
        j0 = pl.multiple_of(kt * blocks_per_tile, blocks_per_tile)
        sel = selt_ref[0, 0, pl.ds(j0, blocks_per_tile), :]
        bias = jnp.concatenate([jnp.broadcast_to(sel[j:j + 1], (SLC_LEN, tq)) for j in range(blocks_per_tile)],
                               axis=0)
        if causal:
            kpos = k0 + lax.broadcasted_iota(jnp.int32, (tk, 1), 0)
            bias = jnp.where(kpos <= qpos, bias, NEG)
        s = _dot(ks_ref[0, 0, pl.ds(k0, tk), :], q_t) + jnp.concatenate([bias] * G_NSA, axis=1)
        s_cur[pl.ds(k0, tk), :] = s
        m_cur[...] = jnp.maximum(m_cur[...], _sublane_fold(s, jnp.maximum))

    def apply(kt):
        k0 = pl.multiple_of(kt * tk, tk)
        p = jnp.exp2(s_prev[pl.ds(k0, tk), :] - m_prev).astype(bf16)
        acc_ref[...] += _dot(jnp.concatenate([vst_ref[0, 0, :, pl.ds(k0, tk)], ones], axis=0), p)

    def both_body(kt, carry):
        score(kt, causal=False)
        apply(kt)
        return carry

    def apply_body(kt, carry):
        apply(kt)
        return carry

    def both2_body(i, carry):
        for u in range(2):
            score(2 * i + u, causal=False)
            apply(2 * i + u)
        return carry

    n_both = jnp.minimum(n_plain, n_apply)
    lax.fori_loop(0, n_both // 2, both2_body, 0)
    lax.fori_loop(n_both // 2 * 2, n_both, both_body, 0)
    lax.fori_loop(n_both, n_apply, apply_body, 0)
    score(jnp.minimum((t * tq) // tk, lq // tk - 1), causal=True)

    w0 = pl.multiple_of(jnp.clip(q0 - WINDOW, 0, lq - band), tq)
    w0_prev = pl.multiple_of(jnp.clip(q0 - tq - WINDOW, 0, lq - band), tq)
    wm_prev = jnp.max(wm_last[...], axis=0, keepdims=True)
    ones_w = jnp.ones((ONES_ROWS, tq), bf16)
    wmax, wacc = None, None
    for j in range(band // tq):
        rows = pl.ds(j * tq, tq)
        kpos = w0 + j * tq + lax.broadcasted_iota(jnp.int32, (tq, 1), 0)
        s = (_dot(kw_ref[0, 0, pl.ds(pl.multiple_of(w0 + j * tq, tq), tq), :], q_t)
             + _mask_bias_t((kpos <= qpos) & (kpos > qpos - WINDOW)))
        ws_cur[rows, :] = s
        fold = _sublane_fold(s, jnp.maximum)
        wmax = fold if wmax is None else jnp.maximum(wmax, fold)
        p = jnp.exp2(ws_prev[rows, :] - wm_prev).astype(bf16)
        v_ext = jnp.concatenate([vwt_ref[0, 0, :, pl.ds(pl.multiple_of(w0_prev + j * tq, tq), tq)], ones_w], axis=0)
        part = _dot(v_ext, p)
        wacc = part if wacc is None else wacc + part
    wm_cur[...] = wmax
    wacc_ref[...] = wacc

    @pl.when(t >= 1)
    def _():
        o_slc = acc_ref[:DH] / acc_ref[DH:DH + 1]
        o_win = wacc_ref[:DH] / wacc_ref[DH:DH + 1]
        gate_t = gatet_ref[0]
        row = lambda c, g: jnp.where(kh == 0, gate_t[c * H_NSA + g:c * H_NSA + g + 1],
                                     gate_t[c * H_NSA + G_NSA + g:c * H_NSA + G_NSA + g + 1])
        g_slc = jnp.concatenate([row(1, g) for g in range(G_NSA)], axis=1)
        g_win = jnp.concatenate([row(2, g) for g in range(G_NSA)], axis=1)
        o_t = g_slc * o_slc + g_win * o_win
        o_heads = jnp.concatenate([o_t[:, g * tq:(g + 1) * tq] for g in range(G_NSA)], axis=0)
        o_ref[0] = ((ocmpt_ref[0] + o_heads).T * zn_ref[...]).astype(o_ref.dtype)


def _slc_win(q_t, ks, vs_t, kw, vw_t, sel_t, gate_t, o_cmp_t, zn, tq, tk):
    b, _, _, lq = q_t.shape
    assert lq >= WINDOW + tq and lq % tk == 0 and tk % (8 * SLC_LEN) == 0
    gw = G_NSA * DH
    k_spec = pl.BlockSpec((1, 1, lq, DH), lambda i, k, t: (i, k, 0, 0))
    vt_spec = pl.BlockSpec((1, 1, DH, lq), lambda i, k, t: (i, k, 0, 0))
    lanes = G_NSA * tq
    band = WINDOW + tq
    nt = lq // tq
    scored = lambda t: jnp.minimum(t, nt - 1)
    drained = lambda t: jnp.maximum(t - 1, 0)
    return pl.pallas_call(
        functools.partial(_slc_win_kernel, tq=tq, tk=tk, nt=nt),
        grid=(b, KVH, nt + 1),
        in_specs=[pl.BlockSpec((1, G_NSA, DH, tq), lambda i, k, t: (i, k, 0, scored(t))),
                  k_spec, vt_spec, k_spec, vt_spec,
                  pl.BlockSpec((1, 1, LANES, tq), lambda i, k, t: (i, k, 0, scored(t))),
                  pl.BlockSpec((1, gate_t.shape[1], tq), lambda i, k, t: (i, 0, drained(t))),
                  pl.BlockSpec((1, gw, tq), lambda i, k, t: (i, k, drained(t))),
                  pl.BlockSpec((tq, gw), lambda i, k, t: (i * nt + drained(t), k))],
        out_specs=pl.BlockSpec((1, tq, gw), lambda i, k, t: (i, drained(t), k)),
        out_shape=jax.ShapeDtypeStruct((b, lq, NSA_WIDTH), bf16),
        scratch_shapes=[pltpu.VMEM((lq, lanes), f32), pltpu.VMEM((lq, lanes), f32),
                        pltpu.VMEM((8, lanes), f32), pltpu.VMEM((8, lanes), f32),
                        pltpu.VMEM((band, lanes), f32), pltpu.VMEM((band, lanes), f32),
                        pltpu.VMEM((8, lanes), f32), pltpu.VMEM((8, lanes), f32),
                        pltpu.VMEM((DH + ONES_ROWS, lanes), f32),
                        pltpu.VMEM((DH + ONES_ROWS, lanes), f32)],
        compiler_params=_cparams(("parallel", "parallel", "arbitrary")),
        name="slc_win",
    )(q_t, ks, vs_t, kw, vw_t, sel_t, gate_t, o_cmp_t, zn)


def _rms_gain(o, gain):
    return o * lax.rsqrt(jnp.mean(o * o, axis=-1, keepdims=True) + EPS) * gain


def _gla_kernel(q_ref, k_ref, v_ref, la_ref, z_ref, gain_ref, s0_ref, y_ref, st_ref, s_scr, *, tl, chunk):
    t = pl.program_id(1)

    @pl.when(t == 0)
    def _():
        s_scr[...] = s0_ref[0]

    c = chunk
    pairs = s_scr.shape[0]
    row = lax.broadcasted_iota(jnp.int32, (c, c), 0)
    col = lax.broadcasted_iota(jnp.int32, (c, c), 1)
    causal = col <= row
    tril = jnp.where(causal, 1.0, 0.0).astype(bf16)
    lane = lax.broadcasted_iota(jnp.int32, (1, LANES), 1)
    head_mask = [jnp.where(lane < DK, 1.0, 0.0), jnp.where(lane >= DK, 1.0, 0.0)]
    gain = gain_ref[...]

    units = [(ci, pr) for ci in range(tl // c) for pr in range(pairs)]
    rows_of = lambda ci: slice(ci * c, (ci + 1) * c)
    grp_of = lambda pr: slice(pr * LANES, (pr + 1) * LANES)
    bcs = []
    for ci, pr in units:
        hi, lo = _split_bf16(la_ref[rows_of(ci), grp_of(pr)])
        bcs.append(_dot(tril, hi) + _dot(tril, lo))
    scaled = []
    for (ci, pr), bc in zip(units, bcs):
        q, k = q_ref[rows_of(ci), grp_of(pr)], k_ref[rows_of(ci), grp_of(pr)]
        ref_row = bc[c // 2 - 1:c // 2, :]
        b_last = bc[c - 1:c, :]
        scaled.append((q * jnp.exp(bc - ref_row), (k * jnp.exp(ref_row - bc)).astype(bf16),
                       q * jnp.exp(bc), k * jnp.exp(b_last - bc), jnp.exp(b_last)))
    attn = []
    for (ci, pr), (q_a, k_a, _, _, _) in zip(units, scaled):
        attn.append([jnp.where(causal, _dot_nt((q_a * head_mask[h]).astype(bf16), k_a), 0.0).astype(bf16)
                     for h in range(2)])
    chunks = []
    for (ci, pr), (_, _, q_s, k_s, decay), a in zip(units, scaled, attn):
        rows = rows_of(ci)
        o_intra, q_inter, add = [], [], None
        for h in range(2):
            vh = v_ref[rows, (2 * pr + h) * DV:(2 * pr + h + 1) * DV].astype(bf16)
            o_intra.append(_dot(a[h], vh))
            q_inter.append((q_s * head_mask[h]).astype(bf16))
            u = _dot_tn(vh, (k_s * head_mask[h]).astype(bf16))
            add = u if add is None else add + u
        chunks.append((rows, pr, decay, add, o_intra, q_inter))

    st = [s_scr[pr] for pr in range(pairs)]
    for rows, pr, decay, add, o_intra, q_inter in chunks:
        st_b = st[pr].astype(bf16)
        for h in range(2):
            cols = slice((2 * pr + h) * DV, (2 * pr + h + 1) * DV)
            o = o_intra[h] + _dot_nt(q_inter[h], st_b)
            y_ref[rows, cols] = (_rms_gain(o, gain) * z_ref[rows, cols]).astype(y_ref.dtype)
        st[pr] = st[pr] * decay + add
    for pr in range(pairs):
        s_scr[pr] = st[pr]
        st_ref[0, pr] = st[pr]


def _gla(gq, gk, gv, la, zg, gain, state0, b, tl, chunk):
    n = gq.shape[0]
    nt = n // b // tl
    pairs = H_GLA // 2
    qk = pl.BlockSpec((tl, H_GLA * DK), lambda i, t: (i * nt + t, 0))
    vz = pl.BlockSpec((tl, GLA_WIDTH), lambda i, t: (i * nt + t, 0))
    st_spec = pl.BlockSpec((1, pairs, DV, LANES), lambda i, t: (i, 0, 0, 0))
    gain2 = gain.reshape(1, DV)
    s0 = state0.reshape(b, pairs, 2, DK, DV).transpose(0, 1, 4, 2, 3).reshape(b, pairs, DV, LANES)
    y, st = pl.pallas_call(
        functools.partial(_gla_kernel, tl=tl, chunk=chunk),
        grid=(b, nt),
        in_specs=[qk, qk, vz, qk, vz, pl.BlockSpec((1, DV), lambda i, t: (0, 0)), st_spec],
        out_specs=(vz, st_spec),
        out_shape=(jax.ShapeDtypeStruct((n, GLA_WIDTH), bf16),
                   jax.ShapeDtypeStruct((b, pairs, DV, LANES), f32)),
        scratch_shapes=[pltpu.VMEM((pairs, DV, LANES), f32)],
        compiler_params=_cparams(("parallel", "arbitrary")),
        name="gla",
    )(gq, gk, gv, la, zg, gain2, s0)
    state = st.reshape(b, pairs, DV, 2, DK).transpose(0, 1, 3, 4, 2).reshape(b, H_GLA, DK, DV)
    return y, state


def _outproj_kernel(x_ref, yn_ref, yg_ref, w_ref, g_ref, y_ref):
    mix = _dot(yn_ref[...].astype(bf16), w_ref[:NSA_WIDTH, :]) + _dot(yg_ref[...].astype(bf16), w_ref[NSA_WIDTH:, :])
    y_ref[...] = _rms_gain(x_ref[...] + mix, g_ref[...])


def _outproj(x, y_nsa, y_gla, w_out, gain, tm):
    n = x.shape[0]
    tok = lambda w: pl.BlockSpec((tm, w), lambda i: (i, 0))
    g2 = gain.reshape(1, D_MODEL)
    return pl.pallas_call(
        _outproj_kernel,
        grid=(n // tm,),
        in_specs=[tok(D_MODEL), tok(NSA_WIDTH), tok(GLA_WIDTH),
                  pl.BlockSpec(w_out.shape, lambda i: (0, 0)), pl.BlockSpec(g2.shape, lambda i: (0, 0))],
        out_specs=tok(D_MODEL),
        out_shape=jax.ShapeDtypeStruct((n, D_MODEL), f32),
        compiler_params=_cparams(("parallel",)),
        name="outproj",
    )(x, y_nsa, y_gla, w_out, g2)


def _prompt_path(x, wts, w_t, w_tok):
    (norm_g, _, wa2_pad, ba, w1_pairs, cmp_bias, w2_heads, gla_gain, w_out, out_gain) = wts
    b, lq, _ = x.shape
    (q_t, kvc_t, kvs_t, kvw_t, ks, kw, vs_t, vw_t, gate_t, zn, gq, gk, gv, la, zg) = _inproj_prompt(
        x, norm_g, w_t, w_tok, wa2_pad, ba, tm=512)
    o_cmp_t, sel_t = _cmp_attn(kvc_t, w1_pairs, cmp_bias, w2_heads, q_t, gate_t, tq=512)
    y_nsa = _slc_win(q_t, ks, vs_t, kw, vw_t, sel_t, gate_t, o_cmp_t, zn, tq=256, tk=512)
    y_gla, state = _gla(gq, gk, gv, la, zg, gla_gain, jnp.zeros((b, H_GLA, DK, DV), f32), b, tl=1024,
                        chunk=GLA_CHUNK)
    y = _outproj(x.reshape(b * lq, D_MODEL), y_nsa.reshape(b * lq, NSA_WIDTH), y_gla, w_out, out_gain, tm=1024)
    kv5 = lambda a: a.reshape(b, 2, KVH, DH, -1).transpose(0, 4, 1, 2, 3)
    wlen = min(WINDOW, lq)
    return (y.reshape(b, lq, D_MODEL), kv5(kvc_t), kv5(kvs_t), kv5(kvw_t[:, :, lq - wlen:]), state)


def _pages_native(cache):
    n_pool = cache.shape[0]
    return cache.transpose(0, 2, 3, 4, 1).reshape(n_pool, 2, KVH * DH, cache.shape[1])


def _page_gather(make_copies):
    i = pl.program_id(0)
    slot = i % 2

    @pl.when(i == 0)
    def _():
        for c in make_copies(0, 0):
            c.start()

    @pl.when(i + 1 < pl.num_programs(0))
    def _():
        for c in make_copies(i + 1, 1 - slot):
            c.start()

    for c in make_copies(i, slot):
        c.wait()
    return slot


def _group_sum_matrix(n_tok):
    r = np.arange(KVH * n_tok)[:, None]
    c = np.arange(H_NSA * n_tok)[None, :]
    m = ((c // (G_NSA * n_tok)) == (r // n_tok)) & ((c % n_tok) == (r % n_tok))
    return m.astype(np.float32)


def _cmp_attn_sample_kernel(pt_ref, cache_ref, perm_ref, w1_ref, bias_ref, w2_ref, cos_ref, sa_ref, sb_ref, q_ref,
                            gate_ref, cov_ref, gs_ref, gst_ref, o_ref, sel_ref, buf, rows_scr, sem,
                            *, n_pages, n_tok, past_len):
    def copies(seq, slot):
        return [pltpu.make_async_copy(cache_ref.at[pt_ref[seq, p]], buf.at[slot, p], sem.at[slot])
                for p in range(n_pages)]

    slot = _page_gather(copies)

    page_at = lambda p: buf[slot, p].reshape(2 * KVH * DH, PAGE_SIZE)
    k, v = _compress_pages(page_at, n_pages, perm_ref[...], rows_scr, w1_ref, bias_ref, w2_ref)
    k = _rope128(k, cos_ref[...], sa_ref[...], sb_ref[...])
    nc = n_pages * CHUNKS_PER_PAGE
    rows = H_NSA * n_tok
    qpos = past_len + lax.broadcasted_iota(jnp.int32, (rows, 1), 0) % n_tok
    end_c = lax.broadcasted_iota(jnp.int32, (1, nc), 1) * CMP_STRIDE + (CMP_LEN - 1)
    mask = end_c <= qpos
    p = _masked_softmax(_dot_nt(q_ref[0], k.astype(bf16)), mask, axis=-1)
    o_ref[0] = _dot(p.astype(bf16), v.astype(bf16)) * gate_ref[0][:, 0:1]
    hi, lo = _split_bf16(p)
    gs = gs_ref[...]
    ph, pl_ = _split_bf16(_dot(gs, hi) + _dot(gs, lo))
    imp = _dot(ph, cov_ref[...]) + _dot(pl_, cov_ref[...])
    n_blk = imp.shape[1]
    jblk = lax.broadcasted_iota(jnp.int32, imp.shape, 1)
    forced = (jblk == 0) | (jblk == n_blk - 1)
    sel = _topk_rows_by_rank(jnp.where(forced, FORCE_SCORE, imp), SLC_TOP - 1)
    sel_ref[0] = _dot(gst_ref[...], sel.astype(bf16)).astype(bf16)


def _cmp_attn_sample(cache_t, page_table, w1_pairs, bias, w2_heads, q_blk, gate_rows, n_tok):
    b, n_pages = page_table.shape
    past_len = n_pages * PAGE_SIZE
    nc = past_len // CMP_STRIDE
    rows = H_NSA * n_tok
    n_blk = past_len // SLC_LEN
    end_pos = jnp.arange(nc) * CMP_STRIDE + (CMP_LEN - 1)
    cos, sa, sb = _rope_tables(end_pos)
    cov = _cover_t(n_blk, nc).T
    gs = jnp.asarray(_group_sum_matrix(n_tok), dtype=bf16)
    gst = gs.T
    perm = _row_gather_matrix()
    full = lambda a: pl.BlockSpec(a.shape, lambda i, pt: (0,) * a.ndim)
    seq = lambda a: pl.BlockSpec((1,) + a.shape[1:], lambda i, pt: (i,) + (0,) * (a.ndim - 1))
    grid_spec = pltpu.PrefetchScalarGridSpec(
        num_scalar_prefetch=1,
        grid=(b,),
        in_specs=[pl.BlockSpec(memory_space=pl.ANY), full(perm), full(w1_pairs), full(bias), full(w2_heads),
                  full(cos), full(sa), full(sb), seq(q_blk), seq(gate_rows), full(cov), full(gs), full(gst)],
        out_specs=(pl.BlockSpec((1, rows, LANES), lambda i, pt: (i, 0, 0)),
                   pl.BlockSpec((1, rows, n_blk), lambda i, pt: (i, 0, 0))),
        scratch_shapes=[pltpu.VMEM((2, n_pages) + cache_t.shape[1:], f32),
                        pltpu.VMEM((2, CMP_STRIDE, nc, KVH * DH), f32),
                        pltpu.SemaphoreType.DMA((2,))],
    )
    return pl.pallas_call(
        functools.partial(_cmp_attn_sample_kernel, n_pages=n_pages, n_tok=n_tok, past_len=past_len),
        grid_spec=grid_spec,
        out_shape=(jax.ShapeDtypeStruct((b, rows, LANES), f32),
                   jax.ShapeDtypeStruct((b, rows, n_blk), bf16)),
        compiler_params=_cparams(("arbitrary",)),
        name="cmp_attn_sample",
    )(page_table, cache_t, perm, w1_pairs, bias, w2_heads, cos, sa, sb, q_blk, gate_rows, cov, gs, gst)


def _slc_win_sample_kernel(pt_ref, cache_ref, q_ref, sel_ref, e_ref, snew_ref, cw_ref, wnew_ref, wnewt_ref,
                           gate_ref, ocmp_ref, zn_ref, o_ref, wout_ref, buf, sem, *, n_pages, n_tok, win_off):
    def copies(seq, slot):
        return [pltpu.make_async_copy(cache_ref.at[pt_ref[seq, p], c],
                                      buf.at[slot, c, :, pl.ds(p * PAGE_SIZE, PAGE_SIZE)], sem.at[slot])
                for p in range(n_pages) for c in range(2)]

    slot = _page_gather(copies)
    rows = H_NSA * n_tok
    q = q_ref[0]
    tok = lax.broadcasted_iota(jnp.int32, (rows, 1), 0) % n_tok
    new_i = lax.broadcasted_iota(jnp.int32, (1, snew_ref.shape[1]), 1)
    new_ok = (new_i <= tok) & (new_i < n_tok)

    def attend(keys_t, vals_t, allowed, new_ref):
        s = jnp.where(allowed, _dot(q, keys_t), NEG)
        k_new = new_ref[0][:, :LANES].astype(bf16)
        v_new = new_ref[0][:, LANES:].astype(bf16)
        s_new = jnp.where(new_ok, _dot_nt(q, k_new), NEG)
        m = jnp.maximum(jnp.max(s, axis=-1, keepdims=True), jnp.max(s_new, axis=-1, keepdims=True))
        p = jnp.where(allowed, jnp.exp(s - m), 0.0)
        p_new = jnp.where(new_ok, jnp.exp(s_new - m), 0.0)
        l = jnp.sum(p, axis=-1, keepdims=True) + jnp.sum(p_new, axis=-1, keepdims=True)
        return (_dot_nt(p.astype(bf16), vals_t) + _dot(p_new.astype(bf16), v_new)) / l

    picked = _dot(sel_ref[0], e_ref[...]) > 0.5
    o_slc = attend(buf[slot, 0].astype(bf16), buf[slot, 1].astype(bf16), picked, snew_ref)

    wbuf = cw_ref.shape[-1]
    win_i = lax.broadcasted_iota(jnp.int32, (1, wbuf), 1)
    o_win = attend(cw_ref[0, 0].astype(bf16), cw_ref[0, 1].astype(bf16), win_i > tok + win_off, wnew_ref)

    lane = lax.broadcasted_iota(jnp.int32, (1, LANES), 1)
    for c in range(2):
        shifted = pltpu.roll(cw_ref[0, c], wbuf - n_tok, 1)
        tail = jnp.where(lane >= LANES - n_tok, wnewt_ref[0, c], shifted[:, wbuf - LANES:])
        wout_ref[0, c] = jnp.concatenate([shifted[:, :wbuf - LANES], tail], axis=-1)

    gate = gate_ref[0]
    o = ocmp_ref[0] + gate[:, 1:2] * o_slc + gate[:, 2:3] * o_win
    second_kvh = lax.broadcasted_iota(jnp.int32, (rows, 1), 0) >= G_NSA * n_tok
    o_ref[0] = jnp.where(second_kvh, o[:, DH:], o[:, :DH]) * zn_ref[0]


def _slc_win_sample(cache_t, page_table, q_blk, sel_rows, kvs_new, win_t, kvw_new, kvw_new_t, gate_rows,
                    o_cmp, zn_rows, n_tok):
    b, n_pages = page_table.shape
    rows = H_NSA * n_tok
    past_len = n_pages * PAGE_SIZE
    wbuf = win_t.shape[-1]
    e = _block_expand(sel_rows.shape[-1], past_len)
    full = lambda a: pl.BlockSpec(a.shape, lambda i, pt: (0,) * a.ndim)
    seq = lambda a: pl.BlockSpec((1,) + a.shape[1:], lambda i, pt: (i,) + (0,) * (a.ndim - 1))
    grid_spec = pltpu.PrefetchScalarGridSpec(
        num_scalar_prefetch=1,
        grid=(b,),
        in_specs=[pl.BlockSpec(memory_space=pl.ANY), seq(q_blk), seq(sel_rows), full(e), seq(kvs_new),
                  seq(win_t), seq(kvw_new), seq(kvw_new_t), seq(gate_rows), seq(o_cmp), seq(zn_rows)],
        out_specs=(pl.BlockSpec((1, rows, DH), lambda i, pt: (i, 0, 0)), seq(win_t)),
        scratch_shapes=[pltpu.VMEM((2, 2, KVH * DH, past_len), f32), pltpu.SemaphoreType.DMA((2,))],
    )
    return pl.pallas_call(
        functools.partial(_slc_win_sample_kernel, n_pages=n_pages, n_tok=n_tok, win_off=wbuf - WINDOW),
        grid_spec=grid_spec,
        out_shape=(jax.ShapeDtypeStruct((b, rows, DH), f32), jax.ShapeDtypeStruct(win_t.shape, f32)),
        compiler_params=_cparams(("arbitrary",)),
        name="slc_win_sample",
    )(page_table, cache_t, q_blk, sel_rows, e, kvs_new, win_t, kvw_new, kvw_new_t, gate_rows, o_cmp, zn_rows)


def _gla_sample(gq, gk, la, gv, zg, state, gain, b, n_tok):
    rows = 8
    pad = lambda a: jnp.pad(a.reshape(b, n_tok, -1), ((0, 0), (0, rows - n_tok), (0, 0))).reshape(b * rows, -1)
    y, s_new = _gla(pad(gq), pad(gk), pad(gv), pad(la), pad(zg), gain, state, b, tl=rows, chunk=rows)
    return y.reshape(b, rows, GLA_WIDTH)[:, :n_tok].reshape(b * n_tok, GLA_WIDTH), s_new


def _sample_path(x, cache_c, cache_s, cache_w, state, page_table, wts):
    (norm_g, w_pack, wa2_pad, ba, w1_pairs, cmp_bias, w2_heads, gla_gain, w_out, out_gain) = wts
    b, n_tok, _ = x.shape
    n = b * n_tok
    n_pages = page_table.shape[1]
    past_len = n_pages * PAGE_SIZE
    assert n_tok <= 8 and past_len % SLC_LEN == 0 and past_len // SLC_LEN <= LANES
    pos = past_len + jnp.arange(n) % n_tok
    (q_hm, kvc, kvs, kvw, gate, zn, gq, gk, gv, la, zg) = _inproj(
        x.reshape(1, n, D_MODEL), pos, norm_g, w_pack, wa2_pad, ba, tm=n)

    rows = H_NSA * n_tok
    q_rows = q_hm[0].reshape(H_NSA, b, n_tok, DH).transpose(1, 0, 2, 3)
    zero = jnp.zeros_like(q_rows[:, :G_NSA])
    q_blk = jnp.concatenate([jnp.concatenate([q_rows[:, :G_NSA], zero], axis=-1),
                             jnp.concatenate([zero, q_rows[:, G_NSA:]], axis=-1)], axis=1).reshape(b, rows, LANES)
    gate_rows = gate[:, :3 * H_NSA].reshape(b, n_tok, 3, H_NSA).transpose(0, 3, 1, 2).reshape(b, rows, 3)
    gate_rows = jnp.pad(gate_rows, ((0, 0), (0, 0), (0, LANES - 3)))
    pad_new = lambda a: jnp.pad(a.reshape(b, n_tok, KV_ROW), ((0, 0), (0, 8 - n_tok), (0, 0)))

    kvw_new_t = kvw.reshape(b, n_tok, 2, KVH * DH).transpose(0, 2, 3, 1)
    kvw_new_t = jnp.pad(kvw_new_t, ((0, 0), (0, 0), (0, 0), (LANES - n_tok, 0)))

    o_cmp, sel_rows = _cmp_attn_sample(_pages_native(cache_c), page_table, w1_pairs, cmp_bias, w2_heads,
                                       q_blk, gate_rows, n_tok)
    o_rows, win_t = _slc_win_sample(_pages_native(cache_s), page_table, q_blk, sel_rows, pad_new(kvs),
                                    _pages_native(cache_w), pad_new(kvw), kvw_new_t, gate_rows, o_cmp,
                                    zn.reshape(b, n_tok, H_NSA, DH).transpose(0, 2, 1, 3).reshape(b, rows, DH),
                                    n_tok)
    y_nsa = o_rows.reshape(b, H_NSA, n_tok, DH).transpose(0, 2, 1, 3).reshape(n, NSA_WIDTH)

    y_gla, s_new = _gla_sample(gq, gk, la, gv, zg, state, gla_gain, b, n_tok)
    y = _outproj(x.reshape(n, D_MODEL), y_nsa, y_gla, w_out, out_gain, tm=n)
    kv5 = lambda a: a.reshape(b, n_tok, 2, KVH, DH)
    win_new = win_t.reshape(b, 2, KVH, DH, -1).transpose(0, 4, 1, 2, 3)
    return (y.reshape(b, n_tok, D_MODEL), kv5(kvc), kv5(kvs), win_new, s_new)


def kernel(x_prompt, x_sample, cache_cmp_kv, cache_slc_kv, cache_win_kv, state_gla, page_table,
           norm_in_gain, w_in, cmp_pe, cmp_w1, cmp_b1, cmp_w2, gla_wa2, gla_ba, gla_norm_gain,
           w_out, norm_out_gain):
    assert w_in.shape[0] == 1, "single-layer step"
    wts = (norm_in_gain[0], _pack_w_in(w_in[0]), _pad_wa2(gla_wa2[0]), gla_ba[0],
           _cmp_w1_pairs(cmp_w1[0]), _cmp_bias(cmp_pe[0], cmp_w1[0], cmp_b1[0]), _cmp_w2_heads(cmp_w2[0]),
           gla_norm_gain[0], w_out[0].astype(bf16), norm_out_gain)
    yp, cmp_p, slc_p, win_p, gla_p = _prompt_path(x_prompt, wts, *_pack_w_in_prompt(w_in[0]))
    ys, cmp_s, slc_s, win_s, gla_s = _sample_path(
        x_sample, cache_cmp_kv[0], cache_slc_kv[0], cache_win_kv[0], state_gla[0], page_table, wts)
    return (yp, ys, cmp_p[None], cmp_s[None], slc_p[None], slc_s[None], win_p[None], win_s[None],
            gla_p[None], gla_s[None])
```

```python
import functools

import numpy as np
import jax
import jax.numpy as jnp
from jax import lax
from jax.experimental import pallas as pl
from jax.experimental.pallas import tpu as pltpu

f32 = jnp.float32
bf16 = jnp.bfloat16

D_MODEL = 1024
DH = 64
H_NSA = 8
KVH = 2
G_NSA = 4
NSA_WIDTH = H_NSA * DH
CMP_LEN = 32
CMP_STRIDE = 16
CMP_HID = 128
SLC_LEN = 64
SLC_TOP = 16
WINDOW = 512
ROT_HALF = 8
ROPE_THETA = 500000.0
FORCE_SCORE = 1.0e4
H_GLA = 4
DK = 64
DV = 128
GLA_WIDTH = H_GLA * DV
GLA_LR = 16
GLA_TAU = 16.0
GLA_CHUNK = 64
EPS = 1e-6
NEG = -1.0e30
LOG2E = 1.4426950408889634
PAGE_SIZE = 128
KV_ROW = 2 * KVH * DH

IN_SIZES = (H_NSA * DH, KV_ROW, KV_ROW, KV_ROW, 3 * H_NSA, NSA_WIDTH,
            H_GLA * DK, H_GLA * DK, H_GLA * DV, GLA_LR, GLA_WIDTH)
IN_OFFSETS = [0] + [int(v) for v in np.cumsum(IN_SIZES)]

LANES = 128
VMEM_LIMIT = 56 * 1024 * 1024

P_Q, P_KVC, P_KVS, P_KVW, P_ZN, P_GQ, P_GK, P_GV, P_ZG, P_MISC = (
    0, 512, 768, 1024, 1280, 1792, 2048, 2304, 2816, 3328)
D_PACK = P_MISC + LANES
MISC_GATE = 0
MISC_GLR = 32


def _cparams(sem):
    return pltpu.CompilerParams(dimension_semantics=sem, vmem_limit_bytes=VMEM_LIMIT)


def _sigmoid(x):
    return 1.0 / (1.0 + jnp.exp(-x))


def _silu(x):
    return x * _sigmoid(x)


def _log_sigmoid(x):
    return jnp.minimum(x, 0.0) - jnp.log1p(jnp.exp(-jnp.abs(x)))


def _dot(a, b):
    return jnp.dot(a, b, preferred_element_type=f32)


def _dot_nt(a, b):
    return lax.dot_general(a, b, (((1,), (1,)), ((), ())), preferred_element_type=f32)


def _dot_tn(a, b):
    return lax.dot_general(a, b, (((0,), (0,)), ((), ())), preferred_element_type=f32)


def _rope_tables(pos):
    n = pos.shape[0]
    inv = ROPE_THETA ** (-(jnp.arange(ROT_HALF, dtype=f32) / ROT_HALF))
    ang = pos.astype(f32)[:, None] * inv[None, :]
    cos, sin = jnp.cos(ang), jnp.sin(ang)
    z8 = jnp.zeros((n, ROT_HALF), f32)
    rest = DH - 2 * ROT_HALF
    c64 = jnp.concatenate([cos, cos, jnp.ones((n, rest), f32)], axis=-1)
    sa64 = jnp.concatenate([-sin, z8, jnp.zeros((n, rest), f32)], axis=-1)
    sb64 = jnp.concatenate([z8, sin, jnp.zeros((n, rest), f32)], axis=-1)
    tile = lambda t: jnp.concatenate([t, t], axis=-1)
    return tile(c64), tile(sa64), tile(sb64)


def _rope128(v, cos, sa, sb):
    return v * cos + pltpu.roll(v, LANES - ROT_HALF, 1) * sa + pltpu.roll(v, ROT_HALF, 1) * sb


def _inproj_kernel(x_ref, g_ref, w_ref, cos_ref, sa_ref, sb_ref, wa2_ref, ba_ref,
                   q_ref, kvc_ref, kvs_ref, kvw_ref, gate_ref, zn_ref, gq_ref, gk_ref, gv_ref, la_ref, zg_ref):
    x = x_ref[...]
    ms = jnp.mean(x * x, axis=-1, keepdims=True)
    hn = (x * lax.rsqrt(ms + EPS) * g_ref[...]).astype(bf16)
    cos, sa, sb = cos_ref[...], sa_ref[...], sb_ref[...]

    def proj(off, width):
        return _dot(hn, w_ref[:, off:off + width])

    qp = proj(P_Q, NSA_WIDTH)
    for c in range(NSA_WIDTH // LANES):
        r = _rope128(qp[:, c * LANES:(c + 1) * LANES], cos, sa, sb) * (DH ** -0.5)
        q_ref[0, 2 * c] = r[:, :DH].astype(bf16)
        q_ref[0, 2 * c + 1] = r[:, DH:].astype(bf16)

    kvc_ref[...] = proj(P_KVC, KV_ROW)

    for off, kv_ref in ((P_KVS, kvs_ref), (P_KVW, kvw_ref)):
        p = proj(off, KV_ROW)
        kv_ref[:, :LANES] = _rope128(p[:, :LANES], cos, sa, sb)
        kv_ref[:, LANES:] = p[:, LANES:]

    zn_ref[...] = _silu(proj(P_ZN, NSA_WIDTH))
    gq_ref[...] = proj(P_GQ, H_GLA * DK) * (DK ** -0.5)
    gk_ref[...] = proj(P_GK, H_GLA * DK)
    gv_ref[...] = proj(P_GV, H_GLA * DV)
    zg_ref[...] = _silu(proj(P_ZG, GLA_WIDTH))

    misc = proj(P_MISC, LANES)
    gate_ref[...] = _sigmoid(misc)
    xa = _dot(misc.astype(bf16), wa2_ref[...]) + ba_ref[...]
    la_ref[...] = _log_sigmoid(xa) / GLA_TAU


def _pack_w_in(w_in):
    o = IN_OFFSETS
    seg = lambda i: w_in[:, o[i]:o[i + 1]]
    misc = jnp.zeros((D_MODEL, LANES), w_in.dtype)
    misc = misc.at[:, MISC_GATE:MISC_GATE + 3 * H_NSA].set(seg(4))
    misc = misc.at[:, MISC_GLR:MISC_GLR + GLA_LR].set(seg(9))
    cols = [seg(0), seg(1), seg(2), seg(3), seg(5), seg(6), seg(7), seg(8), seg(10), misc]
    return jnp.concatenate(cols, axis=1).astype(bf16)


def _pad_wa2(wa2):
    pad = jnp.zeros((LANES, H_GLA * DK), wa2.dtype).at[MISC_GLR:MISC_GLR + GLA_LR].set(wa2)
    return pad.astype(bf16)


def _inproj(x, pos, norm_g, w_pack, wa2_pad, ba, tm):
    bk, lk, _ = x.shape
    n = bk * lk
    nt = lk // tm
    cos, sa, sb = _rope_tables(pos)
    tok = lambda w: pl.BlockSpec((tm, w), lambda i: (i, 0))
    tab = pl.BlockSpec((tm, LANES), lambda i: (i % nt, 0))
    full = lambda a: pl.BlockSpec(a.shape, lambda i: (0,) * a.ndim)
    hm = lambda h: pl.BlockSpec((1, h, tm, DH), lambda i: (i // nt, 0, i % nt, 0))
    g2 = norm_g.reshape(1, D_MODEL)
    ba2 = ba.reshape(1, H_GLA * DK)
    out_shape = (
        jax.ShapeDtypeStruct((bk, H_NSA, lk, DH), bf16),
        jax.ShapeDtypeStruct((n, KV_ROW), f32),
        jax.ShapeDtypeStruct((n, KV_ROW), f32),
        jax.ShapeDtypeStruct((n, KV_ROW), f32),
        jax.ShapeDtypeStruct((n, LANES), f32),
        jax.ShapeDtypeStruct((n, NSA_WIDTH), f32),
        jax.ShapeDtypeStruct((n, H_GLA * DK), f32),
        jax.ShapeDtypeStruct((n, H_GLA * DK), f32),
        jax.ShapeDtypeStruct((n, H_GLA * DV), f32),
        jax.ShapeDtypeStruct((n, H_GLA * DK), f32),
        jax.ShapeDtypeStruct((n, GLA_WIDTH), f32),
    )
    out_specs = (hm(H_NSA), tok(KV_ROW), tok(KV_ROW), tok(KV_ROW),
                 tok(LANES), tok(NSA_WIDTH), tok(H_GLA * DK), tok(H_GLA * DK), tok(H_GLA * DV),
                 tok(H_GLA * DK), tok(GLA_WIDTH))
    return pl.pallas_call(
        _inproj_kernel,
        grid=(n // tm,),
        in_specs=[tok(D_MODEL), full(g2), full(w_pack), tab, tab, tab, full(wa2_pad), full(ba2)],
        out_specs=out_specs,
        out_shape=out_shape,
        compiler_params=_cparams(("parallel",)),
        name="inproj",
    )(x.reshape(n, D_MODEL), g2, w_pack, cos, sa, sb, wa2_pad, ba2)


PT_Q, PT_KVC, PT_KVS, PT_KVW, PT_GATE = 0, 512, 768, 1024, 1280
PT_ROWS = PT_GATE + 32
PK_ZN, PK_GQ, PK_GK, PK_GV, PK_ZG, PK_MISC = 0, 512, 768, 1024, 1536, 2048
PK_COLS = PK_MISC + LANES


def _inproj_prompt_kernel(x_ref, g_ref, wt_ref, w_ref, cos_ref, sin_ref, wa2_ref, ba_ref,
                          qt_ref, kvct_ref, kvst_ref, kvwt_ref, ks_ref, kw_ref, vst_ref, vwt_ref,
                          gatet_ref, zn_ref, gq_ref, gk_ref, gv_ref, la_ref, zg_ref):
    x = x_ref[...]
    ms = jnp.mean(x * x, axis=-1, keepdims=True)
    hn = (x * lax.rsqrt(ms + EPS) * g_ref[...]).astype(bf16)
    cos_t, sin_t = cos_ref[...], sin_ref[...]

    all_t = _dot_nt(wt_ref[...], hn)

    def proj_t(off, rows):
        return all_t[off:off + rows]

    def rope_rows(v):
        x1, x2 = v[:ROT_HALF], v[ROT_HALF:2 * ROT_HALF]
        return jnp.concatenate([x1 * cos_t - x2 * sin_t, x2 * cos_t + x1 * sin_t, v[2 * ROT_HALF:]], axis=0)

    q_t = proj_t(PT_Q, NSA_WIDTH)
    for h in range(H_NSA):
        qt_ref[0, h] = (rope_rows(q_t[h * DH:(h + 1) * DH]) * (DH ** -0.5 * LOG2E)).astype(bf16)

    kvct_ref[0] = proj_t(PT_KVC, KV_ROW)

    for off, kvt_ref, k_ref, vt_ref in ((PT_KVS, kvst_ref, ks_ref, vst_ref), (PT_KVW, kvwt_ref, kw_ref, vwt_ref)):
        t = proj_t(off, KV_ROW)
        k_t = jnp.concatenate([rope_rows(t[h * DH:(h + 1) * DH]) for h in range(KVH)], axis=0)
        kvt_ref[0, :LANES] = k_t
        kvt_ref[0, LANES:] = t[LANES:]
        k_tok = k_t.T
        for h in range(KVH):
            k_ref[0, h] = k_tok[:, h * DH:(h + 1) * DH].astype(bf16)
            vt_ref[0, h] = t[LANES + h * DH:LANES + (h + 1) * DH].astype(bf16)

    gatet_ref[0] = _sigmoid(proj_t(PT_GATE, PT_ROWS - PT_GATE))

    def proj(off, width):
        return _dot(hn, w_ref[:, off:off + width])

    zn_ref[...] = _silu(proj(PK_ZN, NSA_WIDTH))
    gq_ref[...] = proj(PK_GQ, H_GLA * DK) * (DK ** -0.5)
    gk_ref[...] = proj(PK_GK, H_GLA * DK)
    gv_ref[...] = proj(PK_GV, H_GLA * DV)
    zg_ref[...] = _silu(proj(PK_ZG, GLA_WIDTH))
    misc = proj(PK_MISC, LANES)
    xa = _dot(misc.astype(bf16), wa2_ref[...]) + ba_ref[...]
    la_ref[...] = _log_sigmoid(xa) / GLA_TAU


def _pack_w_in_prompt(w_in):
    o = IN_OFFSETS
    seg = lambda i: w_in[:, o[i]:o[i + 1]]
    gate_t = jnp.zeros((PT_ROWS - PT_GATE, D_MODEL), w_in.dtype).at[:3 * H_NSA].set(seg(4).T)
    w_t = jnp.concatenate([seg(0).T, seg(1).T, seg(2).T, seg(3).T, gate_t], axis=0).astype(bf16)
    misc = jnp.zeros((D_MODEL, LANES), w_in.dtype).at[:, MISC_GLR:MISC_GLR + GLA_LR].set(seg(9))
    w_tok = jnp.concatenate([seg(5), seg(6), seg(7), seg(8), seg(10), misc], axis=1).astype(bf16)
    return w_t, w_tok


def _inproj_prompt(x, norm_g, w_t, w_tok, wa2_pad, ba, tm):
    b, lq, _ = x.shape
    n = b * lq
    nt = lq // tm
    inv = ROPE_THETA ** (-(jnp.arange(ROT_HALF, dtype=f32) / ROT_HALF))
    ang = inv[:, None] * jnp.arange(lq).astype(f32)[None, :]
    cos_t, sin_t = jnp.cos(ang), jnp.sin(ang)
    tok = lambda w: pl.BlockSpec((tm, w), lambda i: (i, 0))
    tab = pl.BlockSpec((ROT_HALF, tm), lambda i: (0, i % nt))
    full = lambda a: pl.BlockSpec(a.shape, lambda i: (0,) * a.ndim)
    feat = lambda r: pl.BlockSpec((1, r, tm), lambda i: (i // nt, 0, i % nt))
    headf = lambda h: pl.BlockSpec((1, h, DH, tm), lambda i: (i // nt, 0, 0, i % nt))
    headt = lambda h: pl.BlockSpec((1, h, tm, DH), lambda i: (i // nt, 0, i % nt, 0))
    g2 = norm_g.reshape(1, D_MODEL)
    ba2 = ba.reshape(1, H_GLA * DK)
    sds = jax.ShapeDtypeStruct
    out_shape = (
        sds((b, H_NSA, DH, lq), bf16),
        sds((b, KV_ROW, lq), f32), sds((b, KV_ROW, lq), f32), sds((b, KV_ROW, lq), f32),
        sds((b, KVH, lq, DH), bf16), sds((b, KVH, lq, DH), bf16),
        sds((b, KVH, DH, lq), bf16), sds((b, KVH, DH, lq), bf16),
        sds((b, PT_ROWS - PT_GATE, lq), f32),
        sds((n, NSA_WIDTH), f32), sds((n, H_GLA * DK), f32), sds((n, H_GLA * DK), f32),
        sds((n, H_GLA * DV), f32), sds((n, H_GLA * DK), f32), sds((n, GLA_WIDTH), f32),
    )
    out_specs = (headf(H_NSA), feat(KV_ROW), feat(KV_ROW), feat(KV_ROW), headt(KVH), headt(KVH),
                 headf(KVH), headf(KVH), feat(PT_ROWS - PT_GATE),
                 tok(NSA_WIDTH), tok(H_GLA * DK), tok(H_GLA * DK), tok(H_GLA * DV), tok(H_GLA * DK),
                 tok(GLA_WIDTH))
    return pl.pallas_call(
        _inproj_prompt_kernel,
        grid=(n // tm,),
        in_specs=[tok(D_MODEL), full(g2), full(w_t), full(w_tok), tab, tab, full(wa2_pad), full(ba2)],
        out_specs=out_specs,
        out_shape=out_shape,
        compiler_params=_cparams(("parallel",)),
        name="inproj_prompt",
    )(x.reshape(n, D_MODEL), g2, w_t, w_tok, cos_t, sin_t, wa2_pad, ba2)


HID_W = 2 * KVH * CMP_HID
CHUNKS_PER_PAGE = PAGE_SIZE // CMP_STRIDE
PAGE_GROUP = 32


def _cmp_w1_pairs(cmp_w1):
    ratio = CMP_LEN // CMP_STRIDE
    w1r = cmp_w1.reshape(2, ratio, CMP_STRIDE // 2, 2, DH, CMP_HID)
    eye = jnp.eye(KVH, dtype=cmp_w1.dtype)
    big = jnp.einsum('crpjdh,kK->cpjKdrkh', w1r, eye)
    return big.reshape(2, CMP_STRIDE // 2, 2 * KVH * DH, ratio * KVH * CMP_HID).astype(bf16)


def _cmp_w2_heads(cmp_w2):
    eye = jnp.eye(KVH, dtype=cmp_w2.dtype)
    return jnp.einsum('chd,kK->ckhKd', cmp_w2, eye).reshape(2, KVH * CMP_HID, KVH * DH).astype(bf16)


def _row_gather_matrix():
    r = np.arange(PAGE_SIZE)
    src = (r % CHUNKS_PER_PAGE) * CMP_STRIDE + r // CHUNKS_PER_PAGE
    return jnp.asarray((r[None, :] == src[:, None]).astype(np.float32), dtype=bf16)


def _compress_pages(page_at, n_pages, perm, rows_scr, w1_ref, bias_ref, w2_ref):
    cpp = CHUNKS_PER_PAGE
    kd = KVH * DH

    def to_rows(g, carry):
        pages = [g * PAGE_GROUP + u for u in range(PAGE_GROUP)]
        gathered = [_dot_nt(perm, page_at(p).astype(bf16)) for p in pages]
        for p, t in zip(pages, gathered):
            r0 = pl.multiple_of(p * cpp, cpp)
            for c in range(2):
                for s in range(CMP_STRIDE):
                    rows_scr[c, s, pl.ds(r0, cpp), :] = t[s * cpp:(s + 1) * cpp, c * kd:(c + 1) * kd]
        return carry

    lax.fori_loop(0, n_pages // PAGE_GROUP, to_rows, 0)

    nc = n_pages * cpp
    kv = []
    for c in range(2):
        lhs = jnp.concatenate([rows_scr[c, s] for s in range(CMP_STRIDE)], axis=-1).astype(bf16)
        part = _dot(lhs, w1_ref[c].reshape(CMP_STRIDE * kd, w1_ref.shape[-1]))
        half = part.shape[1] // 2
        pre = part[:, :half] + pltpu.roll(part[:, half:], nc - 1, 0)
        hid = _silu(pre + bias_ref[:, c * half:(c + 1) * half])
        kv.append(_dot(hid.astype(bf16), w2_ref[c]))
    return kv


def _cmp_bias_kernel(pe_ref, w1_ref, b1_ref, o_ref):
    for c in range(2):
        o_ref[c] = _dot(pe_ref[c], w1_ref[c]) + b1_ref[c]


def _cmp_bias(cmp_pe, cmp_w1, cmp_b1):
    pe = jnp.broadcast_to(cmp_pe.reshape(2, 1, CMP_LEN * DH), (2, 8, CMP_LEN * DH))
    b1 = jnp.broadcast_to(cmp_b1.reshape(2, 1, CMP_HID), (2, 8, CMP_HID))
    out = pl.pallas_call(
        _cmp_bias_kernel,
        out_shape=jax.ShapeDtypeStruct((2, 8, CMP_HID), f32),
        name="cmp_bias",
    )(pe, cmp_w1, b1)
    row = out[:, 0, :]
    return jnp.broadcast_to(row[:, None, :], (2, KVH, CMP_HID)).reshape(1, HID_W)


def _masked_softmax(s, mask, axis, exp=jnp.exp):
    s = jnp.where(mask, s, NEG)
    e = exp(s - jnp.max(s, axis=axis, keepdims=True))
    return jnp.where(mask, e * (1.0 / jnp.sum(e, axis=axis, keepdims=True)), 0.0)


def _split_bf16(x):
    hi = x.astype(bf16)
    return hi, (x - hi.astype(f32)).astype(bf16)


def _topk_mask(score, k, axis):
    n = score.shape[axis]
    idx = lax.broadcasted_iota(jnp.int32, score.shape, axis)
    sel = jnp.zeros(score.shape, f32)
    for _ in range(k):
        m = jnp.max(score, axis=axis, keepdims=True)
        first = jnp.min(jnp.where(score == m, idx, n), axis=axis, keepdims=True)
        pick = idx == first
        sel = jnp.where(pick, 1.0, sel)
        score = jnp.where(pick, NEG, score)
    return sel


def _topk_rows_by_rank(score, k):
    r, n = score.shape
    cols = jnp.concatenate([score, jnp.zeros((n - r, n), f32)], axis=0).T
    earlier = jnp.where(lax.broadcasted_iota(jnp.int32, (n, n), 0) < lax.broadcasted_iota(jnp.int32, (n, n), 1),
                        1.0, 0.0)
    rows = []
    for c in range(r):
        col, row = cols[:, c:c + 1], score[c:c + 1, :]
        before = jnp.where(col > row, 1.0, jnp.where(col == row, earlier, 0.0))
        rows.append(jnp.where(jnp.sum(before, axis=0, keepdims=True) < k, 1.0, 0.0))
    return jnp.concatenate(rows, axis=0)


def _cover_t(n_slc, n_chunk):
    start = np.arange(n_chunk)[None, :] * CMP_STRIDE
    j = np.arange(n_slc)[:, None]
    cov = (start < (j + 1) * SLC_LEN) & (start + CMP_LEN > j * SLC_LEN)
    return jnp.asarray(cov.astype(np.float32), dtype=bf16)


def _cmp_attn_kernel(kvct_ref, perm_ref, w1_ref, bias_ref, w2_ref, cos_ref, sa_ref, sb_ref, qt_ref, gatet_ref,
                     cov_ref, ot_ref, selt_ref, rows_scr, kc_ref, vct_ref, *, tq):
    t = pl.program_id(1)

    @pl.when(t == 0)
    def _():
        page_at = lambda p: kvct_ref[0, :, pl.ds(pl.multiple_of(p * PAGE_SIZE, PAGE_SIZE), PAGE_SIZE)]
        k, v = _compress_pages(page_at, kvct_ref.shape[2] // PAGE_SIZE, perm_ref[...], rows_scr,
                               w1_ref, bias_ref, w2_ref)
        k = _rope128(k, cos_ref[...], sa_ref[...], sb_ref[...])
        v_t = v.T
        for h in range(KVH):
            kc_ref[h] = k[:, h * DH:(h + 1) * DH].astype(bf16)
            vct_ref[h] = v_t[h * DH:(h + 1) * DH].astype(bf16)

    nc = kc_ref.shape[1]
    nb = cov_ref.shape[0]
    qpos = t * tq + lax.broadcasted_iota(jnp.int32, (1, tq), 1)
    end_pos = lax.broadcasted_iota(jnp.int32, (nc, 1), 0) * CMP_STRIDE + (CMP_LEN - 1)
    mask_t = end_pos <= qpos
    gate_t = gatet_ref[0]
    cov = cov_ref[...]

    jblk = lax.broadcasted_iota(jnp.int32, (nb, tq), 0)
    qblk = qpos // SLC_LEN
    valid = jblk <= qblk
    forced = (jblk == 0) | (jblk == qblk) | (jblk == qblk - 1)

    heads = range(H_NSA)
    s_t = [_dot(kc_ref[h // G_NSA], qt_ref[0, h]) for h in heads]
    p_t = [_masked_softmax(s, mask_t, axis=0, exp=jnp.exp2) for s in s_t]
    for h in heads:
        ot_ref[0, h * DH:(h + 1) * DH] = _dot(vct_ref[h // G_NSA], p_t[h].astype(bf16)) * gate_t[h:h + 1]
    scores = []
    for kh in range(KVH):
        psum = functools.reduce(lambda a, b: a + b, p_t[kh * G_NSA:(kh + 1) * G_NSA])
        hi, lo = _split_bf16(psum)
        imp = _dot(cov, hi) + _dot(cov, lo)
        scores.append(jnp.where(valid, jnp.where(forced, FORCE_SCORE, imp), -1.0))
    picked = _topk_mask(jnp.concatenate(scores, axis=1), SLC_TOP, axis=0)
    for kh in range(KVH):
        sel_t = jnp.where(valid & (picked[:, kh * tq:(kh + 1) * tq] > 0.5), 0.0, NEG)
        if nb < LANES:
            sel_t = jnp.concatenate([sel_t, jnp.full((LANES - nb, tq), NEG, f32)], axis=0)
        selt_ref[0, kh] = sel_t


def _cmp_attn(kvc_t, w1_pairs, bias, w2_heads, q_t, gate_t, tq):
    b, _, lq = kvc_t.shape
    assert lq % (PAGE_SIZE * PAGE_GROUP) == 0
    nc = lq // CMP_STRIDE
    n_slc = lq // SLC_LEN
    perm = _row_gather_matrix()
    end_pos = jnp.arange(nc) * CMP_STRIDE + (CMP_LEN - 1)
    cos, sa, sb = _rope_tables(end_pos)
    cov = _cover_t(n_slc, nc)
    full = lambda a: pl.BlockSpec(a.shape, lambda i, t: (0,) * a.ndim)
    return pl.pallas_call(
        functools.partial(_cmp_attn_kernel, tq=tq),
        grid=(b, lq // tq),
        in_specs=[pl.BlockSpec((1, KV_ROW, lq), lambda i, t: (i, 0, 0)),
                  full(perm), full(w1_pairs), full(bias), full(w2_heads), full(cos), full(sa), full(sb),
                  pl.BlockSpec((1, H_NSA, DH, tq), lambda i, t: (i, 0, 0, t)),
                  pl.BlockSpec((1, gate_t.shape[1], tq), lambda i, t: (i, 0, t)),
                  full(cov)],
        out_specs=(pl.BlockSpec((1, NSA_WIDTH, tq), lambda i, t: (i, 0, t)),
                   pl.BlockSpec((1, KVH, LANES, tq), lambda i, t: (i, 0, 0, t))),
        out_shape=(jax.ShapeDtypeStruct((b, NSA_WIDTH, lq), f32),
                   jax.ShapeDtypeStruct((b, KVH, LANES, lq), f32)),
        scratch_shapes=[pltpu.VMEM((2, CMP_STRIDE, nc, KVH * DH), f32),
                        pltpu.VMEM((KVH, nc, DH), bf16), pltpu.VMEM((KVH, DH, nc), bf16)],
        compiler_params=_cparams(("parallel", "arbitrary")),
        name="cmp_attn",
    )(kvc_t, perm, w1_pairs, bias, w2_heads, cos, sa, sb, q_t, gate_t, cov)


def _block_expand(n_slc_pad, n_keys):
    e = (np.arange(n_keys)[None, :] // SLC_LEN) == np.arange(n_slc_pad)[:, None]
    return jnp.asarray(e.astype(np.float32), dtype=bf16)


def _mask_bias_t(allowed):
    bias = jnp.where(allowed, 0.0, NEG)
    return jnp.concatenate([bias] * G_NSA, axis=1)


def _sublane_fold(x, op):
    return functools.reduce(op, [x[r:r + 8] for r in range(0, x.shape[0], 8)])


def _slc_win_kernel(qt_ref, ks_ref, vst_ref, kw_ref, vwt_ref, selt_ref, gatet_ref, ocmpt_ref, zn_ref,
                    o_ref, s_a, s_b, m_a, m_b, ws_a, ws_b, wm_a, wm_b, acc_ref, wacc_ref, *, tq, tk, nt):
    t = pl.program_id(2)

    @pl.when(t == 0)
    def _():
        m_b[...] = jnp.full(m_b.shape, NEG, f32)
        ws_b[...] = jnp.zeros(ws_b.shape, f32)
        wm_b[...] = jnp.zeros(wm_b.shape, f32)

    step = functools.partial(_slc_win_step, qt_ref, ks_ref, vst_ref, kw_ref, vwt_ref, selt_ref, gatet_ref,
                             ocmpt_ref, zn_ref, o_ref, acc_ref, wacc_ref, tq=tq, tk=tk, nt=nt)

    @pl.when(t % 2 == 0)
    def _():
        step(s_a, s_b, m_a, m_b, ws_a, ws_b, wm_a, wm_b)

    @pl.when(t % 2 == 1)
    def _():
        step(s_b, s_a, m_b, m_a, ws_b, ws_a, wm_b, wm_a)


ONES_ROWS = 16


def _slc_win_step(qt_ref, ks_ref, vst_ref, kw_ref, vwt_ref, selt_ref, gatet_ref, ocmpt_ref, zn_ref, o_ref,
                  acc_ref, wacc_ref, s_cur, s_prev, m_cur, m_last, ws_cur, ws_prev, wm_cur, wm_last,
                  *, tq, tk, nt):
    kh = pl.program_id(1)
    t = pl.program_id(2)
    lq = ks_ref.shape[2]
    band = WINDOW + tq
    blocks_per_tile = tk // SLC_LEN
    key_tiles = lambda tile: (tile * tq + tq + tk - 1) // tk
    n_apply = jnp.where(t >= 1, key_tiles(t - 1), 0)
    n_plain = jnp.where(t < nt, (t * tq) // tk, 0)
    q0 = t * tq
    qpos = q0 + lax.broadcasted_iota(jnp.int32, (1, tq), 1)
    q_t = jnp.concatenate([qt_ref[0, g] for g in range(G_NSA)], axis=1)
    ones = jnp.ones((ONES_ROWS, tk), bf16)

    m_cur[...] = jnp.full(m_cur.shape, NEG, f32)
    m_prev = jnp.max(m_last[...], axis=0, keepdims=True)
    acc_ref[...] = jnp.zeros(acc_ref.shape, f32)

    def score(kt, causal):
        k0 = pl.multiple_of(kt * tk, tk)
        j0 = pl.multiple_of(kt * blocks_per_tile, blocks_per_tile)
        sel = selt_ref[0, 0, pl.ds(j0, blocks_per_tile), :]
        bias = jnp.concatenate([jnp.broadcast_to(sel[j:j + 1], (SLC_LEN, tq)) for j in range(blocks_per_tile)],
                               axis=0)
        if causal:
            kpos = k0 + lax.broadcasted_iota(jnp.int32, (tk, 1), 0)
            bias = jnp.where(kpos <= qpos, bias, NEG)
        s = _dot(ks_ref[0, 0, pl.ds(k0, tk), :], q_t) + jnp.concatenate([bias] * G_NSA, axis=1)
        s_cur[pl.ds(k0, tk), :] = s
        m_cur[...] = jnp.maximum(m_cur[...], _sublane_fold(s, jnp.maximum))

    def apply(kt):
        k0 = pl.multiple_of(kt * tk, tk)
        p = jnp.exp2(s_prev[pl.ds(k0, tk), :] - m_prev).astype(bf16)
        acc_ref[...] += _dot(jnp.concatenate([vst_ref[0, 0, :, pl.ds(k0, tk)], ones], axis=0), p)

    def both_body(kt, carry):
        score(kt, causal=False)
        apply(kt)
        return carry

    def apply_body(kt, carry):
        apply(kt)
        return carry

    def both2_body(i, carry):
        for u in range(2):
            score(2 * i + u, causal=False)
            apply(2 * i + u)
        return carry

    n_both = jnp.minimum(n_plain, n_apply)
    lax.fori_loop(0, n_both // 2, both2_body, 0)
    lax.fori_loop(n_both // 2 * 2, n_both, both_body, 0)
    lax.fori_loop(n_both, n_apply, apply_body, 0)
    score(jnp.minimum((t * tq) // tk, lq // tk - 1), causal=True)

    w0 = pl.multiple_of(jnp.clip(q0 - WINDOW, 0, lq - band), tq)
    w0_prev = pl.multiple_of(jnp.clip(q0 - tq - WINDOW, 0, lq - band), tq)
    wm_prev = jnp.max(wm_last[...], axis=0, keepdims=True)
    ones_w = jnp.ones((ONES_ROWS, tq), bf16)
    wmax, wacc = None, None
    for j in range(band // tq):
        rows = pl.ds(j * tq, tq)
        kpos = w0 + j * tq + lax.broadcasted_iota(jnp.int32, (tq, 1), 0)
        s = (_dot(kw_ref[0, 0, pl.ds(pl.multiple_of(w0 + j * tq, tq), tq), :], q_t)
             + _mask_bias_t((kpos <= qpos) & (kpos > qpos - WINDOW)))
        ws_cur[rows, :] = s
        fold = _sublane_fold(s, jnp.maximum)
        wmax = fold if wmax is None else jnp.maximum(wmax, fold)
        p = jnp.exp2(ws_prev[rows, :] - wm_prev).astype(bf16)
        v_ext = jnp.concatenate([vwt_ref[0, 0, :, pl.ds(pl.multiple_of(w0_prev + j * tq, tq), tq)], ones_w], axis=0)
        part = _dot(v_ext, p)
        wacc = part if wacc is None else wacc + part
    wm_cur[...] = wmax
    wacc_ref[...] = wacc

    @pl.when(t >= 1)
    def _():
        o_slc = acc_ref[:DH] / acc_ref[DH:DH + 1]
        o_win = wacc_ref[:DH] / wacc_ref[DH:DH + 1]
        gate_t = gatet_ref[0]
        row = lambda c, g: jnp.where(kh == 0, gate_t[c * H_NSA + g:c * H_NSA + g + 1],
                                     gate_t[c * H_NSA + G_NSA + g:c * H_NSA + G_NSA + g + 1])
        g_slc = jnp.concatenate([row(1, g) for g in range(G_NSA)], axis=1)
        g_win = jnp.concatenate([row(2, g) for g in range(G_NSA)], axis=1)
        o_t = g_slc * o_slc + g_win * o_win
        o_heads = jnp.concatenate([o_t[:, g * tq:(g + 1) * tq] for g in range(G_NSA)], axis=0)
        o_ref[0] = ((ocmpt_ref[0] + o_heads).T * zn_ref[...]).astype(o_ref.dtype)


def _slc_win(q_t, ks, vs_t, kw, vw_t, sel_t, gate_t, o_cmp_t, zn, tq, tk):
    b, _, _, lq = q_t.shape
    assert lq >= WINDOW + tq and lq % tk == 0 and tk % (8 * SLC_LEN) == 0
    gw = G_NSA * DH
    k_spec = pl.BlockSpec((1, 1, lq, DH), lambda i, k, t: (i, k, 0, 0))
    vt_spec = pl.BlockSpec((1, 1, DH, lq), lambda i, k, t: (i, k, 0, 0))
    lanes = G_NSA * tq
    band = WINDOW + tq
    nt = lq // tq
    scored = lambda t: jnp.minimum(t, nt - 1)
    drained = lambda t: jnp.maximum(t - 1, 0)
    return pl.pallas_call(
        functools.partial(_slc_win_kernel, tq=tq, tk=tk, nt=nt),
        grid=(b, KVH, nt + 1),
        in_specs=[pl.BlockSpec((1, G_NSA, DH, tq), lambda i, k, t: (i, k, 0, scored(t))),
                  k_spec, vt_spec, k_spec, vt_spec,
                  pl.BlockSpec((1, 1, LANES, tq), lambda i, k, t: (i, k, 0, scored(t))),
                  pl.BlockSpec((1, gate_t.shape[1], tq), lambda i, k, t: (i, 0, drained(t))),
                  pl.BlockSpec((1, gw, tq), lambda i, k, t: (i, k, drained(t))),
                  pl.BlockSpec((tq, gw), lambda i, k, t: (i * nt + drained(t), k))],
        out_specs=pl.BlockSpec((1, tq, gw), lambda i, k, t: (i, drained(t), k)),
        out_shape=jax.ShapeDtypeStruct((b, lq, NSA_WIDTH), bf16),
        scratch_shapes=[pltpu.VMEM((lq, lanes), f32), pltpu.VMEM((lq, lanes), f32),
                        pltpu.VMEM((8, lanes), f32), pltpu.VMEM((8, lanes), f32),
                        pltpu.VMEM((band, lanes), f32), pltpu.VMEM((band, lanes), f32),
                        pltpu.VMEM((8, lanes), f32), pltpu.VMEM((8, lanes), f32),
                        pltpu.VMEM((DH + ONES_ROWS, lanes), f32),
                        pltpu.VMEM((DH + ONES_ROWS, lanes), f32)],
        compiler_params=_cparams(("parallel", "parallel", "arbitrary")),
        name="slc_win",
    )(q_t, ks, vs_t, kw, vw_t, sel_t, gate_t, o_cmp_t, zn)


def _rms_gain(o, gain):
    return o * lax.rsqrt(jnp.mean(o * o, axis=-1, keepdims=True) + EPS) * gain


def _gla_kernel(q_ref, k_ref, v_ref, la_ref, z_ref, gain_ref, s0_ref, y_ref, st_ref, s_scr, *, tl, chunk):
    t = pl.program_id(1)

    @pl.when(t == 0)
    def _():
        s_scr[...] = s0_ref[0]

    c = chunk
    pairs = s_scr.shape[0]
    row = lax.broadcasted_iota(jnp.int32, (c, c), 0)
    col = lax.broadcasted_iota(jnp.int32, (c, c), 1)
    causal = col <= row
    tril = jnp.where(causal, 1.0, 0.0).astype(bf16)
    lane = lax.broadcasted_iota(jnp.int32, (1, LANES), 1)
    head_mask = [jnp.where(lane < DK, 1.0, 0.0), jnp.where(lane >= DK, 1.0, 0.0)]
    gain = gain_ref[...]

    units = [(ci, pr) for ci in range(tl // c) for pr in range(pairs)]
    rows_of = lambda ci: slice(ci * c, (ci + 1) * c)
    grp_of = lambda pr: slice(pr * LANES, (pr + 1) * LANES)
    bcs = []
    for ci, pr in units:
        hi, lo = _split_bf16(la_ref[rows_of(ci), grp_of(pr)])
        bcs.append(_dot(tril, hi) + _dot(tril, lo))
    scaled = []
    for (ci, pr), bc in zip(units, bcs):
        q, k = q_ref[rows_of(ci), grp_of(pr)], k_ref[rows_of(ci), grp_of(pr)]
        ref_row = bc[c // 2 - 1:c // 2, :]
        b_last = bc[c - 1:c, :]
        scaled.append((q * jnp.exp(bc - ref_row), (k * jnp.exp(ref_row - bc)).astype(bf16),
                       q * jnp.exp(bc), k * jnp.exp(b_last - bc), jnp.exp(b_last)))
    attn = []
    for (ci, pr), (q_a, k_a, _, _, _) in zip(units, scaled):
        attn.append([jnp.where(causal, _dot_nt((q_a * head_mask[h]).astype(bf16), k_a), 0.0).astype(bf16)
                     for h in range(2)])
    chunks = []
    for (ci, pr), (_, _, q_s, k_s, decay), a in zip(units, scaled, attn):
        rows = rows_of(ci)
        o_intra, q_inter, add = [], [], None
        for h in range(2):
            vh = v_ref[rows, (2 * pr + h) * DV:(2 * pr + h + 1) * DV].astype(bf16)
            o_intra.append(_dot(a[h], vh))
            q_inter.append((q_s * head_mask[h]).astype(bf16))
            u = _dot_tn(vh, (k_s * head_mask[h]).astype(bf16))
            add = u if add is None else add + u
        chunks.append((rows, pr, decay, add, o_intra, q_inter))

    st = [s_scr[pr] for pr in range(pairs)]
    for rows, pr, decay, add, o_intra, q_inter in chunks:
        st_b = st[pr].astype(bf16)
        for h in range(2):
            cols = slice((2 * pr + h) * DV, (2 * pr + h + 1) * DV)
            o = o_intra[h] + _dot_nt(q_inter[h], st_b)
            y_ref[rows, cols] = (_rms_gain(o, gain) * z_ref[rows, cols]).astype(y_ref.dtype)
        st[pr] = st[pr] * decay + add
    for pr in range(pairs):
        s_scr[pr] = st[pr]
        st_ref[0, pr] = st[pr]


def _gla(gq, gk, gv, la, zg, gain, state0, b, tl, chunk):
    n = gq.shape[0]
    nt = n // b // tl
    pairs = H_GLA // 2
    qk = pl.BlockSpec((tl, H_GLA * DK), lambda i, t: (i * nt + t, 0))
    vz = pl.BlockSpec((tl, GLA_WIDTH), lambda i, t: (i * nt + t, 0))
    st_spec = pl.BlockSpec((1, pairs, DV, LANES), lambda i, t: (i, 0, 0, 0))
    gain2 = gain.reshape(1, DV)
    s0 = state0.reshape(b, pairs, 2, DK, DV).transpose(0, 1, 4, 2, 3).reshape(b, pairs, DV, LANES)
    y, st = pl.pallas_call(
        functools.partial(_gla_kernel, tl=tl, chunk=chunk),
        grid=(b, nt),
        in_specs=[qk, qk, vz, qk, vz, pl.BlockSpec((1, DV), lambda i, t: (0, 0)), st_spec],
        out_specs=(vz, st_spec),
        out_shape=(jax.ShapeDtypeStruct((n, GLA_WIDTH), bf16),
                   jax.ShapeDtypeStruct((b, pairs, DV, LANES), f32)),
        scratch_shapes=[pltpu.VMEM((pairs, DV, LANES), f32)],
        compiler_params=_cparams(("parallel", "arbitrary")),
        name="gla",
    )(gq, gk, gv, la, zg, gain2, s0)
    state = st.reshape(b, pairs, DV, 2, DK).transpose(0, 1, 3, 4, 2).reshape(b, H_GLA, DK, DV)
    return y, state


def _outproj_kernel(x_ref, yn_ref, yg_ref, w_ref, g_ref, y_ref):
    mix = _dot(yn_ref[...].astype(bf16), w_ref[:NSA_WIDTH, :]) + _dot(yg_ref[...].astype(bf16), w_ref[NSA_WIDTH:, :])
    y_ref[...] = _rms_gain(x_ref[...] + mix, g_ref[...])


def _outproj(x, y_nsa, y_gla, w_out, gain, tm):
    n = x.shape[0]
    tok = lambda w: pl.BlockSpec((tm, w), lambda i: (i, 0))
    g2 = gain.reshape(1, D_MODEL)
    return pl.pallas_call(
        _outproj_kernel,
        grid=(n // tm,),
        in_specs=[tok(D_MODEL), tok(NSA_WIDTH), tok(GLA_WIDTH),
                  pl.BlockSpec(w_out.shape, lambda i: (0, 0)), pl.BlockSpec(g2.shape, lambda i: (0, 0))],
        out_specs=tok(D_MODEL),
        out_shape=jax.ShapeDtypeStruct((n, D_MODEL), f32),
        compiler_params=_cparams(("parallel",)),
        name="outproj",
    )(x, y_nsa, y_gla, w_out, g2)


def _prompt_path(x, wts, w_t, w_tok):
    (norm_g, _, wa2_pad, ba, w1_pairs, cmp_bias, w2_heads, gla_gain, w_out, out_gain) = wts
    b, lq, _ = x.shape
    (q_t, kvc_t, kvs_t, kvw_t, ks, kw, vs_t, vw_t, gate_t, zn, gq, gk, gv, la, zg) = _inproj_prompt(
        x, norm_g, w_t, w_tok, wa2_pad, ba, tm=512)
    o_cmp_t, sel_t = _cmp_attn(kvc_t, w1_pairs, cmp_bias, w2_heads, q_t, gate_t, tq=1024)
    y_nsa = _slc_win(q_t, ks, vs_t, kw, vw_t, sel_t, gate_t, o_cmp_t, zn, tq=256, tk=512)
    y_gla, state = _gla(gq, gk, gv, la, zg, gla_gain, jnp.zeros((b, H_GLA, DK, DV), f32), b, tl=1024,
                        chunk=GLA_CHUNK)
    y = _outproj(x.reshape(b * lq, D_MODEL), y_nsa.reshape(b * lq, NSA_WIDTH), y_gla, w_out, out_gain, tm=1024)
    kv5 = lambda a: a.reshape(b, 2, KVH, DH, -1).transpose(0, 4, 1, 2, 3)
    wlen = min(WINDOW, lq)
    return (y.reshape(b, lq, D_MODEL), kv5(kvc_t), kv5(kvs_t), kv5(kvw_t[:, :, lq - wlen:]), state)


def _pages_native(cache):
    n_pool = cache.shape[0]
    return cache.transpose(0, 2, 3, 4, 1).reshape(n_pool, 2, KVH * DH, cache.shape[1])


def _page_gather(make_copies):
    i = pl.program_id(0)
    slot = i % 2

    @pl.when(i == 0)
    def _():
        for c in make_copies(0, 0):
            c.start()

    @pl.when(i + 1 < pl.num_programs(0))
    def _():
        for c in make_copies(i + 1, 1 - slot):
            c.start()

    for c in make_copies(i, slot):
        c.wait()
    return slot


def _group_sum_matrix(n_tok):
    r = np.arange(KVH * n_tok)[:, None]
    c = np.arange(H_NSA * n_tok)[None, :]
    m = ((c // (G_NSA * n_tok)) == (r // n_tok)) & ((c % n_tok) == (r % n_tok))
    return m.astype(np.float32)


def _cmp_attn_sample_kernel(pt_ref, cache_ref, perm_ref, w1_ref, bias_ref, w2_ref, cos_ref, sa_ref, sb_ref, q_ref,
                            gate_ref, cov_ref, gs_ref, gst_ref, o_ref, sel_ref, buf, rows_scr, sem,
                            *, n_pages, n_tok, past_len):
    def copies(seq, slot):
        return [pltpu.make_async_copy(cache_ref.at[pt_ref[seq, p]], buf.at[slot, p], sem.at[slot])
                for p in range(n_pages)]

    slot = _page_gather(copies)

    page_at = lambda p: buf[slot, p].reshape(2 * KVH * DH, PAGE_SIZE)
    k, v = _compress_pages(page_at, n_pages, perm_ref[...], rows_scr, w1_ref, bias_ref, w2_ref)
    k = _rope128(k, cos_ref[...], sa_ref[...], sb_ref[...])
    nc = n_pages * CHUNKS_PER_PAGE
    rows = H_NSA * n_tok
    qpos = past_len + lax.broadcasted_iota(jnp.int32, (rows, 1), 0) % n_tok
    end_c = lax.broadcasted_iota(jnp.int32, (1, nc), 1) * CMP_STRIDE + (CMP_LEN - 1)
    mask = end_c <= qpos
    p = _masked_softmax(_dot_nt(q_ref[0], k.astype(bf16)), mask, axis=-1)
    o_ref[0] = _dot(p.astype(bf16), v.astype(bf16)) * gate_ref[0][:, 0:1]
    hi, lo = _split_bf16(p)
    gs = gs_ref[...]
    ph, pl_ = _split_bf16(_dot(gs, hi) + _dot(gs, lo))
    imp = _dot(ph, cov_ref[...]) + _dot(pl_, cov_ref[...])
    n_blk = imp.shape[1]
    jblk = lax.broadcasted_iota(jnp.int32, imp.shape, 1)
    forced = (jblk == 0) | (jblk == n_blk - 1)
    sel = _topk_rows_by_rank(jnp.where(forced, FORCE_SCORE, imp), SLC_TOP - 1)
    sel_ref[0] = _dot(gst_ref[...], sel.astype(bf16)).astype(bf16)


def _cmp_attn_sample(cache_t, page_table, w1_pairs, bias, w2_heads, q_blk, gate_rows, n_tok):
    b, n_pages = page_table.shape
    past_len = n_pages * PAGE_SIZE
    nc = past_len // CMP_STRIDE
    rows = H_NSA * n_tok
    n_blk = past_len // SLC_LEN
    end_pos = jnp.arange(nc) * CMP_STRIDE + (CMP_LEN - 1)
    cos, sa, sb = _rope_tables(end_pos)
    cov = _cover_t(n_blk, nc).T
    gs = jnp.asarray(_group_sum_matrix(n_tok), dtype=bf16)
    gst = gs.T
    perm = _row_gather_matrix()
    full = lambda a: pl.BlockSpec(a.shape, lambda i, pt: (0,) * a.ndim)
    seq = lambda a: pl.BlockSpec((1,) + a.shape[1:], lambda i, pt: (i,) + (0,) * (a.ndim - 1))
    grid_spec = pltpu.PrefetchScalarGridSpec(
        num_scalar_prefetch=1,
        grid=(b,),
        in_specs=[pl.BlockSpec(memory_space=pl.ANY), full(perm), full(w1_pairs), full(bias), full(w2_heads),
                  full(cos), full(sa), full(sb), seq(q_blk), seq(gate_rows), full(cov), full(gs), full(gst)],
        out_specs=(pl.BlockSpec((1, rows, LANES), lambda i, pt: (i, 0, 0)),
                   pl.BlockSpec((1, rows, n_blk), lambda i, pt: (i, 0, 0))),
        scratch_shapes=[pltpu.VMEM((2, n_pages) + cache_t.shape[1:], f32),
                        pltpu.VMEM((2, CMP_STRIDE, nc, KVH * DH), f32),
                        pltpu.SemaphoreType.DMA((2,))],
    )
    return pl.pallas_call(
        functools.partial(_cmp_attn_sample_kernel, n_pages=n_pages, n_tok=n_tok, past_len=past_len),
        grid_spec=grid_spec,
        out_shape=(jax.ShapeDtypeStruct((b, rows, LANES), f32),
                   jax.ShapeDtypeStruct((b, rows, n_blk), bf16)),
        compiler_params=_cparams(("arbitrary",)),
        name="cmp_attn_sample",
    )(page_table, cache_t, perm, w1_pairs, bias, w2_heads, cos, sa, sb, q_blk, gate_rows, cov, gs, gst)


def _slc_win_sample_kernel(pt_ref, cache_ref, q_ref, sel_ref, e_ref, snew_ref, cw_ref, wnew_ref, wnewt_ref,
                           gate_ref, ocmp_ref, zn_ref, o_ref, wout_ref, buf, sem, *, n_pages, n_tok, win_off):
    def copies(seq, slot):
        return [pltpu.make_async_copy(cache_ref.at[pt_ref[seq, p], c],
                                      buf.at[slot, c, :, pl.ds(p * PAGE_SIZE, PAGE_SIZE)], sem.at[slot])
                for p in range(n_pages) for c in range(2)]

    slot = _page_gather(copies)
    rows = H_NSA * n_tok
    q = q_ref[0]
    tok = lax.broadcasted_iota(jnp.int32, (rows, 1), 0) % n_tok
    new_i = lax.broadcasted_iota(jnp.int32, (1, snew_ref.shape[1]), 1)
    new_ok = (new_i <= tok) & (new_i < n_tok)

    def attend(keys_t, vals_t, allowed, new_ref):
        s = jnp.where(allowed, _dot(q, keys_t), NEG)
        k_new = new_ref[0][:, :LANES].astype(bf16)
        v_new = new_ref[0][:, LANES:].astype(bf16)
        s_new = jnp.where(new_ok, _dot_nt(q, k_new), NEG)
        m = jnp.maximum(jnp.max(s, axis=-1, keepdims=True), jnp.max(s_new, axis=-1, keepdims=True))
        p = jnp.where(allowed, jnp.exp(s - m), 0.0)
        p_new = jnp.where(new_ok, jnp.exp(s_new - m), 0.0)
        l = jnp.sum(p, axis=-1, keepdims=True) + jnp.sum(p_new, axis=-1, keepdims=True)
        return (_dot_nt(p.astype(bf16), vals_t) + _dot(p_new.astype(bf16), v_new)) / l

    picked = _dot(sel_ref[0], e_ref[...]) > 0.5
    o_slc = attend(buf[slot, 0].astype(bf16), buf[slot, 1].astype(bf16), picked, snew_ref)

    wbuf = cw_ref.shape[-1]
    win_i = lax.broadcasted_iota(jnp.int32, (1, wbuf), 1)
    o_win = attend(cw_ref[0, 0].astype(bf16), cw_ref[0, 1].astype(bf16), win_i > tok + win_off, wnew_ref)

    lane = lax.broadcasted_iota(jnp.int32, (1, LANES), 1)
    for c in range(2):
        shifted = pltpu.roll(cw_ref[0, c], wbuf - n_tok, 1)
        tail = jnp.where(lane >= LANES - n_tok, wnewt_ref[0, c], shifted[:, wbuf - LANES:])
        wout_ref[0, c] = jnp.concatenate([shifted[:, :wbuf - LANES], tail], axis=-1)

    gate = gate_ref[0]
    o = ocmp_ref[0] + gate[:, 1:2] * o_slc + gate[:, 2:3] * o_win
    second_kvh = lax.broadcasted_iota(jnp.int32, (rows, 1), 0) >= G_NSA * n_tok
    o_ref[0] = jnp.where(second_kvh, o[:, DH:], o[:, :DH]) * zn_ref[0]


def _slc_win_sample(cache_t, page_table, q_blk, sel_rows, kvs_new, win_t, kvw_new, kvw_new_t, gate_rows,
                    o_cmp, zn_rows, n_tok):
    b, n_pages = page_table.shape
    rows = H_NSA * n_tok
    past_len = n_pages * PAGE_SIZE
    wbuf = win_t.shape[-1]
    e = _block_expand(sel_rows.shape[-1], past_len)
    full = lambda a: pl.BlockSpec(a.shape, lambda i, pt: (0,) * a.ndim)
    seq = lambda a: pl.BlockSpec((1,) + a.shape[1:], lambda i, pt: (i,) + (0,) * (a.ndim - 1))
    grid_spec = pltpu.PrefetchScalarGridSpec(
        num_scalar_prefetch=1,
        grid=(b,),
        in_specs=[pl.BlockSpec(memory_space=pl.ANY), seq(q_blk), seq(sel_rows), full(e), seq(kvs_new),
                  seq(win_t), seq(kvw_new), seq(kvw_new_t), seq(gate_rows), seq(o_cmp), seq(zn_rows)],
        out_specs=(pl.BlockSpec((1, rows, DH), lambda i, pt: (i, 0, 0)), seq(win_t)),
        scratch_shapes=[pltpu.VMEM((2, 2, KVH * DH, past_len), f32), pltpu.SemaphoreType.DMA((2,))],
    )
    return pl.pallas_call(
        functools.partial(_slc_win_sample_kernel, n_pages=n_pages, n_tok=n_tok, win_off=wbuf - WINDOW),
        grid_spec=grid_spec,
        out_shape=(jax.ShapeDtypeStruct((b, rows, DH), f32), jax.ShapeDtypeStruct(win_t.shape, f32)),
        compiler_params=_cparams(("arbitrary",)),
        name="slc_win_sample",
    )(page_table, cache_t, q_blk, sel_rows, e, kvs_new, win_t, kvw_new, kvw_new_t, gate_rows, o_cmp, zn_rows)


def _gla_sample(gq, gk, la, gv, zg, state, gain, b, n_tok):
    rows = 8
    pad = lambda a: jnp.pad(a.reshape(b, n_tok, -1), ((0, 0), (0, rows - n_tok), (0, 0))).reshape(b * rows, -1)
    y, s_new = _gla(pad(gq), pad(gk), pad(gv), pad(la), pad(zg), gain, state, b, tl=rows, chunk=rows)
    return y.reshape(b, rows, GLA_WIDTH)[:, :n_tok].reshape(b * n_tok, GLA_WIDTH), s_new


def _sample_path(x, cache_c, cache_s, cache_w, state, page_table, wts):
    (norm_g, w_pack, wa2_pad, ba, w1_pairs, cmp_bias, w2_heads, gla_gain, w_out, out_gain) = wts
    b, n_tok, _ = x.shape
    n = b * n_tok
    n_pages = page_table.shape[1]
    past_len = n_pages * PAGE_SIZE
    assert n_tok <= 8 and past_len % SLC_LEN == 0 and past_len // SLC_LEN <= LANES
    pos = past_len + jnp.arange(n) % n_tok
    (q_hm, kvc, kvs, kvw, gate, zn, gq, gk, gv, la, zg) = _inproj(
        x.reshape(1, n, D_MODEL), pos, norm_g, w_pack, wa2_pad, ba, tm=n)

    rows = H_NSA * n_tok
    q_rows = q_hm[0].reshape(H_NSA, b, n_tok, DH).transpose(1, 0, 2, 3)
    zero = jnp.zeros_like(q_rows[:, :G_NSA])
    q_blk = jnp.concatenate([jnp.concatenate([q_rows[:, :G_NSA], zero], axis=-1),
                             jnp.concatenate([zero, q_rows[:, G_NSA:]], axis=-1)], axis=1).reshape(b, rows, LANES)
    gate_rows = gate[:, :3 * H_NSA].reshape(b, n_tok, 3, H_NSA).transpose(0, 3, 1, 2).reshape(b, rows, 3)
    gate_rows = jnp.pad(gate_rows, ((0, 0), (0, 0), (0, LANES - 3)))
    pad_new = lambda a: jnp.pad(a.reshape(b, n_tok, KV_ROW), ((0, 0), (0, 8 - n_tok), (0, 0)))

    kvw_new_t = kvw.reshape(b, n_tok, 2, KVH * DH).transpose(0, 2, 3, 1)
    kvw_new_t = jnp.pad(kvw_new_t, ((0, 0), (0, 0), (0, 0), (LANES - n_tok, 0)))

    o_cmp, sel_rows = _cmp_attn_sample(_pages_native(cache_c), page_table, w1_pairs, cmp_bias, w2_heads,
                                       q_blk, gate_rows, n_tok)
    o_rows, win_t = _slc_win_sample(_pages_native(cache_s), page_table, q_blk, sel_rows, pad_new(kvs),
                                    _pages_native(cache_w), pad_new(kvw), kvw_new_t, gate_rows, o_cmp,
                                    zn.reshape(b, n_tok, H_NSA, DH).transpose(0, 2, 1, 3).reshape(b, rows, DH),
                                    n_tok)
    y_nsa = o_rows.reshape(b, H_NSA, n_tok, DH).transpose(0, 2, 1, 3).reshape(n, NSA_WIDTH)

    y_gla, s_new = _gla_sample(gq, gk, la, gv, zg, state, gla_gain, b, n_tok)
    y = _outproj(x.reshape(n, D_MODEL), y_nsa, y_gla, w_out, out_gain, tm=n)
    kv5 = lambda a: a.reshape(b, n_tok, 2, KVH, DH)
    win_new = win_t.reshape(b, 2, KVH, DH, -1).transpose(0, 4, 1, 2, 3)
    return (y.reshape(b, n_tok, D_MODEL), kv5(kvc), kv5(kvs), win_new, s_new)


def kernel(x_prompt, x_sample, cache_cmp_kv, cache_slc_kv, cache_win_kv, state_gla, page_table,
           norm_in_gain, w_in, cmp_pe, cmp_w1, cmp_b1, cmp_w2, gla_wa2, gla_ba, gla_norm_gain,
           w_out, norm_out_gain):
    assert w_in.shape[0] == 1, "single-layer step"
    wts = (norm_in_gain[0], _pack_w_in(w_in[0]), _pad_wa2(gla_wa2[0]), gla_ba[0],
           _cmp_w1_pairs(cmp_w1[0]), _cmp_bias(cmp_pe[0], cmp_w1[0], cmp_b1[0]), _cmp_w2_heads(cmp_w2[0]),
           gla_norm_gain[0], w_out[0].astype(bf16), norm_out_gain)
    yp, cmp_p, slc_p, win_p, gla_p = _prompt_path(x_prompt, wts, *_pack_w_in_prompt(w_in[0]))
    ys, cmp_s, slc_s, win_s, gla_s = _sample_path(
        x_sample, cache_cmp_kv[0], cache_slc_kv[0], cache_win_kv[0], state_gla[0], page_table, wts)
    return (yp, ys, cmp_p[None], cmp_s[None], slc_p[None], slc_s[None], win_p[None], win_s[None],
            gla_p[None], gla_s[None])
```
